```python
import math
import jax, jax.numpy as jnp
from jax import lax
import numpy as np

D_MODEL = 1024
BATCH = 8
SEQ = 16384
DEPTH = 1

CHUNK = 64
ATTN_HEAD_DIM = 64
ATTN_HEADS = (D_MODEL // 2) // ATTN_HEAD_DIM
ATTN_WIDTH = ATTN_HEADS * ATTN_HEAD_DIM
LEFT_CHUNKS = 8
BAND_CHUNKS = LEFT_CHUNKS + 1
REL_CLIP = 128
DN_HEAD_DIM = 128
DN_HEADS = (D_MODEL - ATTN_WIDTH) // DN_HEAD_DIM
DN_WIDTH = DN_HEADS * DN_HEAD_DIM
CONV_K = 4
FFN_HIDDEN = -(-8 * D_MODEL // (3 * 256)) * 256
IN_COLS = 3 * ATTN_WIDTH + 4 * DN_WIDTH + 2 * DN_HEADS
EPS = 1e-6

kernel_name = 'hymba_chunked_attn_gated_deltanet_block'


def rms_norm(x, g):
    xf = x.astype(jnp.float32)
    y = xf * lax.rsqrt(jnp.mean(xf * xf, axis=-1, keepdims=True) + EPS)
    return (y * g.astype(jnp.float32)).astype(x.dtype)


def l2norm(t):
    return t * lax.rsqrt(jnp.sum(t * t, axis=-1, keepdims=True) + EPS)


def chunked_band_attention(q, k, v, q_g, k_g, rel_bias):
    B, T, H, Dh = q.shape
    nc = T // CHUNK
    f32 = jnp.float32
    q = rms_norm(q, q_g).astype(f32)
    k = rms_norm(k, k_g).astype(f32)
    v = v.astype(f32)
    qc = q.reshape(B, nc, CHUNK, H, Dh)

    def band(t):
        tc = t.reshape(B, nc, CHUNK, H, Dh)
        tp = jnp.pad(tc, ((0, 0), (LEFT_CHUNKS, 0), (0, 0), (0, 0), (0, 0)))
        return jnp.concatenate([tp[:, i:i + nc] for i in range(BAND_CHUNKS)], axis=2)

    kb, vb = band(k), band(v)
    scores = jnp.einsum('bcqhd,bckhd->bchqk', qc, kb) * (Dh ** -0.5)
    q_off = jnp.arange(CHUNK)
    k_off = jnp.arange(BAND_CHUNKS * CHUNK) - LEFT_CHUNKS * CHUNK
    rel = q_off[:, None] - k_off[None, :]
    idx = jnp.clip(rel, -REL_CLIP, REL_CLIP) + REL_CLIP
    bias = jnp.transpose(rel_bias.astype(f32)[idx], (2, 0, 1))
    key_chunk = jnp.arange(nc)[:, None] - LEFT_CHUNKS + jnp.arange(BAND_CHUNKS)[None, :]
    valid = jnp.repeat(key_chunk >= 0, CHUNK, axis=1)
    scores = jnp.where(valid[None, :, None, None, :], scores + bias[None, None], -1e30)
    p = jax.nn.softmax(scores, axis=-1)
    out = jnp.einsum('bchqk,bckhd->bcqhd', p, vb)
    return out.reshape(B, T, H * Dh)


def causal_depthwise_conv(x, w):
    C = x.shape[-1]
    return lax.conv_general_dilated(
        x.astype(jnp.float32), w.astype(jnp.float32)[:, None, :],
        window_strides=(1,), padding=[(CONV_K - 1, 0)],
        dimension_numbers=('NWC', 'WIO', 'NWC'), feature_group_count=C)


def gated_delta_rule(q, k, v, beta_logit, alpha_logit, a_log, dt_bias):
    B, T, H, Dk = q.shape
    Dv = v.shape[-1]
    nc = T // CHUNK
    f32 = jnp.float32
    q = l2norm(q.astype(f32)) * (Dk ** -0.5)
    k = l2norm(k.astype(f32))
    v = v.astype(f32)
    beta = jax.nn.sigmoid(beta_logit.astype(f32))
    g = -jnp.exp(a_log.astype(f32)) * jax.nn.softplus(alpha_logit.astype(f32) + dt_bias.astype(f32))

    def to_chunks(t):
        return t.reshape(B, nc, CHUNK, H, -1).transpose(0, 3, 1, 2, 4)

    qc, kc, vc = to_chunks(q), to_chunks(k), to_chunks(v)
    bc = beta.reshape(B, nc, CHUNK, H).transpose(0, 3, 1, 2)
    gc = jnp.cumsum(g.reshape(B, nc, CHUNK, H).transpose(0, 3, 1, 2), axis=-1)
    tri_incl = jnp.tril(jnp.ones((CHUNK, CHUNK), bool))
    tri_strict = jnp.tril(jnp.ones((CHUNK, CHUNK), bool), -1)
    gdiff = gc[..., :, None] - gc[..., None, :]
    decay = jnp.where(tri_incl, jnp.exp(jnp.where(tri_incl, gdiff, 0.0)), 0.0)
    kb = kc * bc[..., None]
    vb = vc * bc[..., None]
    L = jnp.where(tri_strict, jnp.einsum('bhnid,bhnjd->bhnij', kb, kc) * decay, 0.0)
    eye = jnp.eye(CHUNK, dtype=f32)
    rhs = jnp.concatenate([vb, kb * jnp.exp(gc)[..., None]], axis=-1)
    sol = lax.linalg.triangular_solve(eye + L, rhs, left_side=True, lower=True, unit_diagonal=True)
    u, w = sol[..., :Dv], sol[..., Dv:]
    intra = jnp.einsum('bhnid,bhnjd->bhnij', qc, kc) * decay
    q_dec = qc * jnp.exp(gc)[..., None]
    g_last = gc[..., -1]
    k_dec = kc * jnp.exp(g_last[..., None] - gc)[..., None]

    def step(S, inp):
        u_n, w_n, intra_n, qd_n, kd_n, gl_n = inp
        v_new = u_n - jnp.einsum('bhcd,bhde->bhce', w_n, S)
        o = jnp.einsum('bhcd,bhde->bhce', qd_n, S) + jnp.einsum('bhij,bhje->bhie', intra_n, v_new)
        S = S * jnp.exp(gl_n)[..., None, None] + jnp.einsum('bhcd,bhce->bhde', kd_n, v_new)
        return S, o

    xs = (jnp.moveaxis(u, 2, 0), jnp.moveaxis(w, 2, 0), jnp.moveaxis(intra, 2, 0),
          jnp.moveaxis(q_dec, 2, 0), jnp.moveaxis(k_dec, 2, 0), jnp.moveaxis(g_last, 2, 0))
    S0 = jnp.zeros((B, H, Dk, Dv), f32)
    _, o = lax.scan(step, S0, xs)
    return jnp.transpose(o, (1, 0, 3, 2, 4)).reshape(B, T, H, Dv)


def _fwd_setup_inputs(seed: int = 0) -> dict:
    key = jax.random.key(seed)
    ks = jax.random.split(key, 17)
    f32 = jnp.float32

    def nrm(k, shape, scale):
        return jax.random.normal(k, shape, f32) * scale

    def gain(k, n):
        return 1.0 + 0.01 * jax.random.normal(k, (DEPTH, n), f32)

    x = jax.random.normal(ks[0], (BATCH, SEQ, D_MODEL), f32)
    dt = jnp.exp(jax.random.uniform(ks[9], (DEPTH, DN_HEADS), f32, math.log(1e-3), math.log(1e-1)))
    return {
        'x': x,
        'norm_mix_g': gain(ks[1], D_MODEL),
        'w_in': nrm(ks[2], (DEPTH, D_MODEL, IN_COLS), D_MODEL ** -0.5),
        'attn_q_norm_g': gain(ks[3], ATTN_HEAD_DIM),
        'attn_k_norm_g': gain(ks[4], ATTN_HEAD_DIM),
        'rel_bias': nrm(ks[5], (DEPTH, 2 * REL_CLIP + 1, ATTN_HEADS), 0.1),
        'attn_out_norm_g': gain(ks[6], ATTN_WIDTH),
        'conv_w': nrm(ks[7], (DEPTH, CONV_K, 3 * DN_WIDTH), CONV_K ** -0.5),
        'a_log': jnp.log(jax.random.uniform(ks[8], (DEPTH, DN_HEADS), f32, 1.0, 16.0)),
        'dt_bias': dt + jnp.log(-jnp.expm1(-dt)),
        'dn_out_norm_g': gain(ks[10], DN_HEAD_DIM),
        'w_out': nrm(ks[11], (DEPTH, D_MODEL, D_MODEL), D_MODEL ** -0.5),
        'norm_ffn_g': gain(ks[12], D_MODEL),
        'w_gate': nrm(ks[13], (DEPTH, D_MODEL, FFN_HIDDEN), D_MODEL ** -0.5),
        'w_up': nrm(ks[14], (DEPTH, D_MODEL, FFN_HIDDEN), D_MODEL ** -0.5),
        'w_down': nrm(ks[15], (DEPTH, FFN_HIDDEN, D_MODEL), FFN_HIDDEN ** -0.5),
    }


def _fwd_reference(x, norm_mix_g, w_in, attn_q_norm_g, attn_k_norm_g, rel_bias, attn_out_norm_g,
              conv_w, a_log, dt_bias, dn_out_norm_g, w_out, norm_ffn_g, w_gate, w_up, w_down):
    B, T, _ = x.shape
    splits = [ATTN_WIDTH, 2 * ATTN_WIDTH, 3 * ATTN_WIDTH,
              3 * ATTN_WIDTH + 3 * DN_WIDTH, 3 * ATTN_WIDTH + 4 * DN_WIDTH,
              3 * ATTN_WIDTH + 4 * DN_WIDTH + DN_HEADS]
    for l in range(DEPTH):
        h = rms_norm(x, norm_mix_g[l])
        proj = jnp.einsum('bsd,de->bse', h, w_in[l])
        a_q, a_k, a_v, d_qkv, d_z, d_beta, d_alpha = jnp.split(proj, splits, axis=-1)
        a_out = chunked_band_attention(
            a_q.reshape(B, T, ATTN_HEADS, ATTN_HEAD_DIM),
            a_k.reshape(B, T, ATTN_HEADS, ATTN_HEAD_DIM),
            a_v.reshape(B, T, ATTN_HEADS, ATTN_HEAD_DIM),
            attn_q_norm_g[l], attn_k_norm_g[l], rel_bias[l])
        a_out = rms_norm(a_out, attn_out_norm_g[l]).astype(x.dtype)
        d_qkv = jax.nn.silu(causal_depthwise_conv(d_qkv, conv_w[l]))
        d_q, d_k, d_v = jnp.split(d_qkv, 3, axis=-1)
        o = gated_delta_rule(
            d_q.reshape(B, T, DN_HEADS, DN_HEAD_DIM),
            d_k.reshape(B, T, DN_HEADS, DN_HEAD_DIM),
            d_v.reshape(B, T, DN_HEADS, DN_HEAD_DIM),
            d_beta, d_alpha, a_log[l], dt_bias[l])
        z = d_z.reshape(B, T, DN_HEADS, DN_HEAD_DIM).astype(jnp.float32)
        o = rms_norm(o, dn_out_norm_g[l]) * jax.nn.silu(z)
        d_out = o.reshape(B, T, DN_WIDTH).astype(x.dtype)
        mix = jnp.concatenate([a_out, d_out], axis=-1)
        x = x + jnp.einsum('bse,ed->bsd', mix, w_out[l])
        h = rms_norm(x, norm_ffn_g[l])
        gate = jnp.einsum('bsd,df->bsf', h, w_gate[l])
        up = jnp.einsum('bsd,df->bsf', h, w_up[l])
        x = x + jnp.einsum('bsf,fd->bsd', jax.nn.silu(gate) * up, w_down[l])
    return x


import jax as _jax
import jax.numpy as _jnp

TWIN_FORMAT = 'train_step'
FWD_PARAMS = ['x', 'norm_mix_g', 'w_in', 'attn_q_norm_g', 'attn_k_norm_g', 'rel_bias', 'attn_out_norm_g', 'conv_w', 'a_log', 'dt_bias', 'dn_out_norm_g', 'w_out', 'norm_ffn_g', 'w_gate', 'w_up', 'w_down']
TWIN_WEIGHTS = ['norm_mix_g', 'w_in', 'attn_q_norm_g', 'attn_k_norm_g', 'rel_bias', 'attn_out_norm_g', 'conv_w', 'a_log', 'dt_bias', 'dn_out_norm_g', 'w_out', 'norm_ffn_g', 'w_gate', 'w_up', 'w_down']
TWIN_DIFF_INPUT = 'x'
TWIN_INPUTS = ['x', 'norm_mix_g', 'w_in', 'attn_q_norm_g', 'attn_k_norm_g', 'rel_bias', 'attn_out_norm_g', 'conv_w', 'a_log', 'dt_bias', 'dn_out_norm_g', 'w_out', 'norm_ffn_g', 'w_gate', 'w_up', 'w_down', 'loss_target', 'm_norm_mix_g', 'm_w_in', 'm_attn_q_norm_g', 'm_attn_k_norm_g', 'm_rel_bias', 'm_attn_out_norm_g', 'm_conv_w', 'm_a_log', 'm_dt_bias', 'm_dn_out_norm_g', 'm_w_out', 'm_norm_ffn_g', 'm_w_gate', 'm_w_up', 'm_w_down', 'v_norm_mix_g', 'v_w_in', 'v_attn_q_norm_g', 'v_attn_k_norm_g', 'v_rel_bias', 'v_attn_out_norm_g', 'v_conv_w', 'v_a_log', 'v_dt_bias', 'v_dn_out_norm_g', 'v_w_out', 'v_norm_ffn_g', 'v_w_gate', 'v_w_up', 'v_w_down']
TWIN_OUTPUTS = ['loss', 'grad_x', 'grad_norm_mix_g', 'grad_w_in', 'grad_attn_q_norm_g', 'grad_attn_k_norm_g', 'grad_rel_bias', 'grad_attn_out_norm_g', 'grad_conv_w', 'grad_a_log', 'grad_dt_bias', 'grad_dn_out_norm_g', 'grad_w_out', 'grad_norm_ffn_g', 'grad_w_gate', 'grad_w_up', 'grad_w_down', 'delta_norm_mix_g', 'delta_w_in', 'delta_attn_q_norm_g', 'delta_attn_k_norm_g', 'delta_rel_bias', 'delta_attn_out_norm_g', 'delta_conv_w', 'delta_a_log', 'delta_dt_bias', 'delta_dn_out_norm_g', 'delta_w_out', 'delta_norm_ffn_g', 'delta_w_gate', 'delta_w_up', 'delta_w_down', 'new_m_norm_mix_g', 'new_m_w_in', 'new_m_attn_q_norm_g', 'new_m_attn_k_norm_g', 'new_m_rel_bias', 'new_m_attn_out_norm_g', 'new_m_conv_w', 'new_m_a_log', 'new_m_dt_bias', 'new_m_dn_out_norm_g', 'new_m_w_out', 'new_m_norm_ffn_g', 'new_m_w_gate', 'new_m_w_up', 'new_m_w_down', 'new_v_norm_mix_g', 'new_v_w_in', 'new_v_attn_q_norm_g', 'new_v_attn_k_norm_g', 'new_v_rel_bias', 'new_v_attn_out_norm_g', 'new_v_conv_w', 'new_v_a_log', 'new_v_dt_bias', 'new_v_dn_out_norm_g', 'new_v_w_out', 'new_v_norm_ffn_g', 'new_v_w_gate', 'new_v_w_up', 'new_v_w_down']
TWIN_LEAF_KINDS = {'loss': 'loss', 'grad_x': 'grad_x', 'grad_norm_mix_g': 'grad_w', 'grad_w_in': 'grad_w', 'grad_attn_q_norm_g': 'grad_w', 'grad_attn_k_norm_g': 'grad_w', 'grad_rel_bias': 'grad_w', 'grad_attn_out_norm_g': 'grad_w', 'grad_conv_w': 'grad_w', 'grad_a_log': 'grad_w', 'grad_dt_bias': 'grad_w', 'grad_dn_out_norm_g': 'grad_w', 'grad_w_out': 'grad_w', 'grad_norm_ffn_g': 'grad_w', 'grad_w_gate': 'grad_w', 'grad_w_up': 'grad_w', 'grad_w_down': 'grad_w', 'delta_norm_mix_g': 'delta_w', 'delta_w_in': 'delta_w', 'delta_attn_q_norm_g': 'delta_w', 'delta_attn_k_norm_g': 'delta_w', 'delta_rel_bias': 'delta_w', 'delta_attn_out_norm_g': 'delta_w', 'delta_conv_w': 'delta_w', 'delta_a_log': 'delta_w', 'delta_dt_bias': 'delta_w', 'delta_dn_out_norm_g': 'delta_w', 'delta_w_out': 'delta_w', 'delta_norm_ffn_g': 'delta_w', 'delta_w_gate': 'delta_w', 'delta_w_up': 'delta_w', 'delta_w_down': 'delta_w', 'new_m_norm_mix_g': 'new_m', 'new_m_w_in': 'new_m', 'new_m_attn_q_norm_g': 'new_m', 'new_m_attn_k_norm_g': 'new_m', 'new_m_rel_bias': 'new_m', 'new_m_attn_out_norm_g': 'new_m', 'new_m_conv_w': 'new_m', 'new_m_a_log': 'new_m', 'new_m_dt_bias': 'new_m', 'new_m_dn_out_norm_g': 'new_m', 'new_m_w_out': 'new_m', 'new_m_norm_ffn_g': 'new_m', 'new_m_w_gate': 'new_m', 'new_m_w_up': 'new_m', 'new_m_w_down': 'new_m', 'new_v_norm_mix_g': 'new_v', 'new_v_w_in': 'new_v', 'new_v_attn_q_norm_g': 'new_v', 'new_v_attn_k_norm_g': 'new_v', 'new_v_rel_bias': 'new_v', 'new_v_attn_out_norm_g': 'new_v', 'new_v_conv_w': 'new_v', 'new_v_a_log': 'new_v', 'new_v_dt_bias': 'new_v', 'new_v_dn_out_norm_g': 'new_v', 'new_v_w_out': 'new_v', 'new_v_norm_ffn_g': 'new_v', 'new_v_w_gate': 'new_v', 'new_v_w_up': 'new_v', 'new_v_w_down': 'new_v'}


def _forward(args):
    return _fwd_reference(*[args[k] for k in FWD_PARAMS])


def _output_shape():
    def fwd():
        inp = _fwd_setup_inputs(0)
        return _fwd_reference(*[inp[k] for k in FWD_PARAMS])
    out = _jax.eval_shape(fwd)
    return out.shape, out.dtype

N_MICROBATCH = 1
ADAM_LR = 0.001
ADAM_B1 = 0.9
ADAM_B2 = 0.999
ADAM_EPS = 1e-08
ADAM_WD = 0.01
ADAM_STEP = 10
PER_EXAMPLE_BATCH_AXIS = {'x': 0, 'loss_target': 0}
SHARED_INPUTS = []
_WEIGHT_DTYPES = {'norm_mix_g': _jnp.float32, 'w_in': _jnp.float32, 'attn_q_norm_g': _jnp.float32, 'attn_k_norm_g': _jnp.float32, 'rel_bias': _jnp.float32, 'attn_out_norm_g': _jnp.float32, 'conv_w': _jnp.float32, 'a_log': _jnp.float32, 'dt_bias': _jnp.float32, 'dn_out_norm_g': _jnp.float32, 'w_out': _jnp.float32, 'norm_ffn_g': _jnp.float32, 'w_gate': _jnp.float32, 'w_up': _jnp.float32, 'w_down': _jnp.float32}
MOMENT_SCALE = {'norm_mix_g': 2.471340e+01, 'w_in': 1.622552e+00, 'attn_q_norm_g': 2.896749e+00, 'attn_k_norm_g': 2.941409e+00, 'rel_bias': 3.759242e-01, 'attn_out_norm_g': 1.285536e+02, 'conv_w': 1.116146e+00, 'a_log': 1.161712e+01, 'dt_bias': 1.133489e+01, 'dn_out_norm_g': 1.942780e+02, 'w_out': 3.812706e+00, 'norm_ffn_g': 9.954338e+01, 'w_gate': 9.531999e-01, 'w_up': 6.803737e-01, 'w_down': 1.029254e+00}


def _to_microbatches(a, axis):
    t = _jnp.moveaxis(a, axis, 0)
    t = t.reshape((N_MICROBATCH, t.shape[0] // N_MICROBATCH) + t.shape[1:])
    return _jnp.moveaxis(t, 1, axis + 1)


def setup_inputs(seed: int = 0) -> dict:
    inp = _fwd_setup_inputs(seed)
    key = _jax.random.fold_in(_jax.random.key(seed), 7919)
    shape, _ = _output_shape()
    out = dict(inp)
    out["loss_target"] = _jax.random.normal(_jax.random.fold_in(key, 0), shape, _jnp.float32)
    for i, name in enumerate(TWIN_WEIGHTS):
        w = inp[name].astype(_jnp.float32)
        if MOMENT_SCALE is None:
            s = _jnp.sqrt(_jnp.mean(_jnp.square(w)) + 1e-30)
        else:
            s = MOMENT_SCALE[name]
        km, kv = _jax.random.split(_jax.random.fold_in(key, i + 1))
        out[name] = w
        out["m_" + name] = s * _jax.random.normal(km, w.shape, _jnp.float32)
        out["v_" + name] = (s * s) * _jax.random.uniform(kv, w.shape, _jnp.float32, 0.5, 1.5)
    if N_MICROBATCH > 1:
        for name, axis in PER_EXAMPLE_BATCH_AXIS.items():
            out[name] = _to_microbatches(out[name], axis)
    return {'x': out['x'], 'norm_mix_g': out['norm_mix_g'], 'w_in': out['w_in'], 'attn_q_norm_g': out['attn_q_norm_g'], 'attn_k_norm_g': out['attn_k_norm_g'], 'rel_bias': out['rel_bias'], 'attn_out_norm_g': out['attn_out_norm_g'], 'conv_w': out['conv_w'], 'a_log': out['a_log'], 'dt_bias': out['dt_bias'], 'dn_out_norm_g': out['dn_out_norm_g'], 'w_out': out['w_out'], 'norm_ffn_g': out['norm_ffn_g'], 'w_gate': out['w_gate'], 'w_up': out['w_up'], 'w_down': out['w_down'], 'loss_target': out['loss_target'], 'm_norm_mix_g': out['m_norm_mix_g'], 'm_w_in': out['m_w_in'], 'm_attn_q_norm_g': out['m_attn_q_norm_g'], 'm_attn_k_norm_g': out['m_attn_k_norm_g'], 'm_rel_bias': out['m_rel_bias'], 'm_attn_out_norm_g': out['m_attn_out_norm_g'], 'm_conv_w': out['m_conv_w'], 'm_a_log': out['m_a_log'], 'm_dt_bias': out['m_dt_bias'], 'm_dn_out_norm_g': out['m_dn_out_norm_g'], 'm_w_out': out['m_w_out'], 'm_norm_ffn_g': out['m_norm_ffn_g'], 'm_w_gate': out['m_w_gate'], 'm_w_up': out['m_w_up'], 'm_w_down': out['m_w_down'], 'v_norm_mix_g': out['v_norm_mix_g'], 'v_w_in': out['v_w_in'], 'v_attn_q_norm_g': out['v_attn_q_norm_g'], 'v_attn_k_norm_g': out['v_attn_k_norm_g'], 'v_rel_bias': out['v_rel_bias'], 'v_attn_out_norm_g': out['v_attn_out_norm_g'], 'v_conv_w': out['v_conv_w'], 'v_a_log': out['v_a_log'], 'v_dt_bias': out['v_dt_bias'], 'v_dn_out_norm_g': out['v_dn_out_norm_g'], 'v_w_out': out['v_w_out'], 'v_norm_ffn_g': out['v_norm_ffn_g'], 'v_w_gate': out['v_w_gate'], 'v_w_up': out['v_w_up'], 'v_w_down': out['v_w_down']}


def _loss(weights, diff, rest, loss_target):
    with _jax.named_scope("forward"):
        args = {**rest, TWIN_DIFF_INPUT: diff, **{k: w.astype(_WEIGHT_DTYPES[k]) for k, w in weights.items()}}
        y = _forward(args)
    with _jax.named_scope("loss_head"):
        err = _jnp.square(y.astype(_jnp.float32) - loss_target)
        return 0.5 * _jnp.sum(_jnp.mean(err, axis=-1)) if err.ndim else 0.5 * err


def _adamw(w, g, m, v):
    m = ADAM_B1 * m + (1.0 - ADAM_B1) * g
    v = ADAM_B2 * v + (1.0 - ADAM_B2) * _jnp.square(g)
    m_hat = m / (1.0 - ADAM_B1 ** ADAM_STEP)
    v_hat = v / (1.0 - ADAM_B2 ** ADAM_STEP)
    delta = -ADAM_LR * (m_hat / (_jnp.sqrt(v_hat) + ADAM_EPS) + ADAM_WD * w)
    return delta, m, v


def reference(x, norm_mix_g, w_in, attn_q_norm_g, attn_k_norm_g, rel_bias, attn_out_norm_g, conv_w, a_log, dt_bias, dn_out_norm_g, w_out, norm_ffn_g, w_gate, w_up, w_down, loss_target, m_norm_mix_g, m_w_in, m_attn_q_norm_g, m_attn_k_norm_g, m_rel_bias, m_attn_out_norm_g, m_conv_w, m_a_log, m_dt_bias, m_dn_out_norm_g, m_w_out, m_norm_ffn_g, m_w_gate, m_w_up, m_w_down, v_norm_mix_g, v_w_in, v_attn_q_norm_g, v_attn_k_norm_g, v_rel_bias, v_attn_out_norm_g, v_conv_w, v_a_log, v_dt_bias, v_dn_out_norm_g, v_w_out, v_norm_ffn_g, v_w_gate, v_w_up, v_w_down):
    given = dict(x=x, norm_mix_g=norm_mix_g, w_in=w_in, attn_q_norm_g=attn_q_norm_g, attn_k_norm_g=attn_k_norm_g, rel_bias=rel_bias, attn_out_norm_g=attn_out_norm_g, conv_w=conv_w, a_log=a_log, dt_bias=dt_bias, dn_out_norm_g=dn_out_norm_g, w_out=w_out, norm_ffn_g=norm_ffn_g, w_gate=w_gate, w_up=w_up, w_down=w_down, loss_target=loss_target, m_norm_mix_g=m_norm_mix_g, m_w_in=m_w_in, m_attn_q_norm_g=m_attn_q_norm_g, m_attn_k_norm_g=m_attn_k_norm_g, m_rel_bias=m_rel_bias, m_attn_out_norm_g=m_attn_out_norm_g, m_conv_w=m_conv_w, m_a_log=m_a_log, m_dt_bias=m_dt_bias, m_dn_out_norm_g=m_dn_out_norm_g, m_w_out=m_w_out, m_norm_ffn_g=m_norm_ffn_g, m_w_gate=m_w_gate, m_w_up=m_w_up, m_w_down=m_w_down, v_norm_mix_g=v_norm_mix_g, v_w_in=v_w_in, v_attn_q_norm_g=v_attn_q_norm_g, v_attn_k_norm_g=v_attn_k_norm_g, v_rel_bias=v_rel_bias, v_attn_out_norm_g=v_attn_out_norm_g, v_conv_w=v_conv_w, v_a_log=v_a_log, v_dt_bias=v_dt_bias, v_dn_out_norm_g=v_dn_out_norm_g, v_w_out=v_w_out, v_norm_ffn_g=v_norm_ffn_g, v_w_gate=v_w_gate, v_w_up=v_w_up, v_w_down=v_w_down)
    weights = {n: given[n] for n in TWIN_WEIGHTS}
    shared = {n: given[n] for n in SHARED_INPUTS}
    per_example = {n: given[n] for n in ['x']}
    grad_fn = _jax.value_and_grad(_loss, argnums=(0, 1))

    def one_microbatch(ex, loss_target):
        ex = dict(ex)
        diff = ex.pop(TWIN_DIFF_INPUT)
        return grad_fn(weights, diff, {**shared, **ex}, loss_target)

    if N_MICROBATCH == 1:
        loss, (grad_w, grad_x) = one_microbatch(per_example, given["loss_target"])
    else:
        def body(carry, xs):
            loss_sum, grad_sum = carry
            l_k, (gw_k, gx_k) = one_microbatch(xs[0], xs[1])
            with _jax.named_scope("update"):
                return (loss_sum + l_k, _jax.tree.map(_jnp.add, grad_sum, gw_k)), gx_k

        init = (_jnp.zeros((), _jnp.float32), _jax.tree.map(_jnp.zeros_like, weights))
        (loss, grad_w), grad_x = _jax.lax.scan(body, init, (per_example, given["loss_target"]))
    with _jax.named_scope("update"):
        delta_w, new_m, new_v = {}, {}, {}
        for n in TWIN_WEIGHTS:
            delta_w[n], new_m[n], new_v[n] = _adamw(weights[n], grad_w[n], given["m_" + n], given["v_" + n])
    return (loss, grad_x, *[grad_w[n] for n in TWIN_WEIGHTS], *[delta_w[n] for n in TWIN_WEIGHTS],
            *[new_m[n] for n in TWIN_WEIGHTS], *[new_v[n] for n in TWIN_WEIGHTS])
```

```python
import functools
import math

import jax
import jax.numpy as jnp
from jax import lax
from jax.experimental import pallas as pl
from jax.experimental.pallas import tpu as pltpu

F32 = jnp.float32
BF16 = jnp.bfloat16
_MXU = jnp.bfloat16

D = 1024
AW = 512
NHA = 8
DHA = 64
CH = 64
BAND = 9
NHD = 4
DHD = 128
DW = 512
FF = 2816
EPS = 1e-6
NEG = -1e30
N_DEV = 8
LANES = 128
VMEM_LIMIT = 56 * 1024 * 1024

ADAM_LR = 0.001
ADAM_B1 = 0.9
ADAM_B2 = 0.999
ADAM_EPS = 1e-08
ADAM_WD = 0.01
ADAM_STEP = 10

MESH_T = pl.DeviceIdType.MESH


def _cp(sem=None, vmem=VMEM_LIMIT):
    kw = dict(vmem_limit_bytes=vmem)
    if sem is not None:
        kw["dimension_semantics"] = sem
    return pltpu.CompilerParams(**kw)


def _dot(a, b):
    return jnp.dot(a.astype(_MXU), b.astype(_MXU), preferred_element_type=F32)


def _dot_nt(a, b):
    return lax.dot_general(a.astype(_MXU), b.astype(_MXU), (((1,), (1,)), ((), ())), preferred_element_type=F32)


def _dot_tn(a, b):
    return lax.dot_general(a.astype(_MXU), b.astype(_MXU), (((0,), (0,)), ((), ())), preferred_element_type=F32)


def _split2(x):
    hi = x.astype(BF16)
    lo = (x - hi.astype(F32)).astype(BF16)
    return hi, lo


def _dot_x2(x, ones_b):
    hi, lo = _split2(x)
    return jnp.dot(hi, ones_b, preferred_element_type=F32) + jnp.dot(lo, ones_b, preferred_element_type=F32)


def _dot_x2_nt(x, ones_b):
    hi, lo = _split2(x)
    dn = (((1,), (1,)), ((), ()))
    return lax.dot_general(hi, ones_b, dn, preferred_element_type=F32) + lax.dot_general(
        lo, ones_b, dn, preferred_element_type=F32)


def _iota(shape, dim):
    return lax.broadcasted_iota(jnp.int32, shape, dim)


def _block_ones(n, blk, dtype=BF16):
    r, c = _iota((n, n), 0), _iota((n, n), 1)
    return jnp.where((r // blk) == (c // blk), 1.0, 0.0).astype(dtype)


def _sigmoid(x):
    return 1.0 / (1.0 + jnp.exp(-x))


def _softplus(x):
    return jnp.maximum(x, 0.0) + jnp.log(1.0 + jnp.exp(-jnp.abs(x)))


def _col(x, k):
    lane = _iota(x.shape, 1)
    return jnp.sum(jnp.where(lane == k, x, 0.0), axis=1, keepdims=True)


def _row(x, k):
    sub = _iota(x.shape, 0)
    return jnp.sum(jnp.where(sub == k, x, 0.0), axis=0, keepdims=True)


def _my_pos():
    return lax.axis_index("x"), lax.axis_index("y"), lax.axis_index("c")


def _all_gather(x2d, name):
    R, W = x2d.shape

    def body(x_ref, out_ref, send_sems, recv_sems, local_sem):
        x, y, c = _my_pos()
        me, sibling = (x, y, c), (x, y, 1 - c)
        chips = [(1 - x, y), (x, 1 - y), (1 - x, 1 - y)]

        def slot(px, py, pc):
            return out_ref.at[4 * px + 2 * py + pc]

        def copy(k, block, to, src=None):
            return pltpu.make_async_remote_copy(
                src_ref=slot(*block) if src is None else src, dst_ref=slot(*block),
                send_sem=send_sems.at[k], recv_sem=recv_sems.at[k], device_id=to, device_id_type=MESH_T)

        mine = pltpu.make_async_copy(x_ref, slot(*me), local_sem)
        mine.start()
        first = [copy(0, me, sibling, src=x_ref)]
        first += [copy(1 + j, me, (*chip, c), src=x_ref) for j, chip in enumerate(chips)]
        for cp in first:
            cp.start()
        passed = [copy(4 + j, (*chip, c), sibling) for j, chip in enumerate(chips)]
        for j, chip in enumerate(chips):
            copy(1 + j, (*chip, c), me).wait_recv()
            passed[j].start()
        copy(0, sibling, me).wait_recv()
        for j, chip in enumerate(chips):
            copy(4 + j, (*chip, 1 - c), me).wait_recv()
        for cp in first + passed:
            cp.wait_send()
        mine.wait()

    return pl.pallas_call(
        body, name=name,
        out_shape=jax.ShapeDtypeStruct((N_DEV, R, W), x2d.dtype),
        in_specs=[pl.BlockSpec(memory_space=pl.ANY)],
        out_specs=pl.BlockSpec(memory_space=pl.ANY),
        scratch_shapes=[pltpu.SemaphoreType.DMA((7,)), pltpu.SemaphoreType.DMA((7,)), pltpu.SemaphoreType.DMA],
    )(x2d)


def _all_to_all(send, name):
    _, R, W = send.shape

    def body(s_ref, r_ref, send_sems, recv_sems, local_sem):
        x, y, c = _my_pos()
        my_idx = 4 * x + 2 * y + c
        mine = pltpu.make_async_copy(s_ref.at[my_idx], r_ref.at[0], local_sem)
        mine.start()
        copies = []
        for m in range(1, N_DEV):
            px = x ^ ((m >> 2) & 1)
            py = y ^ ((m >> 1) & 1)
            pc = c ^ (m & 1)
            cp = pltpu.make_async_remote_copy(
                src_ref=s_ref.at[4 * px + 2 * py + pc], dst_ref=r_ref.at[m],
                send_sem=send_sems.at[m - 1], recv_sem=recv_sems.at[m - 1],
                device_id=(px, py, pc), device_id_type=MESH_T)
            cp.start()
            copies.append(cp)
        for cp in copies:
            cp.wait_recv()
        for cp in copies:
            cp.wait_send()
        mine.wait()

    return pl.pallas_call(
        body, name=name,
        out_shape=jax.ShapeDtypeStruct(send.shape, send.dtype),
        in_specs=[pl.BlockSpec(memory_space=pl.ANY)],
        out_specs=pl.BlockSpec(memory_space=pl.ANY),
        scratch_shapes=[pltpu.SemaphoreType.DMA((7,)), pltpu.SemaphoreType.DMA((7,)), pltpu.SemaphoreType.DMA],
    )(send)


TM = 256
TG = 512


def _full(shape):
    nd = len(shape)
    return pl.BlockSpec(shape, lambda i: (0,) * nd)


def _rows(tm, w):
    return pl.BlockSpec((tm, w), lambda i: (i, 0))


def _head_rms(x, bd, width):
    ss = _dot_x2(x * x, bd)
    return lax.rsqrt(ss * (1.0 / width) + EPS)


def _inproj(x, g_mix, w_in, qg_t, kg_t):
    T = x.shape[0]

    def body(x_ref, g_ref, w_ref, qg_ref, kg_ref, araw_ref, an_ref, draw_ref, z_ref, ba_ref, h_ref):
        xv = x_ref[...]
        r = lax.rsqrt(jnp.mean(xv * xv, axis=1, keepdims=True) + EPS)
        h = (xv * r * g_ref[...]).astype(_MXU)
        h_ref[...] = h
        a = jnp.dot(h, w_ref[:, 0:1536], preferred_element_type=F32)
        araw_ref[...] = a
        bd = _block_ones(AW, DHA)
        q = a[:, 0:AW]
        k = a[:, AW:2 * AW]
        qn = q * _head_rms(q, bd, DHA) * (qg_ref[...] * (DHA ** -0.5))
        kn = k * _head_rms(k, bd, DHA) * kg_ref[...]
        an_ref[:, 0:AW] = qn.astype(_MXU)
        an_ref[:, AW:2 * AW] = kn.astype(_MXU)
        an_ref[:, 2 * AW:3 * AW] = a[:, 2 * AW:3 * AW].astype(_MXU)
        draw_ref[...] = jnp.dot(h, w_ref[:, 1536:3072], preferred_element_type=F32)
        z_ref[...] = jnp.dot(h, w_ref[:, 3072:3584], preferred_element_type=F32)
        ba_ref[...] = jnp.dot(h, w_ref[:, 3584:3592], preferred_element_type=F32)

    return pl.pallas_call(
        body, name="inproj", grid=(T // TM,),
        in_specs=[_rows(TM, D), _full((1, D)), _full((D, 3592)), _full((1, AW)), _full((1, AW))],
        out_specs=[_rows(TM, 1536), _rows(TM, 1536), _rows(TM, 1536), _rows(TM, DW), _rows(TM, 8), _rows(TM, D)],
        out_shape=[jax.ShapeDtypeStruct((T, 1536), F32), jax.ShapeDtypeStruct((T, 1536), _MXU),
                   jax.ShapeDtypeStruct((T, 1536), F32), jax.ShapeDtypeStruct((T, DW), F32),
                   jax.ShapeDtypeStruct((T, 8), F32), jax.ShapeDtypeStruct((T, D), _MXU)],
        compiler_params=_cp(("arbitrary",)),
    )(x, g_mix, w_in, qg_t, kg_t)


TQ = 256
TW = 768
T_LO, T_HI = 65, 256
VAR0 = 384


def _bias_tables(rb_t):
    def body(rb_ref, tab_ref, tabt_ref):
        h = pl.program_id(0)

        tab_ref[0, :, 0:VAR0] = jnp.where(
            ((_iota((TQ, VAR0), 1) >> 6) >= (_iota((TQ, VAR0), 0) >> 6)), rb_ref[h, T_HI], NEG).astype(F32)
        r = _iota((TQ, TW - VAR0), 0)
        j = _iota((TQ, TW - VAR0), 1) + VAR0
        idx = jnp.clip(r - j + 512, -128, 128) + 128
        ok = ((j >> 6) >= (r >> 6)) & ((j >> 6) <= (r >> 6) + 8)
        acc = lax.fori_loop(T_LO, T_HI, lambda t, a: jnp.where(idx == t, rb_ref[h, t], a),
                            jnp.full((TQ, TW - VAR0), rb_ref[h, T_HI], F32))
        tab_ref[0, :, VAR0:TW] = jnp.where(ok, acc, NEG)

        tabt_ref[0, 0:VAR0, :] = jnp.where(
            ((_iota((VAR0, TQ), 0) >> 6) >= (_iota((VAR0, TQ), 1) >> 6)), rb_ref[h, T_HI], NEG).astype(F32)
        jt = _iota((TW - VAR0, TQ), 0) + VAR0
        rt = _iota((TW - VAR0, TQ), 1)
        idxt = jnp.clip(rt - jt + 512, -128, 128) + 128
        okt = ((jt >> 6) >= (rt >> 6)) & ((jt >> 6) <= (rt >> 6) + 8)
        acct = lax.fori_loop(T_LO, T_HI, lambda t, a: jnp.where(idxt == t, rb_ref[h, t], a),
                             jnp.full((TW - VAR0, TQ), rb_ref[h, T_HI], F32))
        tabt_ref[0, VAR0:TW, :] = jnp.where(okt, acct, NEG)

    return pl.pallas_call(
        body, name="bias_tables", grid=(NHA,),
        in_specs=[pl.BlockSpec(memory_space=pltpu.SMEM)],
        out_specs=[pl.BlockSpec((1, TQ, TW), lambda h: (h, 0, 0)), pl.BlockSpec((1, TW, TQ), lambda h: (h, 0, 0))],
        out_shape=[jax.ShapeDtypeStruct((NHA, TQ, TW), F32), jax.ShapeDtypeStruct((NHA, TW, TQ), F32)],
        compiler_params=_cp(("arbitrary",)),
    )(rb_t)


def _bias_grad(dtabt):
    def body(d_ref, o_ref):
        d = d_ref[0]
        lane = _iota((1, VAR0), 1)
        jt = _iota((TW - VAR0, TQ), 0) + VAR0
        rt = _iota((TW - VAR0, TQ), 1)
        idxt = jnp.clip(rt - jt + 512, -128, 128) + 128
        dv = d[VAR0:TW, :]
        total = jnp.sum(jnp.sum(d, axis=0, keepdims=True), axis=1, keepdims=True)

        def step(t, carry):
            row, seen = carry
            part = jnp.sum(jnp.sum(jnp.where(idxt == t, dv, 0.0), axis=0, keepdims=True), axis=1, keepdims=True)
            return jnp.where(lane == t, part, row), seen + part

        row, seen = lax.fori_loop(T_LO, T_HI, step, (jnp.zeros((1, VAR0), F32), jnp.zeros((1, 1), F32)))
        o_ref[0] = jnp.where(lane == T_HI, total - seen, row)

    return pl.pallas_call(
        body, name="bias_grad", grid=(NHA,),
        in_specs=[pl.BlockSpec((1, TW, TQ), lambda h: (h, 0, 0))],
        out_specs=pl.BlockSpec((1, 1, VAR0), lambda h: (h, 0, 0)),
        out_shape=jax.ShapeDtypeStruct((NHA, 1, VAR0), F32),
        compiler_params=_cp(("arbitrary",)),
    )(dtabt)


def _kv_spec(col, back):
    return pl.BlockSpec((TQ, AW), lambda i: (jnp.maximum(i - back, 0), col))


def _attn_fwd(an, tab):
    T = an.shape[0]

    def body(q_ref, k2_ref, k1_ref, k0_ref, v2_ref, v1_ref, v0_ref, tab_ref, o_ref):
        i = pl.program_id(0)
        kwin = jnp.concatenate([k2_ref[...], k1_ref[...], k0_ref[...]], axis=0)
        vwin = jnp.concatenate([v2_ref[...], v1_ref[...], v0_ref[...]], axis=0)
        q = q_ref[...]
        invalid = (_iota((TQ, TW), 1) + TQ * i) < 512
        lo_half = _iota((TQ, LANES), 1) < DHA
        for p in range(NHA // 2):
            sl = slice(LANES * p, LANES * (p + 1))
            qp, kp, vp = q[:, sl], kwin[:, sl], vwin[:, sl]
            outs = []
            for half in range(2):
                mask = lo_half if half == 0 else jnp.logical_not(lo_half)
                qm = jnp.where(mask, qp, jnp.zeros_like(qp))
                s = _dot_nt(qm, kp) + tab_ref[2 * p + half]
                s = jnp.where(invalid, NEG, s)
                m = jnp.max(s, axis=1, keepdims=True)
                e = jnp.exp(s - m)
                l = jnp.sum(e, axis=1, keepdims=True)
                outs.append(_dot(e, vp) / l)
            o_ref[:, sl] = jnp.where(lo_half, outs[0], outs[1])

    return pl.pallas_call(
        body, name="attn_fwd", grid=(T // TQ,),
        in_specs=[pl.BlockSpec((TQ, AW), lambda i: (i, 0)),
                  _kv_spec(1, 2), _kv_spec(1, 1), _kv_spec(1, 0), _kv_spec(2, 2), _kv_spec(2, 1), _kv_spec(2, 0),
                  _full((NHA, TQ, TW))],
        out_specs=_rows(TQ, AW),
        out_shape=jax.ShapeDtypeStruct((T, AW), F32),
        compiler_params=_cp(("arbitrary",)),
    )(an, an, an, an, an, an, an, tab)


def _attn_bwd(an, dout, tabt):
    T = an.shape[0]
    nq = T // TQ

    def qi(i):
        return jnp.minimum(i, nq - 1)

    def kv_spec(col, back):
        return pl.BlockSpec((TQ, AW), lambda i: (jnp.maximum(qi(i) - back, 0), col))

    def body(q_ref, do_ref, k2_ref, k1_ref, k0_ref, v2_ref, v1_ref, v0_ref, tabt_ref,
             dq_ref, dk_ref, dv_ref, dtab_ref, dk_acc, dv_acc):
        i = pl.program_id(0)

        @pl.when(i == 0)
        def _():
            dtab_ref[...] = jnp.zeros_like(dtab_ref)

        new = i % 3
        dk_acc[new] = jnp.zeros((TQ, AW), F32)
        dv_acc[new] = jnp.zeros((TQ, AW), F32)

        @pl.when(i < nq)
        def _():
            kwin = jnp.concatenate([k2_ref[...], k1_ref[...], k0_ref[...]], axis=0)
            vwin = jnp.concatenate([v2_ref[...], v1_ref[...], v0_ref[...]], axis=0)
            q = q_ref[...]
            do = do_ref[...].astype(_MXU)
            invalid = (_iota((TW, TQ), 0) + TQ * i) < 512
            lo_half = _iota((TQ, LANES), 1) < DHA
            for p in range(NHA // 2):
                sl = slice(LANES * p, LANES * (p + 1))
                qp, kp, vp, dop = q[:, sl], kwin[:, sl], vwin[:, sl], do[:, sl]
                dq_pair = jnp.zeros((TQ, LANES), F32)
                dk_pair = jnp.zeros((TW, LANES), F32)
                dv_pair = jnp.zeros((TW, LANES), F32)
                for half in range(2):
                    h = 2 * p + half
                    mask = lo_half if half == 0 else jnp.logical_not(lo_half)
                    qm = jnp.where(mask, qp, jnp.zeros_like(qp))
                    dom = jnp.where(mask, dop, jnp.zeros_like(dop))
                    st = _dot_nt(kp, qm) + tabt_ref[h]
                    st = jnp.where(invalid, NEG, st)
                    m = jnp.max(st, axis=0, keepdims=True)
                    e = jnp.exp(st - m)
                    pt = e * (1.0 / jnp.sum(e, axis=0, keepdims=True))
                    dpt = _dot_nt(vp, dom)
                    delta = jnp.sum(pt * dpt, axis=0, keepdims=True)
                    dst = pt * (dpt - delta)
                    dtab_ref[h] += dst
                    dsb = dst.astype(_MXU)
                    dv_pair += _dot(pt, dom)
                    dk_pair += _dot(dsb, qm)
                    dq_pair += jnp.where(mask, _dot_tn(dsb, kp), 0.0)
                dq_ref[:, sl] = dq_pair
                for w in range(3):
                    slot = (i + 1 + w) % 3
                    rows = slice(TQ * w, TQ * (w + 1))
                    dk_acc[slot, :, sl] += dk_pair[rows]
                    dv_acc[slot, :, sl] += dv_pair[rows]

        @pl.when(i >= 2)
        def _():
            done = (i + 1) % 3
            dk_ref[...] = dk_acc[done]
            dv_ref[...] = dv_acc[done]

    back2 = pl.BlockSpec((TQ, AW), lambda i: (jnp.maximum(i - 2, 0), 0))
    return pl.pallas_call(
        body, name="attn_bwd", grid=(nq + 2,),
        in_specs=[pl.BlockSpec((TQ, AW), lambda i: (qi(i), 0)), pl.BlockSpec((TQ, AW), lambda i: (qi(i), 0)),
                  kv_spec(1, 2), kv_spec(1, 1), kv_spec(1, 0), kv_spec(2, 2), kv_spec(2, 1), kv_spec(2, 0),
                  _full((NHA, TW, TQ))],
        out_specs=[pl.BlockSpec((TQ, AW), lambda i: (qi(i), 0)), back2, back2, _full((NHA, TW, TQ))],
        out_shape=[jax.ShapeDtypeStruct((T, AW), F32), jax.ShapeDtypeStruct((T, AW), F32),
                   jax.ShapeDtypeStruct((T, AW), F32), jax.ShapeDtypeStruct((NHA, TW, TQ), F32)],
        scratch_shapes=[pltpu.VMEM((3, TQ, AW), F32), pltpu.VMEM((3, TQ, AW), F32)],
        compiler_params=_cp(("arbitrary",)),
    )(an, dout, an, an, an, an, an, an, tabt)


GR = 256
NG = TG // GR
CPT = TG // CH
CONV_K = 4


def _split3(x):
    a = x.astype(BF16)
    r = x - a.astype(F32)
    b = r.astype(BF16)
    c = (r - b.astype(F32)).astype(BF16)
    return a, b, c


def _ones_dot(ones_b, x):
    return sum(jnp.dot(ones_b, t, preferred_element_type=F32) for t in _split3(x))


def _dot_ones_nt(x, ones_b):
    dn = (((1,), (1,)), ((), ()))
    return sum(lax.dot_general(t, ones_b, dn, preferred_element_type=F32) for t in _split3(x))


def _dn_masks():
    r, c = _iota((GR, GR), 0), _iota((GR, GR), 1)
    same = (r >> 6) == (c >> 6)
    one = lambda m: jnp.where(m, 1.0, 0.0).astype(BF16)
    return dict(
        tril=same & (c <= r), strict=same & (c < r), triu=same & (c >= r), strict_u=same & (c > r),
        tril_b=one(same & (c <= r)), triu_b=one(same & (c >= r)), blk_b=one(same), eye_b=one(r == c),
        eye=jnp.where(r == c, 1.0, 0.0).astype(F32),
        fold_b=one((_iota((GR, CH), 0) & (CH - 1)) == _iota((GR, CH), 1)),
        last=(_iota((GR, 1), 0) & (CH - 1)) == CH - 1,
    )


def _shift_down(x, halo, k):
    if k == 0:
        return x
    xs = pltpu.roll(x, k, 0)
    hs = pltpu.roll(halo, k, 0)
    top = jnp.where(_iota(halo.shape, 0) < k, hs, xs[0:8])
    return jnp.concatenate([top, xs[8:]], axis=0)


def _shift_up(x, halo, k):
    if k == 0:
        return x
    n = x.shape[0]
    xs = pltpu.roll(x, n - k, 0)
    hs = pltpu.roll(halo, 8 - k, 0)
    bot = jnp.where(_iota(halo.shape, 0) >= 8 - k, hs, xs[n - 8:n])
    return jnp.concatenate([xs[0:n - 8], bot], axis=0)


def _conv(x, halo, w):
    y = x * w[CONV_K - 1:CONV_K, :]
    for k in range(1, CONV_K):
        y = y + _shift_down(x, halo, k) * w[CONV_K - 1 - k:CONV_K - k, :]
    return y


def _tri_inv(lmat, eye):
    p = -lmat
    r = eye + p
    for _ in range(5):
        p = _dot(p, p)
        r = r + _dot(r, p)
    return r


def _gate_terms(ba_g, bat_g, alog8, dtb8, alog8t, dtb8t, K):
    g8 = -jnp.exp(alog8) * _softplus(ba_g + dtb8)
    g8t = -jnp.exp(alog8t) * _softplus(bat_g + dtb8t)
    gc8 = _ones_dot(K["tril_b"], g8)
    gl8 = _ones_dot(K["blk_b"], g8)
    gcrow8 = _dot_ones_nt(g8t, K["tril_b"])
    return g8, gc8, gl8, gcrow8


def _dn_head(c_tile, rows, h, beta8, gc8, gl8, gcrow8, K):
    qr = c_tile[rows, DHD * h:DHD * (h + 1)]
    kr = c_tile[rows, DW + DHD * h:DW + DHD * (h + 1)]
    v = c_tile[rows, 2 * DW + DHD * h:2 * DW + DHD * (h + 1)]
    rq = lax.rsqrt(jnp.sum(qr * qr, axis=1, keepdims=True) + EPS)
    rk = lax.rsqrt(jnp.sum(kr * kr, axis=1, keepdims=True) + EPS)
    qh, kn = qr * rq, kr * rk
    qn = qh * (DHD ** -0.5)
    beta = _col(beta8, h)
    gccol, glcol, gcrow = _col(gc8, NHD + h), _col(gl8, NHD + h), _row(gcrow8, NHD + h)
    diff = gccol - gcrow
    gam_m = jnp.exp(jnp.where(K["tril"], diff, NEG))
    gam = jnp.exp(gccol)
    egl = jnp.exp(glcol - gccol)
    kb, vb = kn * beta, v * beta
    kg = kb * gam
    pl_ = _dot_nt(kb, kn)
    lmat = jnp.where(K["strict"], pl_ * gam_m, 0.0)
    tm = _tri_inv(lmat, K["eye"])
    u = _dot(tm, vb)
    w = _dot(tm, kg)
    pm = _dot_nt(qn, kn)
    mm = pm * gam_m
    return dict(qr=qr, kr=kr, v=v, rq=rq, rk=rk, qh=qh, qn=qn, kn=kn, beta=beta, diff=diff, gam_m=gam_m, gam=gam,
                egl=egl, el=jnp.exp(glcol), kb=kb, vb=vb, kg=kg, pl=pl_, pm=pm, tm=tm, u=u, w=w, mm=mm,
                qd=qn * gam, kd=kn * egl)


def _halo_prev(width):
    return pl.BlockSpec((8, width), lambda i: (jnp.maximum(i * (TG // 8) - 1, 0), 0))


def _dn_prep(draw, conv_w, ba, bat, alog8, dtb8, alog8t, dtb8t):
    T = draw.shape[0]
    nb = T // TG
    hm = lambda w, dt: jax.ShapeDtypeStruct((NHD, T, w), dt)
    hm_spec = lambda w: pl.BlockSpec((NHD, TG, w), lambda i: (0, i, 0))
    tr = jax.ShapeDtypeStruct((NHD, T // CH, DHD, CH), _MXU)
    tr_spec = pl.BlockSpec((NHD, CPT, DHD, CH), lambda i: (0, i, 0, 0))

    def body(x_ref, halo_ref, cw_ref, ba_ref, bat_ref, al_ref, dt_ref, alt_ref, dtt_ref,
             u_ref, w_ref, qd_ref, kd_ref, mc_ref, mct_ref, kdt_ref, qdt_ref, wt_ref, elb_ref):
        i = pl.program_id(0)
        K = _dn_masks()
        halo = jnp.where(i > 0, halo_ref[...], 0.0)
        cv = _conv(x_ref[...], halo, cw_ref[...])
        c_tile = cv * _sigmoid(cv)
        eye128 = jnp.where(_iota((DHD, DHD), 0) == _iota((DHD, DHD), 1), 1.0, 0.0).astype(_MXU)
        for g in range(NG):
            rows = slice(GR * g, GR * (g + 1))
            ba_g = ba_ref[rows, :]
            _, gc8, gl8, gcrow8 = _gate_terms(ba_g, bat_ref[:, rows], al_ref[...], dt_ref[...], alt_ref[...],
                                              dtt_ref[...], K)
            beta8 = _sigmoid(ba_g)
            for h in range(NHD):
                d = _dn_head(c_tile, rows, h, beta8, gc8, gl8, gcrow8, K)
                gam_t = jnp.exp(jnp.where(K["triu"], -d["diff"], NEG))
                mmt = _dot_nt(d["kn"], d["qn"]) * gam_t
                u_ref[h, rows, :] = d["u"]
                w_ref[h, rows, :] = d["w"].astype(_MXU)
                qd_ref[h, rows, :] = d["qd"].astype(_MXU)
                kd_ref[h, rows, :] = d["kd"].astype(_MXU)
                mc_ref[h, rows, :] = _dot(d["mm"], K["fold_b"]).astype(_MXU)
                mct_ref[h, rows, :] = _dot(mmt, K["fold_b"]).astype(_MXU)
                elb = jnp.broadcast_to(d["el"], (GR, DHD))
                for cc in range(GR // CH):
                    ch = slice(CH * cc, CH * (cc + 1))
                    n = (GR // CH) * g + cc
                    kdt_ref[h, n] = _dot_nt(eye128, d["kd"][ch]).astype(_MXU)
                    qdt_ref[h, n] = _dot_nt(eye128, d["qd"][ch]).astype(_MXU)
                    wt_ref[h, n] = _dot_nt(eye128, d["w"][ch]).astype(_MXU)
                    elb_ref[n:n + 1, DHD * h:DHD * (h + 1)] = elb[CH * cc:CH * cc + 1, :]

    return pl.pallas_call(
        body, name="dn_prep", grid=(nb,),
        in_specs=[_rows(TG, 1536), _halo_prev(1536), _full((CONV_K, 1536)), _rows(TG, 8),
                  pl.BlockSpec((8, TG), lambda i: (0, i)), _full((1, 8)), _full((1, 8)), _full((8, 1)), _full((8, 1))],
        out_specs=[hm_spec(DHD), hm_spec(DHD), hm_spec(DHD), hm_spec(DHD), hm_spec(CH), hm_spec(CH),
                   tr_spec, tr_spec, tr_spec, pl.BlockSpec((CPT, NHD * DHD), lambda i: (i, 0))],
        out_shape=[hm(DHD, F32), hm(DHD, _MXU), hm(DHD, _MXU), hm(DHD, _MXU), hm(CH, _MXU), hm(CH, _MXU),
                   tr, tr, tr, jax.ShapeDtypeStruct((T // CH, NHD * DHD), F32)],
        compiler_params=_cp(("arbitrary",)),
    )(draw, draw, conv_w, ba, bat, alog8, dtb8, alog8t, dtb8t)


def _dn_scan(u, w, qd, mc, kdt, elb):
    T = u.shape[1]
    nb = T // TG
    hm_spec = lambda wd: pl.BlockSpec((NHD, TG, wd), lambda i: (0, i, 0))

    def body(u_ref, w_ref, qd_ref, mc_ref, kdt_ref, elb_ref, o_ref, vn_ref, sn_ref, S):
        @pl.when(pl.program_id(0) == 0)
        def _():
            S[...] = jnp.zeros_like(S)

        sub8 = _iota((CPT, DHD), 0)

        def chunk(cc, carry):
            r0 = pl.multiple_of(cc * CH, CH)
            rs = pl.ds(r0, CH)
            for h in range(NHD):
                sh = S[h]
                sb = sh.astype(_MXU)
                sn_ref[cc, h] = sb
                vn = u_ref[h, rs, :] - _dot(w_ref[h, rs, :], sb)
                vnb = vn.astype(_MXU)
                o_ref[h, rs, :] = _dot(qd_ref[h, rs, :], sb) + _dot(mc_ref[h, rs, :], vnb)
                vn_ref[h, rs, :] = vnb
                el = jnp.sum(jnp.where(sub8 == cc, elb_ref[:, DHD * h:DHD * (h + 1)], 0.0), axis=0, keepdims=True)
                S[h] = sh * el + _dot(kdt_ref[h, cc], vnb)
            return carry

        lax.fori_loop(0, CPT, chunk, 0)

    return pl.pallas_call(
        body, name="dn_scan", grid=(nb,),
        in_specs=[hm_spec(DHD), hm_spec(DHD), hm_spec(DHD), hm_spec(CH),
                  pl.BlockSpec((NHD, CPT, DHD, CH), lambda i: (0, i, 0, 0)),
                  pl.BlockSpec((CPT, NHD * DHD), lambda i: (i, 0))],
        out_specs=[hm_spec(DHD), hm_spec(DHD), pl.BlockSpec((CPT, NHD, DHD, DHD), lambda i: (i, 0, 0, 0))],
        out_shape=[jax.ShapeDtypeStruct((NHD, T, DHD), F32), jax.ShapeDtypeStruct((NHD, T, DHD), _MXU),
                   jax.ShapeDtypeStruct((T // CH, NHD, DHD, DHD), _MXU)],
        scratch_shapes=[pltpu.VMEM((NHD, DHD, DHD), F32)],
        compiler_params=_cp(("arbitrary",)),
    )(u, w, qd, mc, kdt, elb)


def _dn_scan_bwd(do, mct, kd, qdt, wt, sn, vn, elb):
    T = do.shape[1]
    nb = T // TG
    rev = lambda wd: pl.BlockSpec((NHD, TG, wd), lambda i: (0, nb - 1 - i, 0))
    rev_t = pl.BlockSpec((NHD, CPT, DHD, CH), lambda i: (0, nb - 1 - i, 0, 0))

    def body(do_ref, mct_ref, kd_ref, qdt_ref, wt_ref, sn_ref, vn_ref, elb_ref,
             du_ref, dw_ref, dqd_ref, dkd_ref, dgx_ref, dS):
        @pl.when(pl.program_id(0) == 0)
        def _():
            dS[...] = jnp.zeros_like(dS)

        last_row = _iota((CH, DHD), 0) == CH - 1
        sub8 = _iota((CPT, DHD), 0)

        def chunk(k, carry):
            cc = CPT - 1 - k
            r0 = pl.multiple_of(cc * CH, CH)
            rs = pl.ds(r0, CH)
            for h in range(NHD):
                dsh = dS[h]
                dsb = dsh.astype(_MXU)
                sn = sn_ref[cc, h]
                doc = do_ref[h, rs, :].astype(_MXU)
                dvn = _dot(mct_ref[h, rs, :], doc) + _dot(kd_ref[h, rs, :], dsb)
                dvnb = dvn.astype(_MXU)
                du_ref[h, rs, :] = dvn
                dqd_ref[h, rs, :] = _dot_nt(doc, sn)
                dkd_ref[h, rs, :] = _dot_nt(vn_ref[h, rs, :], dsb)
                dw_ref[h, rs, :] = -_dot_nt(dvnb, sn)
                el = jnp.sum(jnp.where(sub8 == cc, elb_ref[:, DHD * h:DHD * (h + 1)], 0.0), axis=0, keepdims=True)
                part = jnp.sum(dsh * sn.astype(F32), axis=0, keepdims=True) * el
                dgx_ref[h, rs, :] = jnp.where(last_row, part, 0.0)
                dS[h] = _dot(qdt_ref[h, cc], doc) + dsh * el - _dot(wt_ref[h, cc], dvnb)
            return carry

        lax.fori_loop(0, CPT, chunk, 0)

    o = jax.ShapeDtypeStruct((NHD, T, DHD), F32)
    return pl.pallas_call(
        body, name="dn_scan_bwd", grid=(nb,),
        in_specs=[rev(DHD), rev(CH), rev(DHD), rev_t, rev_t,
                  pl.BlockSpec((CPT, NHD, DHD, DHD), lambda i: (nb - 1 - i, 0, 0, 0)), rev(DHD),
                  pl.BlockSpec((CPT, NHD * DHD), lambda i: (nb - 1 - i, 0))],
        out_specs=[rev(DHD)] * 5,
        out_shape=[o] * 5,
        scratch_shapes=[pltpu.VMEM((NHD, DHD, DHD), F32)],
        compiler_params=_cp(("arbitrary",)),
    )(do, mct, kd, qdt, wt, sn, vn, elb)


def _put_col(acc, k, col):
    return jnp.where(_iota(acc.shape, 1) == k, col, acc)


def _dn_post_bwd(draw, conv_w, ba, bat, alog8, dtb8, alog8t, dtb8t, du, dw, dqd, dkd, dgx, do, vn):
    T = draw.shape[0]
    nb = T // TG
    hm_spec = lambda wd: pl.BlockSpec((NHD, TG, wd), lambda i: (0, i, 0))

    def body(x_ref, halo_ref, cw_ref, ba_ref, bat_ref, al_ref, dt_ref, alt_ref, dtt_ref,
             du_ref, dw_ref, dqd_ref, dkd_ref, dgx_ref, do_ref, vn_ref, dc_ref, dba_ref, sm_ref):
        i = pl.program_id(0)

        @pl.when(i == 0)
        def _():
            sm_ref[...] = jnp.zeros_like(sm_ref)

        K = _dn_masks()
        halo = jnp.where(i > 0, halo_ref[...], 0.0)
        cv = _conv(x_ref[...], halo, cw_ref[...])
        sg = _sigmoid(cv)
        c_tile = cv * sg
        dsilu = sg * (1.0 + cv * (1.0 - sg))
        for g in range(NG):
            rows = slice(GR * g, GR * (g + 1))
            ba_g = ba_ref[rows, :]
            g8, gc8, gl8, gcrow8 = _gate_terms(ba_g, bat_ref[:, rows], al_ref[...], dt_ref[...], alt_ref[...],
                                               dtt_ref[...], K)
            beta8 = _sigmoid(ba_g)
            dgc8 = jnp.zeros((GR, 8), F32)
            rd8 = jnp.zeros((GR, 8), F32)
            dbeta8 = jnp.zeros((GR, 8), F32)
            for h in range(NHD):
                d = _dn_head(c_tile, rows, h, beta8, gc8, gl8, gcrow8, K)
                gam_m, gam, egl = d["gam_m"], d["gam"], d["egl"]
                gam_t = jnp.exp(jnp.where(K["triu"], -d["diff"], NEG))
                qn, kn, kb, vv = d["qn"], d["kn"], d["kb"], d["v"]
                duh, dwh = du_ref[h, rows, :], dw_ref[h, rows, :]
                dqdh, dkdh = dqd_ref[h, rows, :], dkd_ref[h, rows, :]
                doh, vnh = do_ref[h, rows, :], vn_ref[h, rows, :]
                tt = _dot_nt(K["eye_b"].astype(_MXU), d["tm"])
                dvb = _dot(tt, duh)
                dkg = _dot(tt, dwh)
                da = -(_dot_nt(dvb, d["u"]) + _dot_nt(dkg, d["w"]))
                dat = -(_dot_nt(d["u"], dvb) + _dot_nt(d["w"], dkg))
                dpl = jnp.where(K["strict"], da, 0.0) * gam_m
                dplt = jnp.where(K["strict_u"], dat, 0.0) * gam_t
                dpm = jnp.where(K["tril"], _dot_nt(doh, vnh), 0.0) * gam_m
                dpmt = jnp.where(K["triu"], _dot_nt(vnh, doh), 0.0) * gam_t
                plt = _dot_nt(kn, kb)
                pmt = _dot_nt(kn, qn)
                dkb = _dot(dpl, kn) + dkg * gam
                dkn = _dot(dplt, kb) + _dot(dpmt, qn) + dkdh * egl + dkb * d["beta"]
                dqn = _dot(dpm, kn) + dqdh * gam
                gsum = jnp.sum(dpl * d["pl"] + dpm * d["pm"], axis=1, keepdims=True)
                gsum_t = jnp.sum(dplt * plt + dpmt * pmt, axis=1, keepdims=True)
                rd = jnp.sum(dkdh * d["kd"], axis=1, keepdims=True)
                dgc = (gsum - gsum_t + jnp.sum(dqdh * d["qd"], axis=1, keepdims=True)
                       + jnp.sum(dkg * d["kg"], axis=1, keepdims=True) - rd
                       + jnp.sum(dgx_ref[h, rows, :], axis=1, keepdims=True))
                dgc8 = _put_col(dgc8, NHD + h, dgc)
                rd8 = _put_col(rd8, NHD + h, rd)
                dbeta = jnp.sum(dkb * kn, axis=1, keepdims=True) + jnp.sum(dvb * vv, axis=1, keepdims=True)
                dbeta8 = _put_col(dbeta8, h, dbeta)
                dqh = dqn * (DHD ** -0.5)
                qh = d["qh"]
                dqr = d["rq"] * (dqh - qh * jnp.sum(dqh * qh, axis=1, keepdims=True))
                dkr = d["rk"] * (dkn - kn * jnp.sum(dkn * kn, axis=1, keepdims=True))
                cq = slice(DHD * h, DHD * (h + 1))
                ck = slice(DW + DHD * h, DW + DHD * (h + 1))
                cvv = slice(2 * DW + DHD * h, 2 * DW + DHD * (h + 1))
                dc_ref[rows, cq] = dqr * dsilu[rows, cq]
                dc_ref[rows, ck] = dkr * dsilu[rows, ck]
                dc_ref[rows, cvv] = dvb * d["beta"] * dsilu[rows, cvv]
            dgc8 = dgc8 + jnp.where(K["last"], _ones_dot(K["blk_b"], rd8), 0.0)
            dg8 = _ones_dot(K["triu_b"], dgc8)
            sgm = _sigmoid(ba_g + dt_ref[...])
            dalpha = dg8 * (-jnp.exp(al_ref[...])) * sgm
            lane8 = _iota((GR, 8), 1)
            dba_ref[rows, :] = jnp.where(lane8 < NHD, dbeta8 * beta8 * (1.0 - beta8), dalpha)
            valid = lane8 >= NHD
            sm_ref[0:1, 0:8] += jnp.sum(jnp.where(valid, dg8 * g8, 0.0), axis=0, keepdims=True)
            sm_ref[1:2, 0:8] += jnp.sum(jnp.where(valid, dalpha, 0.0), axis=0, keepdims=True)

    return pl.pallas_call(
        body, name="dn_post_bwd", grid=(nb,),
        in_specs=[_rows(TG, 1536), _halo_prev(1536), _full((CONV_K, 1536)), _rows(TG, 8),
                  pl.BlockSpec((8, TG), lambda i: (0, i)), _full((1, 8)), _full((1, 8)), _full((8, 1)), _full((8, 1)),
                  hm_spec(DHD), hm_spec(DHD), hm_spec(DHD), hm_spec(DHD), hm_spec(DHD), hm_spec(DHD), hm_spec(DHD)],
        out_specs=[_rows(TG, 1536), _rows(TG, 8), _full((8, LANES))],
        out_shape=[jax.ShapeDtypeStruct((T, 1536), F32), jax.ShapeDtypeStruct((T, 8), F32),
                   jax.ShapeDtypeStruct((8, LANES), F32)],
        compiler_params=_cp(("arbitrary",)),
    )(draw, draw, conv_w, ba, bat, alog8, dtb8, alog8t, dtb8t, du, dw, dqd, dkd, dgx, do, vn)


def _conv_bwd(dc, draw, conv_w):
    T = dc.shape[0]
    nb = T // TG

    def body(dc_ref, nxt_ref, x_ref, halo_ref, cw_ref, dx_ref, dcw_ref):
        i = pl.program_id(0)

        @pl.when(i == 0)
        def _():
            dcw_ref[...] = jnp.zeros_like(dcw_ref)

        dcv = dc_ref[...]
        nxt = jnp.where(i < nb - 1, nxt_ref[...], 0.0)
        halo = jnp.where(i > 0, halo_ref[...], 0.0)
        xv = x_ref[...]
        w = cw_ref[...]
        dx = dcv * w[CONV_K - 1:CONV_K, :]
        dcw_ref[CONV_K - 1:CONV_K, :] += jnp.sum(dcv * xv, axis=0, keepdims=True)
        for k in range(1, CONV_K):
            j = CONV_K - 1 - k
            dx = dx + _shift_up(dcv, nxt, k) * w[j:j + 1, :]
            dcw_ref[j:j + 1, :] += jnp.sum(dcv * _shift_down(xv, halo, k), axis=0, keepdims=True)
        dx_ref[...] = dx

    return pl.pallas_call(
        body, name="conv_bwd", grid=(nb,),
        in_specs=[_rows(TG, 1536),
                  pl.BlockSpec((8, 1536), lambda i: (jnp.minimum((i + 1) * (TG // 8), T // 8 - 1), 0)),
                  _rows(TG, 1536), _halo_prev(1536), _full((CONV_K, 1536))],
        out_specs=[_rows(TG, 1536), _full((8, 1536))],
        out_shape=[jax.ShapeDtypeStruct((T, 1536), F32), jax.ShapeDtypeStruct((8, 1536), F32)],
        compiler_params=_cp(("arbitrary",)),
    )(dc, dc, draw, draw, conv_w)


def _rms(x):
    return lax.rsqrt(jnp.mean(x * x, axis=1, keepdims=True) + EPS)


def _rms_bwd(dy, xh, r, g):
    dxh = dy * g
    return r * (dxh - xh * jnp.mean(dxh * xh, axis=1, keepdims=True))


def _hm_rows(tm):
    return pl.BlockSpec((NHD, tm, DHD), lambda i: (0, i, 0))


def _post_mix(apre, o, z, x, w_out, g_a, g_dn):
    T = x.shape[0]

    def body(ap_ref, o_ref, z_ref, x_ref, w_ref, ga_ref, gd_ref, x1_ref, mix_ref):
        ap = ap_ref[...]
        parts = [ap * _rms(ap) * ga_ref[...]]
        zz = z_ref[...]
        for h in range(NHD):
            oh = o_ref[h]
            zh = zz[:, DHD * h:DHD * (h + 1)]
            parts.append(oh * _rms(oh) * gd_ref[...] * (zh * _sigmoid(zh)))
        mix = jnp.concatenate(parts, axis=1).astype(_MXU)
        mix_ref[...] = mix
        x1_ref[...] = x_ref[...] + jnp.dot(mix, w_ref[...], preferred_element_type=F32)

    return pl.pallas_call(
        body, name="post_mix", grid=(T // TM,),
        in_specs=[_rows(TM, AW), _hm_rows(TM), _rows(TM, DW), _rows(TM, D), _full((D, D)), _full((1, AW)),
                  _full((1, DHD))],
        out_specs=[_rows(TM, D), _rows(TM, D)],
        out_shape=[jax.ShapeDtypeStruct((T, D), F32), jax.ShapeDtypeStruct((T, D), _MXU)],
        compiler_params=_cp(("arbitrary",)),
    )(apre, o, z, x, w_out, g_a, g_dn)


def _ffn(x1, tgt, w_gate, w_up, w_down, g_ffn):
    T = x1.shape[0]

    def body(x_ref, t_ref, wg_hbm, wu_hbm, wd_hbm, g_ref,
             dx1_ref, dx1b_ref, h2_ref, act_ref, dgu_ref, dyb_ref, loss_ref, dg_ref, wg, wu, wd, sem):
        @pl.when(pl.program_id(0) == 0)
        def _():
            cps = [pltpu.make_async_copy(s, d, sem.at[k]) for k, (s, d) in
                   enumerate(((wg_hbm, wg), (wu_hbm, wu), (wd_hbm, wd)))]
            for cp in cps:
                cp.start()
            for cp in cps:
                cp.wait()
            loss_ref[...] = jnp.zeros_like(loss_ref)
            dg_ref[...] = jnp.zeros_like(dg_ref)

        xv = x_ref[...]
        r = _rms(xv)
        xh = xv * r
        gg = g_ref[...]
        h2 = (xh * gg).astype(_MXU)
        h2_ref[...] = h2
        gate = jnp.dot(h2, wg[...], preferred_element_type=F32)
        up = jnp.dot(h2, wu[...], preferred_element_type=F32)
        sg = _sigmoid(gate)
        silu = gate * sg
        act = (silu * up).astype(_MXU)
        act_ref[...] = act
        y = xv + jnp.dot(act, wd[...], preferred_element_type=F32)
        err = y - t_ref[...]
        loss_ref[...] += jnp.sum(err * err, axis=0, keepdims=True)
        dy = err * (1.0 / D)
        dyb = dy.astype(_MXU)
        dyb_ref[...] = dyb
        dact = lax.dot_general(dyb, wd[...], (((1,), (1,)), ((), ())), preferred_element_type=F32)
        dgate = (dact * up * (sg * (1.0 + gate * (1.0 - sg)))).astype(_MXU)
        dup = (dact * silu).astype(_MXU)
        dgu_ref[:, 0:FF] = dgate
        dgu_ref[:, FF:2 * FF] = dup
        nt = (((1,), (1,)), ((), ()))
        dh2 = (lax.dot_general(dgate, wg[...], nt, preferred_element_type=F32)
               + lax.dot_general(dup, wu[...], nt, preferred_element_type=F32))
        dg_ref[...] += jnp.sum(dh2 * xh, axis=0, keepdims=True)
        dx1 = dy + _rms_bwd(dh2, xh, r, gg)
        dx1_ref[...] = dx1
        dx1b_ref[...] = dx1.astype(_MXU)

    anyspec = pl.BlockSpec(memory_space=pl.ANY)
    sd = lambda w, dt: jax.ShapeDtypeStruct((T, w), dt)
    return pl.pallas_call(
        body, name="ffn", grid=(T // TM,),
        in_specs=[_rows(TM, D), _rows(TM, D), anyspec, anyspec, anyspec, _full((1, D))],
        out_specs=[_rows(TM, D), _rows(TM, D), _rows(TM, D), _rows(TM, FF), _rows(TM, 2 * FF), _rows(TM, D),
                   _full((1, D)), _full((1, D))],
        out_shape=[sd(D, F32), sd(D, _MXU), sd(D, _MXU), sd(FF, _MXU), sd(2 * FF, _MXU), sd(D, _MXU),
                   jax.ShapeDtypeStruct((1, D), F32), jax.ShapeDtypeStruct((1, D), F32)],
        scratch_shapes=[pltpu.VMEM((D, FF), _MXU), pltpu.VMEM((D, FF), _MXU), pltpu.VMEM((FF, D), _MXU),
                        pltpu.SemaphoreType.DMA((3,))],
        compiler_params=_cp(("arbitrary",)),
    )(x1, tgt, w_gate, w_up, w_down, g_ffn)


def _mix_bwd(dx1b, w_out, apre, o, z, g_a, g_dn):
    T = dx1b.shape[0]

    def body(dx_ref, w_ref, ap_ref, o_ref, z_ref, ga_ref, gd_ref, dap_ref, do_ref, dz_ref, dga_ref, dgd_ref):
        @pl.when(pl.program_id(0) == 0)
        def _():
            dga_ref[...] = jnp.zeros_like(dga_ref)
            dgd_ref[...] = jnp.zeros_like(dgd_ref)

        dmix = lax.dot_general(dx_ref[...], w_ref[...], (((1,), (1,)), ((), ())), preferred_element_type=F32)
        ap = ap_ref[...]
        ra = _rms(ap)
        ah = ap * ra
        da = dmix[:, 0:AW]
        dga_ref[...] += jnp.sum(da * ah, axis=0, keepdims=True)
        dap_ref[...] = _rms_bwd(da, ah, ra, ga_ref[...])
        zz = z_ref[...]
        gd = gd_ref[...]
        for h in range(NHD):
            cs = slice(DHD * h, DHD * (h + 1))
            dd = dmix[:, AW + DHD * h:AW + DHD * (h + 1)]
            oh = o_ref[h]
            ro = _rms(oh)
            ohh = oh * ro
            zh = zz[:, cs]
            sz = _sigmoid(zh)
            dz_ref[:, cs] = dd * (ohh * gd) * (sz * (1.0 + zh * (1.0 - sz)))
            don = dd * (zh * sz)
            dgd_ref[...] += jnp.sum(don * ohh, axis=0, keepdims=True)
            do_ref[h] = _rms_bwd(don, ohh, ro, gd)

    return pl.pallas_call(
        body, name="mix_bwd", grid=(T // TM,),
        in_specs=[_rows(TM, D), _full((D, D)), _rows(TM, AW), _hm_rows(TM), _rows(TM, DW), _full((1, AW)),
                  _full((1, DHD))],
        out_specs=[_rows(TM, AW), _hm_rows(TM), _rows(TM, DW), _full((1, AW)), _full((1, DHD))],
        out_shape=[jax.ShapeDtypeStruct((T, AW), F32), jax.ShapeDtypeStruct((NHD, T, DHD), F32),
                   jax.ShapeDtypeStruct((T, DW), F32), jax.ShapeDtypeStruct((1, AW), F32),
                   jax.ShapeDtypeStruct((1, DHD), F32)],
        compiler_params=_cp(("arbitrary",)),
    )(dx1b, w_out, apre, o, z, g_a, g_dn)


def _inproj_bwd(dqn, dkn, dv, araw, ddraw, dz, dba, x, dx1, w_in, g_mix, qg_t, kg_t):
    T = x.shape[0]

    def body(dqn_ref, dkn_ref, dv_ref, ar_ref, dd_ref, dz_ref, dba_ref, x_ref, dx1_ref, w_ref, g_ref, qg_ref, kg_ref,
             dx_ref, dp_ref, dgm_ref, dqg_ref, dkg_ref):
        @pl.when(pl.program_id(0) == 0)
        def _():
            dgm_ref[...] = jnp.zeros_like(dgm_ref)
            dqg_ref[...] = jnp.zeros_like(dqg_ref)
            dkg_ref[...] = jnp.zeros_like(dkg_ref)

        bd = _block_ones(AW, DHA)

        def head_norm_bwd(raw, dyn, gain, dg_ref):
            r = _head_rms(raw, bd, DHA)
            xh = raw * r
            dg_ref[...] += jnp.sum(dyn * xh, axis=0, keepdims=True)
            dxh = dyn * gain
            return r * (dxh - xh * (_dot_x2(dxh * xh, bd) * (1.0 / DHA)))

        ar = ar_ref[...]
        dq = head_norm_bwd(ar[:, 0:AW], dqn_ref[...] * (DHA ** -0.5), qg_ref[...], dqg_ref)
        dk = head_norm_bwd(ar[:, AW:2 * AW], dkn_ref[...], kg_ref[...], dkg_ref)
        nt = (((1,), (1,)), ((), ()))
        dh = jnp.zeros((TM, D), F32)
        for lo, val in ((0, dq), (AW, dk), (2 * AW, dv_ref[...]), (1536, dd_ref[...]), (3072, dz_ref[...]),
                        (3584, dba_ref[...])):
            vb = val.astype(_MXU)
            wd_ = val.shape[1]
            dp_ref[:, lo:lo + wd_] = vb
            dh = dh + lax.dot_general(vb, w_ref[:, lo:lo + wd_], nt, preferred_element_type=F32)
        xv = x_ref[...]
        r = _rms(xv)
        xh = xv * r
        dgm_ref[...] += jnp.sum(dh * xh, axis=0, keepdims=True)
        dx_ref[...] = dx1_ref[...] + _rms_bwd(dh, xh, r, g_ref[...])

    return pl.pallas_call(
        body, name="inproj_bwd", grid=(T // TM,),
        in_specs=[_rows(TM, AW), _rows(TM, AW), _rows(TM, AW), _rows(TM, 1536), _rows(TM, 1536), _rows(TM, DW),
                  _rows(TM, 8), _rows(TM, D), _rows(TM, D), _full((D, 3592)), _full((1, D)), _full((1, AW)),
                  _full((1, AW))],
        out_specs=[_rows(TM, D), _rows(TM, 3592), _full((1, D)), _full((1, AW)), _full((1, AW))],
        out_shape=[jax.ShapeDtypeStruct((T, D), F32), jax.ShapeDtypeStruct((T, 3592), _MXU),
                   jax.ShapeDtypeStruct((1, D), F32), jax.ShapeDtypeStruct((1, AW), F32),
                   jax.ShapeDtypeStruct((1, AW), F32)],
        compiler_params=_cp(("arbitrary",)),
    )(dqn, dkn, dv, araw, ddraw, dz, dba, x, dx1, w_in, g_mix, qg_t, kg_t)


def _wgrad(a, b, name, tk=512, tn=None):
    T, M = a.shape
    N = b.shape[1]
    tn = N if tn is None else tn
    nk = T // tk

    def body(a_ref, b_ref, o_ref):
        @pl.when(pl.program_id(1) == 0)
        def _():
            o_ref[...] = jnp.zeros_like(o_ref)

        o_ref[...] += lax.dot_general(a_ref[...], b_ref[...], (((0,), (0,)), ((), ())), preferred_element_type=F32)

    return pl.pallas_call(
        body, name=name, grid=(N // tn, nk),
        in_specs=[pl.BlockSpec((tk, M), lambda j, k: (k, 0)), pl.BlockSpec((tk, tn), lambda j, k: (k, j))],
        out_specs=pl.BlockSpec((M, tn), lambda j, k: (0, j)),
        out_shape=jax.ShapeDtypeStruct((M, N), F32),
        compiler_params=_cp(("arbitrary", "arbitrary")),
    )(a, b)


def _adamw(parts, w, m, v, name, tr):
    K, R, W = parts.shape

    def body(p_ref, w_ref, m_ref, v_ref, g_ref, d_ref, nm_ref, nv_ref):
        g = p_ref[0]
        for k in range(1, K):
            g = g + p_ref[k]
        g_ref[...] = g
        nm = ADAM_B1 * m_ref[...] + (1.0 - ADAM_B1) * g
        nv = ADAM_B2 * v_ref[...] + (1.0 - ADAM_B2) * (g * g)
        nm_ref[...] = nm
        nv_ref[...] = nv
        m_hat = nm / (1.0 - ADAM_B1 ** ADAM_STEP)
        v_hat = nv / (1.0 - ADAM_B2 ** ADAM_STEP)
        d_ref[...] = -ADAM_LR * (m_hat / (jnp.sqrt(v_hat) + ADAM_EPS) + ADAM_WD * w_ref[...])

    o = jax.ShapeDtypeStruct((R, W), F32)
    return pl.pallas_call(
        body, name=name, grid=(R // tr,),
        in_specs=[pl.BlockSpec((K, tr, W), lambda i: (0, i, 0)), _rows(tr, W), _rows(tr, W), _rows(tr, W)],
        out_specs=[_rows(tr, W)] * 4,
        out_shape=[o] * 4,
        compiler_params=_cp(("arbitrary",)),
    )(parts, w, m, v)


SM_ROWS = 112
R_GMIX, R_GFFN, R_QG, R_KG, R_GA, R_GDN, R_ALOG, R_DT, R_LOSS, R_CONV, R_REL = 0, 8, 16, 20, 24, 28, 29, 30, 32, 40, 88


def _small_reduce(gathered):
    def body(p_ref, o_ref):
        s = p_ref[0]
        for k in range(1, N_DEV):
            s = s + p_ref[k]
        o_ref[...] = s
        for r0 in (R_QG, R_KG):
            rs = jnp.sum(s[r0:r0 + 4], axis=0, keepdims=True)
            o_ref[r0:r0 + 1, :] = rs + pltpu.roll(rs, DHA, 1)
        tot = jnp.sum(jnp.sum(s[R_LOSS:R_LOSS + 8], axis=0, keepdims=True), axis=1, keepdims=True)
        o_ref[R_LOSS:R_LOSS + 1, :] = jnp.broadcast_to(tot * (0.5 / D), (1, LANES))

    return pl.pallas_call(
        body, name="small_reduce",
        out_shape=jax.ShapeDtypeStruct((SM_ROWS, LANES), F32),
    )(gathered)


RB = 1664
_BIG = (("w_in", 449), ("w_out", 128), ("w_gate", 352), ("w_up", 352), ("w_down", 352))


def _pack_big(parts):
    rows = [p.reshape(-1, D) for p in parts]
    n = sum(r.shape[0] for r in rows)
    return jnp.concatenate(rows + [jnp.zeros((RB - n, D), rows[0].dtype)], axis=0)


def _unpack_big(packed, shapes):
    out, r = [], 0
    for (name, nr), shp in zip(_BIG, shapes):
        out.append(packed[r:r + nr].reshape(shp))
        r += nr
    return out


def _pack_lanes(parts):
    rows = []
    for p in parts:
        f = p.reshape(-1)
        pad = (-f.shape[0]) % LANES
        rows.append(jnp.pad(f, (0, pad)).reshape(-1, LANES))
    return jnp.concatenate(rows, axis=0)


def _unpack_lanes(packed, shapes):
    out, r = [], 0
    for shp in shapes:
        n = math.prod(shp)
        nr = -(-n // LANES)
        out.append(packed[r:r + nr].reshape(-1)[:n].reshape(shp))
        r += nr
    return out


def kernel(x, norm_mix_g, w_in, attn_q_norm_g, attn_k_norm_g, rel_bias, attn_out_norm_g, conv_w, a_log, dt_bias, dn_out_norm_g, w_out, norm_ffn_g, w_gate, w_up, w_down, loss_target, m_norm_mix_g, m_w_in, m_attn_q_norm_g, m_attn_k_norm_g, m_rel_bias, m_attn_out_norm_g, m_conv_w, m_a_log, m_dt_bias, m_dn_out_norm_g, m_w_out, m_norm_ffn_g, m_w_gate, m_w_up, m_w_down, v_norm_mix_g, v_w_in, v_attn_q_norm_g, v_attn_k_norm_g, v_rel_bias, v_attn_out_norm_g, v_conv_w, v_a_log, v_dt_bias, v_dn_out_norm_g, v_w_out, v_norm_ffn_g, v_w_gate, v_w_up, v_w_down):
    xs, tgt = x[0], loss_target[0]
    T = xs.shape[0]
    my_idx = 4 * lax.axis_index("x") + 2 * lax.axis_index("y") + lax.axis_index("c")
    big_w = (w_in[0], w_out[0], w_gate[0], w_up[0], w_down[0])
    big_shapes = [w.shape for w in big_w]

    wg_all = _all_gather(_pack_big([w.astype(_MXU) for w in big_w]), "gather_weights")
    cw_all = _all_gather(jnp.pad(conv_w[0], ((0, 4), (0, 64))), "gather_conv")
    cols = lambda r0, nr, k: wg_all[:, r0:r0 + nr].reshape(N_DEV, D, k).transpose(1, 0, 2).reshape(D, N_DEV * k)
    W_in = cols(0, 449, 449)
    W_out = wg_all[:, 449:577].reshape(D, D)
    W_gate = cols(577, 352, 352)
    W_up = cols(929, 352, 352)
    W_down = wg_all[:, 1281:1633].reshape(FF, D)
    conv_full = cw_all[:, 0:CONV_K, 0:192].transpose(1, 0, 2).reshape(CONV_K, 1536)

    qg_t = jnp.tile(attn_q_norm_g, (1, NHA))
    kg_t = jnp.tile(attn_k_norm_g, (1, NHA))
    z4 = jnp.zeros((1, NHD), F32)
    alog8 = jnp.concatenate([z4, a_log], axis=1)
    dtb8 = jnp.concatenate([z4, dt_bias], axis=1)

    araw, an, draw, z, ba, hb = _inproj(xs, norm_mix_g, W_in, qg_t, kg_t)
    tab, tabt = _bias_tables(rel_bias[0].T)
    apre = _attn_fwd(an, tab)
    bat = ba.T
    dn_args = (draw, conv_full, ba, bat, alog8, dtb8, alog8.T, dtb8.T)
    u, w, qd, kd, mc, mct, kdt, qdt, wt, elb = _dn_prep(*dn_args)
    o, vn, sn = _dn_scan(u, w, qd, mc, kdt, elb)
    x1, mix = _post_mix(apre, o, z, xs, W_out, attn_out_norm_g, dn_out_norm_g)

    dx1, dx1b, h2, act, dgu, dyb, loss_row, dgffn = _ffn(x1, tgt, W_gate, W_up, W_down, norm_ffn_g)

    dap, do, dz, dga, dgdn = _mix_bwd(dx1b, W_out, apre, o, z, attn_out_norm_g, dn_out_norm_g)
    dqn, dkn, dv, dtabt = _attn_bwd(an, dap, tabt)
    drel = _bias_grad(dtabt)
    du, dw, dqd, dkd, dgx = _dn_scan_bwd(do, mct, kd, qdt, wt, sn, vn, elb)
    dc, dba, sm = _dn_post_bwd(*dn_args, du, dw, dqd, dkd, dgx, do, vn)
    ddraw, dcw = _conv_bwd(dc, draw, conv_full)
    gx, dproj, dgmix, dqg, dkg = _inproj_bwd(dqn, dkn, dv, araw, ddraw, dz, dba, xs, dx1, W_in, norm_mix_g, qg_t, kg_t)

    gW_in = _wgrad(hb, dproj, "wgrad_in", tk=256)
    gW_out = _wgrad(mix, dx1b, "wgrad_out")
    gW_gu = _wgrad(h2, dgu, "wgrad_gate_up", tn=FF)
    gW_down_t = _wgrad(dyb, act, "wgrad_down")

    by_cols = lambda g, k: g.reshape(D, N_DEV, k).transpose(1, 0, 2).reshape(N_DEV, -1, D)
    send = jnp.concatenate(
        [by_cols(gW_in, 449), gW_out.reshape(N_DEV, 128, D), by_cols(gW_gu[:, 0:FF], 352), by_cols(gW_gu[:, FF:], 352),
         gW_down_t.T.reshape(N_DEV, 352, D), jnp.zeros((N_DEV, RB - 1633, D), F32)], axis=1)
    recv = _all_to_all(send, "scatter_grads")
    big_m = (m_w_in[0], m_w_out[0], m_w_gate[0], m_w_up[0], m_w_down[0])
    big_v = (v_w_in[0], v_w_out[0], v_w_gate[0], v_w_up[0], v_w_down[0])
    bg, bd_, bm, bv = _adamw(recv, _pack_big(big_w), _pack_big(big_m), _pack_big(big_v), "adamw_big", 128)
    bg, bd_, bm, bv = (_unpack_big(t, big_shapes) for t in (bg, bd_, bm, bv))

    partial = jnp.concatenate(
        [dgmix.reshape(8, LANES), dgffn.reshape(8, LANES), dqg.reshape(4, LANES), dkg.reshape(4, LANES),
         dga.reshape(4, LANES), dgdn, sm[0:2], jnp.zeros((1, LANES), F32), loss_row.reshape(8, LANES),
         dcw[0:CONV_K].reshape(48, LANES), drel.reshape(24, LANES)], axis=0)
    S = _small_reduce(_all_gather(partial, "gather_small"))
    loss = S[R_LOSS, 0]
    g_conv = lax.dynamic_slice(S[R_CONV:R_CONV + 48].reshape(CONV_K, 1536), (0, 192 * my_idx), (CONV_K, 192))
    sg = [S[R_GMIX:R_GMIX + 8].reshape(1, D), S[R_QG:R_QG + 1, 0:DHA], S[R_KG:R_KG + 1, 0:DHA],
          S[R_REL:R_REL + 24].reshape(NHA, 384)[:, 0:257].T, S[R_GA:R_GA + 4].reshape(1, AW), g_conv,
          S[R_ALOG:R_ALOG + 1, NHD:2 * NHD], S[R_DT:R_DT + 1, NHD:2 * NHD], S[R_GDN:R_GDN + 1], S[R_GFFN:R_GFFN + 8].reshape(1, D)]
    sw = [norm_mix_g, attn_q_norm_g, attn_k_norm_g, rel_bias[0], attn_out_norm_g, conv_w[0], a_log, dt_bias, dn_out_norm_g, norm_ffn_g]
    smm = [m_norm_mix_g, m_attn_q_norm_g, m_attn_k_norm_g, m_rel_bias[0], m_attn_out_norm_g, m_conv_w[0], m_a_log, m_dt_bias, m_dn_out_norm_g, m_norm_ffn_g]
    svv = [v_norm_mix_g, v_attn_q_norm_g, v_attn_k_norm_g, v_rel_bias[0], v_attn_out_norm_g, v_conv_w[0], v_a_log, v_dt_bias, v_dn_out_norm_g, v_norm_ffn_g]
    s_shapes = [t.shape for t in sw]
    pk = lambda ts: _pack_lanes(ts)
    pg = pk(sg)
    padr = (-pg.shape[0]) % 8
    padz = lambda t: jnp.pad(t, ((0, padr), (0, 0)))
    s_out = _adamw(padz(pg)[None], padz(pk(sw)), padz(pk(smm)), padz(pk(svv)), "adamw_small", pg.shape[0] + padr)
    s_g, s_d, s_m, s_v = (_unpack_lanes(t, s_shapes) for t in s_out)

    lead = lambda t: t[None]
    def ordered(small, big):
        nm, q, k, rel, ao, cw, al, dtb, dno, nf = small
        wi, wo, wgt, wu, wdn = big
        return [nm, lead(wi), q, k, lead(rel), ao, lead(cw), al, dtb, dno, lead(wo), nf, lead(wgt), lead(wu), lead(wdn)]
    outs = [loss, gx[None]]
    for small, big in ((s_g, bg), (s_d, bd_), (s_m, bm), (s_v, bv)):
        outs += ordered(small, big)
    return tuple(outs)
```

```python
import functools
import math

import jax
import jax.numpy as jnp
from jax import lax
from jax.experimental import pallas as pl
from jax.experimental.pallas import tpu as pltpu

F32 = jnp.float32
BF16 = jnp.bfloat16
_MXU = jnp.bfloat16

D = 1024
AW = 512
NHA = 8
DHA = 64
CH = 64
BAND = 9
NHD = 4
DHD = 128
DW = 512
FF = 2816
EPS = 1e-6
NEG = -1e30
N_DEV = 8
LANES = 128
VMEM_LIMIT = 56 * 1024 * 1024

ADAM_LR = 0.001
ADAM_B1 = 0.9
ADAM_B2 = 0.999
ADAM_EPS = 1e-08
ADAM_WD = 0.01
ADAM_STEP = 10

MESH_T = pl.DeviceIdType.MESH


def _cp(sem=None, vmem=VMEM_LIMIT):
    kw = dict(vmem_limit_bytes=vmem)
    if sem is not None:
        kw["dimension_semantics"] = sem
    return pltpu.CompilerParams(**kw)


def _dot(a, b):
    return jnp.dot(a.astype(_MXU), b.astype(_MXU), preferred_element_type=F32)


def _dot_nt(a, b):
    return lax.dot_general(a.astype(_MXU), b.astype(_MXU), (((1,), (1,)), ((), ())), preferred_element_type=F32)


def _dot_tn(a, b):
    return lax.dot_general(a.astype(_MXU), b.astype(_MXU), (((0,), (0,)), ((), ())), preferred_element_type=F32)


def _split2(x):
    hi = x.astype(BF16)
    lo = (x - hi.astype(F32)).astype(BF16)
    return hi, lo


def _dot_x2(x, ones_b):
    hi, lo = _split2(x)
    return jnp.dot(hi, ones_b, preferred_element_type=F32) + jnp.dot(lo, ones_b, preferred_element_type=F32)


def _dot_x2_nt(x, ones_b):
    hi, lo = _split2(x)
    dn = (((1,), (1,)), ((), ()))
    return lax.dot_general(hi, ones_b, dn, preferred_element_type=F32) + lax.dot_general(
        lo, ones_b, dn, preferred_element_type=F32)


def _iota(shape, dim):
    return lax.broadcasted_iota(jnp.int32, shape, dim)


def _block_ones(n, blk, dtype=BF16):
    r, c = _iota((n, n), 0), _iota((n, n), 1)
    return jnp.where((r // blk) == (c // blk), 1.0, 0.0).astype(dtype)


def _sigmoid(x):
    return 1.0 / (1.0 + jnp.exp(-x))


def _softplus(x):
    return jnp.maximum(x, 0.0) + jnp.log(1.0 + jnp.exp(-jnp.abs(x)))


def _col(x, k):
    lane = _iota(x.shape, 1)
    return jnp.sum(jnp.where(lane == k, x, 0.0), axis=1, keepdims=True)


def _row(x, k):
    sub = _iota(x.shape, 0)
    return jnp.sum(jnp.where(sub == k, x, 0.0), axis=0, keepdims=True)


def _my_pos():
    return lax.axis_index("x"), lax.axis_index("y"), lax.axis_index("c")


def _all_gather(x2d, name):
    R, W = x2d.shape

    def body(x_ref, out_ref, send_sems, recv_sems, local_sem):
        ag = _Gather(x_ref, out_ref, send_sems, recv_sems, local_sem)
        ag.start()
        ag.forward()
        ag.finish()

    return pl.pallas_call(
        body, name=name,
        out_shape=jax.ShapeDtypeStruct((N_DEV, R, W), x2d.dtype),
        in_specs=[pl.BlockSpec(memory_space=pl.ANY)],
        out_specs=pl.BlockSpec(memory_space=pl.ANY),
        scratch_shapes=_COMM_SEMS,
    )(x2d)


_COMM_SEMS = [pltpu.SemaphoreType.DMA((7,)), pltpu.SemaphoreType.DMA((7,)), pltpu.SemaphoreType.DMA]


class _Gather:
    def __init__(self, x_ref, out_ref, send_sems, recv_sems, local_sem):
        x, y, c = _my_pos()
        me, sibling = (x, y, c), (x, y, 1 - c)
        chips = [(1 - x, y), (x, 1 - y), (1 - x, 1 - y)]

        def slot(px, py, pc):
            return out_ref.at[4 * px + 2 * py + pc]

        def copy(k, block, to, src=None):
            return pltpu.make_async_remote_copy(
                src_ref=slot(*block) if src is None else src, dst_ref=slot(*block),
                send_sem=send_sems.at[k], recv_sem=recv_sems.at[k], device_id=to, device_id_type=MESH_T)

        self.mine = pltpu.make_async_copy(x_ref, slot(*me), local_sem)
        self.first = [copy(0, me, sibling, src=x_ref)]
        self.first += [copy(1 + j, me, (*chip, c), src=x_ref) for j, chip in enumerate(chips)]
        self.passed = [copy(4 + j, (*chip, c), sibling) for j, chip in enumerate(chips)]
        self.from_chips = [copy(1 + j, (*chip, c), me) for j, chip in enumerate(chips)]
        self.from_sibling = [copy(0, sibling, me)] + [copy(4 + j, (*chip, 1 - c), me) for j, chip in enumerate(chips)]

    def start(self):
        self.mine.start()
        for cp in self.first:
            cp.start()

    def forward(self):
        for arrived, onward in zip(self.from_chips, self.passed):
            arrived.wait_recv()
            onward.start()

    def finish(self):
        for cp in self.from_sibling:
            cp.wait_recv()
        for cp in self.first + self.passed:
            cp.wait_send()
        self.mine.wait()


class _Scatter:
    def __init__(self, s_ref, r_ref, send_sems, recv_sems, local_sem):
        x, y, c = _my_pos()
        self.mine = pltpu.make_async_copy(s_ref.at[4 * x + 2 * y + c], r_ref.at[0], local_sem)
        self.copies = []
        for m in range(1, N_DEV):
            px = x ^ ((m >> 2) & 1)
            py = y ^ ((m >> 1) & 1)
            pc = c ^ (m & 1)
            self.copies.append(pltpu.make_async_remote_copy(
                src_ref=s_ref.at[4 * px + 2 * py + pc], dst_ref=r_ref.at[m],
                send_sem=send_sems.at[m - 1], recv_sem=recv_sems.at[m - 1],
                device_id=(px, py, pc), device_id_type=MESH_T))

    def start(self):
        self.mine.start()
        for cp in self.copies:
            cp.start()

    def finish(self):
        for cp in self.copies:
            cp.wait_recv()
        for cp in self.copies:
            cp.wait_send()
        self.mine.wait()


def _all_to_all(send, name):
    def body(s_ref, r_ref, send_sems, recv_sems, local_sem):
        sc = _Scatter(s_ref, r_ref, send_sems, recv_sems, local_sem)
        sc.start()
        sc.finish()

    return pl.pallas_call(
        body, name=name,
        out_shape=jax.ShapeDtypeStruct(send.shape, send.dtype),
        in_specs=[pl.BlockSpec(memory_space=pl.ANY)],
        out_specs=pl.BlockSpec(memory_space=pl.ANY),
        scratch_shapes=_COMM_SEMS,
    )(send)


TM = 256
TG = 512


def _full(shape):
    nd = len(shape)
    return pl.BlockSpec(shape, lambda i: (0,) * nd)


def _rows(tm, w):
    return pl.BlockSpec((tm, w), lambda i: (i, 0))


def _head_rms(x, bd, width):
    ss = _dot_x2(x * x, bd)
    return lax.rsqrt(ss * (1.0 / width) + EPS)


def _inproj(x, g_mix, w_in, qg_t, kg_t, later_w):
    T = x.shape[0]
    nt = T // TM

    def body(x_ref, g_ref, w_ref, qg_ref, kg_ref, lw_ref, araw_ref, an_ref, draw_ref, z_ref, ba_ref, h_ref, lw_all,
             send_sems, recv_sems, local_sem):
        i = pl.program_id(0)
        ag = _Gather(lw_ref, lw_all, send_sems, recv_sems, local_sem)
        pl.when(i == 0)(ag.start)
        pl.when(i == nt // 2)(ag.forward)
        xv = x_ref[...]
        r = lax.rsqrt(jnp.mean(xv * xv, axis=1, keepdims=True) + EPS)
        h = (xv * r * g_ref[...]).astype(_MXU)
        h_ref[...] = h
        a = jnp.dot(h, w_ref[:, 0:1536], preferred_element_type=F32)
        araw_ref[...] = a
        bd = _block_ones(AW, DHA)
        q = a[:, 0:AW]
        k = a[:, AW:2 * AW]
        qn = q * _head_rms(q, bd, DHA) * (qg_ref[...] * (DHA ** -0.5))
        kn = k * _head_rms(k, bd, DHA) * kg_ref[...]
        an_ref[:, 0:AW] = qn.astype(_MXU)
        an_ref[:, AW:2 * AW] = kn.astype(_MXU)
        an_ref[:, 2 * AW:3 * AW] = a[:, 2 * AW:3 * AW].astype(_MXU)
        draw_ref[...] = jnp.dot(h, w_ref[:, 1536:3072], preferred_element_type=F32)
        z_ref[...] = jnp.dot(h, w_ref[:, 3072:3584], preferred_element_type=F32)
        ba_ref[...] = jnp.dot(h, w_ref[:, 3584:3592], preferred_element_type=F32)
        pl.when(i == nt - 1)(ag.finish)

    anyspec = pl.BlockSpec(memory_space=pl.ANY)
    return pl.pallas_call(
        body, name="inproj", grid=(nt,),
        in_specs=[_rows(TM, D), _full((1, D)), _full((D, 3592)), _full((1, AW)), _full((1, AW)), anyspec],
        out_specs=[_rows(TM, 1536), _rows(TM, 1536), _rows(TM, 1536), _rows(TM, DW), _rows(TM, 8), _rows(TM, D),
                   anyspec],
        out_shape=[jax.ShapeDtypeStruct((T, 1536), F32), jax.ShapeDtypeStruct((T, 1536), _MXU),
                   jax.ShapeDtypeStruct((T, 1536), F32), jax.ShapeDtypeStruct((T, DW), F32),
                   jax.ShapeDtypeStruct((T, 8), F32), jax.ShapeDtypeStruct((T, D), _MXU),
                   jax.ShapeDtypeStruct((N_DEV,) + later_w.shape, later_w.dtype)],
        scratch_shapes=_COMM_SEMS,
        compiler_params=_cp(("arbitrary",)),
    )(x, g_mix, w_in, qg_t, kg_t, later_w)


TQ = 256
TW = 768
T_LO, T_HI = 65, 256
VAR0 = 384
TOEP = 1024


def _bias_tables(rb_t):
    def body(rb_ref, tab_ref, tabt_ref):
        h = pl.program_id(0)
        n = _iota((8, TOEP), 1)

        def line(m):
            idx = jnp.clip(512 - m, -128, 128) + 128
            return lax.fori_loop(T_LO, T_HI, lambda t, a: jnp.where(idx == t, rb_ref[h, t], a),
                                 jnp.full((8, TOEP), rb_ref[h, T_HI], F32))[0:1, :]

        def band(r, j):
            return ((j >> 6) >= (r >> 6)) & ((j >> 6) <= (r >> 6) + 8)

        g = line(jnp.where(n < TW, n, n - TOEP))
        tab = pltpu.roll(jnp.broadcast_to(g, (TQ, TOEP)), 0, 1, stride=1, stride_axis=0)[:, 0:TW]
        tab_ref[0] = jnp.where(band(_iota((TQ, TW), 0), _iota((TQ, TW), 1)), tab, NEG)
        gt = line(jnp.where(n < TQ, -n, TOEP - n))
        tabt = pltpu.roll(jnp.broadcast_to(gt, (TW, TOEP)), 0, 1, stride=1, stride_axis=0)[:, 0:TQ]
        tabt_ref[0] = jnp.where(band(_iota((TW, TQ), 1), _iota((TW, TQ), 0)), tabt, NEG)

    return pl.pallas_call(
        body, name="bias_tables", grid=(NHA,),
        in_specs=[pl.BlockSpec(memory_space=pltpu.SMEM)],
        out_specs=[pl.BlockSpec((1, TQ, TW), lambda h: (h, 0, 0)), pl.BlockSpec((1, TW, TQ), lambda h: (h, 0, 0))],
        out_shape=[jax.ShapeDtypeStruct((NHA, TQ, TW), F32), jax.ShapeDtypeStruct((NHA, TW, TQ), F32)],
        compiler_params=_cp(("arbitrary",)),
    )(rb_t)


def _bias_grad(dtabt):
    def body(d_ref, o_ref):
        a, b = _iota((TQ, TQ), 0), _iota((TQ, TQ), 1)
        anti = jnp.where(a + b == TQ - 1, 1.0, 0.0).astype(BF16)
        drev = sum(jnp.dot(t, anti, preferred_element_type=F32) for t in _split3(d_ref[0]))
        wide = jnp.concatenate([drev, jnp.zeros((TW, TOEP - TQ), F32)], axis=1)
        cols = jnp.sum(pltpu.roll(wide, 0, 1, stride=1, stride_axis=0), axis=0, keepdims=True)
        c = _iota((1, TOEP), 1)
        idx = jnp.clip(512 + TQ - 1 - c, -128, 128) + 128
        lane = _iota((1, VAR0), 1)

        def step(t, row):
            part = jnp.sum(jnp.where(idx == t, cols, 0.0), axis=1, keepdims=True)
            return jnp.where(lane == t, part, row)

        o_ref[0] = lax.fori_loop(T_LO, T_HI + 1, step, jnp.zeros((1, VAR0), F32))

    return pl.pallas_call(
        body, name="bias_grad", grid=(NHA,),
        in_specs=[pl.BlockSpec((1, TW, TQ), lambda h: (h, 0, 0))],
        out_specs=pl.BlockSpec((1, 1, VAR0), lambda h: (h, 0, 0)),
        out_shape=jax.ShapeDtypeStruct((NHA, 1, VAR0), F32),
        compiler_params=_cp(("arbitrary",)),
    )(dtabt)


def _kv_spec(col, back):
    return pl.BlockSpec((TQ, AW), lambda i: (jnp.maximum(i - back, 0), col))


def _attn_fwd(an, tab):
    T = an.shape[0]

    def body(q_ref, k2_ref, k1_ref, k0_ref, v2_ref, v1_ref, v0_ref, tab_ref, o_ref):
        i = pl.program_id(0)
        kwin = jnp.concatenate([k2_ref[...], k1_ref[...], k0_ref[...]], axis=0)
        vwin = jnp.concatenate([v2_ref[...], v1_ref[...], v0_ref[...]], axis=0)
        q = q_ref[...]
        invalid = (_iota((TQ, TW), 1) + TQ * i) < 512
        lo_half = _iota((TQ, LANES), 1) < DHA
        for p in range(NHA // 2):
            sl = slice(LANES * p, LANES * (p + 1))
            qp, kp, vp = q[:, sl], kwin[:, sl], vwin[:, sl]
            outs = []
            for half in range(2):
                mask = lo_half if half == 0 else jnp.logical_not(lo_half)
                qm = jnp.where(mask, qp, jnp.zeros_like(qp))
                s = _dot_nt(qm, kp) + tab_ref[2 * p + half]
                s = jnp.where(invalid, NEG, s)
                m = jnp.max(s, axis=1, keepdims=True)
                e = jnp.exp(s - m)
                l = jnp.sum(e, axis=1, keepdims=True)
                outs.append(_dot(e, vp) / l)
            o_ref[:, sl] = jnp.where(lo_half, outs[0], outs[1])

    return pl.pallas_call(
        body, name="attn_fwd", grid=(T // TQ,),
        in_specs=[pl.BlockSpec((TQ, AW), lambda i: (i, 0)),
                  _kv_spec(1, 2), _kv_spec(1, 1), _kv_spec(1, 0), _kv_spec(2, 2), _kv_spec(2, 1), _kv_spec(2, 0),
                  _full((NHA, TQ, TW))],
        out_specs=_rows(TQ, AW),
        out_shape=jax.ShapeDtypeStruct((T, AW), F32),
        compiler_params=_cp(("arbitrary",)),
    )(an, an, an, an, an, an, an, tab)


def _attn_bwd(an, dout, tabt, send):
    T = an.shape[0]
    nq = T // TQ

    def qi(i):
        return jnp.minimum(i, nq - 1)

    def kv_spec(col, back):
        return pl.BlockSpec((TQ, AW), lambda i: (jnp.maximum(qi(i) - back, 0), col))

    def body(q_ref, do_ref, k2_ref, k1_ref, k0_ref, v2_ref, v1_ref, v0_ref, tabt_ref, send_ref,
             dq_ref, dk_ref, dv_ref, dtab_ref, recv_ref, dk_acc, dv_acc, send_sems, recv_sems, local_sem):
        i = pl.program_id(0)
        sc = _Scatter(send_ref, recv_ref, send_sems, recv_sems, local_sem)
        pl.when(i == 0)(sc.start)

        @pl.when(i == 0)
        def _():
            dtab_ref[...] = jnp.zeros_like(dtab_ref)

        new = i % 3
        dk_acc[new] = jnp.zeros((TQ, AW), F32)
        dv_acc[new] = jnp.zeros((TQ, AW), F32)

        @pl.when(i < nq)
        def _():
            kwin = jnp.concatenate([k2_ref[...], k1_ref[...], k0_ref[...]], axis=0)
            vwin = jnp.concatenate([v2_ref[...], v1_ref[...], v0_ref[...]], axis=0)
            q = q_ref[...]
            do = do_ref[...].astype(_MXU)
            invalid = (_iota((TW, TQ), 0) + TQ * i) < 512
            lo_half = _iota((TQ, LANES), 1) < DHA
            for p in range(NHA // 2):
                sl = slice(LANES * p, LANES * (p + 1))
                qp, kp, vp, dop = q[:, sl], kwin[:, sl], vwin[:, sl], do[:, sl]
                dq_pair = jnp.zeros((TQ, LANES), F32)
                dk_pair = jnp.zeros((TW, LANES), F32)
                dv_pair = jnp.zeros((TW, LANES), F32)
                for half in range(2):
                    h = 2 * p + half
                    mask = lo_half if half == 0 else jnp.logical_not(lo_half)
                    qm = jnp.where(mask, qp, jnp.zeros_like(qp))
                    dom = jnp.where(mask, dop, jnp.zeros_like(dop))
                    st = _dot_nt(kp, qm) + tabt_ref[h]
                    st = jnp.where(invalid, NEG, st)
                    m = jnp.max(st, axis=0, keepdims=True)
                    e = jnp.exp(st - m)
                    pt = e * (1.0 / jnp.sum(e, axis=0, keepdims=True))
                    dpt = _dot_nt(vp, dom)
                    delta = jnp.sum(pt * dpt, axis=0, keepdims=True)
                    dst = pt * (dpt - delta)
                    dtab_ref[h] += dst
                    dsb = dst.astype(_MXU)
                    dv_pair += _dot(pt, dom)
                    dk_pair += _dot(dsb, qm)
                    dq_pair += jnp.where(mask, _dot_tn(dsb, kp), 0.0)
                dq_ref[:, sl] = dq_pair
                for w in range(3):
                    slot = (i + 1 + w) % 3
                    rows = slice(TQ * w, TQ * (w + 1))
                    dk_acc[slot, :, sl] += dk_pair[rows]
                    dv_acc[slot, :, sl] += dv_pair[rows]

        @pl.when(i >= 2)
        def _():
            done = (i + 1) % 3
            dk_ref[...] = dk_acc[done]
            dv_ref[...] = dv_acc[done]

        pl.when(i == nq + 1)(sc.finish)

    back2 = pl.BlockSpec((TQ, AW), lambda i: (jnp.maximum(i - 2, 0), 0))
    anyspec = pl.BlockSpec(memory_space=pl.ANY)
    return pl.pallas_call(
        body, name="attn_bwd", grid=(nq + 2,),
        in_specs=[pl.BlockSpec((TQ, AW), lambda i: (qi(i), 0)), pl.BlockSpec((TQ, AW), lambda i: (qi(i), 0)),
                  kv_spec(1, 2), kv_spec(1, 1), kv_spec(1, 0), kv_spec(2, 2), kv_spec(2, 1), kv_spec(2, 0),
                  _full((NHA, TW, TQ)), anyspec],
        out_specs=[pl.BlockSpec((TQ, AW), lambda i: (qi(i), 0)), back2, back2, _full((NHA, TW, TQ)), anyspec],
        out_shape=[jax.ShapeDtypeStruct((T, AW), F32), jax.ShapeDtypeStruct((T, AW), F32),
                   jax.ShapeDtypeStruct((T, AW), F32), jax.ShapeDtypeStruct((NHA, TW, TQ), F32),
                   jax.ShapeDtypeStruct(send.shape, send.dtype)],
        scratch_shapes=[pltpu.VMEM((3, TQ, AW), F32), pltpu.VMEM((3, TQ, AW), F32)] + _COMM_SEMS,
        compiler_params=_cp(("arbitrary",)),
    )(an, dout, an, an, an, an, an, an, tabt, send)


GR = 256
NG = TG // GR
CPT = TG // CH
CONV_K = 4


def _split3(x):
    a = x.astype(BF16)
    r = x - a.astype(F32)
    b = r.astype(BF16)
    c = (r - b.astype(F32)).astype(BF16)
    return a, b, c


def _ones_dot(ones_b, x):
    return sum(jnp.dot(ones_b, t, preferred_element_type=F32) for t in _split3(x))


def _dot_ones_nt(x, ones_b):
    dn = (((1,), (1,)), ((), ()))
    return sum(lax.dot_general(t, ones_b, dn, preferred_element_type=F32) for t in _split3(x))


def _dn_masks():
    r, c = _iota((GR, GR), 0), _iota((GR, GR), 1)
    same = (r >> 6) == (c >> 6)
    one = lambda m: jnp.where(m, 1.0, 0.0).astype(BF16)
    return dict(
        tril=same & (c <= r), strict=same & (c < r), triu=same & (c >= r), strict_u=same & (c > r),
        tril_b=one(same & (c <= r)), triu_b=one(same & (c >= r)), blk_b=one(same), eye_b=one(r == c),
        eye=jnp.where(r == c, 1.0, 0.0).astype(F32),
        fold_b=one((_iota((GR, CH), 0) & (CH - 1)) == _iota((GR, CH), 1)),
        last=(_iota((GR, 1), 0) & (CH - 1)) == CH - 1,
    )


def _shift_down(x, halo, k):
    if k == 0:
        return x
    xs = pltpu.roll(x, k, 0)
    hs = pltpu.roll(halo, k, 0)
    top = jnp.where(_iota(halo.shape, 0) < k, hs, xs[0:8])
    return jnp.concatenate([top, xs[8:]], axis=0)


def _shift_up(x, halo, k):
    if k == 0:
        return x
    n = x.shape[0]
    xs = pltpu.roll(x, n - k, 0)
    hs = pltpu.roll(halo, 8 - k, 0)
    bot = jnp.where(_iota(halo.shape, 0) >= 8 - k, hs, xs[n - 8:n])
    return jnp.concatenate([xs[0:n - 8], bot], axis=0)


def _conv(x, halo, w):
    y = x * w[CONV_K - 1:CONV_K, :]
    for k in range(1, CONV_K):
        y = y + _shift_down(x, halo, k) * w[CONV_K - 1 - k:CONV_K - k, :]
    return y


def _tri_inv(lmat, eye):
    p = -lmat
    r = eye + p
    for _ in range(5):
        p = _dot(p, p)
        r = r + _dot(r, p)
    return r


def _gate_terms(ba_g, bat_g, alog8, dtb8, alog8t, dtb8t, K):
    g8 = -jnp.exp(alog8) * _softplus(ba_g + dtb8)
    g8t = -jnp.exp(alog8t) * _softplus(bat_g + dtb8t)
    gc8 = _ones_dot(K["tril_b"], g8)
    gl8 = _ones_dot(K["blk_b"], g8)
    gcrow8 = _dot_ones_nt(g8t, K["tril_b"])
    return g8, gc8, gl8, gcrow8


def _dn_head(c_tile, rows, h, beta8, gc8, gl8, gcrow8, K):
    qr = c_tile[rows, DHD * h:DHD * (h + 1)]
    kr = c_tile[rows, DW + DHD * h:DW + DHD * (h + 1)]
    v = c_tile[rows, 2 * DW + DHD * h:2 * DW + DHD * (h + 1)]
    rq = lax.rsqrt(jnp.sum(qr * qr, axis=1, keepdims=True) + EPS)
    rk = lax.rsqrt(jnp.sum(kr * kr, axis=1, keepdims=True) + EPS)
    qh, kn = qr * rq, kr * rk
    qn = qh * (DHD ** -0.5)
    beta = _col(beta8, h)
    gccol, glcol, gcrow = _col(gc8, NHD + h), _col(gl8, NHD + h), _row(gcrow8, NHD + h)
    diff = gccol - gcrow
    gam_m = jnp.exp(jnp.where(K["tril"], diff, NEG))
    gam = jnp.exp(gccol)
    egl = jnp.exp(glcol - gccol)
    kb, vb = kn * beta, v * beta
    kg = kb * gam
    pl_ = _dot_nt(kb, kn)
    lmat = jnp.where(K["strict"], pl_ * gam_m, 0.0)
    tm = _tri_inv(lmat, K["eye"])
    u = _dot(tm, vb)
    w = _dot(tm, kg)
    pm = _dot_nt(qn, kn)
    mm = pm * gam_m
    return dict(qr=qr, kr=kr, v=v, rq=rq, rk=rk, qh=qh, qn=qn, kn=kn, beta=beta, diff=diff, gam_m=gam_m, gam=gam,
                egl=egl, el=jnp.exp(glcol), kb=kb, vb=vb, kg=kg, pl=pl_, pm=pm, tm=tm, u=u, w=w, mm=mm,
                qd=qn * gam, kd=kn * egl)


def _halo_prev(width):
    return pl.BlockSpec((8, width), lambda i: (jnp.maximum(i * (TG // 8) - 1, 0), 0))


def _dn_prep(draw, conv_w, ba, bat, alog8, dtb8, alog8t, dtb8t):
    T = draw.shape[0]
    nb = T // TG
    hm = lambda w, dt: jax.ShapeDtypeStruct((NHD, T, w), dt)
    hm_spec = lambda w: pl.BlockSpec((NHD, TG, w), lambda i: (0, i, 0))
    tr = jax.ShapeDtypeStruct((NHD, T // CH, DHD, CH), _MXU)
    tr_spec = pl.BlockSpec((NHD, CPT, DHD, CH), lambda i: (0, i, 0, 0))

    def body(x_ref, halo_ref, cw_ref, ba_ref, bat_ref, al_ref, dt_ref, alt_ref, dtt_ref,
             u_ref, w_ref, qd_ref, kd_ref, mc_ref, mct_ref, kdt_ref, qdt_ref, wt_ref, elb_ref):
        i = pl.program_id(0)
        K = _dn_masks()
        halo = jnp.where(i > 0, halo_ref[...], 0.0)
        cv = _conv(x_ref[...], halo, cw_ref[...])
        c_tile = cv * _sigmoid(cv)
        eye128 = jnp.where(_iota((DHD, DHD), 0) == _iota((DHD, DHD), 1), 1.0, 0.0).astype(_MXU)
        for g in range(NG):
            rows = slice(GR * g, GR * (g + 1))
            ba_g = ba_ref[rows, :]
            _, gc8, gl8, gcrow8 = _gate_terms(ba_g, bat_ref[:, rows], al_ref[...], dt_ref[...], alt_ref[...],
                                              dtt_ref[...], K)
            beta8 = _sigmoid(ba_g)
            for h in range(NHD):
                d = _dn_head(c_tile, rows, h, beta8, gc8, gl8, gcrow8, K)
                gam_t = jnp.exp(jnp.where(K["triu"], -d["diff"], NEG))
                mmt = _dot_nt(d["kn"], d["qn"]) * gam_t
                u_ref[h, rows, :] = d["u"]
                w_ref[h, rows, :] = d["w"].astype(_MXU)
                qd_ref[h, rows, :] = d["qd"].astype(_MXU)
                kd_ref[h, rows, :] = d["kd"].astype(_MXU)
                mc_ref[h, rows, :] = _dot(d["mm"], K["fold_b"]).astype(_MXU)
                mct_ref[h, rows, :] = _dot(mmt, K["fold_b"]).astype(_MXU)
                elb = jnp.broadcast_to(d["el"], (GR, DHD))
                for cc in range(GR // CH):
                    ch = slice(CH * cc, CH * (cc + 1))
                    n = (GR // CH) * g + cc
                    kdt_ref[h, n] = _dot_nt(eye128, d["kd"][ch]).astype(_MXU)
                    qdt_ref[h, n] = _dot_nt(eye128, d["qd"][ch]).astype(_MXU)
                    wt_ref[h, n] = _dot_nt(eye128, d["w"][ch]).astype(_MXU)
                    elb_ref[n:n + 1, DHD * h:DHD * (h + 1)] = elb[CH * cc:CH * cc + 1, :]

    return pl.pallas_call(
        body, name="dn_prep", grid=(nb,),
        in_specs=[_rows(TG, 1536), _halo_prev(1536), _full((CONV_K, 1536)), _rows(TG, 8),
                  pl.BlockSpec((8, TG), lambda i: (0, i)), _full((1, 8)), _full((1, 8)), _full((8, 1)), _full((8, 1))],
        out_specs=[hm_spec(DHD), hm_spec(DHD), hm_spec(DHD), hm_spec(DHD), hm_spec(CH), hm_spec(CH),
                   tr_spec, tr_spec, tr_spec, pl.BlockSpec((CPT, NHD * DHD), lambda i: (i, 0))],
        out_shape=[hm(DHD, F32), hm(DHD, _MXU), hm(DHD, _MXU), hm(DHD, _MXU), hm(CH, _MXU), hm(CH, _MXU),
                   tr, tr, tr, jax.ShapeDtypeStruct((T // CH, NHD * DHD), F32)],
        compiler_params=_cp(("arbitrary",)),
    )(draw, draw, conv_w, ba, bat, alog8, dtb8, alog8t, dtb8t)


def _dn_scan(u, w, qd, mc, kdt, elb):
    T = u.shape[1]
    nb = T // TG
    hm_spec = lambda wd: pl.BlockSpec((NHD, TG, wd), lambda i: (0, i, 0))

    def body(u_ref, w_ref, qd_ref, mc_ref, kdt_ref, elb_ref, o_ref, vn_ref, sn_ref, S):
        @pl.when(pl.program_id(0) == 0)
        def _():
            S[...] = jnp.zeros_like(S)

        sub8 = _iota((CPT, DHD), 0)

        def chunk(cc, carry):
            r0 = pl.multiple_of(cc * CH, CH)
            rs = pl.ds(r0, CH)
            for h in range(NHD):
                sh = S[h]
                sb = sh.astype(_MXU)
                sn_ref[cc, h] = sb
                vn = u_ref[h, rs, :] - _dot(w_ref[h, rs, :], sb)
                vnb = vn.astype(_MXU)
                o_ref[h, rs, :] = _dot(qd_ref[h, rs, :], sb) + _dot(mc_ref[h, rs, :], vnb)
                vn_ref[h, rs, :] = vnb
                el = jnp.sum(jnp.where(sub8 == cc, elb_ref[:, DHD * h:DHD * (h + 1)], 0.0), axis=0, keepdims=True)
                S[h] = sh * el + _dot(kdt_ref[h, cc], vnb)
            return carry

        lax.fori_loop(0, CPT, chunk, 0)

    return pl.pallas_call(
        body, name="dn_scan", grid=(nb,),
        in_specs=[hm_spec(DHD), hm_spec(DHD), hm_spec(DHD), hm_spec(CH),
                  pl.BlockSpec((NHD, CPT, DHD, CH), lambda i: (0, i, 0, 0)),
                  pl.BlockSpec((CPT, NHD * DHD), lambda i: (i, 0))],
        out_specs=[hm_spec(DHD), hm_spec(DHD), pl.BlockSpec((CPT, NHD, DHD, DHD), lambda i: (i, 0, 0, 0))],
        out_shape=[jax.ShapeDtypeStruct((NHD, T, DHD), F32), jax.ShapeDtypeStruct((NHD, T, DHD), _MXU),
                   jax.ShapeDtypeStruct((T // CH, NHD, DHD, DHD), _MXU)],
        scratch_shapes=[pltpu.VMEM((NHD, DHD, DHD), F32)],
        compiler_params=_cp(("arbitrary",)),
    )(u, w, qd, mc, kdt, elb)


def _dn_scan_bwd(do, mct, kd, qdt, wt, sn, vn, elb):
    T = do.shape[1]
    nb = T // TG
    rev = lambda wd: pl.BlockSpec((NHD, TG, wd), lambda i: (0, nb - 1 - i, 0))
    rev_t = pl.BlockSpec((NHD, CPT, DHD, CH), lambda i: (0, nb - 1 - i, 0, 0))

    def body(do_ref, mct_ref, kd_ref, qdt_ref, wt_ref, sn_ref, vn_ref, elb_ref,
             du_ref, dw_ref, dqd_ref, dkd_ref, dgx_ref, dS):
        @pl.when(pl.program_id(0) == 0)
        def _():
            dS[...] = jnp.zeros_like(dS)

        last_row = _iota((CH, DHD), 0) == CH - 1
        sub8 = _iota((CPT, DHD), 0)

        def chunk(k, carry):
            cc = CPT - 1 - k
            r0 = pl.multiple_of(cc * CH, CH)
            rs = pl.ds(r0, CH)
            for h in range(NHD):
                dsh = dS[h]
                dsb = dsh.astype(_MXU)
                sn = sn_ref[cc, h]
                doc = do_ref[h, rs, :].astype(_MXU)
                dvn = _dot(mct_ref[h, rs, :], doc) + _dot(kd_ref[h, rs, :], dsb)
                dvnb = dvn.astype(_MXU)
                du_ref[h, rs, :] = dvn
                dqd_ref[h, rs, :] = _dot_nt(doc, sn)
                dkd_ref[h, rs, :] = _dot_nt(vn_ref[h, rs, :], dsb)
                dw_ref[h, rs, :] = -_dot_nt(dvnb, sn)
                el = jnp.sum(jnp.where(sub8 == cc, elb_ref[:, DHD * h:DHD * (h + 1)], 0.0), axis=0, keepdims=True)
                part = jnp.sum(dsh * sn.astype(F32), axis=0, keepdims=True) * el
                dgx_ref[h, rs, :] = jnp.where(last_row, part, 0.0)
                dS[h] = _dot(qdt_ref[h, cc], doc) + dsh * el - _dot(wt_ref[h, cc], dvnb)
            return carry

        lax.fori_loop(0, CPT, chunk, 0)

    o = jax.ShapeDtypeStruct((NHD, T, DHD), F32)
    return pl.pallas_call(
        body, name="dn_scan_bwd", grid=(nb,),
        in_specs=[rev(DHD), rev(CH), rev(DHD), rev_t, rev_t,
                  pl.BlockSpec((CPT, NHD, DHD, DHD), lambda i: (nb - 1 - i, 0, 0, 0)), rev(DHD),
                  pl.BlockSpec((CPT, NHD * DHD), lambda i: (nb - 1 - i, 0))],
        out_specs=[rev(DHD)] * 5,
        out_shape=[o] * 5,
        scratch_shapes=[pltpu.VMEM((NHD, DHD, DHD), F32)],
        compiler_params=_cp(("arbitrary",)),
    )(do, mct, kd, qdt, wt, sn, vn, elb)


def _put_col(acc, k, col):
    return jnp.where(_iota(acc.shape, 1) == k, col, acc)


def _dn_post_bwd(draw, conv_w, ba, bat, alog8, dtb8, alog8t, dtb8t, du, dw, dqd, dkd, dgx, do, vn):
    T = draw.shape[0]
    nb = T // TG
    hm_spec = lambda wd: pl.BlockSpec((NHD, TG, wd), lambda i: (0, i, 0))

    def body(x_ref, halo_ref, cw_ref, ba_ref, bat_ref, al_ref, dt_ref, alt_ref, dtt_ref,
             du_ref, dw_ref, dqd_ref, dkd_ref, dgx_ref, do_ref, vn_ref, dc_ref, dba_ref, sm_ref):
        i = pl.program_id(0)

        @pl.when(i == 0)
        def _():
            sm_ref[...] = jnp.zeros_like(sm_ref)

        K = _dn_masks()
        halo = jnp.where(i > 0, halo_ref[...], 0.0)
        cv = _conv(x_ref[...], halo, cw_ref[...])
        sg = _sigmoid(cv)
        c_tile = cv * sg
        dsilu = sg * (1.0 + cv * (1.0 - sg))
        for g in range(NG):
            rows = slice(GR * g, GR * (g + 1))
            ba_g = ba_ref[rows, :]
            g8, gc8, gl8, gcrow8 = _gate_terms(ba_g, bat_ref[:, rows], al_ref[...], dt_ref[...], alt_ref[...],
                                               dtt_ref[...], K)
            beta8 = _sigmoid(ba_g)
            dgc8 = jnp.zeros((GR, 8), F32)
            rd8 = jnp.zeros((GR, 8), F32)
            dbeta8 = jnp.zeros((GR, 8), F32)
            for h in range(NHD):
                d = _dn_head(c_tile, rows, h, beta8, gc8, gl8, gcrow8, K)
                gam_m, gam, egl = d["gam_m"], d["gam"], d["egl"]
                gam_t = jnp.exp(jnp.where(K["triu"], -d["diff"], NEG))
                qn, kn, kb, vv = d["qn"], d["kn"], d["kb"], d["v"]
                duh, dwh = du_ref[h, rows, :], dw_ref[h, rows, :]
                dqdh, dkdh = dqd_ref[h, rows, :], dkd_ref[h, rows, :]
                doh, vnh = do_ref[h, rows, :], vn_ref[h, rows, :]
                tt = _dot_nt(K["eye_b"].astype(_MXU), d["tm"])
                dvb = _dot(tt, duh)
                dkg = _dot(tt, dwh)
                da = -(_dot_nt(dvb, d["u"]) + _dot_nt(dkg, d["w"]))
                dat = -(_dot_nt(d["u"], dvb) + _dot_nt(d["w"], dkg))
                dpl = jnp.where(K["strict"], da, 0.0) * gam_m
                dplt = jnp.where(K["strict_u"], dat, 0.0) * gam_t
                dpm = jnp.where(K["tril"], _dot_nt(doh, vnh), 0.0) * gam_m
                dpmt = jnp.where(K["triu"], _dot_nt(vnh, doh), 0.0) * gam_t
                plt = _dot_nt(kn, kb)
                pmt = _dot_nt(kn, qn)
                dkb = _dot(dpl, kn) + dkg * gam
                dkn = _dot(dplt, kb) + _dot(dpmt, qn) + dkdh * egl + dkb * d["beta"]
                dqn = _dot(dpm, kn) + dqdh * gam
                gsum = jnp.sum(dpl * d["pl"] + dpm * d["pm"], axis=1, keepdims=True)
                gsum_t = jnp.sum(dplt * plt + dpmt * pmt, axis=1, keepdims=True)
                rd = jnp.sum(dkdh * d["kd"], axis=1, keepdims=True)
                dgc = (gsum - gsum_t + jnp.sum(dqdh * d["qd"], axis=1, keepdims=True)
                       + jnp.sum(dkg * d["kg"], axis=1, keepdims=True) - rd
                       + jnp.sum(dgx_ref[h, rows, :], axis=1, keepdims=True))
                dgc8 = _put_col(dgc8, NHD + h, dgc)
                rd8 = _put_col(rd8, NHD + h, rd)
                dbeta = jnp.sum(dkb * kn, axis=1, keepdims=True) + jnp.sum(dvb * vv, axis=1, keepdims=True)
                dbeta8 = _put_col(dbeta8, h, dbeta)
                dqh = dqn * (DHD ** -0.5)
                qh = d["qh"]
                dqr = d["rq"] * (dqh - qh * jnp.sum(dqh * qh, axis=1, keepdims=True))
                dkr = d["rk"] * (dkn - kn * jnp.sum(dkn * kn, axis=1, keepdims=True))
                cq = slice(DHD * h, DHD * (h + 1))
                ck = slice(DW + DHD * h, DW + DHD * (h + 1))
                cvv = slice(2 * DW + DHD * h, 2 * DW + DHD * (h + 1))
                dc_ref[rows, cq] = dqr * dsilu[rows, cq]
                dc_ref[rows, ck] = dkr * dsilu[rows, ck]
                dc_ref[rows, cvv] = dvb * d["beta"] * dsilu[rows, cvv]
            dgc8 = dgc8 + jnp.where(K["last"], _ones_dot(K["blk_b"], rd8), 0.0)
            dg8 = _ones_dot(K["triu_b"], dgc8)
            sgm = _sigmoid(ba_g + dt_ref[...])
            dalpha = dg8 * (-jnp.exp(al_ref[...])) * sgm
            lane8 = _iota((GR, 8), 1)
            dba_ref[rows, :] = jnp.where(lane8 < NHD, dbeta8 * beta8 * (1.0 - beta8), dalpha)
            valid = lane8 >= NHD
            sm_ref[0:1, 0:8] += jnp.sum(jnp.where(valid, dg8 * g8, 0.0), axis=0, keepdims=True)
            sm_ref[1:2, 0:8] += jnp.sum(jnp.where(valid, dalpha, 0.0), axis=0, keepdims=True)

    return pl.pallas_call(
        body, name="dn_post_bwd", grid=(nb,),
        in_specs=[_rows(TG, 1536), _halo_prev(1536), _full((CONV_K, 1536)), _rows(TG, 8),
                  pl.BlockSpec((8, TG), lambda i: (0, i)), _full((1, 8)), _full((1, 8)), _full((8, 1)), _full((8, 1)),
                  hm_spec(DHD), hm_spec(DHD), hm_spec(DHD), hm_spec(DHD), hm_spec(DHD), hm_spec(DHD), hm_spec(DHD)],
        out_specs=[_rows(TG, 1536), _rows(TG, 8), _full((8, LANES))],
        out_shape=[jax.ShapeDtypeStruct((T, 1536), F32), jax.ShapeDtypeStruct((T, 8), F32),
                   jax.ShapeDtypeStruct((8, LANES), F32)],
        compiler_params=_cp(("arbitrary",)),
    )(draw, draw, conv_w, ba, bat, alog8, dtb8, alog8t, dtb8t, du, dw, dqd, dkd, dgx, do, vn)


def _conv_bwd(dc, draw, conv_w):
    T = dc.shape[0]
    nb = T // TG

    def body(dc_ref, nxt_ref, x_ref, halo_ref, cw_ref, dx_ref, dcw_ref):
        i = pl.program_id(0)

        @pl.when(i == 0)
        def _():
            dcw_ref[...] = jnp.zeros_like(dcw_ref)

        dcv = dc_ref[...]
        nxt = jnp.where(i < nb - 1, nxt_ref[...], 0.0)
        halo = jnp.where(i > 0, halo_ref[...], 0.0)
        xv = x_ref[...]
        w = cw_ref[...]
        dx = dcv * w[CONV_K - 1:CONV_K, :]
        dcw_ref[CONV_K - 1:CONV_K, :] += jnp.sum(dcv * xv, axis=0, keepdims=True)
        for k in range(1, CONV_K):
            j = CONV_K - 1 - k
            dx = dx + _shift_up(dcv, nxt, k) * w[j:j + 1, :]
            dcw_ref[j:j + 1, :] += jnp.sum(dcv * _shift_down(xv, halo, k), axis=0, keepdims=True)
        dx_ref[...] = dx

    return pl.pallas_call(
        body, name="conv_bwd", grid=(nb,),
        in_specs=[_rows(TG, 1536),
                  pl.BlockSpec((8, 1536), lambda i: (jnp.minimum((i + 1) * (TG // 8), T // 8 - 1), 0)),
                  _rows(TG, 1536), _halo_prev(1536), _full((CONV_K, 1536))],
        out_specs=[_rows(TG, 1536), _full((8, 1536))],
        out_shape=[jax.ShapeDtypeStruct((T, 1536), F32), jax.ShapeDtypeStruct((8, 1536), F32)],
        compiler_params=_cp(("arbitrary",)),
    )(dc, dc, draw, draw, conv_w)


def _rms(x):
    return lax.rsqrt(jnp.mean(x * x, axis=1, keepdims=True) + EPS)


def _rms_bwd(dy, xh, r, g):
    dxh = dy * g
    return r * (dxh - xh * jnp.mean(dxh * xh, axis=1, keepdims=True))


def _hm_rows(tm):
    return pl.BlockSpec((NHD, tm, DHD), lambda i: (0, i, 0))


def _post_mix(apre, o, z, x, w_out, g_a, g_dn):
    T = x.shape[0]

    def body(ap_ref, o_ref, z_ref, x_ref, w_ref, ga_ref, gd_ref, x1_ref, mix_ref):
        ap = ap_ref[...]
        parts = [ap * _rms(ap) * ga_ref[...]]
        zz = z_ref[...]
        for h in range(NHD):
            oh = o_ref[h]
            zh = zz[:, DHD * h:DHD * (h + 1)]
            parts.append(oh * _rms(oh) * gd_ref[...] * (zh * _sigmoid(zh)))
        mix = jnp.concatenate(parts, axis=1).astype(_MXU)
        mix_ref[...] = mix
        x1_ref[...] = x_ref[...] + jnp.dot(mix, w_ref[...], preferred_element_type=F32)

    return pl.pallas_call(
        body, name="post_mix", grid=(T // TM,),
        in_specs=[_rows(TM, AW), _hm_rows(TM), _rows(TM, DW), _rows(TM, D), _full((D, D)), _full((1, AW)),
                  _full((1, DHD))],
        out_specs=[_rows(TM, D), _rows(TM, D)],
        out_shape=[jax.ShapeDtypeStruct((T, D), F32), jax.ShapeDtypeStruct((T, D), _MXU)],
        compiler_params=_cp(("arbitrary",)),
    )(apre, o, z, x, w_out, g_a, g_dn)


def _ffn(x1, tgt, w_gate, w_up, w_down, g_ffn):
    T = x1.shape[0]

    def body(x_ref, t_ref, wg_hbm, wu_hbm, wd_hbm, g_ref,
             dx1_ref, dx1b_ref, h2_ref, act_ref, dgu_ref, dyb_ref, loss_ref, dg_ref, wg, wu, wd, sem):
        @pl.when(pl.program_id(0) == 0)
        def _():
            cps = [pltpu.make_async_copy(s, d, sem.at[k]) for k, (s, d) in
                   enumerate(((wg_hbm, wg), (wu_hbm, wu), (wd_hbm, wd)))]
            for cp in cps:
                cp.start()
            for cp in cps:
                cp.wait()
            loss_ref[...] = jnp.zeros_like(loss_ref)
            dg_ref[...] = jnp.zeros_like(dg_ref)

        xv = x_ref[...]
        r = _rms(xv)
        xh = xv * r
        gg = g_ref[...]
        h2 = (xh * gg).astype(_MXU)
        h2_ref[...] = h2
        gate = jnp.dot(h2, wg[...], preferred_element_type=F32)
        up = jnp.dot(h2, wu[...], preferred_element_type=F32)
        sg = _sigmoid(gate)
        silu = gate * sg
        act = (silu * up).astype(_MXU)
        act_ref[...] = act
        y = xv + jnp.dot(act, wd[...], preferred_element_type=F32)
        err = y - t_ref[...]
        loss_ref[...] += jnp.sum(err * err, axis=0, keepdims=True)
        dy = err * (1.0 / D)
        dyb = dy.astype(_MXU)
        dyb_ref[...] = dyb
        dact = lax.dot_general(dyb, wd[...], (((1,), (1,)), ((), ())), preferred_element_type=F32)
        dgate = (dact * up * (sg * (1.0 + gate * (1.0 - sg)))).astype(_MXU)
        dup = (dact * silu).astype(_MXU)
        dgu_ref[:, 0:FF] = dgate
        dgu_ref[:, FF:2 * FF] = dup
        nt = (((1,), (1,)), ((), ()))
        dh2 = (lax.dot_general(dgate, wg[...], nt, preferred_element_type=F32)
               + lax.dot_general(dup, wu[...], nt, preferred_element_type=F32))
        dg_ref[...] += jnp.sum(dh2 * xh, axis=0, keepdims=True)
        dx1 = dy + _rms_bwd(dh2, xh, r, gg)
        dx1_ref[...] = dx1
        dx1b_ref[...] = dx1.astype(_MXU)

    anyspec = pl.BlockSpec(memory_space=pl.ANY)
    sd = lambda w, dt: jax.ShapeDtypeStruct((T, w), dt)
    return pl.pallas_call(
        body, name="ffn", grid=(T // TM,),
        in_specs=[_rows(TM, D), _rows(TM, D), anyspec, anyspec, anyspec, _full((1, D))],
        out_specs=[_rows(TM, D), _rows(TM, D), _rows(TM, D), _rows(TM, FF), _rows(TM, 2 * FF), _rows(TM, D),
                   _full((1, D)), _full((1, D))],
        out_shape=[sd(D, F32), sd(D, _MXU), sd(D, _MXU), sd(FF, _MXU), sd(2 * FF, _MXU), sd(D, _MXU),
                   jax.ShapeDtypeStruct((1, D), F32), jax.ShapeDtypeStruct((1, D), F32)],
        scratch_shapes=[pltpu.VMEM((D, FF), _MXU), pltpu.VMEM((D, FF), _MXU), pltpu.VMEM((FF, D), _MXU),
                        pltpu.SemaphoreType.DMA((3,))],
        compiler_params=_cp(("arbitrary",)),
    )(x1, tgt, w_gate, w_up, w_down, g_ffn)


def _mix_bwd(dx1b, w_out, apre, o, z, g_a, g_dn):
    T = dx1b.shape[0]

    def body(dx_ref, w_ref, ap_ref, o_ref, z_ref, ga_ref, gd_ref, dap_ref, do_ref, dz_ref, dga_ref, dgd_ref):
        @pl.when(pl.program_id(0) == 0)
        def _():
            dga_ref[...] = jnp.zeros_like(dga_ref)
            dgd_ref[...] = jnp.zeros_like(dgd_ref)

        dmix = lax.dot_general(dx_ref[...], w_ref[...], (((1,), (1,)), ((), ())), preferred_element_type=F32)
        ap = ap_ref[...]
        ra = _rms(ap)
        ah = ap * ra
        da = dmix[:, 0:AW]
        dga_ref[...] += jnp.sum(da * ah, axis=0, keepdims=True)
        dap_ref[...] = _rms_bwd(da, ah, ra, ga_ref[...])
        zz = z_ref[...]
        gd = gd_ref[...]
        for h in range(NHD):
            cs = slice(DHD * h, DHD * (h + 1))
            dd = dmix[:, AW + DHD * h:AW + DHD * (h + 1)]
            oh = o_ref[h]
            ro = _rms(oh)
            ohh = oh * ro
            zh = zz[:, cs]
            sz = _sigmoid(zh)
            dz_ref[:, cs] = dd * (ohh * gd) * (sz * (1.0 + zh * (1.0 - sz)))
            don = dd * (zh * sz)
            dgd_ref[...] += jnp.sum(don * ohh, axis=0, keepdims=True)
            do_ref[h] = _rms_bwd(don, ohh, ro, gd)

    return pl.pallas_call(
        body, name="mix_bwd", grid=(T // TM,),
        in_specs=[_rows(TM, D), _full((D, D)), _rows(TM, AW), _hm_rows(TM), _rows(TM, DW), _full((1, AW)),
                  _full((1, DHD))],
        out_specs=[_rows(TM, AW), _hm_rows(TM), _rows(TM, DW), _full((1, AW)), _full((1, DHD))],
        out_shape=[jax.ShapeDtypeStruct((T, AW), F32), jax.ShapeDtypeStruct((NHD, T, DHD), F32),
                   jax.ShapeDtypeStruct((T, DW), F32), jax.ShapeDtypeStruct((1, AW), F32),
                   jax.ShapeDtypeStruct((1, DHD), F32)],
        compiler_params=_cp(("arbitrary",)),
    )(dx1b, w_out, apre, o, z, g_a, g_dn)


def _inproj_bwd(dqn, dkn, dv, araw, ddraw, dz, dba, x, dx1, w_in, g_mix, qg_t, kg_t):
    T = x.shape[0]

    def body(dqn_ref, dkn_ref, dv_ref, ar_ref, dd_ref, dz_ref, dba_ref, x_ref, dx1_ref, w_ref, g_ref, qg_ref, kg_ref,
             dx_ref, dp_ref, dgm_ref, dqg_ref, dkg_ref):
        @pl.when(pl.program_id(0) == 0)
        def _():
            dgm_ref[...] = jnp.zeros_like(dgm_ref)
            dqg_ref[...] = jnp.zeros_like(dqg_ref)
            dkg_ref[...] = jnp.zeros_like(dkg_ref)

        bd = _block_ones(AW, DHA)

        def head_norm_bwd(raw, dyn, gain, dg_ref):
            r = _head_rms(raw, bd, DHA)
            xh = raw * r
            dg_ref[...] += jnp.sum(dyn * xh, axis=0, keepdims=True)
            dxh = dyn * gain
            return r * (dxh - xh * (_dot_x2(dxh * xh, bd) * (1.0 / DHA)))

        ar = ar_ref[...]
        dq = head_norm_bwd(ar[:, 0:AW], dqn_ref[...] * (DHA ** -0.5), qg_ref[...], dqg_ref)
        dk = head_norm_bwd(ar[:, AW:2 * AW], dkn_ref[...], kg_ref[...], dkg_ref)
        nt = (((1,), (1,)), ((), ()))
        dh = jnp.zeros((TM, D), F32)
        for lo, val in ((0, dq), (AW, dk), (2 * AW, dv_ref[...]), (1536, dd_ref[...]), (3072, dz_ref[...]),
                        (3584, dba_ref[...])):
            vb = val.astype(_MXU)
            wd_ = val.shape[1]
            dp_ref[:, lo:lo + wd_] = vb
            dh = dh + lax.dot_general(vb, w_ref[:, lo:lo + wd_], nt, preferred_element_type=F32)
        xv = x_ref[...]
        r = _rms(xv)
        xh = xv * r
        dgm_ref[...] += jnp.sum(dh * xh, axis=0, keepdims=True)
        dx_ref[...] = dx1_ref[...] + _rms_bwd(dh, xh, r, g_ref[...])

    return pl.pallas_call(
        body, name="inproj_bwd", grid=(T // TM,),
        in_specs=[_rows(TM, AW), _rows(TM, AW), _rows(TM, AW), _rows(TM, 1536), _rows(TM, 1536), _rows(TM, DW),
                  _rows(TM, 8), _rows(TM, D), _rows(TM, D), _full((D, 3592)), _full((1, D)), _full((1, AW)),
                  _full((1, AW))],
        out_specs=[_rows(TM, D), _rows(TM, 3592), _full((1, D)), _full((1, AW)), _full((1, AW))],
        out_shape=[jax.ShapeDtypeStruct((T, D), F32), jax.ShapeDtypeStruct((T, 3592), _MXU),
                   jax.ShapeDtypeStruct((1, D), F32), jax.ShapeDtypeStruct((1, AW), F32),
                   jax.ShapeDtypeStruct((1, AW), F32)],
        compiler_params=_cp(("arbitrary",)),
    )(dqn, dkn, dv, araw, ddraw, dz, dba, x, dx1, w_in, g_mix, qg_t, kg_t)


def _wgrad(a, b, name, tk=512, tn=None):
    T, M = a.shape
    N = b.shape[1]
    tn = N if tn is None else tn
    nk = T // tk

    def body(a_ref, b_ref, o_ref):
        @pl.when(pl.program_id(1) == 0)
        def _():
            o_ref[...] = jnp.zeros_like(o_ref)

        o_ref[...] += lax.dot_general(a_ref[...], b_ref[...], (((0,), (0,)), ((), ())), preferred_element_type=F32)

    return pl.pallas_call(
        body, name=name, grid=(N // tn, nk),
        in_specs=[pl.BlockSpec((tk, M), lambda j, k: (k, 0)), pl.BlockSpec((tk, tn), lambda j, k: (k, j))],
        out_specs=pl.BlockSpec((M, tn), lambda j, k: (0, j)),
        out_shape=jax.ShapeDtypeStruct((M, N), F32),
        compiler_params=_cp(("arbitrary", "arbitrary")),
    )(a, b)


def _adamw(parts, w, m, v, name, tr):
    K, R, W = parts.shape

    def body(p_ref, w_ref, m_ref, v_ref, g_ref, d_ref, nm_ref, nv_ref):
        g = p_ref[0].astype(F32)
        for k in range(1, K):
            g = g + p_ref[k].astype(F32)
        g_ref[...] = g
        nm = ADAM_B1 * m_ref[...] + (1.0 - ADAM_B1) * g
        nv = ADAM_B2 * v_ref[...] + (1.0 - ADAM_B2) * (g * g)
        nm_ref[...] = nm
        nv_ref[...] = nv
        m_hat = nm / (1.0 - ADAM_B1 ** ADAM_STEP)
        v_hat = nv / (1.0 - ADAM_B2 ** ADAM_STEP)
        d_ref[...] = -ADAM_LR * (m_hat / (jnp.sqrt(v_hat) + ADAM_EPS) + ADAM_WD * w_ref[...])

    o = jax.ShapeDtypeStruct((R, W), F32)
    return pl.pallas_call(
        body, name=name, grid=(R // tr,),
        in_specs=[pl.BlockSpec((K, tr, W), lambda i: (0, i, 0)), _rows(tr, W), _rows(tr, W), _rows(tr, W)],
        out_specs=[_rows(tr, W)] * 4,
        out_shape=[o] * 4,
        compiler_params=_cp(("arbitrary",)),
    )(parts, w, m, v)


SM_ROWS = 136
R_GMIX, R_GFFN, R_QG, R_KG, R_GA, R_GDN, R_ALOG, R_DT, R_LOSS, R_CONV, R_REL = 0, 8, 16, 24, 32, 40, 48, 49, 56, 64, 112


def _small_reduce(gathered):
    def body(p_ref, o_ref):
        s = p_ref[0]
        for k in range(1, N_DEV):
            s = s + p_ref[k]
        o_ref[...] = s
        for r0 in (R_QG, R_KG):
            rs = jnp.sum(s[r0:r0 + 4], axis=0, keepdims=True)
            o_ref[r0:r0 + 1, :] = rs + pltpu.roll(rs, DHA, 1)
        tot = jnp.sum(jnp.sum(s[R_LOSS:R_LOSS + 8], axis=0, keepdims=True), axis=1, keepdims=True)
        o_ref[R_LOSS:R_LOSS + 1, :] = jnp.broadcast_to(tot * (0.5 / D), (1, LANES))

    return pl.pallas_call(
        body, name="small_reduce",
        out_shape=jax.ShapeDtypeStruct((SM_ROWS, LANES), F32),
    )(gathered)


_WIRE = jnp.bfloat16
RA_USED, RA = 449, 464
RL = 128 + 3 * 352


def _pack_rows(parts, rows=None):
    p = jnp.concatenate([t.reshape(-1, D) for t in parts], axis=0) if len(parts) > 1 else parts[0].reshape(-1, D)
    return p if rows is None else jnp.pad(p, ((0, rows - p.shape[0]), (0, 0)))


def _unpack_rows(packed, shapes):
    out, r = [], 0
    for shp in shapes:
        nr = math.prod(shp) // D
        out.append(packed[r:r + nr].reshape(shp))
        r += nr
    return out


def _pad8(t):
    return jnp.pad(t, ((0, (-t.shape[0]) % 8), (0, 0)))


def _pack_lanes(parts):
    rows = []
    for p in parts:
        f = p.reshape(-1)
        pad = (-f.shape[0]) % LANES
        rows.append(jnp.pad(f, (0, pad)).reshape(-1, LANES))
    return jnp.concatenate(rows, axis=0)


def _unpack_lanes(packed, shapes):
    out, r = [], 0
    for shp in shapes:
        n = math.prod(shp)
        nr = -(-n // LANES)
        out.append(packed[r:r + nr].reshape(-1)[:n].reshape(shp))
        r += nr
    return out


def kernel(x, norm_mix_g, w_in, attn_q_norm_g, attn_k_norm_g, rel_bias, attn_out_norm_g, conv_w, a_log, dt_bias, dn_out_norm_g, w_out, norm_ffn_g, w_gate, w_up, w_down, loss_target, m_norm_mix_g, m_w_in, m_attn_q_norm_g, m_attn_k_norm_g, m_rel_bias, m_attn_out_norm_g, m_conv_w, m_a_log, m_dt_bias, m_dn_out_norm_g, m_w_out, m_norm_ffn_g, m_w_gate, m_w_up, m_w_down, v_norm_mix_g, v_w_in, v_attn_q_norm_g, v_attn_k_norm_g, v_rel_bias, v_attn_out_norm_g, v_conv_w, v_a_log, v_dt_bias, v_dn_out_norm_g, v_w_out, v_norm_ffn_g, v_w_gate, v_w_up, v_w_down):
    xs, tgt = x[0], loss_target[0]
    T = xs.shape[0]
    my_idx = 4 * lax.axis_index("x") + 2 * lax.axis_index("y") + lax.axis_index("c")
    late_w = (w_out[0], w_gate[0], w_up[0], w_down[0])
    late_shapes = [w.shape for w in late_w]

    wa_all = _all_gather(_pack_rows([w_in[0].astype(_MXU)], RA), "gather_w_in")
    cw_all = _all_gather(jnp.pad(conv_w[0], ((0, 4), (0, 64))), "gather_conv")
    by_dev = lambda a, k: a.reshape(N_DEV, D, k).transpose(1, 0, 2).reshape(D, N_DEV * k)
    W_in = by_dev(wa_all[:, 0:RA_USED], RA_USED)
    conv_full = cw_all[:, 0:CONV_K, 0:192].transpose(1, 0, 2).reshape(CONV_K, 1536)

    qg_t = jnp.tile(attn_q_norm_g, (1, NHA))
    kg_t = jnp.tile(attn_k_norm_g, (1, NHA))
    z4 = jnp.zeros((1, NHD), F32)
    alog8 = jnp.concatenate([z4, a_log], axis=1)
    dtb8 = jnp.concatenate([z4, dt_bias], axis=1)

    araw, an, draw, z, ba, hb, wl_all = _inproj(xs, norm_mix_g, W_in, qg_t, kg_t,
                                                _pack_rows([w.astype(_MXU) for w in late_w]))
    W_out = wl_all[:, 0:128].reshape(D, D)
    W_gate = by_dev(wl_all[:, 128:480], 352)
    W_up = by_dev(wl_all[:, 480:832], 352)
    W_down = wl_all[:, 832:RL].reshape(FF, D)
    tab, tabt = _bias_tables(rel_bias[0].T)
    apre = _attn_fwd(an, tab)
    bat = ba.T
    dn_args = (draw, conv_full, ba, bat, alog8, dtb8, alog8.T, dtb8.T)
    u, w, qd, kd, mc, mct, kdt, qdt, wt, elb = _dn_prep(*dn_args)
    o, vn, sn = _dn_scan(u, w, qd, mc, kdt, elb)
    x1, mix = _post_mix(apre, o, z, xs, W_out, attn_out_norm_g, dn_out_norm_g)

    dx1, dx1b, h2, act, dgu, dyb, loss_row, dgffn = _ffn(x1, tgt, W_gate, W_up, W_down, norm_ffn_g)

    by_cols = lambda g, k: g.reshape(D, N_DEV, k).transpose(1, 0, 2).reshape(N_DEV, -1, D)
    gW_out = _wgrad(mix, dx1b, "wgrad_out")
    gW_gu = _wgrad(h2, dgu, "wgrad_gate_up", tn=FF)
    gW_down_t = _wgrad(dyb, act, "wgrad_down")
    send_late = jnp.concatenate(
        [gW_out.reshape(N_DEV, 128, D), by_cols(gW_gu[:, 0:FF], 352), by_cols(gW_gu[:, FF:], 352),
         gW_down_t.T.reshape(N_DEV, 352, D)], axis=1).astype(_WIRE)

    dap, do, dz, dga, dgdn = _mix_bwd(dx1b, W_out, apre, o, z, attn_out_norm_g, dn_out_norm_g)
    dqn, dkn, dv, dtabt, recv_late = _attn_bwd(an, dap, tabt, send_late)
    drel = _bias_grad(dtabt)
    du, dw, dqd, dkd, dgx = _dn_scan_bwd(do, mct, kd, qdt, wt, sn, vn, elb)
    dc, dba, sm = _dn_post_bwd(*dn_args, du, dw, dqd, dkd, dgx, do, vn)
    ddraw, dcw = _conv_bwd(dc, draw, conv_full)
    gx, dproj, dgmix, dqg, dkg = _inproj_bwd(dqn, dkn, dv, araw, ddraw, dz, dba, xs, dx1, W_in, norm_mix_g, qg_t, kg_t)

    gW_in = _wgrad(hb, dproj, "wgrad_in", tk=256)
    send_in = jnp.pad(by_cols(gW_in, RA_USED).astype(_WIRE), ((0, 0), (0, RA - RA_USED), (0, 0)))
    recv_in = _all_to_all(send_in, "scatter_w_in")
    late_m = (m_w_out[0], m_w_gate[0], m_w_up[0], m_w_down[0])
    late_v = (v_w_out[0], v_w_gate[0], v_w_up[0], v_w_down[0])
    outs_late = _adamw(recv_late, _pack_rows(late_w), _pack_rows(late_m), _pack_rows(late_v), "adamw_late", 32)
    outs_in = _adamw(recv_in, _pack_rows([w_in[0]], RA), _pack_rows([m_w_in[0]], RA), _pack_rows([v_w_in[0]], RA),
                     "adamw_w_in", 16)
    big = [[_unpack_rows(a, [w_in[0].shape])[0]] + _unpack_rows(b, late_shapes) for a, b in zip(outs_in, outs_late)]
    bg, bd_, bm, bv = big

    partial = jnp.concatenate(
        [dgmix.reshape(8, LANES), dgffn.reshape(8, LANES), _pad8(dqg.reshape(4, LANES)), _pad8(dkg.reshape(4, LANES)),
         _pad8(dga.reshape(4, LANES)), _pad8(dgdn), sm, loss_row.reshape(8, LANES),
         dcw[0:CONV_K].reshape(48, LANES), drel.reshape(24, LANES)], axis=0)
    S = _small_reduce(_all_gather(partial, "gather_small"))
    loss = S[R_LOSS, 0]
    g_conv = lax.dynamic_slice(S[R_CONV:R_CONV + 48].reshape(CONV_K, 1536), (0, 192 * my_idx), (CONV_K, 192))
    sg = [S[R_GMIX:R_GMIX + 8].reshape(1, D), S[R_QG:R_QG + 1, 0:DHA], S[R_KG:R_KG + 1, 0:DHA],
          S[R_REL:R_REL + 24].reshape(NHA, 384)[:, 0:257].T, S[R_GA:R_GA + 4].reshape(1, AW), g_conv,
          S[R_ALOG:R_ALOG + 1, NHD:2 * NHD], S[R_DT:R_DT + 1, NHD:2 * NHD], S[R_GDN:R_GDN + 1], S[R_GFFN:R_GFFN + 8].reshape(1, D)]
    sw = [norm_mix_g, attn_q_norm_g, attn_k_norm_g, rel_bias[0], attn_out_norm_g, conv_w[0], a_log, dt_bias, dn_out_norm_g, norm_ffn_g]
    smm = [m_norm_mix_g, m_attn_q_norm_g, m_attn_k_norm_g, m_rel_bias[0], m_attn_out_norm_g, m_conv_w[0], m_a_log, m_dt_bias, m_dn_out_norm_g, m_norm_ffn_g]
    svv = [v_norm_mix_g, v_attn_q_norm_g, v_attn_k_norm_g, v_rel_bias[0], v_attn_out_norm_g, v_conv_w[0], v_a_log, v_dt_bias, v_dn_out_norm_g, v_norm_ffn_g]
    s_shapes = [t.shape for t in sw]
    pk = lambda ts: _pack_lanes(ts)
    pg = pk(sg)
    padr = (-pg.shape[0]) % 8
    padz = lambda t: jnp.pad(t, ((0, padr), (0, 0)))
    s_out = _adamw(padz(pg)[None], padz(pk(sw)), padz(pk(smm)), padz(pk(svv)), "adamw_small", pg.shape[0] + padr)
    s_g, s_d, s_m, s_v = (_unpack_lanes(t, s_shapes) for t in s_out)

    lead = lambda t: t[None]
    def ordered(small, big):
        nm, q, k, rel, ao, cw, al, dtb, dno, nf = small
        wi, wo, wgt, wu, wdn = big
        return [nm, lead(wi), q, k, lead(rel), ao, lead(cw), al, dtb, dno, lead(wo), nf, lead(wgt), lead(wu), lead(wdn)]
    outs = [loss, gx[None]]
    for small, big in ((s_g, bg), (s_d, bd_), (s_m, bm), (s_v, bv)):
        outs += ordered(small, big)
    return tuple(outs)
```

```python
import functools
import math

import jax
import jax.numpy as jnp
from jax import lax
from jax.experimental import pallas as pl
from jax.experimental.pallas import tpu as pltpu

F32 = jnp.float32
BF16 = jnp.bfloat16
_MXU = jnp.bfloat16

D = 1024
AW = 512
NHA = 8
DHA = 64
CH = 64
BAND = 9
NHD = 4
DHD = 128
DW = 512
FF = 2816
EPS = 1e-6
NEG = -1e30
N_DEV = 8
LANES = 128
VMEM_LIMIT = 56 * 1024 * 1024

ADAM_LR = 0.001
ADAM_B1 = 0.9
ADAM_B2 = 0.999
ADAM_EPS = 1e-08
ADAM_WD = 0.01
ADAM_STEP = 10

MESH_T = pl.DeviceIdType.MESH


def _cp(sem=None, vmem=VMEM_LIMIT):
    kw = dict(vmem_limit_bytes=vmem)
    if sem is not None:
        kw["dimension_semantics"] = sem
    return pltpu.CompilerParams(**kw)


def _dot(a, b):
    return jnp.dot(a.astype(_MXU), b.astype(_MXU), preferred_element_type=F32)


def _dot_nt(a, b):
    return lax.dot_general(a.astype(_MXU), b.astype(_MXU), (((1,), (1,)), ((), ())), preferred_element_type=F32)


def _dot_tn(a, b):
    return lax.dot_general(a.astype(_MXU), b.astype(_MXU), (((0,), (0,)), ((), ())), preferred_element_type=F32)


def _split2(x):
    hi = x.astype(BF16)
    lo = (x - hi.astype(F32)).astype(BF16)
    return hi, lo


def _dot_x2(x, ones_b):
    hi, lo = _split2(x)
    return jnp.dot(hi, ones_b, preferred_element_type=F32) + jnp.dot(lo, ones_b, preferred_element_type=F32)


def _dot_x2_nt(x, ones_b):
    hi, lo = _split2(x)
    dn = (((1,), (1,)), ((), ()))
    return lax.dot_general(hi, ones_b, dn, preferred_element_type=F32) + lax.dot_general(
        lo, ones_b, dn, preferred_element_type=F32)


def _iota(shape, dim):
    return lax.broadcasted_iota(jnp.int32, shape, dim)


def _block_ones(n, blk, dtype=BF16):
    r, c = _iota((n, n), 0), _iota((n, n), 1)
    return jnp.where((r // blk) == (c // blk), 1.0, 0.0).astype(dtype)


def _sigmoid(x):
    return 1.0 / (1.0 + jnp.exp(-x))


def _softplus(x):
    return jnp.maximum(x, 0.0) + jnp.log(1.0 + jnp.exp(-jnp.abs(x)))


def _col(x, k):
    lane = _iota(x.shape, 1)
    return jnp.sum(jnp.where(lane == k, x, 0.0), axis=1, keepdims=True)


def _row(x, k):
    sub = _iota(x.shape, 0)
    return jnp.sum(jnp.where(sub == k, x, 0.0), axis=0, keepdims=True)


def _my_pos():
    return lax.axis_index("x"), lax.axis_index("y"), lax.axis_index("c")


def _all_gather(x2d, name):
    R, W = x2d.shape

    def body(x_ref, out_ref, send_sems, recv_sems, local_sem):
        ag = _Gather(x_ref, out_ref, send_sems, recv_sems, local_sem)
        ag.start()
        ag.forward()
        ag.finish()

    return pl.pallas_call(
        body, name=name,
        out_shape=jax.ShapeDtypeStruct((N_DEV, R, W), x2d.dtype),
        in_specs=[pl.BlockSpec(memory_space=pl.ANY)],
        out_specs=pl.BlockSpec(memory_space=pl.ANY),
        scratch_shapes=_COMM_SEMS,
    )(x2d)


_COMM_SEMS = [pltpu.SemaphoreType.DMA((7,)), pltpu.SemaphoreType.DMA((7,)), pltpu.SemaphoreType.DMA]


class _Gather:
    def __init__(self, x_ref, out_ref, send_sems, recv_sems, local_sem):
        x, y, c = _my_pos()
        me, sibling = (x, y, c), (x, y, 1 - c)
        chips = [(1 - x, y), (x, 1 - y), (1 - x, 1 - y)]

        def slot(px, py, pc):
            return out_ref.at[4 * px + 2 * py + pc]

        def copy(k, block, to, src=None):
            return pltpu.make_async_remote_copy(
                src_ref=slot(*block) if src is None else src, dst_ref=slot(*block),
                send_sem=send_sems.at[k], recv_sem=recv_sems.at[k], device_id=to, device_id_type=MESH_T)

        self.mine = pltpu.make_async_copy(x_ref, slot(*me), local_sem)
        self.first = [copy(0, me, sibling, src=x_ref)]
        self.first += [copy(1 + j, me, (*chip, c), src=x_ref) for j, chip in enumerate(chips)]
        self.passed = [copy(4 + j, (*chip, c), sibling) for j, chip in enumerate(chips)]
        self.from_chips = [copy(1 + j, (*chip, c), me) for j, chip in enumerate(chips)]
        self.from_sibling = [copy(0, sibling, me)] + [copy(4 + j, (*chip, 1 - c), me) for j, chip in enumerate(chips)]

    def start(self):
        self.mine.start()
        for cp in self.first:
            cp.start()

    def forward(self):
        for arrived, onward in zip(self.from_chips, self.passed):
            arrived.wait_recv()
            onward.start()

    def finish(self):
        for cp in self.from_sibling:
            cp.wait_recv()
        for cp in self.first + self.passed:
            cp.wait_send()
        self.mine.wait()


class _Scatter:
    def __init__(self, s_ref, r_ref, send_sems, recv_sems, local_sem):
        x, y, c = _my_pos()
        self.mine = pltpu.make_async_copy(s_ref.at[4 * x + 2 * y + c], r_ref.at[0], local_sem)
        self.copies = []
        for m in range(1, N_DEV):
            px = x ^ ((m >> 2) & 1)
            py = y ^ ((m >> 1) & 1)
            pc = c ^ (m & 1)
            self.copies.append(pltpu.make_async_remote_copy(
                src_ref=s_ref.at[4 * px + 2 * py + pc], dst_ref=r_ref.at[m],
                send_sem=send_sems.at[m - 1], recv_sem=recv_sems.at[m - 1],
                device_id=(px, py, pc), device_id_type=MESH_T))

    def start(self):
        self.mine.start()
        for cp in self.copies:
            cp.start()

    def finish(self):
        for cp in self.copies:
            cp.wait_recv()
        for cp in self.copies:
            cp.wait_send()
        self.mine.wait()


def _all_to_all(send, name):
    def body(s_ref, r_ref, send_sems, recv_sems, local_sem):
        sc = _Scatter(s_ref, r_ref, send_sems, recv_sems, local_sem)
        sc.start()
        sc.finish()

    return pl.pallas_call(
        body, name=name,
        out_shape=jax.ShapeDtypeStruct(send.shape, send.dtype),
        in_specs=[pl.BlockSpec(memory_space=pl.ANY)],
        out_specs=pl.BlockSpec(memory_space=pl.ANY),
        scratch_shapes=_COMM_SEMS,
    )(send)


TM = 256
TG = 512


def _full(shape):
    nd = len(shape)
    return pl.BlockSpec(shape, lambda i: (0,) * nd)


def _rows(tm, w):
    return pl.BlockSpec((tm, w), lambda i: (i, 0))


def _head_sum(x, bd):
    return jnp.concatenate([_dot_x2(x[:, 0:256], bd), _dot_x2(x[:, 256:512], bd)], axis=1)


def _head_rms(x, bd, width):
    return lax.rsqrt(_head_sum(x * x, bd) * (1.0 / width) + EPS)


def _inproj(x, g_mix, w_in, qg_t, kg_t, later_w):
    T = x.shape[0]
    nt = T // TM

    def body(x_ref, g_ref, w_ref, qg_ref, kg_ref, lw_ref, araw_ref, an_ref, draw_ref, z_ref, ba_ref, h_ref, lw_all,
             send_sems, recv_sems, local_sem):
        i = pl.program_id(0)
        ag = _Gather(lw_ref, lw_all, send_sems, recv_sems, local_sem)
        pl.when(i == 0)(ag.start)
        pl.when(i == nt // 2)(ag.forward)
        xv = x_ref[...]
        r = lax.rsqrt(jnp.mean(xv * xv, axis=1, keepdims=True) + EPS)
        h = (xv * r * g_ref[...]).astype(_MXU)
        h_ref[...] = h
        a = jnp.dot(h, w_ref[:, 0:1536], preferred_element_type=F32)
        araw_ref[...] = a
        bd = _block_ones(AW // 2, DHA)
        q = a[:, 0:AW]
        k = a[:, AW:2 * AW]
        qn = q * _head_rms(q, bd, DHA) * (qg_ref[...] * (DHA ** -0.5))
        kn = k * _head_rms(k, bd, DHA) * kg_ref[...]
        an_ref[:, 0:AW] = qn.astype(_MXU)
        an_ref[:, AW:2 * AW] = kn.astype(_MXU)
        an_ref[:, 2 * AW:3 * AW] = a[:, 2 * AW:3 * AW].astype(_MXU)
        draw_ref[...] = jnp.dot(h, w_ref[:, 1536:3072], preferred_element_type=F32)
        z_ref[...] = jnp.dot(h, w_ref[:, 3072:3584], preferred_element_type=F32)
        ba_ref[...] = jnp.dot(h, w_ref[:, 3584:3592], preferred_element_type=F32)
        pl.when(i == nt - 1)(ag.finish)

    anyspec = pl.BlockSpec(memory_space=pl.ANY)
    return pl.pallas_call(
        body, name="inproj", grid=(nt,),
        in_specs=[_rows(TM, D), _full((1, D)), _full((D, 3592)), _full((1, AW)), _full((1, AW)), anyspec],
        out_specs=[_rows(TM, 1536), _rows(TM, 1536), _rows(TM, 1536), _rows(TM, DW), _rows(TM, 8), _rows(TM, D),
                   anyspec],
        out_shape=[jax.ShapeDtypeStruct((T, 1536), F32), jax.ShapeDtypeStruct((T, 1536), _MXU),
                   jax.ShapeDtypeStruct((T, 1536), F32), jax.ShapeDtypeStruct((T, DW), F32),
                   jax.ShapeDtypeStruct((T, 8), F32), jax.ShapeDtypeStruct((T, D), _MXU),
                   jax.ShapeDtypeStruct((N_DEV,) + later_w.shape, later_w.dtype)],
        scratch_shapes=_COMM_SEMS,
        compiler_params=_cp(("arbitrary",)),
    )(x, g_mix, w_in, qg_t, kg_t, later_w)


TQ = 256
TW = 768
T_LO, T_HI = 65, 256
VAR0 = 384
TOEP = 1024


def _bias_tables(rb_t):
    def body(rb_ref, tab_ref, tabt_ref):
        h = pl.program_id(0)
        n = _iota((8, TOEP), 1)

        def line(m):
            idx = jnp.clip(512 - m, -128, 128) + 128
            return lax.fori_loop(T_LO, T_HI, lambda t, a: jnp.where(idx == t, rb_ref[h, t], a),
                                 jnp.full((8, TOEP), rb_ref[h, T_HI], F32))[0:1, :]

        def band(r, j):
            return ((j >> 6) >= (r >> 6)) & ((j >> 6) <= (r >> 6) + 8)

        g = line(jnp.where(n < TW, n, n - TOEP))
        tab = pltpu.roll(jnp.broadcast_to(g, (TQ, TOEP)), 0, 1, stride=1, stride_axis=0)[:, 0:TW]
        tab_ref[0] = jnp.where(band(_iota((TQ, TW), 0), _iota((TQ, TW), 1)), tab, NEG)
        gt = line(jnp.where(n < TQ, -n, TOEP - n))
        tabt = pltpu.roll(jnp.broadcast_to(gt, (TW, TOEP)), 0, 1, stride=1, stride_axis=0)[:, 0:TQ]
        tabt_ref[0] = jnp.where(band(_iota((TW, TQ), 1), _iota((TW, TQ), 0)), tabt, NEG)

    return pl.pallas_call(
        body, name="bias_tables", grid=(NHA,),
        in_specs=[pl.BlockSpec(memory_space=pltpu.SMEM)],
        out_specs=[pl.BlockSpec((1, TQ, TW), lambda h: (h, 0, 0)), pl.BlockSpec((1, TW, TQ), lambda h: (h, 0, 0))],
        out_shape=[jax.ShapeDtypeStruct((NHA, TQ, TW), F32), jax.ShapeDtypeStruct((NHA, TW, TQ), F32)],
        compiler_params=_cp(("arbitrary",)),
    )(rb_t)


def _bias_grad(dtabt):
    def body(d_ref, o_ref):
        a, b = _iota((TQ, TQ), 0), _iota((TQ, TQ), 1)
        anti = jnp.where(a + b == TQ - 1, 1.0, 0.0).astype(BF16)
        drev = sum(jnp.dot(t, anti, preferred_element_type=F32) for t in _split3(d_ref[0]))
        wide = jnp.concatenate([drev, jnp.zeros((TW, TOEP - TQ), F32)], axis=1)
        cols = jnp.sum(pltpu.roll(wide, 0, 1, stride=1, stride_axis=0), axis=0, keepdims=True)
        c = _iota((TOEP, VAR0), 0)
        idx = jnp.clip(512 + TQ - 1 - c, -128, 128) + 128
        onehot = jnp.where(idx == _iota((TOEP, VAR0), 1), 1.0, 0.0).astype(BF16)
        cols8 = jnp.broadcast_to(cols, (8, TOEP))
        o_ref[0] = sum(jnp.dot(t, onehot, preferred_element_type=F32) for t in _split3(cols8))[0:1, :]

    return pl.pallas_call(
        body, name="bias_grad", grid=(NHA,),
        in_specs=[pl.BlockSpec((1, TW, TQ), lambda h: (h, 0, 0))],
        out_specs=pl.BlockSpec((1, 1, VAR0), lambda h: (h, 0, 0)),
        out_shape=jax.ShapeDtypeStruct((NHA, 1, VAR0), F32),
        compiler_params=_cp(("arbitrary",)),
    )(dtabt)


def _kv_spec(col, back):
    return pl.BlockSpec((TQ, AW), lambda i: (jnp.maximum(i - back, 0), col))


def _attn_fwd(an, tab):
    T = an.shape[0]

    def body(q_ref, k2_ref, k1_ref, k0_ref, v2_ref, v1_ref, v0_ref, tab_ref, o_ref):
        i = pl.program_id(0)
        kwin = jnp.concatenate([k2_ref[...], k1_ref[...], k0_ref[...]], axis=0)
        vwin = jnp.concatenate([v2_ref[...], v1_ref[...], v0_ref[...]], axis=0)
        q = q_ref[...]
        invalid = (_iota((TQ, TW), 1) + TQ * i) < 512
        lo_half = _iota((TQ, LANES), 1) < DHA
        for p in range(NHA // 2):
            sl = slice(LANES * p, LANES * (p + 1))
            qp, kp, vp = q[:, sl], kwin[:, sl], vwin[:, sl]
            outs = []
            for half in range(2):
                mask = lo_half if half == 0 else jnp.logical_not(lo_half)
                qm = jnp.where(mask, qp, jnp.zeros_like(qp))
                s = _dot_nt(qm, kp) + tab_ref[2 * p + half]
                s = jnp.where(invalid, NEG, s)
                m = jnp.max(s, axis=1, keepdims=True)
                e = jnp.exp(s - m)
                l = jnp.sum(e, axis=1, keepdims=True)
                outs.append(_dot(e, vp) / l)
            o_ref[:, sl] = jnp.where(lo_half, outs[0], outs[1])

    return pl.pallas_call(
        body, name="attn_fwd", grid=(T // TQ,),
        in_specs=[pl.BlockSpec((TQ, AW), lambda i: (i, 0)),
                  _kv_spec(1, 2), _kv_spec(1, 1), _kv_spec(1, 0), _kv_spec(2, 2), _kv_spec(2, 1), _kv_spec(2, 0),
                  _full((NHA, TQ, TW))],
        out_specs=_rows(TQ, AW),
        out_shape=jax.ShapeDtypeStruct((T, AW), F32),
        compiler_params=_cp(("arbitrary",)),
    )(an, an, an, an, an, an, an, tab)


def _attn_bwd(an, dout, tabt, send):
    T = an.shape[0]
    nq = T // TQ

    def qi(i):
        return jnp.minimum(i, nq - 1)

    def kv_spec(col, back):
        return pl.BlockSpec((TQ, AW), lambda i: (jnp.maximum(qi(i) - back, 0), col))

    def body(q_ref, do_ref, k2_ref, k1_ref, k0_ref, v2_ref, v1_ref, v0_ref, tabt_ref, send_ref,
             dq_ref, dk_ref, dv_ref, dtab_ref, recv_ref, dk_acc, dv_acc, send_sems, recv_sems, local_sem):
        i = pl.program_id(0)
        sc = _Scatter(send_ref, recv_ref, send_sems, recv_sems, local_sem)
        pl.when(i == 0)(sc.start)

        @pl.when(i == 0)
        def _():
            dtab_ref[...] = jnp.zeros_like(dtab_ref)

        new = i % 3
        dk_acc[new] = jnp.zeros((TQ, AW), F32)
        dv_acc[new] = jnp.zeros((TQ, AW), F32)

        @pl.when(i < nq)
        def _():
            kwin = jnp.concatenate([k2_ref[...], k1_ref[...], k0_ref[...]], axis=0)
            vwin = jnp.concatenate([v2_ref[...], v1_ref[...], v0_ref[...]], axis=0)
            q = q_ref[...]
            do = do_ref[...].astype(_MXU)
            invalid = (_iota((TW, TQ), 0) + TQ * i) < 512
            lo_half = _iota((TQ, LANES), 1) < DHA
            for p in range(NHA // 2):
                sl = slice(LANES * p, LANES * (p + 1))
                qp, kp, vp, dop = q[:, sl], kwin[:, sl], vwin[:, sl], do[:, sl]
                dq_pair = jnp.zeros((TQ, LANES), F32)
                dk_pair = jnp.zeros((TW, LANES), F32)
                dv_pair = jnp.zeros((TW, LANES), F32)
                for half in range(2):
                    h = 2 * p + half
                    mask = lo_half if half == 0 else jnp.logical_not(lo_half)
                    qm = jnp.where(mask, qp, jnp.zeros_like(qp))
                    dom = jnp.where(mask, dop, jnp.zeros_like(dop))
                    st = _dot_nt(kp, qm) + tabt_ref[h]
                    st = jnp.where(invalid, NEG, st)
                    m = jnp.max(st, axis=0, keepdims=True)
                    e = jnp.exp(st - m)
                    pt = e * (1.0 / jnp.sum(e, axis=0, keepdims=True))
                    dpt = _dot_nt(vp, dom)
                    delta = jnp.sum(pt * dpt, axis=0, keepdims=True)
                    dst = pt * (dpt - delta)
                    dtab_ref[h] += dst
                    dsb = dst.astype(_MXU)
                    dv_pair += _dot(pt, dom)
                    dk_pair += _dot(dsb, qm)
                    dq_pair += jnp.where(mask, _dot_tn(dsb, kp), 0.0)
                dq_ref[:, sl] = dq_pair
                for w in range(3):
                    slot = (i + 1 + w) % 3
                    rows = slice(TQ * w, TQ * (w + 1))
                    dk_acc[slot, :, sl] += dk_pair[rows]
                    dv_acc[slot, :, sl] += dv_pair[rows]

        @pl.when(i >= 2)
        def _():
            done = (i + 1) % 3
            dk_ref[...] = dk_acc[done]
            dv_ref[...] = dv_acc[done]

        pl.when(i == nq + 1)(sc.finish)

    back2 = pl.BlockSpec((TQ, AW), lambda i: (jnp.maximum(i - 2, 0), 0))
    anyspec = pl.BlockSpec(memory_space=pl.ANY)
    return pl.pallas_call(
        body, name="attn_bwd", grid=(nq + 2,),
        in_specs=[pl.BlockSpec((TQ, AW), lambda i: (qi(i), 0)), pl.BlockSpec((TQ, AW), lambda i: (qi(i), 0)),
                  kv_spec(1, 2), kv_spec(1, 1), kv_spec(1, 0), kv_spec(2, 2), kv_spec(2, 1), kv_spec(2, 0),
                  _full((NHA, TW, TQ)), anyspec],
        out_specs=[pl.BlockSpec((TQ, AW), lambda i: (qi(i), 0)), back2, back2, _full((NHA, TW, TQ)), anyspec],
        out_shape=[jax.ShapeDtypeStruct((T, AW), F32), jax.ShapeDtypeStruct((T, AW), F32),
                   jax.ShapeDtypeStruct((T, AW), F32), jax.ShapeDtypeStruct((NHA, TW, TQ), F32),
                   jax.ShapeDtypeStruct(send.shape, send.dtype)],
        scratch_shapes=[pltpu.VMEM((3, TQ, AW), F32), pltpu.VMEM((3, TQ, AW), F32)] + _COMM_SEMS,
        compiler_params=_cp(("arbitrary",)),
    )(an, dout, an, an, an, an, an, an, tabt, send)


GR = 128
NG = TG // GR
CPT = TG // CH
CONV_K = 4


def _split3(x):
    a = x.astype(BF16)
    r = x - a.astype(F32)
    b = r.astype(BF16)
    c = (r - b.astype(F32)).astype(BF16)
    return a, b, c


def _ones_dot(ones_b, x):
    return sum(jnp.dot(ones_b, t, preferred_element_type=F32) for t in _split3(x))


def _dot_ones_nt(x, ones_b):
    dn = (((1,), (1,)), ((), ()))
    return sum(lax.dot_general(t, ones_b, dn, preferred_element_type=F32) for t in _split3(x))


def _dn_masks():
    r, c = _iota((GR, GR), 0), _iota((GR, GR), 1)
    same = (r >> 6) == (c >> 6)
    one = lambda m: jnp.where(m, 1.0, 0.0).astype(BF16)
    return dict(
        tril=same & (c <= r), strict=same & (c < r), triu=same & (c >= r), strict_u=same & (c > r),
        tril_b=one(same & (c <= r)), triu_b=one(same & (c >= r)), blk_b=one(same), eye_b=one(r == c),
        eye=jnp.where(r == c, 1.0, 0.0).astype(F32),
        fold_b=one((_iota((GR, CH), 0) & (CH - 1)) == _iota((GR, CH), 1)),
        last=(_iota((GR, 1), 0) & (CH - 1)) == CH - 1,
    )


def _shift_down(x, halo, k):
    if k == 0:
        return x
    xs = pltpu.roll(x, k, 0)
    hs = pltpu.roll(halo, k, 0)
    top = jnp.where(_iota(halo.shape, 0) < k, hs, xs[0:8])
    return jnp.concatenate([top, xs[8:]], axis=0)


def _shift_up(x, halo, k):
    if k == 0:
        return x
    n = x.shape[0]
    xs = pltpu.roll(x, n - k, 0)
    hs = pltpu.roll(halo, 8 - k, 0)
    bot = jnp.where(_iota(halo.shape, 0) >= 8 - k, hs, xs[n - 8:n])
    return jnp.concatenate([xs[0:n - 8], bot], axis=0)


def _conv(x, halo, w):
    y = x * w[CONV_K - 1:CONV_K, :]
    for k in range(1, CONV_K):
        y = y + _shift_down(x, halo, k) * w[CONV_K - 1 - k:CONV_K - k, :]
    return y


def _tri_inv(lmats, eye):
    ps = [-m for m in lmats]
    rs = [eye + p for p in ps]
    for _ in range(5):
        ps = [_dot(p, p) for p in ps]
        rs = [r + _dot(r, p) for r, p in zip(rs, ps)]
    return rs


def _gate_terms(ba_g, bat_g, alog8, dtb8, alog8t, dtb8t, K):
    g8 = -jnp.exp(alog8) * _softplus(ba_g + dtb8)
    g8t = -jnp.exp(alog8t) * _softplus(bat_g + dtb8t)
    gc8 = _ones_dot(K["tril_b"], g8)
    gl8 = _ones_dot(K["blk_b"], g8)
    gcrow8 = _dot_ones_nt(g8t, K["tril_b"])
    return g8, gc8, gl8, gcrow8


def _dn_heads(c_tile, rows, beta8, gc8, gl8, gcrow8, K, pre=None):
    ds = [_dn_head(c_tile, rows, h, beta8, gc8, gl8, gcrow8, K) for h in range(NHD)]
    if pre is None:
        for d, tm in zip(ds, _tri_inv([d["lmat"] for d in ds], K["eye"])):
            d.update(tm=tm, u=_dot(tm, d["vb"]), w=_dot(tm, d["kg"]))
    else:
        for d, (tm, u, w) in zip(ds, pre):
            d.update(tm=tm, u=u, w=w)
    return ds


def _dn_head(c_tile, rows, h, beta8, gc8, gl8, gcrow8, K):
    qr = c_tile[rows, DHD * h:DHD * (h + 1)]
    kr = c_tile[rows, DW + DHD * h:DW + DHD * (h + 1)]
    v = c_tile[rows, 2 * DW + DHD * h:2 * DW + DHD * (h + 1)]
    rq = lax.rsqrt(jnp.sum(qr * qr, axis=1, keepdims=True) + EPS)
    rk = lax.rsqrt(jnp.sum(kr * kr, axis=1, keepdims=True) + EPS)
    qh, kn = qr * rq, kr * rk
    qn = qh * (DHD ** -0.5)
    beta = _col(beta8, h)
    gccol, glcol, gcrow = _col(gc8, NHD + h), _col(gl8, NHD + h), _row(gcrow8, NHD + h)
    diff = gccol - gcrow
    gam_m = jnp.exp(jnp.where(K["tril"], diff, NEG))
    gam = jnp.exp(gccol)
    egl = jnp.exp(glcol - gccol)
    kb, vb = kn * beta, v * beta
    kg = kb * gam
    pl_ = _dot_nt(kb, kn)
    lmat = jnp.where(K["strict"], pl_ * gam_m, 0.0)
    pm = _dot_nt(qn, kn)
    mm = pm * gam_m
    return dict(qr=qr, kr=kr, v=v, rq=rq, rk=rk, qh=qh, qn=qn, kn=kn, beta=beta, diff=diff, gam_m=gam_m, gam=gam,
                egl=egl, el=jnp.exp(glcol), kb=kb, vb=vb, kg=kg, pl=pl_, pm=pm, lmat=lmat, mm=mm,
                qd=qn * gam, kd=kn * egl)


def _halo_prev(width):
    return pl.BlockSpec((8, width), lambda i: (jnp.maximum(i * (TG // 8) - 1, 0), 0))


def _dn_prep(draw, conv_w, ba, bat, alog8, dtb8, alog8t, dtb8t):
    T = draw.shape[0]
    nb = T // TG
    hm = lambda w, dt: jax.ShapeDtypeStruct((NHD, T, w), dt)
    hm_spec = lambda w: pl.BlockSpec((NHD, TG, w), lambda i: (0, i, 0))
    tr = jax.ShapeDtypeStruct((NHD, T // CH, DHD, CH), _MXU)
    tr_spec = pl.BlockSpec((NHD, CPT, DHD, CH), lambda i: (0, i, 0, 0))

    def body(x_ref, halo_ref, cw_ref, ba_ref, bat_ref, al_ref, dt_ref, alt_ref, dtt_ref,
             u_ref, w_ref, qd_ref, kd_ref, mc_ref, mct_ref, tm_ref, kdt_ref, qdt_ref, wt_ref, elb_ref):
        i = pl.program_id(0)
        K = _dn_masks()
        halo = jnp.where(i > 0, halo_ref[...], 0.0)
        cv = _conv(x_ref[...], halo, cw_ref[...])
        c_tile = cv * _sigmoid(cv)
        eye128 = jnp.where(_iota((DHD, DHD), 0) == _iota((DHD, DHD), 1), 1.0, 0.0).astype(_MXU)
        for g in range(NG):
            rows = slice(GR * g, GR * (g + 1))
            ba_g = ba_ref[rows, :]
            _, gc8, gl8, gcrow8 = _gate_terms(ba_g, bat_ref[:, rows], al_ref[...], dt_ref[...], alt_ref[...],
                                              dtt_ref[...], K)
            beta8 = _sigmoid(ba_g)
            for h, d in enumerate(_dn_heads(c_tile, rows, beta8, gc8, gl8, gcrow8, K)):
                gam_t = jnp.exp(jnp.where(K["triu"], -d["diff"], NEG))
                tm_ref[h, rows, :] = d["tm"].astype(_MXU)
                mmt = _dot_nt(d["kn"], d["qn"]) * gam_t
                u_ref[h, rows, :] = d["u"]
                w_ref[h, rows, :] = d["w"].astype(_MXU)
                qd_ref[h, rows, :] = d["qd"].astype(_MXU)
                kd_ref[h, rows, :] = d["kd"].astype(_MXU)
                mc_ref[h, rows, :] = _dot(d["mm"], K["fold_b"]).astype(_MXU)
                mct_ref[h, rows, :] = _dot(mmt, K["fold_b"]).astype(_MXU)
                elb = jnp.broadcast_to(d["el"], (GR, DHD))
                for cc in range(GR // CH):
                    ch = slice(CH * cc, CH * (cc + 1))
                    n = (GR // CH) * g + cc
                    kdt_ref[h, n] = _dot_nt(eye128, d["kd"][ch]).astype(_MXU)
                    qdt_ref[h, n] = _dot_nt(eye128, d["qd"][ch]).astype(_MXU)
                    wt_ref[h, n] = _dot_nt(eye128, d["w"][ch]).astype(_MXU)
                    elb_ref[n:n + 1, DHD * h:DHD * (h + 1)] = elb[CH * cc:CH * cc + 1, :]

    return pl.pallas_call(
        body, name="dn_prep", grid=(nb,),
        in_specs=[_rows(TG, 1536), _halo_prev(1536), _full((CONV_K, 1536)), _rows(TG, 8),
                  pl.BlockSpec((8, TG), lambda i: (0, i)), _full((1, 8)), _full((1, 8)), _full((8, 1)), _full((8, 1))],
        out_specs=[hm_spec(DHD), hm_spec(DHD), hm_spec(DHD), hm_spec(DHD), hm_spec(CH), hm_spec(CH), hm_spec(GR),
                   tr_spec, tr_spec, tr_spec, pl.BlockSpec((CPT, NHD * DHD), lambda i: (i, 0))],
        out_shape=[hm(DHD, F32), hm(DHD, _MXU), hm(DHD, _MXU), hm(DHD, _MXU), hm(CH, _MXU), hm(CH, _MXU), hm(GR, _MXU),
                   tr, tr, tr, jax.ShapeDtypeStruct((T // CH, NHD * DHD), F32)],
        compiler_params=_cp(("arbitrary",)),
    )(draw, draw, conv_w, ba, bat, alog8, dtb8, alog8t, dtb8t)


def _dn_scan(u, w, qd, mc, kdt, elb):
    T = u.shape[1]
    nb = T // TG
    hm_spec = lambda wd: pl.BlockSpec((NHD, TG, wd), lambda i: (0, i, 0))

    def body(u_ref, w_ref, qd_ref, mc_ref, kdt_ref, elb_ref, o_ref, vn_ref, sn_ref, S):
        @pl.when(pl.program_id(0) == 0)
        def _():
            S[...] = jnp.zeros_like(S)

        sub8 = _iota((CPT, DHD), 0)

        def chunk(cc, carry):
            r0 = pl.multiple_of(cc * CH, CH)
            rs = pl.ds(r0, CH)
            for h in range(NHD):
                sh = S[h]
                sb = sh.astype(_MXU)
                sn_ref[cc, h] = sb
                vn = u_ref[h, rs, :] - _dot(w_ref[h, rs, :], sb)
                vnb = vn.astype(_MXU)
                o_ref[h, rs, :] = _dot(qd_ref[h, rs, :], sb) + _dot(mc_ref[h, rs, :], vnb)
                vn_ref[h, rs, :] = vnb
                el = jnp.sum(jnp.where(sub8 == cc, elb_ref[:, DHD * h:DHD * (h + 1)], 0.0), axis=0, keepdims=True)
                S[h] = sh * el + _dot(kdt_ref[h, cc], vnb)
            return carry

        lax.fori_loop(0, CPT, chunk, 0)

    return pl.pallas_call(
        body, name="dn_scan", grid=(nb,),
        in_specs=[hm_spec(DHD), hm_spec(DHD), hm_spec(DHD), hm_spec(CH),
                  pl.BlockSpec((NHD, CPT, DHD, CH), lambda i: (0, i, 0, 0)),
                  pl.BlockSpec((CPT, NHD * DHD), lambda i: (i, 0))],
        out_specs=[hm_spec(DHD), hm_spec(DHD), pl.BlockSpec((CPT, NHD, DHD, DHD), lambda i: (i, 0, 0, 0))],
        out_shape=[jax.ShapeDtypeStruct((NHD, T, DHD), F32), jax.ShapeDtypeStruct((NHD, T, DHD), _MXU),
                   jax.ShapeDtypeStruct((T // CH, NHD, DHD, DHD), _MXU)],
        scratch_shapes=[pltpu.VMEM((NHD, DHD, DHD), F32)],
        compiler_params=_cp(("arbitrary",)),
    )(u, w, qd, mc, kdt, elb)


def _dn_scan_bwd(do, mct, kd, qdt, wt, sn, vn, elb):
    T = do.shape[1]
    nb = T // TG
    rev = lambda wd: pl.BlockSpec((NHD, TG, wd), lambda i: (0, nb - 1 - i, 0))
    rev_t = pl.BlockSpec((NHD, CPT, DHD, CH), lambda i: (0, nb - 1 - i, 0, 0))

    def body(do_ref, mct_ref, kd_ref, qdt_ref, wt_ref, sn_ref, vn_ref, elb_ref,
             du_ref, dw_ref, dqd_ref, dkd_ref, dgx_ref, dS):
        @pl.when(pl.program_id(0) == 0)
        def _():
            dS[...] = jnp.zeros_like(dS)

        last_row = _iota((CH, DHD), 0) == CH - 1
        sub8 = _iota((CPT, DHD), 0)

        def chunk(k, carry):
            cc = CPT - 1 - k
            r0 = pl.multiple_of(cc * CH, CH)
            rs = pl.ds(r0, CH)
            for h in range(NHD):
                dsh = dS[h]
                dsb = dsh.astype(_MXU)
                sn = sn_ref[cc, h]
                doc = do_ref[h, rs, :].astype(_MXU)
                dvn = _dot(mct_ref[h, rs, :], doc) + _dot(kd_ref[h, rs, :], dsb)
                dvnb = dvn.astype(_MXU)
                du_ref[h, rs, :] = dvn
                dqd_ref[h, rs, :] = _dot_nt(doc, sn)
                dkd_ref[h, rs, :] = _dot_nt(vn_ref[h, rs, :], dsb)
                dw_ref[h, rs, :] = -_dot_nt(dvnb, sn)
                el = jnp.sum(jnp.where(sub8 == cc, elb_ref[:, DHD * h:DHD * (h + 1)], 0.0), axis=0, keepdims=True)
                part = jnp.sum(dsh * sn.astype(F32), axis=0, keepdims=True) * el
                dgx_ref[h, rs, :] = jnp.where(last_row, part, 0.0)
                dS[h] = _dot(qdt_ref[h, cc], doc) + dsh * el - _dot(wt_ref[h, cc], dvnb)
            return carry

        lax.fori_loop(0, CPT, chunk, 0)

    o = jax.ShapeDtypeStruct((NHD, T, DHD), F32)
    return pl.pallas_call(
        body, name="dn_scan_bwd", grid=(nb,),
        in_specs=[rev(DHD), rev(CH), rev(DHD), rev_t, rev_t,
                  pl.BlockSpec((CPT, NHD, DHD, DHD), lambda i: (nb - 1 - i, 0, 0, 0)), rev(DHD),
                  pl.BlockSpec((CPT, NHD * DHD), lambda i: (nb - 1 - i, 0))],
        out_specs=[rev(DHD)] * 5,
        out_shape=[o] * 5,
        scratch_shapes=[pltpu.VMEM((NHD, DHD, DHD), F32)],
        compiler_params=_cp(("arbitrary",)),
    )(do, mct, kd, qdt, wt, sn, vn, elb)


def _put_col(acc, k, col):
    return jnp.where(_iota(acc.shape, 1) == k, col, acc)


def _dn_post_bwd(draw, conv_w, ba, bat, alog8, dtb8, alog8t, dtb8t, du, dw, dqd, dkd, dgx, do, vn, tm, u, w):
    T = draw.shape[0]
    nb = T // TG
    hm_spec = lambda wd: pl.BlockSpec((NHD, TG, wd), lambda i: (0, i, 0))

    def body(x_ref, halo_ref, cw_ref, ba_ref, bat_ref, al_ref, dt_ref, alt_ref, dtt_ref,
             du_ref, dw_ref, dqd_ref, dkd_ref, dgx_ref, do_ref, vn_ref, tm_ref, u_ref, w_ref, dc_ref, dba_ref, sm_ref):
        i = pl.program_id(0)

        @pl.when(i == 0)
        def _():
            sm_ref[...] = jnp.zeros_like(sm_ref)

        K = _dn_masks()
        halo = jnp.where(i > 0, halo_ref[...], 0.0)
        cv = _conv(x_ref[...], halo, cw_ref[...])
        sg = _sigmoid(cv)
        c_tile = cv * sg
        dsilu = sg * (1.0 + cv * (1.0 - sg))
        for g in range(NG):
            rows = slice(GR * g, GR * (g + 1))
            ba_g = ba_ref[rows, :]
            g8, gc8, gl8, gcrow8 = _gate_terms(ba_g, bat_ref[:, rows], al_ref[...], dt_ref[...], alt_ref[...],
                                               dtt_ref[...], K)
            beta8 = _sigmoid(ba_g)
            dgc8 = jnp.zeros((GR, 8), F32)
            rd8 = jnp.zeros((GR, 8), F32)
            dbeta8 = jnp.zeros((GR, 8), F32)
            pre = [(tm_ref[h, rows, :], u_ref[h, rows, :], w_ref[h, rows, :]) for h in range(NHD)]
            for h, d in enumerate(_dn_heads(c_tile, rows, beta8, gc8, gl8, gcrow8, K, pre)):
                gam_m, gam, egl = d["gam_m"], d["gam"], d["egl"]
                gam_t = jnp.exp(jnp.where(K["triu"], -d["diff"], NEG))
                qn, kn, kb, vv = d["qn"], d["kn"], d["kb"], d["v"]
                duh, dwh = du_ref[h, rows, :], dw_ref[h, rows, :]
                dqdh, dkdh = dqd_ref[h, rows, :], dkd_ref[h, rows, :]
                doh, vnh = do_ref[h, rows, :], vn_ref[h, rows, :]
                tt = _dot_nt(K["eye_b"].astype(_MXU), d["tm"])
                dvb = _dot(tt, duh)
                dkg = _dot(tt, dwh)
                da = -(_dot_nt(dvb, d["u"]) + _dot_nt(dkg, d["w"]))
                dat = -(_dot_nt(d["u"], dvb) + _dot_nt(d["w"], dkg))
                dpl = jnp.where(K["strict"], da, 0.0) * gam_m
                dplt = jnp.where(K["strict_u"], dat, 0.0) * gam_t
                dpm = jnp.where(K["tril"], _dot_nt(doh, vnh), 0.0) * gam_m
                dpmt = jnp.where(K["triu"], _dot_nt(vnh, doh), 0.0) * gam_t
                plt = _dot_nt(kn, kb)
                pmt = _dot_nt(kn, qn)
                dkb = _dot(dpl, kn) + dkg * gam
                dkn = _dot(dplt, kb) + _dot(dpmt, qn) + dkdh * egl + dkb * d["beta"]
                dqn = _dot(dpm, kn) + dqdh * gam
                gsum = jnp.sum(dpl * d["pl"] + dpm * d["pm"], axis=1, keepdims=True)
                gsum_t = jnp.sum(dplt * plt + dpmt * pmt, axis=1, keepdims=True)
                rd = jnp.sum(dkdh * d["kd"], axis=1, keepdims=True)
                dgc = (gsum - gsum_t + jnp.sum(dqdh * d["qd"], axis=1, keepdims=True)
                       + jnp.sum(dkg * d["kg"], axis=1, keepdims=True) - rd
                       + jnp.sum(dgx_ref[h, rows, :], axis=1, keepdims=True))
                dgc8 = _put_col(dgc8, NHD + h, dgc)
                rd8 = _put_col(rd8, NHD + h, rd)
                dbeta = jnp.sum(dkb * kn, axis=1, keepdims=True) + jnp.sum(dvb * vv, axis=1, keepdims=True)
                dbeta8 = _put_col(dbeta8, h, dbeta)
                dqh = dqn * (DHD ** -0.5)
                qh = d["qh"]
                dqr = d["rq"] * (dqh - qh * jnp.sum(dqh * qh, axis=1, keepdims=True))
                dkr = d["rk"] * (dkn - kn * jnp.sum(dkn * kn, axis=1, keepdims=True))
                cq = slice(DHD * h, DHD * (h + 1))
                ck = slice(DW + DHD * h, DW + DHD * (h + 1))
                cvv = slice(2 * DW + DHD * h, 2 * DW + DHD * (h + 1))
                dc_ref[rows, cq] = dqr * dsilu[rows, cq]
                dc_ref[rows, ck] = dkr * dsilu[rows, ck]
                dc_ref[rows, cvv] = dvb * d["beta"] * dsilu[rows, cvv]
            dgc8 = dgc8 + jnp.where(K["last"], _ones_dot(K["blk_b"], rd8), 0.0)
            dg8 = _ones_dot(K["triu_b"], dgc8)
            sgm = _sigmoid(ba_g + dt_ref[...])
            dalpha = dg8 * (-jnp.exp(al_ref[...])) * sgm
            lane8 = _iota((GR, 8), 1)
            dba_ref[rows, :] = jnp.where(lane8 < NHD, dbeta8 * beta8 * (1.0 - beta8), dalpha)
            valid = lane8 >= NHD
            sm_ref[0:1, 0:8] += jnp.sum(jnp.where(valid, dg8 * g8, 0.0), axis=0, keepdims=True)
            sm_ref[1:2, 0:8] += jnp.sum(jnp.where(valid, dalpha, 0.0), axis=0, keepdims=True)

    return pl.pallas_call(
        body, name="dn_post_bwd", grid=(nb,),
        in_specs=[_rows(TG, 1536), _halo_prev(1536), _full((CONV_K, 1536)), _rows(TG, 8),
                  pl.BlockSpec((8, TG), lambda i: (0, i)), _full((1, 8)), _full((1, 8)), _full((8, 1)), _full((8, 1)),
                  hm_spec(DHD), hm_spec(DHD), hm_spec(DHD), hm_spec(DHD), hm_spec(DHD), hm_spec(DHD), hm_spec(DHD),
                  hm_spec(GR), hm_spec(DHD), hm_spec(DHD)],
        out_specs=[_rows(TG, 1536), _rows(TG, 8), _full((8, LANES))],
        out_shape=[jax.ShapeDtypeStruct((T, 1536), F32), jax.ShapeDtypeStruct((T, 8), F32),
                   jax.ShapeDtypeStruct((8, LANES), F32)],
        compiler_params=_cp(("arbitrary",)),
    )(draw, draw, conv_w, ba, bat, alog8, dtb8, alog8t, dtb8t, du, dw, dqd, dkd, dgx, do, vn, tm, u, w)


def _conv_bwd(dc, draw, conv_w):
    T = dc.shape[0]
    nb = T // TG

    def body(dc_ref, nxt_ref, x_ref, halo_ref, cw_ref, dx_ref, dcw_ref):
        i = pl.program_id(0)

        @pl.when(i == 0)
        def _():
            dcw_ref[...] = jnp.zeros_like(dcw_ref)

        dcv = dc_ref[...]
        nxt = jnp.where(i < nb - 1, nxt_ref[...], 0.0)
        halo = jnp.where(i > 0, halo_ref[...], 0.0)
        xv = x_ref[...]
        w = cw_ref[...]
        dx = dcv * w[CONV_K - 1:CONV_K, :]
        dcw_ref[CONV_K - 1:CONV_K, :] += jnp.sum(dcv * xv, axis=0, keepdims=True)
        for k in range(1, CONV_K):
            j = CONV_K - 1 - k
            dx = dx + _shift_up(dcv, nxt, k) * w[j:j + 1, :]
            dcw_ref[j:j + 1, :] += jnp.sum(dcv * _shift_down(xv, halo, k), axis=0, keepdims=True)
        dx_ref[...] = dx

    return pl.pallas_call(
        body, name="conv_bwd", grid=(nb,),
        in_specs=[_rows(TG, 1536),
                  pl.BlockSpec((8, 1536), lambda i: (jnp.minimum((i + 1) * (TG // 8), T // 8 - 1), 0)),
                  _rows(TG, 1536), _halo_prev(1536), _full((CONV_K, 1536))],
        out_specs=[_rows(TG, 1536), _full((8, 1536))],
        out_shape=[jax.ShapeDtypeStruct((T, 1536), F32), jax.ShapeDtypeStruct((8, 1536), F32)],
        compiler_params=_cp(("arbitrary",)),
    )(dc, dc, draw, draw, conv_w)


def _rms(x):
    return lax.rsqrt(jnp.mean(x * x, axis=1, keepdims=True) + EPS)


def _rms_bwd(dy, xh, r, g):
    dxh = dy * g
    return r * (dxh - xh * jnp.mean(dxh * xh, axis=1, keepdims=True))


def _hm_rows(tm):
    return pl.BlockSpec((NHD, tm, DHD), lambda i: (0, i, 0))


def _post_mix(apre, o, z, x, w_out, g_a, g_dn):
    T = x.shape[0]

    def body(ap_ref, o_ref, z_ref, x_ref, w_ref, ga_ref, gd_ref, x1_ref, mix_ref):
        ap = ap_ref[...]
        parts = [ap * _rms(ap) * ga_ref[...]]
        zz = z_ref[...]
        for h in range(NHD):
            oh = o_ref[h]
            zh = zz[:, DHD * h:DHD * (h + 1)]
            parts.append(oh * _rms(oh) * gd_ref[...] * (zh * _sigmoid(zh)))
        mix = jnp.concatenate(parts, axis=1).astype(_MXU)
        mix_ref[...] = mix
        x1_ref[...] = x_ref[...] + jnp.dot(mix, w_ref[...], preferred_element_type=F32)

    return pl.pallas_call(
        body, name="post_mix", grid=(T // TM,),
        in_specs=[_rows(TM, AW), _hm_rows(TM), _rows(TM, DW), _rows(TM, D), _full((D, D)), _full((1, AW)),
                  _full((1, DHD))],
        out_specs=[_rows(TM, D), _rows(TM, D)],
        out_shape=[jax.ShapeDtypeStruct((T, D), F32), jax.ShapeDtypeStruct((T, D), _MXU)],
        compiler_params=_cp(("arbitrary",)),
    )(apre, o, z, x, w_out, g_a, g_dn)


def _ffn(x1, tgt, w_gate, w_up, w_down, g_ffn):
    T = x1.shape[0]

    def body(x_ref, t_ref, wg_hbm, wu_hbm, wd_hbm, g_ref,
             dx1_ref, dx1b_ref, h2_ref, act_ref, dgu_ref, dyb_ref, loss_ref, dg_ref, wg, wu, wd, sem):
        @pl.when(pl.program_id(0) == 0)
        def _():
            cps = [pltpu.make_async_copy(s, d, sem.at[k]) for k, (s, d) in
                   enumerate(((wg_hbm, wg), (wu_hbm, wu), (wd_hbm, wd)))]
            for cp in cps:
                cp.start()
            for cp in cps:
                cp.wait()
            loss_ref[...] = jnp.zeros_like(loss_ref)
            dg_ref[...] = jnp.zeros_like(dg_ref)

        xv = x_ref[...]
        r = _rms(xv)
        xh = xv * r
        gg = g_ref[...]
        h2 = (xh * gg).astype(_MXU)
        h2_ref[...] = h2
        gate = jnp.dot(h2, wg[...], preferred_element_type=F32)
        up = jnp.dot(h2, wu[...], preferred_element_type=F32)
        sg = _sigmoid(gate)
        silu = gate * sg
        act = (silu * up).astype(_MXU)
        act_ref[...] = act
        y = xv + jnp.dot(act, wd[...], preferred_element_type=F32)
        err = y - t_ref[...]
        loss_ref[...] += jnp.sum(err * err, axis=0, keepdims=True)
        dy = err * (1.0 / D)
        dyb = dy.astype(_MXU)
        dyb_ref[...] = dyb
        dact = lax.dot_general(dyb, wd[...], (((1,), (1,)), ((), ())), preferred_element_type=F32)
        dgate = (dact * up * (sg * (1.0 + gate * (1.0 - sg)))).astype(_MXU)
        dup = (dact * silu).astype(_MXU)
        dgu_ref[:, 0:FF] = dgate
        dgu_ref[:, FF:2 * FF] = dup
        nt = (((1,), (1,)), ((), ()))
        dh2 = (lax.dot_general(dgate, wg[...], nt, preferred_element_type=F32)
               + lax.dot_general(dup, wu[...], nt, preferred_element_type=F32))
        dg_ref[...] += jnp.sum(dh2 * xh, axis=0, keepdims=True)
        dx1 = dy + _rms_bwd(dh2, xh, r, gg)
        dx1_ref[...] = dx1
        dx1b_ref[...] = dx1.astype(_MXU)

    anyspec = pl.BlockSpec(memory_space=pl.ANY)
    sd = lambda w, dt: jax.ShapeDtypeStruct((T, w), dt)
    return pl.pallas_call(
        body, name="ffn", grid=(T // TM,),
        in_specs=[_rows(TM, D), _rows(TM, D), anyspec, anyspec, anyspec, _full((1, D))],
        out_specs=[_rows(TM, D), _rows(TM, D), _rows(TM, D), _rows(TM, FF), _rows(TM, 2 * FF), _rows(TM, D),
                   _full((1, D)), _full((1, D))],
        out_shape=[sd(D, F32), sd(D, _MXU), sd(D, _MXU), sd(FF, _MXU), sd(2 * FF, _MXU), sd(D, _MXU),
                   jax.ShapeDtypeStruct((1, D), F32), jax.ShapeDtypeStruct((1, D), F32)],
        scratch_shapes=[pltpu.VMEM((D, FF), _MXU), pltpu.VMEM((D, FF), _MXU), pltpu.VMEM((FF, D), _MXU),
                        pltpu.SemaphoreType.DMA((3,))],
        compiler_params=_cp(("arbitrary",)),
    )(x1, tgt, w_gate, w_up, w_down, g_ffn)


def _mix_bwd(dx1b, w_out, apre, o, z, g_a, g_dn):
    T = dx1b.shape[0]

    def body(dx_ref, w_ref, ap_ref, o_ref, z_ref, ga_ref, gd_ref, dap_ref, do_ref, dz_ref, dga_ref, dgd_ref):
        @pl.when(pl.program_id(0) == 0)
        def _():
            dga_ref[...] = jnp.zeros_like(dga_ref)
            dgd_ref[...] = jnp.zeros_like(dgd_ref)

        dmix = lax.dot_general(dx_ref[...], w_ref[...], (((1,), (1,)), ((), ())), preferred_element_type=F32)
        ap = ap_ref[...]
        ra = _rms(ap)
        ah = ap * ra
        da = dmix[:, 0:AW]
        dga_ref[...] += jnp.sum(da * ah, axis=0, keepdims=True)
        dap_ref[...] = _rms_bwd(da, ah, ra, ga_ref[...])
        zz = z_ref[...]
        gd = gd_ref[...]
        for h in range(NHD):
            cs = slice(DHD * h, DHD * (h + 1))
            dd = dmix[:, AW + DHD * h:AW + DHD * (h + 1)]
            oh = o_ref[h]
            ro = _rms(oh)
            ohh = oh * ro
            zh = zz[:, cs]
            sz = _sigmoid(zh)
            dz_ref[:, cs] = dd * (ohh * gd) * (sz * (1.0 + zh * (1.0 - sz)))
            don = dd * (zh * sz)
            dgd_ref[...] += jnp.sum(don * ohh, axis=0, keepdims=True)
            do_ref[h] = _rms_bwd(don, ohh, ro, gd)

    return pl.pallas_call(
        body, name="mix_bwd", grid=(T // TM,),
        in_specs=[_rows(TM, D), _full((D, D)), _rows(TM, AW), _hm_rows(TM), _rows(TM, DW), _full((1, AW)),
                  _full((1, DHD))],
        out_specs=[_rows(TM, AW), _hm_rows(TM), _rows(TM, DW), _full((1, AW)), _full((1, DHD))],
        out_shape=[jax.ShapeDtypeStruct((T, AW), F32), jax.ShapeDtypeStruct((NHD, T, DHD), F32),
                   jax.ShapeDtypeStruct((T, DW), F32), jax.ShapeDtypeStruct((1, AW), F32),
                   jax.ShapeDtypeStruct((1, DHD), F32)],
        compiler_params=_cp(("arbitrary",)),
    )(dx1b, w_out, apre, o, z, g_a, g_dn)


def _inproj_bwd(dqn, dkn, dv, araw, ddraw, dz, dba, x, dx1, w_in, g_mix, qg_t, kg_t):
    T = x.shape[0]

    def body(dqn_ref, dkn_ref, dv_ref, ar_ref, dd_ref, dz_ref, dba_ref, x_ref, dx1_ref, w_ref, g_ref, qg_ref, kg_ref,
             dx_ref, dp_ref, dgm_ref, dqg_ref, dkg_ref):
        @pl.when(pl.program_id(0) == 0)
        def _():
            dgm_ref[...] = jnp.zeros_like(dgm_ref)
            dqg_ref[...] = jnp.zeros_like(dqg_ref)
            dkg_ref[...] = jnp.zeros_like(dkg_ref)

        bd = _block_ones(AW // 2, DHA)

        def head_norm_bwd(raw, dyn, gain, dg_ref):
            r = _head_rms(raw, bd, DHA)
            xh = raw * r
            dg_ref[...] += jnp.sum(dyn * xh, axis=0, keepdims=True)
            dxh = dyn * gain
            return r * (dxh - xh * (_head_sum(dxh * xh, bd) * (1.0 / DHA)))

        ar = ar_ref[...]
        dq = head_norm_bwd(ar[:, 0:AW], dqn_ref[...] * (DHA ** -0.5), qg_ref[...], dqg_ref)
        dk = head_norm_bwd(ar[:, AW:2 * AW], dkn_ref[...], kg_ref[...], dkg_ref)
        nt = (((1,), (1,)), ((), ()))
        dh = jnp.zeros((TM, D), F32)
        for lo, val in ((0, dq), (AW, dk), (2 * AW, dv_ref[...]), (1536, dd_ref[...]), (3072, dz_ref[...]),
                        (3584, dba_ref[...])):
            vb = val.astype(_MXU)
            wd_ = val.shape[1]
            dp_ref[:, lo:lo + wd_] = vb
            dh = dh + lax.dot_general(vb, w_ref[:, lo:lo + wd_], nt, preferred_element_type=F32)
        xv = x_ref[...]
        r = _rms(xv)
        xh = xv * r
        dgm_ref[...] += jnp.sum(dh * xh, axis=0, keepdims=True)
        dx_ref[...] = dx1_ref[...] + _rms_bwd(dh, xh, r, g_ref[...])

    return pl.pallas_call(
        body, name="inproj_bwd", grid=(T // TM,),
        in_specs=[_rows(TM, AW), _rows(TM, AW), _rows(TM, AW), _rows(TM, 1536), _rows(TM, 1536), _rows(TM, DW),
                  _rows(TM, 8), _rows(TM, D), _rows(TM, D), _full((D, 3592)), _full((1, D)), _full((1, AW)),
                  _full((1, AW))],
        out_specs=[_rows(TM, D), _rows(TM, 3592), _full((1, D)), _full((1, AW)), _full((1, AW))],
        out_shape=[jax.ShapeDtypeStruct((T, D), F32), jax.ShapeDtypeStruct((T, 3592), _MXU),
                   jax.ShapeDtypeStruct((1, D), F32), jax.ShapeDtypeStruct((1, AW), F32),
                   jax.ShapeDtypeStruct((1, AW), F32)],
        compiler_params=_cp(("arbitrary",)),
    )(dqn, dkn, dv, araw, ddraw, dz, dba, x, dx1, w_in, g_mix, qg_t, kg_t)


def _wgrad(a, b, name, tk=512, tn=None):
    T, M = a.shape
    N = b.shape[1]
    tn = N if tn is None else tn
    nk = T // tk

    def body(a_ref, b_ref, o_ref):
        @pl.when(pl.program_id(1) == 0)
        def _():
            o_ref[...] = jnp.zeros_like(o_ref)

        o_ref[...] += lax.dot_general(a_ref[...], b_ref[...], (((0,), (0,)), ((), ())), preferred_element_type=F32)

    return pl.pallas_call(
        body, name=name, grid=(N // tn, nk),
        in_specs=[pl.BlockSpec((tk, M), lambda j, k: (k, 0)), pl.BlockSpec((tk, tn), lambda j, k: (k, j))],
        out_specs=pl.BlockSpec((M, tn), lambda j, k: (0, j)),
        out_shape=jax.ShapeDtypeStruct((M, N), F32),
        compiler_params=_cp(("arbitrary", "arbitrary")),
    )(a, b)


def _adamw(parts, w, m, v, name, tr):
    K, R, W = parts.shape

    def body(p_ref, w_ref, m_ref, v_ref, g_ref, d_ref, nm_ref, nv_ref):
        g = p_ref[0].astype(F32)
        for k in range(1, K):
            g = g + p_ref[k].astype(F32)
        g_ref[...] = g
        nm = ADAM_B1 * m_ref[...] + (1.0 - ADAM_B1) * g
        nv = ADAM_B2 * v_ref[...] + (1.0 - ADAM_B2) * (g * g)
        nm_ref[...] = nm
        nv_ref[...] = nv
        m_hat = nm / (1.0 - ADAM_B1 ** ADAM_STEP)
        v_hat = nv / (1.0 - ADAM_B2 ** ADAM_STEP)
        d_ref[...] = -ADAM_LR * (m_hat / (jnp.sqrt(v_hat) + ADAM_EPS) + ADAM_WD * w_ref[...])

    o = jax.ShapeDtypeStruct((R, W), F32)
    return pl.pallas_call(
        body, name=name, grid=(R // tr,),
        in_specs=[pl.BlockSpec((K, tr, W), lambda i: (0, i, 0)), _rows(tr, W), _rows(tr, W), _rows(tr, W)],
        out_specs=[_rows(tr, W)] * 4,
        out_shape=[o] * 4,
        compiler_params=_cp(("arbitrary",)),
    )(parts, w, m, v)


SM_ROWS = 136
R_GMIX, R_GFFN, R_QG, R_KG, R_GA, R_GDN, R_ALOG, R_DT, R_LOSS, R_CONV, R_REL = 0, 8, 16, 24, 32, 40, 48, 49, 56, 64, 112


def _small_reduce(gathered):
    def body(p_ref, o_ref):
        s = p_ref[0]
        for k in range(1, N_DEV):
            s = s + p_ref[k]
        o_ref[...] = s
        for r0 in (R_QG, R_KG):
            rs = jnp.sum(s[r0:r0 + 4], axis=0, keepdims=True)
            o_ref[r0:r0 + 1, :] = rs + pltpu.roll(rs, DHA, 1)
        tot = jnp.sum(jnp.sum(s[R_LOSS:R_LOSS + 8], axis=0, keepdims=True), axis=1, keepdims=True)
        o_ref[R_LOSS:R_LOSS + 1, :] = jnp.broadcast_to(tot * (0.5 / D), (1, LANES))

    return pl.pallas_call(
        body, name="small_reduce",
        out_shape=jax.ShapeDtypeStruct((SM_ROWS, LANES), F32),
    )(gathered)


_WIRE = jnp.bfloat16
RA_USED, RA = 449, 464
RL = 128 + 3 * 352


def _pack_rows(parts, rows=None):
    p = jnp.concatenate([t.reshape(-1, D) for t in parts], axis=0) if len(parts) > 1 else parts[0].reshape(-1, D)
    return p if rows is None else jnp.pad(p, ((0, rows - p.shape[0]), (0, 0)))


def _unpack_rows(packed, shapes):
    out, r = [], 0
    for shp in shapes:
        nr = math.prod(shp) // D
        out.append(packed[r:r + nr].reshape(shp))
        r += nr
    return out


def _pad8(t):
    return jnp.pad(t, ((0, (-t.shape[0]) % 8), (0, 0)))


def _pack_lanes(parts):
    rows = []
    for p in parts:
        f = p.reshape(-1)
        pad = (-f.shape[0]) % LANES
        rows.append(jnp.pad(f, (0, pad)).reshape(-1, LANES))
    return jnp.concatenate(rows, axis=0)


def _unpack_lanes(packed, shapes):
    out, r = [], 0
    for shp in shapes:
        n = math.prod(shp)
        nr = -(-n // LANES)
        out.append(packed[r:r + nr].reshape(-1)[:n].reshape(shp))
        r += nr
    return out


def kernel(x, norm_mix_g, w_in, attn_q_norm_g, attn_k_norm_g, rel_bias, attn_out_norm_g, conv_w, a_log, dt_bias, dn_out_norm_g, w_out, norm_ffn_g, w_gate, w_up, w_down, loss_target, m_norm_mix_g, m_w_in, m_attn_q_norm_g, m_attn_k_norm_g, m_rel_bias, m_attn_out_norm_g, m_conv_w, m_a_log, m_dt_bias, m_dn_out_norm_g, m_w_out, m_norm_ffn_g, m_w_gate, m_w_up, m_w_down, v_norm_mix_g, v_w_in, v_attn_q_norm_g, v_attn_k_norm_g, v_rel_bias, v_attn_out_norm_g, v_conv_w, v_a_log, v_dt_bias, v_dn_out_norm_g, v_w_out, v_norm_ffn_g, v_w_gate, v_w_up, v_w_down):
    xs, tgt = x[0], loss_target[0]
    T = xs.shape[0]
    my_idx = 4 * lax.axis_index("x") + 2 * lax.axis_index("y") + lax.axis_index("c")
    late_w = (w_out[0], w_gate[0], w_up[0], w_down[0])
    late_shapes = [w.shape for w in late_w]

    wa_all = _all_gather(_pack_rows([w_in[0].astype(_MXU)], RA), "gather_w_in")
    cw_all = _all_gather(jnp.pad(conv_w[0], ((0, 4), (0, 64))), "gather_conv")
    by_dev = lambda a, k: a.reshape(N_DEV, D, k).transpose(1, 0, 2).reshape(D, N_DEV * k)
    W_in = by_dev(wa_all[:, 0:RA_USED], RA_USED)
    conv_full = cw_all[:, 0:CONV_K, 0:192].transpose(1, 0, 2).reshape(CONV_K, 1536)

    qg_t = jnp.tile(attn_q_norm_g, (1, NHA))
    kg_t = jnp.tile(attn_k_norm_g, (1, NHA))
    z4 = jnp.zeros((1, NHD), F32)
    alog8 = jnp.concatenate([z4, a_log], axis=1)
    dtb8 = jnp.concatenate([z4, dt_bias], axis=1)

    araw, an, draw, z, ba, hb, wl_all = _inproj(xs, norm_mix_g, W_in, qg_t, kg_t,
                                                _pack_rows([w.astype(_MXU) for w in late_w]))
    W_out = wl_all[:, 0:128].reshape(D, D)
    W_gate = by_dev(wl_all[:, 128:480], 352)
    W_up = by_dev(wl_all[:, 480:832], 352)
    W_down = wl_all[:, 832:RL].reshape(FF, D)
    tab, tabt = _bias_tables(rel_bias[0].T)
    apre = _attn_fwd(an, tab)
    bat = ba.T
    dn_args = (draw, conv_full, ba, bat, alog8, dtb8, alog8.T, dtb8.T)
    u, w, qd, kd, mc, mct, tm, kdt, qdt, wt, elb = _dn_prep(*dn_args)
    o, vn, sn = _dn_scan(u, w, qd, mc, kdt, elb)
    x1, mix = _post_mix(apre, o, z, xs, W_out, attn_out_norm_g, dn_out_norm_g)

    dx1, dx1b, h2, act, dgu, dyb, loss_row, dgffn = _ffn(x1, tgt, W_gate, W_up, W_down, norm_ffn_g)

    by_cols = lambda g, k: g.reshape(D, N_DEV, k).transpose(1, 0, 2).reshape(N_DEV, -1, D)
    gW_out = _wgrad(mix, dx1b, "wgrad_out")
    gW_gu = _wgrad(h2, dgu, "wgrad_gate_up", tn=FF)
    gW_down_t = _wgrad(dyb, act, "wgrad_down")
    send_late = jnp.concatenate(
        [gW_out.reshape(N_DEV, 128, D), by_cols(gW_gu[:, 0:FF], 352), by_cols(gW_gu[:, FF:], 352),
         gW_down_t.T.reshape(N_DEV, 352, D)], axis=1).astype(_WIRE)

    dap, do, dz, dga, dgdn = _mix_bwd(dx1b, W_out, apre, o, z, attn_out_norm_g, dn_out_norm_g)
    dqn, dkn, dv, dtabt, recv_late = _attn_bwd(an, dap, tabt, send_late)
    drel = _bias_grad(dtabt)
    du, dw, dqd, dkd, dgx = _dn_scan_bwd(do, mct, kd, qdt, wt, sn, vn, elb)
    dc, dba, sm = _dn_post_bwd(*dn_args, du, dw, dqd, dkd, dgx, do, vn, tm, u, w)
    ddraw, dcw = _conv_bwd(dc, draw, conv_full)
    gx, dproj, dgmix, dqg, dkg = _inproj_bwd(dqn, dkn, dv, araw, ddraw, dz, dba, xs, dx1, W_in, norm_mix_g, qg_t, kg_t)

    gW_in = _wgrad(hb, dproj, "wgrad_in", tk=256)
    send_in = jnp.pad(by_cols(gW_in, RA_USED).astype(_WIRE), ((0, 0), (0, RA - RA_USED), (0, 0)))
    recv_in = _all_to_all(send_in, "scatter_w_in")
    late_m = (m_w_out[0], m_w_gate[0], m_w_up[0], m_w_down[0])
    late_v = (v_w_out[0], v_w_gate[0], v_w_up[0], v_w_down[0])
    outs_late = _adamw(recv_late, _pack_rows(late_w), _pack_rows(late_m), _pack_rows(late_v), "adamw_late", 32)
    outs_in = _adamw(recv_in, _pack_rows([w_in[0]], RA), _pack_rows([m_w_in[0]], RA), _pack_rows([v_w_in[0]], RA),
                     "adamw_w_in", 16)
    big = [[_unpack_rows(a, [w_in[0].shape])[0]] + _unpack_rows(b, late_shapes) for a, b in zip(outs_in, outs_late)]
    bg, bd_, bm, bv = big

    partial = jnp.concatenate(
        [dgmix.reshape(8, LANES), dgffn.reshape(8, LANES), _pad8(dqg.reshape(4, LANES)), _pad8(dkg.reshape(4, LANES)),
         _pad8(dga.reshape(4, LANES)), _pad8(dgdn), sm, loss_row.reshape(8, LANES),
         dcw[0:CONV_K].reshape(48, LANES), drel.reshape(24, LANES)], axis=0)
    S = _small_reduce(_all_gather(partial, "gather_small"))
    loss = S[R_LOSS, 0]
    g_conv = lax.dynamic_slice(S[R_CONV:R_CONV + 48].reshape(CONV_K, 1536), (0, 192 * my_idx), (CONV_K, 192))
    sg = [S[R_GMIX:R_GMIX + 8].reshape(1, D), S[R_QG:R_QG + 1, 0:DHA], S[R_KG:R_KG + 1, 0:DHA],
          S[R_REL:R_REL + 24].reshape(NHA, 384)[:, 0:257].T, S[R_GA:R_GA + 4].reshape(1, AW), g_conv,
          S[R_ALOG:R_ALOG + 1, NHD:2 * NHD], S[R_DT:R_DT + 1, NHD:2 * NHD], S[R_GDN:R_GDN + 1], S[R_GFFN:R_GFFN + 8].reshape(1, D)]
    sw = [norm_mix_g, attn_q_norm_g, attn_k_norm_g, rel_bias[0], attn_out_norm_g, conv_w[0], a_log, dt_bias, dn_out_norm_g, norm_ffn_g]
    smm = [m_norm_mix_g, m_attn_q_norm_g, m_attn_k_norm_g, m_rel_bias[0], m_attn_out_norm_g, m_conv_w[0], m_a_log, m_dt_bias, m_dn_out_norm_g, m_norm_ffn_g]
    svv = [v_norm_mix_g, v_attn_q_norm_g, v_attn_k_norm_g, v_rel_bias[0], v_attn_out_norm_g, v_conv_w[0], v_a_log, v_dt_bias, v_dn_out_norm_g, v_norm_ffn_g]
    s_shapes = [t.shape for t in sw]
    pk = lambda ts: _pack_lanes(ts)
    pg = pk(sg)
    padr = (-pg.shape[0]) % 8
    padz = lambda t: jnp.pad(t, ((0, padr), (0, 0)))
    s_out = _adamw(padz(pg)[None], padz(pk(sw)), padz(pk(smm)), padz(pk(svv)), "adamw_small", pg.shape[0] + padr)
    s_g, s_d, s_m, s_v = (_unpack_lanes(t, s_shapes) for t in s_out)

    lead = lambda t: t[None]
    def ordered(small, big):
        nm, q, k, rel, ao, cw, al, dtb, dno, nf = small
        wi, wo, wgt, wu, wdn = big
        return [nm, lead(wi), q, k, lead(rel), ao, lead(cw), al, dtb, dno, lead(wo), nf, lead(wgt), lead(wu), lead(wdn)]
    outs = [loss, gx[None]]
    for small, big in ((s_g, bg), (s_d, bd_), (s_m, bm), (s_v, bv)):
        outs += ordered(small, big)
    return tuple(outs)
```

```python
import functools
import math

import jax
import jax.numpy as jnp
from jax import lax
from jax.experimental import pallas as pl
from jax.experimental.pallas import tpu as pltpu

F32 = jnp.float32
BF16 = jnp.bfloat16
_MXU = jnp.bfloat16

D = 1024
AW = 512
NHA = 8
DHA = 64
CH = 64
BAND = 9
NHD = 4
DHD = 128
DW = 512
FF = 2816
EPS = 1e-6
NEG = -1e30
N_DEV = 8
LANES = 128
VMEM_LIMIT = 56 * 1024 * 1024

ADAM_LR = 0.001
ADAM_B1 = 0.9
ADAM_B2 = 0.999
ADAM_EPS = 1e-08
ADAM_WD = 0.01
ADAM_STEP = 10

MESH_T = pl.DeviceIdType.MESH


def _cp(sem=None, vmem=VMEM_LIMIT):
    kw = dict(vmem_limit_bytes=vmem)
    if sem is not None:
        kw["dimension_semantics"] = sem
    return pltpu.CompilerParams(**kw)


def _dot(a, b):
    return jnp.dot(a.astype(_MXU), b.astype(_MXU), preferred_element_type=F32)


def _dot_nt(a, b):
    return lax.dot_general(a.astype(_MXU), b.astype(_MXU), (((1,), (1,)), ((), ())), preferred_element_type=F32)


def _dot_tn(a, b):
    return lax.dot_general(a.astype(_MXU), b.astype(_MXU), (((0,), (0,)), ((), ())), preferred_element_type=F32)


def _split2(x):
    hi = x.astype(BF16)
    lo = (x - hi.astype(F32)).astype(BF16)
    return hi, lo


def _dot_x2(x, ones_b):
    hi, lo = _split2(x)
    return jnp.dot(hi, ones_b, preferred_element_type=F32) + jnp.dot(lo, ones_b, preferred_element_type=F32)


def _dot_x2_nt(x, ones_b):
    hi, lo = _split2(x)
    dn = (((1,), (1,)), ((), ()))
    return lax.dot_general(hi, ones_b, dn, preferred_element_type=F32) + lax.dot_general(
        lo, ones_b, dn, preferred_element_type=F32)


def _iota(shape, dim):
    return lax.broadcasted_iota(jnp.int32, shape, dim)


def _block_ones(n, blk, dtype=BF16):
    r, c = _iota((n, n), 0), _iota((n, n), 1)
    return jnp.where((r // blk) == (c // blk), 1.0, 0.0).astype(dtype)


def _sigmoid(x):
    return 1.0 / (1.0 + jnp.exp(-x))


def _softplus(x):
    return jnp.maximum(x, 0.0) + jnp.log(1.0 + jnp.exp(-jnp.abs(x)))


def _col(x, k):
    lane = _iota(x.shape, 1)
    return jnp.sum(jnp.where(lane == k, x, 0.0), axis=1, keepdims=True)


def _row(x, k):
    sub = _iota(x.shape, 0)
    return jnp.sum(jnp.where(sub == k, x, 0.0), axis=0, keepdims=True)


def _my_pos():
    return lax.axis_index("x"), lax.axis_index("y"), lax.axis_index("c")


def _all_gather(x2d, name):
    R, W = x2d.shape

    def body(x_ref, out_ref, send_sems, recv_sems, local_sem):
        ag = _Gather(x_ref, out_ref, send_sems, recv_sems, local_sem)
        ag.start()
        ag.forward()
        ag.finish()

    return pl.pallas_call(
        body, name=name,
        out_shape=jax.ShapeDtypeStruct((N_DEV, R, W), x2d.dtype),
        in_specs=[pl.BlockSpec(memory_space=pl.ANY)],
        out_specs=pl.BlockSpec(memory_space=pl.ANY),
        scratch_shapes=_COMM_SEMS,
    )(x2d)


_COMM_SEMS = [pltpu.SemaphoreType.DMA((7,)), pltpu.SemaphoreType.DMA((7,)), pltpu.SemaphoreType.DMA]


class _Gather:
    def __init__(self, x_ref, out_ref, send_sems, recv_sems, local_sem):
        x, y, c = _my_pos()
        me, sibling = (x, y, c), (x, y, 1 - c)
        chips = [(1 - x, y), (x, 1 - y), (1 - x, 1 - y)]

        def slot(px, py, pc):
            return out_ref.at[4 * px + 2 * py + pc]

        def copy(k, block, to, src=None):
            return pltpu.make_async_remote_copy(
                src_ref=slot(*block) if src is None else src, dst_ref=slot(*block),
                send_sem=send_sems.at[k], recv_sem=recv_sems.at[k], device_id=to, device_id_type=MESH_T)

        self.mine = pltpu.make_async_copy(x_ref, slot(*me), local_sem)
        self.first = [copy(0, me, sibling, src=x_ref)]
        self.first += [copy(1 + j, me, (*chip, c), src=x_ref) for j, chip in enumerate(chips)]
        self.passed = [copy(4 + j, (*chip, c), sibling) for j, chip in enumerate(chips)]
        self.from_chips = [copy(1 + j, (*chip, c), me) for j, chip in enumerate(chips)]
        self.from_sibling = [copy(0, sibling, me)] + [copy(4 + j, (*chip, 1 - c), me) for j, chip in enumerate(chips)]

    def start(self):
        self.mine.start()
        for cp in self.first:
            cp.start()

    def forward(self):
        for arrived, onward in zip(self.from_chips, self.passed):
            arrived.wait_recv()
            onward.start()

    def finish(self):
        for cp in self.from_sibling:
            cp.wait_recv()
        for cp in self.first + self.passed:
            cp.wait_send()
        self.mine.wait()


class _Scatter:
    def __init__(self, s_ref, r_ref, send_sems, recv_sems, local_sem):
        x, y, c = _my_pos()
        self.mine = pltpu.make_async_copy(s_ref.at[4 * x + 2 * y + c], r_ref.at[0], local_sem)
        self.copies = []
        for m in range(1, N_DEV):
            px = x ^ ((m >> 2) & 1)
            py = y ^ ((m >> 1) & 1)
            pc = c ^ (m & 1)
            self.copies.append(pltpu.make_async_remote_copy(
                src_ref=s_ref.at[4 * px + 2 * py + pc], dst_ref=r_ref.at[m],
                send_sem=send_sems.at[m - 1], recv_sem=recv_sems.at[m - 1],
                device_id=(px, py, pc), device_id_type=MESH_T))

    def start(self):
        self.mine.start()
        for cp in self.copies:
            cp.start()

    def finish(self):
        for cp in self.copies:
            cp.wait_recv()
        for cp in self.copies:
            cp.wait_send()
        self.mine.wait()


def _all_to_all(send, name):
    def body(s_ref, r_ref, send_sems, recv_sems, local_sem):
        sc = _Scatter(s_ref, r_ref, send_sems, recv_sems, local_sem)
        sc.start()
        sc.finish()

    return pl.pallas_call(
        body, name=name,
        out_shape=jax.ShapeDtypeStruct(send.shape, send.dtype),
        in_specs=[pl.BlockSpec(memory_space=pl.ANY)],
        out_specs=pl.BlockSpec(memory_space=pl.ANY),
        scratch_shapes=_COMM_SEMS,
    )(send)


TM = 256
TG = 512


def _full(shape):
    nd = len(shape)
    return pl.BlockSpec(shape, lambda i: (0,) * nd)


def _rows(tm, w):
    return pl.BlockSpec((tm, w), lambda i: (i, 0))


def _head_sum(x, bd):
    return jnp.concatenate([_dot_x2(x[:, 0:256], bd), _dot_x2(x[:, 256:512], bd)], axis=1)


def _head_rms(x, bd, width):
    return lax.rsqrt(_head_sum(x * x, bd) * (1.0 / width) + EPS)


def _inproj(x, g_mix, w_in, qg_t, kg_t, later_w):
    T = x.shape[0]
    nt = T // TM

    def body(x_ref, g_ref, w_ref, qg_ref, kg_ref, lw_ref, araw_ref, an_ref, draw_ref, z_ref, ba_ref, h_ref, lw_all,
             send_sems, recv_sems, local_sem):
        i = pl.program_id(0)
        ag = _Gather(lw_ref, lw_all, send_sems, recv_sems, local_sem)
        pl.when(i == 0)(ag.start)
        pl.when(i == nt // 2)(ag.forward)
        xv = x_ref[...]
        r = lax.rsqrt(jnp.mean(xv * xv, axis=1, keepdims=True) + EPS)
        h = (xv * r * g_ref[...]).astype(_MXU)
        h_ref[...] = h
        a = jnp.dot(h, w_ref[:, 0:1536], preferred_element_type=F32)
        araw_ref[...] = a
        bd = _block_ones(AW // 2, DHA)
        q = a[:, 0:AW]
        k = a[:, AW:2 * AW]
        qn = q * _head_rms(q, bd, DHA) * (qg_ref[...] * (DHA ** -0.5))
        kn = k * _head_rms(k, bd, DHA) * kg_ref[...]
        an_ref[:, 0:AW] = qn.astype(_MXU)
        an_ref[:, AW:2 * AW] = kn.astype(_MXU)
        an_ref[:, 2 * AW:3 * AW] = a[:, 2 * AW:3 * AW].astype(_MXU)
        draw_ref[...] = jnp.dot(h, w_ref[:, 1536:3072], preferred_element_type=F32)
        z_ref[...] = jnp.dot(h, w_ref[:, 3072:3584], preferred_element_type=F32)
        ba_ref[...] = jnp.dot(h, w_ref[:, 3584:3592], preferred_element_type=F32)
        pl.when(i == nt - 1)(ag.finish)

    anyspec = pl.BlockSpec(memory_space=pl.ANY)
    return pl.pallas_call(
        body, name="inproj", grid=(nt,),
        in_specs=[_rows(TM, D), _full((1, D)), _full((D, 3592)), _full((1, AW)), _full((1, AW)), anyspec],
        out_specs=[_rows(TM, 1536), _rows(TM, 1536), _rows(TM, 1536), _rows(TM, DW), _rows(TM, 8), _rows(TM, D),
                   anyspec],
        out_shape=[jax.ShapeDtypeStruct((T, 1536), F32), jax.ShapeDtypeStruct((T, 1536), _MXU),
                   jax.ShapeDtypeStruct((T, 1536), F32), jax.ShapeDtypeStruct((T, DW), F32),
                   jax.ShapeDtypeStruct((T, 8), F32), jax.ShapeDtypeStruct((T, D), _MXU),
                   jax.ShapeDtypeStruct((N_DEV,) + later_w.shape, later_w.dtype)],
        scratch_shapes=_COMM_SEMS,
        compiler_params=_cp(("arbitrary",)),
    )(x, g_mix, w_in, qg_t, kg_t, later_w)


TQ = 256
TW = 768
T_LO, T_HI = 65, 256
VAR0 = 384
TOEP = 1024


def _bias_tables(rb_t):
    def body(rb_ref, tab_ref, tabt_ref):
        h = pl.program_id(0)
        n = _iota((8, TOEP), 1)

        def line(m):
            idx = jnp.clip(512 - m, -128, 128) + 128
            return lax.fori_loop(T_LO, T_HI, lambda t, a: jnp.where(idx == t, rb_ref[h, t], a),
                                 jnp.full((8, TOEP), rb_ref[h, T_HI], F32))[0:1, :]

        def band(r, j):
            return ((j >> 6) >= (r >> 6)) & ((j >> 6) <= (r >> 6) + 8)

        g = line(jnp.where(n < TW, n, n - TOEP))
        tab = pltpu.roll(jnp.broadcast_to(g, (TQ, TOEP)), 0, 1, stride=1, stride_axis=0)[:, 0:TW]
        tab_ref[0] = jnp.where(band(_iota((TQ, TW), 0), _iota((TQ, TW), 1)), tab, NEG)
        gt = line(jnp.where(n < TQ, -n, TOEP - n))
        tabt = pltpu.roll(jnp.broadcast_to(gt, (TW, TOEP)), 0, 1, stride=1, stride_axis=0)[:, 0:TQ]
        tabt_ref[0] = jnp.where(band(_iota((TW, TQ), 1), _iota((TW, TQ), 0)), tabt, NEG)

    return pl.pallas_call(
        body, name="bias_tables", grid=(NHA,),
        in_specs=[pl.BlockSpec(memory_space=pltpu.SMEM)],
        out_specs=[pl.BlockSpec((1, TQ, TW), lambda h: (h, 0, 0)), pl.BlockSpec((1, TW, TQ), lambda h: (h, 0, 0))],
        out_shape=[jax.ShapeDtypeStruct((NHA, TQ, TW), F32), jax.ShapeDtypeStruct((NHA, TW, TQ), F32)],
        compiler_params=_cp(("arbitrary",)),
    )(rb_t)


def _bias_grad(dtabt):
    def body(d_ref, o_ref):
        a, b = _iota((TQ, TQ), 0), _iota((TQ, TQ), 1)
        anti = jnp.where(a + b == TQ - 1, 1.0, 0.0).astype(BF16)
        drev = sum(jnp.dot(t, anti, preferred_element_type=F32) for t in _split3(d_ref[0]))
        wide = jnp.concatenate([drev, jnp.zeros((TW, TOEP - TQ), F32)], axis=1)
        cols = jnp.sum(pltpu.roll(wide, 0, 1, stride=1, stride_axis=0), axis=0, keepdims=True)
        c = _iota((TOEP, VAR0), 0)
        idx = jnp.clip(512 + TQ - 1 - c, -128, 128) + 128
        onehot = jnp.where(idx == _iota((TOEP, VAR0), 1), 1.0, 0.0).astype(BF16)
        cols8 = jnp.broadcast_to(cols, (8, TOEP))
        o_ref[0] = sum(jnp.dot(t, onehot, preferred_element_type=F32) for t in _split3(cols8))[0:1, :]

    return pl.pallas_call(
        body, name="bias_grad", grid=(NHA,),
        in_specs=[pl.BlockSpec((1, TW, TQ), lambda h: (h, 0, 0))],
        out_specs=pl.BlockSpec((1, 1, VAR0), lambda h: (h, 0, 0)),
        out_shape=jax.ShapeDtypeStruct((NHA, 1, VAR0), F32),
        compiler_params=_cp(("arbitrary",)),
    )(dtabt)


def _kv_spec(col, back):
    return pl.BlockSpec((TQ, AW), lambda i: (jnp.maximum(i - back, 0), col))


def _attn_fwd(an, tab):
    T = an.shape[0]

    def body(q_ref, k2_ref, k1_ref, k0_ref, v2_ref, v1_ref, v0_ref, tab_ref, o_ref):
        i = pl.program_id(0)
        kwin = jnp.concatenate([k2_ref[...], k1_ref[...], k0_ref[...]], axis=0)
        vwin = jnp.concatenate([v2_ref[...], v1_ref[...], v0_ref[...]], axis=0)
        q = q_ref[...]
        invalid = (_iota((TQ, TW), 1) + TQ * i) < 512
        lo_half = _iota((TQ, LANES), 1) < DHA
        for p in range(NHA // 2):
            sl = slice(LANES * p, LANES * (p + 1))
            qp, kp, vp = q[:, sl], kwin[:, sl], vwin[:, sl]
            outs = []
            for half in range(2):
                mask = lo_half if half == 0 else jnp.logical_not(lo_half)
                qm = jnp.where(mask, qp, jnp.zeros_like(qp))
                s = _dot_nt(qm, kp) + tab_ref[2 * p + half]
                s = jnp.where(invalid, NEG, s)
                m = jnp.max(s, axis=1, keepdims=True)
                e = jnp.exp(s - m)
                l = jnp.sum(e, axis=1, keepdims=True)
                outs.append(_dot(e, vp) / l)
            o_ref[:, sl] = jnp.where(lo_half, outs[0], outs[1])

    return pl.pallas_call(
        body, name="attn_fwd", grid=(T // TQ,),
        in_specs=[pl.BlockSpec((TQ, AW), lambda i: (i, 0)),
                  _kv_spec(1, 2), _kv_spec(1, 1), _kv_spec(1, 0), _kv_spec(2, 2), _kv_spec(2, 1), _kv_spec(2, 0),
                  _full((NHA, TQ, TW))],
        out_specs=_rows(TQ, AW),
        out_shape=jax.ShapeDtypeStruct((T, AW), F32),
        compiler_params=_cp(("arbitrary",)),
    )(an, an, an, an, an, an, an, tab)


def _attn_bwd(an, dout, tabt, send):
    T = an.shape[0]
    nq = T // TQ

    def qi(i):
        return jnp.minimum(i, nq - 1)

    def kv_spec(col, back):
        return pl.BlockSpec((TQ, AW), lambda i: (jnp.maximum(qi(i) - back, 0), col))

    def body(q_ref, do_ref, k2_ref, k1_ref, k0_ref, v2_ref, v1_ref, v0_ref, tabt_ref, send_ref,
             dq_ref, dk_ref, dv_ref, dtab_ref, recv_ref, dk_acc, dv_acc, send_sems, recv_sems, local_sem):
        i = pl.program_id(0)
        sc = _Scatter(send_ref, recv_ref, send_sems, recv_sems, local_sem)
        pl.when(i == 0)(sc.start)

        @pl.when(i == 0)
        def _():
            dtab_ref[...] = jnp.zeros_like(dtab_ref)

        new = i % 3
        dk_acc[new] = jnp.zeros((TQ, AW), F32)
        dv_acc[new] = jnp.zeros((TQ, AW), F32)

        @pl.when(i < nq)
        def _():
            kwin = jnp.concatenate([k2_ref[...], k1_ref[...], k0_ref[...]], axis=0)
            vwin = jnp.concatenate([v2_ref[...], v1_ref[...], v0_ref[...]], axis=0)
            q = q_ref[...]
            do = do_ref[...].astype(_MXU)
            invalid = (_iota((TW, TQ), 0) + TQ * i) < 512
            lo_half = _iota((TQ, LANES), 1) < DHA
            for p in range(NHA // 2):
                sl = slice(LANES * p, LANES * (p + 1))
                qp, kp, vp, dop = q[:, sl], kwin[:, sl], vwin[:, sl], do[:, sl]
                dq_pair = jnp.zeros((TQ, LANES), F32)
                dk_pair = jnp.zeros((TW, LANES), F32)
                dv_pair = jnp.zeros((TW, LANES), F32)
                for half in range(2):
                    h = 2 * p + half
                    mask = lo_half if half == 0 else jnp.logical_not(lo_half)
                    qm = jnp.where(mask, qp, jnp.zeros_like(qp))
                    dom = jnp.where(mask, dop, jnp.zeros_like(dop))
                    st = _dot_nt(kp, qm) + tabt_ref[h]
                    st = jnp.where(invalid, NEG, st)
                    m = jnp.max(st, axis=0, keepdims=True)
                    e = jnp.exp(st - m)
                    pt = e * (1.0 / jnp.sum(e, axis=0, keepdims=True))
                    dpt = _dot_nt(vp, dom)
                    delta = jnp.sum(pt * dpt, axis=0, keepdims=True)
                    dst = pt * (dpt - delta)
                    dtab_ref[h] += dst
                    dsb = dst.astype(_MXU)
                    dv_pair += _dot(pt, dom)
                    dk_pair += _dot(dsb, qm)
                    dq_pair += jnp.where(mask, _dot_tn(dsb, kp), 0.0)
                dq_ref[:, sl] = dq_pair
                for w in range(3):
                    slot = (i + 1 + w) % 3
                    rows = slice(TQ * w, TQ * (w + 1))
                    dk_acc[slot, :, sl] += dk_pair[rows]
                    dv_acc[slot, :, sl] += dv_pair[rows]

        @pl.when(i >= 2)
        def _():
            done = (i + 1) % 3
            dk_ref[...] = dk_acc[done]
            dv_ref[...] = dv_acc[done]

        pl.when(i == nq + 1)(sc.finish)

    back2 = pl.BlockSpec((TQ, AW), lambda i: (jnp.maximum(i - 2, 0), 0))
    anyspec = pl.BlockSpec(memory_space=pl.ANY)
    return pl.pallas_call(
        body, name="attn_bwd", grid=(nq + 2,),
        in_specs=[pl.BlockSpec((TQ, AW), lambda i: (qi(i), 0)), pl.BlockSpec((TQ, AW), lambda i: (qi(i), 0)),
                  kv_spec(1, 2), kv_spec(1, 1), kv_spec(1, 0), kv_spec(2, 2), kv_spec(2, 1), kv_spec(2, 0),
                  _full((NHA, TW, TQ)), anyspec],
        out_specs=[pl.BlockSpec((TQ, AW), lambda i: (qi(i), 0)), back2, back2, _full((NHA, TW, TQ)), anyspec],
        out_shape=[jax.ShapeDtypeStruct((T, AW), F32), jax.ShapeDtypeStruct((T, AW), F32),
                   jax.ShapeDtypeStruct((T, AW), F32), jax.ShapeDtypeStruct((NHA, TW, TQ), F32),
                   jax.ShapeDtypeStruct(send.shape, send.dtype)],
        scratch_shapes=[pltpu.VMEM((3, TQ, AW), F32), pltpu.VMEM((3, TQ, AW), F32)] + _COMM_SEMS,
        compiler_params=_cp(("arbitrary",)),
    )(an, dout, an, an, an, an, an, an, tabt, send)


GR = 128
NG = TG // GR
CPT = TG // CH
CONV_K = 4


def _split3(x):
    a = x.astype(BF16)
    r = x - a.astype(F32)
    b = r.astype(BF16)
    c = (r - b.astype(F32)).astype(BF16)
    return a, b, c


def _ones_dot(ones_b, x):
    return sum(jnp.dot(ones_b, t, preferred_element_type=F32) for t in _split3(x))


def _dot_ones_nt(x, ones_b):
    dn = (((1,), (1,)), ((), ()))
    return sum(lax.dot_general(t, ones_b, dn, preferred_element_type=F32) for t in _split3(x))


def _dn_masks():
    r, c = _iota((GR, GR), 0), _iota((GR, GR), 1)
    same = (r >> 6) == (c >> 6)
    one = lambda m: jnp.where(m, 1.0, 0.0).astype(BF16)
    return dict(
        tril=same & (c <= r), strict=same & (c < r), triu=same & (c >= r), strict_u=same & (c > r),
        tril_b=one(same & (c <= r)), triu_b=one(same & (c >= r)), blk_b=one(same), eye_b=one(r == c),
        eye=jnp.where(r == c, 1.0, 0.0).astype(F32),
        fold_b=one((_iota((GR, CH), 0) & (CH - 1)) == _iota((GR, CH), 1)),
        last=(_iota((GR, 1), 0) & (CH - 1)) == CH - 1,
    )


def _shift_down(x, halo, k):
    if k == 0:
        return x
    xs = pltpu.roll(x, k, 0)
    hs = pltpu.roll(halo, k, 0)
    top = jnp.where(_iota(halo.shape, 0) < k, hs, xs[0:8])
    return jnp.concatenate([top, xs[8:]], axis=0)


def _shift_up(x, halo, k):
    if k == 0:
        return x
    n = x.shape[0]
    xs = pltpu.roll(x, n - k, 0)
    hs = pltpu.roll(halo, 8 - k, 0)
    bot = jnp.where(_iota(halo.shape, 0) >= 8 - k, hs, xs[n - 8:n])
    return jnp.concatenate([xs[0:n - 8], bot], axis=0)


def _conv(x, halo, w):
    y = x * w[CONV_K - 1:CONV_K, :]
    for k in range(1, CONV_K):
        y = y + _shift_down(x, halo, k) * w[CONV_K - 1 - k:CONV_K - k, :]
    return y


def _tri_inv(lmats, eye):
    ps = [-m for m in lmats]
    rs = [eye + p for p in ps]
    for _ in range(5):
        ps = [_dot(p, p) for p in ps]
        rs = [r + _dot(r, p) for r, p in zip(rs, ps)]
    return rs


def _gate_terms(ba_g, bat_g, alog8, dtb8, alog8t, dtb8t, K):
    g8 = -jnp.exp(alog8) * _softplus(ba_g + dtb8)
    g8t = -jnp.exp(alog8t) * _softplus(bat_g + dtb8t)
    gc8 = _ones_dot(K["tril_b"], g8)
    gl8 = _ones_dot(K["blk_b"], g8)
    gcrow8 = _dot_ones_nt(g8t, K["tril_b"])
    return g8, gc8, gl8, gcrow8


def _dn_heads(c_tile, rows, beta8, gc8, gl8, gcrow8, K, pre=None):
    ds = [_dn_head(c_tile, rows, h, beta8, gc8, gl8, gcrow8, K) for h in range(NHD)]
    if pre is None:
        for d, tm in zip(ds, _tri_inv([d["lmat"] for d in ds], K["eye"])):
            d.update(tm=tm, u=_dot(tm, d["vb"]), w=_dot(tm, d["kg"]))
    else:
        for d, (tm, u, w) in zip(ds, pre):
            d.update(tm=tm, u=u, w=w)
    return ds


def _dn_head(c_tile, rows, h, beta8, gc8, gl8, gcrow8, K):
    qr = c_tile[rows, DHD * h:DHD * (h + 1)]
    kr = c_tile[rows, DW + DHD * h:DW + DHD * (h + 1)]
    v = c_tile[rows, 2 * DW + DHD * h:2 * DW + DHD * (h + 1)]
    rq = lax.rsqrt(jnp.sum(qr * qr, axis=1, keepdims=True) + EPS)
    rk = lax.rsqrt(jnp.sum(kr * kr, axis=1, keepdims=True) + EPS)
    qh, kn = qr * rq, kr * rk
    qn = qh * (DHD ** -0.5)
    beta = _col(beta8, h)
    gccol, glcol, gcrow = _col(gc8, NHD + h), _col(gl8, NHD + h), _row(gcrow8, NHD + h)
    diff = gccol - gcrow
    gam_m = jnp.exp(jnp.where(K["tril"], diff, NEG))
    gam = jnp.exp(gccol)
    egl = jnp.exp(glcol - gccol)
    kb, vb = kn * beta, v * beta
    kg = kb * gam
    pl_ = _dot_nt(kb, kn)
    lmat = jnp.where(K["strict"], pl_ * gam_m, 0.0)
    pm = _dot_nt(qn, kn)
    mm = pm * gam_m
    return dict(qr=qr, kr=kr, v=v, rq=rq, rk=rk, qh=qh, qn=qn, kn=kn, beta=beta, diff=diff, gam_m=gam_m, gam=gam,
                egl=egl, el=jnp.exp(glcol), kb=kb, vb=vb, kg=kg, pl=pl_, pm=pm, lmat=lmat, mm=mm,
                qd=qn * gam, kd=kn * egl)


def _halo_prev(width):
    return pl.BlockSpec((8, width), lambda i: (jnp.maximum(i * (TG // 8) - 1, 0), 0))


def _dn_prep(draw, conv_w, ba, bat, alog8, dtb8, alog8t, dtb8t):
    T = draw.shape[0]
    nb = T // TG
    hm = lambda w, dt: jax.ShapeDtypeStruct((NHD, T, w), dt)
    hm_spec = lambda w: pl.BlockSpec((NHD, TG, w), lambda i: (0, i, 0))
    pc = lambda r, c: jax.ShapeDtypeStruct((NHD, T // CH, r, c), _MXU)
    pc_spec = lambda r, c: pl.BlockSpec((NHD, CPT, r, c), lambda i: (0, i, 0, 0))

    def body(x_ref, halo_ref, cw_ref, ba_ref, bat_ref, al_ref, dt_ref, alt_ref, dtt_ref,
             u_ref, w_ref, kd_ref, tm_ref, wq_ref, km_ref, mq_ref, wt_ref, elb_ref):
        i = pl.program_id(0)
        K = _dn_masks()
        halo = jnp.where(i > 0, halo_ref[...], 0.0)
        cv = _conv(x_ref[...], halo, cw_ref[...])
        c_tile = cv * _sigmoid(cv)
        eye128 = jnp.where(_iota((DHD, DHD), 0) == _iota((DHD, DHD), 1), 1.0, 0.0).astype(_MXU)
        for g in range(NG):
            rows = slice(GR * g, GR * (g + 1))
            ba_g = ba_ref[rows, :]
            _, gc8, gl8, gcrow8 = _gate_terms(ba_g, bat_ref[:, rows], al_ref[...], dt_ref[...], alt_ref[...],
                                              dtt_ref[...], K)
            beta8 = _sigmoid(ba_g)
            for h, d in enumerate(_dn_heads(c_tile, rows, beta8, gc8, gl8, gcrow8, K)):
                gam_t = jnp.exp(jnp.where(K["triu"], -d["diff"], NEG))
                tm_ref[h, rows, :] = d["tm"].astype(_MXU)
                mmt = _dot_nt(d["kn"], d["qn"]) * gam_t
                u_ref[h, rows, :] = d["u"]
                w_ref[h, rows, :] = d["w"].astype(_MXU)
                kd_ref[h, rows, :] = d["kd"].astype(_MXU)
                mc = _dot(d["mm"], K["fold_b"])
                mct = _dot(mmt, K["fold_b"])
                elb = jnp.broadcast_to(d["el"], (GR, DHD))
                for cc in range(GR // CH):
                    ch = slice(CH * cc, CH * (cc + 1))
                    n = (GR // CH) * g + cc
                    wq_ref[h, n, 0:CH, :] = d["w"][ch].astype(_MXU)
                    wq_ref[h, n, CH:2 * CH, :] = d["qd"][ch].astype(_MXU)
                    km_ref[h, n, 0:DHD, :] = _dot_nt(eye128, d["kd"][ch]).astype(_MXU)
                    km_ref[h, n, DHD:DHD + CH, :] = mc[ch].astype(_MXU)
                    mq_ref[h, n, 0:CH, :] = mct[ch].astype(_MXU)
                    mq_ref[h, n, CH:CH + DHD, :] = _dot_nt(eye128, d["qd"][ch]).astype(_MXU)
                    wt_ref[h, n] = _dot_nt(eye128, d["w"][ch]).astype(_MXU)
                    elb_ref[n:n + 1, DHD * h:DHD * (h + 1)] = elb[CH * cc:CH * cc + 1, :]

    return pl.pallas_call(
        body, name="dn_prep", grid=(nb,),
        in_specs=[_rows(TG, 1536), _halo_prev(1536), _full((CONV_K, 1536)), _rows(TG, 8),
                  pl.BlockSpec((8, TG), lambda i: (0, i)), _full((1, 8)), _full((1, 8)), _full((8, 1)), _full((8, 1))],
        out_specs=[hm_spec(DHD), hm_spec(DHD), hm_spec(DHD), hm_spec(GR),
                   pc_spec(2 * CH, DHD), pc_spec(DHD + CH, CH), pc_spec(CH + DHD, CH), pc_spec(DHD, CH),
                   pl.BlockSpec((CPT, NHD * DHD), lambda i: (i, 0))],
        out_shape=[hm(DHD, F32), hm(DHD, _MXU), hm(DHD, _MXU), hm(GR, _MXU),
                   pc(2 * CH, DHD), pc(DHD + CH, CH), pc(CH + DHD, CH), pc(DHD, CH),
                   jax.ShapeDtypeStruct((T // CH, NHD * DHD), F32)],
        compiler_params=_cp(("arbitrary",)),
    )(draw, draw, conv_w, ba, bat, alog8, dtb8, alog8t, dtb8t)


def _dn_scan(u, wq, km, elb):
    T = u.shape[1]
    nb = T // TG
    hm_spec = lambda wd: pl.BlockSpec((NHD, TG, wd), lambda i: (0, i, 0))

    def body(u_ref, wq_ref, km_ref, elb_ref, o_ref, vn_ref, sn_ref, S):
        @pl.when(pl.program_id(0) == 0)
        def _():
            S[...] = jnp.zeros_like(S)

        sub8 = _iota((CPT, DHD), 0)
        heads = range(NHD)

        def chunk(cc, carry):
            rs = pl.ds(pl.multiple_of(cc * CH, CH), CH)
            sh = [S[h] for h in heads]
            sb = [s.astype(_MXU) for s in sh]
            r1 = [_dot(wq_ref[h, cc], sb[h]) for h in heads]
            vnb = [(u_ref[h, rs, :] - r1[h][0:CH]).astype(_MXU) for h in heads]
            r2 = [_dot(km_ref[h, cc], vnb[h]) for h in heads]
            for h in heads:
                el = jnp.sum(jnp.where(sub8 == cc, elb_ref[:, DHD * h:DHD * (h + 1)], 0.0), axis=0, keepdims=True)
                S[h] = sh[h] * el + r2[h][0:DHD]
                sn_ref[cc, h] = sb[h]
                vn_ref[h, rs, :] = vnb[h]
                o_ref[h, rs, :] = r1[h][CH:2 * CH] + r2[h][DHD:DHD + CH]
            return carry

        lax.fori_loop(0, CPT, chunk, 0)

    return pl.pallas_call(
        body, name="dn_scan", grid=(nb,),
        in_specs=[hm_spec(DHD), pl.BlockSpec((NHD, CPT, 2 * CH, DHD), lambda i: (0, i, 0, 0)),
                  pl.BlockSpec((NHD, CPT, DHD + CH, CH), lambda i: (0, i, 0, 0)),
                  pl.BlockSpec((CPT, NHD * DHD), lambda i: (i, 0))],
        out_specs=[hm_spec(DHD), hm_spec(DHD), pl.BlockSpec((CPT, NHD, DHD, DHD), lambda i: (i, 0, 0, 0))],
        out_shape=[jax.ShapeDtypeStruct((NHD, T, DHD), F32), jax.ShapeDtypeStruct((NHD, T, DHD), _MXU),
                   jax.ShapeDtypeStruct((T // CH, NHD, DHD, DHD), _MXU)],
        scratch_shapes=[pltpu.VMEM((NHD, DHD, DHD), F32)],
        compiler_params=_cp(("arbitrary",)),
    )(u, wq, km, elb)


def _dn_scan_bwd(do, mq, kd, wt, sn, vn, elb):
    T = do.shape[1]
    nb = T // TG
    rev = lambda wd: pl.BlockSpec((NHD, TG, wd), lambda i: (0, nb - 1 - i, 0))
    rev_t = lambda r: pl.BlockSpec((NHD, CPT, r, CH), lambda i: (0, nb - 1 - i, 0, 0))

    def body(do_ref, mq_ref, kd_ref, wt_ref, sn_ref, vn_ref, elb_ref,
             du_ref, dw_ref, dqd_ref, dkd_ref, dgx_ref, dS):
        @pl.when(pl.program_id(0) == 0)
        def _():
            dS[...] = jnp.zeros_like(dS)

        last_row = _iota((CH, DHD), 0) == CH - 1
        sub8 = _iota((CPT, DHD), 0)
        heads = range(NHD)

        def chunk(k, carry):
            cc = CPT - 1 - k
            rs = pl.ds(pl.multiple_of(cc * CH, CH), CH)
            dsh = [dS[h] for h in heads]
            dsb = [d.astype(_MXU) for d in dsh]
            doc = [do_ref[h, rs, :].astype(_MXU) for h in heads]
            a = [_dot(mq_ref[h, cc], doc[h]) for h in heads]
            b = [_dot(kd_ref[h, rs, :], dsb[h]) for h in heads]
            dvn = [a[h][0:CH] + b[h] for h in heads]
            dvnb = [d.astype(_MXU) for d in dvn]
            e = [_dot(wt_ref[h, cc], dvnb[h]) for h in heads]
            for h in heads:
                el = jnp.sum(jnp.where(sub8 == cc, elb_ref[:, DHD * h:DHD * (h + 1)], 0.0), axis=0, keepdims=True)
                sn = sn_ref[cc, h]
                dS[h] = a[h][CH:CH + DHD] + dsh[h] * el - e[h]
                du_ref[h, rs, :] = dvn[h]
                c = _dot_nt(jnp.concatenate([doc[h], dvnb[h]], axis=0), sn)
                dqd_ref[h, rs, :] = c[0:CH]
                dw_ref[h, rs, :] = -c[CH:2 * CH]
                dkd_ref[h, rs, :] = _dot_nt(vn_ref[h, rs, :], dsb[h])
                part = jnp.sum(dsh[h] * sn.astype(F32), axis=0, keepdims=True) * el
                dgx_ref[h, rs, :] = jnp.where(last_row, part, 0.0)
            return carry

        lax.fori_loop(0, CPT, chunk, 0)

    o = jax.ShapeDtypeStruct((NHD, T, DHD), F32)
    return pl.pallas_call(
        body, name="dn_scan_bwd", grid=(nb,),
        in_specs=[rev(DHD), rev_t(CH + DHD), rev(DHD), rev_t(DHD),
                  pl.BlockSpec((CPT, NHD, DHD, DHD), lambda i: (nb - 1 - i, 0, 0, 0)), rev(DHD),
                  pl.BlockSpec((CPT, NHD * DHD), lambda i: (nb - 1 - i, 0))],
        out_specs=[rev(DHD)] * 5,
        out_shape=[o] * 5,
        scratch_shapes=[pltpu.VMEM((NHD, DHD, DHD), F32)],
        compiler_params=_cp(("arbitrary",)),
    )(do, mq, kd, wt, sn, vn, elb)


def _put_col(acc, k, col):
    return jnp.where(_iota(acc.shape, 1) == k, col, acc)


def _dn_post_bwd(draw, conv_w, ba, bat, alog8, dtb8, alog8t, dtb8t, du, dw, dqd, dkd, dgx, do, vn, tm, u, w):
    T = draw.shape[0]
    nb = T // TG
    hm_spec = lambda wd: pl.BlockSpec((NHD, TG, wd), lambda i: (0, nb - 1 - i, 0))
    rrows = lambda w: pl.BlockSpec((TG, w), lambda i: (nb - 1 - i, 0))

    def body(x_ref, halo_ref, cw_ref, ba_ref, bat_ref, al_ref, dt_ref, alt_ref, dtt_ref,
             du_ref, dw_ref, dqd_ref, dkd_ref, dgx_ref, do_ref, vn_ref, tm_ref, u_ref, w_ref,
             dx_ref, dba_ref, sm_ref, dcw_ref, dc_ref, nxt_ref):
        i = pl.program_id(0)

        @pl.when(i == 0)
        def _():
            sm_ref[...] = jnp.zeros_like(sm_ref)
            dcw_ref[...] = jnp.zeros_like(dcw_ref)
            nxt_ref[...] = jnp.zeros_like(nxt_ref)

        K = _dn_masks()
        halo = jnp.where(i < nb - 1, halo_ref[...], 0.0)
        xv = x_ref[...]
        cv = _conv(xv, halo, cw_ref[...])
        sg = _sigmoid(cv)
        c_tile = cv * sg
        dsilu = sg * (1.0 + cv * (1.0 - sg))
        for g in range(NG):
            rows = slice(GR * g, GR * (g + 1))
            ba_g = ba_ref[rows, :]
            g8, gc8, gl8, gcrow8 = _gate_terms(ba_g, bat_ref[:, rows], al_ref[...], dt_ref[...], alt_ref[...],
                                               dtt_ref[...], K)
            beta8 = _sigmoid(ba_g)
            dgc8 = jnp.zeros((GR, 8), F32)
            rd8 = jnp.zeros((GR, 8), F32)
            dbeta8 = jnp.zeros((GR, 8), F32)
            pre = [(tm_ref[h, rows, :], u_ref[h, rows, :], w_ref[h, rows, :]) for h in range(NHD)]
            for h, d in enumerate(_dn_heads(c_tile, rows, beta8, gc8, gl8, gcrow8, K, pre)):
                gam_m, gam, egl = d["gam_m"], d["gam"], d["egl"]
                gam_t = jnp.exp(jnp.where(K["triu"], -d["diff"], NEG))
                qn, kn, kb, vv = d["qn"], d["kn"], d["kb"], d["v"]
                duh, dwh = du_ref[h, rows, :], dw_ref[h, rows, :]
                dqdh, dkdh = dqd_ref[h, rows, :], dkd_ref[h, rows, :]
                doh, vnh = do_ref[h, rows, :], vn_ref[h, rows, :]
                tt = _dot_nt(K["eye_b"].astype(_MXU), d["tm"])
                dvb = _dot(tt, duh)
                dkg = _dot(tt, dwh)
                da = -(_dot_nt(dvb, d["u"]) + _dot_nt(dkg, d["w"]))
                dat = -(_dot_nt(d["u"], dvb) + _dot_nt(d["w"], dkg))
                dpl = jnp.where(K["strict"], da, 0.0) * gam_m
                dplt = jnp.where(K["strict_u"], dat, 0.0) * gam_t
                dpm = jnp.where(K["tril"], _dot_nt(doh, vnh), 0.0) * gam_m
                dpmt = jnp.where(K["triu"], _dot_nt(vnh, doh), 0.0) * gam_t
                plt = _dot_nt(kn, kb)
                pmt = _dot_nt(kn, qn)
                dkb = _dot(dpl, kn) + dkg * gam
                dkn = _dot(dplt, kb) + _dot(dpmt, qn) + dkdh * egl + dkb * d["beta"]
                dqn = _dot(dpm, kn) + dqdh * gam
                gsum = jnp.sum(dpl * d["pl"] + dpm * d["pm"], axis=1, keepdims=True)
                gsum_t = jnp.sum(dplt * plt + dpmt * pmt, axis=1, keepdims=True)
                rd = jnp.sum(dkdh * d["kd"], axis=1, keepdims=True)
                dgc = (gsum - gsum_t + jnp.sum(dqdh * d["qd"], axis=1, keepdims=True)
                       + jnp.sum(dkg * d["kg"], axis=1, keepdims=True) - rd
                       + jnp.sum(dgx_ref[h, rows, :], axis=1, keepdims=True))
                dgc8 = _put_col(dgc8, NHD + h, dgc)
                rd8 = _put_col(rd8, NHD + h, rd)
                dbeta = jnp.sum(dkb * kn, axis=1, keepdims=True) + jnp.sum(dvb * vv, axis=1, keepdims=True)
                dbeta8 = _put_col(dbeta8, h, dbeta)
                dqh = dqn * (DHD ** -0.5)
                qh = d["qh"]
                dqr = d["rq"] * (dqh - qh * jnp.sum(dqh * qh, axis=1, keepdims=True))
                dkr = d["rk"] * (dkn - kn * jnp.sum(dkn * kn, axis=1, keepdims=True))
                cq = slice(DHD * h, DHD * (h + 1))
                ck = slice(DW + DHD * h, DW + DHD * (h + 1))
                cvv = slice(2 * DW + DHD * h, 2 * DW + DHD * (h + 1))
                dc_ref[rows, cq] = dqr * dsilu[rows, cq]
                dc_ref[rows, ck] = dkr * dsilu[rows, ck]
                dc_ref[rows, cvv] = dvb * d["beta"] * dsilu[rows, cvv]
            dgc8 = dgc8 + jnp.where(K["last"], _ones_dot(K["blk_b"], rd8), 0.0)
            dg8 = _ones_dot(K["triu_b"], dgc8)
            sgm = _sigmoid(ba_g + dt_ref[...])
            dalpha = dg8 * (-jnp.exp(al_ref[...])) * sgm
            lane8 = _iota((GR, 8), 1)
            dba_ref[rows, :] = jnp.where(lane8 < NHD, dbeta8 * beta8 * (1.0 - beta8), dalpha)
            valid = lane8 >= NHD
            sm_ref[0:1, 0:8] += jnp.sum(jnp.where(valid, dg8 * g8, 0.0), axis=0, keepdims=True)
            sm_ref[1:2, 0:8] += jnp.sum(jnp.where(valid, dalpha, 0.0), axis=0, keepdims=True)

        dcv = dc_ref[...]
        nxt = nxt_ref[...]
        w = cw_ref[...]
        dx = dcv * w[CONV_K - 1:CONV_K, :]
        dcw_ref[CONV_K - 1:CONV_K, :] += jnp.sum(dcv * xv, axis=0, keepdims=True)
        for k in range(1, CONV_K):
            j = CONV_K - 1 - k
            dx = dx + _shift_up(dcv, nxt, k) * w[j:j + 1, :]
            dcw_ref[j:j + 1, :] += jnp.sum(dcv * _shift_down(xv, halo, k), axis=0, keepdims=True)
        dx_ref[...] = dx
        nxt_ref[...] = dcv[0:8]

    return pl.pallas_call(
        body, name="dn_post_bwd", grid=(nb,),
        in_specs=[rrows(1536), pl.BlockSpec((8, 1536), lambda i: (jnp.maximum((nb - 1 - i) * (TG // 8) - 1, 0), 0)),
                  _full((CONV_K, 1536)), rrows(8),
                  pl.BlockSpec((8, TG), lambda i: (0, nb - 1 - i)), _full((1, 8)), _full((1, 8)), _full((8, 1)),
                  _full((8, 1)),
                  hm_spec(DHD), hm_spec(DHD), hm_spec(DHD), hm_spec(DHD), hm_spec(DHD), hm_spec(DHD), hm_spec(DHD),
                  hm_spec(GR), hm_spec(DHD), hm_spec(DHD)],
        out_specs=[rrows(1536), rrows(8), _full((8, LANES)), _full((8, 1536))],
        out_shape=[jax.ShapeDtypeStruct((T, 1536), F32), jax.ShapeDtypeStruct((T, 8), F32),
                   jax.ShapeDtypeStruct((8, LANES), F32), jax.ShapeDtypeStruct((8, 1536), F32)],
        scratch_shapes=[pltpu.VMEM((TG, 1536), F32), pltpu.VMEM((8, 1536), F32)],
        compiler_params=_cp(("arbitrary",)),
    )(draw, draw, conv_w, ba, bat, alog8, dtb8, alog8t, dtb8t, du, dw, dqd, dkd, dgx, do, vn, tm, u, w)


def _rms(x):
    return lax.rsqrt(jnp.mean(x * x, axis=1, keepdims=True) + EPS)


def _rms_bwd(dy, xh, r, g):
    dxh = dy * g
    return r * (dxh - xh * jnp.mean(dxh * xh, axis=1, keepdims=True))


def _hm_rows(tm):
    return pl.BlockSpec((NHD, tm, DHD), lambda i: (0, i, 0))


def _post_mix(apre, o, z, x, w_out, g_a, g_dn):
    T = x.shape[0]

    def body(ap_ref, o_ref, z_ref, x_ref, w_ref, ga_ref, gd_ref, x1_ref, mix_ref):
        ap = ap_ref[...]
        parts = [ap * _rms(ap) * ga_ref[...]]
        zz = z_ref[...]
        for h in range(NHD):
            oh = o_ref[h]
            zh = zz[:, DHD * h:DHD * (h + 1)]
            parts.append(oh * _rms(oh) * gd_ref[...] * (zh * _sigmoid(zh)))
        mix = jnp.concatenate(parts, axis=1).astype(_MXU)
        mix_ref[...] = mix
        x1_ref[...] = x_ref[...] + jnp.dot(mix, w_ref[...], preferred_element_type=F32)

    return pl.pallas_call(
        body, name="post_mix", grid=(T // TM,),
        in_specs=[_rows(TM, AW), _hm_rows(TM), _rows(TM, DW), _rows(TM, D), _full((D, D)), _full((1, AW)),
                  _full((1, DHD))],
        out_specs=[_rows(TM, D), _rows(TM, D)],
        out_shape=[jax.ShapeDtypeStruct((T, D), F32), jax.ShapeDtypeStruct((T, D), _MXU)],
        compiler_params=_cp(("arbitrary",)),
    )(apre, o, z, x, w_out, g_a, g_dn)


def _ffn(x1, tgt, wl_all, g_ffn):
    T = x1.shape[0]
    SH = FF // N_DEV
    nt = (((1,), (1,)), ((), ()))

    def body(x_ref, t_ref, wl_hbm, g_ref,
             dx1_ref, dx1b_ref, h2_ref, act_ref, dgu_ref, dyb_ref, loss_ref, dg_ref, wg, wu, wd, sem):
        @pl.when(pl.program_id(0) == 0)
        def _():
            cps = [pltpu.make_async_copy(wl_hbm.at[dev, pl.ds(128 + SH * k, SH), :], dst.at[pl.ds(SH * dev, SH), :],
                                         sem.at[N_DEV * k + dev])
                   for k, dst in enumerate((wg, wu, wd)) for dev in range(N_DEV)]
            for cp in cps:
                cp.start()
            for cp in cps:
                cp.wait()
            loss_ref[...] = jnp.zeros_like(loss_ref)
            dg_ref[...] = jnp.zeros_like(dg_ref)

        xv = x_ref[...]
        r = _rms(xv)
        xh = xv * r
        gg = g_ref[...]
        h2 = (xh * gg).astype(_MXU)
        h2_ref[...] = h2
        gate = lax.dot_general(h2, wg[...], nt, preferred_element_type=F32)
        up = lax.dot_general(h2, wu[...], nt, preferred_element_type=F32)
        sg = _sigmoid(gate)
        silu = gate * sg
        act = (silu * up).astype(_MXU)
        act_ref[...] = act
        y = xv + jnp.dot(act, wd[...], preferred_element_type=F32)
        err = y - t_ref[...]
        loss_ref[...] += jnp.sum(err * err, axis=0, keepdims=True)
        dy = err * (1.0 / D)
        dyb = dy.astype(_MXU)
        dyb_ref[...] = dyb
        dact = lax.dot_general(dyb, wd[...], nt, preferred_element_type=F32)
        dgate = (dact * up * (sg * (1.0 + gate * (1.0 - sg)))).astype(_MXU)
        dup = (dact * silu).astype(_MXU)
        dgu_ref[:, 0:FF] = dgate
        dgu_ref[:, FF:2 * FF] = dup
        dh2 = (jnp.dot(dgate, wg[...], preferred_element_type=F32)
               + jnp.dot(dup, wu[...], preferred_element_type=F32))
        dg_ref[...] += jnp.sum(dh2 * xh, axis=0, keepdims=True)
        dx1 = dy + _rms_bwd(dh2, xh, r, gg)
        dx1_ref[...] = dx1
        dx1b_ref[...] = dx1.astype(_MXU)

    anyspec = pl.BlockSpec(memory_space=pl.ANY)
    sd = lambda w, dt: jax.ShapeDtypeStruct((T, w), dt)
    return pl.pallas_call(
        body, name="ffn", grid=(T // TM,),
        in_specs=[_rows(TM, D), _rows(TM, D), anyspec, _full((1, D))],
        out_specs=[_rows(TM, D), _rows(TM, D), _rows(TM, D), _rows(TM, FF), _rows(TM, 2 * FF), _rows(TM, D),
                   _full((1, D)), _full((1, D))],
        out_shape=[sd(D, F32), sd(D, _MXU), sd(D, _MXU), sd(FF, _MXU), sd(2 * FF, _MXU), sd(D, _MXU),
                   jax.ShapeDtypeStruct((1, D), F32), jax.ShapeDtypeStruct((1, D), F32)],
        scratch_shapes=[pltpu.VMEM((FF, D), _MXU)] * 3 + [pltpu.SemaphoreType.DMA((3 * N_DEV,))],
        compiler_params=_cp(("arbitrary",)),
    )(x1, tgt, wl_all, g_ffn)


def _mix_bwd(dx1b, w_out, apre, o, z, g_a, g_dn):
    T = dx1b.shape[0]

    def body(dx_ref, w_ref, ap_ref, o_ref, z_ref, ga_ref, gd_ref, dap_ref, do_ref, dz_ref, dga_ref, dgd_ref):
        @pl.when(pl.program_id(0) == 0)
        def _():
            dga_ref[...] = jnp.zeros_like(dga_ref)
            dgd_ref[...] = jnp.zeros_like(dgd_ref)

        dmix = lax.dot_general(dx_ref[...], w_ref[...], (((1,), (1,)), ((), ())), preferred_element_type=F32)
        ap = ap_ref[...]
        ra = _rms(ap)
        ah = ap * ra
        da = dmix[:, 0:AW]
        dga_ref[...] += jnp.sum(da * ah, axis=0, keepdims=True)
        dap_ref[...] = _rms_bwd(da, ah, ra, ga_ref[...])
        zz = z_ref[...]
        gd = gd_ref[...]
        for h in range(NHD):
            cs = slice(DHD * h, DHD * (h + 1))
            dd = dmix[:, AW + DHD * h:AW + DHD * (h + 1)]
            oh = o_ref[h]
            ro = _rms(oh)
            ohh = oh * ro
            zh = zz[:, cs]
            sz = _sigmoid(zh)
            dz_ref[:, cs] = dd * (ohh * gd) * (sz * (1.0 + zh * (1.0 - sz)))
            don = dd * (zh * sz)
            dgd_ref[...] += jnp.sum(don * ohh, axis=0, keepdims=True)
            do_ref[h] = _rms_bwd(don, ohh, ro, gd)

    return pl.pallas_call(
        body, name="mix_bwd", grid=(T // TM,),
        in_specs=[_rows(TM, D), _full((D, D)), _rows(TM, AW), _hm_rows(TM), _rows(TM, DW), _full((1, AW)),
                  _full((1, DHD))],
        out_specs=[_rows(TM, AW), _hm_rows(TM), _rows(TM, DW), _full((1, AW)), _full((1, DHD))],
        out_shape=[jax.ShapeDtypeStruct((T, AW), F32), jax.ShapeDtypeStruct((NHD, T, DHD), F32),
                   jax.ShapeDtypeStruct((T, DW), F32), jax.ShapeDtypeStruct((1, AW), F32),
                   jax.ShapeDtypeStruct((1, DHD), F32)],
        compiler_params=_cp(("arbitrary",)),
    )(dx1b, w_out, apre, o, z, g_a, g_dn)


def _inproj_bwd(dqn, dkn, dv, araw, ddraw, dz, dba, x, dx1, w_in, g_mix, qg_t, kg_t):
    T = x.shape[0]

    def body(dqn_ref, dkn_ref, dv_ref, ar_ref, dd_ref, dz_ref, dba_ref, x_ref, dx1_ref, w_ref, g_ref, qg_ref, kg_ref,
             dx_ref, dp_ref, dgm_ref, dqg_ref, dkg_ref):
        @pl.when(pl.program_id(0) == 0)
        def _():
            dgm_ref[...] = jnp.zeros_like(dgm_ref)
            dqg_ref[...] = jnp.zeros_like(dqg_ref)
            dkg_ref[...] = jnp.zeros_like(dkg_ref)

        bd = _block_ones(AW // 2, DHA)

        def head_norm_bwd(raw, dyn, gain, dg_ref):
            r = _head_rms(raw, bd, DHA)
            xh = raw * r
            dg_ref[...] += jnp.sum(dyn * xh, axis=0, keepdims=True)
            dxh = dyn * gain
            return r * (dxh - xh * (_head_sum(dxh * xh, bd) * (1.0 / DHA)))

        ar = ar_ref[...]
        dq = head_norm_bwd(ar[:, 0:AW], dqn_ref[...] * (DHA ** -0.5), qg_ref[...], dqg_ref)
        dk = head_norm_bwd(ar[:, AW:2 * AW], dkn_ref[...], kg_ref[...], dkg_ref)
        nt = (((1,), (1,)), ((), ()))
        dh = jnp.zeros((TM, D), F32)
        for lo, val in ((0, dq), (AW, dk), (2 * AW, dv_ref[...]), (1536, dd_ref[...]), (3072, dz_ref[...]),
                        (3584, dba_ref[...])):
            vb = val.astype(_MXU)
            wd_ = val.shape[1]
            dp_ref[:, lo:lo + wd_] = vb
            dh = dh + lax.dot_general(vb, w_ref[:, lo:lo + wd_], nt, preferred_element_type=F32)
        xv = x_ref[...]
        r = _rms(xv)
        xh = xv * r
        dgm_ref[...] += jnp.sum(dh * xh, axis=0, keepdims=True)
        dx_ref[...] = dx1_ref[...] + _rms_bwd(dh, xh, r, g_ref[...])

    return pl.pallas_call(
        body, name="inproj_bwd", grid=(T // TM,),
        in_specs=[_rows(TM, AW), _rows(TM, AW), _rows(TM, AW), _rows(TM, 1536), _rows(TM, 1536), _rows(TM, DW),
                  _rows(TM, 8), _rows(TM, D), _rows(TM, D), _full((D, 3592)), _full((1, D)), _full((1, AW)),
                  _full((1, AW))],
        out_specs=[_rows(TM, D), _rows(TM, 3592), _full((1, D)), _full((1, AW)), _full((1, AW))],
        out_shape=[jax.ShapeDtypeStruct((T, D), F32), jax.ShapeDtypeStruct((T, 3592), _MXU),
                   jax.ShapeDtypeStruct((1, D), F32), jax.ShapeDtypeStruct((1, AW), F32),
                   jax.ShapeDtypeStruct((1, AW), F32)],
        compiler_params=_cp(("arbitrary",)),
    )(dqn, dkn, dv, araw, ddraw, dz, dba, x, dx1, w_in, g_mix, qg_t, kg_t)


def _wgrad(a, b, name, tk=512, tn=None, out_dtype=F32, transposed=False):
    T, M = a.shape
    N = b.shape[1]
    tn = N if tn is None else tn
    nk = T // tk

    def body(a_ref, b_ref, o_ref, acc):
        k = pl.program_id(1)

        @pl.when(k == 0)
        def _():
            acc[...] = jnp.zeros_like(acc)

        acc[...] += lax.dot_general(a_ref[...], b_ref[...], (((0,), (0,)), ((), ())), preferred_element_type=F32)

        @pl.when(k == nk - 1)
        def _():
            r = acc[...]
            o_ref[...] = (r.T if transposed else r).astype(out_dtype)

    if transposed:
        out_spec, out_shape = pl.BlockSpec((tn, M), lambda j, k: (j, 0)), (N, M)
    else:
        out_spec, out_shape = pl.BlockSpec((M, tn), lambda j, k: (0, j)), (M, N)
    return pl.pallas_call(
        body, name=name, grid=(N // tn, nk),
        in_specs=[pl.BlockSpec((tk, M), lambda j, k: (k, 0)), pl.BlockSpec((tk, tn), lambda j, k: (k, j))],
        out_specs=out_spec,
        out_shape=jax.ShapeDtypeStruct(out_shape, out_dtype),
        scratch_shapes=[pltpu.VMEM((M, tn), F32)],
        compiler_params=_cp(("arbitrary", "arbitrary")),
    )(a, b)


def _adamw(parts, w, m, v, name, tr):
    K, R, W = parts.shape

    def body(p_ref, w_ref, m_ref, v_ref, g_ref, d_ref, nm_ref, nv_ref):
        g = p_ref[0].astype(F32)
        for k in range(1, K):
            g = g + p_ref[k].astype(F32)
        g_ref[...] = g
        nm = ADAM_B1 * m_ref[...] + (1.0 - ADAM_B1) * g
        nv = ADAM_B2 * v_ref[...] + (1.0 - ADAM_B2) * (g * g)
        nm_ref[...] = nm
        nv_ref[...] = nv
        m_hat = nm / (1.0 - ADAM_B1 ** ADAM_STEP)
        v_hat = nv / (1.0 - ADAM_B2 ** ADAM_STEP)
        d_ref[...] = -ADAM_LR * (m_hat / (jnp.sqrt(v_hat) + ADAM_EPS) + ADAM_WD * w_ref[...])

    o = jax.ShapeDtypeStruct((R, W), F32)
    return pl.pallas_call(
        body, name=name, grid=(R // tr,),
        in_specs=[pl.BlockSpec((K, tr, W), lambda i: (0, i, 0)), _rows(tr, W), _rows(tr, W), _rows(tr, W)],
        out_specs=[_rows(tr, W)] * 4,
        out_shape=[o] * 4,
        compiler_params=_cp(("arbitrary",)),
    )(parts, w, m, v)


SM_ROWS = 136
R_GMIX, R_GFFN, R_QG, R_KG, R_GA, R_GDN, R_ALOG, R_DT, R_LOSS, R_CONV, R_REL = 0, 8, 16, 24, 32, 40, 48, 49, 56, 64, 112


def _small_reduce(gathered):
    def body(p_ref, o_ref):
        s = p_ref[0]
        for k in range(1, N_DEV):
            s = s + p_ref[k]
        o_ref[...] = s
        for r0 in (R_QG, R_KG):
            rs = jnp.sum(s[r0:r0 + 4], axis=0, keepdims=True)
            o_ref[r0:r0 + 1, :] = rs + pltpu.roll(rs, DHA, 1)
        tot = jnp.sum(jnp.sum(s[R_LOSS:R_LOSS + 8], axis=0, keepdims=True), axis=1, keepdims=True)
        o_ref[R_LOSS:R_LOSS + 1, :] = jnp.broadcast_to(tot * (0.5 / D), (1, LANES))

    return pl.pallas_call(
        body, name="small_reduce",
        out_shape=jax.ShapeDtypeStruct((SM_ROWS, LANES), F32),
    )(gathered)


_WIRE = jnp.bfloat16
RA_USED, RA = 449, 464
RL = 128 + 3 * 352


def _pack_rows(parts, rows=None):
    p = jnp.concatenate([t.reshape(-1, D) for t in parts], axis=0) if len(parts) > 1 else parts[0].reshape(-1, D)
    return p if rows is None else jnp.pad(p, ((0, rows - p.shape[0]), (0, 0)))


def _unpack_rows(packed, shapes):
    out, r = [], 0
    for shp in shapes:
        nr = math.prod(shp) // D
        out.append(packed[r:r + nr].reshape(shp))
        r += nr
    return out


def _pad8(t):
    return jnp.pad(t, ((0, (-t.shape[0]) % 8), (0, 0)))


def _pack_lanes(parts):
    rows = []
    for p in parts:
        f = p.reshape(-1)
        pad = (-f.shape[0]) % LANES
        rows.append(jnp.pad(f, (0, pad)).reshape(-1, LANES))
    return jnp.concatenate(rows, axis=0)


def _unpack_lanes(packed, shapes):
    out, r = [], 0
    for shp in shapes:
        n = math.prod(shp)
        nr = -(-n // LANES)
        out.append(packed[r:r + nr].reshape(-1)[:n].reshape(shp))
        r += nr
    return out


def kernel(x, norm_mix_g, w_in, attn_q_norm_g, attn_k_norm_g, rel_bias, attn_out_norm_g, conv_w, a_log, dt_bias, dn_out_norm_g, w_out, norm_ffn_g, w_gate, w_up, w_down, loss_target, m_norm_mix_g, m_w_in, m_attn_q_norm_g, m_attn_k_norm_g, m_rel_bias, m_attn_out_norm_g, m_conv_w, m_a_log, m_dt_bias, m_dn_out_norm_g, m_w_out, m_norm_ffn_g, m_w_gate, m_w_up, m_w_down, v_norm_mix_g, v_w_in, v_attn_q_norm_g, v_attn_k_norm_g, v_rel_bias, v_attn_out_norm_g, v_conv_w, v_a_log, v_dt_bias, v_dn_out_norm_g, v_w_out, v_norm_ffn_g, v_w_gate, v_w_up, v_w_down):
    xs, tgt = x[0], loss_target[0]
    T = xs.shape[0]
    my_idx = 4 * lax.axis_index("x") + 2 * lax.axis_index("y") + lax.axis_index("c")
    late_w = (w_out[0], w_gate[0], w_up[0], w_down[0])
    late_shapes = [w.shape for w in late_w]

    wa_all = _all_gather(_pack_rows([w_in[0].astype(_MXU)], RA), "gather_w_in")
    cw_all = _all_gather(jnp.pad(conv_w[0], ((0, 4), (0, 64))), "gather_conv")
    by_dev = lambda a, k: a.reshape(N_DEV, D, k).transpose(1, 0, 2).reshape(D, N_DEV * k)
    W_in = by_dev(wa_all[:, 0:RA_USED], RA_USED)
    conv_full = cw_all[:, 0:CONV_K, 0:192].transpose(1, 0, 2).reshape(CONV_K, 1536)

    qg_t = jnp.tile(attn_q_norm_g, (1, NHA))
    kg_t = jnp.tile(attn_k_norm_g, (1, NHA))
    z4 = jnp.zeros((1, NHD), F32)
    alog8 = jnp.concatenate([z4, a_log], axis=1)
    dtb8 = jnp.concatenate([z4, dt_bias], axis=1)

    late_t = lambda ts: (ts[0], ts[1].T, ts[2].T, ts[3])
    araw, an, draw, z, ba, hb, wl_all = _inproj(xs, norm_mix_g, W_in, qg_t, kg_t,
                                                _pack_rows([w.astype(_MXU) for w in late_t(late_w)]))
    W_out = wl_all[:, 0:128].reshape(D, D)
    tab, tabt = _bias_tables(rel_bias[0].T)
    apre = _attn_fwd(an, tab)
    bat = ba.T
    dn_args = (draw, conv_full, ba, bat, alog8, dtb8, alog8.T, dtb8.T)
    u, w, kd, tm, wq, km, mq, wt, elb = _dn_prep(*dn_args)
    o, vn, sn = _dn_scan(u, wq, km, elb)
    x1, mix = _post_mix(apre, o, z, xs, W_out, attn_out_norm_g, dn_out_norm_g)

    dx1, dx1b, h2, act, dgu, dyb, loss_row, dgffn = _ffn(x1, tgt, wl_all, norm_ffn_g)

    by_cols = lambda g, k: g.reshape(D, N_DEV, k).transpose(1, 0, 2).reshape(N_DEV, -1, D)
    gW_out = _wgrad(mix, dx1b, "wgrad_out", out_dtype=_WIRE)
    gW_gu_t = _wgrad(h2, dgu, "wgrad_gate_up", tn=FF, out_dtype=_WIRE, transposed=True)
    gW_down = _wgrad(dyb, act, "wgrad_down", out_dtype=_WIRE, transposed=True)
    send_late = jnp.concatenate(
        [gW_out.reshape(N_DEV, 128, D), gW_gu_t[0:FF].reshape(N_DEV, 352, D), gW_gu_t[FF:].reshape(N_DEV, 352, D),
         gW_down.reshape(N_DEV, 352, D)], axis=1)

    dap, do, dz, dga, dgdn = _mix_bwd(dx1b, W_out, apre, o, z, attn_out_norm_g, dn_out_norm_g)
    dqn, dkn, dv, dtabt, recv_late = _attn_bwd(an, dap, tabt, send_late)
    drel = _bias_grad(dtabt)
    du, dw, dqd, dkd, dgx = _dn_scan_bwd(do, mq, kd, wt, sn, vn, elb)
    ddraw, dba, sm, dcw = _dn_post_bwd(*dn_args, du, dw, dqd, dkd, dgx, do, vn, tm, u, w)
    gx, dproj, dgmix, dqg, dkg = _inproj_bwd(dqn, dkn, dv, araw, ddraw, dz, dba, xs, dx1, W_in, norm_mix_g, qg_t, kg_t)

    gW_in = _wgrad(hb, dproj, "wgrad_in", tk=256, out_dtype=_WIRE)
    send_in = jnp.pad(by_cols(gW_in, RA_USED), ((0, 0), (0, RA - RA_USED), (0, 0)))
    recv_in = _all_to_all(send_in, "scatter_w_in")
    late_m = (m_w_out[0], m_w_gate[0], m_w_up[0], m_w_down[0])
    late_v = (v_w_out[0], v_w_gate[0], v_w_up[0], v_w_down[0])
    outs_late = _adamw(recv_late, _pack_rows(late_t(late_w)), _pack_rows(late_t(late_m)), _pack_rows(late_t(late_v)),
                       "adamw_late", 32)
    outs_in = _adamw(recv_in, _pack_rows([w_in[0]], RA), _pack_rows([m_w_in[0]], RA), _pack_rows([v_w_in[0]], RA),
                     "adamw_w_in", 16)
    late_t_shapes = [t.shape for t in late_t(late_w)]
    big = [[_unpack_rows(a, [w_in[0].shape])[0]] + list(late_t(_unpack_rows(b, late_t_shapes)))
           for a, b in zip(outs_in, outs_late)]
    bg, bd_, bm, bv = big

    partial = jnp.concatenate(
        [dgmix.reshape(8, LANES), dgffn.reshape(8, LANES), _pad8(dqg.reshape(4, LANES)), _pad8(dkg.reshape(4, LANES)),
         _pad8(dga.reshape(4, LANES)), _pad8(dgdn), sm, loss_row.reshape(8, LANES),
         dcw[0:CONV_K].reshape(48, LANES), drel.reshape(24, LANES)], axis=0)
    S = _small_reduce(_all_gather(partial, "gather_small"))
    loss = S[R_LOSS, 0]
    g_conv = lax.dynamic_slice(S[R_CONV:R_CONV + 48].reshape(CONV_K, 1536), (0, 192 * my_idx), (CONV_K, 192))
    sg = [S[R_GMIX:R_GMIX + 8].reshape(1, D), S[R_QG:R_QG + 1, 0:DHA], S[R_KG:R_KG + 1, 0:DHA],
          S[R_REL:R_REL + 24].reshape(NHA, 384)[:, 0:257].T, S[R_GA:R_GA + 4].reshape(1, AW), g_conv,
          S[R_ALOG:R_ALOG + 1, NHD:2 * NHD], S[R_DT:R_DT + 1, NHD:2 * NHD], S[R_GDN:R_GDN + 1], S[R_GFFN:R_GFFN + 8].reshape(1, D)]
    sw = [norm_mix_g, attn_q_norm_g, attn_k_norm_g, rel_bias[0], attn_out_norm_g, conv_w[0], a_log, dt_bias, dn_out_norm_g, norm_ffn_g]
    smm = [m_norm_mix_g, m_attn_q_norm_g, m_attn_k_norm_g, m_rel_bias[0], m_attn_out_norm_g, m_conv_w[0], m_a_log, m_dt_bias, m_dn_out_norm_g, m_norm_ffn_g]
    svv = [v_norm_mix_g, v_attn_q_norm_g, v_attn_k_norm_g, v_rel_bias[0], v_attn_out_norm_g, v_conv_w[0], v_a_log, v_dt_bias, v_dn_out_norm_g, v_norm_ffn_g]
    s_shapes = [t.shape for t in sw]
    pk = lambda ts: _pack_lanes(ts)
    pg = pk(sg)
    padr = (-pg.shape[0]) % 8
    padz = lambda t: jnp.pad(t, ((0, padr), (0, 0)))
    s_out = _adamw(padz(pg)[None], padz(pk(sw)), padz(pk(smm)), padz(pk(svv)), "adamw_small", pg.shape[0] + padr)
    s_g, s_d, s_m, s_v = (_unpack_lanes(t, s_shapes) for t in s_out)

    lead = lambda t: t[None]
    def ordered(small, big):
        nm, q, k, rel, ao, cw, al, dtb, dno, nf = small
        wi, wo, wgt, wu, wdn = big
        return [nm, lead(wi), q, k, lead(rel), ao, lead(cw), al, dtb, dno, lead(wo), nf, lead(wgt), lead(wu), lead(wdn)]
    outs = [loss, gx[None]]
    for small, big in ((s_g, bg), (s_d, bd_), (s_m, bm), (s_v, bv)):
        outs += ordered(small, big)
    return tuple(outs)
```

```python
import functools
import math

import jax
import jax.numpy as jnp
from jax import lax
from jax.experimental import pallas as pl
from jax.experimental.pallas import tpu as pltpu

F32 = jnp.float32
BF16 = jnp.bfloat16
_MXU = jnp.bfloat16

D = 1024
AW = 512
NHA = 8
DHA = 64
CH = 64
BAND = 9
NHD = 4
DHD = 128
DW = 512
FF = 2816
EPS = 1e-6
NEG = -1e30
N_DEV = 8
LANES = 128
VMEM_LIMIT = 56 * 1024 * 1024

ADAM_LR = 0.001
ADAM_B1 = 0.9
ADAM_B2 = 0.999
ADAM_EPS = 1e-08
ADAM_WD = 0.01
ADAM_STEP = 10

MESH_T = pl.DeviceIdType.MESH


def _cp(sem=None, vmem=VMEM_LIMIT):
    kw = dict(vmem_limit_bytes=vmem)
    if sem is not None:
        kw["dimension_semantics"] = sem
    return pltpu.CompilerParams(**kw)


def _dot(a, b):
    return jnp.dot(a.astype(_MXU), b.astype(_MXU), preferred_element_type=F32)


def _dot_nt(a, b):
    return lax.dot_general(a.astype(_MXU), b.astype(_MXU), (((1,), (1,)), ((), ())), preferred_element_type=F32)


def _dot_tn(a, b):
    return lax.dot_general(a.astype(_MXU), b.astype(_MXU), (((0,), (0,)), ((), ())), preferred_element_type=F32)


def _split2(x):
    hi = x.astype(BF16)
    lo = (x - hi.astype(F32)).astype(BF16)
    return hi, lo


def _dot_x2(x, ones_b):
    hi, lo = _split2(x)
    return jnp.dot(hi, ones_b, preferred_element_type=F32) + jnp.dot(lo, ones_b, preferred_element_type=F32)


def _dot_x2_nt(x, ones_b):
    hi, lo = _split2(x)
    dn = (((1,), (1,)), ((), ()))
    return lax.dot_general(hi, ones_b, dn, preferred_element_type=F32) + lax.dot_general(
        lo, ones_b, dn, preferred_element_type=F32)


def _iota(shape, dim):
    return lax.broadcasted_iota(jnp.int32, shape, dim)


def _block_ones(n, blk, dtype=BF16):
    r, c = _iota((n, n), 0), _iota((n, n), 1)
    return jnp.where((r // blk) == (c // blk), 1.0, 0.0).astype(dtype)


def _sigmoid(x):
    return 1.0 / (1.0 + jnp.exp(-x))


def _softplus(x):
    return jnp.maximum(x, 0.0) + jnp.log(1.0 + jnp.exp(-jnp.abs(x)))


def _col(x, k):
    lane = _iota(x.shape, 1)
    return jnp.sum(jnp.where(lane == k, x, 0.0), axis=1, keepdims=True)


def _row(x, k):
    sub = _iota(x.shape, 0)
    return jnp.sum(jnp.where(sub == k, x, 0.0), axis=0, keepdims=True)


def _my_pos():
    return lax.axis_index("x"), lax.axis_index("y"), lax.axis_index("c")


def _all_gather(x2d, name):
    R, W = x2d.shape

    def body(x_ref, out_ref, send_sems, recv_sems, local_sem):
        ag = _Gather(x_ref, out_ref, send_sems, recv_sems, local_sem)
        ag.start()
        ag.forward()
        ag.finish()

    return pl.pallas_call(
        body, name=name,
        out_shape=jax.ShapeDtypeStruct((N_DEV, R, W), x2d.dtype),
        in_specs=[pl.BlockSpec(memory_space=pl.ANY)],
        out_specs=pl.BlockSpec(memory_space=pl.ANY),
        scratch_shapes=_COMM_SEMS,
    )(x2d)


_COMM_SEMS = [pltpu.SemaphoreType.DMA((7,)), pltpu.SemaphoreType.DMA((7,)), pltpu.SemaphoreType.DMA]


class _Gather:
    def __init__(self, x_ref, out_ref, send_sems, recv_sems, local_sem):
        x, y, c = _my_pos()
        me, sibling = (x, y, c), (x, y, 1 - c)
        chips = [(1 - x, y), (x, 1 - y), (1 - x, 1 - y)]

        def slot(px, py, pc):
            return out_ref.at[4 * px + 2 * py + pc]

        def copy(k, block, to, src=None):
            return pltpu.make_async_remote_copy(
                src_ref=slot(*block) if src is None else src, dst_ref=slot(*block),
                send_sem=send_sems.at[k], recv_sem=recv_sems.at[k], device_id=to, device_id_type=MESH_T)

        self.mine = pltpu.make_async_copy(x_ref, slot(*me), local_sem)
        self.first = [copy(0, me, sibling, src=x_ref)]
        self.first += [copy(1 + j, me, (*chip, c), src=x_ref) for j, chip in enumerate(chips)]
        self.passed = [copy(4 + j, (*chip, c), sibling) for j, chip in enumerate(chips)]
        self.from_chips = [copy(1 + j, (*chip, c), me) for j, chip in enumerate(chips)]
        self.from_sibling = [copy(0, sibling, me)] + [copy(4 + j, (*chip, 1 - c), me) for j, chip in enumerate(chips)]

    def start(self):
        self.mine.start()
        for cp in self.first:
            cp.start()

    def forward(self):
        for arrived, onward in zip(self.from_chips, self.passed):
            arrived.wait_recv()
            onward.start()

    def finish(self):
        for cp in self.from_sibling:
            cp.wait_recv()
        for cp in self.first + self.passed:
            cp.wait_send()
        self.mine.wait()


class _Scatter:
    def __init__(self, s_ref, r_ref, send_sems, recv_sems, local_sem):
        x, y, c = _my_pos()
        self.mine = pltpu.make_async_copy(s_ref.at[4 * x + 2 * y + c], r_ref.at[0], local_sem)
        self.copies = []
        for m in range(1, N_DEV):
            px = x ^ ((m >> 2) & 1)
            py = y ^ ((m >> 1) & 1)
            pc = c ^ (m & 1)
            self.copies.append(pltpu.make_async_remote_copy(
                src_ref=s_ref.at[4 * px + 2 * py + pc], dst_ref=r_ref.at[m],
                send_sem=send_sems.at[m - 1], recv_sem=recv_sems.at[m - 1],
                device_id=(px, py, pc), device_id_type=MESH_T))

    def start(self):
        self.mine.start()
        for cp in self.copies:
            cp.start()

    def finish(self):
        for cp in self.copies:
            cp.wait_recv()
        for cp in self.copies:
            cp.wait_send()
        self.mine.wait()


def _all_to_all(send, name):
    def body(s_ref, r_ref, send_sems, recv_sems, local_sem):
        sc = _Scatter(s_ref, r_ref, send_sems, recv_sems, local_sem)
        sc.start()
        sc.finish()

    return pl.pallas_call(
        body, name=name,
        out_shape=jax.ShapeDtypeStruct(send.shape, send.dtype),
        in_specs=[pl.BlockSpec(memory_space=pl.ANY)],
        out_specs=pl.BlockSpec(memory_space=pl.ANY),
        scratch_shapes=_COMM_SEMS,
    )(send)


TM = 256
TG = 512


def _full(shape):
    nd = len(shape)
    return pl.BlockSpec(shape, lambda i: (0,) * nd)


def _rows(tm, w):
    return pl.BlockSpec((tm, w), lambda i: (i, 0))


def _head_sum(x, bd):
    return jnp.concatenate([_dot_x2(x[:, 0:256], bd), _dot_x2(x[:, 256:512], bd)], axis=1)


def _head_rms(x, bd, width):
    return lax.rsqrt(_head_sum(x * x, bd) * (1.0 / width) + EPS)


def _inproj(x, g_mix, w_in, qg_t, kg_t, later_w):
    T = x.shape[0]
    nt = T // TM

    def body(x_ref, g_ref, w_ref, qg_ref, kg_ref, lw_ref, araw_ref, an_ref, draw_ref, z_ref, ba_ref, h_ref, lw_all,
             send_sems, recv_sems, local_sem):
        i = pl.program_id(0)
        ag = _Gather(lw_ref, lw_all, send_sems, recv_sems, local_sem)
        pl.when(i == 0)(ag.start)
        pl.when(i == nt // 2)(ag.forward)
        xv = x_ref[...]
        r = lax.rsqrt(jnp.mean(xv * xv, axis=1, keepdims=True) + EPS)
        h = (xv * r * g_ref[...]).astype(_MXU)
        h_ref[...] = h
        a = jnp.dot(h, w_ref[:, 0:1536], preferred_element_type=F32)
        araw_ref[...] = a
        bd = _block_ones(AW // 2, DHA)
        q = a[:, 0:AW]
        k = a[:, AW:2 * AW]
        qn = q * _head_rms(q, bd, DHA) * (qg_ref[...] * (DHA ** -0.5))
        kn = k * _head_rms(k, bd, DHA) * kg_ref[...]
        an_ref[:, 0:AW] = qn.astype(_MXU)
        an_ref[:, AW:2 * AW] = kn.astype(_MXU)
        an_ref[:, 2 * AW:3 * AW] = a[:, 2 * AW:3 * AW].astype(_MXU)
        draw_ref[...] = jnp.dot(h, w_ref[:, 1536:3072], preferred_element_type=F32)
        z_ref[...] = jnp.dot(h, w_ref[:, 3072:3584], preferred_element_type=F32)
        ba_ref[...] = jnp.dot(h, w_ref[:, 3584:3592], preferred_element_type=F32)
        pl.when(i == nt - 1)(ag.finish)

    anyspec = pl.BlockSpec(memory_space=pl.ANY)
    return pl.pallas_call(
        body, name="inproj", grid=(nt,),
        in_specs=[_rows(TM, D), _full((1, D)), _full((D, 3592)), _full((1, AW)), _full((1, AW)), anyspec],
        out_specs=[_rows(TM, 1536), _rows(TM, 1536), _rows(TM, 1536), _rows(TM, DW), _rows(TM, 8), _rows(TM, D),
                   anyspec],
        out_shape=[jax.ShapeDtypeStruct((T, 1536), F32), jax.ShapeDtypeStruct((T, 1536), _MXU),
                   jax.ShapeDtypeStruct((T, 1536), F32), jax.ShapeDtypeStruct((T, DW), F32),
                   jax.ShapeDtypeStruct((T, 8), F32), jax.ShapeDtypeStruct((T, D), _MXU),
                   jax.ShapeDtypeStruct((N_DEV,) + later_w.shape, later_w.dtype)],
        scratch_shapes=_COMM_SEMS,
        compiler_params=_cp(("arbitrary",)),
    )(x, g_mix, w_in, qg_t, kg_t, later_w)


TQ = 256
TW = 768
T_LO, T_HI = 65, 256
VAR0 = 384
TOEP = 1024


def _bias_tables(rb_t):
    def body(rb_ref, tab_ref, tabt_ref):
        v = pl.program_id(0)
        h = pl.program_id(1)
        n = _iota((8, TOEP), 1)
        first_key = jnp.where(v < 2, 512 - TQ * v, 0)

        def line(m):
            idx = jnp.clip(512 - m, -128, 128) + 128
            return lax.fori_loop(T_LO, T_HI, lambda t, a: jnp.where(idx == t, rb_ref[h, t], a),
                                 jnp.full((8, TOEP), rb_ref[h, T_HI], F32))[0:1, :]

        def band(r, j):
            return ((j >> 6) >= (r >> 6)) & ((j >> 6) <= (r >> 6) + 8) & (j >= first_key)

        g = line(jnp.where(n < TW, n, n - TOEP))
        tab = pltpu.roll(jnp.broadcast_to(g, (TQ, TOEP)), 0, 1, stride=1, stride_axis=0)[:, 0:TW]
        tab_ref[0, 0] = jnp.where(band(_iota((TQ, TW), 0), _iota((TQ, TW), 1)), tab, NEG)
        gt = line(jnp.where(n < TQ, -n, TOEP - n))
        tabt = pltpu.roll(jnp.broadcast_to(gt, (TW, TOEP)), 0, 1, stride=1, stride_axis=0)[:, 0:TQ]
        tabt_ref[0, 0] = jnp.where(band(_iota((TW, TQ), 1), _iota((TW, TQ), 0)), tabt, NEG)

    return pl.pallas_call(
        body, name="bias_tables", grid=(3, NHA),
        in_specs=[pl.BlockSpec(memory_space=pltpu.SMEM)],
        out_specs=[pl.BlockSpec((1, 1, TQ, TW), lambda v, h: (v, h, 0, 0)),
                   pl.BlockSpec((1, 1, TW, TQ), lambda v, h: (v, h, 0, 0))],
        out_shape=[jax.ShapeDtypeStruct((3, NHA, TQ, TW), F32), jax.ShapeDtypeStruct((3, NHA, TW, TQ), F32)],
        compiler_params=_cp(("arbitrary", "arbitrary")),
    )(rb_t)


def _bias_grad(dtabt):
    def body(d_ref, o_ref):
        a, b = _iota((TQ, TQ), 0), _iota((TQ, TQ), 1)
        anti = jnp.where(a + b == TQ - 1, 1.0, 0.0).astype(BF16)
        drev = sum(jnp.dot(t, anti, preferred_element_type=F32) for t in _split3(d_ref[0]))
        wide = jnp.concatenate([drev, jnp.zeros((TW, TOEP - TQ), F32)], axis=1)
        cols = jnp.sum(pltpu.roll(wide, 0, 1, stride=1, stride_axis=0), axis=0, keepdims=True)
        c = _iota((TOEP, VAR0), 0)
        idx = jnp.clip(512 + TQ - 1 - c, -128, 128) + 128
        onehot = jnp.where(idx == _iota((TOEP, VAR0), 1), 1.0, 0.0).astype(BF16)
        cols8 = jnp.broadcast_to(cols, (8, TOEP))
        o_ref[0] = sum(jnp.dot(t, onehot, preferred_element_type=F32) for t in _split3(cols8))[0:1, :]

    return pl.pallas_call(
        body, name="bias_grad", grid=(NHA,),
        in_specs=[pl.BlockSpec((1, TW, TQ), lambda h: (h, 0, 0))],
        out_specs=pl.BlockSpec((1, 1, VAR0), lambda h: (h, 0, 0)),
        out_shape=jax.ShapeDtypeStruct((NHA, 1, VAR0), F32),
        compiler_params=_cp(("arbitrary",)),
    )(dtabt)


def _kv_spec(col, back):
    return pl.BlockSpec((TQ, AW), lambda i: (jnp.maximum(i - back, 0), col))


def _attn_fwd(an, tab):
    T = an.shape[0]

    def body(q_ref, k2_ref, k1_ref, k0_ref, v2_ref, v1_ref, v0_ref, tab_ref, o_ref):
        i = pl.program_id(0)
        kwin = jnp.concatenate([k2_ref[...], k1_ref[...], k0_ref[...]], axis=0)
        vwin = jnp.concatenate([v2_ref[...], v1_ref[...], v0_ref[...]], axis=0)
        q = q_ref[...]
        lo_half = _iota((TQ, LANES), 1) < DHA
        for p in range(NHA // 2):
            sl = slice(LANES * p, LANES * (p + 1))
            qp, kp, vp = q[:, sl], kwin[:, sl], vwin[:, sl]
            outs = []
            for half in range(2):
                mask = lo_half if half == 0 else jnp.logical_not(lo_half)
                qm = jnp.where(mask, qp, jnp.zeros_like(qp))
                s = _dot_nt(qm, kp) + tab_ref[0, 2 * p + half]
                m = jnp.max(s, axis=1, keepdims=True)
                e = jnp.exp(s - m)
                l = jnp.sum(e, axis=1, keepdims=True)
                outs.append(_dot(e, vp) / l)
            o_ref[:, sl] = jnp.where(lo_half, outs[0], outs[1])

    return pl.pallas_call(
        body, name="attn_fwd", grid=(T // TQ,),
        in_specs=[pl.BlockSpec((TQ, AW), lambda i: (i, 0)),
                  _kv_spec(1, 2), _kv_spec(1, 1), _kv_spec(1, 0), _kv_spec(2, 2), _kv_spec(2, 1), _kv_spec(2, 0),
                  pl.BlockSpec((1, NHA, TQ, TW), lambda i: (jnp.minimum(i, 2), 0, 0, 0))],
        out_specs=_rows(TQ, AW),
        out_shape=jax.ShapeDtypeStruct((T, AW), F32),
        compiler_params=_cp(("arbitrary",)),
    )(an, an, an, an, an, an, an, tab)


def _attn_bwd(an, dout, tabt, send):
    T = an.shape[0]
    nq = T // TQ

    def qi(i):
        return jnp.minimum(i, nq - 1)

    def kv_spec(col, back):
        return pl.BlockSpec((TQ, AW), lambda i: (jnp.maximum(qi(i) - back, 0), col))

    def body(q_ref, do_ref, k2_ref, k1_ref, k0_ref, v2_ref, v1_ref, v0_ref, tabt_ref, send_ref,
             dq_ref, dk_ref, dv_ref, dtab_ref, recv_ref, dk_acc, dv_acc, send_sems, recv_sems, local_sem):
        i = pl.program_id(0)
        sc = _Scatter(send_ref, recv_ref, send_sems, recv_sems, local_sem)
        pl.when(i == 0)(sc.start)

        @pl.when(i == 0)
        def _():
            dtab_ref[...] = jnp.zeros_like(dtab_ref)

        new = i % 3
        dk_acc[new] = jnp.zeros((TQ, AW), F32)
        dv_acc[new] = jnp.zeros((TQ, AW), F32)

        @pl.when(i < nq)
        def _():
            kwin = jnp.concatenate([k2_ref[...], k1_ref[...], k0_ref[...]], axis=0)
            vwin = jnp.concatenate([v2_ref[...], v1_ref[...], v0_ref[...]], axis=0)
            q = q_ref[...]
            do = do_ref[...].astype(_MXU)
            lo_half = _iota((TQ, LANES), 1) < DHA
            for p in range(NHA // 2):
                sl = slice(LANES * p, LANES * (p + 1))
                qp, kp, vp, dop = q[:, sl], kwin[:, sl], vwin[:, sl], do[:, sl]
                dq_pair = jnp.zeros((TQ, LANES), F32)
                dk_pair = jnp.zeros((TW, LANES), F32)
                dv_pair = jnp.zeros((TW, LANES), F32)
                for half in range(2):
                    h = 2 * p + half
                    mask = lo_half if half == 0 else jnp.logical_not(lo_half)
                    qm = jnp.where(mask, qp, jnp.zeros_like(qp))
                    dom = jnp.where(mask, dop, jnp.zeros_like(dop))
                    st = _dot_nt(kp, qm) + tabt_ref[0, h]
                    m = jnp.max(st, axis=0, keepdims=True)
                    e = jnp.exp(st - m)
                    pt = e * (1.0 / jnp.sum(e, axis=0, keepdims=True))
                    dpt = _dot_nt(vp, dom)
                    delta = jnp.sum(pt * dpt, axis=0, keepdims=True)
                    dst = pt * (dpt - delta)
                    dtab_ref[h] += dst
                    dsb = dst.astype(_MXU)
                    dv_pair += _dot(pt, dom)
                    dk_pair += _dot(dsb, qm)
                    dq_pair += jnp.where(mask, _dot_tn(dsb, kp), 0.0)
                dq_ref[:, sl] = dq_pair
                for w in range(3):
                    slot = (i + 1 + w) % 3
                    rows = slice(TQ * w, TQ * (w + 1))
                    dk_acc[slot, :, sl] += dk_pair[rows]
                    dv_acc[slot, :, sl] += dv_pair[rows]

        @pl.when(i >= 2)
        def _():
            done = (i + 1) % 3
            dk_ref[...] = dk_acc[done]
            dv_ref[...] = dv_acc[done]

        pl.when(i == nq + 1)(sc.finish)

    back2 = pl.BlockSpec((TQ, AW), lambda i: (jnp.maximum(i - 2, 0), 0))
    anyspec = pl.BlockSpec(memory_space=pl.ANY)
    return pl.pallas_call(
        body, name="attn_bwd", grid=(nq + 2,),
        in_specs=[pl.BlockSpec((TQ, AW), lambda i: (qi(i), 0)), pl.BlockSpec((TQ, AW), lambda i: (qi(i), 0)),
                  kv_spec(1, 2), kv_spec(1, 1), kv_spec(1, 0), kv_spec(2, 2), kv_spec(2, 1), kv_spec(2, 0),
                  pl.BlockSpec((1, NHA, TW, TQ), lambda i: (jnp.minimum(i, 2), 0, 0, 0)), anyspec],
        out_specs=[pl.BlockSpec((TQ, AW), lambda i: (qi(i), 0)), back2, back2, _full((NHA, TW, TQ)), anyspec],
        out_shape=[jax.ShapeDtypeStruct((T, AW), F32), jax.ShapeDtypeStruct((T, AW), F32),
                   jax.ShapeDtypeStruct((T, AW), F32), jax.ShapeDtypeStruct((NHA, TW, TQ), F32),
                   jax.ShapeDtypeStruct(send.shape, send.dtype)],
        scratch_shapes=[pltpu.VMEM((3, TQ, AW), F32), pltpu.VMEM((3, TQ, AW), F32)] + _COMM_SEMS,
        compiler_params=_cp(("arbitrary",)),
    )(an, dout, an, an, an, an, an, an, tabt, send)


GR = 128
NG = TG // GR
CPT = TG // CH
CONV_K = 4


def _split3(x):
    a = x.astype(BF16)
    r = x - a.astype(F32)
    b = r.astype(BF16)
    c = (r - b.astype(F32)).astype(BF16)
    return a, b, c


def _ones_dot(ones_b, x):
    return sum(jnp.dot(ones_b, t, preferred_element_type=F32) for t in _split3(x))


def _dot_ones_nt(x, ones_b):
    dn = (((1,), (1,)), ((), ()))
    return sum(lax.dot_general(t, ones_b, dn, preferred_element_type=F32) for t in _split3(x))


def _dn_masks():
    r, c = _iota((GR, GR), 0), _iota((GR, GR), 1)
    same = (r >> 6) == (c >> 6)
    one = lambda m: jnp.where(m, 1.0, 0.0).astype(BF16)
    return dict(
        tril=same & (c <= r), strict=same & (c < r), triu=same & (c >= r), strict_u=same & (c > r),
        tril_b=one(same & (c <= r)), triu_b=one(same & (c >= r)), blk_b=one(same), eye_b=one(r == c),
        eye=jnp.where(r == c, 1.0, 0.0).astype(F32),
        fold_b=one((_iota((GR, CH), 0) & (CH - 1)) == _iota((GR, CH), 1)),
        last=(_iota((GR, 1), 0) & (CH - 1)) == CH - 1,
    )


def _shift_down(x, halo, k):
    if k == 0:
        return x
    xs = pltpu.roll(x, k, 0)
    hs = pltpu.roll(halo, k, 0)
    top = jnp.where(_iota(halo.shape, 0) < k, hs, xs[0:8])
    return jnp.concatenate([top, xs[8:]], axis=0)


def _shift_up(x, halo, k):
    if k == 0:
        return x
    n = x.shape[0]
    xs = pltpu.roll(x, n - k, 0)
    hs = pltpu.roll(halo, 8 - k, 0)
    bot = jnp.where(_iota(halo.shape, 0) >= 8 - k, hs, xs[n - 8:n])
    return jnp.concatenate([xs[0:n - 8], bot], axis=0)


def _conv(x, halo, w):
    y = x * w[CONV_K - 1:CONV_K, :]
    for k in range(1, CONV_K):
        y = y + _shift_down(x, halo, k) * w[CONV_K - 1 - k:CONV_K - k, :]
    return y


def _tri_inv(lmats, eye):
    ps = [-m for m in lmats]
    rs = [eye + p for p in ps]
    for _ in range(5):
        ps = [_dot(p, p) for p in ps]
        rs = [r + _dot(r, p) for r, p in zip(rs, ps)]
    return rs


def _gate_terms(ba_g, bat_g, alog8, dtb8, alog8t, dtb8t, K):
    g8 = -jnp.exp(alog8) * _softplus(ba_g + dtb8)
    g8t = -jnp.exp(alog8t) * _softplus(bat_g + dtb8t)
    gc8 = _ones_dot(K["tril_b"], g8)
    gl8 = _ones_dot(K["blk_b"], g8)
    gcrow8 = _dot_ones_nt(g8t, K["tril_b"])
    return g8, gc8, gl8, gcrow8


def _dn_heads(c_tile, rows, beta8, gc8, gl8, gcrow8, K, pre=None):
    ds = [_dn_head(c_tile, rows, h, beta8, gc8, gl8, gcrow8, K) for h in range(NHD)]
    if pre is None:
        for d, tm in zip(ds, _tri_inv([d["lmat"] for d in ds], K["eye"])):
            d.update(tm=tm, u=_dot(tm, d["vb"]), w=_dot(tm, d["kg"]))
    else:
        for d, (tm, u, w) in zip(ds, pre):
            d.update(tm=tm, u=u, w=w)
    return ds


def _dn_head(c_tile, rows, h, beta8, gc8, gl8, gcrow8, K):
    qr = c_tile[rows, DHD * h:DHD * (h + 1)]
    kr = c_tile[rows, DW + DHD * h:DW + DHD * (h + 1)]
    v = c_tile[rows, 2 * DW + DHD * h:2 * DW + DHD * (h + 1)]
    rq = lax.rsqrt(jnp.sum(qr * qr, axis=1, keepdims=True) + EPS)
    rk = lax.rsqrt(jnp.sum(kr * kr, axis=1, keepdims=True) + EPS)
    qh, kn = qr * rq, kr * rk
    qn = qh * (DHD ** -0.5)
    beta = _col(beta8, h)
    gccol, glcol, gcrow = _col(gc8, NHD + h), _col(gl8, NHD + h), _row(gcrow8, NHD + h)
    diff = gccol - gcrow
    gam_m = jnp.exp(jnp.where(K["tril"], diff, NEG))
    gam = jnp.exp(gccol)
    egl = jnp.exp(glcol - gccol)
    kb, vb = kn * beta, v * beta
    kg = kb * gam
    pl_ = _dot_nt(kb, kn)
    lmat = jnp.where(K["strict"], pl_ * gam_m, 0.0)
    pm = _dot_nt(qn, kn)
    mm = pm * gam_m
    return dict(qr=qr, kr=kr, v=v, rq=rq, rk=rk, qh=qh, qn=qn, kn=kn, beta=beta, diff=diff, gam_m=gam_m, gam=gam,
                egl=egl, el=jnp.exp(glcol), kb=kb, vb=vb, kg=kg, pl=pl_, pm=pm, lmat=lmat, mm=mm,
                qd=qn * gam, kd=kn * egl)


def _halo_prev(width):
    return pl.BlockSpec((8, width), lambda i: (jnp.maximum(i * (TG // 8) - 1, 0), 0))


def _dn_prep(draw, conv_w, ba, bat, alog8, dtb8, alog8t, dtb8t):
    T = draw.shape[0]
    nb = T // TG
    hm = lambda w, dt: jax.ShapeDtypeStruct((NHD, T, w), dt)
    hm_spec = lambda w: pl.BlockSpec((NHD, TG, w), lambda i: (0, i, 0))
    pc = lambda r, c: jax.ShapeDtypeStruct((NHD, T // CH, r, c), _MXU)
    pc_spec = lambda r, c: pl.BlockSpec((NHD, CPT, r, c), lambda i: (0, i, 0, 0))

    def body(x_ref, halo_ref, cw_ref, ba_ref, bat_ref, al_ref, dt_ref, alt_ref, dtt_ref,
             u_ref, w_ref, kd_ref, tm_ref, wq_ref, km_ref, mq_ref, wt_ref, elb_ref, cv_ref):
        i = pl.program_id(0)
        K = _dn_masks()
        halo = jnp.where(i > 0, halo_ref[...], 0.0)
        cv = _conv(x_ref[...], halo, cw_ref[...])
        cv_ref[...] = cv
        c_tile = cv * _sigmoid(cv)
        eye128 = jnp.where(_iota((DHD, DHD), 0) == _iota((DHD, DHD), 1), 1.0, 0.0).astype(_MXU)
        for g in range(NG):
            rows = slice(GR * g, GR * (g + 1))
            ba_g = ba_ref[rows, :]
            _, gc8, gl8, gcrow8 = _gate_terms(ba_g, bat_ref[:, rows], al_ref[...], dt_ref[...], alt_ref[...],
                                              dtt_ref[...], K)
            beta8 = _sigmoid(ba_g)
            for h, d in enumerate(_dn_heads(c_tile, rows, beta8, gc8, gl8, gcrow8, K)):
                gam_t = jnp.exp(jnp.where(K["triu"], -d["diff"], NEG))
                tm_ref[h, rows, :] = d["tm"].astype(_MXU)
                mmt = _dot_nt(d["kn"], d["qn"]) * gam_t
                u_ref[h, rows, :] = d["u"]
                w_ref[h, rows, :] = d["w"].astype(_MXU)
                kd_ref[h, rows, :] = d["kd"].astype(_MXU)
                mc = _dot(d["mm"], K["fold_b"])
                mct = _dot(mmt, K["fold_b"])
                elb = jnp.broadcast_to(d["el"], (GR, DHD))
                for cc in range(GR // CH):
                    ch = slice(CH * cc, CH * (cc + 1))
                    n = (GR // CH) * g + cc
                    wq_ref[h, n, 0:CH, :] = d["w"][ch].astype(_MXU)
                    wq_ref[h, n, CH:2 * CH, :] = d["qd"][ch].astype(_MXU)
                    km_ref[h, n, 0:DHD, :] = _dot_nt(eye128, d["kd"][ch]).astype(_MXU)
                    km_ref[h, n, DHD:DHD + CH, :] = mc[ch].astype(_MXU)
                    mq_ref[h, n, 0:CH, :] = mct[ch].astype(_MXU)
                    mq_ref[h, n, CH:CH + DHD, :] = _dot_nt(eye128, d["qd"][ch]).astype(_MXU)
                    wt_ref[h, n] = _dot_nt(eye128, d["w"][ch]).astype(_MXU)
                    elb_ref[n:n + 1, DHD * h:DHD * (h + 1)] = elb[CH * cc:CH * cc + 1, :]

    return pl.pallas_call(
        body, name="dn_prep", grid=(nb,),
        in_specs=[_rows(TG, 1536), _halo_prev(1536), _full((CONV_K, 1536)), _rows(TG, 8),
                  pl.BlockSpec((8, TG), lambda i: (0, i)), _full((1, 8)), _full((1, 8)), _full((8, 1)), _full((8, 1))],
        out_specs=[hm_spec(DHD), hm_spec(DHD), hm_spec(DHD), hm_spec(GR),
                   pc_spec(2 * CH, DHD), pc_spec(DHD + CH, CH), pc_spec(CH + DHD, CH), pc_spec(DHD, CH),
                   pl.BlockSpec((CPT, NHD * DHD), lambda i: (i, 0)), _rows(TG, 1536)],
        out_shape=[hm(DHD, F32), hm(DHD, _MXU), hm(DHD, _MXU), hm(GR, _MXU),
                   pc(2 * CH, DHD), pc(DHD + CH, CH), pc(CH + DHD, CH), pc(DHD, CH),
                   jax.ShapeDtypeStruct((T // CH, NHD * DHD), F32), jax.ShapeDtypeStruct((T, 1536), F32)],
        compiler_params=_cp(("arbitrary",)),
    )(draw, draw, conv_w, ba, bat, alog8, dtb8, alog8t, dtb8t)


def _dn_scan(u, wq, km, elb):
    T = u.shape[1]
    nb = T // TG
    hm_spec = lambda wd: pl.BlockSpec((NHD, TG, wd), lambda i: (0, i, 0))

    def body(u_ref, wq_ref, km_ref, elb_ref, o_ref, vn_ref, sn_ref, S):
        @pl.when(pl.program_id(0) == 0)
        def _():
            S[...] = jnp.zeros_like(S)

        sub8 = _iota((CPT, DHD), 0)
        heads = range(NHD)

        def chunk(cc, carry):
            rs = pl.ds(pl.multiple_of(cc * CH, CH), CH)
            sh = [S[h] for h in heads]
            sb = [s.astype(_MXU) for s in sh]
            r1 = [_dot(wq_ref[h, cc], sb[h]) for h in heads]
            vnb = [(u_ref[h, rs, :] - r1[h][0:CH]).astype(_MXU) for h in heads]
            r2 = [_dot(km_ref[h, cc], vnb[h]) for h in heads]
            for h in heads:
                el = jnp.sum(jnp.where(sub8 == cc, elb_ref[:, DHD * h:DHD * (h + 1)], 0.0), axis=0, keepdims=True)
                S[h] = sh[h] * el + r2[h][0:DHD]
                sn_ref[cc, h] = sb[h]
                vn_ref[h, rs, :] = vnb[h]
                o_ref[h, rs, :] = r1[h][CH:2 * CH] + r2[h][DHD:DHD + CH]
            return carry

        lax.fori_loop(0, CPT, chunk, 0)

    return pl.pallas_call(
        body, name="dn_scan", grid=(nb,),
        in_specs=[hm_spec(DHD), pl.BlockSpec((NHD, CPT, 2 * CH, DHD), lambda i: (0, i, 0, 0)),
                  pl.BlockSpec((NHD, CPT, DHD + CH, CH), lambda i: (0, i, 0, 0)),
                  pl.BlockSpec((CPT, NHD * DHD), lambda i: (i, 0))],
        out_specs=[hm_spec(DHD), hm_spec(DHD), pl.BlockSpec((CPT, NHD, DHD, DHD), lambda i: (i, 0, 0, 0))],
        out_shape=[jax.ShapeDtypeStruct((NHD, T, DHD), F32), jax.ShapeDtypeStruct((NHD, T, DHD), _MXU),
                   jax.ShapeDtypeStruct((T // CH, NHD, DHD, DHD), _MXU)],
        scratch_shapes=[pltpu.VMEM((NHD, DHD, DHD), F32)],
        compiler_params=_cp(("arbitrary",)),
    )(u, wq, km, elb)


def _dn_scan_bwd(do, mq, kd, wt, sn, vn, elb):
    T = do.shape[1]
    nb = T // TG
    rev = lambda wd: pl.BlockSpec((NHD, TG, wd), lambda i: (0, nb - 1 - i, 0))
    rev_t = lambda r: pl.BlockSpec((NHD, CPT, r, CH), lambda i: (0, nb - 1 - i, 0, 0))

    def body(do_ref, mq_ref, kd_ref, wt_ref, sn_ref, vn_ref, elb_ref,
             du_ref, dw_ref, dqd_ref, dkd_ref, dgx_ref, dS):
        @pl.when(pl.program_id(0) == 0)
        def _():
            dS[...] = jnp.zeros_like(dS)

        last_row = _iota((CH, DHD), 0) == CH - 1
        sub8 = _iota((CPT, DHD), 0)
        heads = range(NHD)

        def chunk(k, carry):
            cc = CPT - 1 - k
            rs = pl.ds(pl.multiple_of(cc * CH, CH), CH)
            dsh = [dS[h] for h in heads]
            dsb = [d.astype(_MXU) for d in dsh]
            doc = [do_ref[h, rs, :].astype(_MXU) for h in heads]
            a = [_dot(mq_ref[h, cc], doc[h]) for h in heads]
            b = [_dot(kd_ref[h, rs, :], dsb[h]) for h in heads]
            dvn = [a[h][0:CH] + b[h] for h in heads]
            dvnb = [d.astype(_MXU) for d in dvn]
            e = [_dot(wt_ref[h, cc], dvnb[h]) for h in heads]
            for h in heads:
                el = jnp.sum(jnp.where(sub8 == cc, elb_ref[:, DHD * h:DHD * (h + 1)], 0.0), axis=0, keepdims=True)
                sn = sn_ref[cc, h]
                dS[h] = a[h][CH:CH + DHD] + dsh[h] * el - e[h]
                du_ref[h, rs, :] = dvn[h]
                c = _dot_nt(jnp.concatenate([doc[h], dvnb[h]], axis=0), sn)
                dqd_ref[h, rs, :] = c[0:CH]
                dw_ref[h, rs, :] = -c[CH:2 * CH]
                dkd_ref[h, rs, :] = _dot_nt(vn_ref[h, rs, :], dsb[h])
                part = jnp.sum(dsh[h] * sn.astype(F32), axis=0, keepdims=True) * el
                dgx_ref[h, rs, :] = jnp.where(last_row, part, 0.0)
            return carry

        lax.fori_loop(0, CPT, chunk, 0)

    o = jax.ShapeDtypeStruct((NHD, T, DHD), F32)
    return pl.pallas_call(
        body, name="dn_scan_bwd", grid=(nb,),
        in_specs=[rev(DHD), rev_t(CH + DHD), rev(DHD), rev_t(DHD),
                  pl.BlockSpec((CPT, NHD, DHD, DHD), lambda i: (nb - 1 - i, 0, 0, 0)), rev(DHD),
                  pl.BlockSpec((CPT, NHD * DHD), lambda i: (nb - 1 - i, 0))],
        out_specs=[rev(DHD)] * 5,
        out_shape=[o] * 5,
        scratch_shapes=[pltpu.VMEM((NHD, DHD, DHD), F32)],
        compiler_params=_cp(("arbitrary",)),
    )(do, mq, kd, wt, sn, vn, elb)


def _put_col(acc, k, col):
    return jnp.where(_iota(acc.shape, 1) == k, col, acc)


def _dn_post_bwd(draw, cv, conv_w, ba, bat, alog8, dtb8, alog8t, dtb8t, du, dw, dqd, dkd, dgx, do, vn, tm, u, w):
    T = draw.shape[0]
    nb = T // TG
    hm_spec = lambda wd: pl.BlockSpec((NHD, TG, wd), lambda i: (0, nb - 1 - i, 0))
    rrows = lambda w: pl.BlockSpec((TG, w), lambda i: (nb - 1 - i, 0))

    def body(x_ref, cv_ref, cw_ref, ba_ref, bat_ref, al_ref, dt_ref, alt_ref, dtt_ref,
             du_ref, dw_ref, dqd_ref, dkd_ref, dgx_ref, do_ref, vn_ref, tm_ref, u_ref, w_ref,
             dx_ref, dba_ref, sm_ref, dcw_ref, dc_ref, nxt_ref):
        i = pl.program_id(0)

        @pl.when(i == 0)
        def _():
            sm_ref[...] = jnp.zeros_like(sm_ref)
            dcw_ref[...] = jnp.zeros_like(dcw_ref)
            nxt_ref[...] = jnp.zeros_like(nxt_ref)

        K = _dn_masks()
        cv = cv_ref[...]
        sg = _sigmoid(cv)
        c_tile = cv * sg
        dsilu = sg * (1.0 + cv * (1.0 - sg))
        for g in range(NG):
            rows = slice(GR * g, GR * (g + 1))
            ba_g = ba_ref[rows, :]
            g8, gc8, gl8, gcrow8 = _gate_terms(ba_g, bat_ref[:, rows], al_ref[...], dt_ref[...], alt_ref[...],
                                               dtt_ref[...], K)
            beta8 = _sigmoid(ba_g)
            dgc8 = jnp.zeros((GR, 8), F32)
            rd8 = jnp.zeros((GR, 8), F32)
            dbeta8 = jnp.zeros((GR, 8), F32)
            pre = [(tm_ref[h, rows, :], u_ref[h, rows, :], w_ref[h, rows, :]) for h in range(NHD)]
            for h, d in enumerate(_dn_heads(c_tile, rows, beta8, gc8, gl8, gcrow8, K, pre)):
                gam_m, gam, egl = d["gam_m"], d["gam"], d["egl"]
                gam_t = jnp.exp(jnp.where(K["triu"], -d["diff"], NEG))
                qn, kn, kb, vv = d["qn"], d["kn"], d["kb"], d["v"]
                duh, dwh = du_ref[h, rows, :], dw_ref[h, rows, :]
                dqdh, dkdh = dqd_ref[h, rows, :], dkd_ref[h, rows, :]
                doh, vnh = do_ref[h, rows, :], vn_ref[h, rows, :]
                tt = _dot_nt(K["eye_b"].astype(_MXU), d["tm"])
                dvb = _dot(tt, duh)
                dkg = _dot(tt, dwh)
                da = -(_dot_nt(dvb, d["u"]) + _dot_nt(dkg, d["w"]))
                dat = -(_dot_nt(d["u"], dvb) + _dot_nt(d["w"], dkg))
                dpl = jnp.where(K["strict"], da, 0.0) * gam_m
                dplt = jnp.where(K["strict_u"], dat, 0.0) * gam_t
                dpm = jnp.where(K["tril"], _dot_nt(doh, vnh), 0.0) * gam_m
                dpmt = jnp.where(K["triu"], _dot_nt(vnh, doh), 0.0) * gam_t
                plt = _dot_nt(kn, kb)
                pmt = _dot_nt(kn, qn)
                dkb = _dot(dpl, kn) + dkg * gam
                dkn = _dot(dplt, kb) + _dot(dpmt, qn) + dkdh * egl + dkb * d["beta"]
                dqn = _dot(dpm, kn) + dqdh * gam
                dkd_kd = dkdh * d["kd"]
                rd = jnp.sum(dkd_kd, axis=1, keepdims=True)
                dgc = jnp.sum(dpl * d["pl"] + dpm * d["pm"] - dplt * plt - dpmt * pmt + dqdh * d["qd"]
                              + dkg * d["kg"] - dkd_kd + dgx_ref[h, rows, :], axis=1, keepdims=True)
                dgc8 = _put_col(dgc8, NHD + h, dgc)
                rd8 = _put_col(rd8, NHD + h, rd)
                dbeta = jnp.sum(dkb * kn + dvb * vv, axis=1, keepdims=True)
                dbeta8 = _put_col(dbeta8, h, dbeta)
                dqh = dqn * (DHD ** -0.5)
                qh = d["qh"]
                dqr = d["rq"] * (dqh - qh * jnp.sum(dqh * qh, axis=1, keepdims=True))
                dkr = d["rk"] * (dkn - kn * jnp.sum(dkn * kn, axis=1, keepdims=True))
                cq = slice(DHD * h, DHD * (h + 1))
                ck = slice(DW + DHD * h, DW + DHD * (h + 1))
                cvv = slice(2 * DW + DHD * h, 2 * DW + DHD * (h + 1))
                dc_ref[rows, cq] = dqr * dsilu[rows, cq]
                dc_ref[rows, ck] = dkr * dsilu[rows, ck]
                dc_ref[rows, cvv] = dvb * d["beta"] * dsilu[rows, cvv]
            dgc8 = dgc8 + jnp.where(K["last"], _ones_dot(K["blk_b"], rd8), 0.0)
            dg8 = _ones_dot(K["triu_b"], dgc8)
            sgm = _sigmoid(ba_g + dt_ref[...])
            dalpha = dg8 * (-jnp.exp(al_ref[...])) * sgm
            lane8 = _iota((GR, 8), 1)
            dba_ref[rows, :] = jnp.where(lane8 < NHD, dbeta8 * beta8 * (1.0 - beta8), dalpha)
            valid = lane8 >= NHD
            sm_ref[0:1, 0:8] += jnp.sum(jnp.where(valid, dg8 * g8, 0.0), axis=0, keepdims=True)
            sm_ref[1:2, 0:8] += jnp.sum(jnp.where(valid, dalpha, 0.0), axis=0, keepdims=True)

        dcv = dc_ref[...]
        xv = x_ref[...]
        nxt = nxt_ref[...]
        w = cw_ref[...]
        dx = dcv * w[CONV_K - 1:CONV_K, :]
        dcw_ref[CONV_K - 1:CONV_K, :] += jnp.sum(dcv * xv, axis=0, keepdims=True)
        for k in range(1, CONV_K):
            j = CONV_K - 1 - k
            up = _shift_up(dcv, nxt, k)
            dx = dx + up * w[j:j + 1, :]
            dcw_ref[j:j + 1, :] += jnp.sum(up * xv, axis=0, keepdims=True)
        dx_ref[...] = dx
        nxt_ref[...] = dcv[0:8]

    return pl.pallas_call(
        body, name="dn_post_bwd", grid=(nb,),
        in_specs=[rrows(1536), rrows(1536), _full((CONV_K, 1536)), rrows(8),
                  pl.BlockSpec((8, TG), lambda i: (0, nb - 1 - i)), _full((1, 8)), _full((1, 8)), _full((8, 1)),
                  _full((8, 1)),
                  hm_spec(DHD), hm_spec(DHD), hm_spec(DHD), hm_spec(DHD), hm_spec(DHD), hm_spec(DHD), hm_spec(DHD),
                  hm_spec(GR), hm_spec(DHD), hm_spec(DHD)],
        out_specs=[rrows(1536), rrows(8), _full((8, LANES)), _full((8, 1536))],
        out_shape=[jax.ShapeDtypeStruct((T, 1536), F32), jax.ShapeDtypeStruct((T, 8), F32),
                   jax.ShapeDtypeStruct((8, LANES), F32), jax.ShapeDtypeStruct((8, 1536), F32)],
        scratch_shapes=[pltpu.VMEM((TG, 1536), F32), pltpu.VMEM((8, 1536), F32)],
        compiler_params=_cp(("arbitrary",)),
    )(draw, cv, conv_w, ba, bat, alog8, dtb8, alog8t, dtb8t, du, dw, dqd, dkd, dgx, do, vn, tm, u, w)


def _rms(x):
    return lax.rsqrt(jnp.mean(x * x, axis=1, keepdims=True) + EPS)


def _rms_bwd(dy, xh, r, g):
    dxh = dy * g
    return r * (dxh - xh * jnp.mean(dxh * xh, axis=1, keepdims=True))


def _hm_rows(tm):
    return pl.BlockSpec((NHD, tm, DHD), lambda i: (0, i, 0))


def _post_mix(apre, o, z, x, w_out, g_a, g_dn):
    T = x.shape[0]

    def body(ap_ref, o_ref, z_ref, x_ref, w_ref, ga_ref, gd_ref, x1_ref, mix_ref):
        ap = ap_ref[...]
        parts = [ap * _rms(ap) * ga_ref[...]]
        zz = z_ref[...]
        for h in range(NHD):
            oh = o_ref[h]
            zh = zz[:, DHD * h:DHD * (h + 1)]
            parts.append(oh * _rms(oh) * gd_ref[...] * (zh * _sigmoid(zh)))
        mix = jnp.concatenate(parts, axis=1).astype(_MXU)
        mix_ref[...] = mix
        x1_ref[...] = x_ref[...] + jnp.dot(mix, w_ref[...], preferred_element_type=F32)

    return pl.pallas_call(
        body, name="post_mix", grid=(T // TM,),
        in_specs=[_rows(TM, AW), _hm_rows(TM), _rows(TM, DW), _rows(TM, D), _full((D, D)), _full((1, AW)),
                  _full((1, DHD))],
        out_specs=[_rows(TM, D), _rows(TM, D)],
        out_shape=[jax.ShapeDtypeStruct((T, D), F32), jax.ShapeDtypeStruct((T, D), _MXU)],
        compiler_params=_cp(("arbitrary",)),
    )(apre, o, z, x, w_out, g_a, g_dn)


def _ffn(x1, tgt, wl_all, g_ffn):
    T = x1.shape[0]
    SH = FF // N_DEV
    nt = (((1,), (1,)), ((), ()))

    def body(x_ref, t_ref, wl_hbm, g_ref,
             dx1_ref, dx1b_ref, h2_ref, act_ref, dgu_ref, dyb_ref, loss_ref, dg_ref, wg, wu, wd, sem):
        @pl.when(pl.program_id(0) == 0)
        def _():
            cps = [pltpu.make_async_copy(wl_hbm.at[dev, pl.ds(128 + SH * k, SH), :], dst.at[pl.ds(SH * dev, SH), :],
                                         sem.at[N_DEV * k + dev])
                   for k, dst in enumerate((wg, wu, wd)) for dev in range(N_DEV)]
            for cp in cps:
                cp.start()
            for cp in cps:
                cp.wait()
            loss_ref[...] = jnp.zeros_like(loss_ref)
            dg_ref[...] = jnp.zeros_like(dg_ref)

        xv = x_ref[...]
        r = _rms(xv)
        xh = xv * r
        gg = g_ref[...]
        h2 = (xh * gg).astype(_MXU)
        h2_ref[...] = h2
        gate = lax.dot_general(h2, wg[...], nt, preferred_element_type=F32)
        up = lax.dot_general(h2, wu[...], nt, preferred_element_type=F32)
        sg = _sigmoid(gate)
        silu = gate * sg
        act = (silu * up).astype(_MXU)
        act_ref[...] = act
        y = xv + jnp.dot(act, wd[...], preferred_element_type=F32)
        err = y - t_ref[...]
        loss_ref[...] += jnp.sum(err * err, axis=0, keepdims=True)
        dy = err * (1.0 / D)
        dyb = dy.astype(_MXU)
        dyb_ref[...] = dyb
        dact = lax.dot_general(dyb, wd[...], nt, preferred_element_type=F32)
        dgate = (dact * up * (sg * (1.0 + gate * (1.0 - sg)))).astype(_MXU)
        dup = (dact * silu).astype(_MXU)
        dgu_ref[:, 0:FF] = dgate
        dgu_ref[:, FF:2 * FF] = dup
        dh2 = (jnp.dot(dgate, wg[...], preferred_element_type=F32)
               + jnp.dot(dup, wu[...], preferred_element_type=F32))
        dg_ref[...] += jnp.sum(dh2 * xh, axis=0, keepdims=True)
        dx1 = dy + _rms_bwd(dh2, xh, r, gg)
        dx1_ref[...] = dx1
        dx1b_ref[...] = dx1.astype(_MXU)

    anyspec = pl.BlockSpec(memory_space=pl.ANY)
    sd = lambda w, dt: jax.ShapeDtypeStruct((T, w), dt)
    return pl.pallas_call(
        body, name="ffn", grid=(T // TM,),
        in_specs=[_rows(TM, D), _rows(TM, D), anyspec, _full((1, D))],
        out_specs=[_rows(TM, D), _rows(TM, D), _rows(TM, D), _rows(TM, FF), _rows(TM, 2 * FF), _rows(TM, D),
                   _full((1, D)), _full((1, D))],
        out_shape=[sd(D, F32), sd(D, _MXU), sd(D, _MXU), sd(FF, _MXU), sd(2 * FF, _MXU), sd(D, _MXU),
                   jax.ShapeDtypeStruct((1, D), F32), jax.ShapeDtypeStruct((1, D), F32)],
        scratch_shapes=[pltpu.VMEM((FF, D), _MXU)] * 3 + [pltpu.SemaphoreType.DMA((3 * N_DEV,))],
        compiler_params=_cp(("arbitrary",)),
    )(x1, tgt, wl_all, g_ffn)


def _mix_bwd(dx1b, w_out, apre, o, z, g_a, g_dn):
    T = dx1b.shape[0]

    def body(dx_ref, w_ref, ap_ref, o_ref, z_ref, ga_ref, gd_ref, dap_ref, do_ref, dz_ref, dga_ref, dgd_ref):
        @pl.when(pl.program_id(0) == 0)
        def _():
            dga_ref[...] = jnp.zeros_like(dga_ref)
            dgd_ref[...] = jnp.zeros_like(dgd_ref)

        dmix = lax.dot_general(dx_ref[...], w_ref[...], (((1,), (1,)), ((), ())), preferred_element_type=F32)
        ap = ap_ref[...]
        ra = _rms(ap)
        ah = ap * ra
        da = dmix[:, 0:AW]
        dga_ref[...] += jnp.sum(da * ah, axis=0, keepdims=True)
        dap_ref[...] = _rms_bwd(da, ah, ra, ga_ref[...])
        zz = z_ref[...]
        gd = gd_ref[...]
        for h in range(NHD):
            cs = slice(DHD * h, DHD * (h + 1))
            dd = dmix[:, AW + DHD * h:AW + DHD * (h + 1)]
            oh = o_ref[h]
            ro = _rms(oh)
            ohh = oh * ro
            zh = zz[:, cs]
            sz = _sigmoid(zh)
            dz_ref[:, cs] = dd * (ohh * gd) * (sz * (1.0 + zh * (1.0 - sz)))
            don = dd * (zh * sz)
            dgd_ref[...] += jnp.sum(don * ohh, axis=0, keepdims=True)
            do_ref[h] = _rms_bwd(don, ohh, ro, gd)

    return pl.pallas_call(
        body, name="mix_bwd", grid=(T // TM,),
        in_specs=[_rows(TM, D), _full((D, D)), _rows(TM, AW), _hm_rows(TM), _rows(TM, DW), _full((1, AW)),
                  _full((1, DHD))],
        out_specs=[_rows(TM, AW), _hm_rows(TM), _rows(TM, DW), _full((1, AW)), _full((1, DHD))],
        out_shape=[jax.ShapeDtypeStruct((T, AW), F32), jax.ShapeDtypeStruct((NHD, T, DHD), F32),
                   jax.ShapeDtypeStruct((T, DW), F32), jax.ShapeDtypeStruct((1, AW), F32),
                   jax.ShapeDtypeStruct((1, DHD), F32)],
        compiler_params=_cp(("arbitrary",)),
    )(dx1b, w_out, apre, o, z, g_a, g_dn)


def _inproj_bwd(dqn, dkn, dv, araw, ddraw, dz, dba, x, dx1, w_in, g_mix, qg_t, kg_t):
    T = x.shape[0]

    def body(dqn_ref, dkn_ref, dv_ref, ar_ref, dd_ref, dz_ref, dba_ref, x_ref, dx1_ref, w_ref, g_ref, qg_ref, kg_ref,
             dx_ref, dp_ref, dgm_ref, dqg_ref, dkg_ref):
        @pl.when(pl.program_id(0) == 0)
        def _():
            dgm_ref[...] = jnp.zeros_like(dgm_ref)
            dqg_ref[...] = jnp.zeros_like(dqg_ref)
            dkg_ref[...] = jnp.zeros_like(dkg_ref)

        bd = _block_ones(AW // 2, DHA)

        def head_norm_bwd(raw, dyn, gain, dg_ref):
            r = _head_rms(raw, bd, DHA)
            xh = raw * r
            dg_ref[...] += jnp.sum(dyn * xh, axis=0, keepdims=True)
            dxh = dyn * gain
            return r * (dxh - xh * (_head_sum(dxh * xh, bd) * (1.0 / DHA)))

        ar = ar_ref[...]
        dq = head_norm_bwd(ar[:, 0:AW], dqn_ref[...] * (DHA ** -0.5), qg_ref[...], dqg_ref)
        dk = head_norm_bwd(ar[:, AW:2 * AW], dkn_ref[...], kg_ref[...], dkg_ref)
        nt = (((1,), (1,)), ((), ()))
        dh = jnp.zeros((TM, D), F32)
        for lo, val in ((0, dq), (AW, dk), (2 * AW, dv_ref[...]), (1536, dd_ref[...]), (3072, dz_ref[...]),
                        (3584, dba_ref[...])):
            vb = val.astype(_MXU)
            wd_ = val.shape[1]
            dp_ref[:, lo:lo + wd_] = vb
            dh = dh + lax.dot_general(vb, w_ref[:, lo:lo + wd_], nt, preferred_element_type=F32)
        xv = x_ref[...]
        r = _rms(xv)
        xh = xv * r
        dgm_ref[...] += jnp.sum(dh * xh, axis=0, keepdims=True)
        dx_ref[...] = dx1_ref[...] + _rms_bwd(dh, xh, r, g_ref[...])

    return pl.pallas_call(
        body, name="inproj_bwd", grid=(T // TM,),
        in_specs=[_rows(TM, AW), _rows(TM, AW), _rows(TM, AW), _rows(TM, 1536), _rows(TM, 1536), _rows(TM, DW),
                  _rows(TM, 8), _rows(TM, D), _rows(TM, D), _full((D, 3592)), _full((1, D)), _full((1, AW)),
                  _full((1, AW))],
        out_specs=[_rows(TM, D), _rows(TM, 3592), _full((1, D)), _full((1, AW)), _full((1, AW))],
        out_shape=[jax.ShapeDtypeStruct((T, D), F32), jax.ShapeDtypeStruct((T, 3592), _MXU),
                   jax.ShapeDtypeStruct((1, D), F32), jax.ShapeDtypeStruct((1, AW), F32),
                   jax.ShapeDtypeStruct((1, AW), F32)],
        compiler_params=_cp(("arbitrary",)),
    )(dqn, dkn, dv, araw, ddraw, dz, dba, x, dx1, w_in, g_mix, qg_t, kg_t)


def _wgrad(a, b, name, tk=512, tn=None, out_dtype=F32, transposed=False):
    T, M = a.shape
    N = b.shape[1]
    tn = N if tn is None else tn
    nk = T // tk

    def body(a_ref, b_ref, o_ref, acc):
        k = pl.program_id(1)

        @pl.when(k == 0)
        def _():
            acc[...] = jnp.zeros_like(acc)

        acc[...] += lax.dot_general(a_ref[...], b_ref[...], (((0,), (0,)), ((), ())), preferred_element_type=F32)

        @pl.when(k == nk - 1)
        def _():
            r = acc[...]
            o_ref[...] = (r.T if transposed else r).astype(out_dtype)

    if transposed:
        out_spec, out_shape = pl.BlockSpec((tn, M), lambda j, k: (j, 0)), (N, M)
    else:
        out_spec, out_shape = pl.BlockSpec((M, tn), lambda j, k: (0, j)), (M, N)
    return pl.pallas_call(
        body, name=name, grid=(N // tn, nk),
        in_specs=[pl.BlockSpec((tk, M), lambda j, k: (k, 0)), pl.BlockSpec((tk, tn), lambda j, k: (k, j))],
        out_specs=out_spec,
        out_shape=jax.ShapeDtypeStruct(out_shape, out_dtype),
        scratch_shapes=[pltpu.VMEM((M, tn), F32)],
        compiler_params=_cp(("arbitrary", "arbitrary")),
    )(a, b)


def _adamw(parts, w, m, v, name, tr):
    K, R, W = parts.shape

    def body(p_ref, w_ref, m_ref, v_ref, g_ref, d_ref, nm_ref, nv_ref):
        g = p_ref[0].astype(F32)
        for k in range(1, K):
            g = g + p_ref[k].astype(F32)
        g_ref[...] = g
        nm = ADAM_B1 * m_ref[...] + (1.0 - ADAM_B1) * g
        nv = ADAM_B2 * v_ref[...] + (1.0 - ADAM_B2) * (g * g)
        nm_ref[...] = nm
        nv_ref[...] = nv
        m_hat = nm / (1.0 - ADAM_B1 ** ADAM_STEP)
        v_hat = nv / (1.0 - ADAM_B2 ** ADAM_STEP)
        d_ref[...] = -ADAM_LR * (m_hat / (jnp.sqrt(v_hat) + ADAM_EPS) + ADAM_WD * w_ref[...])

    o = jax.ShapeDtypeStruct((R, W), F32)
    return pl.pallas_call(
        body, name=name, grid=(R // tr,),
        in_specs=[pl.BlockSpec((K, tr, W), lambda i: (0, i, 0)), _rows(tr, W), _rows(tr, W), _rows(tr, W)],
        out_specs=[_rows(tr, W)] * 4,
        out_shape=[o] * 4,
        compiler_params=_cp(("arbitrary",)),
    )(parts, w, m, v)


SM_ROWS = 136
R_GMIX, R_GFFN, R_QG, R_KG, R_GA, R_GDN, R_ALOG, R_DT, R_LOSS, R_CONV, R_REL = 0, 8, 16, 24, 32, 40, 48, 49, 56, 64, 112


def _small_reduce(gathered):
    def body(p_ref, o_ref):
        s = p_ref[0]
        for k in range(1, N_DEV):
            s = s + p_ref[k]
        o_ref[...] = s
        for r0 in (R_QG, R_KG):
            rs = jnp.sum(s[r0:r0 + 4], axis=0, keepdims=True)
            o_ref[r0:r0 + 1, :] = rs + pltpu.roll(rs, DHA, 1)
        tot = jnp.sum(jnp.sum(s[R_LOSS:R_LOSS + 8], axis=0, keepdims=True), axis=1, keepdims=True)
        o_ref[R_LOSS:R_LOSS + 1, :] = jnp.broadcast_to(tot * (0.5 / D), (1, LANES))

    return pl.pallas_call(
        body, name="small_reduce",
        out_shape=jax.ShapeDtypeStruct((SM_ROWS, LANES), F32),
    )(gathered)


_WIRE = jnp.bfloat16
RA_USED, RA = 449, 464
RL = 128 + 3 * 352


def _pack_rows(parts, rows=None):
    p = jnp.concatenate([t.reshape(-1, D) for t in parts], axis=0) if len(parts) > 1 else parts[0].reshape(-1, D)
    return p if rows is None else jnp.pad(p, ((0, rows - p.shape[0]), (0, 0)))


def _unpack_rows(packed, shapes):
    out, r = [], 0
    for shp in shapes:
        nr = math.prod(shp) // D
        out.append(packed[r:r + nr].reshape(shp))
        r += nr
    return out


def _pad8(t):
    return jnp.pad(t, ((0, (-t.shape[0]) % 8), (0, 0)))


def _pack_lanes(parts):
    rows = []
    for p in parts:
        f = p.reshape(-1)
        pad = (-f.shape[0]) % LANES
        rows.append(jnp.pad(f, (0, pad)).reshape(-1, LANES))
    return jnp.concatenate(rows, axis=0)


def _unpack_lanes(packed, shapes):
    out, r = [], 0
    for shp in shapes:
        n = math.prod(shp)
        nr = -(-n // LANES)
        out.append(packed[r:r + nr].reshape(-1)[:n].reshape(shp))
        r += nr
    return out


def kernel(x, norm_mix_g, w_in, attn_q_norm_g, attn_k_norm_g, rel_bias, attn_out_norm_g, conv_w, a_log, dt_bias, dn_out_norm_g, w_out, norm_ffn_g, w_gate, w_up, w_down, loss_target, m_norm_mix_g, m_w_in, m_attn_q_norm_g, m_attn_k_norm_g, m_rel_bias, m_attn_out_norm_g, m_conv_w, m_a_log, m_dt_bias, m_dn_out_norm_g, m_w_out, m_norm_ffn_g, m_w_gate, m_w_up, m_w_down, v_norm_mix_g, v_w_in, v_attn_q_norm_g, v_attn_k_norm_g, v_rel_bias, v_attn_out_norm_g, v_conv_w, v_a_log, v_dt_bias, v_dn_out_norm_g, v_w_out, v_norm_ffn_g, v_w_gate, v_w_up, v_w_down):
    xs, tgt = x[0], loss_target[0]
    T = xs.shape[0]
    my_idx = 4 * lax.axis_index("x") + 2 * lax.axis_index("y") + lax.axis_index("c")
    late_w = (w_out[0], w_gate[0], w_up[0], w_down[0])
    late_shapes = [w.shape for w in late_w]

    wa_all = _all_gather(_pack_rows([w_in[0].astype(_MXU)], RA), "gather_w_in")
    cw_all = _all_gather(jnp.pad(conv_w[0], ((0, 4), (0, 64))), "gather_conv")
    by_dev = lambda a, k: a.reshape(N_DEV, D, k).transpose(1, 0, 2).reshape(D, N_DEV * k)
    W_in = by_dev(wa_all[:, 0:RA_USED], RA_USED)
    conv_full = cw_all[:, 0:CONV_K, 0:192].transpose(1, 0, 2).reshape(CONV_K, 1536)

    qg_t = jnp.tile(attn_q_norm_g, (1, NHA))
    kg_t = jnp.tile(attn_k_norm_g, (1, NHA))
    z4 = jnp.zeros((1, NHD), F32)
    alog8 = jnp.concatenate([z4, a_log], axis=1)
    dtb8 = jnp.concatenate([z4, dt_bias], axis=1)

    late_t = lambda ts: (ts[0], ts[1].T, ts[2].T, ts[3])
    araw, an, draw, z, ba, hb, wl_all = _inproj(xs, norm_mix_g, W_in, qg_t, kg_t,
                                                _pack_rows([w.astype(_MXU) for w in late_t(late_w)]))
    W_out = wl_all[:, 0:128].reshape(D, D)
    tab, tabt = _bias_tables(rel_bias[0].T)
    apre = _attn_fwd(an, tab)
    bat = ba.T
    dn_args = (draw, conv_full, ba, bat, alog8, dtb8, alog8.T, dtb8.T)
    u, w, kd, tm, wq, km, mq, wt, elb, cv = _dn_prep(*dn_args)
    o, vn, sn = _dn_scan(u, wq, km, elb)
    x1, mix = _post_mix(apre, o, z, xs, W_out, attn_out_norm_g, dn_out_norm_g)

    dx1, dx1b, h2, act, dgu, dyb, loss_row, dgffn = _ffn(x1, tgt, wl_all, norm_ffn_g)

    by_cols = lambda g, k: g.reshape(D, N_DEV, k).transpose(1, 0, 2).reshape(N_DEV, -1, D)
    gW_out = _wgrad(mix, dx1b, "wgrad_out", out_dtype=_WIRE)
    gW_gu_t = _wgrad(h2, dgu, "wgrad_gate_up", tn=FF, out_dtype=_WIRE, transposed=True)
    gW_down = _wgrad(dyb, act, "wgrad_down", out_dtype=_WIRE, transposed=True)
    send_late = jnp.concatenate(
        [gW_out.reshape(N_DEV, 128, D), gW_gu_t[0:FF].reshape(N_DEV, 352, D), gW_gu_t[FF:].reshape(N_DEV, 352, D),
         gW_down.reshape(N_DEV, 352, D)], axis=1)

    dap, do, dz, dga, dgdn = _mix_bwd(dx1b, W_out, apre, o, z, attn_out_norm_g, dn_out_norm_g)
    dqn, dkn, dv, dtabt, recv_late = _attn_bwd(an, dap, tabt, send_late)
    drel = _bias_grad(dtabt)
    du, dw, dqd, dkd, dgx = _dn_scan_bwd(do, mq, kd, wt, sn, vn, elb)
    ddraw, dba, sm, dcw = _dn_post_bwd(draw, cv, *dn_args[1:], du, dw, dqd, dkd, dgx, do, vn, tm, u, w)
    gx, dproj, dgmix, dqg, dkg = _inproj_bwd(dqn, dkn, dv, araw, ddraw, dz, dba, xs, dx1, W_in, norm_mix_g, qg_t, kg_t)

    gW_in = _wgrad(hb, dproj, "wgrad_in", tk=256, out_dtype=_WIRE)
    send_in = jnp.pad(by_cols(gW_in, RA_USED), ((0, 0), (0, RA - RA_USED), (0, 0)))
    recv_in = _all_to_all(send_in, "scatter_w_in")
    late_m = (m_w_out[0], m_w_gate[0], m_w_up[0], m_w_down[0])
    late_v = (v_w_out[0], v_w_gate[0], v_w_up[0], v_w_down[0])
    outs_late = _adamw(recv_late, _pack_rows(late_t(late_w)), _pack_rows(late_t(late_m)), _pack_rows(late_t(late_v)),
                       "adamw_late", 32)
    outs_in = _adamw(recv_in, _pack_rows([w_in[0]], RA), _pack_rows([m_w_in[0]], RA), _pack_rows([v_w_in[0]], RA),
                     "adamw_w_in", 16)
    late_t_shapes = [t.shape for t in late_t(late_w)]
    big = [[_unpack_rows(a, [w_in[0].shape])[0]] + list(late_t(_unpack_rows(b, late_t_shapes)))
           for a, b in zip(outs_in, outs_late)]
    bg, bd_, bm, bv = big

    partial = jnp.concatenate(
        [dgmix.reshape(8, LANES), dgffn.reshape(8, LANES), _pad8(dqg.reshape(4, LANES)), _pad8(dkg.reshape(4, LANES)),
         _pad8(dga.reshape(4, LANES)), _pad8(dgdn), sm, loss_row.reshape(8, LANES),
         dcw[0:CONV_K].reshape(48, LANES), drel.reshape(24, LANES)], axis=0)
    S = _small_reduce(_all_gather(partial, "gather_small"))
    loss = S[R_LOSS, 0]
    g_conv = lax.dynamic_slice(S[R_CONV:R_CONV + 48].reshape(CONV_K, 1536), (0, 192 * my_idx), (CONV_K, 192))
    sg = [S[R_GMIX:R_GMIX + 8].reshape(1, D), S[R_QG:R_QG + 1, 0:DHA], S[R_KG:R_KG + 1, 0:DHA],
          S[R_REL:R_REL + 24].reshape(NHA, 384)[:, 0:257].T, S[R_GA:R_GA + 4].reshape(1, AW), g_conv,
          S[R_ALOG:R_ALOG + 1, NHD:2 * NHD], S[R_DT:R_DT + 1, NHD:2 * NHD], S[R_GDN:R_GDN + 1], S[R_GFFN:R_GFFN + 8].reshape(1, D)]
    sw = [norm_mix_g, attn_q_norm_g, attn_k_norm_g, rel_bias[0], attn_out_norm_g, conv_w[0], a_log, dt_bias, dn_out_norm_g, norm_ffn_g]
    smm = [m_norm_mix_g, m_attn_q_norm_g, m_attn_k_norm_g, m_rel_bias[0], m_attn_out_norm_g, m_conv_w[0], m_a_log, m_dt_bias, m_dn_out_norm_g, m_norm_ffn_g]
    svv = [v_norm_mix_g, v_attn_q_norm_g, v_attn_k_norm_g, v_rel_bias[0], v_attn_out_norm_g, v_conv_w[0], v_a_log, v_dt_bias, v_dn_out_norm_g, v_norm_ffn_g]
    s_shapes = [t.shape for t in sw]
    pk = lambda ts: _pack_lanes(ts)
    pg = pk(sg)
    padr = (-pg.shape[0]) % 8
    padz = lambda t: jnp.pad(t, ((0, padr), (0, 0)))
    s_out = _adamw(padz(pg)[None], padz(pk(sw)), padz(pk(smm)), padz(pk(svv)), "adamw_small", pg.shape[0] + padr)
    s_g, s_d, s_m, s_v = (_unpack_lanes(t, s_shapes) for t in s_out)

    lead = lambda t: t[None]
    def ordered(small, big):
        nm, q, k, rel, ao, cw, al, dtb, dno, nf = small
        wi, wo, wgt, wu, wdn = big
        return [nm, lead(wi), q, k, lead(rel), ao, lead(cw), al, dtb, dno, lead(wo), nf, lead(wgt), lead(wu), lead(wdn)]
    outs = [loss, gx[None]]
    for small, big in ((s_g, bg), (s_d, bd_), (s_m, bm), (s_v, bv)):
        outs += ordered(small, big)
    return tuple(outs)
```

```python
import functools
import math

import jax
import jax.numpy as jnp
from jax import lax
from jax.experimental import pallas as pl
from jax.experimental.pallas import tpu as pltpu

F32 = jnp.float32
BF16 = jnp.bfloat16
_MXU = jnp.bfloat16

D = 1024
AW = 512
NHA = 8
DHA = 64
CH = 64
BAND = 9
NHD = 4
DHD = 128
DW = 512
FF = 2816
EPS = 1e-6
NEG = -1e30
N_DEV = 8
LANES = 128
VMEM_LIMIT = 56 * 1024 * 1024

ADAM_LR = 0.001
ADAM_B1 = 0.9
ADAM_B2 = 0.999
ADAM_EPS = 1e-08
ADAM_WD = 0.01
ADAM_STEP = 10

MESH_T = pl.DeviceIdType.MESH


def _cp(sem=None, vmem=VMEM_LIMIT):
    kw = dict(vmem_limit_bytes=vmem)
    if sem is not None:
        kw["dimension_semantics"] = sem
    return pltpu.CompilerParams(**kw)


def _dot(a, b):
    return jnp.dot(a.astype(_MXU), b.astype(_MXU), preferred_element_type=F32)


def _dot_nt(a, b):
    return lax.dot_general(a.astype(_MXU), b.astype(_MXU), (((1,), (1,)), ((), ())), preferred_element_type=F32)


def _dot_tn(a, b):
    return lax.dot_general(a.astype(_MXU), b.astype(_MXU), (((0,), (0,)), ((), ())), preferred_element_type=F32)


def _split2(x):
    hi = x.astype(BF16)
    lo = (x - hi.astype(F32)).astype(BF16)
    return hi, lo


def _dot_x2(x, ones_b):
    hi, lo = _split2(x)
    return jnp.dot(hi, ones_b, preferred_element_type=F32) + jnp.dot(lo, ones_b, preferred_element_type=F32)


def _dot_x2_nt(x, ones_b):
    hi, lo = _split2(x)
    dn = (((1,), (1,)), ((), ()))
    return lax.dot_general(hi, ones_b, dn, preferred_element_type=F32) + lax.dot_general(
        lo, ones_b, dn, preferred_element_type=F32)


def _iota(shape, dim):
    return lax.broadcasted_iota(jnp.int32, shape, dim)


def _block_ones(n, blk, dtype=BF16):
    r, c = _iota((n, n), 0), _iota((n, n), 1)
    return jnp.where((r // blk) == (c // blk), 1.0, 0.0).astype(dtype)


def _sigmoid(x):
    return 1.0 / (1.0 + jnp.exp(-x))


def _softplus(x):
    return jnp.maximum(x, 0.0) + jnp.log(1.0 + jnp.exp(-jnp.abs(x)))


def _col(x, k):
    lane = _iota(x.shape, 1)
    return jnp.sum(jnp.where(lane == k, x, 0.0), axis=1, keepdims=True)


def _row(x, k):
    sub = _iota(x.shape, 0)
    return jnp.sum(jnp.where(sub == k, x, 0.0), axis=0, keepdims=True)


def _my_pos():
    return lax.axis_index("x"), lax.axis_index("y"), lax.axis_index("c")


def _all_gather(x2d, name):
    R, W = x2d.shape

    def body(x_ref, out_ref, send_sems, recv_sems, local_sem):
        ag = _Gather(x_ref, out_ref, send_sems, recv_sems, local_sem)
        ag.start()
        ag.forward()
        ag.finish()

    return pl.pallas_call(
        body, name=name,
        out_shape=jax.ShapeDtypeStruct((N_DEV, R, W), x2d.dtype),
        in_specs=[pl.BlockSpec(memory_space=pl.ANY)],
        out_specs=pl.BlockSpec(memory_space=pl.ANY),
        scratch_shapes=_COMM_SEMS,
    )(x2d)


_COMM_SEMS = [pltpu.SemaphoreType.DMA((7,)), pltpu.SemaphoreType.DMA((7,)), pltpu.SemaphoreType.DMA]


class _Gather:
    def __init__(self, x_ref, out_ref, send_sems, recv_sems, local_sem):
        x, y, c = _my_pos()
        me, sibling = (x, y, c), (x, y, 1 - c)
        chips = [(1 - x, y), (x, 1 - y), (1 - x, 1 - y)]

        def slot(px, py, pc):
            return out_ref.at[4 * px + 2 * py + pc]

        def copy(k, block, to, src=None):
            return pltpu.make_async_remote_copy(
                src_ref=slot(*block) if src is None else src, dst_ref=slot(*block),
                send_sem=send_sems.at[k], recv_sem=recv_sems.at[k], device_id=to, device_id_type=MESH_T)

        self.mine = pltpu.make_async_copy(x_ref, slot(*me), local_sem)
        self.first = [copy(0, me, sibling, src=x_ref)]
        self.first += [copy(1 + j, me, (*chip, c), src=x_ref) for j, chip in enumerate(chips)]
        self.passed = [copy(4 + j, (*chip, c), sibling) for j, chip in enumerate(chips)]
        self.from_chips = [copy(1 + j, (*chip, c), me) for j, chip in enumerate(chips)]
        self.from_sibling = [copy(0, sibling, me)] + [copy(4 + j, (*chip, 1 - c), me) for j, chip in enumerate(chips)]

    def start(self):
        self.mine.start()
        for cp in self.first:
            cp.start()

    def forward(self):
        for arrived, onward in zip(self.from_chips, self.passed):
            arrived.wait_recv()
            onward.start()

    def finish(self):
        for cp in self.from_sibling:
            cp.wait_recv()
        for cp in self.first + self.passed:
            cp.wait_send()
        self.mine.wait()


class _Scatter:
    def __init__(self, s_ref, r_ref, send_sems, recv_sems, local_sem):
        x, y, c = _my_pos()
        self.mine = pltpu.make_async_copy(s_ref.at[4 * x + 2 * y + c], r_ref.at[0], local_sem)
        self.copies = []
        for m in range(1, N_DEV):
            px = x ^ ((m >> 2) & 1)
            py = y ^ ((m >> 1) & 1)
            pc = c ^ (m & 1)
            self.copies.append(pltpu.make_async_remote_copy(
                src_ref=s_ref.at[4 * px + 2 * py + pc], dst_ref=r_ref.at[m],
                send_sem=send_sems.at[m - 1], recv_sem=recv_sems.at[m - 1],
                device_id=(px, py, pc), device_id_type=MESH_T))

    def start(self):
        self.mine.start()
        for cp in self.copies:
            cp.start()

    def finish(self):
        for cp in self.copies:
            cp.wait_recv()
        for cp in self.copies:
            cp.wait_send()
        self.mine.wait()


def _all_to_all(send, name):
    def body(s_ref, r_ref, send_sems, recv_sems, local_sem):
        sc = _Scatter(s_ref, r_ref, send_sems, recv_sems, local_sem)
        sc.start()
        sc.finish()

    return pl.pallas_call(
        body, name=name,
        out_shape=jax.ShapeDtypeStruct(send.shape, send.dtype),
        in_specs=[pl.BlockSpec(memory_space=pl.ANY)],
        out_specs=pl.BlockSpec(memory_space=pl.ANY),
        scratch_shapes=_COMM_SEMS,
    )(send)


TM = 512
TF = 256
TG = 512


def _full(shape):
    nd = len(shape)
    return pl.BlockSpec(shape, lambda i: (0,) * nd)


def _rows(tm, w):
    return pl.BlockSpec((tm, w), lambda i: (i, 0))


def _head_sum(x, bd):
    return jnp.concatenate([_dot_x2(x[:, 0:256], bd), _dot_x2(x[:, 256:512], bd)], axis=1)


def _head_rms(x, bd, width):
    return lax.rsqrt(_head_sum(x * x, bd) * (1.0 / width) + EPS)


def _inproj(x, g_mix, w_in, qg_t, kg_t, later_w):
    T = x.shape[0]
    nt = T // TM

    def body(x_ref, g_ref, w_ref, qg_ref, kg_ref, lw_ref, araw_ref, an_ref, draw_ref, z_ref, ba_ref, h_ref, lw_all,
             send_sems, recv_sems, local_sem):
        i = pl.program_id(0)
        ag = _Gather(lw_ref, lw_all, send_sems, recv_sems, local_sem)
        pl.when(i == 0)(ag.start)
        pl.when(i == nt // 2)(ag.forward)
        xv = x_ref[...]
        r = lax.rsqrt(jnp.mean(xv * xv, axis=1, keepdims=True) + EPS)
        h = (xv * r * g_ref[...]).astype(_MXU)
        h_ref[...] = h
        a = jnp.dot(h, w_ref[:, 0:1536], preferred_element_type=F32)
        araw_ref[...] = a
        bd = _block_ones(AW // 2, DHA)
        q = a[:, 0:AW]
        k = a[:, AW:2 * AW]
        qn = q * _head_rms(q, bd, DHA) * (qg_ref[...] * (DHA ** -0.5))
        kn = k * _head_rms(k, bd, DHA) * kg_ref[...]
        an_ref[:, 0:AW] = qn.astype(_MXU)
        an_ref[:, AW:2 * AW] = kn.astype(_MXU)
        an_ref[:, 2 * AW:3 * AW] = a[:, 2 * AW:3 * AW].astype(_MXU)
        draw_ref[...] = jnp.dot(h, w_ref[:, 1536:3072], preferred_element_type=F32)
        z_ref[...] = jnp.dot(h, w_ref[:, 3072:3584], preferred_element_type=F32)
        ba_ref[...] = jnp.dot(h, w_ref[:, 3584:3592], preferred_element_type=F32)
        pl.when(i == nt - 1)(ag.finish)

    anyspec = pl.BlockSpec(memory_space=pl.ANY)
    return pl.pallas_call(
        body, name="inproj", grid=(nt,),
        in_specs=[_rows(TM, D), _full((1, D)), _full((D, 3592)), _full((1, AW)), _full((1, AW)), anyspec],
        out_specs=[_rows(TM, 1536), _rows(TM, 1536), _rows(TM, 1536), _rows(TM, DW), _rows(TM, 8), _rows(TM, D),
                   anyspec],
        out_shape=[jax.ShapeDtypeStruct((T, 1536), F32), jax.ShapeDtypeStruct((T, 1536), _MXU),
                   jax.ShapeDtypeStruct((T, 1536), F32), jax.ShapeDtypeStruct((T, DW), F32),
                   jax.ShapeDtypeStruct((T, 8), F32), jax.ShapeDtypeStruct((T, D), _MXU),
                   jax.ShapeDtypeStruct((N_DEV,) + later_w.shape, later_w.dtype)],
        scratch_shapes=_COMM_SEMS,
        compiler_params=_cp(("arbitrary",)),
    )(x, g_mix, w_in, qg_t, kg_t, later_w)


TQ = 256
TW = 768
T_LO, T_HI = 65, 256
VAR0 = 384
TOEP = 1024


def _bias_tables(rb_t):
    def body(rb_ref, tab_ref, tabt_ref):
        h = pl.program_id(0)
        rb8 = jnp.broadcast_to(_row(rb_ref[...], h), (8, VAR0))
        n = _iota((VAR0, TOEP), 1)
        t = _iota((VAR0, TOEP), 0)

        def line(m):
            onehot = jnp.where(jnp.clip(512 - m, -128, 128) + 128 == t, 1.0, 0.0).astype(BF16)
            return sum(jnp.dot(p, onehot, preferred_element_type=F32) for p in _split3(rb8))[0:1, :]

        def band(r, j, first_key):
            return ((j >> 6) >= (r >> 6)) & ((j >> 6) <= (r >> 6) + 8) & (j >= first_key)

        g = line(jnp.where(n < TW, n, n - TOEP))
        tab = pltpu.roll(jnp.broadcast_to(g, (TQ, TOEP)), 0, 1, stride=1, stride_axis=0)[:, 0:TW]
        gt = line(jnp.where(n < TQ, -n, TOEP - n))
        tabt = pltpu.roll(jnp.broadcast_to(gt, (TW, TOEP)), 0, 1, stride=1, stride_axis=0)[:, 0:TQ]
        for v in range(3):
            first_key = max(512 - TQ * v, 0)
            tab_ref[v, 0] = jnp.where(band(_iota((TQ, TW), 0), _iota((TQ, TW), 1), first_key), tab, NEG)
            tabt_ref[v, 0] = jnp.where(band(_iota((TW, TQ), 1), _iota((TW, TQ), 0), first_key), tabt, NEG)

    return pl.pallas_call(
        body, name="bias_tables", grid=(NHA,),
        in_specs=[_full((NHA, VAR0))],
        out_specs=[pl.BlockSpec((3, 1, TQ, TW), lambda h: (0, h, 0, 0)),
                   pl.BlockSpec((3, 1, TW, TQ), lambda h: (0, h, 0, 0))],
        out_shape=[jax.ShapeDtypeStruct((3, NHA, TQ, TW), F32), jax.ShapeDtypeStruct((3, NHA, TW, TQ), F32)],
        compiler_params=_cp(("arbitrary",)),
    )(rb_t)


def _bias_grad(dtabt):
    def body(d_ref, o_ref):
        a, b = _iota((TQ, TQ), 0), _iota((TQ, TQ), 1)
        anti = jnp.where(a + b == TQ - 1, 1.0, 0.0).astype(BF16)
        drev = sum(jnp.dot(t, anti, preferred_element_type=F32) for t in _split3(d_ref[0]))
        wide = jnp.concatenate([drev, jnp.zeros((TW, TOEP - TQ), F32)], axis=1)
        cols = jnp.sum(pltpu.roll(wide, 0, 1, stride=1, stride_axis=0), axis=0, keepdims=True)
        c = _iota((TOEP, VAR0), 0)
        idx = jnp.clip(512 + TQ - 1 - c, -128, 128) + 128
        onehot = jnp.where(idx == _iota((TOEP, VAR0), 1), 1.0, 0.0).astype(BF16)
        cols8 = jnp.broadcast_to(cols, (8, TOEP))
        o_ref[0] = sum(jnp.dot(t, onehot, preferred_element_type=F32) for t in _split3(cols8))[0:1, :]

    return pl.pallas_call(
        body, name="bias_grad", grid=(NHA,),
        in_specs=[pl.BlockSpec((1, TW, TQ), lambda h: (h, 0, 0))],
        out_specs=pl.BlockSpec((1, 1, VAR0), lambda h: (h, 0, 0)),
        out_shape=jax.ShapeDtypeStruct((NHA, 1, VAR0), F32),
        compiler_params=_cp(("arbitrary",)),
    )(dtabt)


def _kv_spec(col, back):
    return pl.BlockSpec((TQ, AW), lambda i: (jnp.maximum(i - back, 0), col))


def _attn_fwd(an, tab):
    T = an.shape[0]

    def body(q_ref, k2_ref, k1_ref, k0_ref, v2_ref, v1_ref, v0_ref, tab_ref, o_ref):
        i = pl.program_id(0)
        kwin = jnp.concatenate([k2_ref[...], k1_ref[...], k0_ref[...]], axis=0)
        vwin = jnp.concatenate([v2_ref[...], v1_ref[...], v0_ref[...]], axis=0)
        q = q_ref[...]
        lo_half = _iota((TQ, LANES), 1) < DHA
        for p in range(NHA // 2):
            sl = slice(LANES * p, LANES * (p + 1))
            qp, kp, vp = q[:, sl], kwin[:, sl], vwin[:, sl]
            outs = []
            for half in range(2):
                mask = lo_half if half == 0 else jnp.logical_not(lo_half)
                qm = jnp.where(mask, qp, jnp.zeros_like(qp))
                s = _dot_nt(qm, kp) + tab_ref[0, 2 * p + half]
                m = jnp.max(s, axis=1, keepdims=True)
                e = jnp.exp(s - m)
                l = jnp.sum(e, axis=1, keepdims=True)
                outs.append(_dot(e, vp) / l)
            o_ref[:, sl] = jnp.where(lo_half, outs[0], outs[1])

    return pl.pallas_call(
        body, name="attn_fwd", grid=(T // TQ,),
        in_specs=[pl.BlockSpec((TQ, AW), lambda i: (i, 0)),
                  _kv_spec(1, 2), _kv_spec(1, 1), _kv_spec(1, 0), _kv_spec(2, 2), _kv_spec(2, 1), _kv_spec(2, 0),
                  pl.BlockSpec((1, NHA, TQ, TW), lambda i: (jnp.minimum(i, 2), 0, 0, 0))],
        out_specs=_rows(TQ, AW),
        out_shape=jax.ShapeDtypeStruct((T, AW), F32),
        compiler_params=_cp(("arbitrary",)),
    )(an, an, an, an, an, an, an, tab)


def _attn_bwd(an, dout, tabt, send):
    T = an.shape[0]
    nq = T // TQ

    def qi(i):
        return jnp.minimum(i, nq - 1)

    def kv_spec(col, back):
        return pl.BlockSpec((TQ, AW), lambda i: (jnp.maximum(qi(i) - back, 0), col))

    def body(q_ref, do_ref, k2_ref, k1_ref, k0_ref, v2_ref, v1_ref, v0_ref, tabt_ref, send_ref,
             dq_ref, dk_ref, dv_ref, dtab_ref, recv_ref, dk_acc, dv_acc, send_sems, recv_sems, local_sem):
        i = pl.program_id(0)
        sc = _Scatter(send_ref, recv_ref, send_sems, recv_sems, local_sem)
        pl.when(i == 0)(sc.start)

        @pl.when(i == 0)
        def _():
            dtab_ref[...] = jnp.zeros_like(dtab_ref)

        new = i % 3
        dk_acc[new] = jnp.zeros((TQ, AW), F32)
        dv_acc[new] = jnp.zeros((TQ, AW), F32)

        @pl.when(i < nq)
        def _():
            kwin = jnp.concatenate([k2_ref[...], k1_ref[...], k0_ref[...]], axis=0)
            vwin = jnp.concatenate([v2_ref[...], v1_ref[...], v0_ref[...]], axis=0)
            q = q_ref[...]
            do = do_ref[...].astype(_MXU)
            lo_half = _iota((TQ, LANES), 1) < DHA
            for p in range(NHA // 2):
                sl = slice(LANES * p, LANES * (p + 1))
                qp, kp, vp, dop = q[:, sl], kwin[:, sl], vwin[:, sl], do[:, sl]
                dq_pair = jnp.zeros((TQ, LANES), F32)
                dk_pair = jnp.zeros((TW, LANES), F32)
                dv_pair = jnp.zeros((TW, LANES), F32)
                for half in range(2):
                    h = 2 * p + half
                    mask = lo_half if half == 0 else jnp.logical_not(lo_half)
                    qm = jnp.where(mask, qp, jnp.zeros_like(qp))
                    dom = jnp.where(mask, dop, jnp.zeros_like(dop))
                    st = _dot_nt(kp, qm) + tabt_ref[0, h]
                    m = jnp.max(st, axis=0, keepdims=True)
                    e = jnp.exp(st - m)
                    pt = e * (1.0 / jnp.sum(e, axis=0, keepdims=True))
                    dpt = _dot_nt(vp, dom)
                    delta = jnp.sum(pt * dpt, axis=0, keepdims=True)
                    dst = pt * (dpt - delta)
                    dtab_ref[h] += dst
                    dsb = dst.astype(_MXU)
                    dv_pair += _dot(pt, dom)
                    dk_pair += _dot(dsb, qm)
                    dq_pair += jnp.where(mask, _dot_tn(dsb, kp), 0.0)
                dq_ref[:, sl] = dq_pair
                for w in range(3):
                    slot = (i + 1 + w) % 3
                    rows = slice(TQ * w, TQ * (w + 1))
                    dk_acc[slot, :, sl] += dk_pair[rows]
                    dv_acc[slot, :, sl] += dv_pair[rows]

        @pl.when(i >= 2)
        def _():
            done = (i + 1) % 3
            dk_ref[...] = dk_acc[done]
            dv_ref[...] = dv_acc[done]

        pl.when(i == nq + 1)(sc.finish)

    back2 = pl.BlockSpec((TQ, AW), lambda i: (jnp.maximum(i - 2, 0), 0))
    anyspec = pl.BlockSpec(memory_space=pl.ANY)
    return pl.pallas_call(
        body, name="attn_bwd", grid=(nq + 2,),
        in_specs=[pl.BlockSpec((TQ, AW), lambda i: (qi(i), 0)), pl.BlockSpec((TQ, AW), lambda i: (qi(i), 0)),
                  kv_spec(1, 2), kv_spec(1, 1), kv_spec(1, 0), kv_spec(2, 2), kv_spec(2, 1), kv_spec(2, 0),
                  pl.BlockSpec((1, NHA, TW, TQ), lambda i: (jnp.minimum(i, 2), 0, 0, 0)), anyspec],
        out_specs=[pl.BlockSpec((TQ, AW), lambda i: (qi(i), 0)), back2, back2, _full((NHA, TW, TQ)), anyspec],
        out_shape=[jax.ShapeDtypeStruct((T, AW), F32), jax.ShapeDtypeStruct((T, AW), F32),
                   jax.ShapeDtypeStruct((T, AW), F32), jax.ShapeDtypeStruct((NHA, TW, TQ), F32),
                   jax.ShapeDtypeStruct(send.shape, send.dtype)],
        scratch_shapes=[pltpu.VMEM((3, TQ, AW), F32), pltpu.VMEM((3, TQ, AW), F32)] + _COMM_SEMS,
        compiler_params=_cp(("arbitrary",)),
    )(an, dout, an, an, an, an, an, an, tabt, send)


GR = 128
NG = TG // GR
CPT = TG // CH
CONV_K = 4


def _split3(x):
    a = x.astype(BF16)
    r = x - a.astype(F32)
    b = r.astype(BF16)
    c = (r - b.astype(F32)).astype(BF16)
    return a, b, c


def _ones_dot(ones_b, x):
    return sum(jnp.dot(ones_b, t, preferred_element_type=F32) for t in _split3(x))


def _dot_ones_nt(x, ones_b):
    dn = (((1,), (1,)), ((), ()))
    return sum(lax.dot_general(t, ones_b, dn, preferred_element_type=F32) for t in _split3(x))


def _dn_masks():
    r, c = _iota((GR, GR), 0), _iota((GR, GR), 1)
    same = (r >> 6) == (c >> 6)
    one = lambda m: jnp.where(m, 1.0, 0.0).astype(BF16)
    return dict(
        tril=same & (c <= r), strict=same & (c < r), triu=same & (c >= r), strict_u=same & (c > r),
        tril_b=one(same & (c <= r)), triu_b=one(same & (c >= r)), blk_b=one(same), eye_b=one(r == c),
        eye=jnp.where(r == c, 1.0, 0.0).astype(F32),
        fold_b=one((_iota((GR, CH), 0) & (CH - 1)) == _iota((GR, CH), 1)),
        last=(_iota((GR, 1), 0) & (CH - 1)) == CH - 1,
    )


def _shift_down(x, halo, k):
    if k == 0:
        return x
    xs = pltpu.roll(x, k, 0)
    hs = pltpu.roll(halo, k, 0)
    top = jnp.where(_iota(halo.shape, 0) < k, hs, xs[0:8])
    return jnp.concatenate([top, xs[8:]], axis=0)


def _shift_up(x, halo, k):
    if k == 0:
        return x
    n = x.shape[0]
    xs = pltpu.roll(x, n - k, 0)
    hs = pltpu.roll(halo, 8 - k, 0)
    bot = jnp.where(_iota(halo.shape, 0) >= 8 - k, hs, xs[n - 8:n])
    return jnp.concatenate([xs[0:n - 8], bot], axis=0)


def _conv(x, halo, w):
    y = x * w[CONV_K - 1:CONV_K, :]
    for k in range(1, CONV_K):
        y = y + _shift_down(x, halo, k) * w[CONV_K - 1 - k:CONV_K - k, :]
    return y


def _tri_inv(lmats, eye):
    ps = [-m for m in lmats]
    rs = [eye + p for p in ps]
    for _ in range(5):
        ps = [_dot(p, p) for p in ps]
        rs = [r + _dot(r, p) for r, p in zip(rs, ps)]
    return rs


def _gate_terms(ba_g, bat_g, alog8, dtb8, alog8t, dtb8t, K):
    g8 = -jnp.exp(alog8) * _softplus(ba_g + dtb8)
    g8t = -jnp.exp(alog8t) * _softplus(bat_g + dtb8t)
    gc8 = _ones_dot(K["tril_b"], g8)
    gl8 = _ones_dot(K["blk_b"], g8)
    gcrow8 = _dot_ones_nt(g8t, K["tril_b"])
    return g8, gc8, gl8, gcrow8


def _dn_heads(c_tile, rows, beta8, gc8, gl8, gcrow8, K, pre=None):
    ds = [_dn_head(c_tile, rows, h, beta8, gc8, gl8, gcrow8, K) for h in range(NHD)]
    if pre is None:
        for d, tm in zip(ds, _tri_inv([d["lmat"] for d in ds], K["eye"])):
            d.update(tm=tm, u=_dot(tm, d["vb"]), w=_dot(tm, d["kg"]))
    else:
        for d, (tm, u, w) in zip(ds, pre):
            d.update(tm=tm, u=u, w=w)
    return ds


def _dn_head(c_tile, rows, h, beta8, gc8, gl8, gcrow8, K):
    qr = c_tile[rows, DHD * h:DHD * (h + 1)]
    kr = c_tile[rows, DW + DHD * h:DW + DHD * (h + 1)]
    v = c_tile[rows, 2 * DW + DHD * h:2 * DW + DHD * (h + 1)]
    rq = lax.rsqrt(jnp.sum(qr * qr, axis=1, keepdims=True) + EPS)
    rk = lax.rsqrt(jnp.sum(kr * kr, axis=1, keepdims=True) + EPS)
    qh, kn = qr * rq, kr * rk
    qn = qh * (DHD ** -0.5)
    beta = _col(beta8, h)
    gccol, glcol, gcrow = _col(gc8, NHD + h), _col(gl8, NHD + h), _row(gcrow8, NHD + h)
    diff = gccol - gcrow
    gam_m = jnp.exp(jnp.where(K["tril"], diff, NEG))
    gam = jnp.exp(gccol)
    egl = jnp.exp(glcol - gccol)
    kb, vb = kn * beta, v * beta
    kg = kb * gam
    pl_ = _dot_nt(kb, kn)
    lmat = jnp.where(K["strict"], pl_ * gam_m, 0.0)
    pm = _dot_nt(qn, kn)
    mm = pm * gam_m
    return dict(qr=qr, kr=kr, v=v, rq=rq, rk=rk, qh=qh, qn=qn, kn=kn, beta=beta, diff=diff, gam_m=gam_m, gam=gam,
                egl=egl, el=jnp.exp(glcol), kb=kb, vb=vb, kg=kg, pl=pl_, pm=pm, lmat=lmat, mm=mm,
                qd=qn * gam, kd=kn * egl)


def _halo_prev(width):
    return pl.BlockSpec((8, width), lambda i: (jnp.maximum(i * (TG // 8) - 1, 0), 0))


def _dn_prep(draw, conv_w, ba, bat, alog8, dtb8, alog8t, dtb8t):
    T = draw.shape[0]
    nb = T // TG
    hm = lambda w, dt: jax.ShapeDtypeStruct((NHD, T, w), dt)
    hm_spec = lambda w: pl.BlockSpec((NHD, TG, w), lambda i: (0, i, 0))
    pc = lambda r, c: jax.ShapeDtypeStruct((NHD, T // CH, r, c), _MXU)
    pc_spec = lambda r, c: pl.BlockSpec((NHD, CPT, r, c), lambda i: (0, i, 0, 0))

    def body(x_ref, halo_ref, cw_ref, ba_ref, bat_ref, al_ref, dt_ref, alt_ref, dtt_ref,
             u_ref, w_ref, kd_ref, tm_ref, wq_ref, km_ref, mq_ref, wt_ref, elb_ref, cv_ref):
        i = pl.program_id(0)
        K = _dn_masks()
        halo = jnp.where(i > 0, halo_ref[...], 0.0)
        cv = _conv(x_ref[...], halo, cw_ref[...])
        cv_ref[...] = cv
        c_tile = cv * _sigmoid(cv)
        eye128 = jnp.where(_iota((DHD, DHD), 0) == _iota((DHD, DHD), 1), 1.0, 0.0).astype(_MXU)
        for g in range(NG):
            rows = slice(GR * g, GR * (g + 1))
            ba_g = ba_ref[rows, :]
            _, gc8, gl8, gcrow8 = _gate_terms(ba_g, bat_ref[:, rows], al_ref[...], dt_ref[...], alt_ref[...],
                                              dtt_ref[...], K)
            beta8 = _sigmoid(ba_g)
            for h, d in enumerate(_dn_heads(c_tile, rows, beta8, gc8, gl8, gcrow8, K)):
                gam_t = jnp.exp(jnp.where(K["triu"], -d["diff"], NEG))
                tm_ref[h, rows, :] = d["tm"].astype(_MXU)
                mmt = _dot_nt(d["kn"], d["qn"]) * gam_t
                u_ref[h, rows, :] = d["u"]
                w_ref[h, rows, :] = d["w"].astype(_MXU)
                kd_ref[h, rows, :] = d["kd"].astype(_MXU)
                mc = _dot(d["mm"], K["fold_b"])
                mct = _dot(mmt, K["fold_b"])
                elb = jnp.broadcast_to(d["el"], (GR, DHD))
                for cc in range(GR // CH):
                    ch = slice(CH * cc, CH * (cc + 1))
                    n = (GR // CH) * g + cc
                    wq_ref[h, n, 0:CH, :] = d["w"][ch].astype(_MXU)
                    wq_ref[h, n, CH:2 * CH, :] = d["qd"][ch].astype(_MXU)
                    km_ref[h, n, 0:DHD, :] = _dot_nt(eye128, d["kd"][ch]).astype(_MXU)
                    km_ref[h, n, DHD:DHD + CH, :] = mc[ch].astype(_MXU)
                    mq_ref[h, n, 0:CH, :] = mct[ch].astype(_MXU)
                    mq_ref[h, n, CH:CH + DHD, :] = _dot_nt(eye128, d["qd"][ch]).astype(_MXU)
                    wt_ref[h, n] = _dot_nt(eye128, d["w"][ch]).astype(_MXU)
                    elb_ref[n:n + 1, DHD * h:DHD * (h + 1)] = elb[CH * cc:CH * cc + 1, :]

    return pl.pallas_call(
        body, name="dn_prep", grid=(nb,),
        in_specs=[_rows(TG, 1536), _halo_prev(1536), _full((CONV_K, 1536)), _rows(TG, 8),
                  pl.BlockSpec((8, TG), lambda i: (0, i)), _full((1, 8)), _full((1, 8)), _full((8, 1)), _full((8, 1))],
        out_specs=[hm_spec(DHD), hm_spec(DHD), hm_spec(DHD), hm_spec(GR),
                   pc_spec(2 * CH, DHD), pc_spec(DHD + CH, CH), pc_spec(CH + DHD, CH), pc_spec(DHD, CH),
                   pl.BlockSpec((CPT, NHD * DHD), lambda i: (i, 0)), _rows(TG, 1536)],
        out_shape=[hm(DHD, F32), hm(DHD, _MXU), hm(DHD, _MXU), hm(GR, _MXU),
                   pc(2 * CH, DHD), pc(DHD + CH, CH), pc(CH + DHD, CH), pc(DHD, CH),
                   jax.ShapeDtypeStruct((T // CH, NHD * DHD), F32), jax.ShapeDtypeStruct((T, 1536), F32)],
        compiler_params=_cp(("arbitrary",)),
    )(draw, draw, conv_w, ba, bat, alog8, dtb8, alog8t, dtb8t)


def _dn_scan(u, wq, km, elb):
    T = u.shape[1]
    nb = T // TG
    hm_spec = lambda wd: pl.BlockSpec((NHD, TG, wd), lambda i: (0, i, 0))

    def body(u_ref, wq_ref, km_ref, elb_ref, o_ref, vn_ref, sn_ref, S):
        @pl.when(pl.program_id(0) == 0)
        def _():
            S[...] = jnp.zeros_like(S)

        sub8 = _iota((CPT, DHD), 0)
        heads = range(NHD)

        def chunk(cc, carry):
            rs = pl.ds(pl.multiple_of(cc * CH, CH), CH)
            sh = [S[h] for h in heads]
            sb = [s.astype(_MXU) for s in sh]
            r1 = [_dot(wq_ref[h, cc], sb[h]) for h in heads]
            vnb = [(u_ref[h, rs, :] - r1[h][0:CH]).astype(_MXU) for h in heads]
            r2 = [_dot(km_ref[h, cc], vnb[h]) for h in heads]
            for h in heads:
                el = jnp.sum(jnp.where(sub8 == cc, elb_ref[:, DHD * h:DHD * (h + 1)], 0.0), axis=0, keepdims=True)
                S[h] = sh[h] * el + r2[h][0:DHD]
                sn_ref[cc, h] = sb[h]
                vn_ref[h, rs, :] = vnb[h]
                o_ref[h, rs, :] = r1[h][CH:2 * CH] + r2[h][DHD:DHD + CH]
            return carry

        lax.fori_loop(0, CPT, chunk, 0)

    return pl.pallas_call(
        body, name="dn_scan", grid=(nb,),
        in_specs=[hm_spec(DHD), pl.BlockSpec((NHD, CPT, 2 * CH, DHD), lambda i: (0, i, 0, 0)),
                  pl.BlockSpec((NHD, CPT, DHD + CH, CH), lambda i: (0, i, 0, 0)),
                  pl.BlockSpec((CPT, NHD * DHD), lambda i: (i, 0))],
        out_specs=[hm_spec(DHD), hm_spec(DHD), pl.BlockSpec((CPT, NHD, DHD, DHD), lambda i: (i, 0, 0, 0))],
        out_shape=[jax.ShapeDtypeStruct((NHD, T, DHD), F32), jax.ShapeDtypeStruct((NHD, T, DHD), _MXU),
                   jax.ShapeDtypeStruct((T // CH, NHD, DHD, DHD), _MXU)],
        scratch_shapes=[pltpu.VMEM((NHD, DHD, DHD), F32)],
        compiler_params=_cp(("arbitrary",)),
    )(u, wq, km, elb)


def _dn_scan_bwd(do, mq, kd, wt, sn, vn, elb):
    T = do.shape[1]
    nb = T // TG
    rev = lambda wd: pl.BlockSpec((NHD, TG, wd), lambda i: (0, nb - 1 - i, 0))
    rev_t = lambda r: pl.BlockSpec((NHD, CPT, r, CH), lambda i: (0, nb - 1 - i, 0, 0))

    def body(do_ref, mq_ref, kd_ref, wt_ref, sn_ref, vn_ref, elb_ref,
             du_ref, dw_ref, dqd_ref, dkd_ref, dgx_ref, dS):
        @pl.when(pl.program_id(0) == 0)
        def _():
            dS[...] = jnp.zeros_like(dS)

        last_row = _iota((CH, DHD), 0) == CH - 1
        sub8 = _iota((CPT, DHD), 0)
        heads = range(NHD)

        def chunk(k, carry):
            cc = CPT - 1 - k
            rs = pl.ds(pl.multiple_of(cc * CH, CH), CH)
            dsh = [dS[h] for h in heads]
            dsb = [d.astype(_MXU) for d in dsh]
            doc = [do_ref[h, rs, :].astype(_MXU) for h in heads]
            a = [_dot(mq_ref[h, cc], doc[h]) for h in heads]
            b = [_dot(kd_ref[h, rs, :], dsb[h]) for h in heads]
            dvn = [a[h][0:CH] + b[h] for h in heads]
            dvnb = [d.astype(_MXU) for d in dvn]
            e = [_dot(wt_ref[h, cc], dvnb[h]) for h in heads]
            for h in heads:
                el = jnp.sum(jnp.where(sub8 == cc, elb_ref[:, DHD * h:DHD * (h + 1)], 0.0), axis=0, keepdims=True)
                sn = sn_ref[cc, h]
                dS[h] = a[h][CH:CH + DHD] + dsh[h] * el - e[h]
                du_ref[h, rs, :] = dvn[h]
                c = _dot_nt(jnp.concatenate([doc[h], dvnb[h]], axis=0), sn)
                dqd_ref[h, rs, :] = c[0:CH]
                dw_ref[h, rs, :] = -c[CH:2 * CH]
                dkd_ref[h, rs, :] = _dot_nt(vn_ref[h, rs, :], dsb[h])
                part = jnp.sum(dsh[h] * sn.astype(F32), axis=0, keepdims=True) * el
                dgx_ref[h, rs, :] = jnp.where(last_row, part, 0.0)
            return carry

        lax.fori_loop(0, CPT, chunk, 0)

    o = jax.ShapeDtypeStruct((NHD, T, DHD), F32)
    return pl.pallas_call(
        body, name="dn_scan_bwd", grid=(nb,),
        in_specs=[rev(DHD), rev_t(CH + DHD), rev(DHD), rev_t(DHD),
                  pl.BlockSpec((CPT, NHD, DHD, DHD), lambda i: (nb - 1 - i, 0, 0, 0)), rev(DHD),
                  pl.BlockSpec((CPT, NHD * DHD), lambda i: (nb - 1 - i, 0))],
        out_specs=[rev(DHD)] * 5,
        out_shape=[o] * 5,
        scratch_shapes=[pltpu.VMEM((NHD, DHD, DHD), F32)],
        compiler_params=_cp(("arbitrary",)),
    )(do, mq, kd, wt, sn, vn, elb)


def _put_col(acc, k, col):
    return jnp.where(_iota(acc.shape, 1) == k, col, acc)


def _dn_post_bwd(draw, cv, conv_w, ba, bat, alog8, dtb8, alog8t, dtb8t, du, dw, dqd, dkd, dgx, do, vn, tm, u, w):
    T = draw.shape[0]
    nb = T // TG
    hm_spec = lambda wd: pl.BlockSpec((NHD, TG, wd), lambda i: (0, nb - 1 - i, 0))
    rrows = lambda w: pl.BlockSpec((TG, w), lambda i: (nb - 1 - i, 0))

    def body(x_ref, cv_ref, cw_ref, ba_ref, bat_ref, al_ref, dt_ref, alt_ref, dtt_ref,
             du_ref, dw_ref, dqd_ref, dkd_ref, dgx_ref, do_ref, vn_ref, tm_ref, u_ref, w_ref,
             dx_ref, dba_ref, sm_ref, dcw_ref, dc_ref, nxt_ref):
        i = pl.program_id(0)

        @pl.when(i == 0)
        def _():
            sm_ref[...] = jnp.zeros_like(sm_ref)
            dcw_ref[...] = jnp.zeros_like(dcw_ref)
            nxt_ref[...] = jnp.zeros_like(nxt_ref)

        K = _dn_masks()
        cv = cv_ref[...]
        sg = _sigmoid(cv)
        c_tile = cv * sg
        dsilu = sg * (1.0 + cv * (1.0 - sg))
        for g in range(NG):
            rows = slice(GR * g, GR * (g + 1))
            ba_g = ba_ref[rows, :]
            g8, gc8, gl8, gcrow8 = _gate_terms(ba_g, bat_ref[:, rows], al_ref[...], dt_ref[...], alt_ref[...],
                                               dtt_ref[...], K)
            beta8 = _sigmoid(ba_g)
            dgc8 = jnp.zeros((GR, 8), F32)
            rd8 = jnp.zeros((GR, 8), F32)
            dbeta8 = jnp.zeros((GR, 8), F32)
            pre = [(tm_ref[h, rows, :], u_ref[h, rows, :], w_ref[h, rows, :]) for h in range(NHD)]
            for h, d in enumerate(_dn_heads(c_tile, rows, beta8, gc8, gl8, gcrow8, K, pre)):
                gam_m, gam, egl = d["gam_m"], d["gam"], d["egl"]
                gam_t = jnp.exp(jnp.where(K["triu"], -d["diff"], NEG))
                qn, kn, kb, vv = d["qn"], d["kn"], d["kb"], d["v"]
                duh, dwh = du_ref[h, rows, :], dw_ref[h, rows, :]
                dqdh, dkdh = dqd_ref[h, rows, :], dkd_ref[h, rows, :]
                doh, vnh = do_ref[h, rows, :], vn_ref[h, rows, :]
                tt = _dot_nt(K["eye_b"].astype(_MXU), d["tm"])
                dvb = _dot(tt, duh)
                dkg = _dot(tt, dwh)
                da = -(_dot_nt(dvb, d["u"]) + _dot_nt(dkg, d["w"]))
                dat = -(_dot_nt(d["u"], dvb) + _dot_nt(d["w"], dkg))
                dpl = jnp.where(K["strict"], da, 0.0) * gam_m
                dplt = jnp.where(K["strict_u"], dat, 0.0) * gam_t
                dpm = jnp.where(K["tril"], _dot_nt(doh, vnh), 0.0) * gam_m
                dpmt = jnp.where(K["triu"], _dot_nt(vnh, doh), 0.0) * gam_t
                plt = _dot_nt(kn, kb)
                pmt = _dot_nt(kn, qn)
                dkb = _dot(dpl, kn) + dkg * gam
                dkn = _dot(dplt, kb) + _dot(dpmt, qn) + dkdh * egl + dkb * d["beta"]
                dqn = _dot(dpm, kn) + dqdh * gam
                dkd_kd = dkdh * d["kd"]
                rd = jnp.sum(dkd_kd, axis=1, keepdims=True)
                dgc = jnp.sum(dpl * d["pl"] + dpm * d["pm"] - dplt * plt - dpmt * pmt + dqdh * d["qd"]
                              + dkg * d["kg"] - dkd_kd + dgx_ref[h, rows, :], axis=1, keepdims=True)
                dgc8 = _put_col(dgc8, NHD + h, dgc)
                rd8 = _put_col(rd8, NHD + h, rd)
                dbeta = jnp.sum(dkb * kn + dvb * vv, axis=1, keepdims=True)
                dbeta8 = _put_col(dbeta8, h, dbeta)
                dqh = dqn * (DHD ** -0.5)
                qh = d["qh"]
                dqr = d["rq"] * (dqh - qh * jnp.sum(dqh * qh, axis=1, keepdims=True))
                dkr = d["rk"] * (dkn - kn * jnp.sum(dkn * kn, axis=1, keepdims=True))
                cq = slice(DHD * h, DHD * (h + 1))
                ck = slice(DW + DHD * h, DW + DHD * (h + 1))
                cvv = slice(2 * DW + DHD * h, 2 * DW + DHD * (h + 1))
                dc_ref[rows, cq] = dqr * dsilu[rows, cq]
                dc_ref[rows, ck] = dkr * dsilu[rows, ck]
                dc_ref[rows, cvv] = dvb * d["beta"] * dsilu[rows, cvv]
            dgc8 = dgc8 + jnp.where(K["last"], _ones_dot(K["blk_b"], rd8), 0.0)
            dg8 = _ones_dot(K["triu_b"], dgc8)
            sgm = _sigmoid(ba_g + dt_ref[...])
            dalpha = dg8 * (-jnp.exp(al_ref[...])) * sgm
            lane8 = _iota((GR, 8), 1)
            dba_ref[rows, :] = jnp.where(lane8 < NHD, dbeta8 * beta8 * (1.0 - beta8), dalpha)
            valid = lane8 >= NHD
            sm_ref[0:1, 0:8] += jnp.sum(jnp.where(valid, dg8 * g8, 0.0), axis=0, keepdims=True)
            sm_ref[1:2, 0:8] += jnp.sum(jnp.where(valid, dalpha, 0.0), axis=0, keepdims=True)

        dcv = dc_ref[...]
        xv = x_ref[...]
        nxt = nxt_ref[...]
        w = cw_ref[...]
        dx = dcv * w[CONV_K - 1:CONV_K, :]
        dcw_ref[CONV_K - 1:CONV_K, :] += jnp.sum(dcv * xv, axis=0, keepdims=True)
        for k in range(1, CONV_K):
            j = CONV_K - 1 - k
            up = _shift_up(dcv, nxt, k)
            dx = dx + up * w[j:j + 1, :]
            dcw_ref[j:j + 1, :] += jnp.sum(up * xv, axis=0, keepdims=True)
        dx_ref[...] = dx
        nxt_ref[...] = dcv[0:8]

    return pl.pallas_call(
        body, name="dn_post_bwd", grid=(nb,),
        in_specs=[rrows(1536), rrows(1536), _full((CONV_K, 1536)), rrows(8),
                  pl.BlockSpec((8, TG), lambda i: (0, nb - 1 - i)), _full((1, 8)), _full((1, 8)), _full((8, 1)),
                  _full((8, 1)),
                  hm_spec(DHD), hm_spec(DHD), hm_spec(DHD), hm_spec(DHD), hm_spec(DHD), hm_spec(DHD), hm_spec(DHD),
                  hm_spec(GR), hm_spec(DHD), hm_spec(DHD)],
        out_specs=[rrows(1536), rrows(8), _full((8, LANES)), _full((8, 1536))],
        out_shape=[jax.ShapeDtypeStruct((T, 1536), F32), jax.ShapeDtypeStruct((T, 8), F32),
                   jax.ShapeDtypeStruct((8, LANES), F32), jax.ShapeDtypeStruct((8, 1536), F32)],
        scratch_shapes=[pltpu.VMEM((TG, 1536), F32), pltpu.VMEM((8, 1536), F32)],
        compiler_params=_cp(("arbitrary",)),
    )(draw, cv, conv_w, ba, bat, alog8, dtb8, alog8t, dtb8t, du, dw, dqd, dkd, dgx, do, vn, tm, u, w)


def _rms(x):
    return lax.rsqrt(jnp.mean(x * x, axis=1, keepdims=True) + EPS)


def _rms_bwd(dy, xh, r, g):
    dxh = dy * g
    return r * (dxh - xh * jnp.mean(dxh * xh, axis=1, keepdims=True))


def _hm_rows(tm):
    return pl.BlockSpec((NHD, tm, DHD), lambda i: (0, i, 0))


def _post_mix(apre, o, z, x, w_out, g_a, g_dn):
    T = x.shape[0]

    def body(ap_ref, o_ref, z_ref, x_ref, w_ref, ga_ref, gd_ref, x1_ref, mix_ref):
        ap = ap_ref[...]
        parts = [ap * _rms(ap) * ga_ref[...]]
        zz = z_ref[...]
        for h in range(NHD):
            oh = o_ref[h]
            zh = zz[:, DHD * h:DHD * (h + 1)]
            parts.append(oh * _rms(oh) * gd_ref[...] * (zh * _sigmoid(zh)))
        mix = jnp.concatenate(parts, axis=1).astype(_MXU)
        mix_ref[...] = mix
        x1_ref[...] = x_ref[...] + jnp.dot(mix, w_ref[...], preferred_element_type=F32)

    return pl.pallas_call(
        body, name="post_mix", grid=(T // TM,),
        in_specs=[_rows(TM, AW), _hm_rows(TM), _rows(TM, DW), _rows(TM, D), _full((D, D)), _full((1, AW)),
                  _full((1, DHD))],
        out_specs=[_rows(TM, D), _rows(TM, D)],
        out_shape=[jax.ShapeDtypeStruct((T, D), F32), jax.ShapeDtypeStruct((T, D), _MXU)],
        compiler_params=_cp(("arbitrary",)),
    )(apre, o, z, x, w_out, g_a, g_dn)


def _ffn(x1, tgt, wl_all, g_ffn):
    T = x1.shape[0]
    SH = FF // N_DEV
    nt = (((1,), (1,)), ((), ()))

    def body(x_ref, t_ref, wl_hbm, g_ref,
             dx1_ref, dx1b_ref, h2_ref, act_ref, dgu_ref, dyb_ref, loss_ref, dg_ref, wg, wu, wd, sem):
        @pl.when(pl.program_id(0) == 0)
        def _():
            cps = [pltpu.make_async_copy(wl_hbm.at[dev, pl.ds(128 + SH * k, SH), :], dst.at[pl.ds(SH * dev, SH), :],
                                         sem.at[N_DEV * k + dev])
                   for k, dst in enumerate((wg, wu, wd)) for dev in range(N_DEV)]
            for cp in cps:
                cp.start()
            for cp in cps:
                cp.wait()
            loss_ref[...] = jnp.zeros_like(loss_ref)
            dg_ref[...] = jnp.zeros_like(dg_ref)

        xv = x_ref[...]
        r = _rms(xv)
        xh = xv * r
        gg = g_ref[...]
        h2 = (xh * gg).astype(_MXU)
        h2_ref[...] = h2
        gate = lax.dot_general(h2, wg[...], nt, preferred_element_type=F32)
        up = lax.dot_general(h2, wu[...], nt, preferred_element_type=F32)
        sg = _sigmoid(gate)
        silu = gate * sg
        act = (silu * up).astype(_MXU)
        act_ref[...] = act
        y = xv + jnp.dot(act, wd[...], preferred_element_type=F32)
        err = y - t_ref[...]
        loss_ref[...] += jnp.sum(err * err, axis=0, keepdims=True)
        dy = err * (1.0 / D)
        dyb = dy.astype(_MXU)
        dyb_ref[...] = dyb
        dact = lax.dot_general(dyb, wd[...], nt, preferred_element_type=F32)
        dgate = (dact * up * (sg * (1.0 + gate * (1.0 - sg)))).astype(_MXU)
        dup = (dact * silu).astype(_MXU)
        dgu_ref[:, 0:FF] = dgate
        dgu_ref[:, FF:2 * FF] = dup
        dh2 = (jnp.dot(dgate, wg[...], preferred_element_type=F32)
               + jnp.dot(dup, wu[...], preferred_element_type=F32))
        dg_ref[...] += jnp.sum(dh2 * xh, axis=0, keepdims=True)
        dx1 = dy + _rms_bwd(dh2, xh, r, gg)
        dx1_ref[...] = dx1
        dx1b_ref[...] = dx1.astype(_MXU)

    anyspec = pl.BlockSpec(memory_space=pl.ANY)
    sd = lambda w, dt: jax.ShapeDtypeStruct((T, w), dt)
    return pl.pallas_call(
        body, name="ffn", grid=(T // TF,),
        in_specs=[_rows(TF, D), _rows(TF, D), anyspec, _full((1, D))],
        out_specs=[_rows(TF, D), _rows(TF, D), _rows(TF, D), _rows(TF, FF), _rows(TF, 2 * FF), _rows(TF, D),
                   _full((1, D)), _full((1, D))],
        out_shape=[sd(D, F32), sd(D, _MXU), sd(D, _MXU), sd(FF, _MXU), sd(2 * FF, _MXU), sd(D, _MXU),
                   jax.ShapeDtypeStruct((1, D), F32), jax.ShapeDtypeStruct((1, D), F32)],
        scratch_shapes=[pltpu.VMEM((FF, D), _MXU)] * 3 + [pltpu.SemaphoreType.DMA((3 * N_DEV,))],
        compiler_params=_cp(("arbitrary",)),
    )(x1, tgt, wl_all, g_ffn)


def _mix_bwd(dx1b, w_out, apre, o, z, g_a, g_dn):
    T = dx1b.shape[0]

    def body(dx_ref, w_ref, ap_ref, o_ref, z_ref, ga_ref, gd_ref, dap_ref, do_ref, dz_ref, dga_ref, dgd_ref):
        @pl.when(pl.program_id(0) == 0)
        def _():
            dga_ref[...] = jnp.zeros_like(dga_ref)
            dgd_ref[...] = jnp.zeros_like(dgd_ref)

        dmix = lax.dot_general(dx_ref[...], w_ref[...], (((1,), (1,)), ((), ())), preferred_element_type=F32)
        ap = ap_ref[...]
        ra = _rms(ap)
        ah = ap * ra
        da = dmix[:, 0:AW]
        dga_ref[...] += jnp.sum(da * ah, axis=0, keepdims=True)
        dap_ref[...] = _rms_bwd(da, ah, ra, ga_ref[...])
        zz = z_ref[...]
        gd = gd_ref[...]
        for h in range(NHD):
            cs = slice(DHD * h, DHD * (h + 1))
            dd = dmix[:, AW + DHD * h:AW + DHD * (h + 1)]
            oh = o_ref[h]
            ro = _rms(oh)
            ohh = oh * ro
            zh = zz[:, cs]
            sz = _sigmoid(zh)
            dz_ref[:, cs] = dd * (ohh * gd) * (sz * (1.0 + zh * (1.0 - sz)))
            don = dd * (zh * sz)
            dgd_ref[...] += jnp.sum(don * ohh, axis=0, keepdims=True)
            do_ref[h] = _rms_bwd(don, ohh, ro, gd)

    return pl.pallas_call(
        body, name="mix_bwd", grid=(T // TM,),
        in_specs=[_rows(TM, D), _full((D, D)), _rows(TM, AW), _hm_rows(TM), _rows(TM, DW), _full((1, AW)),
                  _full((1, DHD))],
        out_specs=[_rows(TM, AW), _hm_rows(TM), _rows(TM, DW), _full((1, AW)), _full((1, DHD))],
        out_shape=[jax.ShapeDtypeStruct((T, AW), F32), jax.ShapeDtypeStruct((NHD, T, DHD), F32),
                   jax.ShapeDtypeStruct((T, DW), F32), jax.ShapeDtypeStruct((1, AW), F32),
                   jax.ShapeDtypeStruct((1, DHD), F32)],
        compiler_params=_cp(("arbitrary",)),
    )(dx1b, w_out, apre, o, z, g_a, g_dn)


def _inproj_bwd(dqn, dkn, dv, araw, ddraw, dz, dba, x, dx1, w_in, g_mix, qg_t, kg_t):
    T = x.shape[0]

    def body(dqn_ref, dkn_ref, dv_ref, ar_ref, dd_ref, dz_ref, dba_ref, x_ref, dx1_ref, w_hbm, g_ref, qg_ref, kg_ref,
             dx_ref, dp_ref, dgm_ref, dqg_ref, dkg_ref, w_ref, w_sem):
        @pl.when(pl.program_id(0) == 0)
        def _():
            cp = pltpu.make_async_copy(w_hbm, w_ref, w_sem)
            cp.start()
            cp.wait()
            dgm_ref[...] = jnp.zeros_like(dgm_ref)
            dqg_ref[...] = jnp.zeros_like(dqg_ref)
            dkg_ref[...] = jnp.zeros_like(dkg_ref)

        bd = _block_ones(AW // 2, DHA)

        def head_norm_bwd(raw, dyn, gain, dg_ref):
            r = _head_rms(raw, bd, DHA)
            xh = raw * r
            dg_ref[...] += jnp.sum(dyn * xh, axis=0, keepdims=True)
            dxh = dyn * gain
            return r * (dxh - xh * (_head_sum(dxh * xh, bd) * (1.0 / DHA)))

        ar = ar_ref[...]
        dq = head_norm_bwd(ar[:, 0:AW], dqn_ref[...] * (DHA ** -0.5), qg_ref[...], dqg_ref)
        dk = head_norm_bwd(ar[:, AW:2 * AW], dkn_ref[...], kg_ref[...], dkg_ref)
        nt = (((1,), (1,)), ((), ()))
        dh = jnp.zeros((TM, D), F32)
        for lo, val in ((0, dq), (AW, dk), (2 * AW, dv_ref[...]), (1536, dd_ref[...]), (3072, dz_ref[...]),
                        (3584, dba_ref[...])):
            vb = val.astype(_MXU)
            wd_ = val.shape[1]
            dp_ref[:, lo:lo + wd_] = vb
            dh = dh + lax.dot_general(vb, w_ref[:, lo:lo + wd_], nt, preferred_element_type=F32)
        xv = x_ref[...]
        r = _rms(xv)
        xh = xv * r
        dgm_ref[...] += jnp.sum(dh * xh, axis=0, keepdims=True)
        dx_ref[...] = dx1_ref[...] + _rms_bwd(dh, xh, r, g_ref[...])

    return pl.pallas_call(
        body, name="inproj_bwd", grid=(T // TM,),
        in_specs=[_rows(TM, AW), _rows(TM, AW), _rows(TM, AW), _rows(TM, 1536), _rows(TM, 1536), _rows(TM, DW),
                  _rows(TM, 8), _rows(TM, D), _rows(TM, D), pl.BlockSpec(memory_space=pl.ANY), _full((1, D)),
                  _full((1, AW)), _full((1, AW))],
        out_specs=[_rows(TM, D), _rows(TM, 3592), _full((1, D)), _full((1, AW)), _full((1, AW))],
        out_shape=[jax.ShapeDtypeStruct((T, D), F32), jax.ShapeDtypeStruct((T, 3592), _MXU),
                   jax.ShapeDtypeStruct((1, D), F32), jax.ShapeDtypeStruct((1, AW), F32),
                   jax.ShapeDtypeStruct((1, AW), F32)],
        scratch_shapes=[pltpu.VMEM((D, 3592), _MXU), pltpu.SemaphoreType.DMA],
        compiler_params=_cp(("arbitrary",)),
    )(dqn, dkn, dv, araw, ddraw, dz, dba, x, dx1, w_in, g_mix, qg_t, kg_t)


def _wgrad(a, b, name, tk=512, tn=None, out_dtype=F32, transposed=False):
    T, M = a.shape
    N = b.shape[1]
    tn = N if tn is None else tn
    nk = T // tk

    def body(a_ref, b_ref, o_ref, acc):
        k = pl.program_id(1)

        @pl.when(k == 0)
        def _():
            acc[...] = jnp.zeros_like(acc)

        acc[...] += lax.dot_general(a_ref[...], b_ref[...], (((0,), (0,)), ((), ())), preferred_element_type=F32)

        @pl.when(k == nk - 1)
        def _():
            r = acc[...]
            o_ref[...] = (r.T if transposed else r).astype(out_dtype)

    if transposed:
        out_spec, out_shape = pl.BlockSpec((tn, M), lambda j, k: (j, 0)), (N, M)
    else:
        out_spec, out_shape = pl.BlockSpec((M, tn), lambda j, k: (0, j)), (M, N)
    return pl.pallas_call(
        body, name=name, grid=(N // tn, nk),
        in_specs=[pl.BlockSpec((tk, M), lambda j, k: (k, 0)), pl.BlockSpec((tk, tn), lambda j, k: (k, j))],
        out_specs=out_spec,
        out_shape=jax.ShapeDtypeStruct(out_shape, out_dtype),
        scratch_shapes=[pltpu.VMEM((M, tn), F32)],
        compiler_params=_cp(("arbitrary", "arbitrary")),
    )(a, b)


def _adamw(parts, w, m, v, name, tr):
    K, R, W = parts.shape

    def body(p_ref, w_ref, m_ref, v_ref, g_ref, d_ref, nm_ref, nv_ref):
        g = p_ref[0].astype(F32)
        for k in range(1, K):
            g = g + p_ref[k].astype(F32)
        g_ref[...] = g
        nm = ADAM_B1 * m_ref[...] + (1.0 - ADAM_B1) * g
        nv = ADAM_B2 * v_ref[...] + (1.0 - ADAM_B2) * (g * g)
        nm_ref[...] = nm
        nv_ref[...] = nv
        m_hat = nm / (1.0 - ADAM_B1 ** ADAM_STEP)
        v_hat = nv / (1.0 - ADAM_B2 ** ADAM_STEP)
        d_ref[...] = -ADAM_LR * (m_hat / (jnp.sqrt(v_hat) + ADAM_EPS) + ADAM_WD * w_ref[...])

    o = jax.ShapeDtypeStruct((R, W), F32)
    return pl.pallas_call(
        body, name=name, grid=(R // tr,),
        in_specs=[pl.BlockSpec((K, tr, W), lambda i: (0, i, 0)), _rows(tr, W), _rows(tr, W), _rows(tr, W)],
        out_specs=[_rows(tr, W)] * 4,
        out_shape=[o] * 4,
        compiler_params=_cp(("arbitrary",)),
    )(parts, w, m, v)


SM_ROWS = 136
R_GMIX, R_GFFN, R_QG, R_KG, R_GA, R_GDN, R_ALOG, R_DT, R_LOSS, R_CONV, R_REL = 0, 8, 16, 24, 32, 40, 48, 49, 56, 64, 112


def _small_reduce(gathered):
    def body(p_ref, o_ref):
        s = p_ref[0]
        for k in range(1, N_DEV):
            s = s + p_ref[k]
        o_ref[...] = s
        for r0 in (R_QG, R_KG):
            rs = jnp.sum(s[r0:r0 + 4], axis=0, keepdims=True)
            o_ref[r0:r0 + 1, :] = rs + pltpu.roll(rs, DHA, 1)
        tot = jnp.sum(jnp.sum(s[R_LOSS:R_LOSS + 8], axis=0, keepdims=True), axis=1, keepdims=True)
        o_ref[R_LOSS:R_LOSS + 1, :] = jnp.broadcast_to(tot * (0.5 / D), (1, LANES))

    return pl.pallas_call(
        body, name="small_reduce",
        out_shape=jax.ShapeDtypeStruct((SM_ROWS, LANES), F32),
    )(gathered)


_WIRE = jnp.bfloat16
RA_USED, RA = 449, 464
RL = 128 + 3 * 352


def _pack_rows(parts, rows=None):
    p = jnp.concatenate([t.reshape(-1, D) for t in parts], axis=0) if len(parts) > 1 else parts[0].reshape(-1, D)
    return p if rows is None else jnp.pad(p, ((0, rows - p.shape[0]), (0, 0)))


def _unpack_rows(packed, shapes):
    out, r = [], 0
    for shp in shapes:
        nr = math.prod(shp) // D
        out.append(packed[r:r + nr].reshape(shp))
        r += nr
    return out


def _pad8(t):
    return jnp.pad(t, ((0, (-t.shape[0]) % 8), (0, 0)))


def _pack_lanes(parts):
    rows = []
    for p in parts:
        f = p.reshape(-1)
        pad = (-f.shape[0]) % LANES
        rows.append(jnp.pad(f, (0, pad)).reshape(-1, LANES))
    return jnp.concatenate(rows, axis=0)


def _unpack_lanes(packed, shapes):
    out, r = [], 0
    for shp in shapes:
        n = math.prod(shp)
        nr = -(-n // LANES)
        out.append(packed[r:r + nr].reshape(-1)[:n].reshape(shp))
        r += nr
    return out


def kernel(x, norm_mix_g, w_in, attn_q_norm_g, attn_k_norm_g, rel_bias, attn_out_norm_g, conv_w, a_log, dt_bias, dn_out_norm_g, w_out, norm_ffn_g, w_gate, w_up, w_down, loss_target, m_norm_mix_g, m_w_in, m_attn_q_norm_g, m_attn_k_norm_g, m_rel_bias, m_attn_out_norm_g, m_conv_w, m_a_log, m_dt_bias, m_dn_out_norm_g, m_w_out, m_norm_ffn_g, m_w_gate, m_w_up, m_w_down, v_norm_mix_g, v_w_in, v_attn_q_norm_g, v_attn_k_norm_g, v_rel_bias, v_attn_out_norm_g, v_conv_w, v_a_log, v_dt_bias, v_dn_out_norm_g, v_w_out, v_norm_ffn_g, v_w_gate, v_w_up, v_w_down):
    xs, tgt = x[0], loss_target[0]
    T = xs.shape[0]
    my_idx = 4 * lax.axis_index("x") + 2 * lax.axis_index("y") + lax.axis_index("c")
    late_w = (w_out[0], w_gate[0], w_up[0], w_down[0])
    late_shapes = [w.shape for w in late_w]

    wa_all = _all_gather(_pack_rows([w_in[0].astype(_MXU)], RA), "gather_w_in")
    cw_all = _all_gather(jnp.pad(conv_w[0], ((0, 4), (0, 64))), "gather_conv")
    by_dev = lambda a, k: a.reshape(N_DEV, D, k).transpose(1, 0, 2).reshape(D, N_DEV * k)
    W_in = by_dev(wa_all[:, 0:RA_USED], RA_USED)
    conv_full = cw_all[:, 0:CONV_K, 0:192].transpose(1, 0, 2).reshape(CONV_K, 1536)

    qg_t = jnp.tile(attn_q_norm_g, (1, NHA))
    kg_t = jnp.tile(attn_k_norm_g, (1, NHA))
    z4 = jnp.zeros((1, NHD), F32)
    alog8 = jnp.concatenate([z4, a_log], axis=1)
    dtb8 = jnp.concatenate([z4, dt_bias], axis=1)

    late_t = lambda ts: (ts[0], ts[1].T, ts[2].T, ts[3])
    araw, an, draw, z, ba, hb, wl_all = _inproj(xs, norm_mix_g, W_in, qg_t, kg_t,
                                                _pack_rows([w.astype(_MXU) for w in late_t(late_w)]))
    W_out = wl_all[:, 0:128].reshape(D, D)
    tab, tabt = _bias_tables(jnp.pad(rel_bias[0].T, ((0, 0), (0, VAR0 - 257))))
    apre = _attn_fwd(an, tab)
    bat = ba.T
    dn_args = (draw, conv_full, ba, bat, alog8, dtb8, alog8.T, dtb8.T)
    u, w, kd, tm, wq, km, mq, wt, elb, cv = _dn_prep(*dn_args)
    o, vn, sn = _dn_scan(u, wq, km, elb)
    x1, mix = _post_mix(apre, o, z, xs, W_out, attn_out_norm_g, dn_out_norm_g)

    dx1, dx1b, h2, act, dgu, dyb, loss_row, dgffn = _ffn(x1, tgt, wl_all, norm_ffn_g)

    by_cols = lambda g, k: g.reshape(D, N_DEV, k).transpose(1, 0, 2).reshape(N_DEV, -1, D)
    gW_out = _wgrad(mix, dx1b, "wgrad_out", out_dtype=_WIRE)
    gW_gu_t = _wgrad(h2, dgu, "wgrad_gate_up", tn=FF, out_dtype=_WIRE, transposed=True)
    gW_down = _wgrad(dyb, act, "wgrad_down", out_dtype=_WIRE, transposed=True)
    send_late = jnp.concatenate(
        [gW_out.reshape(N_DEV, 128, D), gW_gu_t[0:FF].reshape(N_DEV, 352, D), gW_gu_t[FF:].reshape(N_DEV, 352, D),
         gW_down.reshape(N_DEV, 352, D)], axis=1)

    dap, do, dz, dga, dgdn = _mix_bwd(dx1b, W_out, apre, o, z, attn_out_norm_g, dn_out_norm_g)
    dqn, dkn, dv, dtabt, recv_late = _attn_bwd(an, dap, tabt, send_late)
    drel = _bias_grad(dtabt)
    du, dw, dqd, dkd, dgx = _dn_scan_bwd(do, mq, kd, wt, sn, vn, elb)
    ddraw, dba, sm, dcw = _dn_post_bwd(draw, cv, *dn_args[1:], du, dw, dqd, dkd, dgx, do, vn, tm, u, w)
    gx, dproj, dgmix, dqg, dkg = _inproj_bwd(dqn, dkn, dv, araw, ddraw, dz, dba, xs, dx1, W_in, norm_mix_g, qg_t, kg_t)

    gW_in = _wgrad(hb, dproj, "wgrad_in", tk=256, out_dtype=_WIRE)
    send_in = jnp.pad(by_cols(gW_in, RA_USED), ((0, 0), (0, RA - RA_USED), (0, 0)))
    recv_in = _all_to_all(send_in, "scatter_w_in")
    late_m = (m_w_out[0], m_w_gate[0], m_w_up[0], m_w_down[0])
    late_v = (v_w_out[0], v_w_gate[0], v_w_up[0], v_w_down[0])
    outs_late = _adamw(recv_late, _pack_rows(late_t(late_w)), _pack_rows(late_t(late_m)), _pack_rows(late_t(late_v)),
                       "adamw_late", 32)
    outs_in = _adamw(recv_in, _pack_rows([w_in[0]], RA), _pack_rows([m_w_in[0]], RA), _pack_rows([v_w_in[0]], RA),
                     "adamw_w_in", 16)
    late_t_shapes = [t.shape for t in late_t(late_w)]
    big = [[_unpack_rows(a, [w_in[0].shape])[0]] + list(late_t(_unpack_rows(b, late_t_shapes)))
           for a, b in zip(outs_in, outs_late)]
    bg, bd_, bm, bv = big

    partial = jnp.concatenate(
        [dgmix.reshape(8, LANES), dgffn.reshape(8, LANES), _pad8(dqg.reshape(4, LANES)), _pad8(dkg.reshape(4, LANES)),
         _pad8(dga.reshape(4, LANES)), _pad8(dgdn), sm, loss_row.reshape(8, LANES),
         dcw[0:CONV_K].reshape(48, LANES), drel.reshape(24, LANES)], axis=0)
    S = _small_reduce(_all_gather(partial, "gather_small"))
    loss = S[R_LOSS, 0]
    g_conv = lax.dynamic_slice(S[R_CONV:R_CONV + 48].reshape(CONV_K, 1536), (0, 192 * my_idx), (CONV_K, 192))
    sg = [S[R_GMIX:R_GMIX + 8].reshape(1, D), S[R_QG:R_QG + 1, 0:DHA], S[R_KG:R_KG + 1, 0:DHA],
          S[R_REL:R_REL + 24].reshape(NHA, 384)[:, 0:257].T, S[R_GA:R_GA + 4].reshape(1, AW), g_conv,
          S[R_ALOG:R_ALOG + 1, NHD:2 * NHD], S[R_DT:R_DT + 1, NHD:2 * NHD], S[R_GDN:R_GDN + 1], S[R_GFFN:R_GFFN + 8].reshape(1, D)]
    sw = [norm_mix_g, attn_q_norm_g, attn_k_norm_g, rel_bias[0], attn_out_norm_g, conv_w[0], a_log, dt_bias, dn_out_norm_g, norm_ffn_g]
    smm = [m_norm_mix_g, m_attn_q_norm_g, m_attn_k_norm_g, m_rel_bias[0], m_attn_out_norm_g, m_conv_w[0], m_a_log, m_dt_bias, m_dn_out_norm_g, m_norm_ffn_g]
    svv = [v_norm_mix_g, v_attn_q_norm_g, v_attn_k_norm_g, v_rel_bias[0], v_attn_out_norm_g, v_conv_w[0], v_a_log, v_dt_bias, v_dn_out_norm_g, v_norm_ffn_g]
    s_shapes = [t.shape for t in sw]
    pk = lambda ts: _pack_lanes(ts)
    pg = pk(sg)
    padr = (-pg.shape[0]) % 8
    padz = lambda t: jnp.pad(t, ((0, padr), (0, 0)))
    s_out = _adamw(padz(pg)[None], padz(pk(sw)), padz(pk(smm)), padz(pk(svv)), "adamw_small", pg.shape[0] + padr)
    s_g, s_d, s_m, s_v = (_unpack_lanes(t, s_shapes) for t in s_out)

    lead = lambda t: t[None]
    def ordered(small, big):
        nm, q, k, rel, ao, cw, al, dtb, dno, nf = small
        wi, wo, wgt, wu, wdn = big
        return [nm, lead(wi), q, k, lead(rel), ao, lead(cw), al, dtb, dno, lead(wo), nf, lead(wgt), lead(wu), lead(wdn)]
    outs = [loss, gx[None]]
    for small, big in ((s_g, bg), (s_d, bd_), (s_m, bm), (s_v, bv)):
        outs += ordered(small, big)
    return tuple(outs)
```

```python
import functools
import math

import jax
import jax.numpy as jnp
from jax import lax
from jax.experimental import pallas as pl
from jax.experimental.pallas import tpu as pltpu

F32 = jnp.float32
BF16 = jnp.bfloat16
_MXU = jnp.bfloat16

D = 1024
AW = 512
NHA = 8
DHA = 64
CH = 64
BAND = 9
NHD = 4
DHD = 128
DW = 512
FF = 2816
EPS = 1e-6
NEG = -1e30
N_DEV = 8
LANES = 128
VMEM_LIMIT = 56 * 1024 * 1024

ADAM_LR = 0.001
ADAM_B1 = 0.9
ADAM_B2 = 0.999
ADAM_EPS = 1e-08
ADAM_WD = 0.01
ADAM_STEP = 10

MESH_T = pl.DeviceIdType.MESH


def _cp(sem=None, vmem=VMEM_LIMIT):
    kw = dict(vmem_limit_bytes=vmem)
    if sem is not None:
        kw["dimension_semantics"] = sem
    return pltpu.CompilerParams(**kw)


def _dot(a, b):
    return jnp.dot(a.astype(_MXU), b.astype(_MXU), preferred_element_type=F32)


def _dot_nt(a, b):
    return lax.dot_general(a.astype(_MXU), b.astype(_MXU), (((1,), (1,)), ((), ())), preferred_element_type=F32)


def _dot_tn(a, b):
    return lax.dot_general(a.astype(_MXU), b.astype(_MXU), (((0,), (0,)), ((), ())), preferred_element_type=F32)


def _split2(x):
    hi = x.astype(BF16)
    lo = (x - hi.astype(F32)).astype(BF16)
    return hi, lo


def _dot_x2(x, ones_b):
    hi, lo = _split2(x)
    return jnp.dot(hi, ones_b, preferred_element_type=F32) + jnp.dot(lo, ones_b, preferred_element_type=F32)


def _dot_x2_nt(x, ones_b):
    hi, lo = _split2(x)
    dn = (((1,), (1,)), ((), ()))
    return lax.dot_general(hi, ones_b, dn, preferred_element_type=F32) + lax.dot_general(
        lo, ones_b, dn, preferred_element_type=F32)


def _iota(shape, dim):
    return lax.broadcasted_iota(jnp.int32, shape, dim)


def _block_ones(n, blk, dtype=BF16):
    r, c = _iota((n, n), 0), _iota((n, n), 1)
    return jnp.where((r // blk) == (c // blk), 1.0, 0.0).astype(dtype)


def _sigmoid(x):
    return 1.0 / (1.0 + jnp.exp(-x))


def _softplus(x):
    return jnp.maximum(x, 0.0) + jnp.log(1.0 + jnp.exp(-jnp.abs(x)))


def _col(x, k):
    lane = _iota(x.shape, 1)
    return jnp.sum(jnp.where(lane == k, x, 0.0), axis=1, keepdims=True)


def _row(x, k):
    sub = _iota(x.shape, 0)
    return jnp.sum(jnp.where(sub == k, x, 0.0), axis=0, keepdims=True)


def _my_pos():
    return lax.axis_index("x"), lax.axis_index("y"), lax.axis_index("c")


def _all_gather(x2d, name):
    R, W = x2d.shape

    def body(x_ref, out_ref, send_sems, recv_sems, local_sem):
        ag = _Gather(x_ref, out_ref, send_sems, recv_sems, local_sem)
        ag.start()
        ag.forward()
        ag.finish()

    return pl.pallas_call(
        body, name=name,
        out_shape=jax.ShapeDtypeStruct((N_DEV, R, W), x2d.dtype),
        in_specs=[pl.BlockSpec(memory_space=pl.ANY)],
        out_specs=pl.BlockSpec(memory_space=pl.ANY),
        scratch_shapes=_COMM_SEMS,
    )(x2d)


_COMM_SEMS = [pltpu.SemaphoreType.DMA((7,)), pltpu.SemaphoreType.DMA((7,)), pltpu.SemaphoreType.DMA]


class _Gather:
    def __init__(self, x_ref, out_ref, send_sems, recv_sems, local_sem):
        x, y, c = _my_pos()
        me, sibling = (x, y, c), (x, y, 1 - c)
        chips = [(1 - x, y), (x, 1 - y), (1 - x, 1 - y)]

        def slot(px, py, pc):
            return out_ref.at[4 * px + 2 * py + pc]

        def copy(k, block, to, src=None):
            return pltpu.make_async_remote_copy(
                src_ref=slot(*block) if src is None else src, dst_ref=slot(*block),
                send_sem=send_sems.at[k], recv_sem=recv_sems.at[k], device_id=to, device_id_type=MESH_T)

        self.mine = pltpu.make_async_copy(x_ref, slot(*me), local_sem)
        self.first = [copy(0, me, sibling, src=x_ref)]
        self.first += [copy(1 + j, me, (*chip, c), src=x_ref) for j, chip in enumerate(chips)]
        self.passed = [copy(4 + j, (*chip, c), sibling) for j, chip in enumerate(chips)]
        self.from_chips = [copy(1 + j, (*chip, c), me) for j, chip in enumerate(chips)]
        self.from_sibling = [copy(0, sibling, me)] + [copy(4 + j, (*chip, 1 - c), me) for j, chip in enumerate(chips)]

    def start(self):
        self.mine.start()
        for cp in self.first:
            cp.start()

    def forward(self):
        for arrived, onward in zip(self.from_chips, self.passed):
            arrived.wait_recv()
            onward.start()

    def finish(self):
        for cp in self.from_sibling:
            cp.wait_recv()
        for cp in self.first + self.passed:
            cp.wait_send()
        self.mine.wait()


class _Scatter:
    def __init__(self, s_ref, r_ref, send_sems, recv_sems, local_sem):
        x, y, c = _my_pos()
        self.mine = pltpu.make_async_copy(s_ref.at[4 * x + 2 * y + c], r_ref.at[0], local_sem)
        self.copies = []
        for m in range(1, N_DEV):
            px = x ^ ((m >> 2) & 1)
            py = y ^ ((m >> 1) & 1)
            pc = c ^ (m & 1)
            self.copies.append(pltpu.make_async_remote_copy(
                src_ref=s_ref.at[4 * px + 2 * py + pc], dst_ref=r_ref.at[m],
                send_sem=send_sems.at[m - 1], recv_sem=recv_sems.at[m - 1],
                device_id=(px, py, pc), device_id_type=MESH_T))

    def start(self):
        self.mine.start()
        for cp in self.copies:
            cp.start()

    def finish(self):
        for cp in self.copies:
            cp.wait_recv()
        for cp in self.copies:
            cp.wait_send()
        self.mine.wait()


def _all_to_all(send, name):
    def body(s_ref, r_ref, send_sems, recv_sems, local_sem):
        sc = _Scatter(s_ref, r_ref, send_sems, recv_sems, local_sem)
        sc.start()
        sc.finish()

    return pl.pallas_call(
        body, name=name,
        out_shape=jax.ShapeDtypeStruct(send.shape, send.dtype),
        in_specs=[pl.BlockSpec(memory_space=pl.ANY)],
        out_specs=pl.BlockSpec(memory_space=pl.ANY),
        scratch_shapes=_COMM_SEMS,
    )(send)


TM = 512
TF = 256
TG = 512


def _full(shape):
    nd = len(shape)
    return pl.BlockSpec(shape, lambda i: (0,) * nd)


def _rows(tm, w):
    return pl.BlockSpec((tm, w), lambda i: (i, 0))


def _head_sum(x, bd):
    return jnp.concatenate([_dot_x2(x[:, 0:256], bd), _dot_x2(x[:, 256:512], bd)], axis=1)


def _head_rms(x, bd, width):
    return lax.rsqrt(_head_sum(x * x, bd) * (1.0 / width) + EPS)


def _inproj(x, g_mix, w_in, qg_t, kg_t, later_w):
    T = x.shape[0]
    nt = T // TM

    def body(x_ref, g_ref, w_ref, qg_ref, kg_ref, lw_ref, araw_ref, an_ref, draw_ref, z_ref, ba_ref, h_ref, lw_all,
             send_sems, recv_sems, local_sem):
        i = pl.program_id(0)
        ag = _Gather(lw_ref, lw_all, send_sems, recv_sems, local_sem)
        pl.when(i == 0)(ag.start)
        pl.when(i == nt // 2)(ag.forward)
        xv = x_ref[...]
        r = lax.rsqrt(jnp.mean(xv * xv, axis=1, keepdims=True) + EPS)
        h = (xv * r * g_ref[...]).astype(_MXU)
        h_ref[...] = h
        a = jnp.dot(h, w_ref[:, 0:1536], preferred_element_type=F32)
        araw_ref[...] = a
        bd = _block_ones(AW // 2, DHA)
        q = a[:, 0:AW]
        k = a[:, AW:2 * AW]
        qn = q * _head_rms(q, bd, DHA) * (qg_ref[...] * (DHA ** -0.5))
        kn = k * _head_rms(k, bd, DHA) * kg_ref[...]
        an_ref[:, 0:AW] = qn.astype(_MXU)
        an_ref[:, AW:2 * AW] = kn.astype(_MXU)
        an_ref[:, 2 * AW:3 * AW] = a[:, 2 * AW:3 * AW].astype(_MXU)
        draw_ref[...] = jnp.dot(h, w_ref[:, 1536:3072], preferred_element_type=F32)
        z_ref[...] = jnp.dot(h, w_ref[:, 3072:3584], preferred_element_type=F32)
        ba_ref[...] = jnp.dot(h, w_ref[:, 3584:3592], preferred_element_type=F32)
        pl.when(i == nt - 1)(ag.finish)

    anyspec = pl.BlockSpec(memory_space=pl.ANY)
    return pl.pallas_call(
        body, name="inproj", grid=(nt,),
        in_specs=[_rows(TM, D), _full((1, D)), _full((D, 3592)), _full((1, AW)), _full((1, AW)), anyspec],
        out_specs=[_rows(TM, 1536), _rows(TM, 1536), _rows(TM, 1536), _rows(TM, DW), _rows(TM, 8), _rows(TM, D),
                   anyspec],
        out_shape=[jax.ShapeDtypeStruct((T, 1536), F32), jax.ShapeDtypeStruct((T, 1536), _MXU),
                   jax.ShapeDtypeStruct((T, 1536), F32), jax.ShapeDtypeStruct((T, DW), F32),
                   jax.ShapeDtypeStruct((T, 8), F32), jax.ShapeDtypeStruct((T, D), _MXU),
                   jax.ShapeDtypeStruct((N_DEV,) + later_w.shape, later_w.dtype)],
        scratch_shapes=_COMM_SEMS,
        compiler_params=_cp(("arbitrary",)),
    )(x, g_mix, w_in, qg_t, kg_t, later_w)


TQ = 256
TW = 768
T_LO, T_HI = 65, 256
VAR0 = 384
TOEP = 1024


def _bias_tables(rb_t):
    def body(rb_ref, tab_ref, tabt_ref):
        h = pl.program_id(0)
        rb8 = jnp.broadcast_to(_row(rb_ref[...], h), (8, VAR0))
        n = _iota((VAR0, TOEP), 1)
        t = _iota((VAR0, TOEP), 0)

        def line(m):
            onehot = jnp.where(jnp.clip(512 - m, -128, 128) + 128 == t, 1.0, 0.0).astype(BF16)
            return sum(jnp.dot(p, onehot, preferred_element_type=F32) for p in _split3(rb8))[0:1, :]

        def band(r, j, first_key):
            return ((j >> 6) >= (r >> 6)) & ((j >> 6) <= (r >> 6) + 8) & (j >= first_key)

        g = line(jnp.where(n < TW, n, n - TOEP))
        tab = pltpu.roll(jnp.broadcast_to(g, (TQ, TOEP)), 0, 1, stride=1, stride_axis=0)[:, 0:TW]
        gt = line(jnp.where(n < TQ, -n, TOEP - n))
        tabt = pltpu.roll(jnp.broadcast_to(gt, (TW, TOEP)), 0, 1, stride=1, stride_axis=0)[:, 0:TQ]
        for v in range(3):
            first_key = max(512 - TQ * v, 0)
            tab_ref[v, 0] = jnp.where(band(_iota((TQ, TW), 0), _iota((TQ, TW), 1), first_key), tab, NEG)
            tabt_ref[v, 0] = jnp.where(band(_iota((TW, TQ), 1), _iota((TW, TQ), 0), first_key), tabt, NEG)

    return pl.pallas_call(
        body, name="bias_tables", grid=(NHA,),
        in_specs=[_full((NHA, VAR0))],
        out_specs=[pl.BlockSpec((3, 1, TQ, TW), lambda h: (0, h, 0, 0)),
                   pl.BlockSpec((3, 1, TW, TQ), lambda h: (0, h, 0, 0))],
        out_shape=[jax.ShapeDtypeStruct((3, NHA, TQ, TW), F32), jax.ShapeDtypeStruct((3, NHA, TW, TQ), F32)],
        compiler_params=_cp(("arbitrary",)),
    )(rb_t)


def _bias_grad(dtabt):
    def body(d_ref, o_ref):
        a, b = _iota((TQ, TQ), 0), _iota((TQ, TQ), 1)
        anti = jnp.where(a + b == TQ - 1, 1.0, 0.0).astype(BF16)
        drev = sum(jnp.dot(t, anti, preferred_element_type=F32) for t in _split3(d_ref[0]))
        wide = jnp.concatenate([drev, jnp.zeros((TW, TOEP - TQ), F32)], axis=1)
        cols = jnp.sum(pltpu.roll(wide, 0, 1, stride=1, stride_axis=0), axis=0, keepdims=True)
        c = _iota((TOEP, VAR0), 0)
        idx = jnp.clip(512 + TQ - 1 - c, -128, 128) + 128
        onehot = jnp.where(idx == _iota((TOEP, VAR0), 1), 1.0, 0.0).astype(BF16)
        cols8 = jnp.broadcast_to(cols, (8, TOEP))
        o_ref[0] = sum(jnp.dot(t, onehot, preferred_element_type=F32) for t in _split3(cols8))[0:1, :]

    return pl.pallas_call(
        body, name="bias_grad", grid=(NHA,),
        in_specs=[pl.BlockSpec((1, TW, TQ), lambda h: (h, 0, 0))],
        out_specs=pl.BlockSpec((1, 1, VAR0), lambda h: (h, 0, 0)),
        out_shape=jax.ShapeDtypeStruct((NHA, 1, VAR0), F32),
        compiler_params=_cp(("arbitrary",)),
    )(dtabt)


def _kv_spec(col, back):
    return pl.BlockSpec((TQ, AW), lambda i: (jnp.maximum(i - back, 0), col))


def _attn_fwd(an, tab):
    T = an.shape[0]

    def body(q_ref, k2_ref, k1_ref, k0_ref, v2_ref, v1_ref, v0_ref, tab_ref, o_ref):
        i = pl.program_id(0)
        kwin = jnp.concatenate([k2_ref[...], k1_ref[...], k0_ref[...]], axis=0)
        vwin = jnp.concatenate([v2_ref[...], v1_ref[...], v0_ref[...]], axis=0)
        q = q_ref[...]
        lo_half = _iota((TQ, LANES), 1) < DHA

        def scores(h):
            sl = slice(LANES * (h // 2), LANES * (h // 2 + 1))
            mask = lo_half if h % 2 == 0 else jnp.logical_not(lo_half)
            qm = jnp.where(mask, q[:, sl], jnp.zeros((TQ, LANES), q.dtype))
            return _dot_nt(qm, kwin[:, sl]) + tab_ref[0, h]

        s_next = scores(0)
        outs = []
        for h in range(NHA):
            s = s_next
            if h + 1 < NHA:
                s_next = scores(h + 1)
            sl = slice(LANES * (h // 2), LANES * (h // 2 + 1))
            m = jnp.max(s, axis=1, keepdims=True)
            e = jnp.exp(s - m)
            l = jnp.sum(e, axis=1, keepdims=True)
            outs.append(_dot(e, vwin[:, sl]) / l)
            if h % 2 == 1:
                o_ref[:, sl] = jnp.where(lo_half, outs[h - 1], outs[h])

    return pl.pallas_call(
        body, name="attn_fwd", grid=(T // TQ,),
        in_specs=[pl.BlockSpec((TQ, AW), lambda i: (i, 0)),
                  _kv_spec(1, 2), _kv_spec(1, 1), _kv_spec(1, 0), _kv_spec(2, 2), _kv_spec(2, 1), _kv_spec(2, 0),
                  pl.BlockSpec((1, NHA, TQ, TW), lambda i: (jnp.minimum(i, 2), 0, 0, 0))],
        out_specs=_rows(TQ, AW),
        out_shape=jax.ShapeDtypeStruct((T, AW), F32),
        compiler_params=_cp(("arbitrary",)),
    )(an, an, an, an, an, an, an, tab)


def _attn_bwd(an, dout, tabt, send):
    T = an.shape[0]
    nq = T // TQ

    def qi(i):
        return jnp.minimum(i, nq - 1)

    def kv_spec(col, back):
        return pl.BlockSpec((TQ, AW), lambda i: (jnp.maximum(qi(i) - back, 0), col))

    def body(q_ref, do_ref, k2_ref, k1_ref, k0_ref, v2_ref, v1_ref, v0_ref, tabt_ref, send_ref,
             dq_ref, dk_ref, dv_ref, dtab_ref, recv_ref, dk_acc, dv_acc, send_sems, recv_sems, local_sem):
        i = pl.program_id(0)
        sc = _Scatter(send_ref, recv_ref, send_sems, recv_sems, local_sem)
        pl.when(i == 0)(sc.start)

        @pl.when(i == 0)
        def _():
            dtab_ref[...] = jnp.zeros_like(dtab_ref)

        new = i % 3
        dk_acc[new] = jnp.zeros((TQ, AW), F32)
        dv_acc[new] = jnp.zeros((TQ, AW), F32)

        @pl.when(i < nq)
        def _():
            kwin = jnp.concatenate([k2_ref[...], k1_ref[...], k0_ref[...]], axis=0)
            vwin = jnp.concatenate([v2_ref[...], v1_ref[...], v0_ref[...]], axis=0)
            q = q_ref[...]
            do = do_ref[...].astype(_MXU)
            lo_half = _iota((TQ, LANES), 1) < DHA

            def front(h):
                sl = slice(LANES * (h // 2), LANES * (h // 2 + 1))
                mask = lo_half if h % 2 == 0 else jnp.logical_not(lo_half)
                zero = jnp.zeros((TQ, LANES), q.dtype)
                qm = jnp.where(mask, q[:, sl], zero)
                dom = jnp.where(mask, do[:, sl], zero)
                st = _dot_nt(kwin[:, sl], qm) + tabt_ref[0, h]
                return st, _dot_nt(vwin[:, sl], dom), qm, dom, mask

            pairs = {}

            def back(h, ptb, dsb, qm, dom, mask):
                sl = slice(LANES * (h // 2), LANES * (h // 2 + 1))
                dv = _dot(ptb, dom)
                dk = _dot(dsb, qm)
                dq = jnp.where(mask, _dot_tn(dsb, kwin[:, sl]), 0.0)
                if h % 2 == 0:
                    pairs[h // 2] = (dq, dk, dv)
                    return
                dq0, dk0, dv0 = pairs.pop(h // 2)
                dq_ref[:, sl] = dq0 + dq
                dk_pair, dv_pair = dk0 + dk, dv0 + dv
                for w in range(3):
                    slot = (i + 1 + w) % 3
                    rows = slice(TQ * w, TQ * (w + 1))
                    dk_acc[slot, :, sl] += dk_pair[rows]
                    dv_acc[slot, :, sl] += dv_pair[rows]

            nxt = front(0)
            pending = None
            for h in range(NHA):
                st, dpt, qm, dom, mask = nxt
                if h + 1 < NHA:
                    nxt = front(h + 1)
                m = jnp.max(st, axis=0, keepdims=True)
                e = jnp.exp(st - m)
                pt = e * (1.0 / jnp.sum(e, axis=0, keepdims=True))
                delta = jnp.sum(pt * dpt, axis=0, keepdims=True)
                dst = pt * (dpt - delta)
                dtab_ref[h] += dst
                if pending is not None:
                    back(*pending)
                pending = (h, pt.astype(_MXU), dst.astype(_MXU), qm, dom, mask)
            back(*pending)

        @pl.when(i >= 2)
        def _():
            done = (i + 1) % 3
            dk_ref[...] = dk_acc[done]
            dv_ref[...] = dv_acc[done]

        pl.when(i == nq + 1)(sc.finish)

    back2 = pl.BlockSpec((TQ, AW), lambda i: (jnp.maximum(i - 2, 0), 0))
    anyspec = pl.BlockSpec(memory_space=pl.ANY)
    return pl.pallas_call(
        body, name="attn_bwd", grid=(nq + 2,),
        in_specs=[pl.BlockSpec((TQ, AW), lambda i: (qi(i), 0)), pl.BlockSpec((TQ, AW), lambda i: (qi(i), 0)),
                  kv_spec(1, 2), kv_spec(1, 1), kv_spec(1, 0), kv_spec(2, 2), kv_spec(2, 1), kv_spec(2, 0),
                  pl.BlockSpec((1, NHA, TW, TQ), lambda i: (jnp.minimum(i, 2), 0, 0, 0)), anyspec],
        out_specs=[pl.BlockSpec((TQ, AW), lambda i: (qi(i), 0)), back2, back2, _full((NHA, TW, TQ)), anyspec],
        out_shape=[jax.ShapeDtypeStruct((T, AW), F32), jax.ShapeDtypeStruct((T, AW), F32),
                   jax.ShapeDtypeStruct((T, AW), F32), jax.ShapeDtypeStruct((NHA, TW, TQ), F32),
                   jax.ShapeDtypeStruct(send.shape, send.dtype)],
        scratch_shapes=[pltpu.VMEM((3, TQ, AW), F32), pltpu.VMEM((3, TQ, AW), F32)] + _COMM_SEMS,
        compiler_params=_cp(("arbitrary",)),
    )(an, dout, an, an, an, an, an, an, tabt, send)


GR = 128
NG = TG // GR
CPT = TG // CH
CONV_K = 4


def _split3(x):
    a = x.astype(BF16)
    r = x - a.astype(F32)
    b = r.astype(BF16)
    c = (r - b.astype(F32)).astype(BF16)
    return a, b, c


def _ones_dot(ones_b, x):
    return sum(jnp.dot(ones_b, t, preferred_element_type=F32) for t in _split3(x))


def _dot_ones_nt(x, ones_b):
    dn = (((1,), (1,)), ((), ()))
    return sum(lax.dot_general(t, ones_b, dn, preferred_element_type=F32) for t in _split3(x))


def _dn_masks():
    r, c = _iota((GR, GR), 0), _iota((GR, GR), 1)
    same = (r >> 6) == (c >> 6)
    one = lambda m: jnp.where(m, 1.0, 0.0).astype(BF16)
    return dict(
        tril=same & (c <= r), strict=same & (c < r), triu=same & (c >= r), strict_u=same & (c > r),
        tril_b=one(same & (c <= r)), triu_b=one(same & (c >= r)), blk_b=one(same), eye_b=one(r == c),
        eye=jnp.where(r == c, 1.0, 0.0).astype(F32),
        fold_b=one((_iota((GR, CH), 0) & (CH - 1)) == _iota((GR, CH), 1)),
        last=(_iota((GR, 1), 0) & (CH - 1)) == CH - 1,
    )


def _shift_down(x, halo, k):
    if k == 0:
        return x
    xs = pltpu.roll(x, k, 0)
    hs = pltpu.roll(halo, k, 0)
    top = jnp.where(_iota(halo.shape, 0) < k, hs, xs[0:8])
    return jnp.concatenate([top, xs[8:]], axis=0)


def _shift_up(x, halo, k):
    if k == 0:
        return x
    n = x.shape[0]
    xs = pltpu.roll(x, n - k, 0)
    hs = pltpu.roll(halo, 8 - k, 0)
    bot = jnp.where(_iota(halo.shape, 0) >= 8 - k, hs, xs[n - 8:n])
    return jnp.concatenate([xs[0:n - 8], bot], axis=0)


def _conv(x, halo, w):
    y = x * w[CONV_K - 1:CONV_K, :]
    for k in range(1, CONV_K):
        y = y + _shift_down(x, halo, k) * w[CONV_K - 1 - k:CONV_K - k, :]
    return y


def _tri_inv(lmats, eye):
    ps = [-m for m in lmats]
    rs = [eye + p for p in ps]
    for _ in range(5):
        ps = [_dot(p, p) for p in ps]
        rs = [r + _dot(r, p) for r, p in zip(rs, ps)]
    return rs


def _gate_terms(ba_g, bat_g, alog8, dtb8, alog8t, dtb8t, K):
    g8 = -jnp.exp(alog8) * _softplus(ba_g + dtb8)
    g8t = -jnp.exp(alog8t) * _softplus(bat_g + dtb8t)
    gc8 = _ones_dot(K["tril_b"], g8)
    gl8 = _ones_dot(K["blk_b"], g8)
    gcrow8 = _dot_ones_nt(g8t, K["tril_b"])
    return g8, gc8, gl8, gcrow8


def _dn_heads(c_tile, rows, beta8, gc8, gl8, gcrow8, K, pre=None):
    return _dn_heads_groups(c_tile, [(rows, beta8, gc8, gl8, gcrow8)], K, None if pre is None else [pre])[0]


def _dn_heads_groups(c_tile, groups, K, pres=None):
    ds = [_dn_head_vec(c_tile, rows, h, beta8, gc8, gl8, gcrow8, K)
          for rows, beta8, gc8, gl8, gcrow8 in groups for h in range(NHD)]
    pls = [_dot_nt(d["kb"], d["kn"]) for d in ds]
    pms = [_dot_nt(d["qn"], d["kn"]) for d in ds]
    for d, pl_, pm in zip(ds, pls, pms):
        d.update(pl=pl_, pm=pm, lmat=jnp.where(K["strict"], pl_ * d["gam_m"], 0.0), mm=pm * d["gam_m"])
    if pres is None:
        for d, tm in zip(ds, _tri_inv([d["lmat"] for d in ds], K["eye"])):
            d.update(tm=tm, u=_dot(tm, d["vb"]), w=_dot(tm, d["kg"]))
    else:
        for d, (tm, u, w) in zip(ds, [p for pre in pres for p in pre]):
            d.update(tm=tm, u=u, w=w)
    return [ds[NHD * k:NHD * (k + 1)] for k in range(len(groups))]


def _dn_head_vec(c_tile, rows, h, beta8, gc8, gl8, gcrow8, K):
    qr = c_tile[rows, DHD * h:DHD * (h + 1)]
    kr = c_tile[rows, DW + DHD * h:DW + DHD * (h + 1)]
    v = c_tile[rows, 2 * DW + DHD * h:2 * DW + DHD * (h + 1)]
    rq = lax.rsqrt(jnp.sum(qr * qr, axis=1, keepdims=True) + EPS)
    rk = lax.rsqrt(jnp.sum(kr * kr, axis=1, keepdims=True) + EPS)
    qh, kn = qr * rq, kr * rk
    qn = qh * (DHD ** -0.5)
    beta = _col(beta8, h)
    gccol, glcol, gcrow = _col(gc8, NHD + h), _col(gl8, NHD + h), _row(gcrow8, NHD + h)
    diff = gccol - gcrow
    gam_m = jnp.exp(jnp.where(K["tril"], diff, NEG))
    gam = jnp.exp(gccol)
    egl = jnp.exp(glcol - gccol)
    kb, vb = kn * beta, v * beta
    kg = kb * gam
    return dict(qr=qr, kr=kr, v=v, rq=rq, rk=rk, qh=qh, qn=qn, kn=kn, beta=beta, diff=diff, gam_m=gam_m, gam=gam,
                egl=egl, el=jnp.exp(glcol), kb=kb, vb=vb, kg=kg, qd=qn * gam, kd=kn * egl)


def _halo_prev(width):
    return pl.BlockSpec((8, width), lambda i: (jnp.maximum(i * (TG // 8) - 1, 0), 0))


def _dn_prep(draw, conv_w, ba, bat, alog8, dtb8, alog8t, dtb8t):
    T = draw.shape[0]
    nb = T // TG
    hm = lambda w, dt: jax.ShapeDtypeStruct((NHD, T, w), dt)
    hm_spec = lambda w: pl.BlockSpec((NHD, TG, w), lambda i: (0, i, 0))
    pc = lambda r, c: jax.ShapeDtypeStruct((NHD, T // CH, r, c), _MXU)
    pc_spec = lambda r, c: pl.BlockSpec((NHD, CPT, r, c), lambda i: (0, i, 0, 0))

    def body(x_ref, halo_ref, cw_ref, ba_ref, bat_ref, al_ref, dt_ref, alt_ref, dtt_ref,
             u_ref, w_ref, kd_ref, tm_ref, wq_ref, km_ref, mq_ref, wt_ref, elb_ref, cv_ref):
        i = pl.program_id(0)
        K = _dn_masks()
        halo = jnp.where(i > 0, halo_ref[...], 0.0)
        cv = _conv(x_ref[...], halo, cw_ref[...])
        cv_ref[...] = cv
        c_tile = cv * _sigmoid(cv)
        eye128 = jnp.where(_iota((DHD, DHD), 0) == _iota((DHD, DHD), 1), 1.0, 0.0).astype(_MXU)
        def gate_inputs(g):
            rows = slice(GR * g, GR * (g + 1))
            ba_g = ba_ref[rows, :]
            _, gc8, gl8, gcrow8 = _gate_terms(ba_g, bat_ref[:, rows], al_ref[...], dt_ref[...], alt_ref[...],
                                              dtt_ref[...], K)
            return rows, _sigmoid(ba_g), gc8, gl8, gcrow8

        def store(g, rows, ds):
            mmts = [_dot_nt(d["kn"], d["qn"]) * jnp.exp(jnp.where(K["triu"], -d["diff"], NEG)) for d in ds]
            mcs = [_dot(d["mm"], K["fold_b"]) for d in ds]
            mcts = [_dot(m, K["fold_b"]) for m in mmts]
            for h, d in enumerate(ds):
                tm_ref[h, rows, :] = d["tm"].astype(_MXU)
                u_ref[h, rows, :] = d["u"]
                w_ref[h, rows, :] = d["w"].astype(_MXU)
                kd_ref[h, rows, :] = d["kd"].astype(_MXU)
                elb = jnp.broadcast_to(d["el"], (GR, DHD))
                for cc in range(GR // CH):
                    ch = slice(CH * cc, CH * (cc + 1))
                    n = (GR // CH) * g + cc
                    wq_ref[h, n, 0:CH, :] = d["w"][ch].astype(_MXU)
                    wq_ref[h, n, CH:2 * CH, :] = d["qd"][ch].astype(_MXU)
                    km_ref[h, n, 0:DHD, :] = _dot_nt(eye128, d["kd"][ch]).astype(_MXU)
                    km_ref[h, n, DHD:DHD + CH, :] = mcs[h][ch].astype(_MXU)
                    mq_ref[h, n, 0:CH, :] = mcts[h][ch].astype(_MXU)
                    mq_ref[h, n, CH:CH + DHD, :] = _dot_nt(eye128, d["qd"][ch]).astype(_MXU)
                    wt_ref[h, n] = _dot_nt(eye128, d["w"][ch]).astype(_MXU)
                    elb_ref[n:n + 1, DHD * h:DHD * (h + 1)] = elb[CH * cc:CH * cc + 1, :]

        PAIR = 2
        for g0 in range(0, NG, PAIR):
            pair = [gate_inputs(g) for g in range(g0, g0 + PAIR)]
            for k, ds in enumerate(_dn_heads_groups(c_tile, pair, K)):
                store(g0 + k, pair[k][0], ds)

    return pl.pallas_call(
        body, name="dn_prep", grid=(nb,),
        in_specs=[_rows(TG, 1536), _halo_prev(1536), _full((CONV_K, 1536)), _rows(TG, 8),
                  pl.BlockSpec((8, TG), lambda i: (0, i)), _full((1, 8)), _full((1, 8)), _full((8, 1)), _full((8, 1))],
        out_specs=[hm_spec(DHD), hm_spec(DHD), hm_spec(DHD), hm_spec(GR),
                   pc_spec(2 * CH, DHD), pc_spec(DHD + CH, CH), pc_spec(CH + DHD, CH), pc_spec(DHD, CH),
                   pl.BlockSpec((CPT, NHD * DHD), lambda i: (i, 0)), _rows(TG, 1536)],
        out_shape=[hm(DHD, F32), hm(DHD, _MXU), hm(DHD, _MXU), hm(GR, _MXU),
                   pc(2 * CH, DHD), pc(DHD + CH, CH), pc(CH + DHD, CH), pc(DHD, CH),
                   jax.ShapeDtypeStruct((T // CH, NHD * DHD), F32), jax.ShapeDtypeStruct((T, 1536), F32)],
        compiler_params=_cp(("arbitrary",)),
    )(draw, draw, conv_w, ba, bat, alog8, dtb8, alog8t, dtb8t)


def _dn_scan(u, wq, km, elb):
    T = u.shape[1]
    nb = T // TG
    hm_spec = lambda wd: pl.BlockSpec((NHD, TG, wd), lambda i: (0, i, 0))

    def body(u_ref, wq_ref, km_ref, elb_ref, o_ref, vn_ref, sn_ref, S):
        @pl.when(pl.program_id(0) == 0)
        def _():
            S[...] = jnp.zeros_like(S)

        sub8 = _iota((CPT, DHD), 0)
        heads = range(NHD)

        def chunk(cc, carry):
            rs = pl.ds(pl.multiple_of(cc * CH, CH), CH)
            sh = [S[h] for h in heads]
            sb = [s.astype(_MXU) for s in sh]
            r1 = [_dot(wq_ref[h, cc], sb[h]) for h in heads]
            vnb = [(u_ref[h, rs, :] - r1[h][0:CH]).astype(_MXU) for h in heads]
            r2 = [_dot(km_ref[h, cc], vnb[h]) for h in heads]
            for h in heads:
                el = jnp.sum(jnp.where(sub8 == cc, elb_ref[:, DHD * h:DHD * (h + 1)], 0.0), axis=0, keepdims=True)
                S[h] = sh[h] * el + r2[h][0:DHD]
                sn_ref[cc, h] = sb[h]
                vn_ref[h, rs, :] = vnb[h]
                o_ref[h, rs, :] = r1[h][CH:2 * CH] + r2[h][DHD:DHD + CH]
            return carry

        lax.fori_loop(0, CPT, chunk, 0)

    return pl.pallas_call(
        body, name="dn_scan", grid=(nb,),
        in_specs=[hm_spec(DHD), pl.BlockSpec((NHD, CPT, 2 * CH, DHD), lambda i: (0, i, 0, 0)),
                  pl.BlockSpec((NHD, CPT, DHD + CH, CH), lambda i: (0, i, 0, 0)),
                  pl.BlockSpec((CPT, NHD * DHD), lambda i: (i, 0))],
        out_specs=[hm_spec(DHD), hm_spec(DHD), pl.BlockSpec((CPT, NHD, DHD, DHD), lambda i: (i, 0, 0, 0))],
        out_shape=[jax.ShapeDtypeStruct((NHD, T, DHD), F32), jax.ShapeDtypeStruct((NHD, T, DHD), _MXU),
                   jax.ShapeDtypeStruct((T // CH, NHD, DHD, DHD), _MXU)],
        scratch_shapes=[pltpu.VMEM((NHD, DHD, DHD), F32)],
        compiler_params=_cp(("arbitrary",)),
    )(u, wq, km, elb)


def _dn_scan_bwd(do, mq, kd, wt, sn, vn, elb):
    T = do.shape[1]
    nb = T // TG
    rev = lambda wd: pl.BlockSpec((NHD, TG, wd), lambda i: (0, nb - 1 - i, 0))
    rev_t = lambda r: pl.BlockSpec((NHD, CPT, r, CH), lambda i: (0, nb - 1 - i, 0, 0))

    def body(do_ref, mq_ref, kd_ref, wt_ref, sn_ref, vn_ref, elb_ref,
             du_ref, dw_ref, dqd_ref, dkd_ref, dgx_ref, dS):
        @pl.when(pl.program_id(0) == 0)
        def _():
            dS[...] = jnp.zeros_like(dS)

        last_row = _iota((CH, DHD), 0) == CH - 1
        sub8 = _iota((CPT, DHD), 0)
        heads = range(NHD)

        def chunk(k, carry):
            cc = CPT - 1 - k
            rs = pl.ds(pl.multiple_of(cc * CH, CH), CH)
            dsh = [dS[h] for h in heads]
            dsb = [d.astype(_MXU) for d in dsh]
            doc = [do_ref[h, rs, :].astype(_MXU) for h in heads]
            a = [_dot(mq_ref[h, cc], doc[h]) for h in heads]
            b = [_dot(kd_ref[h, rs, :], dsb[h]) for h in heads]
            dvn = [a[h][0:CH] + b[h] for h in heads]
            dvnb = [d.astype(_MXU) for d in dvn]
            e = [_dot(wt_ref[h, cc], dvnb[h]) for h in heads]
            for h in heads:
                el = jnp.sum(jnp.where(sub8 == cc, elb_ref[:, DHD * h:DHD * (h + 1)], 0.0), axis=0, keepdims=True)
                sn = sn_ref[cc, h]
                dS[h] = a[h][CH:CH + DHD] + dsh[h] * el - e[h]
                du_ref[h, rs, :] = dvn[h]
                c = _dot_nt(jnp.concatenate([doc[h], dvnb[h]], axis=0), sn)
                dqd_ref[h, rs, :] = c[0:CH]
                dw_ref[h, rs, :] = -c[CH:2 * CH]
                dkd_ref[h, rs, :] = _dot_nt(vn_ref[h, rs, :], dsb[h])
                part = jnp.sum(dsh[h] * sn.astype(F32), axis=0, keepdims=True) * el
                dgx_ref[h, rs, :] = jnp.where(last_row, part, 0.0)
            return carry

        lax.fori_loop(0, CPT, chunk, 0)

    o = jax.ShapeDtypeStruct((NHD, T, DHD), F32)
    return pl.pallas_call(
        body, name="dn_scan_bwd", grid=(nb,),
        in_specs=[rev(DHD), rev_t(CH + DHD), rev(DHD), rev_t(DHD),
                  pl.BlockSpec((CPT, NHD, DHD, DHD), lambda i: (nb - 1 - i, 0, 0, 0)), rev(DHD),
                  pl.BlockSpec((CPT, NHD * DHD), lambda i: (nb - 1 - i, 0))],
        out_specs=[rev(DHD)] * 5,
        out_shape=[o] * 5,
        scratch_shapes=[pltpu.VMEM((NHD, DHD, DHD), F32)],
        compiler_params=_cp(("arbitrary",)),
    )(do, mq, kd, wt, sn, vn, elb)


def _put_col(acc, k, col):
    return jnp.where(_iota(acc.shape, 1) == k, col, acc)


def _dn_post_bwd(draw, cv, conv_w, ba, bat, alog8, dtb8, alog8t, dtb8t, du, dw, dqd, dkd, dgx, do, vn, tm, u, w):
    T = draw.shape[0]
    nb = T // TG
    hm_spec = lambda wd: pl.BlockSpec((NHD, TG, wd), lambda i: (0, nb - 1 - i, 0))
    rrows = lambda w: pl.BlockSpec((TG, w), lambda i: (nb - 1 - i, 0))

    def body(x_ref, cv_ref, cw_ref, ba_ref, bat_ref, al_ref, dt_ref, alt_ref, dtt_ref,
             du_ref, dw_ref, dqd_ref, dkd_ref, dgx_ref, do_ref, vn_ref, tm_ref, u_ref, w_ref,
             dx_ref, dba_ref, sm_ref, dcw_ref, dc_ref, nxt_ref):
        i = pl.program_id(0)

        @pl.when(i == 0)
        def _():
            sm_ref[...] = jnp.zeros_like(sm_ref)
            dcw_ref[...] = jnp.zeros_like(dcw_ref)
            nxt_ref[...] = jnp.zeros_like(nxt_ref)

        K = _dn_masks()
        cv = cv_ref[...]
        sg = _sigmoid(cv)
        c_tile = cv * sg
        dsilu = sg * (1.0 + cv * (1.0 - sg))
        for g in range(NG):
            rows = slice(GR * g, GR * (g + 1))
            ba_g = ba_ref[rows, :]
            g8, gc8, gl8, gcrow8 = _gate_terms(ba_g, bat_ref[:, rows], al_ref[...], dt_ref[...], alt_ref[...],
                                               dtt_ref[...], K)
            beta8 = _sigmoid(ba_g)
            dgc8 = jnp.zeros((GR, 8), F32)
            rd8 = jnp.zeros((GR, 8), F32)
            dbeta8 = jnp.zeros((GR, 8), F32)
            pre = [(tm_ref[h, rows, :], u_ref[h, rows, :], w_ref[h, rows, :]) for h in range(NHD)]
            ds = _dn_heads(c_tile, rows, beta8, gc8, gl8, gcrow8, K, pre)
            H = range(NHD)
            eye_b = K["eye_b"].astype(_MXU)
            gam_t = [jnp.exp(jnp.where(K["triu"], -d["diff"], NEG)) for d in ds]
            doh = [do_ref[h, rows, :] for h in H]
            vnh = [vn_ref[h, rows, :] for h in H]
            tt = [_dot_nt(eye_b, d["tm"]) for d in ds]
            dvb = [_dot(tt[h], du_ref[h, rows, :]) for h in H]
            dkg = [_dot(tt[h], dw_ref[h, rows, :]) for h in H]
            plt = [_dot_nt(d["kn"], d["kb"]) for d in ds]
            pmt = [_dot_nt(d["kn"], d["qn"]) for d in ds]
            da = [-(_dot_nt(dvb[h], ds[h]["u"]) + _dot_nt(dkg[h], ds[h]["w"])) for h in H]
            dat = [-(_dot_nt(ds[h]["u"], dvb[h]) + _dot_nt(ds[h]["w"], dkg[h])) for h in H]
            dpm = [jnp.where(K["tril"], _dot_nt(doh[h], vnh[h]), 0.0) * ds[h]["gam_m"] for h in H]
            dpmt = [jnp.where(K["triu"], _dot_nt(vnh[h], doh[h]), 0.0) * gam_t[h] for h in H]
            dpl = [jnp.where(K["strict"], da[h], 0.0) * ds[h]["gam_m"] for h in H]
            dplt = [jnp.where(K["strict_u"], dat[h], 0.0) * gam_t[h] for h in H]
            dkb = [_dot(dpl[h], ds[h]["kn"]) + dkg[h] * ds[h]["gam"] for h in H]
            dqn = [_dot(dpm[h], ds[h]["kn"]) + dqd_ref[h, rows, :] * ds[h]["gam"] for h in H]
            dknm = [_dot(dplt[h], ds[h]["kb"]) + _dot(dpmt[h], ds[h]["qn"]) for h in H]
            for h, d in enumerate(ds):
                kn, dqdh, dkdh = d["kn"], dqd_ref[h, rows, :], dkd_ref[h, rows, :]
                dkn = dknm[h] + dkdh * d["egl"] + dkb[h] * d["beta"]
                dkd_kd = dkdh * d["kd"]
                rd = jnp.sum(dkd_kd, axis=1, keepdims=True)
                dgc = jnp.sum(dpl[h] * d["pl"] + dpm[h] * d["pm"] - dplt[h] * plt[h] - dpmt[h] * pmt[h]
                              + dqdh * d["qd"] + dkg[h] * d["kg"] - dkd_kd + dgx_ref[h, rows, :],
                              axis=1, keepdims=True)
                dgc8 = _put_col(dgc8, NHD + h, dgc)
                rd8 = _put_col(rd8, NHD + h, rd)
                dbeta = jnp.sum(dkb[h] * kn + dvb[h] * d["v"], axis=1, keepdims=True)
                dbeta8 = _put_col(dbeta8, h, dbeta)
                dqh = dqn[h] * (DHD ** -0.5)
                qh = d["qh"]
                dqr = d["rq"] * (dqh - qh * jnp.sum(dqh * qh, axis=1, keepdims=True))
                dkr = d["rk"] * (dkn - kn * jnp.sum(dkn * kn, axis=1, keepdims=True))
                cq = slice(DHD * h, DHD * (h + 1))
                ck = slice(DW + DHD * h, DW + DHD * (h + 1))
                cvv = slice(2 * DW + DHD * h, 2 * DW + DHD * (h + 1))
                dc_ref[rows, cq] = dqr * dsilu[rows, cq]
                dc_ref[rows, ck] = dkr * dsilu[rows, ck]
                dc_ref[rows, cvv] = dvb[h] * d["beta"] * dsilu[rows, cvv]
            dgc8 = dgc8 + jnp.where(K["last"], _ones_dot(K["blk_b"], rd8), 0.0)
            dg8 = _ones_dot(K["triu_b"], dgc8)
            sgm = _sigmoid(ba_g + dt_ref[...])
            dalpha = dg8 * (-jnp.exp(al_ref[...])) * sgm
            lane8 = _iota((GR, 8), 1)
            dba_ref[rows, :] = jnp.where(lane8 < NHD, dbeta8 * beta8 * (1.0 - beta8), dalpha)
            valid = lane8 >= NHD
            sm_ref[0:1, 0:8] += jnp.sum(jnp.where(valid, dg8 * g8, 0.0), axis=0, keepdims=True)
            sm_ref[1:2, 0:8] += jnp.sum(jnp.where(valid, dalpha, 0.0), axis=0, keepdims=True)

        dcv = dc_ref[...]
        xv = x_ref[...]
        nxt = nxt_ref[...]
        w = cw_ref[...]
        dx = dcv * w[CONV_K - 1:CONV_K, :]
        dcw_ref[CONV_K - 1:CONV_K, :] += jnp.sum(dcv * xv, axis=0, keepdims=True)
        for k in range(1, CONV_K):
            j = CONV_K - 1 - k
            up = _shift_up(dcv, nxt, k)
            dx = dx + up * w[j:j + 1, :]
            dcw_ref[j:j + 1, :] += jnp.sum(up * xv, axis=0, keepdims=True)
        dx_ref[...] = dx
        nxt_ref[...] = dcv[0:8]

    return pl.pallas_call(
        body, name="dn_post_bwd", grid=(nb,),
        in_specs=[rrows(1536), rrows(1536), _full((CONV_K, 1536)), rrows(8),
                  pl.BlockSpec((8, TG), lambda i: (0, nb - 1 - i)), _full((1, 8)), _full((1, 8)), _full((8, 1)),
                  _full((8, 1)),
                  hm_spec(DHD), hm_spec(DHD), hm_spec(DHD), hm_spec(DHD), hm_spec(DHD), hm_spec(DHD), hm_spec(DHD),
                  hm_spec(GR), hm_spec(DHD), hm_spec(DHD)],
        out_specs=[rrows(1536), rrows(8), _full((8, LANES)), _full((8, 1536))],
        out_shape=[jax.ShapeDtypeStruct((T, 1536), F32), jax.ShapeDtypeStruct((T, 8), F32),
                   jax.ShapeDtypeStruct((8, LANES), F32), jax.ShapeDtypeStruct((8, 1536), F32)],
        scratch_shapes=[pltpu.VMEM((TG, 1536), F32), pltpu.VMEM((8, 1536), F32)],
        compiler_params=_cp(("arbitrary",)),
    )(draw, cv, conv_w, ba, bat, alog8, dtb8, alog8t, dtb8t, du, dw, dqd, dkd, dgx, do, vn, tm, u, w)


def _rms(x):
    return lax.rsqrt(jnp.mean(x * x, axis=1, keepdims=True) + EPS)


def _rms_bwd(dy, xh, r, g):
    dxh = dy * g
    return r * (dxh - xh * jnp.mean(dxh * xh, axis=1, keepdims=True))


def _hm_rows(tm):
    return pl.BlockSpec((NHD, tm, DHD), lambda i: (0, i, 0))


def _post_mix(apre, o, z, x, w_out, g_a, g_dn):
    T = x.shape[0]

    def body(ap_ref, o_ref, z_ref, x_ref, w_ref, ga_ref, gd_ref, x1_ref, mix_ref):
        ap = ap_ref[...]
        parts = [ap * _rms(ap) * ga_ref[...]]
        zz = z_ref[...]
        for h in range(NHD):
            oh = o_ref[h]
            zh = zz[:, DHD * h:DHD * (h + 1)]
            parts.append(oh * _rms(oh) * gd_ref[...] * (zh * _sigmoid(zh)))
        mix = jnp.concatenate(parts, axis=1).astype(_MXU)
        mix_ref[...] = mix
        x1_ref[...] = x_ref[...] + jnp.dot(mix, w_ref[...], preferred_element_type=F32)

    return pl.pallas_call(
        body, name="post_mix", grid=(T // TM,),
        in_specs=[_rows(TM, AW), _hm_rows(TM), _rows(TM, DW), _rows(TM, D), _full((D, D)), _full((1, AW)),
                  _full((1, DHD))],
        out_specs=[_rows(TM, D), _rows(TM, D)],
        out_shape=[jax.ShapeDtypeStruct((T, D), F32), jax.ShapeDtypeStruct((T, D), _MXU)],
        compiler_params=_cp(("arbitrary",)),
    )(apre, o, z, x, w_out, g_a, g_dn)


def _ffn(x1, tgt, wl_all, g_ffn):
    T = x1.shape[0]
    SH = FF // N_DEV
    nt = (((1,), (1,)), ((), ()))

    def body(x_ref, t_ref, wl_hbm, g_ref,
             dx1_ref, dx1b_ref, h2_ref, act_ref, dgu_ref, dyb_ref, loss_ref, dg_ref, wg, wu, wd, sem):
        @pl.when(pl.program_id(0) == 0)
        def _():
            cps = [pltpu.make_async_copy(wl_hbm.at[dev, pl.ds(128 + SH * k, SH), :], dst.at[pl.ds(SH * dev, SH), :],
                                         sem.at[N_DEV * k + dev])
                   for k, dst in enumerate((wg, wu, wd)) for dev in range(N_DEV)]
            for cp in cps:
                cp.start()
            for cp in cps:
                cp.wait()
            loss_ref[...] = jnp.zeros_like(loss_ref)
            dg_ref[...] = jnp.zeros_like(dg_ref)

        xv = x_ref[...]
        r = _rms(xv)
        xh = xv * r
        gg = g_ref[...]
        h2 = (xh * gg).astype(_MXU)
        h2_ref[...] = h2
        gate = lax.dot_general(h2, wg[...], nt, preferred_element_type=F32)
        up = lax.dot_general(h2, wu[...], nt, preferred_element_type=F32)
        sg = _sigmoid(gate)
        silu = gate * sg
        act = (silu * up).astype(_MXU)
        act_ref[...] = act
        y = xv + jnp.dot(act, wd[...], preferred_element_type=F32)
        err = y - t_ref[...]
        loss_ref[...] += jnp.sum(err * err, axis=0, keepdims=True)
        dy = err * (1.0 / D)
        dyb = dy.astype(_MXU)
        dyb_ref[...] = dyb
        dact = lax.dot_general(dyb, wd[...], nt, preferred_element_type=F32)
        dgate = (dact * up * (sg * (1.0 + gate * (1.0 - sg)))).astype(_MXU)
        dup = (dact * silu).astype(_MXU)
        dgu_ref[:, 0:FF] = dgate
        dgu_ref[:, FF:2 * FF] = dup
        dh2 = (jnp.dot(dgate, wg[...], preferred_element_type=F32)
               + jnp.dot(dup, wu[...], preferred_element_type=F32))
        dg_ref[...] += jnp.sum(dh2 * xh, axis=0, keepdims=True)
        dx1 = dy + _rms_bwd(dh2, xh, r, gg)
        dx1_ref[...] = dx1
        dx1b_ref[...] = dx1.astype(_MXU)

    anyspec = pl.BlockSpec(memory_space=pl.ANY)
    sd = lambda w, dt: jax.ShapeDtypeStruct((T, w), dt)
    return pl.pallas_call(
        body, name="ffn", grid=(T // TF,),
        in_specs=[_rows(TF, D), _rows(TF, D), anyspec, _full((1, D))],
        out_specs=[_rows(TF, D), _rows(TF, D), _rows(TF, D), _rows(TF, FF), _rows(TF, 2 * FF), _rows(TF, D),
                   _full((1, D)), _full((1, D))],
        out_shape=[sd(D, F32), sd(D, _MXU), sd(D, _MXU), sd(FF, _MXU), sd(2 * FF, _MXU), sd(D, _MXU),
                   jax.ShapeDtypeStruct((1, D), F32), jax.ShapeDtypeStruct((1, D), F32)],
        scratch_shapes=[pltpu.VMEM((FF, D), _MXU)] * 3 + [pltpu.SemaphoreType.DMA((3 * N_DEV,))],
        compiler_params=_cp(("arbitrary",)),
    )(x1, tgt, wl_all, g_ffn)


def _mix_bwd(dx1b, w_out, apre, o, z, g_a, g_dn):
    T = dx1b.shape[0]

    def body(dx_ref, w_ref, ap_ref, o_ref, z_ref, ga_ref, gd_ref, dap_ref, do_ref, dz_ref, dga_ref, dgd_ref):
        @pl.when(pl.program_id(0) == 0)
        def _():
            dga_ref[...] = jnp.zeros_like(dga_ref)
            dgd_ref[...] = jnp.zeros_like(dgd_ref)

        dmix = lax.dot_general(dx_ref[...], w_ref[...], (((1,), (1,)), ((), ())), preferred_element_type=F32)
        ap = ap_ref[...]
        ra = _rms(ap)
        ah = ap * ra
        da = dmix[:, 0:AW]
        dga_ref[...] += jnp.sum(da * ah, axis=0, keepdims=True)
        dap_ref[...] = _rms_bwd(da, ah, ra, ga_ref[...])
        zz = z_ref[...]
        gd = gd_ref[...]
        for h in range(NHD):
            cs = slice(DHD * h, DHD * (h + 1))
            dd = dmix[:, AW + DHD * h:AW + DHD * (h + 1)]
            oh = o_ref[h]
            ro = _rms(oh)
            ohh = oh * ro
            zh = zz[:, cs]
            sz = _sigmoid(zh)
            dz_ref[:, cs] = dd * (ohh * gd) * (sz * (1.0 + zh * (1.0 - sz)))
            don = dd * (zh * sz)
            dgd_ref[...] += jnp.sum(don * ohh, axis=0, keepdims=True)
            do_ref[h] = _rms_bwd(don, ohh, ro, gd)

    return pl.pallas_call(
        body, name="mix_bwd", grid=(T // TM,),
        in_specs=[_rows(TM, D), _full((D, D)), _rows(TM, AW), _hm_rows(TM), _rows(TM, DW), _full((1, AW)),
                  _full((1, DHD))],
        out_specs=[_rows(TM, AW), _hm_rows(TM), _rows(TM, DW), _full((1, AW)), _full((1, DHD))],
        out_shape=[jax.ShapeDtypeStruct((T, AW), F32), jax.ShapeDtypeStruct((NHD, T, DHD), F32),
                   jax.ShapeDtypeStruct((T, DW), F32), jax.ShapeDtypeStruct((1, AW), F32),
                   jax.ShapeDtypeStruct((1, DHD), F32)],
        compiler_params=_cp(("arbitrary",)),
    )(dx1b, w_out, apre, o, z, g_a, g_dn)


def _inproj_bwd(dqn, dkn, dv, araw, ddraw, dz, dba, x, dx1, w_in, g_mix, qg_t, kg_t):
    T = x.shape[0]

    def body(dqn_ref, dkn_ref, dv_ref, ar_ref, dd_ref, dz_ref, dba_ref, x_ref, dx1_ref, w_hbm, g_ref, qg_ref, kg_ref,
             dx_ref, dp_ref, dgm_ref, dqg_ref, dkg_ref, w_ref, w_sem):
        @pl.when(pl.program_id(0) == 0)
        def _():
            cp = pltpu.make_async_copy(w_hbm, w_ref, w_sem)
            cp.start()
            cp.wait()
            dgm_ref[...] = jnp.zeros_like(dgm_ref)
            dqg_ref[...] = jnp.zeros_like(dqg_ref)
            dkg_ref[...] = jnp.zeros_like(dkg_ref)

        bd = _block_ones(AW // 2, DHA)

        def head_norm_bwd(raw, dyn, gain, dg_ref):
            r = _head_rms(raw, bd, DHA)
            xh = raw * r
            dg_ref[...] += jnp.sum(dyn * xh, axis=0, keepdims=True)
            dxh = dyn * gain
            return r * (dxh - xh * (_head_sum(dxh * xh, bd) * (1.0 / DHA)))

        ar = ar_ref[...]
        dq = head_norm_bwd(ar[:, 0:AW], dqn_ref[...] * (DHA ** -0.5), qg_ref[...], dqg_ref)
        dk = head_norm_bwd(ar[:, AW:2 * AW], dkn_ref[...], kg_ref[...], dkg_ref)
        nt = (((1,), (1,)), ((), ()))
        dh = jnp.zeros((TM, D), F32)
        for lo, val in ((0, dq), (AW, dk), (2 * AW, dv_ref[...]), (1536, dd_ref[...]), (3072, dz_ref[...]),
                        (3584, dba_ref[...])):
            vb = val.astype(_MXU)
            wd_ = val.shape[1]
            dp_ref[:, lo:lo + wd_] = vb
            dh = dh + lax.dot_general(vb, w_ref[:, lo:lo + wd_], nt, preferred_element_type=F32)
        xv = x_ref[...]
        r = _rms(xv)
        xh = xv * r
        dgm_ref[...] += jnp.sum(dh * xh, axis=0, keepdims=True)
        dx_ref[...] = dx1_ref[...] + _rms_bwd(dh, xh, r, g_ref[...])

    return pl.pallas_call(
        body, name="inproj_bwd", grid=(T // TM,),
        in_specs=[_rows(TM, AW), _rows(TM, AW), _rows(TM, AW), _rows(TM, 1536), _rows(TM, 1536), _rows(TM, DW),
                  _rows(TM, 8), _rows(TM, D), _rows(TM, D), pl.BlockSpec(memory_space=pl.ANY), _full((1, D)),
                  _full((1, AW)), _full((1, AW))],
        out_specs=[_rows(TM, D), _rows(TM, 3592), _full((1, D)), _full((1, AW)), _full((1, AW))],
        out_shape=[jax.ShapeDtypeStruct((T, D), F32), jax.ShapeDtypeStruct((T, 3592), _MXU),
                   jax.ShapeDtypeStruct((1, D), F32), jax.ShapeDtypeStruct((1, AW), F32),
                   jax.ShapeDtypeStruct((1, AW), F32)],
        scratch_shapes=[pltpu.VMEM((D, 3592), _MXU), pltpu.SemaphoreType.DMA],
        compiler_params=_cp(("arbitrary",)),
    )(dqn, dkn, dv, araw, ddraw, dz, dba, x, dx1, w_in, g_mix, qg_t, kg_t)


def _wgrad(a, b, name, tk=512, tn=None, out_dtype=F32, transposed=False):
    T, M = a.shape
    N = b.shape[1]
    tn = N if tn is None else tn
    nk = T // tk

    def body(a_ref, b_ref, o_ref, acc):
        k = pl.program_id(1)

        @pl.when(k == 0)
        def _():
            acc[...] = jnp.zeros_like(acc)

        acc[...] += lax.dot_general(a_ref[...], b_ref[...], (((0,), (0,)), ((), ())), preferred_element_type=F32)

        @pl.when(k == nk - 1)
        def _():
            r = acc[...]
            o_ref[...] = (r.T if transposed else r).astype(out_dtype)

    if transposed:
        out_spec, out_shape = pl.BlockSpec((tn, M), lambda j, k: (j, 0)), (N, M)
    else:
        out_spec, out_shape = pl.BlockSpec((M, tn), lambda j, k: (0, j)), (M, N)
    return pl.pallas_call(
        body, name=name, grid=(N // tn, nk),
        in_specs=[pl.BlockSpec((tk, M), lambda j, k: (k, 0)), pl.BlockSpec((tk, tn), lambda j, k: (k, j))],
        out_specs=out_spec,
        out_shape=jax.ShapeDtypeStruct(out_shape, out_dtype),
        scratch_shapes=[pltpu.VMEM((M, tn), F32)],
        compiler_params=_cp(("arbitrary", "arbitrary")),
    )(a, b)


def _adamw(parts, w, m, v, name, tr):
    K, R, W = parts.shape

    def body(p_ref, w_ref, m_ref, v_ref, g_ref, d_ref, nm_ref, nv_ref):
        g = p_ref[0].astype(F32)
        for k in range(1, K):
            g = g + p_ref[k].astype(F32)
        g_ref[...] = g
        nm = ADAM_B1 * m_ref[...] + (1.0 - ADAM_B1) * g
        nv = ADAM_B2 * v_ref[...] + (1.0 - ADAM_B2) * (g * g)
        nm_ref[...] = nm
        nv_ref[...] = nv
        m_hat = nm / (1.0 - ADAM_B1 ** ADAM_STEP)
        v_hat = nv / (1.0 - ADAM_B2 ** ADAM_STEP)
        d_ref[...] = -ADAM_LR * (m_hat / (jnp.sqrt(v_hat) + ADAM_EPS) + ADAM_WD * w_ref[...])

    o = jax.ShapeDtypeStruct((R, W), F32)
    return pl.pallas_call(
        body, name=name, grid=(R // tr,),
        in_specs=[pl.BlockSpec((K, tr, W), lambda i: (0, i, 0)), _rows(tr, W), _rows(tr, W), _rows(tr, W)],
        out_specs=[_rows(tr, W)] * 4,
        out_shape=[o] * 4,
        compiler_params=_cp(("arbitrary",)),
    )(parts, w, m, v)


SM_ROWS = 136
R_GMIX, R_GFFN, R_QG, R_KG, R_GA, R_GDN, R_ALOG, R_DT, R_LOSS, R_CONV, R_REL = 0, 8, 16, 24, 32, 40, 48, 49, 56, 64, 112


def _small_reduce(gathered):
    def body(p_ref, o_ref):
        s = p_ref[0]
        for k in range(1, N_DEV):
            s = s + p_ref[k]
        o_ref[...] = s
        for r0 in (R_QG, R_KG):
            rs = jnp.sum(s[r0:r0 + 4], axis=0, keepdims=True)
            o_ref[r0:r0 + 1, :] = rs + pltpu.roll(rs, DHA, 1)
        tot = jnp.sum(jnp.sum(s[R_LOSS:R_LOSS + 8], axis=0, keepdims=True), axis=1, keepdims=True)
        o_ref[R_LOSS:R_LOSS + 1, :] = jnp.broadcast_to(tot * (0.5 / D), (1, LANES))

    return pl.pallas_call(
        body, name="small_reduce",
        out_shape=jax.ShapeDtypeStruct((SM_ROWS, LANES), F32),
    )(gathered)


_WIRE = jnp.bfloat16
RA_USED, RA = 449, 464
RL = 128 + 3 * 352


def _pack_rows(parts, rows=None):
    p = jnp.concatenate([t.reshape(-1, D) for t in parts], axis=0) if len(parts) > 1 else parts[0].reshape(-1, D)
    return p if rows is None else jnp.pad(p, ((0, rows - p.shape[0]), (0, 0)))


def _unpack_rows(packed, shapes):
    out, r = [], 0
    for shp in shapes:
        nr = math.prod(shp) // D
        out.append(packed[r:r + nr].reshape(shp))
        r += nr
    return out


def _pad8(t):
    return jnp.pad(t, ((0, (-t.shape[0]) % 8), (0, 0)))


def _pack_lanes(parts):
    rows = []
    for p in parts:
        f = p.reshape(-1)
        pad = (-f.shape[0]) % LANES
        rows.append(jnp.pad(f, (0, pad)).reshape(-1, LANES))
    return jnp.concatenate(rows, axis=0)


def _unpack_lanes(packed, shapes):
    out, r = [], 0
    for shp in shapes:
        n = math.prod(shp)
        nr = -(-n // LANES)
        out.append(packed[r:r + nr].reshape(-1)[:n].reshape(shp))
        r += nr
    return out


def kernel(x, norm_mix_g, w_in, attn_q_norm_g, attn_k_norm_g, rel_bias, attn_out_norm_g, conv_w, a_log, dt_bias, dn_out_norm_g, w_out, norm_ffn_g, w_gate, w_up, w_down, loss_target, m_norm_mix_g, m_w_in, m_attn_q_norm_g, m_attn_k_norm_g, m_rel_bias, m_attn_out_norm_g, m_conv_w, m_a_log, m_dt_bias, m_dn_out_norm_g, m_w_out, m_norm_ffn_g, m_w_gate, m_w_up, m_w_down, v_norm_mix_g, v_w_in, v_attn_q_norm_g, v_attn_k_norm_g, v_rel_bias, v_attn_out_norm_g, v_conv_w, v_a_log, v_dt_bias, v_dn_out_norm_g, v_w_out, v_norm_ffn_g, v_w_gate, v_w_up, v_w_down):
    xs, tgt = x[0], loss_target[0]
    T = xs.shape[0]
    my_idx = 4 * lax.axis_index("x") + 2 * lax.axis_index("y") + lax.axis_index("c")
    late_w = (w_out[0], w_gate[0], w_up[0], w_down[0])
    late_shapes = [w.shape for w in late_w]

    wa_all = _all_gather(_pack_rows([w_in[0].astype(_MXU)], RA), "gather_w_in")
    cw_all = _all_gather(jnp.pad(conv_w[0], ((0, 4), (0, 64))), "gather_conv")
    by_dev = lambda a, k: a.reshape(N_DEV, D, k).transpose(1, 0, 2).reshape(D, N_DEV * k)
    W_in = by_dev(wa_all[:, 0:RA_USED], RA_USED)
    conv_full = cw_all[:, 0:CONV_K, 0:192].transpose(1, 0, 2).reshape(CONV_K, 1536)

    qg_t = jnp.tile(attn_q_norm_g, (1, NHA))
    kg_t = jnp.tile(attn_k_norm_g, (1, NHA))
    z4 = jnp.zeros((1, NHD), F32)
    alog8 = jnp.concatenate([z4, a_log], axis=1)
    dtb8 = jnp.concatenate([z4, dt_bias], axis=1)

    late_t = lambda ts: (ts[0], ts[1].T, ts[2].T, ts[3])
    araw, an, draw, z, ba, hb, wl_all = _inproj(xs, norm_mix_g, W_in, qg_t, kg_t,
                                                _pack_rows([w.astype(_MXU) for w in late_t(late_w)]))
    W_out = wl_all[:, 0:128].reshape(D, D)
    tab, tabt = _bias_tables(jnp.pad(rel_bias[0].T, ((0, 0), (0, VAR0 - 257))))
    apre = _attn_fwd(an, tab)
    bat = ba.T
    dn_args = (draw, conv_full, ba, bat, alog8, dtb8, alog8.T, dtb8.T)
    u, w, kd, tm, wq, km, mq, wt, elb, cv = _dn_prep(*dn_args)
    o, vn, sn = _dn_scan(u, wq, km, elb)
    x1, mix = _post_mix(apre, o, z, xs, W_out, attn_out_norm_g, dn_out_norm_g)

    dx1, dx1b, h2, act, dgu, dyb, loss_row, dgffn = _ffn(x1, tgt, wl_all, norm_ffn_g)

    by_cols = lambda g, k: g.reshape(D, N_DEV, k).transpose(1, 0, 2).reshape(N_DEV, -1, D)
    gW_out = _wgrad(mix, dx1b, "wgrad_out", out_dtype=_WIRE)
    gW_gu_t = _wgrad(h2, dgu, "wgrad_gate_up", tn=FF, out_dtype=_WIRE, transposed=True)
    gW_down = _wgrad(dyb, act, "wgrad_down", out_dtype=_WIRE, transposed=True)
    send_late = jnp.concatenate(
        [gW_out.reshape(N_DEV, 128, D), gW_gu_t[0:FF].reshape(N_DEV, 352, D), gW_gu_t[FF:].reshape(N_DEV, 352, D),
         gW_down.reshape(N_DEV, 352, D)], axis=1)

    dap, do, dz, dga, dgdn = _mix_bwd(dx1b, W_out, apre, o, z, attn_out_norm_g, dn_out_norm_g)
    dqn, dkn, dv, dtabt, recv_late = _attn_bwd(an, dap, tabt, send_late)
    drel = _bias_grad(dtabt)
    du, dw, dqd, dkd, dgx = _dn_scan_bwd(do, mq, kd, wt, sn, vn, elb)
    ddraw, dba, sm, dcw = _dn_post_bwd(draw, cv, *dn_args[1:], du, dw, dqd, dkd, dgx, do, vn, tm, u, w)
    gx, dproj, dgmix, dqg, dkg = _inproj_bwd(dqn, dkn, dv, araw, ddraw, dz, dba, xs, dx1, W_in, norm_mix_g, qg_t, kg_t)

    gW_in = _wgrad(hb, dproj, "wgrad_in", tk=256, out_dtype=_WIRE)
    send_in = jnp.pad(by_cols(gW_in, RA_USED), ((0, 0), (0, RA - RA_USED), (0, 0)))
    recv_in = _all_to_all(send_in, "scatter_w_in")
    late_m = (m_w_out[0], m_w_gate[0], m_w_up[0], m_w_down[0])
    late_v = (v_w_out[0], v_w_gate[0], v_w_up[0], v_w_down[0])
    outs_late = _adamw(recv_late, _pack_rows(late_t(late_w)), _pack_rows(late_t(late_m)), _pack_rows(late_t(late_v)),
                       "adamw_late", 32)
    outs_in = _adamw(recv_in, _pack_rows([w_in[0]], RA), _pack_rows([m_w_in[0]], RA), _pack_rows([v_w_in[0]], RA),
                     "adamw_w_in", 16)
    late_t_shapes = [t.shape for t in late_t(late_w)]
    big = [[_unpack_rows(a, [w_in[0].shape])[0]] + list(late_t(_unpack_rows(b, late_t_shapes)))
           for a, b in zip(outs_in, outs_late)]
    bg, bd_, bm, bv = big

    partial = jnp.concatenate(
        [dgmix.reshape(8, LANES), dgffn.reshape(8, LANES), _pad8(dqg.reshape(4, LANES)), _pad8(dkg.reshape(4, LANES)),
         _pad8(dga.reshape(4, LANES)), _pad8(dgdn), sm, loss_row.reshape(8, LANES),
         dcw[0:CONV_K].reshape(48, LANES), drel.reshape(24, LANES)], axis=0)
    S = _small_reduce(_all_gather(partial, "gather_small"))
    loss = S[R_LOSS, 0]
    g_conv = lax.dynamic_slice(S[R_CONV:R_CONV + 48].reshape(CONV_K, 1536), (0, 192 * my_idx), (CONV_K, 192))
    sg = [S[R_GMIX:R_GMIX + 8].reshape(1, D), S[R_QG:R_QG + 1, 0:DHA], S[R_KG:R_KG + 1, 0:DHA],
          S[R_REL:R_REL + 24].reshape(NHA, 384)[:, 0:257].T, S[R_GA:R_GA + 4].reshape(1, AW), g_conv,
          S[R_ALOG:R_ALOG + 1, NHD:2 * NHD], S[R_DT:R_DT + 1, NHD:2 * NHD], S[R_GDN:R_GDN + 1], S[R_GFFN:R_GFFN + 8].reshape(1, D)]
    sw = [norm_mix_g, attn_q_norm_g, attn_k_norm_g, rel_bias[0], attn_out_norm_g, conv_w[0], a_log, dt_bias, dn_out_norm_g, norm_ffn_g]
    smm = [m_norm_mix_g, m_attn_q_norm_g, m_attn_k_norm_g, m_rel_bias[0], m_attn_out_norm_g, m_conv_w[0], m_a_log, m_dt_bias, m_dn_out_norm_g, m_norm_ffn_g]
    svv = [v_norm_mix_g, v_attn_q_norm_g, v_attn_k_norm_g, v_rel_bias[0], v_attn_out_norm_g, v_conv_w[0], v_a_log, v_dt_bias, v_dn_out_norm_g, v_norm_ffn_g]
    s_shapes = [t.shape for t in sw]
    pk = lambda ts: _pack_lanes(ts)
    pg = pk(sg)
    padr = (-pg.shape[0]) % 8
    padz = lambda t: jnp.pad(t, ((0, padr), (0, 0)))
    s_out = _adamw(padz(pg)[None], padz(pk(sw)), padz(pk(smm)), padz(pk(svv)), "adamw_small", pg.shape[0] + padr)
    s_g, s_d, s_m, s_v = (_unpack_lanes(t, s_shapes) for t in s_out)

    lead = lambda t: t[None]
    def ordered(small, big):
        nm, q, k, rel, ao, cw, al, dtb, dno, nf = small
        wi, wo, wgt, wu, wdn = big
        return [nm, lead(wi), q, k, lead(rel), ao, lead(cw), al, dtb, dno, lead(wo), nf, lead(wgt), lead(wu), lead(wdn)]
    outs = [loss, gx[None]]
    for small, big in ((s_g, bg), (s_d, bd_), (s_m, bm), (s_v, bv)):
        outs += ordered(small, big)
    return tuple(outs)
```

```python
import functools
import math

import jax
import jax.numpy as jnp
from jax import lax
from jax.experimental import pallas as pl
from jax.experimental.pallas import tpu as pltpu

F32 = jnp.float32
BF16 = jnp.bfloat16
_MXU = jnp.bfloat16

D = 1024
AW = 512
NHA = 8
DHA = 64
CH = 64
BAND = 9
NHD = 4
DHD = 128
DW = 512
FF = 2816
EPS = 1e-6
NEG = -1e30
N_DEV = 8
LANES = 128
VMEM_LIMIT = 56 * 1024 * 1024

ADAM_LR = 0.001
ADAM_B1 = 0.9
ADAM_B2 = 0.999
ADAM_EPS = 1e-08
ADAM_WD = 0.01
ADAM_STEP = 10

MESH_T = pl.DeviceIdType.MESH


def _cp(sem=None, vmem=VMEM_LIMIT):
    kw = dict(vmem_limit_bytes=vmem)
    if sem is not None:
        kw["dimension_semantics"] = sem
    return pltpu.CompilerParams(**kw)


def _dot(a, b):
    return jnp.dot(a.astype(_MXU), b.astype(_MXU), preferred_element_type=F32)


def _dot_nt(a, b):
    return lax.dot_general(a.astype(_MXU), b.astype(_MXU), (((1,), (1,)), ((), ())), preferred_element_type=F32)


def _dot_tn(a, b):
    return lax.dot_general(a.astype(_MXU), b.astype(_MXU), (((0,), (0,)), ((), ())), preferred_element_type=F32)


def _split2(x):
    hi = x.astype(BF16)
    lo = (x - hi.astype(F32)).astype(BF16)
    return hi, lo


def _dot_x2(x, ones_b):
    hi, lo = _split2(x)
    return jnp.dot(hi, ones_b, preferred_element_type=F32) + jnp.dot(lo, ones_b, preferred_element_type=F32)


def _dot_x2_nt(x, ones_b):
    hi, lo = _split2(x)
    dn = (((1,), (1,)), ((), ()))
    return lax.dot_general(hi, ones_b, dn, preferred_element_type=F32) + lax.dot_general(
        lo, ones_b, dn, preferred_element_type=F32)


def _iota(shape, dim):
    return lax.broadcasted_iota(jnp.int32, shape, dim)


def _block_ones(n, blk, dtype=BF16):
    r, c = _iota((n, n), 0), _iota((n, n), 1)
    return jnp.where((r // blk) == (c // blk), 1.0, 0.0).astype(dtype)


def _sigmoid(x):
    return 1.0 / (1.0 + jnp.exp(-x))


def _softplus(x):
    return jnp.maximum(x, 0.0) + jnp.log(1.0 + jnp.exp(-jnp.abs(x)))


def _col(x, k):
    lane = _iota(x.shape, 1)
    return jnp.sum(jnp.where(lane == k, x, 0.0), axis=1, keepdims=True)


def _row(x, k):
    sub = _iota(x.shape, 0)
    return jnp.sum(jnp.where(sub == k, x, 0.0), axis=0, keepdims=True)


def _my_pos():
    return lax.axis_index("x"), lax.axis_index("y"), lax.axis_index("c")


def _all_gather(x2d, name):
    R, W = x2d.shape

    def body(x_ref, out_ref, send_sems, recv_sems, local_sem):
        ag = _Gather(x_ref, out_ref, send_sems, recv_sems, local_sem)
        ag.start()
        ag.forward()
        ag.finish()

    return pl.pallas_call(
        body, name=name,
        out_shape=jax.ShapeDtypeStruct((N_DEV, R, W), x2d.dtype),
        in_specs=[pl.BlockSpec(memory_space=pl.ANY)],
        out_specs=pl.BlockSpec(memory_space=pl.ANY),
        scratch_shapes=_COMM_SEMS,
    )(x2d)


_COMM_SEMS = [pltpu.SemaphoreType.DMA((7,)), pltpu.SemaphoreType.DMA((7,)), pltpu.SemaphoreType.DMA]


class _Gather:
    def __init__(self, x_ref, out_ref, send_sems, recv_sems, local_sem):
        x, y, c = _my_pos()
        me, sibling = (x, y, c), (x, y, 1 - c)
        chips = [(1 - x, y), (x, 1 - y), (1 - x, 1 - y)]

        def slot(px, py, pc):
            return out_ref.at[4 * px + 2 * py + pc]

        def copy(k, block, to, src=None):
            return pltpu.make_async_remote_copy(
                src_ref=slot(*block) if src is None else src, dst_ref=slot(*block),
                send_sem=send_sems.at[k], recv_sem=recv_sems.at[k], device_id=to, device_id_type=MESH_T)

        self.mine = pltpu.make_async_copy(x_ref, slot(*me), local_sem)
        self.first = [copy(0, me, sibling, src=x_ref)]
        self.first += [copy(1 + j, me, (*chip, c), src=x_ref) for j, chip in enumerate(chips)]
        self.passed = [copy(4 + j, (*chip, c), sibling) for j, chip in enumerate(chips)]
        self.from_chips = [copy(1 + j, (*chip, c), me) for j, chip in enumerate(chips)]
        self.from_sibling = [copy(0, sibling, me)] + [copy(4 + j, (*chip, 1 - c), me) for j, chip in enumerate(chips)]

    def start(self):
        self.mine.start()
        for cp in self.first:
            cp.start()

    def forward(self):
        for arrived, onward in zip(self.from_chips, self.passed):
            arrived.wait_recv()
            onward.start()

    def finish(self):
        for cp in self.from_sibling:
            cp.wait_recv()
        for cp in self.first + self.passed:
            cp.wait_send()
        self.mine.wait()


class _Scatter:
    def __init__(self, s_ref, r_ref, send_sems, recv_sems, local_sem):
        x, y, c = _my_pos()
        self.mine = pltpu.make_async_copy(s_ref.at[4 * x + 2 * y + c], r_ref.at[0], local_sem)
        self.copies = []
        for m in range(1, N_DEV):
            px = x ^ ((m >> 2) & 1)
            py = y ^ ((m >> 1) & 1)
            pc = c ^ (m & 1)
            self.copies.append(pltpu.make_async_remote_copy(
                src_ref=s_ref.at[4 * px + 2 * py + pc], dst_ref=r_ref.at[m],
                send_sem=send_sems.at[m - 1], recv_sem=recv_sems.at[m - 1],
                device_id=(px, py, pc), device_id_type=MESH_T))

    def start(self):
        self.mine.start()
        for cp in self.copies:
            cp.start()

    def finish(self):
        for cp in self.copies:
            cp.wait_recv()
        for cp in self.copies:
            cp.wait_send()
        self.mine.wait()


def _all_to_all(send, name):
    def body(s_ref, r_ref, send_sems, recv_sems, local_sem):
        sc = _Scatter(s_ref, r_ref, send_sems, recv_sems, local_sem)
        sc.start()
        sc.finish()

    return pl.pallas_call(
        body, name=name,
        out_shape=jax.ShapeDtypeStruct(send.shape, send.dtype),
        in_specs=[pl.BlockSpec(memory_space=pl.ANY)],
        out_specs=pl.BlockSpec(memory_space=pl.ANY),
        scratch_shapes=_COMM_SEMS,
    )(send)


TM = 512
TF = 256
TG = 512


def _full(shape):
    nd = len(shape)
    return pl.BlockSpec(shape, lambda i: (0,) * nd)


def _rows(tm, w):
    return pl.BlockSpec((tm, w), lambda i: (i, 0))


def _head_sum(x, bd):
    one_pass = lambda t: jnp.dot(t.astype(_MXU), bd.astype(_MXU), preferred_element_type=F32)
    return jnp.concatenate([one_pass(x[:, 0:256]), one_pass(x[:, 256:512])], axis=1)


def _head_rms(x, bd, width):
    return lax.rsqrt(_head_sum(x * x, bd) * (1.0 / width) + EPS)


def _inproj(x, g_mix, w_in, qg_t, kg_t, later_w):
    T = x.shape[0]
    nt = T // TM

    def body(x_ref, g_ref, w_ref, qg_ref, kg_ref, lw_ref, araw_ref, an_ref, draw_ref, z_ref, ba_ref, h_ref, lw_all,
             send_sems, recv_sems, local_sem):
        i = pl.program_id(0)
        ag = _Gather(lw_ref, lw_all, send_sems, recv_sems, local_sem)
        pl.when(i == 0)(ag.start)
        pl.when(i == nt // 2)(ag.forward)
        xv = x_ref[...]
        r = lax.rsqrt(jnp.mean(xv * xv, axis=1, keepdims=True) + EPS)
        h = (xv * r * g_ref[...]).astype(_MXU)
        h_ref[...] = h
        q = jnp.dot(h, w_ref[:, 0:AW], preferred_element_type=F32)
        k = jnp.dot(h, w_ref[:, AW:2 * AW], preferred_element_type=F32)
        v = jnp.dot(h, w_ref[:, 2 * AW:3 * AW], preferred_element_type=F32)
        draw_ref[...] = jnp.dot(h, w_ref[:, 1536:3072], preferred_element_type=F32)
        z_ref[...] = jnp.dot(h, w_ref[:, 3072:3584], preferred_element_type=F32)
        ba_ref[...] = jnp.dot(h, w_ref[:, 3584:3592], preferred_element_type=F32)
        araw_ref[:, 0:AW] = q
        araw_ref[:, AW:2 * AW] = k
        araw_ref[:, 2 * AW:3 * AW] = v
        bd = _block_ones(AW // 2, DHA)
        qn = q * _head_rms(q, bd, DHA) * (qg_ref[...] * (DHA ** -0.5))
        kn = k * _head_rms(k, bd, DHA) * kg_ref[...]
        an_ref[:, 0:AW] = qn.astype(_MXU)
        an_ref[:, AW:2 * AW] = kn.astype(_MXU)
        an_ref[:, 2 * AW:3 * AW] = v.astype(_MXU)
        pl.when(i == nt - 1)(ag.finish)

    anyspec = pl.BlockSpec(memory_space=pl.ANY)
    return pl.pallas_call(
        body, name="inproj", grid=(nt,),
        in_specs=[_rows(TM, D), _full((1, D)), _full((D, 3592)), _full((1, AW)), _full((1, AW)), anyspec],
        out_specs=[_rows(TM, 1536), _rows(TM, 1536), _rows(TM, 1536), _rows(TM, DW), _rows(TM, 8), _rows(TM, D),
                   anyspec],
        out_shape=[jax.ShapeDtypeStruct((T, 1536), F32), jax.ShapeDtypeStruct((T, 1536), _MXU),
                   jax.ShapeDtypeStruct((T, 1536), F32), jax.ShapeDtypeStruct((T, DW), F32),
                   jax.ShapeDtypeStruct((T, 8), F32), jax.ShapeDtypeStruct((T, D), _MXU),
                   jax.ShapeDtypeStruct((N_DEV,) + later_w.shape, later_w.dtype)],
        scratch_shapes=_COMM_SEMS,
        compiler_params=_cp(("arbitrary",)),
    )(x, g_mix, w_in, qg_t, kg_t, later_w)


TQ = 256
TW = 768
T_LO, T_HI = 65, 256
VAR0 = 384
TOEP = 1024


def _bias_tables(rb_t):
    def body(rb_ref, tab_ref, tabt_ref):
        h = pl.program_id(0)
        rb8 = jnp.broadcast_to(_row(rb_ref[...], h), (8, VAR0))
        n = _iota((VAR0, TOEP), 1)
        t = _iota((VAR0, TOEP), 0)

        def line(m):
            onehot = jnp.where(jnp.clip(512 - m, -128, 128) + 128 == t, 1.0, 0.0).astype(BF16)
            return sum(jnp.dot(p, onehot, preferred_element_type=F32) for p in _split3(rb8))[0:1, :]

        def band(r, j, first_key):
            return ((j >> 6) >= (r >> 6)) & ((j >> 6) <= (r >> 6) + 8) & (j >= first_key)

        g = line(jnp.where(n < TW, n, n - TOEP))
        tab = pltpu.roll(jnp.broadcast_to(g, (TQ, TOEP)), 0, 1, stride=1, stride_axis=0)[:, 0:TW]
        gt = line(jnp.where(n < TQ, -n, TOEP - n))
        tabt = pltpu.roll(jnp.broadcast_to(gt, (TW, TOEP)), 0, 1, stride=1, stride_axis=0)[:, 0:TQ]
        for v in range(3):
            first_key = max(512 - TQ * v, 0)
            tab_ref[v, 0] = jnp.where(band(_iota((TQ, TW), 0), _iota((TQ, TW), 1), first_key), tab, NEG)
            tabt_ref[v, 0] = jnp.where(band(_iota((TW, TQ), 1), _iota((TW, TQ), 0), first_key), tabt, NEG)

    return pl.pallas_call(
        body, name="bias_tables", grid=(NHA,),
        in_specs=[_full((NHA, VAR0))],
        out_specs=[pl.BlockSpec((3, 1, TQ, TW), lambda h: (0, h, 0, 0)),
                   pl.BlockSpec((3, 1, TW, TQ), lambda h: (0, h, 0, 0))],
        out_shape=[jax.ShapeDtypeStruct((3, NHA, TQ, TW), F32), jax.ShapeDtypeStruct((3, NHA, TW, TQ), F32)],
        compiler_params=_cp(("arbitrary",)),
    )(rb_t)


def _bias_grad(dtabt):
    def body(d_ref, o_ref):
        a, b = _iota((TQ, TQ), 0), _iota((TQ, TQ), 1)
        anti = jnp.where(a + b == TQ - 1, 1.0, 0.0).astype(BF16)
        drev = sum(jnp.dot(t, anti, preferred_element_type=F32) for t in _split3(d_ref[0]))
        wide = jnp.concatenate([drev, jnp.zeros((TW, TOEP - TQ), F32)], axis=1)
        cols = jnp.sum(pltpu.roll(wide, 0, 1, stride=1, stride_axis=0), axis=0, keepdims=True)
        c = _iota((TOEP, VAR0), 0)
        idx = jnp.clip(512 + TQ - 1 - c, -128, 128) + 128
        onehot = jnp.where(idx == _iota((TOEP, VAR0), 1), 1.0, 0.0).astype(BF16)
        cols8 = jnp.broadcast_to(cols, (8, TOEP))
        o_ref[0] = sum(jnp.dot(t, onehot, preferred_element_type=F32) for t in _split3(cols8))[0:1, :]

    return pl.pallas_call(
        body, name="bias_grad", grid=(NHA,),
        in_specs=[pl.BlockSpec((1, TW, TQ), lambda h: (h, 0, 0))],
        out_specs=pl.BlockSpec((1, 1, VAR0), lambda h: (h, 0, 0)),
        out_shape=jax.ShapeDtypeStruct((NHA, 1, VAR0), F32),
        compiler_params=_cp(("arbitrary",)),
    )(dtabt)


def _kv_spec(col, back):
    return pl.BlockSpec((TQ, AW), lambda i: (jnp.maximum(i - back, 0), col))


def _attn_fwd(an, tab):
    T = an.shape[0]

    def body(q_ref, k2_ref, k1_ref, k0_ref, v2_ref, v1_ref, v0_ref, tab_ref, o_ref):
        i = pl.program_id(0)
        kwin = jnp.concatenate([k2_ref[...], k1_ref[...], k0_ref[...]], axis=0)
        vwin = jnp.concatenate([v2_ref[...], v1_ref[...], v0_ref[...]], axis=0)
        q = q_ref[...]
        lo_half = _iota((TQ, LANES), 1) < DHA

        def scores(h):
            sl = slice(LANES * (h // 2), LANES * (h // 2 + 1))
            mask = lo_half if h % 2 == 0 else jnp.logical_not(lo_half)
            qm = jnp.where(mask, q[:, sl], jnp.zeros((TQ, LANES), q.dtype))
            return _dot_nt(qm, kwin[:, sl]) + tab_ref[0, h]

        s_next = scores(0)
        outs = []
        for h in range(NHA):
            s = s_next
            if h + 1 < NHA:
                s_next = scores(h + 1)
            sl = slice(LANES * (h // 2), LANES * (h // 2 + 1))
            m = jnp.max(s, axis=1, keepdims=True)
            e = jnp.exp(s - m)
            l = jnp.sum(e, axis=1, keepdims=True)
            outs.append(_dot(e, vwin[:, sl]) / l)
            if h % 2 == 1:
                o_ref[:, sl] = jnp.where(lo_half, outs[h - 1], outs[h])

    return pl.pallas_call(
        body, name="attn_fwd", grid=(T // TQ,),
        in_specs=[pl.BlockSpec((TQ, AW), lambda i: (i, 0)),
                  _kv_spec(1, 2), _kv_spec(1, 1), _kv_spec(1, 0), _kv_spec(2, 2), _kv_spec(2, 1), _kv_spec(2, 0),
                  pl.BlockSpec((1, NHA, TQ, TW), lambda i: (jnp.minimum(i, 2), 0, 0, 0))],
        out_specs=_rows(TQ, AW),
        out_shape=jax.ShapeDtypeStruct((T, AW), F32),
        compiler_params=_cp(("arbitrary",)),
    )(an, an, an, an, an, an, an, tab)


def _attn_bwd(an, dout, tabt, send):
    T = an.shape[0]
    nq = T // TQ

    def qi(i):
        return jnp.minimum(i, nq - 1)

    def kv_spec(col, back):
        return pl.BlockSpec((TQ, AW), lambda i: (jnp.maximum(qi(i) - back, 0), col))

    def body(q_ref, do_ref, k2_ref, k1_ref, k0_ref, v2_ref, v1_ref, v0_ref, tabt_ref, send_ref,
             dq_ref, dk_ref, dv_ref, dtab_ref, recv_ref, dk_acc, dv_acc, send_sems, recv_sems, local_sem):
        i = pl.program_id(0)
        sc = _Scatter(send_ref, recv_ref, send_sems, recv_sems, local_sem)
        pl.when(i == 0)(sc.start)

        @pl.when(i == 0)
        def _():
            dtab_ref[...] = jnp.zeros_like(dtab_ref)

        new = i % 3
        dk_acc[new] = jnp.zeros((TQ, AW), F32)
        dv_acc[new] = jnp.zeros((TQ, AW), F32)

        @pl.when(i < nq)
        def _():
            kwin = jnp.concatenate([k2_ref[...], k1_ref[...], k0_ref[...]], axis=0)
            vwin = jnp.concatenate([v2_ref[...], v1_ref[...], v0_ref[...]], axis=0)
            q = q_ref[...]
            do = do_ref[...].astype(_MXU)
            lo_half = _iota((TQ, LANES), 1) < DHA

            def front(h):
                sl = slice(LANES * (h // 2), LANES * (h // 2 + 1))
                mask = lo_half if h % 2 == 0 else jnp.logical_not(lo_half)
                zero = jnp.zeros((TQ, LANES), q.dtype)
                qm = jnp.where(mask, q[:, sl], zero)
                dom = jnp.where(mask, do[:, sl], zero)
                st = _dot_nt(kwin[:, sl], qm) + tabt_ref[0, h]
                return st, _dot_nt(vwin[:, sl], dom), qm, dom, mask

            pairs = {}

            def back(h, ptb, dsb, qm, dom, mask):
                sl = slice(LANES * (h // 2), LANES * (h // 2 + 1))
                dv = _dot(ptb, dom)
                dk = _dot(dsb, qm)
                dq = jnp.where(mask, _dot_tn(dsb, kwin[:, sl]), 0.0)
                if h % 2 == 0:
                    pairs[h // 2] = (dq, dk, dv)
                    return
                dq0, dk0, dv0 = pairs.pop(h // 2)
                dq_ref[:, sl] = dq0 + dq
                dk_pair, dv_pair = dk0 + dk, dv0 + dv
                for w in range(3):
                    slot = (i + 1 + w) % 3
                    rows = slice(TQ * w, TQ * (w + 1))
                    dk_acc[slot, :, sl] += dk_pair[rows]
                    dv_acc[slot, :, sl] += dv_pair[rows]

            nxt = front(0)
            pending = None
            for h in range(NHA):
                st, dpt, qm, dom, mask = nxt
                if h + 1 < NHA:
                    nxt = front(h + 1)
                m = jnp.max(st, axis=0, keepdims=True)
                e = jnp.exp(st - m)
                pt = e * (1.0 / jnp.sum(e, axis=0, keepdims=True))
                delta = jnp.sum(pt * dpt, axis=0, keepdims=True)
                dst = pt * (dpt - delta)
                dtab_ref[h] += dst
                if pending is not None:
                    back(*pending)
                pending = (h, pt.astype(_MXU), dst.astype(_MXU), qm, dom, mask)
            back(*pending)

        @pl.when(i >= 2)
        def _():
            done = (i + 1) % 3
            dk_ref[...] = dk_acc[done]
            dv_ref[...] = dv_acc[done]

        pl.when(i == nq + 1)(sc.finish)

    back2 = pl.BlockSpec((TQ, AW), lambda i: (jnp.maximum(i - 2, 0), 0))
    anyspec = pl.BlockSpec(memory_space=pl.ANY)
    return pl.pallas_call(
        body, name="attn_bwd", grid=(nq + 2,),
        in_specs=[pl.BlockSpec((TQ, AW), lambda i: (qi(i), 0)), pl.BlockSpec((TQ, AW), lambda i: (qi(i), 0)),
                  kv_spec(1, 2), kv_spec(1, 1), kv_spec(1, 0), kv_spec(2, 2), kv_spec(2, 1), kv_spec(2, 0),
                  pl.BlockSpec((1, NHA, TW, TQ), lambda i: (jnp.minimum(i, 2), 0, 0, 0)), anyspec],
        out_specs=[pl.BlockSpec((TQ, AW), lambda i: (qi(i), 0)), back2, back2, _full((NHA, TW, TQ)), anyspec],
        out_shape=[jax.ShapeDtypeStruct((T, AW), F32), jax.ShapeDtypeStruct((T, AW), F32),
                   jax.ShapeDtypeStruct((T, AW), F32), jax.ShapeDtypeStruct((NHA, TW, TQ), F32),
                   jax.ShapeDtypeStruct(send.shape, send.dtype)],
        scratch_shapes=[pltpu.VMEM((3, TQ, AW), F32), pltpu.VMEM((3, TQ, AW), F32)] + _COMM_SEMS,
        compiler_params=_cp(("arbitrary",)),
    )(an, dout, an, an, an, an, an, an, tabt, send)


GR = 128
NG = TG // GR
CPT = TG // CH
CONV_K = 4


def _split3(x):
    a = x.astype(BF16)
    r = x - a.astype(F32)
    b = r.astype(BF16)
    c = (r - b.astype(F32)).astype(BF16)
    return a, b, c


def _ones_dot(ones_b, x):
    return sum(jnp.dot(ones_b, t, preferred_element_type=F32) for t in _split3(x))


def _dot_ones_nt(x, ones_b):
    dn = (((1,), (1,)), ((), ()))
    return sum(lax.dot_general(t, ones_b, dn, preferred_element_type=F32) for t in _split3(x))


def _dn_masks():
    r, c = _iota((GR, GR), 0), _iota((GR, GR), 1)
    same = (r >> 6) == (c >> 6)
    one = lambda m: jnp.where(m, 1.0, 0.0).astype(BF16)
    return dict(
        tril=same & (c <= r), strict=same & (c < r), triu=same & (c >= r), strict_u=same & (c > r),
        tril_b=one(same & (c <= r)), triu_b=one(same & (c >= r)), blk_b=one(same), eye_b=one(r == c),
        eye=jnp.where(r == c, 1.0, 0.0).astype(F32),
        fold_b=one((_iota((GR, CH), 0) & (CH - 1)) == _iota((GR, CH), 1)),
        last=(_iota((GR, 1), 0) & (CH - 1)) == CH - 1,
    )


def _shift_down(x, halo, k):
    if k == 0:
        return x
    xs = pltpu.roll(x, k, 0)
    hs = pltpu.roll(halo, k, 0)
    top = jnp.where(_iota(halo.shape, 0) < k, hs, xs[0:8])
    return jnp.concatenate([top, xs[8:]], axis=0)


def _shift_up(x, halo, k):
    if k == 0:
        return x
    n = x.shape[0]
    xs = pltpu.roll(x, n - k, 0)
    hs = pltpu.roll(halo, 8 - k, 0)
    bot = jnp.where(_iota(halo.shape, 0) >= 8 - k, hs, xs[n - 8:n])
    return jnp.concatenate([xs[0:n - 8], bot], axis=0)


def _conv(x, halo, w):
    y = x * w[CONV_K - 1:CONV_K, :]
    for k in range(1, CONV_K):
        y = y + _shift_down(x, halo, k) * w[CONV_K - 1 - k:CONV_K - k, :]
    return y


def _tri_inv(lmats, eye):
    ps = [-m for m in lmats]
    rs = [eye + p for p in ps]
    for _ in range(5):
        ps = [_dot(p, p) for p in ps]
        rs = [r + _dot(r, p) for r, p in zip(rs, ps)]
    return rs


def _gate_terms(ba_g, bat_g, alog8, dtb8, alog8t, dtb8t, K):
    g8 = -jnp.exp(alog8) * _softplus(ba_g + dtb8)
    g8t = -jnp.exp(alog8t) * _softplus(bat_g + dtb8t)
    gc8 = _ones_dot(K["tril_b"], g8)
    gl8 = _ones_dot(K["blk_b"], g8)
    gcrow8 = _dot_ones_nt(g8t, K["tril_b"])
    return g8, gc8, gl8, gcrow8


def _dn_heads(c_tile, rows, beta8, gc8, gl8, gcrow8, K, pre=None):
    return _dn_heads_groups(c_tile, [(rows, beta8, gc8, gl8, gcrow8)], K, None if pre is None else [pre])[0]


def _dn_heads_groups(c_tile, groups, K, pres=None):
    ds = [_dn_head_vec(c_tile, rows, h, beta8, gc8, gl8, gcrow8, K)
          for rows, beta8, gc8, gl8, gcrow8 in groups for h in range(NHD)]
    pls = [_dot_nt(d["kb"], d["kn"]) for d in ds]
    pms = [_dot_nt(d["qn"], d["kn"]) for d in ds]
    for d, pl_, pm in zip(ds, pls, pms):
        d.update(pl=pl_, pm=pm, lmat=jnp.where(K["strict"], pl_ * d["gam_m"], 0.0), mm=pm * d["gam_m"])
    if pres is None:
        for d, tm in zip(ds, _tri_inv([d["lmat"] for d in ds], K["eye"])):
            d.update(tm=tm, u=_dot(tm, d["vb"]), w=_dot(tm, d["kg"]))
    else:
        for d, (tm, u, w) in zip(ds, [p for pre in pres for p in pre]):
            d.update(tm=tm, u=u, w=w)
    return [ds[NHD * k:NHD * (k + 1)] for k in range(len(groups))]


def _dn_head_vec(c_tile, rows, h, beta8, gc8, gl8, gcrow8, K):
    qr = c_tile[rows, DHD * h:DHD * (h + 1)]
    kr = c_tile[rows, DW + DHD * h:DW + DHD * (h + 1)]
    v = c_tile[rows, 2 * DW + DHD * h:2 * DW + DHD * (h + 1)]
    rq = lax.rsqrt(jnp.sum(qr * qr, axis=1, keepdims=True) + EPS)
    rk = lax.rsqrt(jnp.sum(kr * kr, axis=1, keepdims=True) + EPS)
    qh, kn = qr * rq, kr * rk
    qn = qh * (DHD ** -0.5)
    beta = _col(beta8, h)
    gccol, glcol, gcrow = _col(gc8, NHD + h), _col(gl8, NHD + h), _row(gcrow8, NHD + h)
    diff = gccol - gcrow
    gam_m = jnp.exp(jnp.where(K["tril"], diff, NEG))
    gam = jnp.exp(gccol)
    egl = jnp.exp(glcol - gccol)
    kb, vb = kn * beta, v * beta
    kg = kb * gam
    return dict(qr=qr, kr=kr, v=v, rq=rq, rk=rk, qh=qh, qn=qn, kn=kn, beta=beta, diff=diff, gam_m=gam_m, gam=gam,
                egl=egl, el=jnp.exp(glcol), kb=kb, vb=vb, kg=kg, qd=qn * gam, kd=kn * egl)


def _halo_prev(width):
    return pl.BlockSpec((8, width), lambda i: (jnp.maximum(i * (TG // 8) - 1, 0), 0))


def _dn_prep(draw, conv_w, ba, bat, alog8, dtb8, alog8t, dtb8t):
    T = draw.shape[0]
    nb = T // TG
    hm = lambda w, dt: jax.ShapeDtypeStruct((NHD, T, w), dt)
    hm_spec = lambda w: pl.BlockSpec((NHD, TG, w), lambda i: (0, i, 0))
    pc = lambda r, c: jax.ShapeDtypeStruct((NHD, T // CH, r, c), _MXU)
    pc_spec = lambda r, c: pl.BlockSpec((NHD, CPT, r, c), lambda i: (0, i, 0, 0))

    def body(x_ref, halo_ref, cw_ref, ba_ref, bat_ref, al_ref, dt_ref, alt_ref, dtt_ref,
             u_ref, w_ref, kd_ref, tm_ref, wq_ref, km_ref, mq_ref, wt_ref, elb_ref, cv_ref):
        i = pl.program_id(0)
        K = _dn_masks()
        halo = jnp.where(i > 0, halo_ref[...], 0.0)
        cv = _conv(x_ref[...], halo, cw_ref[...])
        cv_ref[...] = cv
        c_tile = cv * _sigmoid(cv)
        eye128 = jnp.where(_iota((DHD, DHD), 0) == _iota((DHD, DHD), 1), 1.0, 0.0).astype(_MXU)
        def gate_inputs(g):
            rows = slice(GR * g, GR * (g + 1))
            ba_g = ba_ref[rows, :]
            _, gc8, gl8, gcrow8 = _gate_terms(ba_g, bat_ref[:, rows], al_ref[...], dt_ref[...], alt_ref[...],
                                              dtt_ref[...], K)
            return rows, _sigmoid(ba_g), gc8, gl8, gcrow8

        def store(g, rows, ds):
            mmts = [_dot_nt(d["kn"], d["qn"]) * jnp.exp(jnp.where(K["triu"], -d["diff"], NEG)) for d in ds]
            mcs = [_dot(d["mm"], K["fold_b"]) for d in ds]
            mcts = [_dot(m, K["fold_b"]) for m in mmts]
            for h, d in enumerate(ds):
                tm_ref[h, rows, :] = d["tm"].astype(_MXU)
                u_ref[h, rows, :] = d["u"]
                w_ref[h, rows, :] = d["w"].astype(_MXU)
                kd_ref[h, rows, :] = d["kd"].astype(_MXU)
                elb = jnp.broadcast_to(d["el"], (GR, DHD))
                for cc in range(GR // CH):
                    ch = slice(CH * cc, CH * (cc + 1))
                    n = (GR // CH) * g + cc
                    wq_ref[h, n, 0:CH, :] = d["w"][ch].astype(_MXU)
                    wq_ref[h, n, CH:2 * CH, :] = d["qd"][ch].astype(_MXU)
                    km_ref[h, n, 0:DHD, :] = _dot_nt(eye128, d["kd"][ch]).astype(_MXU)
                    km_ref[h, n, DHD:DHD + CH, :] = mcs[h][ch].astype(_MXU)
                    mq_ref[h, n, 0:CH, :] = mcts[h][ch].astype(_MXU)
                    mq_ref[h, n, CH:CH + DHD, :] = _dot_nt(eye128, d["qd"][ch]).astype(_MXU)
                    wt_ref[h, n] = _dot_nt(eye128, d["w"][ch]).astype(_MXU)
                    elb_ref[n:n + 1, DHD * h:DHD * (h + 1)] = elb[CH * cc:CH * cc + 1, :]

        PAIR = 2
        for g0 in range(0, NG, PAIR):
            pair = [gate_inputs(g) for g in range(g0, g0 + PAIR)]
            for k, ds in enumerate(_dn_heads_groups(c_tile, pair, K)):
                store(g0 + k, pair[k][0], ds)

    return pl.pallas_call(
        body, name="dn_prep", grid=(nb,),
        in_specs=[_rows(TG, 1536), _halo_prev(1536), _full((CONV_K, 1536)), _rows(TG, 8),
                  pl.BlockSpec((8, TG), lambda i: (0, i)), _full((1, 8)), _full((1, 8)), _full((8, 1)), _full((8, 1))],
        out_specs=[hm_spec(DHD), hm_spec(DHD), hm_spec(DHD), hm_spec(GR),
                   pc_spec(2 * CH, DHD), pc_spec(DHD + CH, CH), pc_spec(CH + DHD, CH), pc_spec(DHD, CH),
                   pl.BlockSpec((CPT, NHD * DHD), lambda i: (i, 0)), _rows(TG, 1536)],
        out_shape=[hm(DHD, F32), hm(DHD, _MXU), hm(DHD, _MXU), hm(GR, _MXU),
                   pc(2 * CH, DHD), pc(DHD + CH, CH), pc(CH + DHD, CH), pc(DHD, CH),
                   jax.ShapeDtypeStruct((T // CH, NHD * DHD), F32), jax.ShapeDtypeStruct((T, 1536), F32)],
        compiler_params=_cp(("arbitrary",)),
    )(draw, draw, conv_w, ba, bat, alog8, dtb8, alog8t, dtb8t)


def _dn_scan(u, wq, km, elb):
    T = u.shape[1]
    nb = T // TG
    hm_spec = lambda wd: pl.BlockSpec((NHD, TG, wd), lambda i: (0, i, 0))

    def body(u_ref, wq_ref, km_ref, elb_ref, o_ref, vn_ref, sn_ref, S):
        @pl.when(pl.program_id(0) == 0)
        def _():
            S[...] = jnp.zeros_like(S)

        sub8 = _iota((CPT, DHD), 0)
        heads = range(NHD)

        def chunk(cc, carry):
            rs = pl.ds(pl.multiple_of(cc * CH, CH), CH)
            sh = [S[h] for h in heads]
            sb = [s.astype(_MXU) for s in sh]
            r1 = [_dot(wq_ref[h, cc], sb[h]) for h in heads]
            vnb = [(u_ref[h, rs, :] - r1[h][0:CH]).astype(_MXU) for h in heads]
            r2 = [_dot(km_ref[h, cc], vnb[h]) for h in heads]
            for h in heads:
                el = jnp.sum(jnp.where(sub8 == cc, elb_ref[:, DHD * h:DHD * (h + 1)], 0.0), axis=0, keepdims=True)
                S[h] = sh[h] * el + r2[h][0:DHD]
                sn_ref[cc, h] = sb[h]
                vn_ref[h, rs, :] = vnb[h]
                o_ref[h, rs, :] = r1[h][CH:2 * CH] + r2[h][DHD:DHD + CH]
            return carry

        lax.fori_loop(0, CPT, chunk, 0)

    return pl.pallas_call(
        body, name="dn_scan", grid=(nb,),
        in_specs=[hm_spec(DHD), pl.BlockSpec((NHD, CPT, 2 * CH, DHD), lambda i: (0, i, 0, 0)),
                  pl.BlockSpec((NHD, CPT, DHD + CH, CH), lambda i: (0, i, 0, 0)),
                  pl.BlockSpec((CPT, NHD * DHD), lambda i: (i, 0))],
        out_specs=[hm_spec(DHD), hm_spec(DHD), pl.BlockSpec((CPT, NHD, DHD, DHD), lambda i: (i, 0, 0, 0))],
        out_shape=[jax.ShapeDtypeStruct((NHD, T, DHD), F32), jax.ShapeDtypeStruct((NHD, T, DHD), _MXU),
                   jax.ShapeDtypeStruct((T // CH, NHD, DHD, DHD), _MXU)],
        scratch_shapes=[pltpu.VMEM((NHD, DHD, DHD), F32)],
        compiler_params=_cp(("arbitrary",)),
    )(u, wq, km, elb)


def _dn_scan_bwd(do, mq, kd, wt, sn, vn, elb):
    T = do.shape[1]
    nb = T // TG
    rev = lambda wd: pl.BlockSpec((NHD, TG, wd), lambda i: (0, nb - 1 - i, 0))
    rev_t = lambda r: pl.BlockSpec((NHD, CPT, r, CH), lambda i: (0, nb - 1 - i, 0, 0))

    def body(do_ref, mq_ref, kd_ref, wt_ref, sn_ref, vn_ref, elb_ref,
             du_ref, dw_ref, dqd_ref, dkd_ref, dgx_ref, dS):
        @pl.when(pl.program_id(0) == 0)
        def _():
            dS[...] = jnp.zeros_like(dS)

        last_row = _iota((CH, DHD), 0) == CH - 1
        sub8 = _iota((CPT, DHD), 0)
        heads = range(NHD)

        def chunk(k, carry):
            cc = CPT - 1 - k
            rs = pl.ds(pl.multiple_of(cc * CH, CH), CH)
            dsh = [dS[h] for h in heads]
            dsb = [d.astype(_MXU) for d in dsh]
            doc = [do_ref[h, rs, :].astype(_MXU) for h in heads]
            a = [_dot(mq_ref[h, cc], doc[h]) for h in heads]
            b = [_dot(kd_ref[h, rs, :], dsb[h]) for h in heads]
            dvn = [a[h][0:CH] + b[h] for h in heads]
            dvnb = [d.astype(_MXU) for d in dvn]
            e = [_dot(wt_ref[h, cc], dvnb[h]) for h in heads]
            for h in heads:
                el = jnp.sum(jnp.where(sub8 == cc, elb_ref[:, DHD * h:DHD * (h + 1)], 0.0), axis=0, keepdims=True)
                sn = sn_ref[cc, h]
                dS[h] = a[h][CH:CH + DHD] + dsh[h] * el - e[h]
                du_ref[h, rs, :] = dvn[h]
                c = _dot_nt(jnp.concatenate([doc[h], dvnb[h]], axis=0), sn)
                dqd_ref[h, rs, :] = c[0:CH]
                dw_ref[h, rs, :] = -c[CH:2 * CH]
                dkd_ref[h, rs, :] = _dot_nt(vn_ref[h, rs, :], dsb[h])
                part = jnp.sum(dsh[h] * sn.astype(F32), axis=0, keepdims=True) * el
                dgx_ref[h, rs, :] = jnp.where(last_row, part, 0.0)
            return carry

        lax.fori_loop(0, CPT, chunk, 0)

    o = jax.ShapeDtypeStruct((NHD, T, DHD), F32)
    return pl.pallas_call(
        body, name="dn_scan_bwd", grid=(nb,),
        in_specs=[rev(DHD), rev_t(CH + DHD), rev(DHD), rev_t(DHD),
                  pl.BlockSpec((CPT, NHD, DHD, DHD), lambda i: (nb - 1 - i, 0, 0, 0)), rev(DHD),
                  pl.BlockSpec((CPT, NHD * DHD), lambda i: (nb - 1 - i, 0))],
        out_specs=[rev(DHD)] * 5,
        out_shape=[o] * 5,
        scratch_shapes=[pltpu.VMEM((NHD, DHD, DHD), F32)],
        compiler_params=_cp(("arbitrary",)),
    )(do, mq, kd, wt, sn, vn, elb)


def _put_col(acc, k, col):
    return jnp.where(_iota(acc.shape, 1) == k, col, acc)


def _dn_post_bwd(draw, cv, conv_w, ba, bat, alog8, dtb8, alog8t, dtb8t, du, dw, dqd, dkd, dgx, do, vn, tm, u, w):
    T = draw.shape[0]
    nb = T // TG
    hm_spec = lambda wd: pl.BlockSpec((NHD, TG, wd), lambda i: (0, nb - 1 - i, 0))
    rrows = lambda w: pl.BlockSpec((TG, w), lambda i: (nb - 1 - i, 0))

    def body(x_ref, cv_ref, cw_ref, ba_ref, bat_ref, al_ref, dt_ref, alt_ref, dtt_ref,
             du_ref, dw_ref, dqd_ref, dkd_ref, dgx_ref, do_ref, vn_ref, tm_ref, u_ref, w_ref,
             dx_ref, dba_ref, sm_ref, dcw_ref, dc_ref, nxt_ref):
        i = pl.program_id(0)

        @pl.when(i == 0)
        def _():
            sm_ref[...] = jnp.zeros_like(sm_ref)
            dcw_ref[...] = jnp.zeros_like(dcw_ref)
            nxt_ref[...] = jnp.zeros_like(nxt_ref)

        K = _dn_masks()
        cv = cv_ref[...]
        sg = _sigmoid(cv)
        c_tile = cv * sg
        dsilu = sg * (1.0 + cv * (1.0 - sg))
        for g in range(NG):
            rows = slice(GR * g, GR * (g + 1))
            ba_g = ba_ref[rows, :]
            g8, gc8, gl8, gcrow8 = _gate_terms(ba_g, bat_ref[:, rows], al_ref[...], dt_ref[...], alt_ref[...],
                                               dtt_ref[...], K)
            beta8 = _sigmoid(ba_g)
            dgc8 = jnp.zeros((GR, 8), F32)
            rd8 = jnp.zeros((GR, 8), F32)
            dbeta8 = jnp.zeros((GR, 8), F32)
            pre = [(tm_ref[h, rows, :], u_ref[h, rows, :], w_ref[h, rows, :]) for h in range(NHD)]
            ds = _dn_heads(c_tile, rows, beta8, gc8, gl8, gcrow8, K, pre)
            H = range(NHD)
            eye_b = K["eye_b"].astype(_MXU)
            gam_t = [jnp.exp(jnp.where(K["triu"], -d["diff"], NEG)) for d in ds]
            doh = [do_ref[h, rows, :] for h in H]
            vnh = [vn_ref[h, rows, :] for h in H]
            tt = [_dot_nt(eye_b, d["tm"]) for d in ds]
            dvb = [_dot(tt[h], du_ref[h, rows, :]) for h in H]
            dkg = [_dot(tt[h], dw_ref[h, rows, :]) for h in H]
            plt = [_dot_nt(d["kn"], d["kb"]) for d in ds]
            pmt = [_dot_nt(d["kn"], d["qn"]) for d in ds]
            da = [-(_dot_nt(dvb[h], ds[h]["u"]) + _dot_nt(dkg[h], ds[h]["w"])) for h in H]
            dat = [-(_dot_nt(ds[h]["u"], dvb[h]) + _dot_nt(ds[h]["w"], dkg[h])) for h in H]
            dpm = [jnp.where(K["tril"], _dot_nt(doh[h], vnh[h]), 0.0) * ds[h]["gam_m"] for h in H]
            dpmt = [jnp.where(K["triu"], _dot_nt(vnh[h], doh[h]), 0.0) * gam_t[h] for h in H]
            dpl = [jnp.where(K["strict"], da[h], 0.0) * ds[h]["gam_m"] for h in H]
            dplt = [jnp.where(K["strict_u"], dat[h], 0.0) * gam_t[h] for h in H]
            dkb = [_dot(dpl[h], ds[h]["kn"]) + dkg[h] * ds[h]["gam"] for h in H]
            dqn = [_dot(dpm[h], ds[h]["kn"]) + dqd_ref[h, rows, :] * ds[h]["gam"] for h in H]
            dknm = [_dot(dplt[h], ds[h]["kb"]) + _dot(dpmt[h], ds[h]["qn"]) for h in H]
            for h, d in enumerate(ds):
                kn, dqdh, dkdh = d["kn"], dqd_ref[h, rows, :], dkd_ref[h, rows, :]
                dkn = dknm[h] + dkdh * d["egl"] + dkb[h] * d["beta"]
                dkd_kd = dkdh * d["kd"]
                rd = jnp.sum(dkd_kd, axis=1, keepdims=True)
                dgc = jnp.sum(dpl[h] * d["pl"] + dpm[h] * d["pm"] - dplt[h] * plt[h] - dpmt[h] * pmt[h]
                              + dqdh * d["qd"] + dkg[h] * d["kg"] - dkd_kd + dgx_ref[h, rows, :],
                              axis=1, keepdims=True)
                dgc8 = _put_col(dgc8, NHD + h, dgc)
                rd8 = _put_col(rd8, NHD + h, rd)
                dbeta = jnp.sum(dkb[h] * kn + dvb[h] * d["v"], axis=1, keepdims=True)
                dbeta8 = _put_col(dbeta8, h, dbeta)
                dqh = dqn[h] * (DHD ** -0.5)
                qh = d["qh"]
                dqr = d["rq"] * (dqh - qh * jnp.sum(dqh * qh, axis=1, keepdims=True))
                dkr = d["rk"] * (dkn - kn * jnp.sum(dkn * kn, axis=1, keepdims=True))
                cq = slice(DHD * h, DHD * (h + 1))
                ck = slice(DW + DHD * h, DW + DHD * (h + 1))
                cvv = slice(2 * DW + DHD * h, 2 * DW + DHD * (h + 1))
                dc_ref[rows, cq] = dqr * dsilu[rows, cq]
                dc_ref[rows, ck] = dkr * dsilu[rows, ck]
                dc_ref[rows, cvv] = dvb[h] * d["beta"] * dsilu[rows, cvv]
            dgc8 = dgc8 + jnp.where(K["last"], _ones_dot(K["blk_b"], rd8), 0.0)
            dg8 = _ones_dot(K["triu_b"], dgc8)
            sgm = _sigmoid(ba_g + dt_ref[...])
            dalpha = dg8 * (-jnp.exp(al_ref[...])) * sgm
            lane8 = _iota((GR, 8), 1)
            dba_ref[rows, :] = jnp.where(lane8 < NHD, dbeta8 * beta8 * (1.0 - beta8), dalpha)
            valid = lane8 >= NHD
            sm_ref[0:1, 0:8] += jnp.sum(jnp.where(valid, dg8 * g8, 0.0), axis=0, keepdims=True)
            sm_ref[1:2, 0:8] += jnp.sum(jnp.where(valid, dalpha, 0.0), axis=0, keepdims=True)

        dcv = dc_ref[...]
        xv = x_ref[...]
        nxt = nxt_ref[...]
        w = cw_ref[...]
        dx = dcv * w[CONV_K - 1:CONV_K, :]
        dcw_ref[CONV_K - 1:CONV_K, :] += jnp.sum(dcv * xv, axis=0, keepdims=True)
        for k in range(1, CONV_K):
            j = CONV_K - 1 - k
            up = _shift_up(dcv, nxt, k)
            dx = dx + up * w[j:j + 1, :]
            dcw_ref[j:j + 1, :] += jnp.sum(up * xv, axis=0, keepdims=True)
        dx_ref[...] = dx
        nxt_ref[...] = dcv[0:8]

    return pl.pallas_call(
        body, name="dn_post_bwd", grid=(nb,),
        in_specs=[rrows(1536), rrows(1536), _full((CONV_K, 1536)), rrows(8),
                  pl.BlockSpec((8, TG), lambda i: (0, nb - 1 - i)), _full((1, 8)), _full((1, 8)), _full((8, 1)),
                  _full((8, 1)),
                  hm_spec(DHD), hm_spec(DHD), hm_spec(DHD), hm_spec(DHD), hm_spec(DHD), hm_spec(DHD), hm_spec(DHD),
                  hm_spec(GR), hm_spec(DHD), hm_spec(DHD)],
        out_specs=[rrows(1536), rrows(8), _full((8, LANES)), _full((8, 1536))],
        out_shape=[jax.ShapeDtypeStruct((T, 1536), F32), jax.ShapeDtypeStruct((T, 8), F32),
                   jax.ShapeDtypeStruct((8, LANES), F32), jax.ShapeDtypeStruct((8, 1536), F32)],
        scratch_shapes=[pltpu.VMEM((TG, 1536), F32), pltpu.VMEM((8, 1536), F32)],
        compiler_params=_cp(("arbitrary",)),
    )(draw, cv, conv_w, ba, bat, alog8, dtb8, alog8t, dtb8t, du, dw, dqd, dkd, dgx, do, vn, tm, u, w)


def _rms(x):
    return lax.rsqrt(jnp.mean(x * x, axis=1, keepdims=True) + EPS)


def _rms_bwd(dy, xh, r, g):
    dxh = dy * g
    return r * (dxh - xh * jnp.mean(dxh * xh, axis=1, keepdims=True))


def _hm_rows(tm):
    return pl.BlockSpec((NHD, tm, DHD), lambda i: (0, i, 0))


def _post_mix(apre, o, z, x, w_out, g_a, g_dn):
    T = x.shape[0]

    def body(ap_ref, o_ref, z_ref, x_ref, w_ref, ga_ref, gd_ref, x1_ref, mix_ref):
        ap = ap_ref[...]
        parts = [ap * _rms(ap) * ga_ref[...]]
        zz = z_ref[...]
        for h in range(NHD):
            oh = o_ref[h]
            zh = zz[:, DHD * h:DHD * (h + 1)]
            parts.append(oh * _rms(oh) * gd_ref[...] * (zh * _sigmoid(zh)))
        mix = jnp.concatenate(parts, axis=1).astype(_MXU)
        mix_ref[...] = mix
        x1_ref[...] = x_ref[...] + jnp.dot(mix, w_ref[...], preferred_element_type=F32)

    return pl.pallas_call(
        body, name="post_mix", grid=(T // TM,),
        in_specs=[_rows(TM, AW), _hm_rows(TM), _rows(TM, DW), _rows(TM, D), _full((D, D)), _full((1, AW)),
                  _full((1, DHD))],
        out_specs=[_rows(TM, D), _rows(TM, D)],
        out_shape=[jax.ShapeDtypeStruct((T, D), F32), jax.ShapeDtypeStruct((T, D), _MXU)],
        compiler_params=_cp(("arbitrary",)),
    )(apre, o, z, x, w_out, g_a, g_dn)


def _ffn(x1, tgt, wl_all, g_ffn):
    T = x1.shape[0]
    SH = FF // N_DEV
    nt = (((1,), (1,)), ((), ()))

    def body(x_ref, t_ref, wl_hbm, g_ref,
             dx1_ref, dx1b_ref, h2_ref, act_ref, dgu_ref, dyb_ref, loss_ref, dg_ref, wg, wu, wd, sem):
        @pl.when(pl.program_id(0) == 0)
        def _():
            cps = [pltpu.make_async_copy(wl_hbm.at[dev, pl.ds(128 + SH * k, SH), :], dst.at[pl.ds(SH * dev, SH), :],
                                         sem.at[N_DEV * k + dev])
                   for k, dst in enumerate((wg, wu, wd)) for dev in range(N_DEV)]
            for cp in cps:
                cp.start()
            for cp in cps:
                cp.wait()
            loss_ref[...] = jnp.zeros_like(loss_ref)
            dg_ref[...] = jnp.zeros_like(dg_ref)

        xv = x_ref[...]
        r = _rms(xv)
        xh = xv * r
        gg = g_ref[...]
        h2 = (xh * gg).astype(_MXU)
        h2_ref[...] = h2
        gate = lax.dot_general(h2, wg[...], nt, preferred_element_type=F32)
        up = lax.dot_general(h2, wu[...], nt, preferred_element_type=F32)
        sg = _sigmoid(gate)
        silu = gate * sg
        act = (silu * up).astype(_MXU)
        act_ref[...] = act
        y = xv + jnp.dot(act, wd[...], preferred_element_type=F32)
        err = y - t_ref[...]
        loss_ref[...] += jnp.sum(err * err, axis=0, keepdims=True)
        dy = err * (1.0 / D)
        dyb = dy.astype(_MXU)
        dyb_ref[...] = dyb
        dact = lax.dot_general(dyb, wd[...], nt, preferred_element_type=F32)
        dgate = (dact * up * (sg * (1.0 + gate * (1.0 - sg)))).astype(_MXU)
        dup = (dact * silu).astype(_MXU)
        dgu_ref[:, 0:FF] = dgate
        dgu_ref[:, FF:2 * FF] = dup
        dh2 = (jnp.dot(dgate, wg[...], preferred_element_type=F32)
               + jnp.dot(dup, wu[...], preferred_element_type=F32))
        dg_ref[...] += jnp.sum(dh2 * xh, axis=0, keepdims=True)
        dx1 = dy + _rms_bwd(dh2, xh, r, gg)
        dx1_ref[...] = dx1
        dx1b_ref[...] = dx1.astype(_MXU)

    anyspec = pl.BlockSpec(memory_space=pl.ANY)
    sd = lambda w, dt: jax.ShapeDtypeStruct((T, w), dt)
    return pl.pallas_call(
        body, name="ffn", grid=(T // TF,),
        in_specs=[_rows(TF, D), _rows(TF, D), anyspec, _full((1, D))],
        out_specs=[_rows(TF, D), _rows(TF, D), _rows(TF, D), _rows(TF, FF), _rows(TF, 2 * FF), _rows(TF, D),
                   _full((1, D)), _full((1, D))],
        out_shape=[sd(D, F32), sd(D, _MXU), sd(D, _MXU), sd(FF, _MXU), sd(2 * FF, _MXU), sd(D, _MXU),
                   jax.ShapeDtypeStruct((1, D), F32), jax.ShapeDtypeStruct((1, D), F32)],
        scratch_shapes=[pltpu.VMEM((FF, D), _MXU)] * 3 + [pltpu.SemaphoreType.DMA((3 * N_DEV,))],
        compiler_params=_cp(("arbitrary",)),
    )(x1, tgt, wl_all, g_ffn)


def _mix_bwd(dx1b, w_out, apre, o, z, g_a, g_dn):
    T = dx1b.shape[0]

    def body(dx_ref, w_ref, ap_ref, o_ref, z_ref, ga_ref, gd_ref, dap_ref, do_ref, dz_ref, dga_ref, dgd_ref):
        @pl.when(pl.program_id(0) == 0)
        def _():
            dga_ref[...] = jnp.zeros_like(dga_ref)
            dgd_ref[...] = jnp.zeros_like(dgd_ref)

        dmix = lax.dot_general(dx_ref[...], w_ref[...], (((1,), (1,)), ((), ())), preferred_element_type=F32)
        ap = ap_ref[...]
        ra = _rms(ap)
        ah = ap * ra
        da = dmix[:, 0:AW]
        dga_ref[...] += jnp.sum(da * ah, axis=0, keepdims=True)
        dap_ref[...] = _rms_bwd(da, ah, ra, ga_ref[...])
        zz = z_ref[...]
        gd = gd_ref[...]
        for h in range(NHD):
            cs = slice(DHD * h, DHD * (h + 1))
            dd = dmix[:, AW + DHD * h:AW + DHD * (h + 1)]
            oh = o_ref[h]
            ro = _rms(oh)
            ohh = oh * ro
            zh = zz[:, cs]
            sz = _sigmoid(zh)
            dz_ref[:, cs] = dd * (ohh * gd) * (sz * (1.0 + zh * (1.0 - sz)))
            don = dd * (zh * sz)
            dgd_ref[...] += jnp.sum(don * ohh, axis=0, keepdims=True)
            do_ref[h] = _rms_bwd(don, ohh, ro, gd)

    return pl.pallas_call(
        body, name="mix_bwd", grid=(T // TM,),
        in_specs=[_rows(TM, D), _full((D, D)), _rows(TM, AW), _hm_rows(TM), _rows(TM, DW), _full((1, AW)),
                  _full((1, DHD))],
        out_specs=[_rows(TM, AW), _hm_rows(TM), _rows(TM, DW), _full((1, AW)), _full((1, DHD))],
        out_shape=[jax.ShapeDtypeStruct((T, AW), F32), jax.ShapeDtypeStruct((NHD, T, DHD), F32),
                   jax.ShapeDtypeStruct((T, DW), F32), jax.ShapeDtypeStruct((1, AW), F32),
                   jax.ShapeDtypeStruct((1, DHD), F32)],
        compiler_params=_cp(("arbitrary",)),
    )(dx1b, w_out, apre, o, z, g_a, g_dn)


def _inproj_bwd(dqn, dkn, dv, araw, ddraw, dz, dba, x, dx1, w_in, g_mix, qg_t, kg_t):
    T = x.shape[0]

    def body(dqn_ref, dkn_ref, dv_ref, ar_ref, dd_ref, dz_ref, dba_ref, x_ref, dx1_ref, w_hbm, g_ref, qg_ref, kg_ref,
             dx_ref, dp_ref, dgm_ref, dqg_ref, dkg_ref, w_ref, w_sem):
        @pl.when(pl.program_id(0) == 0)
        def _():
            cp = pltpu.make_async_copy(w_hbm, w_ref, w_sem)
            cp.start()
            cp.wait()
            dgm_ref[...] = jnp.zeros_like(dgm_ref)
            dqg_ref[...] = jnp.zeros_like(dqg_ref)
            dkg_ref[...] = jnp.zeros_like(dkg_ref)

        bd = _block_ones(AW // 2, DHA)

        def head_norm_bwd(raw, dyn, gain, dg_ref):
            r = _head_rms(raw, bd, DHA)
            xh = raw * r
            dg_ref[...] += jnp.sum(dyn * xh, axis=0, keepdims=True)
            dxh = dyn * gain
            return r * (dxh - xh * (_head_sum(dxh * xh, bd) * (1.0 / DHA)))

        nt = (((1,), (1,)), ((), ()))

        def segment(lo, val):
            vb = val.astype(_MXU)
            dp_ref[:, lo:lo + val.shape[1]] = vb
            return lax.dot_general(vb, w_ref[:, lo:lo + val.shape[1]], nt, preferred_element_type=F32)

        dh = segment(1536, dd_ref[...]) + segment(2 * AW, dv_ref[...]) + segment(3072, dz_ref[...])
        dh = dh + segment(3584, dba_ref[...])
        ar = ar_ref[...]
        dq = head_norm_bwd(ar[:, 0:AW], dqn_ref[...] * (DHA ** -0.5), qg_ref[...], dqg_ref)
        dk = head_norm_bwd(ar[:, AW:2 * AW], dkn_ref[...], kg_ref[...], dkg_ref)
        dh = dh + segment(0, dq) + segment(AW, dk)
        xv = x_ref[...]
        r = _rms(xv)
        xh = xv * r
        dgm_ref[...] += jnp.sum(dh * xh, axis=0, keepdims=True)
        dx_ref[...] = dx1_ref[...] + _rms_bwd(dh, xh, r, g_ref[...])

    return pl.pallas_call(
        body, name="inproj_bwd", grid=(T // TM,),
        in_specs=[_rows(TM, AW), _rows(TM, AW), _rows(TM, AW), _rows(TM, 1536), _rows(TM, 1536), _rows(TM, DW),
                  _rows(TM, 8), _rows(TM, D), _rows(TM, D), pl.BlockSpec(memory_space=pl.ANY), _full((1, D)),
                  _full((1, AW)), _full((1, AW))],
        out_specs=[_rows(TM, D), _rows(TM, 3592), _full((1, D)), _full((1, AW)), _full((1, AW))],
        out_shape=[jax.ShapeDtypeStruct((T, D), F32), jax.ShapeDtypeStruct((T, 3592), _MXU),
                   jax.ShapeDtypeStruct((1, D), F32), jax.ShapeDtypeStruct((1, AW), F32),
                   jax.ShapeDtypeStruct((1, AW), F32)],
        scratch_shapes=[pltpu.VMEM((D, 3592), _MXU), pltpu.SemaphoreType.DMA],
        compiler_params=_cp(("arbitrary",)),
    )(dqn, dkn, dv, araw, ddraw, dz, dba, x, dx1, w_in, g_mix, qg_t, kg_t)


def _wgrad(a, b, name, tk=512, tn=None, out_dtype=F32, transposed=False):
    T, M = a.shape
    N = b.shape[1]
    tn = N if tn is None else tn
    nk = T // tk

    def body(a_ref, b_ref, o_ref, acc):
        k = pl.program_id(1)

        @pl.when(k == 0)
        def _():
            acc[...] = jnp.zeros_like(acc)

        acc[...] += lax.dot_general(a_ref[...], b_ref[...], (((0,), (0,)), ((), ())), preferred_element_type=F32)

        @pl.when(k == nk - 1)
        def _():
            r = acc[...]
            o_ref[...] = (r.T if transposed else r).astype(out_dtype)

    if transposed:
        out_spec, out_shape = pl.BlockSpec((tn, M), lambda j, k: (j, 0)), (N, M)
    else:
        out_spec, out_shape = pl.BlockSpec((M, tn), lambda j, k: (0, j)), (M, N)
    return pl.pallas_call(
        body, name=name, grid=(N // tn, nk),
        in_specs=[pl.BlockSpec((tk, M), lambda j, k: (k, 0)), pl.BlockSpec((tk, tn), lambda j, k: (k, j))],
        out_specs=out_spec,
        out_shape=jax.ShapeDtypeStruct(out_shape, out_dtype),
        scratch_shapes=[pltpu.VMEM((M, tn), F32)],
        compiler_params=_cp(("arbitrary", "arbitrary")),
    )(a, b)


def _adamw(parts, w, m, v, name, tr, send=None):
    K, R, W = parts.shape
    n = R // tr

    def body(p_ref, w_ref, m_ref, v_ref, *rest):
        if send is not None:
            send_ref, g_ref, d_ref, nm_ref, nv_ref, recv_ref, send_sems, recv_sems, local_sem = rest
            sc = _Scatter(send_ref, recv_ref, send_sems, recv_sems, local_sem)
            pl.when(pl.program_id(0) == 0)(sc.start)
        else:
            g_ref, d_ref, nm_ref, nv_ref = rest
        g = p_ref[0].astype(F32)
        for k in range(1, K):
            g = g + p_ref[k].astype(F32)
        g_ref[...] = g
        nm = ADAM_B1 * m_ref[...] + (1.0 - ADAM_B1) * g
        nv = ADAM_B2 * v_ref[...] + (1.0 - ADAM_B2) * (g * g)
        nm_ref[...] = nm
        nv_ref[...] = nv
        m_hat = nm / (1.0 - ADAM_B1 ** ADAM_STEP)
        v_hat = nv / (1.0 - ADAM_B2 ** ADAM_STEP)
        d_ref[...] = -ADAM_LR * (m_hat / (jnp.sqrt(v_hat) + ADAM_EPS) + ADAM_WD * w_ref[...])
        if send is not None:
            pl.when(pl.program_id(0) == n - 1)(sc.finish)

    o = jax.ShapeDtypeStruct((R, W), F32)
    anyspec = pl.BlockSpec(memory_space=pl.ANY)
    hosted = send is not None
    return pl.pallas_call(
        body, name=name, grid=(n,),
        in_specs=[pl.BlockSpec((K, tr, W), lambda i: (0, i, 0)), _rows(tr, W), _rows(tr, W), _rows(tr, W)]
        + ([anyspec] if hosted else []),
        out_specs=[_rows(tr, W)] * 4 + ([anyspec] if hosted else []),
        out_shape=[o] * 4 + ([jax.ShapeDtypeStruct(send.shape, send.dtype)] if hosted else []),
        scratch_shapes=_COMM_SEMS if hosted else [],
        compiler_params=_cp(("arbitrary",)),
    )(*((parts, w, m, v) + ((send,) if hosted else ())))


SM_ROWS = 136
R_GMIX, R_GFFN, R_QG, R_KG, R_GA, R_GDN, R_ALOG, R_DT, R_LOSS, R_CONV, R_REL = 0, 8, 16, 24, 32, 40, 48, 49, 56, 64, 112


def _small_reduce(gathered):
    def body(p_ref, o_ref):
        s = p_ref[0]
        for k in range(1, N_DEV):
            s = s + p_ref[k]
        o_ref[...] = s
        for r0 in (R_QG, R_KG):
            rs = jnp.sum(s[r0:r0 + 4], axis=0, keepdims=True)
            o_ref[r0:r0 + 1, :] = rs + pltpu.roll(rs, DHA, 1)
        tot = jnp.sum(jnp.sum(s[R_LOSS:R_LOSS + 8], axis=0, keepdims=True), axis=1, keepdims=True)
        o_ref[R_LOSS:R_LOSS + 1, :] = jnp.broadcast_to(tot * (0.5 / D), (1, LANES))

    return pl.pallas_call(
        body, name="small_reduce",
        out_shape=jax.ShapeDtypeStruct((SM_ROWS, LANES), F32),
    )(gathered)


_WIRE = jnp.bfloat16
RA_USED, RA = 449, 464
RL = 128 + 3 * 352


def _pack_rows(parts, rows=None):
    p = jnp.concatenate([t.reshape(-1, D) for t in parts], axis=0) if len(parts) > 1 else parts[0].reshape(-1, D)
    return p if rows is None else jnp.pad(p, ((0, rows - p.shape[0]), (0, 0)))


def _unpack_rows(packed, shapes):
    out, r = [], 0
    for shp in shapes:
        nr = math.prod(shp) // D
        out.append(packed[r:r + nr].reshape(shp))
        r += nr
    return out


def _pad8(t):
    return jnp.pad(t, ((0, (-t.shape[0]) % 8), (0, 0)))


def _pack_lanes(parts):
    rows = []
    for p in parts:
        f = p.reshape(-1)
        pad = (-f.shape[0]) % LANES
        rows.append(jnp.pad(f, (0, pad)).reshape(-1, LANES))
    return jnp.concatenate(rows, axis=0)


def _unpack_lanes(packed, shapes):
    out, r = [], 0
    for shp in shapes:
        n = math.prod(shp)
        nr = -(-n // LANES)
        out.append(packed[r:r + nr].reshape(-1)[:n].reshape(shp))
        r += nr
    return out


def kernel(x, norm_mix_g, w_in, attn_q_norm_g, attn_k_norm_g, rel_bias, attn_out_norm_g, conv_w, a_log, dt_bias, dn_out_norm_g, w_out, norm_ffn_g, w_gate, w_up, w_down, loss_target, m_norm_mix_g, m_w_in, m_attn_q_norm_g, m_attn_k_norm_g, m_rel_bias, m_attn_out_norm_g, m_conv_w, m_a_log, m_dt_bias, m_dn_out_norm_g, m_w_out, m_norm_ffn_g, m_w_gate, m_w_up, m_w_down, v_norm_mix_g, v_w_in, v_attn_q_norm_g, v_attn_k_norm_g, v_rel_bias, v_attn_out_norm_g, v_conv_w, v_a_log, v_dt_bias, v_dn_out_norm_g, v_w_out, v_norm_ffn_g, v_w_gate, v_w_up, v_w_down):
    xs, tgt = x[0], loss_target[0]
    T = xs.shape[0]
    my_idx = 4 * lax.axis_index("x") + 2 * lax.axis_index("y") + lax.axis_index("c")
    late_w = (w_out[0], w_gate[0], w_up[0], w_down[0])
    late_shapes = [w.shape for w in late_w]

    wa_all = _all_gather(_pack_rows([w_in[0].astype(_MXU)], RA), "gather_w_in")
    cw_all = _all_gather(jnp.pad(conv_w[0], ((0, 4), (0, 64))), "gather_conv")
    by_dev = lambda a, k: a.reshape(N_DEV, D, k).transpose(1, 0, 2).reshape(D, N_DEV * k)
    W_in = by_dev(wa_all[:, 0:RA_USED], RA_USED)
    conv_full = cw_all[:, 0:CONV_K, 0:192].transpose(1, 0, 2).reshape(CONV_K, 1536)

    qg_t = jnp.tile(attn_q_norm_g, (1, NHA))
    kg_t = jnp.tile(attn_k_norm_g, (1, NHA))
    z4 = jnp.zeros((1, NHD), F32)
    alog8 = jnp.concatenate([z4, a_log], axis=1)
    dtb8 = jnp.concatenate([z4, dt_bias], axis=1)

    late_t = lambda ts: (ts[0], ts[1].T, ts[2].T, ts[3])
    araw, an, draw, z, ba, hb, wl_all = _inproj(xs, norm_mix_g, W_in, qg_t, kg_t,
                                                _pack_rows([w.astype(_MXU) for w in late_t(late_w)]))
    W_out = wl_all[:, 0:128].reshape(D, D)
    tab, tabt = _bias_tables(jnp.pad(rel_bias[0].T, ((0, 0), (0, VAR0 - 257))))
    apre = _attn_fwd(an, tab)
    bat = ba.T
    dn_args = (draw, conv_full, ba, bat, alog8, dtb8, alog8.T, dtb8.T)
    u, w, kd, tm, wq, km, mq, wt, elb, cv = _dn_prep(*dn_args)
    o, vn, sn = _dn_scan(u, wq, km, elb)
    x1, mix = _post_mix(apre, o, z, xs, W_out, attn_out_norm_g, dn_out_norm_g)

    dx1, dx1b, h2, act, dgu, dyb, loss_row, dgffn = _ffn(x1, tgt, wl_all, norm_ffn_g)

    by_cols = lambda g, k: g.reshape(D, N_DEV, k).transpose(1, 0, 2).reshape(N_DEV, -1, D)
    gW_out = _wgrad(mix, dx1b, "wgrad_out", out_dtype=_WIRE)
    gW_gu_t = _wgrad(h2, dgu, "wgrad_gate_up", tn=FF, out_dtype=_WIRE, transposed=True)
    gW_down = _wgrad(dyb, act, "wgrad_down", out_dtype=_WIRE, transposed=True)
    send_late = jnp.concatenate(
        [gW_out.reshape(N_DEV, 128, D), gW_gu_t[0:FF].reshape(N_DEV, 352, D), gW_gu_t[FF:].reshape(N_DEV, 352, D),
         gW_down.reshape(N_DEV, 352, D)], axis=1)

    dap, do, dz, dga, dgdn = _mix_bwd(dx1b, W_out, apre, o, z, attn_out_norm_g, dn_out_norm_g)
    dqn, dkn, dv, dtabt, recv_late = _attn_bwd(an, dap, tabt, send_late)
    drel = _bias_grad(dtabt)
    du, dw, dqd, dkd, dgx = _dn_scan_bwd(do, mq, kd, wt, sn, vn, elb)
    ddraw, dba, sm, dcw = _dn_post_bwd(draw, cv, *dn_args[1:], du, dw, dqd, dkd, dgx, do, vn, tm, u, w)
    gx, dproj, dgmix, dqg, dkg = _inproj_bwd(dqn, dkn, dv, araw, ddraw, dz, dba, xs, dx1, W_in, norm_mix_g, qg_t, kg_t)

    gW_in = _wgrad(hb, dproj, "wgrad_in", tk=256, out_dtype=_WIRE)
    send_in = jnp.pad(by_cols(gW_in, RA_USED), ((0, 0), (0, RA - RA_USED), (0, 0)))
    late_m = (m_w_out[0], m_w_gate[0], m_w_up[0], m_w_down[0])
    late_v = (v_w_out[0], v_w_gate[0], v_w_up[0], v_w_down[0])
    *outs_late, recv_in = _adamw(recv_late, _pack_rows(late_t(late_w)), _pack_rows(late_t(late_m)),
                                 _pack_rows(late_t(late_v)), "adamw_late", 32, send=send_in)
    outs_in = _adamw(recv_in, _pack_rows([w_in[0]], RA), _pack_rows([m_w_in[0]], RA), _pack_rows([v_w_in[0]], RA),
                     "adamw_w_in", 16)
    late_t_shapes = [t.shape for t in late_t(late_w)]
    big = [[_unpack_rows(a, [w_in[0].shape])[0]] + list(late_t(_unpack_rows(b, late_t_shapes)))
           for a, b in zip(outs_in, outs_late)]
    bg, bd_, bm, bv = big

    partial = jnp.concatenate(
        [dgmix.reshape(8, LANES), dgffn.reshape(8, LANES), _pad8(dqg.reshape(4, LANES)), _pad8(dkg.reshape(4, LANES)),
         _pad8(dga.reshape(4, LANES)), _pad8(dgdn), sm, loss_row.reshape(8, LANES),
         dcw[0:CONV_K].reshape(48, LANES), drel.reshape(24, LANES)], axis=0)
    S = _small_reduce(_all_gather(partial, "gather_small"))
    loss = S[R_LOSS, 0]
    g_conv = lax.dynamic_slice(S[R_CONV:R_CONV + 48].reshape(CONV_K, 1536), (0, 192 * my_idx), (CONV_K, 192))
    sg = [S[R_GMIX:R_GMIX + 8].reshape(1, D), S[R_QG:R_QG + 1, 0:DHA], S[R_KG:R_KG + 1, 0:DHA],
          S[R_REL:R_REL + 24].reshape(NHA, 384)[:, 0:257].T, S[R_GA:R_GA + 4].reshape(1, AW), g_conv,
          S[R_ALOG:R_ALOG + 1, NHD:2 * NHD], S[R_DT:R_DT + 1, NHD:2 * NHD], S[R_GDN:R_GDN + 1], S[R_GFFN:R_GFFN + 8].reshape(1, D)]
    sw = [norm_mix_g, attn_q_norm_g, attn_k_norm_g, rel_bias[0], attn_out_norm_g, conv_w[0], a_log, dt_bias, dn_out_norm_g, norm_ffn_g]
    smm = [m_norm_mix_g, m_attn_q_norm_g, m_attn_k_norm_g, m_rel_bias[0], m_attn_out_norm_g, m_conv_w[0], m_a_log, m_dt_bias, m_dn_out_norm_g, m_norm_ffn_g]
    svv = [v_norm_mix_g, v_attn_q_norm_g, v_attn_k_norm_g, v_rel_bias[0], v_attn_out_norm_g, v_conv_w[0], v_a_log, v_dt_bias, v_dn_out_norm_g, v_norm_ffn_g]
    s_shapes = [t.shape for t in sw]
    pk = lambda ts: _pack_lanes(ts)
    pg = pk(sg)
    padr = (-pg.shape[0]) % 8
    padz = lambda t: jnp.pad(t, ((0, padr), (0, 0)))
    s_out = _adamw(padz(pg)[None], padz(pk(sw)), padz(pk(smm)), padz(pk(svv)), "adamw_small", pg.shape[0] + padr)
    s_g, s_d, s_m, s_v = (_unpack_lanes(t, s_shapes) for t in s_out)

    lead = lambda t: t[None]
    def ordered(small, big):
        nm, q, k, rel, ao, cw, al, dtb, dno, nf = small
        wi, wo, wgt, wu, wdn = big
        return [nm, lead(wi), q, k, lead(rel), ao, lead(cw), al, dtb, dno, lead(wo), nf, lead(wgt), lead(wu), lead(wdn)]
    outs = [loss, gx[None]]
    for small, big in ((s_g, bg), (s_d, bd_), (s_m, bm), (s_v, bv)):
        outs += ordered(small, big)
    return tuple(outs)
```

```python
import functools
import math

import jax
import jax.numpy as jnp
from jax import lax
from jax.experimental import pallas as pl
from jax.experimental.pallas import tpu as pltpu

F32 = jnp.float32
BF16 = jnp.bfloat16
_MXU = jnp.bfloat16

D = 1024
AW = 512
NHA = 8
DHA = 64
CH = 64
BAND = 9
NHD = 4
DHD = 128
DW = 512
FF = 2816
EPS = 1e-6
NEG = -1e30
N_DEV = 8
LANES = 128
VMEM_LIMIT = 56 * 1024 * 1024

ADAM_LR = 0.001
ADAM_B1 = 0.9
ADAM_B2 = 0.999
ADAM_EPS = 1e-08
ADAM_WD = 0.01
ADAM_STEP = 10

MESH_T = pl.DeviceIdType.MESH


def _cp(sem=None, vmem=VMEM_LIMIT):
    kw = dict(vmem_limit_bytes=vmem)
    if sem is not None:
        kw["dimension_semantics"] = sem
    return pltpu.CompilerParams(**kw)


def _dot(a, b):
    return jnp.dot(a.astype(_MXU), b.astype(_MXU), preferred_element_type=F32)


def _dot_nt(a, b):
    return lax.dot_general(a.astype(_MXU), b.astype(_MXU), (((1,), (1,)), ((), ())), preferred_element_type=F32)


def _dot_tn(a, b):
    return lax.dot_general(a.astype(_MXU), b.astype(_MXU), (((0,), (0,)), ((), ())), preferred_element_type=F32)


def _split2(x):
    hi = x.astype(BF16)
    lo = (x - hi.astype(F32)).astype(BF16)
    return hi, lo


def _dot_x2(x, ones_b):
    hi, lo = _split2(x)
    return jnp.dot(hi, ones_b, preferred_element_type=F32) + jnp.dot(lo, ones_b, preferred_element_type=F32)


def _dot_x2_nt(x, ones_b):
    hi, lo = _split2(x)
    dn = (((1,), (1,)), ((), ()))
    return lax.dot_general(hi, ones_b, dn, preferred_element_type=F32) + lax.dot_general(
        lo, ones_b, dn, preferred_element_type=F32)


def _iota(shape, dim):
    return lax.broadcasted_iota(jnp.int32, shape, dim)


def _block_ones(n, blk, dtype=BF16):
    r, c = _iota((n, n), 0), _iota((n, n), 1)
    return jnp.where((r // blk) == (c // blk), 1.0, 0.0).astype(dtype)


def _sigmoid(x):
    return 1.0 / (1.0 + jnp.exp(-x))


def _softplus(x):
    return jnp.maximum(x, 0.0) + jnp.log(1.0 + jnp.exp(-jnp.abs(x)))


def _col(x, k):
    lane = _iota(x.shape, 1)
    return jnp.sum(jnp.where(lane == k, x, 0.0), axis=1, keepdims=True)


def _row(x, k):
    sub = _iota(x.shape, 0)
    return jnp.sum(jnp.where(sub == k, x, 0.0), axis=0, keepdims=True)


def _my_pos():
    return lax.axis_index("x"), lax.axis_index("y"), lax.axis_index("c")


def _all_gather(x2d, name):
    R, W = x2d.shape

    def body(x_ref, out_ref, send_sems, recv_sems, local_sem):
        ag = _Gather(x_ref, out_ref, send_sems, recv_sems, local_sem)
        ag.start()
        ag.forward()
        ag.finish()

    return pl.pallas_call(
        body, name=name,
        out_shape=jax.ShapeDtypeStruct((N_DEV, R, W), x2d.dtype),
        in_specs=[pl.BlockSpec(memory_space=pl.ANY)],
        out_specs=pl.BlockSpec(memory_space=pl.ANY),
        scratch_shapes=_COMM_SEMS,
    )(x2d)


_COMM_SEMS = [pltpu.SemaphoreType.DMA((7,)), pltpu.SemaphoreType.DMA((7,)), pltpu.SemaphoreType.DMA]


class _Gather:
    def __init__(self, x_ref, out_ref, send_sems, recv_sems, local_sem):
        x, y, c = _my_pos()
        me, sibling = (x, y, c), (x, y, 1 - c)
        chips = [(1 - x, y), (x, 1 - y), (1 - x, 1 - y)]

        def slot(px, py, pc):
            return out_ref.at[4 * px + 2 * py + pc]

        def copy(k, block, to, src=None):
            return pltpu.make_async_remote_copy(
                src_ref=slot(*block) if src is None else src, dst_ref=slot(*block),
                send_sem=send_sems.at[k], recv_sem=recv_sems.at[k], device_id=to, device_id_type=MESH_T)

        self.mine = pltpu.make_async_copy(x_ref, slot(*me), local_sem)
        self.first = [copy(0, me, sibling, src=x_ref)]
        self.first += [copy(1 + j, me, (*chip, c), src=x_ref) for j, chip in enumerate(chips)]
        self.passed = [copy(4 + j, (*chip, c), sibling) for j, chip in enumerate(chips)]
        self.from_chips = [copy(1 + j, (*chip, c), me) for j, chip in enumerate(chips)]
        self.from_sibling = [copy(0, sibling, me)] + [copy(4 + j, (*chip, 1 - c), me) for j, chip in enumerate(chips)]

    def start(self):
        self.mine.start()
        for cp in self.first:
            cp.start()

    def forward(self):
        for arrived, onward in zip(self.from_chips, self.passed):
            arrived.wait_recv()
            onward.start()

    def finish(self):
        for cp in self.from_sibling:
            cp.wait_recv()
        for cp in self.first + self.passed:
            cp.wait_send()
        self.mine.wait()


class _Scatter:
    def __init__(self, s_ref, r_ref, send_sems, recv_sems, local_sem):
        x, y, c = _my_pos()
        self.mine = pltpu.make_async_copy(s_ref.at[4 * x + 2 * y + c], r_ref.at[0], local_sem)
        self.copies = []
        for m in range(1, N_DEV):
            px = x ^ ((m >> 2) & 1)
            py = y ^ ((m >> 1) & 1)
            pc = c ^ (m & 1)
            self.copies.append(pltpu.make_async_remote_copy(
                src_ref=s_ref.at[4 * px + 2 * py + pc], dst_ref=r_ref.at[m],
                send_sem=send_sems.at[m - 1], recv_sem=recv_sems.at[m - 1],
                device_id=(px, py, pc), device_id_type=MESH_T))

    def start(self):
        self.mine.start()
        for cp in self.copies:
            cp.start()

    def finish(self):
        for cp in self.copies:
            cp.wait_recv()
        for cp in self.copies:
            cp.wait_send()
        self.mine.wait()


def _all_to_all(send, name):
    def body(s_ref, r_ref, send_sems, recv_sems, local_sem):
        sc = _Scatter(s_ref, r_ref, send_sems, recv_sems, local_sem)
        sc.start()
        sc.finish()

    return pl.pallas_call(
        body, name=name,
        out_shape=jax.ShapeDtypeStruct(send.shape, send.dtype),
        in_specs=[pl.BlockSpec(memory_space=pl.ANY)],
        out_specs=pl.BlockSpec(memory_space=pl.ANY),
        scratch_shapes=_COMM_SEMS,
    )(send)


TM = 512
TF = 256
TG = 512


def _full(shape):
    nd = len(shape)
    return pl.BlockSpec(shape, lambda i: (0,) * nd)


def _rows(tm, w):
    return pl.BlockSpec((tm, w), lambda i: (i, 0))


def _head_sum(x, bd):
    one_pass = lambda t: jnp.dot(t.astype(_MXU), bd.astype(_MXU), preferred_element_type=F32)
    return jnp.concatenate([one_pass(x[:, 0:256]), one_pass(x[:, 256:512])], axis=1)


def _head_rms(x, bd, width):
    return lax.rsqrt(_head_sum(x * x, bd) * (1.0 / width) + EPS)


def _inproj(x, g_mix, w_int, w_ba, qg_t, kg_t, later_w):
    T = x.shape[0]
    nt = T // TM
    ntd = (((1,), (1,)), ((), ()))

    def body(x_ref, g_ref, w_ref, wba_ref, qg_ref, kg_ref, lw_ref, araw_ref, an_ref, draw_ref, z_ref, ba_ref, h_ref,
             lw_all, send_sems, recv_sems, local_sem):
        i = pl.program_id(0)
        ag = _Gather(lw_ref, lw_all, send_sems, recv_sems, local_sem)
        pl.when(i == 0)(ag.start)
        pl.when(i == nt // 2)(ag.forward)
        xv = x_ref[...]
        r = lax.rsqrt(jnp.mean(xv * xv, axis=1, keepdims=True) + EPS)
        h = (xv * r * g_ref[...]).astype(_MXU)
        h_ref[...] = h
        proj = lambda lo, hi: lax.dot_general(h, w_ref[lo:hi, :], ntd, preferred_element_type=F32)
        q, k, v = proj(0, AW), proj(AW, 2 * AW), proj(2 * AW, 3 * AW)
        draw_ref[...] = proj(1536, 3072)
        z_ref[...] = proj(3072, 3584)
        ba_ref[...] = jnp.dot(h, wba_ref[...], preferred_element_type=F32)
        araw_ref[:, 0:AW] = q
        araw_ref[:, AW:2 * AW] = k
        araw_ref[:, 2 * AW:3 * AW] = v
        bd = _block_ones(AW // 2, DHA)
        qn = q * _head_rms(q, bd, DHA) * (qg_ref[...] * (DHA ** -0.5))
        kn = k * _head_rms(k, bd, DHA) * kg_ref[...]
        an_ref[:, 0:AW] = qn.astype(_MXU)
        an_ref[:, AW:2 * AW] = kn.astype(_MXU)
        an_ref[:, 2 * AW:3 * AW] = v.astype(_MXU)
        pl.when(i == nt - 1)(ag.finish)

    anyspec = pl.BlockSpec(memory_space=pl.ANY)
    return pl.pallas_call(
        body, name="inproj", grid=(nt,),
        in_specs=[_rows(TM, D), _full((1, D)), _full((3584, D)), _full((D, 8)), _full((1, AW)), _full((1, AW)),
                  anyspec],
        out_specs=[_rows(TM, 1536), _rows(TM, 1536), _rows(TM, 1536), _rows(TM, DW), _rows(TM, 8), _rows(TM, D),
                   anyspec],
        out_shape=[jax.ShapeDtypeStruct((T, 1536), F32), jax.ShapeDtypeStruct((T, 1536), _MXU),
                   jax.ShapeDtypeStruct((T, 1536), F32), jax.ShapeDtypeStruct((T, DW), F32),
                   jax.ShapeDtypeStruct((T, 8), F32), jax.ShapeDtypeStruct((T, D), _MXU),
                   jax.ShapeDtypeStruct((N_DEV,) + later_w.shape, later_w.dtype)],
        scratch_shapes=_COMM_SEMS,
        compiler_params=_cp(("arbitrary",)),
    )(x, g_mix, w_int, w_ba, qg_t, kg_t, later_w)


TQ = 256
TW = 768
T_LO, T_HI = 65, 256
VAR0 = 384
TOEP = 1024


def _bias_tables(rb_t):
    def body(rb_ref, tab_ref, tabt_ref):
        h = pl.program_id(0)
        rb8 = jnp.broadcast_to(_row(rb_ref[...], h), (8, VAR0))
        n = _iota((VAR0, TOEP), 1)
        t = _iota((VAR0, TOEP), 0)

        def line(m):
            onehot = jnp.where(jnp.clip(512 - m, -128, 128) + 128 == t, 1.0, 0.0).astype(BF16)
            return sum(jnp.dot(p, onehot, preferred_element_type=F32) for p in _split3(rb8))[0:1, :]

        def band(r, j, first_key):
            return ((j >> 6) >= (r >> 6)) & ((j >> 6) <= (r >> 6) + 8) & (j >= first_key)

        g = line(jnp.where(n < TW, n, n - TOEP))
        tab = pltpu.roll(jnp.broadcast_to(g, (TQ, TOEP)), 0, 1, stride=1, stride_axis=0)[:, 0:TW]
        gt = line(jnp.where(n < TQ, -n, TOEP - n))
        tabt = pltpu.roll(jnp.broadcast_to(gt, (TW, TOEP)), 0, 1, stride=1, stride_axis=0)[:, 0:TQ]
        for v in range(3):
            first_key = max(512 - TQ * v, 0)
            tab_ref[v, 0] = jnp.where(band(_iota((TQ, TW), 0), _iota((TQ, TW), 1), first_key), tab, NEG)
            tabt_ref[v, 0] = jnp.where(band(_iota((TW, TQ), 1), _iota((TW, TQ), 0), first_key), tabt, NEG)

    return pl.pallas_call(
        body, name="bias_tables", grid=(NHA,),
        in_specs=[_full((NHA, VAR0))],
        out_specs=[pl.BlockSpec((3, 1, TQ, TW), lambda h: (0, h, 0, 0)),
                   pl.BlockSpec((3, 1, TW, TQ), lambda h: (0, h, 0, 0))],
        out_shape=[jax.ShapeDtypeStruct((3, NHA, TQ, TW), F32), jax.ShapeDtypeStruct((3, NHA, TW, TQ), F32)],
        compiler_params=_cp(("arbitrary",)),
    )(rb_t)


def _bias_grad(dtabt):
    def body(d_ref, o_ref):
        a, b = _iota((TQ, TQ), 0), _iota((TQ, TQ), 1)
        anti = jnp.where(a + b == TQ - 1, 1.0, 0.0).astype(BF16)
        drev = sum(jnp.dot(t, anti, preferred_element_type=F32) for t in _split3(d_ref[0]))
        wide = jnp.concatenate([drev, jnp.zeros((TW, TOEP - TQ), F32)], axis=1)
        cols = jnp.sum(pltpu.roll(wide, 0, 1, stride=1, stride_axis=0), axis=0, keepdims=True)
        c = _iota((TOEP, VAR0), 0)
        idx = jnp.clip(512 + TQ - 1 - c, -128, 128) + 128
        onehot = jnp.where(idx == _iota((TOEP, VAR0), 1), 1.0, 0.0).astype(BF16)
        cols8 = jnp.broadcast_to(cols, (8, TOEP))
        o_ref[0] = sum(jnp.dot(t, onehot, preferred_element_type=F32) for t in _split3(cols8))[0:1, :]

    return pl.pallas_call(
        body, name="bias_grad", grid=(NHA,),
        in_specs=[pl.BlockSpec((1, TW, TQ), lambda h: (h, 0, 0))],
        out_specs=pl.BlockSpec((1, 1, VAR0), lambda h: (h, 0, 0)),
        out_shape=jax.ShapeDtypeStruct((NHA, 1, VAR0), F32),
        compiler_params=_cp(("arbitrary",)),
    )(dtabt)


def _kv_spec(col, back):
    return pl.BlockSpec((TQ, AW), lambda i: (jnp.maximum(i - back, 0), col))


def _attn_fwd(an, tab):
    T = an.shape[0]

    def body(q_ref, k2_ref, k1_ref, k0_ref, v2_ref, v1_ref, v0_ref, tab_ref, o_ref):
        i = pl.program_id(0)
        kwin = jnp.concatenate([k2_ref[...], k1_ref[...], k0_ref[...]], axis=0)
        vwin = jnp.concatenate([v2_ref[...], v1_ref[...], v0_ref[...]], axis=0)
        q = q_ref[...]
        lo_half = _iota((TQ, LANES), 1) < DHA

        def scores(h):
            sl = slice(LANES * (h // 2), LANES * (h // 2 + 1))
            mask = lo_half if h % 2 == 0 else jnp.logical_not(lo_half)
            qm = jnp.where(mask, q[:, sl], jnp.zeros((TQ, LANES), q.dtype))
            return _dot_nt(qm, kwin[:, sl]) + tab_ref[0, h]

        s_next = scores(0)
        outs = []
        for h in range(NHA):
            s = s_next
            if h + 1 < NHA:
                s_next = scores(h + 1)
            sl = slice(LANES * (h // 2), LANES * (h // 2 + 1))
            m = jnp.max(s, axis=1, keepdims=True)
            e = jnp.exp(s - m)
            l = jnp.sum(e, axis=1, keepdims=True)
            outs.append(_dot(e, vwin[:, sl]) / l)
            if h % 2 == 1:
                o_ref[:, sl] = jnp.where(lo_half, outs[h - 1], outs[h])

    return pl.pallas_call(
        body, name="attn_fwd", grid=(T // TQ,),
        in_specs=[pl.BlockSpec((TQ, AW), lambda i: (i, 0)),
                  _kv_spec(1, 2), _kv_spec(1, 1), _kv_spec(1, 0), _kv_spec(2, 2), _kv_spec(2, 1), _kv_spec(2, 0),
                  pl.BlockSpec((1, NHA, TQ, TW), lambda i: (jnp.minimum(i, 2), 0, 0, 0))],
        out_specs=_rows(TQ, AW),
        out_shape=jax.ShapeDtypeStruct((T, AW), F32),
        compiler_params=_cp(("arbitrary",)),
    )(an, an, an, an, an, an, an, tab)


def _attn_bwd(an, dout, tabt, send):
    T = an.shape[0]
    nq = T // TQ

    def qi(i):
        return jnp.minimum(i, nq - 1)

    def kv_spec(col, back):
        return pl.BlockSpec((TQ, AW), lambda i: (jnp.maximum(qi(i) - back, 0), col))

    def body(q_ref, do_ref, k2_ref, k1_ref, k0_ref, v2_ref, v1_ref, v0_ref, tabt_ref, send_ref,
             dq_ref, dk_ref, dv_ref, dtab_ref, recv_ref, dk_acc, dv_acc, send_sems, recv_sems, local_sem):
        i = pl.program_id(0)
        sc = _Scatter(send_ref, recv_ref, send_sems, recv_sems, local_sem)
        pl.when(i == 0)(sc.start)

        @pl.when(i == 0)
        def _():
            dtab_ref[...] = jnp.zeros_like(dtab_ref)

        new = i % 3
        dk_acc[new] = jnp.zeros((TQ, AW), F32)
        dv_acc[new] = jnp.zeros((TQ, AW), F32)

        @pl.when(i < nq)
        def _():
            kwin = jnp.concatenate([k2_ref[...], k1_ref[...], k0_ref[...]], axis=0)
            vwin = jnp.concatenate([v2_ref[...], v1_ref[...], v0_ref[...]], axis=0)
            q = q_ref[...]
            do = do_ref[...].astype(_MXU)
            lo_half = _iota((TQ, LANES), 1) < DHA

            def front(h):
                sl = slice(LANES * (h // 2), LANES * (h // 2 + 1))
                mask = lo_half if h % 2 == 0 else jnp.logical_not(lo_half)
                zero = jnp.zeros((TQ, LANES), q.dtype)
                qm = jnp.where(mask, q[:, sl], zero)
                dom = jnp.where(mask, do[:, sl], zero)
                st = _dot_nt(kwin[:, sl], qm) + tabt_ref[0, h]
                return st, _dot_nt(vwin[:, sl], dom), qm, dom, mask

            pairs = {}

            def back(h, ptb, dsb, qm, dom, mask):
                sl = slice(LANES * (h // 2), LANES * (h // 2 + 1))
                dv = _dot(ptb, dom)
                dk = _dot(dsb, qm)
                dq = jnp.where(mask, _dot_tn(dsb, kwin[:, sl]), 0.0)
                if h % 2 == 0:
                    pairs[h // 2] = (dq, dk, dv)
                    return
                dq0, dk0, dv0 = pairs.pop(h // 2)
                dq_ref[:, sl] = dq0 + dq
                dk_pair, dv_pair = dk0 + dk, dv0 + dv
                for w in range(3):
                    slot = (i + 1 + w) % 3
                    rows = slice(TQ * w, TQ * (w + 1))
                    dk_acc[slot, :, sl] += dk_pair[rows]
                    dv_acc[slot, :, sl] += dv_pair[rows]

            nxt = front(0)
            pending = None
            for h in range(NHA):
                st, dpt, qm, dom, mask = nxt
                if h + 1 < NHA:
                    nxt = front(h + 1)
                m = jnp.max(st, axis=0, keepdims=True)
                e = jnp.exp(st - m)
                pt = e * (1.0 / jnp.sum(e, axis=0, keepdims=True))
                delta = jnp.sum(pt * dpt, axis=0, keepdims=True)
                dst = pt * (dpt - delta)
                dtab_ref[h] += dst
                if pending is not None:
                    back(*pending)
                pending = (h, pt.astype(_MXU), dst.astype(_MXU), qm, dom, mask)
            back(*pending)

        @pl.when(i >= 2)
        def _():
            done = (i + 1) % 3
            dk_ref[...] = dk_acc[done]
            dv_ref[...] = dv_acc[done]

        pl.when(i == nq + 1)(sc.finish)

    back2 = pl.BlockSpec((TQ, AW), lambda i: (jnp.maximum(i - 2, 0), 0))
    anyspec = pl.BlockSpec(memory_space=pl.ANY)
    return pl.pallas_call(
        body, name="attn_bwd", grid=(nq + 2,),
        in_specs=[pl.BlockSpec((TQ, AW), lambda i: (qi(i), 0)), pl.BlockSpec((TQ, AW), lambda i: (qi(i), 0)),
                  kv_spec(1, 2), kv_spec(1, 1), kv_spec(1, 0), kv_spec(2, 2), kv_spec(2, 1), kv_spec(2, 0),
                  pl.BlockSpec((1, NHA, TW, TQ), lambda i: (jnp.minimum(i, 2), 0, 0, 0)), anyspec],
        out_specs=[pl.BlockSpec((TQ, AW), lambda i: (qi(i), 0)), back2, back2, _full((NHA, TW, TQ)), anyspec],
        out_shape=[jax.ShapeDtypeStruct((T, AW), F32), jax.ShapeDtypeStruct((T, AW), F32),
                   jax.ShapeDtypeStruct((T, AW), F32), jax.ShapeDtypeStruct((NHA, TW, TQ), F32),
                   jax.ShapeDtypeStruct(send.shape, send.dtype)],
        scratch_shapes=[pltpu.VMEM((3, TQ, AW), F32), pltpu.VMEM((3, TQ, AW), F32)] + _COMM_SEMS,
        compiler_params=_cp(("arbitrary",)),
    )(an, dout, an, an, an, an, an, an, tabt, send)


GR = 128
NG = TG // GR
CPT = TG // CH
CONV_K = 4


def _split3(x):
    a = x.astype(BF16)
    r = x - a.astype(F32)
    b = r.astype(BF16)
    c = (r - b.astype(F32)).astype(BF16)
    return a, b, c


def _ones_dot(ones_b, x):
    return sum(jnp.dot(ones_b, t, preferred_element_type=F32) for t in _split3(x))


def _dot_ones_nt(x, ones_b):
    dn = (((1,), (1,)), ((), ()))
    return sum(lax.dot_general(t, ones_b, dn, preferred_element_type=F32) for t in _split3(x))


def _dn_masks():
    r, c = _iota((GR, GR), 0), _iota((GR, GR), 1)
    same = (r >> 6) == (c >> 6)
    one = lambda m: jnp.where(m, 1.0, 0.0).astype(BF16)
    return dict(
        tril=same & (c <= r), strict=same & (c < r), triu=same & (c >= r), strict_u=same & (c > r),
        tril_b=one(same & (c <= r)), triu_b=one(same & (c >= r)), blk_b=one(same), eye_b=one(r == c),
        eye=jnp.where(r == c, 1.0, 0.0).astype(F32),
        fold_b=one((_iota((GR, CH), 0) & (CH - 1)) == _iota((GR, CH), 1)),
        last=(_iota((GR, 1), 0) & (CH - 1)) == CH - 1,
    )


def _shift_down(x, halo, k):
    if k == 0:
        return x
    xs = pltpu.roll(x, k, 0)
    hs = pltpu.roll(halo, k, 0)
    top = jnp.where(_iota(halo.shape, 0) < k, hs, xs[0:8])
    return jnp.concatenate([top, xs[8:]], axis=0)


def _shift_up(x, halo, k):
    if k == 0:
        return x
    n = x.shape[0]
    xs = pltpu.roll(x, n - k, 0)
    hs = pltpu.roll(halo, 8 - k, 0)
    bot = jnp.where(_iota(halo.shape, 0) >= 8 - k, hs, xs[n - 8:n])
    return jnp.concatenate([xs[0:n - 8], bot], axis=0)


def _conv(x, halo, w):
    y = x * w[CONV_K - 1:CONV_K, :]
    for k in range(1, CONV_K):
        y = y + _shift_down(x, halo, k) * w[CONV_K - 1 - k:CONV_K - k, :]
    return y


def _tri_inv(lmats, eye):
    ps = [-m for m in lmats]
    rs = [eye + p for p in ps]
    for _ in range(5):
        ps = [_dot(p, p) for p in ps]
        rs = [r + _dot(r, p) for r, p in zip(rs, ps)]
    return rs


def _gate_terms(ba_g, bat_g, alog8, dtb8, alog8t, dtb8t, K):
    g8 = -jnp.exp(alog8) * _softplus(ba_g + dtb8)
    g8t = -jnp.exp(alog8t) * _softplus(bat_g + dtb8t)
    gc8 = _ones_dot(K["tril_b"], g8)
    gl8 = _ones_dot(K["blk_b"], g8)
    gcrow8 = _dot_ones_nt(g8t, K["tril_b"])
    return g8, gc8, gl8, gcrow8


def _dn_heads(c_tile, rows, beta8, gc8, gl8, gcrow8, K, pre=None):
    return _dn_heads_groups(c_tile, [(rows, beta8, gc8, gl8, gcrow8)], K, None if pre is None else [pre])[0]


def _dn_heads_groups(c_tile, groups, K, pres=None):
    ds = [_dn_head_vec(c_tile, rows, h, beta8, gc8, gl8, gcrow8, K)
          for rows, beta8, gc8, gl8, gcrow8 in groups for h in range(NHD)]
    pls = [_dot_nt(d["kb"], d["kn"]) for d in ds]
    pms = [_dot_nt(d["qn"], d["kn"]) for d in ds]
    for d, pl_, pm in zip(ds, pls, pms):
        d.update(pl=pl_, pm=pm, lmat=jnp.where(K["strict"], pl_ * d["gam_m"], 0.0), mm=pm * d["gam_m"])
    if pres is None:
        for d, tm in zip(ds, _tri_inv([d["lmat"] for d in ds], K["eye"])):
            d.update(tm=tm, u=_dot(tm, d["vb"]), w=_dot(tm, d["kg"]))
    else:
        for d, (tm, u, w) in zip(ds, [p for pre in pres for p in pre]):
            d.update(tm=tm, u=u, w=w)
    return [ds[NHD * k:NHD * (k + 1)] for k in range(len(groups))]


def _dn_head_vec(c_tile, rows, h, beta8, gc8, gl8, gcrow8, K):
    qr = c_tile[rows, DHD * h:DHD * (h + 1)]
    kr = c_tile[rows, DW + DHD * h:DW + DHD * (h + 1)]
    v = c_tile[rows, 2 * DW + DHD * h:2 * DW + DHD * (h + 1)]
    rq = lax.rsqrt(jnp.sum(qr * qr, axis=1, keepdims=True) + EPS)
    rk = lax.rsqrt(jnp.sum(kr * kr, axis=1, keepdims=True) + EPS)
    qh, kn = qr * rq, kr * rk
    qn = qh * (DHD ** -0.5)
    beta = _col(beta8, h)
    gccol, glcol, gcrow = _col(gc8, NHD + h), _col(gl8, NHD + h), _row(gcrow8, NHD + h)
    diff = gccol - gcrow
    gam_m = jnp.exp(jnp.where(K["tril"], diff, NEG))
    gam = jnp.exp(gccol)
    egl = jnp.exp(glcol - gccol)
    kb, vb = kn * beta, v * beta
    kg = kb * gam
    return dict(qr=qr, kr=kr, v=v, rq=rq, rk=rk, qh=qh, qn=qn, kn=kn, beta=beta, diff=diff, gam_m=gam_m, gam=gam,
                egl=egl, el=jnp.exp(glcol), kb=kb, vb=vb, kg=kg, qd=qn * gam, kd=kn * egl)


def _halo_prev(width):
    return pl.BlockSpec((8, width), lambda i: (jnp.maximum(i * (TG // 8) - 1, 0), 0))


def _dn_prep(draw, conv_w, ba, bat, alog8, dtb8, alog8t, dtb8t):
    T = draw.shape[0]
    nb = T // TG
    hm = lambda w, dt: jax.ShapeDtypeStruct((NHD, T, w), dt)
    hm_spec = lambda w: pl.BlockSpec((NHD, TG, w), lambda i: (0, i, 0))
    pc = lambda r, c: jax.ShapeDtypeStruct((NHD, T // CH, r, c), _MXU)
    pc_spec = lambda r, c: pl.BlockSpec((NHD, CPT, r, c), lambda i: (0, i, 0, 0))

    def body(x_ref, halo_ref, cw_ref, ba_ref, bat_ref, al_ref, dt_ref, alt_ref, dtt_ref,
             u_ref, w_ref, kd_ref, tm_ref, wq_ref, km_ref, mq_ref, wt_ref, elb_ref, cv_ref):
        i = pl.program_id(0)
        K = _dn_masks()
        halo = jnp.where(i > 0, halo_ref[...], 0.0)
        cv = _conv(x_ref[...], halo, cw_ref[...])
        cv_ref[...] = cv
        c_tile = cv * _sigmoid(cv)
        eye128 = jnp.where(_iota((DHD, DHD), 0) == _iota((DHD, DHD), 1), 1.0, 0.0).astype(_MXU)
        def gate_inputs(g):
            rows = slice(GR * g, GR * (g + 1))
            ba_g = ba_ref[rows, :]
            _, gc8, gl8, gcrow8 = _gate_terms(ba_g, bat_ref[:, rows], al_ref[...], dt_ref[...], alt_ref[...],
                                              dtt_ref[...], K)
            return rows, _sigmoid(ba_g), gc8, gl8, gcrow8

        def store(g, rows, ds):
            mmts = [_dot_nt(d["kn"], d["qn"]) * jnp.exp(jnp.where(K["triu"], -d["diff"], NEG)) for d in ds]
            mcs = [_dot(d["mm"], K["fold_b"]) for d in ds]
            mcts = [_dot(m, K["fold_b"]) for m in mmts]
            for h, d in enumerate(ds):
                tm_ref[h, rows, :] = d["tm"].astype(_MXU)
                u_ref[h, rows, :] = d["u"]
                w_ref[h, rows, :] = d["w"].astype(_MXU)
                kd_ref[h, rows, :] = d["kd"].astype(_MXU)
                elb = jnp.broadcast_to(d["el"], (GR, DHD))
                for cc in range(GR // CH):
                    ch = slice(CH * cc, CH * (cc + 1))
                    n = (GR // CH) * g + cc
                    wq_ref[h, n, 0:CH, :] = d["w"][ch].astype(_MXU)
                    wq_ref[h, n, CH:2 * CH, :] = d["qd"][ch].astype(_MXU)
                    km_ref[h, n, 0:DHD, :] = _dot_nt(eye128, d["kd"][ch]).astype(_MXU)
                    km_ref[h, n, DHD:DHD + CH, :] = mcs[h][ch].astype(_MXU)
                    mq_ref[h, n, 0:CH, :] = mcts[h][ch].astype(_MXU)
                    mq_ref[h, n, CH:CH + DHD, :] = _dot_nt(eye128, d["qd"][ch]).astype(_MXU)
                    wt_ref[h, n] = _dot_nt(eye128, d["w"][ch]).astype(_MXU)
                    elb_ref[n:n + 1, DHD * h:DHD * (h + 1)] = elb[CH * cc:CH * cc + 1, :]

        PAIR = 2
        for g0 in range(0, NG, PAIR):
            pair = [gate_inputs(g) for g in range(g0, g0 + PAIR)]
            for k, ds in enumerate(_dn_heads_groups(c_tile, pair, K)):
                store(g0 + k, pair[k][0], ds)

    return pl.pallas_call(
        body, name="dn_prep", grid=(nb,),
        in_specs=[_rows(TG, 1536), _halo_prev(1536), _full((CONV_K, 1536)), _rows(TG, 8),
                  pl.BlockSpec((8, TG), lambda i: (0, i)), _full((1, 8)), _full((1, 8)), _full((8, 1)), _full((8, 1))],
        out_specs=[hm_spec(DHD), hm_spec(DHD), hm_spec(DHD), hm_spec(GR),
                   pc_spec(2 * CH, DHD), pc_spec(DHD + CH, CH), pc_spec(CH + DHD, CH), pc_spec(DHD, CH),
                   pl.BlockSpec((CPT, NHD * DHD), lambda i: (i, 0)), _rows(TG, 1536)],
        out_shape=[hm(DHD, F32), hm(DHD, _MXU), hm(DHD, _MXU), hm(GR, _MXU),
                   pc(2 * CH, DHD), pc(DHD + CH, CH), pc(CH + DHD, CH), pc(DHD, CH),
                   jax.ShapeDtypeStruct((T // CH, NHD * DHD), F32), jax.ShapeDtypeStruct((T, 1536), F32)],
        compiler_params=_cp(("arbitrary",)),
    )(draw, draw, conv_w, ba, bat, alog8, dtb8, alog8t, dtb8t)


def _dn_scan(u, wq, km, elb):
    T = u.shape[1]
    nb = T // TG
    hm_spec = lambda wd: pl.BlockSpec((NHD, TG, wd), lambda i: (0, i, 0))

    def body(u_ref, wq_ref, km_ref, elb_ref, o_ref, vn_ref, sn_ref, S):
        @pl.when(pl.program_id(0) == 0)
        def _():
            S[...] = jnp.zeros_like(S)

        sub8 = _iota((CPT, DHD), 0)
        heads = range(NHD)

        def chunk(cc, carry):
            rs = pl.ds(pl.multiple_of(cc * CH, CH), CH)
            sh = [S[h] for h in heads]
            sb = [s.astype(_MXU) for s in sh]
            r1 = [_dot(wq_ref[h, cc], sb[h]) for h in heads]
            vnb = [(u_ref[h, rs, :] - r1[h][0:CH]).astype(_MXU) for h in heads]
            r2 = [_dot(km_ref[h, cc], vnb[h]) for h in heads]
            for h in heads:
                el = jnp.sum(jnp.where(sub8 == cc, elb_ref[:, DHD * h:DHD * (h + 1)], 0.0), axis=0, keepdims=True)
                S[h] = sh[h] * el + r2[h][0:DHD]
                sn_ref[cc, h] = sb[h]
                vn_ref[h, rs, :] = vnb[h]
                o_ref[h, rs, :] = r1[h][CH:2 * CH] + r2[h][DHD:DHD + CH]
            return carry

        lax.fori_loop(0, CPT, chunk, 0)

    return pl.pallas_call(
        body, name="dn_scan", grid=(nb,),
        in_specs=[hm_spec(DHD), pl.BlockSpec((NHD, CPT, 2 * CH, DHD), lambda i: (0, i, 0, 0)),
                  pl.BlockSpec((NHD, CPT, DHD + CH, CH), lambda i: (0, i, 0, 0)),
                  pl.BlockSpec((CPT, NHD * DHD), lambda i: (i, 0))],
        out_specs=[hm_spec(DHD), hm_spec(DHD), pl.BlockSpec((CPT, NHD, DHD, DHD), lambda i: (i, 0, 0, 0))],
        out_shape=[jax.ShapeDtypeStruct((NHD, T, DHD), F32), jax.ShapeDtypeStruct((NHD, T, DHD), _MXU),
                   jax.ShapeDtypeStruct((T // CH, NHD, DHD, DHD), _MXU)],
        scratch_shapes=[pltpu.VMEM((NHD, DHD, DHD), F32)],
        compiler_params=_cp(("arbitrary",)),
    )(u, wq, km, elb)


def _dn_scan_bwd(do, mq, kd, wt, sn, vn, elb):
    T = do.shape[1]
    nb = T // TG
    rev = lambda wd: pl.BlockSpec((NHD, TG, wd), lambda i: (0, nb - 1 - i, 0))
    rev_t = lambda r: pl.BlockSpec((NHD, CPT, r, CH), lambda i: (0, nb - 1 - i, 0, 0))

    def body(do_ref, mq_ref, kd_ref, wt_ref, sn_ref, vn_ref, elb_ref,
             du_ref, dw_ref, dqd_ref, dkd_ref, dgx_ref, dS):
        @pl.when(pl.program_id(0) == 0)
        def _():
            dS[...] = jnp.zeros_like(dS)

        last_row = _iota((CH, DHD), 0) == CH - 1
        sub8 = _iota((CPT, DHD), 0)
        heads = range(NHD)

        def chunk(k, carry):
            cc = CPT - 1 - k
            rs = pl.ds(pl.multiple_of(cc * CH, CH), CH)
            dsh = [dS[h] for h in heads]
            dsb = [d.astype(_MXU) for d in dsh]
            doc = [do_ref[h, rs, :].astype(_MXU) for h in heads]
            a = [_dot(mq_ref[h, cc], doc[h]) for h in heads]
            b = [_dot(kd_ref[h, rs, :], dsb[h]) for h in heads]
            dvn = [a[h][0:CH] + b[h] for h in heads]
            dvnb = [d.astype(_MXU) for d in dvn]
            e = [_dot(wt_ref[h, cc], dvnb[h]) for h in heads]
            for h in heads:
                el = jnp.sum(jnp.where(sub8 == cc, elb_ref[:, DHD * h:DHD * (h + 1)], 0.0), axis=0, keepdims=True)
                sn = sn_ref[cc, h]
                dS[h] = a[h][CH:CH + DHD] + dsh[h] * el - e[h]
                du_ref[h, rs, :] = dvn[h]
                c = _dot_nt(jnp.concatenate([doc[h], dvnb[h]], axis=0), sn)
                dqd_ref[h, rs, :] = c[0:CH]
                dw_ref[h, rs, :] = -c[CH:2 * CH]
                dkd_ref[h, rs, :] = _dot_nt(vn_ref[h, rs, :], dsb[h])
                part = jnp.sum(dsh[h] * sn.astype(F32), axis=0, keepdims=True) * el
                dgx_ref[h, rs, :] = jnp.where(last_row, part, 0.0)
            return carry

        lax.fori_loop(0, CPT, chunk, 0)

    o = jax.ShapeDtypeStruct((NHD, T, DHD), F32)
    return pl.pallas_call(
        body, name="dn_scan_bwd", grid=(nb,),
        in_specs=[rev(DHD), rev_t(CH + DHD), rev(DHD), rev_t(DHD),
                  pl.BlockSpec((CPT, NHD, DHD, DHD), lambda i: (nb - 1 - i, 0, 0, 0)), rev(DHD),
                  pl.BlockSpec((CPT, NHD * DHD), lambda i: (nb - 1 - i, 0))],
        out_specs=[rev(DHD)] * 5,
        out_shape=[o] * 5,
        scratch_shapes=[pltpu.VMEM((NHD, DHD, DHD), F32)],
        compiler_params=_cp(("arbitrary",)),
    )(do, mq, kd, wt, sn, vn, elb)


def _put_col(acc, k, col):
    return jnp.where(_iota(acc.shape, 1) == k, col, acc)


def _dn_post_bwd(draw, cv, conv_w, ba, bat, alog8, dtb8, alog8t, dtb8t, du, dw, dqd, dkd, dgx, do, vn, tm, u, w):
    T = draw.shape[0]
    nb = T // TG
    hm_spec = lambda wd: pl.BlockSpec((NHD, TG, wd), lambda i: (0, nb - 1 - i, 0))
    rrows = lambda w: pl.BlockSpec((TG, w), lambda i: (nb - 1 - i, 0))

    def body(x_ref, cv_ref, cw_ref, ba_ref, bat_ref, al_ref, dt_ref, alt_ref, dtt_ref,
             du_ref, dw_ref, dqd_ref, dkd_ref, dgx_ref, do_ref, vn_ref, tm_ref, u_ref, w_ref,
             dx_ref, dba_ref, sm_ref, dcw_ref, dc_ref, nxt_ref):
        i = pl.program_id(0)

        @pl.when(i == 0)
        def _():
            sm_ref[...] = jnp.zeros_like(sm_ref)
            dcw_ref[...] = jnp.zeros_like(dcw_ref)
            nxt_ref[...] = jnp.zeros_like(nxt_ref)

        K = _dn_masks()
        cv = cv_ref[...]
        sg = _sigmoid(cv)
        c_tile = cv * sg
        dsilu = sg * (1.0 + cv * (1.0 - sg))
        for g in range(NG):
            rows = slice(GR * g, GR * (g + 1))
            ba_g = ba_ref[rows, :]
            g8, gc8, gl8, gcrow8 = _gate_terms(ba_g, bat_ref[:, rows], al_ref[...], dt_ref[...], alt_ref[...],
                                               dtt_ref[...], K)
            beta8 = _sigmoid(ba_g)
            dgc8 = jnp.zeros((GR, 8), F32)
            rd8 = jnp.zeros((GR, 8), F32)
            dbeta8 = jnp.zeros((GR, 8), F32)
            pre = [(tm_ref[h, rows, :], u_ref[h, rows, :], w_ref[h, rows, :]) for h in range(NHD)]
            ds = _dn_heads(c_tile, rows, beta8, gc8, gl8, gcrow8, K, pre)
            H = range(NHD)
            eye_b = K["eye_b"].astype(_MXU)
            gam_t = [jnp.exp(jnp.where(K["triu"], -d["diff"], NEG)) for d in ds]
            doh = [do_ref[h, rows, :] for h in H]
            vnh = [vn_ref[h, rows, :] for h in H]
            tt = [_dot_nt(eye_b, d["tm"]) for d in ds]
            dvb = [_dot(tt[h], du_ref[h, rows, :]) for h in H]
            dkg = [_dot(tt[h], dw_ref[h, rows, :]) for h in H]
            plt = [_dot_nt(d["kn"], d["kb"]) for d in ds]
            pmt = [_dot_nt(d["kn"], d["qn"]) for d in ds]
            da = [-(_dot_nt(dvb[h], ds[h]["u"]) + _dot_nt(dkg[h], ds[h]["w"])) for h in H]
            dat = [-(_dot_nt(ds[h]["u"], dvb[h]) + _dot_nt(ds[h]["w"], dkg[h])) for h in H]
            dpm = [jnp.where(K["tril"], _dot_nt(doh[h], vnh[h]), 0.0) * ds[h]["gam_m"] for h in H]
            dpmt = [jnp.where(K["triu"], _dot_nt(vnh[h], doh[h]), 0.0) * gam_t[h] for h in H]
            dpl = [jnp.where(K["strict"], da[h], 0.0) * ds[h]["gam_m"] for h in H]
            dplt = [jnp.where(K["strict_u"], dat[h], 0.0) * gam_t[h] for h in H]
            dkb = [_dot(dpl[h], ds[h]["kn"]) + dkg[h] * ds[h]["gam"] for h in H]
            dqn = [_dot(dpm[h], ds[h]["kn"]) + dqd_ref[h, rows, :] * ds[h]["gam"] for h in H]
            dknm = [_dot(dplt[h], ds[h]["kb"]) + _dot(dpmt[h], ds[h]["qn"]) for h in H]
            for h, d in enumerate(ds):
                kn, dqdh, dkdh = d["kn"], dqd_ref[h, rows, :], dkd_ref[h, rows, :]
                dkn = dknm[h] + dkdh * d["egl"] + dkb[h] * d["beta"]
                dkd_kd = dkdh * d["kd"]
                rd = jnp.sum(dkd_kd, axis=1, keepdims=True)
                dgc = jnp.sum(dpl[h] * d["pl"] + dpm[h] * d["pm"] - dplt[h] * plt[h] - dpmt[h] * pmt[h]
                              + dqdh * d["qd"] + dkg[h] * d["kg"] - dkd_kd + dgx_ref[h, rows, :],
                              axis=1, keepdims=True)
                dgc8 = _put_col(dgc8, NHD + h, dgc)
                rd8 = _put_col(rd8, NHD + h, rd)
                dbeta = jnp.sum(dkb[h] * kn + dvb[h] * d["v"], axis=1, keepdims=True)
                dbeta8 = _put_col(dbeta8, h, dbeta)
                dqh = dqn[h] * (DHD ** -0.5)
                qh = d["qh"]
                dqr = d["rq"] * (dqh - qh * jnp.sum(dqh * qh, axis=1, keepdims=True))
                dkr = d["rk"] * (dkn - kn * jnp.sum(dkn * kn, axis=1, keepdims=True))
                cq = slice(DHD * h, DHD * (h + 1))
                ck = slice(DW + DHD * h, DW + DHD * (h + 1))
                cvv = slice(2 * DW + DHD * h, 2 * DW + DHD * (h + 1))
                dc_ref[rows, cq] = dqr * dsilu[rows, cq]
                dc_ref[rows, ck] = dkr * dsilu[rows, ck]
                dc_ref[rows, cvv] = dvb[h] * d["beta"] * dsilu[rows, cvv]
            dgc8 = dgc8 + jnp.where(K["last"], _ones_dot(K["blk_b"], rd8), 0.0)
            dg8 = _ones_dot(K["triu_b"], dgc8)
            sgm = _sigmoid(ba_g + dt_ref[...])
            dalpha = dg8 * (-jnp.exp(al_ref[...])) * sgm
            lane8 = _iota((GR, 8), 1)
            dba_ref[rows, :] = jnp.where(lane8 < NHD, dbeta8 * beta8 * (1.0 - beta8), dalpha)
            valid = lane8 >= NHD
            sm_ref[0:1, 0:8] += jnp.sum(jnp.where(valid, dg8 * g8, 0.0), axis=0, keepdims=True)
            sm_ref[1:2, 0:8] += jnp.sum(jnp.where(valid, dalpha, 0.0), axis=0, keepdims=True)

        dcv = dc_ref[...]
        xv = x_ref[...]
        nxt = nxt_ref[...]
        w = cw_ref[...]
        dx = dcv * w[CONV_K - 1:CONV_K, :]
        dcw_ref[CONV_K - 1:CONV_K, :] += jnp.sum(dcv * xv, axis=0, keepdims=True)
        for k in range(1, CONV_K):
            j = CONV_K - 1 - k
            up = _shift_up(dcv, nxt, k)
            dx = dx + up * w[j:j + 1, :]
            dcw_ref[j:j + 1, :] += jnp.sum(up * xv, axis=0, keepdims=True)
        dx_ref[...] = dx
        nxt_ref[...] = dcv[0:8]

    return pl.pallas_call(
        body, name="dn_post_bwd", grid=(nb,),
        in_specs=[rrows(1536), rrows(1536), _full((CONV_K, 1536)), rrows(8),
                  pl.BlockSpec((8, TG), lambda i: (0, nb - 1 - i)), _full((1, 8)), _full((1, 8)), _full((8, 1)),
                  _full((8, 1)),
                  hm_spec(DHD), hm_spec(DHD), hm_spec(DHD), hm_spec(DHD), hm_spec(DHD), hm_spec(DHD), hm_spec(DHD),
                  hm_spec(GR), hm_spec(DHD), hm_spec(DHD)],
        out_specs=[rrows(1536), rrows(8), _full((8, LANES)), _full((8, 1536))],
        out_shape=[jax.ShapeDtypeStruct((T, 1536), F32), jax.ShapeDtypeStruct((T, 8), F32),
                   jax.ShapeDtypeStruct((8, LANES), F32), jax.ShapeDtypeStruct((8, 1536), F32)],
        scratch_shapes=[pltpu.VMEM((TG, 1536), F32), pltpu.VMEM((8, 1536), F32)],
        compiler_params=_cp(("arbitrary",)),
    )(draw, cv, conv_w, ba, bat, alog8, dtb8, alog8t, dtb8t, du, dw, dqd, dkd, dgx, do, vn, tm, u, w)


def _rms(x):
    return lax.rsqrt(jnp.mean(x * x, axis=1, keepdims=True) + EPS)


def _rms_bwd(dy, xh, r, g):
    dxh = dy * g
    return r * (dxh - xh * jnp.mean(dxh * xh, axis=1, keepdims=True))


def _hm_rows(tm):
    return pl.BlockSpec((NHD, tm, DHD), lambda i: (0, i, 0))


def _post_mix(apre, o, z, x, w_out, g_a, g_dn):
    T = x.shape[0]

    def body(ap_ref, o_ref, z_ref, x_ref, w_ref, ga_ref, gd_ref, x1_ref, mix_ref):
        ap = ap_ref[...]
        parts = [ap * _rms(ap) * ga_ref[...]]
        zz = z_ref[...]
        for h in range(NHD):
            oh = o_ref[h]
            zh = zz[:, DHD * h:DHD * (h + 1)]
            parts.append(oh * _rms(oh) * gd_ref[...] * (zh * _sigmoid(zh)))
        mix = jnp.concatenate(parts, axis=1).astype(_MXU)
        mix_ref[...] = mix
        x1_ref[...] = x_ref[...] + jnp.dot(mix, w_ref[...], preferred_element_type=F32)

    return pl.pallas_call(
        body, name="post_mix", grid=(T // TM,),
        in_specs=[_rows(TM, AW), _hm_rows(TM), _rows(TM, DW), _rows(TM, D), _full((D, D)), _full((1, AW)),
                  _full((1, DHD))],
        out_specs=[_rows(TM, D), _rows(TM, D)],
        out_shape=[jax.ShapeDtypeStruct((T, D), F32), jax.ShapeDtypeStruct((T, D), _MXU)],
        compiler_params=_cp(("arbitrary",)),
    )(apre, o, z, x, w_out, g_a, g_dn)


def _ffn(x1, tgt, wl_all, g_ffn):
    T = x1.shape[0]
    SH = FF // N_DEV
    nt = (((1,), (1,)), ((), ()))

    def body(x_ref, t_ref, wl_hbm, g_ref,
             dx1_ref, dx1b_ref, h2_ref, act_ref, dgu_ref, dyb_ref, loss_ref, dg_ref, wg, wu, wd, sem):
        @pl.when(pl.program_id(0) == 0)
        def _():
            cps = [pltpu.make_async_copy(wl_hbm.at[dev, pl.ds(128 + SH * k, SH), :], dst.at[pl.ds(SH * dev, SH), :],
                                         sem.at[N_DEV * k + dev])
                   for k, dst in enumerate((wg, wu, wd)) for dev in range(N_DEV)]
            for cp in cps:
                cp.start()
            for cp in cps:
                cp.wait()
            loss_ref[...] = jnp.zeros_like(loss_ref)
            dg_ref[...] = jnp.zeros_like(dg_ref)

        xv = x_ref[...]
        r = _rms(xv)
        xh = xv * r
        gg = g_ref[...]
        h2 = (xh * gg).astype(_MXU)
        h2_ref[...] = h2
        gate = lax.dot_general(h2, wg[...], nt, preferred_element_type=F32)
        up = lax.dot_general(h2, wu[...], nt, preferred_element_type=F32)
        sg = _sigmoid(gate)
        silu = gate * sg
        act = (silu * up).astype(_MXU)
        act_ref[...] = act
        y = xv + jnp.dot(act, wd[...], preferred_element_type=F32)
        err = y - t_ref[...]
        loss_ref[...] += jnp.sum(err * err, axis=0, keepdims=True)
        dy = err * (1.0 / D)
        dyb = dy.astype(_MXU)
        dyb_ref[...] = dyb
        dact = lax.dot_general(dyb, wd[...], nt, preferred_element_type=F32)
        dgate = (dact * up * (sg * (1.0 + gate * (1.0 - sg)))).astype(_MXU)
        dup = (dact * silu).astype(_MXU)
        dgu_ref[:, 0:FF] = dgate
        dgu_ref[:, FF:2 * FF] = dup
        dh2 = (jnp.dot(dgate, wg[...], preferred_element_type=F32)
               + jnp.dot(dup, wu[...], preferred_element_type=F32))
        dg_ref[...] += jnp.sum(dh2 * xh, axis=0, keepdims=True)
        dx1 = dy + _rms_bwd(dh2, xh, r, gg)
        dx1_ref[...] = dx1
        dx1b_ref[...] = dx1.astype(_MXU)

    anyspec = pl.BlockSpec(memory_space=pl.ANY)
    sd = lambda w, dt: jax.ShapeDtypeStruct((T, w), dt)
    return pl.pallas_call(
        body, name="ffn", grid=(T // TF,),
        in_specs=[_rows(TF, D), _rows(TF, D), anyspec, _full((1, D))],
        out_specs=[_rows(TF, D), _rows(TF, D), _rows(TF, D), _rows(TF, FF), _rows(TF, 2 * FF), _rows(TF, D),
                   _full((1, D)), _full((1, D))],
        out_shape=[sd(D, F32), sd(D, _MXU), sd(D, _MXU), sd(FF, _MXU), sd(2 * FF, _MXU), sd(D, _MXU),
                   jax.ShapeDtypeStruct((1, D), F32), jax.ShapeDtypeStruct((1, D), F32)],
        scratch_shapes=[pltpu.VMEM((FF, D), _MXU)] * 3 + [pltpu.SemaphoreType.DMA((3 * N_DEV,))],
        compiler_params=_cp(("arbitrary",)),
    )(x1, tgt, wl_all, g_ffn)


def _mix_bwd(dx1b, w_out, apre, o, z, g_a, g_dn):
    T = dx1b.shape[0]

    def body(dx_ref, w_ref, ap_ref, o_ref, z_ref, ga_ref, gd_ref, dap_ref, do_ref, dz_ref, dga_ref, dgd_ref):
        @pl.when(pl.program_id(0) == 0)
        def _():
            dga_ref[...] = jnp.zeros_like(dga_ref)
            dgd_ref[...] = jnp.zeros_like(dgd_ref)

        dmix = lax.dot_general(dx_ref[...], w_ref[...], (((1,), (1,)), ((), ())), preferred_element_type=F32)
        ap = ap_ref[...]
        ra = _rms(ap)
        ah = ap * ra
        da = dmix[:, 0:AW]
        dga_ref[...] += jnp.sum(da * ah, axis=0, keepdims=True)
        dap_ref[...] = _rms_bwd(da, ah, ra, ga_ref[...])
        zz = z_ref[...]
        gd = gd_ref[...]
        for h in range(NHD):
            cs = slice(DHD * h, DHD * (h + 1))
            dd = dmix[:, AW + DHD * h:AW + DHD * (h + 1)]
            oh = o_ref[h]
            ro = _rms(oh)
            ohh = oh * ro
            zh = zz[:, cs]
            sz = _sigmoid(zh)
            dz_ref[:, cs] = dd * (ohh * gd) * (sz * (1.0 + zh * (1.0 - sz)))
            don = dd * (zh * sz)
            dgd_ref[...] += jnp.sum(don * ohh, axis=0, keepdims=True)
            do_ref[h] = _rms_bwd(don, ohh, ro, gd)

    return pl.pallas_call(
        body, name="mix_bwd", grid=(T // TM,),
        in_specs=[_rows(TM, D), _full((D, D)), _rows(TM, AW), _hm_rows(TM), _rows(TM, DW), _full((1, AW)),
                  _full((1, DHD))],
        out_specs=[_rows(TM, AW), _hm_rows(TM), _rows(TM, DW), _full((1, AW)), _full((1, DHD))],
        out_shape=[jax.ShapeDtypeStruct((T, AW), F32), jax.ShapeDtypeStruct((NHD, T, DHD), F32),
                   jax.ShapeDtypeStruct((T, DW), F32), jax.ShapeDtypeStruct((1, AW), F32),
                   jax.ShapeDtypeStruct((1, DHD), F32)],
        compiler_params=_cp(("arbitrary",)),
    )(dx1b, w_out, apre, o, z, g_a, g_dn)


DPW = 3712


def _inproj_bwd(dqn, dkn, dv, araw, ddraw, dz, dba, x, dx1, w_int, w_ba, g_mix, qg_t, kg_t):
    T = x.shape[0]

    def body(dqn_ref, dkn_ref, dv_ref, ar_ref, dd_ref, dz_ref, dba_ref, x_ref, dx1_ref, w_hbm, wba_ref, g_ref, qg_ref,
             kg_ref, dx_ref, dp_ref, dgm_ref, dqg_ref, dkg_ref, w_ref, w_sem):
        @pl.when(pl.program_id(0) == 0)
        def _():
            cp = pltpu.make_async_copy(w_hbm, w_ref, w_sem)
            cp.start()
            cp.wait()
            dgm_ref[...] = jnp.zeros_like(dgm_ref)
            dqg_ref[...] = jnp.zeros_like(dqg_ref)
            dkg_ref[...] = jnp.zeros_like(dkg_ref)

        bd = _block_ones(AW // 2, DHA)

        def head_norm_bwd(raw, dyn, gain, dg_ref):
            r = _head_rms(raw, bd, DHA)
            xh = raw * r
            dg_ref[...] += jnp.sum(dyn * xh, axis=0, keepdims=True)
            dxh = dyn * gain
            return r * (dxh - xh * (_head_sum(dxh * xh, bd) * (1.0 / DHA)))

        def segment(lo, val):
            vb = val.astype(_MXU)
            dp_ref[:, lo:lo + val.shape[1]] = vb
            return jnp.dot(vb, w_ref[lo:lo + val.shape[1], :], preferred_element_type=F32)

        dh = segment(1536, dd_ref[...]) + segment(2 * AW, dv_ref[...]) + segment(3072, dz_ref[...])
        dbab = dba_ref[...].astype(_MXU)
        dp_ref[:, 3584:DPW] = jnp.zeros((TM, DPW - 3584), _MXU)
        dp_ref[:, 3584:3592] = dbab
        dh = dh + lax.dot_general(dbab, wba_ref[...], (((1,), (1,)), ((), ())), preferred_element_type=F32)
        ar = ar_ref[...]
        dq = head_norm_bwd(ar[:, 0:AW], dqn_ref[...] * (DHA ** -0.5), qg_ref[...], dqg_ref)
        dk = head_norm_bwd(ar[:, AW:2 * AW], dkn_ref[...], kg_ref[...], dkg_ref)
        dh = dh + segment(0, dq) + segment(AW, dk)
        xv = x_ref[...]
        r = _rms(xv)
        xh = xv * r
        dgm_ref[...] += jnp.sum(dh * xh, axis=0, keepdims=True)
        dx_ref[...] = dx1_ref[...] + _rms_bwd(dh, xh, r, g_ref[...])

    return pl.pallas_call(
        body, name="inproj_bwd", grid=(T // TM,),
        in_specs=[_rows(TM, AW), _rows(TM, AW), _rows(TM, AW), _rows(TM, 1536), _rows(TM, 1536), _rows(TM, DW),
                  _rows(TM, 8), _rows(TM, D), _rows(TM, D), pl.BlockSpec(memory_space=pl.ANY), _full((D, 8)),
                  _full((1, D)), _full((1, AW)), _full((1, AW))],
        out_specs=[_rows(TM, D), _rows(TM, DPW), _full((1, D)), _full((1, AW)), _full((1, AW))],
        out_shape=[jax.ShapeDtypeStruct((T, D), F32), jax.ShapeDtypeStruct((T, DPW), _MXU),
                   jax.ShapeDtypeStruct((1, D), F32), jax.ShapeDtypeStruct((1, AW), F32),
                   jax.ShapeDtypeStruct((1, AW), F32)],
        scratch_shapes=[pltpu.VMEM((3584, D), _MXU), pltpu.SemaphoreType.DMA],
        compiler_params=_cp(("arbitrary",)),
    )(dqn, dkn, dv, araw, ddraw, dz, dba, x, dx1, w_int, w_ba, g_mix, qg_t, kg_t)


def _wgrad(a, b, name, tk=512, tn=None, out_dtype=F32, transposed=False):
    T, M = a.shape
    N = b.shape[1]
    tn = N if tn is None else tn
    nk = T // tk

    def body(a_ref, b_ref, o_ref, acc):
        k = pl.program_id(1)

        @pl.when(k == 0)
        def _():
            acc[...] = jnp.zeros_like(acc)

        acc[...] += lax.dot_general(a_ref[...], b_ref[...], (((0,), (0,)), ((), ())), preferred_element_type=F32)

        @pl.when(k == nk - 1)
        def _():
            r = acc[...]
            o_ref[...] = (r.T if transposed else r).astype(out_dtype)

    if transposed:
        out_spec, out_shape = pl.BlockSpec((tn, M), lambda j, k: (j, 0)), (N, M)
    else:
        out_spec, out_shape = pl.BlockSpec((M, tn), lambda j, k: (0, j)), (M, N)
    return pl.pallas_call(
        body, name=name, grid=(N // tn, nk),
        in_specs=[pl.BlockSpec((tk, M), lambda j, k: (k, 0)), pl.BlockSpec((tk, tn), lambda j, k: (k, j))],
        out_specs=out_spec,
        out_shape=jax.ShapeDtypeStruct(out_shape, out_dtype),
        scratch_shapes=[pltpu.VMEM((M, tn), F32)],
        compiler_params=_cp(("arbitrary", "arbitrary")),
    )(a, b)


def _adamw(parts, w, m, v, name, tr, send=None):
    K, R, W = parts.shape
    n = R // tr

    def body(p_ref, w_ref, m_ref, v_ref, *rest):
        if send is not None:
            send_ref, g_ref, d_ref, nm_ref, nv_ref, recv_ref, send_sems, recv_sems, local_sem = rest
            sc = _Scatter(send_ref, recv_ref, send_sems, recv_sems, local_sem)
            pl.when(pl.program_id(0) == 0)(sc.start)
        else:
            g_ref, d_ref, nm_ref, nv_ref = rest
        g = p_ref[0].astype(F32)
        for k in range(1, K):
            g = g + p_ref[k].astype(F32)
        g_ref[...] = g
        nm = ADAM_B1 * m_ref[...] + (1.0 - ADAM_B1) * g
        nv = ADAM_B2 * v_ref[...] + (1.0 - ADAM_B2) * (g * g)
        nm_ref[...] = nm
        nv_ref[...] = nv
        m_hat = nm / (1.0 - ADAM_B1 ** ADAM_STEP)
        v_hat = nv / (1.0 - ADAM_B2 ** ADAM_STEP)
        d_ref[...] = -ADAM_LR * (m_hat / (jnp.sqrt(v_hat) + ADAM_EPS) + ADAM_WD * w_ref[...])
        if send is not None:
            pl.when(pl.program_id(0) == n - 1)(sc.finish)

    o = jax.ShapeDtypeStruct((R, W), F32)
    anyspec = pl.BlockSpec(memory_space=pl.ANY)
    hosted = send is not None
    return pl.pallas_call(
        body, name=name, grid=(n,),
        in_specs=[pl.BlockSpec((K, tr, W), lambda i: (0, i, 0)), _rows(tr, W), _rows(tr, W), _rows(tr, W)]
        + ([anyspec] if hosted else []),
        out_specs=[_rows(tr, W)] * 4 + ([anyspec] if hosted else []),
        out_shape=[o] * 4 + ([jax.ShapeDtypeStruct(send.shape, send.dtype)] if hosted else []),
        scratch_shapes=_COMM_SEMS if hosted else [],
        compiler_params=_cp(("arbitrary",)),
    )(*((parts, w, m, v) + ((send,) if hosted else ())))


SM_ROWS = 136
R_GMIX, R_GFFN, R_QG, R_KG, R_GA, R_GDN, R_ALOG, R_DT, R_LOSS, R_CONV, R_REL = 0, 8, 16, 24, 32, 40, 48, 49, 56, 64, 112


def _small_reduce(gathered):
    def body(p_ref, o_ref):
        s = p_ref[0]
        for k in range(1, N_DEV):
            s = s + p_ref[k]
        o_ref[...] = s
        for r0 in (R_QG, R_KG):
            rs = jnp.sum(s[r0:r0 + 4], axis=0, keepdims=True)
            o_ref[r0:r0 + 1, :] = rs + pltpu.roll(rs, DHA, 1)
        tot = jnp.sum(jnp.sum(s[R_LOSS:R_LOSS + 8], axis=0, keepdims=True), axis=1, keepdims=True)
        o_ref[R_LOSS:R_LOSS + 1, :] = jnp.broadcast_to(tot * (0.5 / D), (1, LANES))

    return pl.pallas_call(
        body, name="small_reduce",
        out_shape=jax.ShapeDtypeStruct((SM_ROWS, LANES), F32),
    )(gathered)


_WIRE = jnp.bfloat16
RA_USED, RA = 449, 464
RL = 128 + 3 * 352


def _pack_rows(parts, rows=None):
    p = jnp.concatenate([t.reshape(-1, D) for t in parts], axis=0) if len(parts) > 1 else parts[0].reshape(-1, D)
    return p if rows is None else jnp.pad(p, ((0, rows - p.shape[0]), (0, 0)))


def _unpack_rows(packed, shapes):
    out, r = [], 0
    for shp in shapes:
        nr = math.prod(shp) // D
        out.append(packed[r:r + nr].reshape(shp))
        r += nr
    return out


def _pad8(t):
    return jnp.pad(t, ((0, (-t.shape[0]) % 8), (0, 0)))


def _pack_lanes(parts):
    rows = []
    for p in parts:
        f = p.reshape(-1)
        pad = (-f.shape[0]) % LANES
        rows.append(jnp.pad(f, (0, pad)).reshape(-1, LANES))
    return jnp.concatenate(rows, axis=0)


def _unpack_lanes(packed, shapes):
    out, r = [], 0
    for shp in shapes:
        n = math.prod(shp)
        nr = -(-n // LANES)
        out.append(packed[r:r + nr].reshape(-1)[:n].reshape(shp))
        r += nr
    return out


def kernel(x, norm_mix_g, w_in, attn_q_norm_g, attn_k_norm_g, rel_bias, attn_out_norm_g, conv_w, a_log, dt_bias, dn_out_norm_g, w_out, norm_ffn_g, w_gate, w_up, w_down, loss_target, m_norm_mix_g, m_w_in, m_attn_q_norm_g, m_attn_k_norm_g, m_rel_bias, m_attn_out_norm_g, m_conv_w, m_a_log, m_dt_bias, m_dn_out_norm_g, m_w_out, m_norm_ffn_g, m_w_gate, m_w_up, m_w_down, v_norm_mix_g, v_w_in, v_attn_q_norm_g, v_attn_k_norm_g, v_rel_bias, v_attn_out_norm_g, v_conv_w, v_a_log, v_dt_bias, v_dn_out_norm_g, v_w_out, v_norm_ffn_g, v_w_gate, v_w_up, v_w_down):
    xs, tgt = x[0], loss_target[0]
    T = xs.shape[0]
    my_idx = 4 * lax.axis_index("x") + 2 * lax.axis_index("y") + lax.axis_index("c")
    late_w = (w_out[0], w_gate[0], w_up[0], w_down[0])
    late_shapes = [w.shape for w in late_w]

    wa_all = _all_gather(_pack_rows([w_in[0].T.astype(_MXU)], RA), "gather_w_in")
    cw_all = _all_gather(jnp.pad(conv_w[0], ((0, 4), (0, 64))), "gather_conv")
    W_in_t = wa_all[:, 0:RA_USED].reshape(N_DEV * RA_USED, D)
    W_int, W_ba = W_in_t[0:3584], W_in_t[3584:3592].T
    conv_full = cw_all[:, 0:CONV_K, 0:192].transpose(1, 0, 2).reshape(CONV_K, 1536)

    qg_t = jnp.tile(attn_q_norm_g, (1, NHA))
    kg_t = jnp.tile(attn_k_norm_g, (1, NHA))
    z4 = jnp.zeros((1, NHD), F32)
    alog8 = jnp.concatenate([z4, a_log], axis=1)
    dtb8 = jnp.concatenate([z4, dt_bias], axis=1)

    late_t = lambda ts: (ts[0], ts[1].T, ts[2].T, ts[3])
    araw, an, draw, z, ba, hb, wl_all = _inproj(xs, norm_mix_g, W_int, W_ba, qg_t, kg_t,
                                                _pack_rows([w.astype(_MXU) for w in late_t(late_w)]))
    W_out = wl_all[:, 0:128].reshape(D, D)
    tab, tabt = _bias_tables(jnp.pad(rel_bias[0].T, ((0, 0), (0, VAR0 - 257))))
    apre = _attn_fwd(an, tab)
    bat = ba.T
    dn_args = (draw, conv_full, ba, bat, alog8, dtb8, alog8.T, dtb8.T)
    u, w, kd, tm, wq, km, mq, wt, elb, cv = _dn_prep(*dn_args)
    o, vn, sn = _dn_scan(u, wq, km, elb)
    x1, mix = _post_mix(apre, o, z, xs, W_out, attn_out_norm_g, dn_out_norm_g)

    dx1, dx1b, h2, act, dgu, dyb, loss_row, dgffn = _ffn(x1, tgt, wl_all, norm_ffn_g)

    by_cols = lambda g, k: g.reshape(D, N_DEV, k).transpose(1, 0, 2).reshape(N_DEV, -1, D)
    gW_out = _wgrad(mix, dx1b, "wgrad_out", out_dtype=_WIRE)
    gW_gu_t = _wgrad(h2, dgu, "wgrad_gate_up", tn=FF, out_dtype=_WIRE, transposed=True)
    gW_down = _wgrad(dyb, act, "wgrad_down", out_dtype=_WIRE, transposed=True)
    send_late = jnp.concatenate(
        [gW_out.reshape(N_DEV, 128, D), gW_gu_t[0:FF].reshape(N_DEV, 352, D), gW_gu_t[FF:].reshape(N_DEV, 352, D),
         gW_down.reshape(N_DEV, 352, D)], axis=1)

    dap, do, dz, dga, dgdn = _mix_bwd(dx1b, W_out, apre, o, z, attn_out_norm_g, dn_out_norm_g)
    dqn, dkn, dv, dtabt, recv_late = _attn_bwd(an, dap, tabt, send_late)
    drel = _bias_grad(dtabt)
    du, dw, dqd, dkd, dgx = _dn_scan_bwd(do, mq, kd, wt, sn, vn, elb)
    ddraw, dba, sm, dcw = _dn_post_bwd(draw, cv, *dn_args[1:], du, dw, dqd, dkd, dgx, do, vn, tm, u, w)
    gx, dproj, dgmix, dqg, dkg = _inproj_bwd(dqn, dkn, dv, araw, ddraw, dz, dba, xs, dx1, W_int, W_ba, norm_mix_g,
                                             qg_t, kg_t)

    gW_in_t = _wgrad(hb, dproj, "wgrad_in", tk=256, out_dtype=_WIRE, transposed=True)
    send_in = jnp.pad(gW_in_t[0:N_DEV * RA_USED].reshape(N_DEV, RA_USED, D), ((0, 0), (0, RA - RA_USED), (0, 0)))
    late_m = (m_w_out[0], m_w_gate[0], m_w_up[0], m_w_down[0])
    late_v = (v_w_out[0], v_w_gate[0], v_w_up[0], v_w_down[0])
    *outs_late, recv_in = _adamw(recv_late, _pack_rows(late_t(late_w)), _pack_rows(late_t(late_m)),
                                 _pack_rows(late_t(late_v)), "adamw_late", 32, send=send_in)
    outs_in = _adamw(recv_in, _pack_rows([w_in[0].T], RA), _pack_rows([m_w_in[0].T], RA),
                     _pack_rows([v_w_in[0].T], RA), "adamw_w_in", 16)
    late_t_shapes = [t.shape for t in late_t(late_w)]
    big = [[a[0:RA_USED].T] + list(late_t(_unpack_rows(b, late_t_shapes))) for a, b in zip(outs_in, outs_late)]
    bg, bd_, bm, bv = big

    partial = jnp.concatenate(
        [dgmix.reshape(8, LANES), dgffn.reshape(8, LANES), _pad8(dqg.reshape(4, LANES)), _pad8(dkg.reshape(4, LANES)),
         _pad8(dga.reshape(4, LANES)), _pad8(dgdn), sm, loss_row.reshape(8, LANES),
         dcw[0:CONV_K].reshape(48, LANES), drel.reshape(24, LANES)], axis=0)
    S = _small_reduce(_all_gather(partial, "gather_small"))
    loss = S[R_LOSS, 0]
    g_conv = lax.dynamic_slice(S[R_CONV:R_CONV + 48].reshape(CONV_K, 1536), (0, 192 * my_idx), (CONV_K, 192))
    sg = [S[R_GMIX:R_GMIX + 8].reshape(1, D), S[R_QG:R_QG + 1, 0:DHA], S[R_KG:R_KG + 1, 0:DHA],
          S[R_REL:R_REL + 24].reshape(NHA, 384)[:, 0:257].T, S[R_GA:R_GA + 4].reshape(1, AW), g_conv,
          S[R_ALOG:R_ALOG + 1, NHD:2 * NHD], S[R_DT:R_DT + 1, NHD:2 * NHD], S[R_GDN:R_GDN + 1], S[R_GFFN:R_GFFN + 8].reshape(1, D)]
    sw = [norm_mix_g, attn_q_norm_g, attn_k_norm_g, rel_bias[0], attn_out_norm_g, conv_w[0], a_log, dt_bias, dn_out_norm_g, norm_ffn_g]
    smm = [m_norm_mix_g, m_attn_q_norm_g, m_attn_k_norm_g, m_rel_bias[0], m_attn_out_norm_g, m_conv_w[0], m_a_log, m_dt_bias, m_dn_out_norm_g, m_norm_ffn_g]
    svv = [v_norm_mix_g, v_attn_q_norm_g, v_attn_k_norm_g, v_rel_bias[0], v_attn_out_norm_g, v_conv_w[0], v_a_log, v_dt_bias, v_dn_out_norm_g, v_norm_ffn_g]
    s_shapes = [t.shape for t in sw]
    pk = lambda ts: _pack_lanes(ts)
    pg = pk(sg)
    padr = (-pg.shape[0]) % 8
    padz = lambda t: jnp.pad(t, ((0, padr), (0, 0)))
    s_out = _adamw(padz(pg)[None], padz(pk(sw)), padz(pk(smm)), padz(pk(svv)), "adamw_small", pg.shape[0] + padr)
    s_g, s_d, s_m, s_v = (_unpack_lanes(t, s_shapes) for t in s_out)

    lead = lambda t: t[None]
    def ordered(small, big):
        nm, q, k, rel, ao, cw, al, dtb, dno, nf = small
        wi, wo, wgt, wu, wdn = big
        return [nm, lead(wi), q, k, lead(rel), ao, lead(cw), al, dtb, dno, lead(wo), nf, lead(wgt), lead(wu), lead(wdn)]
    outs = [loss, gx[None]]
    for small, big in ((s_g, bg), (s_d, bd_), (s_m, bm), (s_v, bv)):
        outs += ordered(small, big)
    return tuple(outs)
```

```python
import functools
import math

import jax
import jax.numpy as jnp
from jax import lax
from jax.experimental import pallas as pl
from jax.experimental.pallas import tpu as pltpu

F32 = jnp.float32
BF16 = jnp.bfloat16
_MXU = jnp.bfloat16

D = 1024
AW = 512
NHA = 8
DHA = 64
CH = 64
BAND = 9
NHD = 4
DHD = 128
DW = 512
FF = 2816
EPS = 1e-6
NEG = -1e30
N_DEV = 8
LANES = 128
VMEM_LIMIT = 56 * 1024 * 1024

ADAM_LR = 0.001
ADAM_B1 = 0.9
ADAM_B2 = 0.999
ADAM_EPS = 1e-08
ADAM_WD = 0.01
ADAM_STEP = 10

MESH_T = pl.DeviceIdType.MESH


def _cp(sem=None, vmem=VMEM_LIMIT):
    kw = dict(vmem_limit_bytes=vmem)
    if sem is not None:
        kw["dimension_semantics"] = sem
    return pltpu.CompilerParams(**kw)


def _dot(a, b):
    return jnp.dot(a.astype(_MXU), b.astype(_MXU), preferred_element_type=F32)


def _dot_nt(a, b):
    return lax.dot_general(a.astype(_MXU), b.astype(_MXU), (((1,), (1,)), ((), ())), preferred_element_type=F32)


def _dot_tn(a, b):
    return lax.dot_general(a.astype(_MXU), b.astype(_MXU), (((0,), (0,)), ((), ())), preferred_element_type=F32)


def _split2(x):
    hi = x.astype(BF16)
    lo = (x - hi.astype(F32)).astype(BF16)
    return hi, lo


def _dot_x2(x, ones_b):
    hi, lo = _split2(x)
    return jnp.dot(hi, ones_b, preferred_element_type=F32) + jnp.dot(lo, ones_b, preferred_element_type=F32)


def _dot_x2_nt(x, ones_b):
    hi, lo = _split2(x)
    dn = (((1,), (1,)), ((), ()))
    return lax.dot_general(hi, ones_b, dn, preferred_element_type=F32) + lax.dot_general(
        lo, ones_b, dn, preferred_element_type=F32)


def _iota(shape, dim):
    return lax.broadcasted_iota(jnp.int32, shape, dim)


def _block_ones(n, blk, dtype=BF16):
    r, c = _iota((n, n), 0), _iota((n, n), 1)
    return jnp.where((r // blk) == (c // blk), 1.0, 0.0).astype(dtype)


def _sigmoid(x):
    return 1.0 / (1.0 + jnp.exp(-x))


def _softplus(x):
    return jnp.maximum(x, 0.0) + jnp.log(1.0 + jnp.exp(-jnp.abs(x)))


def _col(x, k):
    lane = _iota(x.shape, 1)
    return jnp.sum(jnp.where(lane == k, x, 0.0), axis=1, keepdims=True)


def _row(x, k):
    sub = _iota(x.shape, 0)
    return jnp.sum(jnp.where(sub == k, x, 0.0), axis=0, keepdims=True)


def _my_pos():
    return lax.axis_index("x"), lax.axis_index("y"), lax.axis_index("c")


def _all_gather(x2d, name):
    R, W = x2d.shape

    def body(x_ref, out_ref, send_sems, recv_sems, local_sem):
        ag = _Gather(x_ref, out_ref, send_sems, recv_sems, local_sem)
        ag.start()
        ag.forward()
        ag.finish()

    return pl.pallas_call(
        body, name=name,
        out_shape=jax.ShapeDtypeStruct((N_DEV, R, W), x2d.dtype),
        in_specs=[pl.BlockSpec(memory_space=pl.ANY)],
        out_specs=pl.BlockSpec(memory_space=pl.ANY),
        scratch_shapes=_COMM_SEMS,
    )(x2d)


_COMM_SEMS = [pltpu.SemaphoreType.DMA((7,)), pltpu.SemaphoreType.DMA((7,)), pltpu.SemaphoreType.DMA]


class _Gather:
    def __init__(self, x_ref, out_ref, send_sems, recv_sems, local_sem):
        x, y, c = _my_pos()
        me, sibling = (x, y, c), (x, y, 1 - c)
        chips = [(1 - x, y), (x, 1 - y), (1 - x, 1 - y)]

        def slot(px, py, pc):
            return out_ref.at[4 * px + 2 * py + pc]

        def copy(k, block, to, src=None):
            return pltpu.make_async_remote_copy(
                src_ref=slot(*block) if src is None else src, dst_ref=slot(*block),
                send_sem=send_sems.at[k], recv_sem=recv_sems.at[k], device_id=to, device_id_type=MESH_T)

        self.mine = pltpu.make_async_copy(x_ref, slot(*me), local_sem)
        self.first = [copy(0, me, sibling, src=x_ref)]
        self.first += [copy(1 + j, me, (*chip, c), src=x_ref) for j, chip in enumerate(chips)]
        self.passed = [copy(4 + j, (*chip, c), sibling) for j, chip in enumerate(chips)]
        self.from_chips = [copy(1 + j, (*chip, c), me) for j, chip in enumerate(chips)]
        self.from_sibling = [copy(0, sibling, me)] + [copy(4 + j, (*chip, 1 - c), me) for j, chip in enumerate(chips)]

    def start(self):
        self.mine.start()
        for cp in self.first:
            cp.start()

    def forward(self):
        for arrived, onward in zip(self.from_chips, self.passed):
            arrived.wait_recv()
            onward.start()

    def finish(self):
        for cp in self.from_sibling:
            cp.wait_recv()
        for cp in self.first + self.passed:
            cp.wait_send()
        self.mine.wait()


class _Scatter:
    def __init__(self, s_ref, r_ref, send_sems, recv_sems, local_sem):
        x, y, c = _my_pos()
        self.mine = pltpu.make_async_copy(s_ref.at[4 * x + 2 * y + c], r_ref.at[0], local_sem)
        self.copies = []
        for m in range(1, N_DEV):
            px = x ^ ((m >> 2) & 1)
            py = y ^ ((m >> 1) & 1)
            pc = c ^ (m & 1)
            self.copies.append(pltpu.make_async_remote_copy(
                src_ref=s_ref.at[4 * px + 2 * py + pc], dst_ref=r_ref.at[m],
                send_sem=send_sems.at[m - 1], recv_sem=recv_sems.at[m - 1],
                device_id=(px, py, pc), device_id_type=MESH_T))

    def start(self):
        self.mine.start()
        for cp in self.copies:
            cp.start()

    def finish(self):
        for cp in self.copies:
            cp.wait_recv()
        for cp in self.copies:
            cp.wait_send()
        self.mine.wait()


def _all_to_all(send, name):
    def body(s_ref, r_ref, send_sems, recv_sems, local_sem):
        sc = _Scatter(s_ref, r_ref, send_sems, recv_sems, local_sem)
        sc.start()
        sc.finish()

    return pl.pallas_call(
        body, name=name,
        out_shape=jax.ShapeDtypeStruct(send.shape, send.dtype),
        in_specs=[pl.BlockSpec(memory_space=pl.ANY)],
        out_specs=pl.BlockSpec(memory_space=pl.ANY),
        scratch_shapes=_COMM_SEMS,
    )(send)


TM = 512
TF = 256
TG = 512


def _full(shape):
    nd = len(shape)
    return pl.BlockSpec(shape, lambda i: (0,) * nd)


def _rows(tm, w):
    return pl.BlockSpec((tm, w), lambda i: (i, 0))


def _head_sum(x, bd):
    one_pass = lambda t: jnp.dot(t.astype(_MXU), bd.astype(_MXU), preferred_element_type=F32)
    return jnp.concatenate([one_pass(x[:, 0:256]), one_pass(x[:, 256:512])], axis=1)


def _head_rms(x, bd, width):
    return lax.rsqrt(_head_sum(x * x, bd) * (1.0 / width) + EPS)


def _inproj(x, g_mix, w_int, w_ba, qg_t, kg_t, later_w):
    T = x.shape[0]
    nt = T // TM
    ntd = (((1,), (1,)), ((), ()))

    def body(x_ref, g_ref, w_ref, wba_ref, qg_ref, kg_ref, lw_ref, araw_ref, an_ref, draw_ref, z_ref, ba_ref, h_ref,
             lw_all, send_sems, recv_sems, local_sem):
        i = pl.program_id(0)
        ag = _Gather(lw_ref, lw_all, send_sems, recv_sems, local_sem)
        pl.when(i == 0)(ag.start)
        pl.when(i == nt // 2)(ag.forward)
        xv = x_ref[...]
        r = lax.rsqrt(jnp.mean(xv * xv, axis=1, keepdims=True) + EPS)
        h = (xv * r * g_ref[...]).astype(_MXU)
        h_ref[...] = h
        proj = lambda lo, hi: lax.dot_general(h, w_ref[lo:hi, :], ntd, preferred_element_type=F32)
        q, k, v = proj(0, AW), proj(AW, 2 * AW), proj(2 * AW, 3 * AW)
        draw_ref[...] = proj(1536, 3072)
        z_ref[...] = proj(3072, 3584)
        ba_ref[...] = jnp.dot(h, wba_ref[...], preferred_element_type=F32)
        araw_ref[:, 0:AW] = q
        araw_ref[:, AW:2 * AW] = k
        araw_ref[:, 2 * AW:3 * AW] = v
        bd = _block_ones(AW // 2, DHA)
        qn = q * _head_rms(q, bd, DHA) * (qg_ref[...] * (DHA ** -0.5))
        kn = k * _head_rms(k, bd, DHA) * kg_ref[...]
        an_ref[:, 0:AW] = qn.astype(_MXU)
        an_ref[:, AW:2 * AW] = kn.astype(_MXU)
        an_ref[:, 2 * AW:3 * AW] = v.astype(_MXU)
        pl.when(i == nt - 1)(ag.finish)

    anyspec = pl.BlockSpec(memory_space=pl.ANY)
    return pl.pallas_call(
        body, name="inproj", grid=(nt,),
        in_specs=[_rows(TM, D), _full((1, D)), _full((3584, D)), _full((D, 8)), _full((1, AW)), _full((1, AW)),
                  anyspec],
        out_specs=[_rows(TM, 1536), _rows(TM, 1536), _rows(TM, 1536), _rows(TM, DW), _rows(TM, 8), _rows(TM, D),
                   anyspec],
        out_shape=[jax.ShapeDtypeStruct((T, 1536), F32), jax.ShapeDtypeStruct((T, 1536), _MXU),
                   jax.ShapeDtypeStruct((T, 1536), F32), jax.ShapeDtypeStruct((T, DW), F32),
                   jax.ShapeDtypeStruct((T, 8), F32), jax.ShapeDtypeStruct((T, D), _MXU),
                   jax.ShapeDtypeStruct((N_DEV,) + later_w.shape, later_w.dtype)],
        scratch_shapes=_COMM_SEMS,
        compiler_params=_cp(("arbitrary",)),
    )(x, g_mix, w_int, w_ba, qg_t, kg_t, later_w)


TQ = 256
TW = 768
T_LO, T_HI = 65, 256
VAR0 = 384
TOEP = 1024


def _bias_tables(rb_t):
    def body(rb_ref, tab_ref, tabt_ref):
        h = pl.program_id(0)
        rb8 = jnp.broadcast_to(_row(rb_ref[...], h), (8, VAR0))
        n = _iota((VAR0, TOEP), 1)
        t = _iota((VAR0, TOEP), 0)

        def line(m):
            onehot = jnp.where(jnp.clip(512 - m, -128, 128) + 128 == t, 1.0, 0.0).astype(BF16)
            return sum(jnp.dot(p, onehot, preferred_element_type=F32) for p in _split3(rb8))[0:1, :]

        def band(r, j, first_key):
            return ((j >> 6) >= (r >> 6)) & ((j >> 6) <= (r >> 6) + 8) & (j >= first_key)

        g = line(jnp.where(n < TW, n, n - TOEP))
        tab = pltpu.roll(jnp.broadcast_to(g, (TQ, TOEP)), 0, 1, stride=1, stride_axis=0)[:, 0:TW]
        gt = line(jnp.where(n < TQ, -n, TOEP - n))
        tabt = pltpu.roll(jnp.broadcast_to(gt, (TW, TOEP)), 0, 1, stride=1, stride_axis=0)[:, 0:TQ]
        for v in range(3):
            first_key = max(512 - TQ * v, 0)
            tab_ref[v, 0] = jnp.where(band(_iota((TQ, TW), 0), _iota((TQ, TW), 1), first_key), tab, NEG)
            tabt_ref[v, 0] = jnp.where(band(_iota((TW, TQ), 1), _iota((TW, TQ), 0), first_key), tabt, NEG)

    return pl.pallas_call(
        body, name="bias_tables", grid=(NHA,),
        in_specs=[_full((NHA, VAR0))],
        out_specs=[pl.BlockSpec((3, 1, TQ, TW), lambda h: (0, h, 0, 0)),
                   pl.BlockSpec((3, 1, TW, TQ), lambda h: (0, h, 0, 0))],
        out_shape=[jax.ShapeDtypeStruct((3, NHA, TQ, TW), F32), jax.ShapeDtypeStruct((3, NHA, TW, TQ), F32)],
        compiler_params=_cp(("arbitrary",)),
    )(rb_t)


def _bias_grad(dtabt):
    def body(d_ref, o_ref):
        a, b = _iota((TQ, TQ), 0), _iota((TQ, TQ), 1)
        anti = jnp.where(a + b == TQ - 1, 1.0, 0.0).astype(BF16)
        drev = sum(jnp.dot(t, anti, preferred_element_type=F32) for t in _split3(d_ref[0]))
        wide = jnp.concatenate([drev, jnp.zeros((TW, TOEP - TQ), F32)], axis=1)
        cols = jnp.sum(pltpu.roll(wide, 0, 1, stride=1, stride_axis=0), axis=0, keepdims=True)
        c = _iota((TOEP, VAR0), 0)
        idx = jnp.clip(512 + TQ - 1 - c, -128, 128) + 128
        onehot = jnp.where(idx == _iota((TOEP, VAR0), 1), 1.0, 0.0).astype(BF16)
        cols8 = jnp.broadcast_to(cols, (8, TOEP))
        o_ref[0] = sum(jnp.dot(t, onehot, preferred_element_type=F32) for t in _split3(cols8))[0:1, :]

    return pl.pallas_call(
        body, name="bias_grad", grid=(NHA,),
        in_specs=[pl.BlockSpec((1, TW, TQ), lambda h: (h, 0, 0))],
        out_specs=pl.BlockSpec((1, 1, VAR0), lambda h: (h, 0, 0)),
        out_shape=jax.ShapeDtypeStruct((NHA, 1, VAR0), F32),
        compiler_params=_cp(("arbitrary",)),
    )(dtabt)


def _kv_spec(col, back):
    return pl.BlockSpec((TQ, AW), lambda i: (jnp.maximum(i - back, 0), col))


def _attn_fwd(an, tab):
    T = an.shape[0]

    def body(q_ref, k2_ref, k1_ref, k0_ref, v2_ref, v1_ref, v0_ref, tab_ref, o_ref):
        i = pl.program_id(0)
        kwin = jnp.concatenate([k2_ref[...], k1_ref[...], k0_ref[...]], axis=0)
        vwin = jnp.concatenate([v2_ref[...], v1_ref[...], v0_ref[...]], axis=0)
        q = q_ref[...]
        lo_half = _iota((TQ, LANES), 1) < DHA

        def scores(h):
            sl = slice(LANES * (h // 2), LANES * (h // 2 + 1))
            mask = lo_half if h % 2 == 0 else jnp.logical_not(lo_half)
            qm = jnp.where(mask, q[:, sl], jnp.zeros((TQ, LANES), q.dtype))
            return _dot_nt(qm, kwin[:, sl]) + tab_ref[0, h]

        s_next = scores(0)
        outs = []
        for h in range(NHA):
            s = s_next
            if h + 1 < NHA:
                s_next = scores(h + 1)
            sl = slice(LANES * (h // 2), LANES * (h // 2 + 1))
            m = jnp.max(s, axis=1, keepdims=True)
            e = jnp.exp(s - m)
            l = jnp.sum(e, axis=1, keepdims=True)
            outs.append(_dot(e, vwin[:, sl]) / l)
            if h % 2 == 1:
                o_ref[:, sl] = jnp.where(lo_half, outs[h - 1], outs[h])

    return pl.pallas_call(
        body, name="attn_fwd", grid=(T // TQ,),
        in_specs=[pl.BlockSpec((TQ, AW), lambda i: (i, 0)),
                  _kv_spec(1, 2), _kv_spec(1, 1), _kv_spec(1, 0), _kv_spec(2, 2), _kv_spec(2, 1), _kv_spec(2, 0),
                  pl.BlockSpec((1, NHA, TQ, TW), lambda i: (jnp.minimum(i, 2), 0, 0, 0))],
        out_specs=_rows(TQ, AW),
        out_shape=jax.ShapeDtypeStruct((T, AW), F32),
        compiler_params=_cp(("arbitrary",)),
    )(an, an, an, an, an, an, an, tab)


def _attn_bwd(an, dout, tabt, send):
    T = an.shape[0]
    nq = T // TQ

    def qi(i):
        return jnp.minimum(i, nq - 1)

    def kv_spec(col, back):
        return pl.BlockSpec((TQ, AW), lambda i: (jnp.maximum(qi(i) - back, 0), col))

    def body(q_ref, do_ref, k2_ref, k1_ref, k0_ref, v2_ref, v1_ref, v0_ref, tabt_ref, send_ref,
             dq_ref, dk_ref, dv_ref, dtab_ref, recv_ref, dk_acc, dv_acc, send_sems, recv_sems, local_sem):
        i = pl.program_id(0)
        sc = _Scatter(send_ref, recv_ref, send_sems, recv_sems, local_sem)
        pl.when(i == 0)(sc.start)

        @pl.when(i == 0)
        def _():
            dtab_ref[...] = jnp.zeros_like(dtab_ref)

        new = i % 3
        dk_acc[new] = jnp.zeros((TQ, AW), F32)
        dv_acc[new] = jnp.zeros((TQ, AW), F32)

        @pl.when(i < nq)
        def _():
            kwin = jnp.concatenate([k2_ref[...], k1_ref[...], k0_ref[...]], axis=0)
            vwin = jnp.concatenate([v2_ref[...], v1_ref[...], v0_ref[...]], axis=0)
            q = q_ref[...]
            do = do_ref[...].astype(_MXU)
            lo_half = _iota((TQ, LANES), 1) < DHA

            def front(h):
                sl = slice(LANES * (h // 2), LANES * (h // 2 + 1))
                mask = lo_half if h % 2 == 0 else jnp.logical_not(lo_half)
                zero = jnp.zeros((TQ, LANES), q.dtype)
                qm = jnp.where(mask, q[:, sl], zero)
                dom = jnp.where(mask, do[:, sl], zero)
                st = _dot_nt(kwin[:, sl], qm) + tabt_ref[0, h]
                return st, _dot_nt(vwin[:, sl], dom), qm, dom, mask

            pairs = {}

            def back(h, ptb, dsb, qm, dom, mask):
                sl = slice(LANES * (h // 2), LANES * (h // 2 + 1))
                dv = _dot(ptb, dom)
                dk = _dot(dsb, qm)
                dq = jnp.where(mask, _dot_tn(dsb, kwin[:, sl]), 0.0)
                if h % 2 == 0:
                    pairs[h // 2] = (dq, dk, dv)
                    return
                dq0, dk0, dv0 = pairs.pop(h // 2)
                dq_ref[:, sl] = dq0 + dq
                dk_pair, dv_pair = dk0 + dk, dv0 + dv
                for w in range(3):
                    slot = (i + 1 + w) % 3
                    rows = slice(TQ * w, TQ * (w + 1))
                    dk_acc[slot, :, sl] += dk_pair[rows]
                    dv_acc[slot, :, sl] += dv_pair[rows]

            nxt = front(0)
            pending = None
            for h in range(NHA):
                st, dpt, qm, dom, mask = nxt
                if h + 1 < NHA:
                    nxt = front(h + 1)
                m = jnp.max(st, axis=0, keepdims=True)
                e = jnp.exp(st - m)
                pt = e * (1.0 / jnp.sum(e, axis=0, keepdims=True))
                delta = jnp.sum(pt * dpt, axis=0, keepdims=True)
                dst = pt * (dpt - delta)
                dtab_ref[h] += dst
                if pending is not None:
                    back(*pending)
                pending = (h, pt.astype(_MXU), dst.astype(_MXU), qm, dom, mask)
            back(*pending)

        @pl.when(i >= 2)
        def _():
            done = (i + 1) % 3
            dk_ref[...] = dk_acc[done]
            dv_ref[...] = dv_acc[done]

        pl.when(i == nq + 1)(sc.finish)

    back2 = pl.BlockSpec((TQ, AW), lambda i: (jnp.maximum(i - 2, 0), 0))
    anyspec = pl.BlockSpec(memory_space=pl.ANY)
    return pl.pallas_call(
        body, name="attn_bwd", grid=(nq + 2,),
        in_specs=[pl.BlockSpec((TQ, AW), lambda i: (qi(i), 0)), pl.BlockSpec((TQ, AW), lambda i: (qi(i), 0)),
                  kv_spec(1, 2), kv_spec(1, 1), kv_spec(1, 0), kv_spec(2, 2), kv_spec(2, 1), kv_spec(2, 0),
                  pl.BlockSpec((1, NHA, TW, TQ), lambda i: (jnp.minimum(i, 2), 0, 0, 0)), anyspec],
        out_specs=[pl.BlockSpec((TQ, AW), lambda i: (qi(i), 0)), back2, back2, _full((NHA, TW, TQ)), anyspec],
        out_shape=[jax.ShapeDtypeStruct((T, AW), F32), jax.ShapeDtypeStruct((T, AW), F32),
                   jax.ShapeDtypeStruct((T, AW), F32), jax.ShapeDtypeStruct((NHA, TW, TQ), F32),
                   jax.ShapeDtypeStruct(send.shape, send.dtype)],
        scratch_shapes=[pltpu.VMEM((3, TQ, AW), F32), pltpu.VMEM((3, TQ, AW), F32)] + _COMM_SEMS,
        compiler_params=_cp(("arbitrary",)),
    )(an, dout, an, an, an, an, an, an, tabt, send)


GR = 128
NG = TG // GR
CPT = TG // CH
CONV_K = 4


def _split3(x):
    a = x.astype(BF16)
    r = x - a.astype(F32)
    b = r.astype(BF16)
    c = (r - b.astype(F32)).astype(BF16)
    return a, b, c


def _ones_dot(ones_b, x):
    return sum(jnp.dot(ones_b, t, preferred_element_type=F32) for t in _split3(x))


def _dot_ones_nt(x, ones_b):
    dn = (((1,), (1,)), ((), ()))
    return sum(lax.dot_general(t, ones_b, dn, preferred_element_type=F32) for t in _split3(x))


def _dn_masks():
    r, c = _iota((GR, GR), 0), _iota((GR, GR), 1)
    same = (r >> 6) == (c >> 6)
    one = lambda m: jnp.where(m, 1.0, 0.0).astype(BF16)
    return dict(
        tril=same & (c <= r), strict=same & (c < r), triu=same & (c >= r), strict_u=same & (c > r),
        tril_b=one(same & (c <= r)), triu_b=one(same & (c >= r)), blk_b=one(same), eye_b=one(r == c),
        eye=jnp.where(r == c, 1.0, 0.0).astype(F32),
        fold_b=one((_iota((GR, CH), 0) & (CH - 1)) == _iota((GR, CH), 1)),
        last=(_iota((GR, 1), 0) & (CH - 1)) == CH - 1,
    )


def _shift_down(x, halo, k):
    if k == 0:
        return x
    xs = pltpu.roll(x, k, 0)
    hs = pltpu.roll(halo, k, 0)
    top = jnp.where(_iota(halo.shape, 0) < k, hs, xs[0:8])
    return jnp.concatenate([top, xs[8:]], axis=0)


def _shift_up(x, halo, k):
    if k == 0:
        return x
    n = x.shape[0]
    xs = pltpu.roll(x, n - k, 0)
    hs = pltpu.roll(halo, 8 - k, 0)
    bot = jnp.where(_iota(halo.shape, 0) >= 8 - k, hs, xs[n - 8:n])
    return jnp.concatenate([xs[0:n - 8], bot], axis=0)


def _conv(x, halo, w):
    y = x * w[CONV_K - 1:CONV_K, :]
    for k in range(1, CONV_K):
        y = y + _shift_down(x, halo, k) * w[CONV_K - 1 - k:CONV_K - k, :]
    return y


def _tri_inv(lmats, eye):
    ps = [-m for m in lmats]
    rs = [eye + p for p in ps]
    for _ in range(5):
        ps = [_dot(p, p) for p in ps]
        rs = [r + _dot(r, p) for r, p in zip(rs, ps)]
    return rs


def _gate_terms(ba_g, bat_g, alog8, dtb8, alog8t, dtb8t, K):
    g8 = -jnp.exp(alog8) * _softplus(ba_g + dtb8)
    g8t = -jnp.exp(alog8t) * _softplus(bat_g + dtb8t)
    gc8 = _ones_dot(K["tril_b"], g8)
    gl8 = _ones_dot(K["blk_b"], g8)
    gcrow8 = _dot_ones_nt(g8t, K["tril_b"])
    return g8, gc8, gl8, gcrow8


def _dn_heads(c_tile, rows, beta8, gc8, gl8, gcrow8, K, pre=None):
    return _dn_heads_groups(c_tile, [(rows, beta8, gc8, gl8, gcrow8)], K, None if pre is None else [pre])[0]


def _dn_heads_groups(c_tile, groups, K, pres=None):
    ds = [_dn_head_vec(c_tile, rows, h, beta8, gc8, gl8, gcrow8, K)
          for rows, beta8, gc8, gl8, gcrow8 in groups for h in range(NHD)]
    pls = [_dot_nt(d["kb"], d["kn"]) for d in ds]
    pms = [_dot_nt(d["qn"], d["kn"]) for d in ds]
    for d, pl_, pm in zip(ds, pls, pms):
        d.update(pl=pl_, pm=pm, lmat=jnp.where(K["strict"], pl_ * d["gam_m"], 0.0), mm=pm * d["gam_m"])
    if pres is None:
        for d, tm in zip(ds, _tri_inv([d["lmat"] for d in ds], K["eye"])):
            d.update(tm=tm, u=_dot(tm, d["vb"]), w=_dot(tm, d["kg"]))
    else:
        for d, (tm, u, w) in zip(ds, [p for pre in pres for p in pre]):
            d.update(tm=tm, u=u, w=w)
    return [ds[NHD * k:NHD * (k + 1)] for k in range(len(groups))]


def _dn_head_vec(c_tile, rows, h, beta8, gc8, gl8, gcrow8, K):
    qr = c_tile[rows, DHD * h:DHD * (h + 1)]
    kr = c_tile[rows, DW + DHD * h:DW + DHD * (h + 1)]
    v = c_tile[rows, 2 * DW + DHD * h:2 * DW + DHD * (h + 1)]
    rq = lax.rsqrt(jnp.sum(qr * qr, axis=1, keepdims=True) + EPS)
    rk = lax.rsqrt(jnp.sum(kr * kr, axis=1, keepdims=True) + EPS)
    qh, kn = qr * rq, kr * rk
    qn = qh * (DHD ** -0.5)
    beta = _col(beta8, h)
    gccol, glcol, gcrow = _col(gc8, NHD + h), _col(gl8, NHD + h), _row(gcrow8, NHD + h)
    diff = gccol - gcrow
    gam_m = jnp.exp(jnp.where(K["tril"], diff, NEG))
    gam = jnp.exp(gccol)
    egl = jnp.exp(glcol - gccol)
    kb, vb = kn * beta, v * beta
    kg = kb * gam
    return dict(qr=qr, kr=kr, v=v, rq=rq, rk=rk, qh=qh, qn=qn, kn=kn, beta=beta, diff=diff, gam_m=gam_m, gam=gam,
                egl=egl, el=jnp.exp(glcol), kb=kb, vb=vb, kg=kg, qd=qn * gam, kd=kn * egl)


def _halo_prev(width):
    return pl.BlockSpec((8, width), lambda i: (jnp.maximum(i * (TG // 8) - 1, 0), 0))


def _dn_prep(draw, conv_w, ba, bat, alog8, dtb8, alog8t, dtb8t):
    T = draw.shape[0]
    nb = T // TG
    hm = lambda w, dt: jax.ShapeDtypeStruct((NHD, T, w), dt)
    hm_spec = lambda w: pl.BlockSpec((NHD, TG, w), lambda i: (0, i, 0))
    pc = lambda r, c: jax.ShapeDtypeStruct((NHD, T // CH, r, c), _MXU)
    pc_spec = lambda r, c: pl.BlockSpec((NHD, CPT, r, c), lambda i: (0, i, 0, 0))

    def body(x_ref, halo_ref, cw_ref, ba_ref, bat_ref, al_ref, dt_ref, alt_ref, dtt_ref,
             u_ref, w_ref, kd_ref, tm_ref, wq_ref, km_ref, mq_ref, wt_ref, elb_ref, cv_ref):
        i = pl.program_id(0)
        K = _dn_masks()
        halo = jnp.where(i > 0, halo_ref[...], 0.0)
        cv = _conv(x_ref[...], halo, cw_ref[...])
        cv_ref[...] = cv
        c_tile = cv * _sigmoid(cv)
        eye128 = jnp.where(_iota((DHD, DHD), 0) == _iota((DHD, DHD), 1), 1.0, 0.0).astype(_MXU)
        def gate_inputs(g):
            rows = slice(GR * g, GR * (g + 1))
            ba_g = ba_ref[rows, :]
            _, gc8, gl8, gcrow8 = _gate_terms(ba_g, bat_ref[:, rows], al_ref[...], dt_ref[...], alt_ref[...],
                                              dtt_ref[...], K)
            return rows, _sigmoid(ba_g), gc8, gl8, gcrow8

        def store(g, rows, ds):
            mmts = [_dot_nt(d["kn"], d["qn"]) * jnp.exp(jnp.where(K["triu"], -d["diff"], NEG)) for d in ds]
            mcs = [_dot(d["mm"], K["fold_b"]) for d in ds]
            mcts = [_dot(m, K["fold_b"]) for m in mmts]
            for h, d in enumerate(ds):
                tm_ref[h, rows, :] = d["tm"].astype(_MXU)
                u_ref[h, rows, :] = d["u"]
                w_ref[h, rows, :] = d["w"].astype(_MXU)
                kd_ref[h, rows, :] = d["kd"].astype(_MXU)
                elb = jnp.broadcast_to(d["el"], (GR, DHD))
                for cc in range(GR // CH):
                    ch = slice(CH * cc, CH * (cc + 1))
                    n = (GR // CH) * g + cc
                    wq_ref[h, n, 0:CH, :] = d["w"][ch].astype(_MXU)
                    wq_ref[h, n, CH:2 * CH, :] = d["qd"][ch].astype(_MXU)
                    km_ref[h, n, 0:DHD, :] = _dot_nt(eye128, d["kd"][ch]).astype(_MXU)
                    km_ref[h, n, DHD:DHD + CH, :] = mcs[h][ch].astype(_MXU)
                    mq_ref[h, n, 0:CH, :] = mcts[h][ch].astype(_MXU)
                    mq_ref[h, n, CH:CH + DHD, :] = _dot_nt(eye128, d["qd"][ch]).astype(_MXU)
                    wt_ref[h, n] = _dot_nt(eye128, d["w"][ch]).astype(_MXU)
                    elb_ref[n:n + 1, DHD * h:DHD * (h + 1)] = elb[CH * cc:CH * cc + 1, :]

        PAIR = 4
        for g0 in range(0, NG, PAIR):
            pair = [gate_inputs(g) for g in range(g0, g0 + PAIR)]
            for k, ds in enumerate(_dn_heads_groups(c_tile, pair, K)):
                store(g0 + k, pair[k][0], ds)

    return pl.pallas_call(
        body, name="dn_prep", grid=(nb,),
        in_specs=[_rows(TG, 1536), _halo_prev(1536), _full((CONV_K, 1536)), _rows(TG, 8),
                  pl.BlockSpec((8, TG), lambda i: (0, i)), _full((1, 8)), _full((1, 8)), _full((8, 1)), _full((8, 1))],
        out_specs=[hm_spec(DHD), hm_spec(DHD), hm_spec(DHD), hm_spec(GR),
                   pc_spec(2 * CH, DHD), pc_spec(DHD + CH, CH), pc_spec(CH + DHD, CH), pc_spec(DHD, CH),
                   pl.BlockSpec((CPT, NHD * DHD), lambda i: (i, 0)), _rows(TG, 1536)],
        out_shape=[hm(DHD, F32), hm(DHD, _MXU), hm(DHD, _MXU), hm(GR, _MXU),
                   pc(2 * CH, DHD), pc(DHD + CH, CH), pc(CH + DHD, CH), pc(DHD, CH),
                   jax.ShapeDtypeStruct((T // CH, NHD * DHD), F32), jax.ShapeDtypeStruct((T, 1536), F32)],
        compiler_params=_cp(("arbitrary",)),
    )(draw, draw, conv_w, ba, bat, alog8, dtb8, alog8t, dtb8t)


def _dn_scan(u, wq, km, elb):
    T = u.shape[1]
    nb = T // TG
    hm_spec = lambda wd: pl.BlockSpec((NHD, TG, wd), lambda i: (0, i, 0))

    def body(u_ref, wq_ref, km_ref, elb_ref, o_ref, vn_ref, sn_ref, S):
        @pl.when(pl.program_id(0) == 0)
        def _():
            S[...] = jnp.zeros_like(S)

        sub8 = _iota((CPT, DHD), 0)
        heads = range(NHD)

        def chunk(cc, carry):
            rs = pl.ds(pl.multiple_of(cc * CH, CH), CH)
            sh = [S[h] for h in heads]
            sb = [s.astype(_MXU) for s in sh]
            r1 = [_dot(wq_ref[h, cc], sb[h]) for h in heads]
            vnb = [(u_ref[h, rs, :] - r1[h][0:CH]).astype(_MXU) for h in heads]
            r2 = [_dot(km_ref[h, cc], vnb[h]) for h in heads]
            for h in heads:
                el = jnp.sum(jnp.where(sub8 == cc, elb_ref[:, DHD * h:DHD * (h + 1)], 0.0), axis=0, keepdims=True)
                S[h] = sh[h] * el + r2[h][0:DHD]
                sn_ref[cc, h] = sb[h]
                vn_ref[h, rs, :] = vnb[h]
                o_ref[h, rs, :] = r1[h][CH:2 * CH] + r2[h][DHD:DHD + CH]
            return carry

        lax.fori_loop(0, CPT, chunk, 0)

    return pl.pallas_call(
        body, name="dn_scan", grid=(nb,),
        in_specs=[hm_spec(DHD), pl.BlockSpec((NHD, CPT, 2 * CH, DHD), lambda i: (0, i, 0, 0)),
                  pl.BlockSpec((NHD, CPT, DHD + CH, CH), lambda i: (0, i, 0, 0)),
                  pl.BlockSpec((CPT, NHD * DHD), lambda i: (i, 0))],
        out_specs=[hm_spec(DHD), hm_spec(DHD), pl.BlockSpec((CPT, NHD, DHD, DHD), lambda i: (i, 0, 0, 0))],
        out_shape=[jax.ShapeDtypeStruct((NHD, T, DHD), F32), jax.ShapeDtypeStruct((NHD, T, DHD), _MXU),
                   jax.ShapeDtypeStruct((T // CH, NHD, DHD, DHD), _MXU)],
        scratch_shapes=[pltpu.VMEM((NHD, DHD, DHD), F32)],
        compiler_params=_cp(("arbitrary",)),
    )(u, wq, km, elb)


def _dn_scan_bwd(do, mq, kd, wt, sn, vn, elb):
    T = do.shape[1]
    nb = T // TG
    rev = lambda wd: pl.BlockSpec((NHD, TG, wd), lambda i: (0, nb - 1 - i, 0))
    rev_t = lambda r: pl.BlockSpec((NHD, CPT, r, CH), lambda i: (0, nb - 1 - i, 0, 0))

    def body(do_ref, mq_ref, kd_ref, wt_ref, sn_ref, vn_ref, elb_ref,
             du_ref, dw_ref, dqd_ref, dkd_ref, dgx_ref, dS):
        @pl.when(pl.program_id(0) == 0)
        def _():
            dS[...] = jnp.zeros_like(dS)

        last_row = _iota((CH, DHD), 0) == CH - 1
        sub8 = _iota((CPT, DHD), 0)
        heads = range(NHD)

        def chunk(k, carry):
            cc = CPT - 1 - k
            rs = pl.ds(pl.multiple_of(cc * CH, CH), CH)
            dsh = [dS[h] for h in heads]
            dsb = [d.astype(_MXU) for d in dsh]
            doc = [do_ref[h, rs, :].astype(_MXU) for h in heads]
            a = [_dot(mq_ref[h, cc], doc[h]) for h in heads]
            b = [_dot(kd_ref[h, rs, :], dsb[h]) for h in heads]
            dvn = [a[h][0:CH] + b[h] for h in heads]
            dvnb = [d.astype(_MXU) for d in dvn]
            e = [_dot(wt_ref[h, cc], dvnb[h]) for h in heads]
            for h in heads:
                el = jnp.sum(jnp.where(sub8 == cc, elb_ref[:, DHD * h:DHD * (h + 1)], 0.0), axis=0, keepdims=True)
                sn = sn_ref[cc, h]
                dS[h] = a[h][CH:CH + DHD] + dsh[h] * el - e[h]
                du_ref[h, rs, :] = dvn[h]
                c = _dot_nt(jnp.concatenate([doc[h], dvnb[h]], axis=0), sn)
                dqd_ref[h, rs, :] = c[0:CH]
                dw_ref[h, rs, :] = -c[CH:2 * CH]
                dkd_ref[h, rs, :] = _dot_nt(vn_ref[h, rs, :], dsb[h])
                part = jnp.sum(dsh[h] * sn.astype(F32), axis=0, keepdims=True) * el
                dgx_ref[h, rs, :] = jnp.where(last_row, part, 0.0)
            return carry

        lax.fori_loop(0, CPT, chunk, 0)

    o = jax.ShapeDtypeStruct((NHD, T, DHD), F32)
    return pl.pallas_call(
        body, name="dn_scan_bwd", grid=(nb,),
        in_specs=[rev(DHD), rev_t(CH + DHD), rev(DHD), rev_t(DHD),
                  pl.BlockSpec((CPT, NHD, DHD, DHD), lambda i: (nb - 1 - i, 0, 0, 0)), rev(DHD),
                  pl.BlockSpec((CPT, NHD * DHD), lambda i: (nb - 1 - i, 0))],
        out_specs=[rev(DHD)] * 5,
        out_shape=[o] * 5,
        scratch_shapes=[pltpu.VMEM((NHD, DHD, DHD), F32)],
        compiler_params=_cp(("arbitrary",)),
    )(do, mq, kd, wt, sn, vn, elb)


def _put_col(acc, k, col):
    return jnp.where(_iota(acc.shape, 1) == k, col, acc)


def _dn_post_bwd(draw, cv, conv_w, ba, bat, alog8, dtb8, alog8t, dtb8t, du, dw, dqd, dkd, dgx, do, vn, tm, u, w):
    T = draw.shape[0]
    nb = T // TG
    hm_spec = lambda wd: pl.BlockSpec((NHD, TG, wd), lambda i: (0, nb - 1 - i, 0))
    rrows = lambda w: pl.BlockSpec((TG, w), lambda i: (nb - 1 - i, 0))

    def body(x_ref, cv_ref, cw_ref, ba_ref, bat_ref, al_ref, dt_ref, alt_ref, dtt_ref,
             du_ref, dw_ref, dqd_ref, dkd_ref, dgx_ref, do_ref, vn_ref, tm_ref, u_ref, w_ref,
             dx_ref, dba_ref, sm_ref, dcw_ref, dc_ref, nxt_ref):
        i = pl.program_id(0)

        @pl.when(i == 0)
        def _():
            sm_ref[...] = jnp.zeros_like(sm_ref)
            dcw_ref[...] = jnp.zeros_like(dcw_ref)
            nxt_ref[...] = jnp.zeros_like(nxt_ref)

        K = _dn_masks()
        cv = cv_ref[...]
        sg = _sigmoid(cv)
        c_tile = cv * sg
        dsilu = sg * (1.0 + cv * (1.0 - sg))
        for g in range(NG):
            rows = slice(GR * g, GR * (g + 1))
            ba_g = ba_ref[rows, :]
            g8, gc8, gl8, gcrow8 = _gate_terms(ba_g, bat_ref[:, rows], al_ref[...], dt_ref[...], alt_ref[...],
                                               dtt_ref[...], K)
            beta8 = _sigmoid(ba_g)
            dgc8 = jnp.zeros((GR, 8), F32)
            rd8 = jnp.zeros((GR, 8), F32)
            dbeta8 = jnp.zeros((GR, 8), F32)
            pre = [(tm_ref[h, rows, :], u_ref[h, rows, :], w_ref[h, rows, :]) for h in range(NHD)]
            ds = _dn_heads(c_tile, rows, beta8, gc8, gl8, gcrow8, K, pre)
            H = range(NHD)
            eye_b = K["eye_b"].astype(_MXU)
            gam_t = [jnp.exp(jnp.where(K["triu"], -d["diff"], NEG)) for d in ds]
            doh = [do_ref[h, rows, :] for h in H]
            vnh = [vn_ref[h, rows, :] for h in H]
            tt = [_dot_nt(eye_b, d["tm"]) for d in ds]
            dvb = [_dot(tt[h], du_ref[h, rows, :]) for h in H]
            dkg = [_dot(tt[h], dw_ref[h, rows, :]) for h in H]
            plt = [_dot_nt(d["kn"], d["kb"]) for d in ds]
            pmt = [_dot_nt(d["kn"], d["qn"]) for d in ds]
            da = [-(_dot_nt(dvb[h], ds[h]["u"]) + _dot_nt(dkg[h], ds[h]["w"])) for h in H]
            dat = [-(_dot_nt(ds[h]["u"], dvb[h]) + _dot_nt(ds[h]["w"], dkg[h])) for h in H]
            dpm = [jnp.where(K["tril"], _dot_nt(doh[h], vnh[h]), 0.0) * ds[h]["gam_m"] for h in H]
            dpmt = [jnp.where(K["triu"], _dot_nt(vnh[h], doh[h]), 0.0) * gam_t[h] for h in H]
            dpl = [jnp.where(K["strict"], da[h], 0.0) * ds[h]["gam_m"] for h in H]
            dplt = [jnp.where(K["strict_u"], dat[h], 0.0) * gam_t[h] for h in H]
            dkb = [_dot(dpl[h], ds[h]["kn"]) + dkg[h] * ds[h]["gam"] for h in H]
            dqn = [_dot(dpm[h], ds[h]["kn"]) + dqd_ref[h, rows, :] * ds[h]["gam"] for h in H]
            dknm = [_dot(dplt[h], ds[h]["kb"]) + _dot(dpmt[h], ds[h]["qn"]) for h in H]
            for h, d in enumerate(ds):
                kn, dqdh, dkdh = d["kn"], dqd_ref[h, rows, :], dkd_ref[h, rows, :]
                dkn = dknm[h] + dkdh * d["egl"] + dkb[h] * d["beta"]
                dkd_kd = dkdh * d["kd"]
                rd = jnp.sum(dkd_kd, axis=1, keepdims=True)
                dgc = jnp.sum(dpl[h] * d["pl"] + dpm[h] * d["pm"] - dplt[h] * plt[h] - dpmt[h] * pmt[h]
                              + dqdh * d["qd"] + dkg[h] * d["kg"] - dkd_kd + dgx_ref[h, rows, :],
                              axis=1, keepdims=True)
                dgc8 = _put_col(dgc8, NHD + h, dgc)
                rd8 = _put_col(rd8, NHD + h, rd)
                dbeta = jnp.sum(dkb[h] * kn + dvb[h] * d["v"], axis=1, keepdims=True)
                dbeta8 = _put_col(dbeta8, h, dbeta)
                dqh = dqn[h] * (DHD ** -0.5)
                qh = d["qh"]
                dqr = d["rq"] * (dqh - qh * jnp.sum(dqh * qh, axis=1, keepdims=True))
                dkr = d["rk"] * (dkn - kn * jnp.sum(dkn * kn, axis=1, keepdims=True))
                cq = slice(DHD * h, DHD * (h + 1))
                ck = slice(DW + DHD * h, DW + DHD * (h + 1))
                cvv = slice(2 * DW + DHD * h, 2 * DW + DHD * (h + 1))
                dc_ref[rows, cq] = dqr * dsilu[rows, cq]
                dc_ref[rows, ck] = dkr * dsilu[rows, ck]
                dc_ref[rows, cvv] = dvb[h] * d["beta"] * dsilu[rows, cvv]
            dgc8 = dgc8 + jnp.where(K["last"], _ones_dot(K["blk_b"], rd8), 0.0)
            dg8 = _ones_dot(K["triu_b"], dgc8)
            sgm = _sigmoid(ba_g + dt_ref[...])
            dalpha = dg8 * (-jnp.exp(al_ref[...])) * sgm
            lane8 = _iota((GR, 8), 1)
            dba_ref[rows, :] = jnp.where(lane8 < NHD, dbeta8 * beta8 * (1.0 - beta8), dalpha)
            valid = lane8 >= NHD
            sm_ref[0:1, 0:8] += jnp.sum(jnp.where(valid, dg8 * g8, 0.0), axis=0, keepdims=True)
            sm_ref[1:2, 0:8] += jnp.sum(jnp.where(valid, dalpha, 0.0), axis=0, keepdims=True)

        dcv = dc_ref[...]
        xv = x_ref[...]
        nxt = nxt_ref[...]
        w = cw_ref[...]
        dx = dcv * w[CONV_K - 1:CONV_K, :]
        dcw_ref[CONV_K - 1:CONV_K, :] += jnp.sum(dcv * xv, axis=0, keepdims=True)
        for k in range(1, CONV_K):
            j = CONV_K - 1 - k
            up = _shift_up(dcv, nxt, k)
            dx = dx + up * w[j:j + 1, :]
            dcw_ref[j:j + 1, :] += jnp.sum(up * xv, axis=0, keepdims=True)
        dx_ref[...] = dx
        nxt_ref[...] = dcv[0:8]

    return pl.pallas_call(
        body, name="dn_post_bwd", grid=(nb,),
        in_specs=[rrows(1536), rrows(1536), _full((CONV_K, 1536)), rrows(8),
                  pl.BlockSpec((8, TG), lambda i: (0, nb - 1 - i)), _full((1, 8)), _full((1, 8)), _full((8, 1)),
                  _full((8, 1)),
                  hm_spec(DHD), hm_spec(DHD), hm_spec(DHD), hm_spec(DHD), hm_spec(DHD), hm_spec(DHD), hm_spec(DHD),
                  hm_spec(GR), hm_spec(DHD), hm_spec(DHD)],
        out_specs=[rrows(1536), rrows(8), _full((8, LANES)), _full((8, 1536))],
        out_shape=[jax.ShapeDtypeStruct((T, 1536), F32), jax.ShapeDtypeStruct((T, 8), F32),
                   jax.ShapeDtypeStruct((8, LANES), F32), jax.ShapeDtypeStruct((8, 1536), F32)],
        scratch_shapes=[pltpu.VMEM((TG, 1536), F32), pltpu.VMEM((8, 1536), F32)],
        compiler_params=_cp(("arbitrary",)),
    )(draw, cv, conv_w, ba, bat, alog8, dtb8, alog8t, dtb8t, du, dw, dqd, dkd, dgx, do, vn, tm, u, w)


def _rms(x):
    return lax.rsqrt(jnp.mean(x * x, axis=1, keepdims=True) + EPS)


def _rms_bwd(dy, xh, r, g):
    dxh = dy * g
    return r * (dxh - xh * jnp.mean(dxh * xh, axis=1, keepdims=True))


def _hm_rows(tm):
    return pl.BlockSpec((NHD, tm, DHD), lambda i: (0, i, 0))


def _post_mix(apre, o, z, x, w_out, g_a, g_dn):
    T = x.shape[0]

    def body(ap_ref, o_ref, z_ref, x_ref, w_ref, ga_ref, gd_ref, x1_ref, mix_ref):
        ap = ap_ref[...]
        parts = [ap * _rms(ap) * ga_ref[...]]
        zz = z_ref[...]
        for h in range(NHD):
            oh = o_ref[h]
            zh = zz[:, DHD * h:DHD * (h + 1)]
            parts.append(oh * _rms(oh) * gd_ref[...] * (zh * _sigmoid(zh)))
        mix = jnp.concatenate(parts, axis=1).astype(_MXU)
        mix_ref[...] = mix
        x1_ref[...] = x_ref[...] + jnp.dot(mix, w_ref[...], preferred_element_type=F32)

    return pl.pallas_call(
        body, name="post_mix", grid=(T // TM,),
        in_specs=[_rows(TM, AW), _hm_rows(TM), _rows(TM, DW), _rows(TM, D), _full((D, D)), _full((1, AW)),
                  _full((1, DHD))],
        out_specs=[_rows(TM, D), _rows(TM, D)],
        out_shape=[jax.ShapeDtypeStruct((T, D), F32), jax.ShapeDtypeStruct((T, D), _MXU)],
        compiler_params=_cp(("arbitrary",)),
    )(apre, o, z, x, w_out, g_a, g_dn)


def _ffn(x1, tgt, wl_all, g_ffn):
    T = x1.shape[0]
    SH = FF // N_DEV
    nt = (((1,), (1,)), ((), ()))

    def body(x_ref, t_ref, wl_hbm, g_ref,
             dx1_ref, dx1b_ref, h2_ref, act_ref, dgu_ref, dyb_ref, loss_ref, dg_ref, wg, wu, wd, sem):
        @pl.when(pl.program_id(0) == 0)
        def _():
            cps = [pltpu.make_async_copy(wl_hbm.at[dev, pl.ds(128 + SH * k, SH), :], dst.at[pl.ds(SH * dev, SH), :],
                                         sem.at[N_DEV * k + dev])
                   for k, dst in enumerate((wg, wu, wd)) for dev in range(N_DEV)]
            for cp in cps:
                cp.start()
            for cp in cps:
                cp.wait()
            loss_ref[...] = jnp.zeros_like(loss_ref)
            dg_ref[...] = jnp.zeros_like(dg_ref)

        xv = x_ref[...]
        r = _rms(xv)
        xh = xv * r
        gg = g_ref[...]
        h2 = (xh * gg).astype(_MXU)
        h2_ref[...] = h2
        gate = lax.dot_general(h2, wg[...], nt, preferred_element_type=F32)
        up = lax.dot_general(h2, wu[...], nt, preferred_element_type=F32)
        sg = _sigmoid(gate)
        silu = gate * sg
        act = (silu * up).astype(_MXU)
        act_ref[...] = act
        y = xv + jnp.dot(act, wd[...], preferred_element_type=F32)
        err = y - t_ref[...]
        loss_ref[...] += jnp.sum(err * err, axis=0, keepdims=True)
        dy = err * (1.0 / D)
        dyb = dy.astype(_MXU)
        dyb_ref[...] = dyb
        dact = lax.dot_general(dyb, wd[...], nt, preferred_element_type=F32)
        dgate = (dact * up * (sg * (1.0 + gate * (1.0 - sg)))).astype(_MXU)
        dup = (dact * silu).astype(_MXU)
        dgu_ref[:, 0:FF] = dgate
        dgu_ref[:, FF:2 * FF] = dup
        dh2 = (jnp.dot(dgate, wg[...], preferred_element_type=F32)
               + jnp.dot(dup, wu[...], preferred_element_type=F32))
        dg_ref[...] += jnp.sum(dh2 * xh, axis=0, keepdims=True)
        dx1 = dy + _rms_bwd(dh2, xh, r, gg)
        dx1_ref[...] = dx1
        dx1b_ref[...] = dx1.astype(_MXU)

    anyspec = pl.BlockSpec(memory_space=pl.ANY)
    sd = lambda w, dt: jax.ShapeDtypeStruct((T, w), dt)
    return pl.pallas_call(
        body, name="ffn", grid=(T // TF,),
        in_specs=[_rows(TF, D), _rows(TF, D), anyspec, _full((1, D))],
        out_specs=[_rows(TF, D), _rows(TF, D), _rows(TF, D), _rows(TF, FF), _rows(TF, 2 * FF), _rows(TF, D),
                   _full((1, D)), _full((1, D))],
        out_shape=[sd(D, F32), sd(D, _MXU), sd(D, _MXU), sd(FF, _MXU), sd(2 * FF, _MXU), sd(D, _MXU),
                   jax.ShapeDtypeStruct((1, D), F32), jax.ShapeDtypeStruct((1, D), F32)],
        scratch_shapes=[pltpu.VMEM((FF, D), _MXU)] * 3 + [pltpu.SemaphoreType.DMA((3 * N_DEV,))],
        compiler_params=_cp(("arbitrary",)),
    )(x1, tgt, wl_all, g_ffn)


def _mix_bwd(dx1b, w_out, apre, o, z, g_a, g_dn):
    T = dx1b.shape[0]

    def body(dx_ref, w_ref, ap_ref, o_ref, z_ref, ga_ref, gd_ref, dap_ref, do_ref, dz_ref, dga_ref, dgd_ref):
        @pl.when(pl.program_id(0) == 0)
        def _():
            dga_ref[...] = jnp.zeros_like(dga_ref)
            dgd_ref[...] = jnp.zeros_like(dgd_ref)

        dmix = lax.dot_general(dx_ref[...], w_ref[...], (((1,), (1,)), ((), ())), preferred_element_type=F32)
        ap = ap_ref[...]
        ra = _rms(ap)
        ah = ap * ra
        da = dmix[:, 0:AW]
        dga_ref[...] += jnp.sum(da * ah, axis=0, keepdims=True)
        dap_ref[...] = _rms_bwd(da, ah, ra, ga_ref[...])
        zz = z_ref[...]
        gd = gd_ref[...]
        for h in range(NHD):
            cs = slice(DHD * h, DHD * (h + 1))
            dd = dmix[:, AW + DHD * h:AW + DHD * (h + 1)]
            oh = o_ref[h]
            ro = _rms(oh)
            ohh = oh * ro
            zh = zz[:, cs]
            sz = _sigmoid(zh)
            dz_ref[:, cs] = dd * (ohh * gd) * (sz * (1.0 + zh * (1.0 - sz)))
            don = dd * (zh * sz)
            dgd_ref[...] += jnp.sum(don * ohh, axis=0, keepdims=True)
            do_ref[h] = _rms_bwd(don, ohh, ro, gd)

    return pl.pallas_call(
        body, name="mix_bwd", grid=(T // TM,),
        in_specs=[_rows(TM, D), _full((D, D)), _rows(TM, AW), _hm_rows(TM), _rows(TM, DW), _full((1, AW)),
                  _full((1, DHD))],
        out_specs=[_rows(TM, AW), _hm_rows(TM), _rows(TM, DW), _full((1, AW)), _full((1, DHD))],
        out_shape=[jax.ShapeDtypeStruct((T, AW), F32), jax.ShapeDtypeStruct((NHD, T, DHD), F32),
                   jax.ShapeDtypeStruct((T, DW), F32), jax.ShapeDtypeStruct((1, AW), F32),
                   jax.ShapeDtypeStruct((1, DHD), F32)],
        compiler_params=_cp(("arbitrary",)),
    )(dx1b, w_out, apre, o, z, g_a, g_dn)


DPW = 3712


def _inproj_bwd(dqn, dkn, dv, araw, ddraw, dz, dba, x, dx1, w_int, w_ba, g_mix, qg_t, kg_t):
    T = x.shape[0]

    def body(dqn_ref, dkn_ref, dv_ref, ar_ref, dd_ref, dz_ref, dba_ref, x_ref, dx1_ref, w_hbm, wba_ref, g_ref, qg_ref,
             kg_ref, dx_ref, dp_ref, dgm_ref, dqg_ref, dkg_ref, w_ref, w_sem):
        @pl.when(pl.program_id(0) == 0)
        def _():
            cp = pltpu.make_async_copy(w_hbm, w_ref, w_sem)
            cp.start()
            cp.wait()
            dgm_ref[...] = jnp.zeros_like(dgm_ref)
            dqg_ref[...] = jnp.zeros_like(dqg_ref)
            dkg_ref[...] = jnp.zeros_like(dkg_ref)

        bd = _block_ones(AW // 2, DHA)

        def head_norm_bwd(raw, dyn, gain, dg_ref):
            r = _head_rms(raw, bd, DHA)
            xh = raw * r
            dg_ref[...] += jnp.sum(dyn * xh, axis=0, keepdims=True)
            dxh = dyn * gain
            return r * (dxh - xh * (_head_sum(dxh * xh, bd) * (1.0 / DHA)))

        def segment(lo, val):
            vb = val.astype(_MXU)
            dp_ref[:, lo:lo + val.shape[1]] = vb
            return jnp.dot(vb, w_ref[lo:lo + val.shape[1], :], preferred_element_type=F32)

        dh = segment(1536, dd_ref[...]) + segment(2 * AW, dv_ref[...]) + segment(3072, dz_ref[...])
        dbab = dba_ref[...].astype(_MXU)
        dp_ref[:, 3584:DPW] = jnp.zeros((TM, DPW - 3584), _MXU)
        dp_ref[:, 3584:3592] = dbab
        dh = dh + lax.dot_general(dbab, wba_ref[...], (((1,), (1,)), ((), ())), preferred_element_type=F32)
        ar = ar_ref[...]
        dq = head_norm_bwd(ar[:, 0:AW], dqn_ref[...] * (DHA ** -0.5), qg_ref[...], dqg_ref)
        dk = head_norm_bwd(ar[:, AW:2 * AW], dkn_ref[...], kg_ref[...], dkg_ref)
        dh = dh + segment(0, dq) + segment(AW, dk)
        xv = x_ref[...]
        r = _rms(xv)
        xh = xv * r
        dgm_ref[...] += jnp.sum(dh * xh, axis=0, keepdims=True)
        dx_ref[...] = dx1_ref[...] + _rms_bwd(dh, xh, r, g_ref[...])

    return pl.pallas_call(
        body, name="inproj_bwd", grid=(T // TM,),
        in_specs=[_rows(TM, AW), _rows(TM, AW), _rows(TM, AW), _rows(TM, 1536), _rows(TM, 1536), _rows(TM, DW),
                  _rows(TM, 8), _rows(TM, D), _rows(TM, D), pl.BlockSpec(memory_space=pl.ANY), _full((D, 8)),
                  _full((1, D)), _full((1, AW)), _full((1, AW))],
        out_specs=[_rows(TM, D), _rows(TM, DPW), _full((1, D)), _full((1, AW)), _full((1, AW))],
        out_shape=[jax.ShapeDtypeStruct((T, D), F32), jax.ShapeDtypeStruct((T, DPW), _MXU),
                   jax.ShapeDtypeStruct((1, D), F32), jax.ShapeDtypeStruct((1, AW), F32),
                   jax.ShapeDtypeStruct((1, AW), F32)],
        scratch_shapes=[pltpu.VMEM((3584, D), _MXU), pltpu.SemaphoreType.DMA],
        compiler_params=_cp(("arbitrary",)),
    )(dqn, dkn, dv, araw, ddraw, dz, dba, x, dx1, w_int, w_ba, g_mix, qg_t, kg_t)


def _wgrad(a, b, name, tk=1024, tn=None, out_dtype=F32, transposed=False):
    T, M = a.shape
    N = b.shape[1]
    tn = N if tn is None else tn
    assert T % tk == 0 and N % tn == 0, (T, tk, N, tn)
    nk = T // tk

    def body(a_ref, b_ref, o_ref, acc):
        k = pl.program_id(1)

        @pl.when(k == 0)
        def _():
            acc[...] = jnp.zeros_like(acc)

        acc[...] += lax.dot_general(a_ref[...], b_ref[...], (((0,), (0,)), ((), ())), preferred_element_type=F32)

        @pl.when(k == nk - 1)
        def _():
            r = acc[...]
            o_ref[...] = (r.T if transposed else r).astype(out_dtype)

    if transposed:
        out_spec, out_shape = pl.BlockSpec((tn, M), lambda j, k: (j, 0)), (N, M)
    else:
        out_spec, out_shape = pl.BlockSpec((M, tn), lambda j, k: (0, j)), (M, N)
    return pl.pallas_call(
        body, name=name, grid=(N // tn, nk),
        in_specs=[pl.BlockSpec((tk, M), lambda j, k: (k, 0)), pl.BlockSpec((tk, tn), lambda j, k: (k, j))],
        out_specs=out_spec,
        out_shape=jax.ShapeDtypeStruct(out_shape, out_dtype),
        scratch_shapes=[pltpu.VMEM((M, tn), F32)],
        compiler_params=_cp(("arbitrary", "arbitrary")),
    )(a, b)


def _adamw(parts, w, m, v, name, tr, send=None):
    K, R, W = parts.shape
    n = R // tr

    def body(p_ref, w_ref, m_ref, v_ref, *rest):
        if send is not None:
            send_ref, g_ref, d_ref, nm_ref, nv_ref, recv_ref, send_sems, recv_sems, local_sem = rest
            sc = _Scatter(send_ref, recv_ref, send_sems, recv_sems, local_sem)
            pl.when(pl.program_id(0) == 0)(sc.start)
        else:
            g_ref, d_ref, nm_ref, nv_ref = rest
        g = p_ref[0].astype(F32)
        for k in range(1, K):
            g = g + p_ref[k].astype(F32)
        g_ref[...] = g
        nm = ADAM_B1 * m_ref[...] + (1.0 - ADAM_B1) * g
        nv = ADAM_B2 * v_ref[...] + (1.0 - ADAM_B2) * (g * g)
        nm_ref[...] = nm
        nv_ref[...] = nv
        m_hat = nm / (1.0 - ADAM_B1 ** ADAM_STEP)
        v_hat = nv / (1.0 - ADAM_B2 ** ADAM_STEP)
        d_ref[...] = -ADAM_LR * (m_hat / (jnp.sqrt(v_hat) + ADAM_EPS) + ADAM_WD * w_ref[...])
        if send is not None:
            pl.when(pl.program_id(0) == n - 1)(sc.finish)

    o = jax.ShapeDtypeStruct((R, W), F32)
    anyspec = pl.BlockSpec(memory_space=pl.ANY)
    hosted = send is not None
    return pl.pallas_call(
        body, name=name, grid=(n,),
        in_specs=[pl.BlockSpec((K, tr, W), lambda i: (0, i, 0)), _rows(tr, W), _rows(tr, W), _rows(tr, W)]
        + ([anyspec] if hosted else []),
        out_specs=[_rows(tr, W)] * 4 + ([anyspec] if hosted else []),
        out_shape=[o] * 4 + ([jax.ShapeDtypeStruct(send.shape, send.dtype)] if hosted else []),
        scratch_shapes=_COMM_SEMS if hosted else [],
        compiler_params=_cp(("arbitrary",)),
    )(*((parts, w, m, v) + ((send,) if hosted else ())))


SM_ROWS = 136
R_GMIX, R_GFFN, R_QG, R_KG, R_GA, R_GDN, R_ALOG, R_DT, R_LOSS, R_CONV, R_REL = 0, 8, 16, 24, 32, 40, 48, 49, 56, 64, 112


def _small_reduce(gathered):
    def body(p_ref, o_ref):
        s = p_ref[0]
        for k in range(1, N_DEV):
            s = s + p_ref[k]
        o_ref[...] = s
        for r0 in (R_QG, R_KG):
            rs = jnp.sum(s[r0:r0 + 4], axis=0, keepdims=True)
            o_ref[r0:r0 + 1, :] = rs + pltpu.roll(rs, DHA, 1)
        tot = jnp.sum(jnp.sum(s[R_LOSS:R_LOSS + 8], axis=0, keepdims=True), axis=1, keepdims=True)
        o_ref[R_LOSS:R_LOSS + 1, :] = jnp.broadcast_to(tot * (0.5 / D), (1, LANES))

    return pl.pallas_call(
        body, name="small_reduce",
        out_shape=jax.ShapeDtypeStruct((SM_ROWS, LANES), F32),
    )(gathered)


_WIRE = jnp.bfloat16
RA_USED, RA = 449, 464
RL = 128 + 3 * 352


def _pack_rows(parts, rows=None):
    p = jnp.concatenate([t.reshape(-1, D) for t in parts], axis=0) if len(parts) > 1 else parts[0].reshape(-1, D)
    return p if rows is None else jnp.pad(p, ((0, rows - p.shape[0]), (0, 0)))


def _unpack_rows(packed, shapes):
    out, r = [], 0
    for shp in shapes:
        nr = math.prod(shp) // D
        out.append(packed[r:r + nr].reshape(shp))
        r += nr
    return out


def _pad8(t):
    return jnp.pad(t, ((0, (-t.shape[0]) % 8), (0, 0)))


def _pack_lanes(parts):
    rows = []
    for p in parts:
        f = p.reshape(-1)
        pad = (-f.shape[0]) % LANES
        rows.append(jnp.pad(f, (0, pad)).reshape(-1, LANES))
    return jnp.concatenate(rows, axis=0)


def _unpack_lanes(packed, shapes):
    out, r = [], 0
    for shp in shapes:
        n = math.prod(shp)
        nr = -(-n // LANES)
        out.append(packed[r:r + nr].reshape(-1)[:n].reshape(shp))
        r += nr
    return out


def kernel(x, norm_mix_g, w_in, attn_q_norm_g, attn_k_norm_g, rel_bias, attn_out_norm_g, conv_w, a_log, dt_bias, dn_out_norm_g, w_out, norm_ffn_g, w_gate, w_up, w_down, loss_target, m_norm_mix_g, m_w_in, m_attn_q_norm_g, m_attn_k_norm_g, m_rel_bias, m_attn_out_norm_g, m_conv_w, m_a_log, m_dt_bias, m_dn_out_norm_g, m_w_out, m_norm_ffn_g, m_w_gate, m_w_up, m_w_down, v_norm_mix_g, v_w_in, v_attn_q_norm_g, v_attn_k_norm_g, v_rel_bias, v_attn_out_norm_g, v_conv_w, v_a_log, v_dt_bias, v_dn_out_norm_g, v_w_out, v_norm_ffn_g, v_w_gate, v_w_up, v_w_down):
    xs, tgt = x[0], loss_target[0]
    T = xs.shape[0]
    my_idx = 4 * lax.axis_index("x") + 2 * lax.axis_index("y") + lax.axis_index("c")
    late_w = (w_out[0], w_gate[0], w_up[0], w_down[0])
    late_shapes = [w.shape for w in late_w]

    wa_all = _all_gather(_pack_rows([w_in[0].T.astype(_MXU)], RA), "gather_w_in")
    cw_all = _all_gather(jnp.pad(conv_w[0], ((0, 4), (0, 64))), "gather_conv")
    W_in_t = wa_all[:, 0:RA_USED].reshape(N_DEV * RA_USED, D)
    W_int, W_ba = W_in_t[0:3584], W_in_t[3584:3592].T
    conv_full = cw_all[:, 0:CONV_K, 0:192].transpose(1, 0, 2).reshape(CONV_K, 1536)

    qg_t = jnp.tile(attn_q_norm_g, (1, NHA))
    kg_t = jnp.tile(attn_k_norm_g, (1, NHA))
    z4 = jnp.zeros((1, NHD), F32)
    alog8 = jnp.concatenate([z4, a_log], axis=1)
    dtb8 = jnp.concatenate([z4, dt_bias], axis=1)

    late_t = lambda ts: (ts[0], ts[1].T, ts[2].T, ts[3])
    araw, an, draw, z, ba, hb, wl_all = _inproj(xs, norm_mix_g, W_int, W_ba, qg_t, kg_t,
                                                _pack_rows([w.astype(_MXU) for w in late_t(late_w)]))
    W_out = wl_all[:, 0:128].reshape(D, D)
    tab, tabt = _bias_tables(jnp.pad(rel_bias[0].T, ((0, 0), (0, VAR0 - 257))))
    apre = _attn_fwd(an, tab)
    bat = ba.T
    dn_args = (draw, conv_full, ba, bat, alog8, dtb8, alog8.T, dtb8.T)
    u, w, kd, tm, wq, km, mq, wt, elb, cv = _dn_prep(*dn_args)
    o, vn, sn = _dn_scan(u, wq, km, elb)
    x1, mix = _post_mix(apre, o, z, xs, W_out, attn_out_norm_g, dn_out_norm_g)

    dx1, dx1b, h2, act, dgu, dyb, loss_row, dgffn = _ffn(x1, tgt, wl_all, norm_ffn_g)

    by_cols = lambda g, k: g.reshape(D, N_DEV, k).transpose(1, 0, 2).reshape(N_DEV, -1, D)
    gW_out = _wgrad(mix, dx1b, "wgrad_out", out_dtype=_WIRE)
    gW_gu_t = _wgrad(h2, dgu, "wgrad_gate_up", tn=FF, out_dtype=_WIRE, transposed=True)
    gW_down = _wgrad(dyb, act, "wgrad_down", out_dtype=_WIRE, transposed=True)
    send_late = jnp.concatenate(
        [gW_out.reshape(N_DEV, 128, D), gW_gu_t[0:FF].reshape(N_DEV, 352, D), gW_gu_t[FF:].reshape(N_DEV, 352, D),
         gW_down.reshape(N_DEV, 352, D)], axis=1)

    dap, do, dz, dga, dgdn = _mix_bwd(dx1b, W_out, apre, o, z, attn_out_norm_g, dn_out_norm_g)
    dqn, dkn, dv, dtabt, recv_late = _attn_bwd(an, dap, tabt, send_late)
    drel = _bias_grad(dtabt)
    du, dw, dqd, dkd, dgx = _dn_scan_bwd(do, mq, kd, wt, sn, vn, elb)
    ddraw, dba, sm, dcw = _dn_post_bwd(draw, cv, *dn_args[1:], du, dw, dqd, dkd, dgx, do, vn, tm, u, w)
    gx, dproj, dgmix, dqg, dkg = _inproj_bwd(dqn, dkn, dv, araw, ddraw, dz, dba, xs, dx1, W_int, W_ba, norm_mix_g,
                                             qg_t, kg_t)

    gW_in_t = _wgrad(hb, dproj, "wgrad_in", tk=512, out_dtype=_WIRE, transposed=True)
    send_in = jnp.pad(gW_in_t[0:N_DEV * RA_USED].reshape(N_DEV, RA_USED, D), ((0, 0), (0, RA - RA_USED), (0, 0)))
    late_m = (m_w_out[0], m_w_gate[0], m_w_up[0], m_w_down[0])
    late_v = (v_w_out[0], v_w_gate[0], v_w_up[0], v_w_down[0])
    *outs_late, recv_in = _adamw(recv_late, _pack_rows(late_t(late_w)), _pack_rows(late_t(late_m)),
                                 _pack_rows(late_t(late_v)), "adamw_late", 32, send=send_in)
    outs_in = _adamw(recv_in, _pack_rows([w_in[0].T], RA), _pack_rows([m_w_in[0].T], RA),
                     _pack_rows([v_w_in[0].T], RA), "adamw_w_in", 16)
    late_t_shapes = [t.shape for t in late_t(late_w)]
    big = [[a[0:RA_USED].T] + list(late_t(_unpack_rows(b, late_t_shapes))) for a, b in zip(outs_in, outs_late)]
    bg, bd_, bm, bv = big

    partial = jnp.concatenate(
        [dgmix.reshape(8, LANES), dgffn.reshape(8, LANES), _pad8(dqg.reshape(4, LANES)), _pad8(dkg.reshape(4, LANES)),
         _pad8(dga.reshape(4, LANES)), _pad8(dgdn), sm, loss_row.reshape(8, LANES),
         dcw[0:CONV_K].reshape(48, LANES), drel.reshape(24, LANES)], axis=0)
    S = _small_reduce(_all_gather(partial, "gather_small"))
    loss = S[R_LOSS, 0]
    g_conv = lax.dynamic_slice(S[R_CONV:R_CONV + 48].reshape(CONV_K, 1536), (0, 192 * my_idx), (CONV_K, 192))
    sg = [S[R_GMIX:R_GMIX + 8].reshape(1, D), S[R_QG:R_QG + 1, 0:DHA], S[R_KG:R_KG + 1, 0:DHA],
          S[R_REL:R_REL + 24].reshape(NHA, 384)[:, 0:257].T, S[R_GA:R_GA + 4].reshape(1, AW), g_conv,
          S[R_ALOG:R_ALOG + 1, NHD:2 * NHD], S[R_DT:R_DT + 1, NHD:2 * NHD], S[R_GDN:R_GDN + 1], S[R_GFFN:R_GFFN + 8].reshape(1, D)]
    sw = [norm_mix_g, attn_q_norm_g, attn_k_norm_g, rel_bias[0], attn_out_norm_g, conv_w[0], a_log, dt_bias, dn_out_norm_g, norm_ffn_g]
    smm = [m_norm_mix_g, m_attn_q_norm_g, m_attn_k_norm_g, m_rel_bias[0], m_attn_out_norm_g, m_conv_w[0], m_a_log, m_dt_bias, m_dn_out_norm_g, m_norm_ffn_g]
    svv = [v_norm_mix_g, v_attn_q_norm_g, v_attn_k_norm_g, v_rel_bias[0], v_attn_out_norm_g, v_conv_w[0], v_a_log, v_dt_bias, v_dn_out_norm_g, v_norm_ffn_g]
    s_shapes = [t.shape for t in sw]
    pk = lambda ts: _pack_lanes(ts)
    pg = pk(sg)
    padr = (-pg.shape[0]) % 8
    padz = lambda t: jnp.pad(t, ((0, padr), (0, 0)))
    s_out = _adamw(padz(pg)[None], padz(pk(sw)), padz(pk(smm)), padz(pk(svv)), "adamw_small", pg.shape[0] + padr)
    s_g, s_d, s_m, s_v = (_unpack_lanes(t, s_shapes) for t in s_out)

    lead = lambda t: t[None]
    def ordered(small, big):
        nm, q, k, rel, ao, cw, al, dtb, dno, nf = small
        wi, wo, wgt, wu, wdn = big
        return [nm, lead(wi), q, k, lead(rel), ao, lead(cw), al, dtb, dno, lead(wo), nf, lead(wgt), lead(wu), lead(wdn)]
    outs = [loss, gx[None]]
    for small, big in ((s_g, bg), (s_d, bd_), (s_m, bm), (s_v, bv)):
        outs += ordered(small, big)
    return tuple(outs)
```

```python
import math

import jax
import jax.numpy as jnp
from jax import lax
from jax.experimental import pallas as pl
from jax.experimental.pallas import tpu as pltpu

F32 = jnp.float32
BF16 = jnp.bfloat16
_MXU = jnp.bfloat16

D = 1024
AW = 512
NHA = 8
DHA = 64
CH = 64
NHD = 4
DHD = 128
DW = 512
FF = 2816
EPS = 1e-6
NEG = -1e30
N_DEV = 8
LANES = 128
VMEM_LIMIT = 56 * 1024 * 1024

ADAM_LR = 0.001
ADAM_B1 = 0.9
ADAM_B2 = 0.999
ADAM_EPS = 1e-08
ADAM_WD = 0.01
ADAM_STEP = 10

MESH_T = pl.DeviceIdType.MESH


def _cp(sem=None, vmem=VMEM_LIMIT):
    kw = dict(vmem_limit_bytes=vmem)
    if sem is not None:
        kw["dimension_semantics"] = sem
    return pltpu.CompilerParams(**kw)


def _dot(a, b):
    return jnp.dot(a.astype(_MXU), b.astype(_MXU), preferred_element_type=F32)


def _dot_nt(a, b):
    return lax.dot_general(a.astype(_MXU), b.astype(_MXU), (((1,), (1,)), ((), ())), preferred_element_type=F32)


def _dot_tn(a, b):
    return lax.dot_general(a.astype(_MXU), b.astype(_MXU), (((0,), (0,)), ((), ())), preferred_element_type=F32)


def _iota(shape, dim):
    return lax.broadcasted_iota(jnp.int32, shape, dim)


def _block_ones(n, blk, dtype=BF16):
    r, c = _iota((n, n), 0), _iota((n, n), 1)
    return jnp.where((r // blk) == (c // blk), 1.0, 0.0).astype(dtype)


def _sigmoid(x):
    return 1.0 / (1.0 + jnp.exp(-x))


def _softplus(x):
    return jnp.maximum(x, 0.0) + jnp.log(1.0 + jnp.exp(-jnp.abs(x)))


def _col(x, k):
    lane = _iota(x.shape, 1)
    return jnp.sum(jnp.where(lane == k, x, 0.0), axis=1, keepdims=True)


def _row(x, k):
    sub = _iota(x.shape, 0)
    return jnp.sum(jnp.where(sub == k, x, 0.0), axis=0, keepdims=True)


def _my_pos():
    return lax.axis_index("x"), lax.axis_index("y"), lax.axis_index("c")


def _all_gather(x2d, name):
    R, W = x2d.shape

    def body(x_ref, out_ref, send_sems, recv_sems, local_sem):
        ag = _Gather(x_ref, out_ref, send_sems, recv_sems, local_sem)
        ag.start()
        ag.forward()
        ag.finish()

    return pl.pallas_call(
        body, name=name,
        out_shape=jax.ShapeDtypeStruct((N_DEV, R, W), x2d.dtype),
        in_specs=[pl.BlockSpec(memory_space=pl.ANY)],
        out_specs=pl.BlockSpec(memory_space=pl.ANY),
        scratch_shapes=_COMM_SEMS,
    )(x2d)


_COMM_SEMS = [pltpu.SemaphoreType.DMA((7,)), pltpu.SemaphoreType.DMA((7,)), pltpu.SemaphoreType.DMA]


class _Gather:
    def __init__(self, x_ref, out_ref, send_sems, recv_sems, local_sem):
        x, y, c = _my_pos()
        me, sibling = (x, y, c), (x, y, 1 - c)
        chips = [(1 - x, y), (x, 1 - y), (1 - x, 1 - y)]

        def slot(px, py, pc):
            return out_ref.at[4 * px + 2 * py + pc]

        def copy(k, block, to, src=None):
            return pltpu.make_async_remote_copy(
                src_ref=slot(*block) if src is None else src, dst_ref=slot(*block),
                send_sem=send_sems.at[k], recv_sem=recv_sems.at[k], device_id=to, device_id_type=MESH_T)

        self.mine = pltpu.make_async_copy(x_ref, slot(*me), local_sem)
        self.first = [copy(0, me, sibling, src=x_ref)]
        self.first += [copy(1 + j, me, (*chip, c), src=x_ref) for j, chip in enumerate(chips)]
        self.passed = [copy(4 + j, (*chip, c), sibling) for j, chip in enumerate(chips)]
        self.from_chips = [copy(1 + j, (*chip, c), me) for j, chip in enumerate(chips)]
        self.from_sibling = [copy(0, sibling, me)] + [copy(4 + j, (*chip, 1 - c), me) for j, chip in enumerate(chips)]

    def start(self):
        self.mine.start()
        for cp in self.first:
            cp.start()

    def forward(self):
        for arrived, onward in zip(self.from_chips, self.passed):
            arrived.wait_recv()
            onward.start()

    def finish(self):
        for cp in self.from_sibling:
            cp.wait_recv()
        for cp in self.first + self.passed:
            cp.wait_send()
        self.mine.wait()


class _Scatter:
    def __init__(self, s_ref, r_ref, send_sems, recv_sems, local_sem):
        x, y, c = _my_pos()
        self.mine = pltpu.make_async_copy(s_ref.at[4 * x + 2 * y + c], r_ref.at[0], local_sem)
        self.copies = []
        for m in range(1, N_DEV):
            px = x ^ ((m >> 2) & 1)
            py = y ^ ((m >> 1) & 1)
            pc = c ^ (m & 1)
            self.copies.append(pltpu.make_async_remote_copy(
                src_ref=s_ref.at[4 * px + 2 * py + pc], dst_ref=r_ref.at[m],
                send_sem=send_sems.at[m - 1], recv_sem=recv_sems.at[m - 1],
                device_id=(px, py, pc), device_id_type=MESH_T))

    def start(self):
        self.mine.start()
        for cp in self.copies:
            cp.start()

    def finish(self):
        for cp in self.copies:
            cp.wait_recv()
        for cp in self.copies:
            cp.wait_send()
        self.mine.wait()


TM = 512
TF = 256
TG = 512


def _full(shape):
    nd = len(shape)
    return pl.BlockSpec(shape, lambda i: (0,) * nd)


def _rows(tm, w):
    return pl.BlockSpec((tm, w), lambda i: (i, 0))


def _head_sum(x, bd):
    one_pass = lambda t: jnp.dot(t.astype(_MXU), bd.astype(_MXU), preferred_element_type=F32)
    return jnp.concatenate([one_pass(x[:, 0:256]), one_pass(x[:, 256:512])], axis=1)


def _head_rms(x, bd, width):
    return lax.rsqrt(_head_sum(x * x, bd) * (1.0 / width) + EPS)


def _inproj(x, g_mix, w_int, w_ba, qg_t, kg_t, later_w):
    T = x.shape[0]
    nt = T // TM
    ntd = (((1,), (1,)), ((), ()))

    def body(x_ref, g_ref, w_ref, wba_ref, qg_ref, kg_ref, lw_ref, araw_ref, an_ref, draw_ref, z_ref, ba_ref, h_ref,
             lw_all, send_sems, recv_sems, local_sem):
        i = pl.program_id(0)
        ag = _Gather(lw_ref, lw_all, send_sems, recv_sems, local_sem)
        pl.when(i == 0)(ag.start)
        pl.when(i == nt // 2)(ag.forward)
        xv = x_ref[...]
        r = lax.rsqrt(jnp.mean(xv * xv, axis=1, keepdims=True) + EPS)
        h = (xv * r * g_ref[...]).astype(_MXU)
        h_ref[...] = h
        proj = lambda lo, hi: lax.dot_general(h, w_ref[lo:hi, :], ntd, preferred_element_type=F32)
        q, k, v = proj(0, AW), proj(AW, 2 * AW), proj(2 * AW, 3 * AW)
        draw_ref[...] = proj(1536, 3072)
        z_ref[...] = proj(3072, 3584)
        ba_ref[...] = jnp.dot(h, wba_ref[...], preferred_element_type=F32)
        araw_ref[:, 0:AW] = q
        araw_ref[:, AW:2 * AW] = k
        araw_ref[:, 2 * AW:3 * AW] = v
        bd = _block_ones(AW // 2, DHA)
        qn = q * _head_rms(q, bd, DHA) * (qg_ref[...] * (DHA ** -0.5))
        kn = k * _head_rms(k, bd, DHA) * kg_ref[...]
        an_ref[:, 0:AW] = qn.astype(_MXU)
        an_ref[:, AW:2 * AW] = kn.astype(_MXU)
        an_ref[:, 2 * AW:3 * AW] = v.astype(_MXU)
        pl.when(i == nt - 1)(ag.finish)

    anyspec = pl.BlockSpec(memory_space=pl.ANY)
    return pl.pallas_call(
        body, name="inproj", grid=(nt,),
        in_specs=[_rows(TM, D), _full((1, D)), _full((3584, D)), _full((D, 8)), _full((1, AW)), _full((1, AW)),
                  anyspec],
        out_specs=[_rows(TM, 1536), _rows(TM, 1536), _rows(TM, 1536), _rows(TM, DW), _rows(TM, 8), _rows(TM, D),
                   anyspec],
        out_shape=[jax.ShapeDtypeStruct((T, 1536), F32), jax.ShapeDtypeStruct((T, 1536), _MXU),
                   jax.ShapeDtypeStruct((T, 1536), F32), jax.ShapeDtypeStruct((T, DW), F32),
                   jax.ShapeDtypeStruct((T, 8), F32), jax.ShapeDtypeStruct((T, D), _MXU),
                   jax.ShapeDtypeStruct((N_DEV,) + later_w.shape, later_w.dtype)],
        scratch_shapes=_COMM_SEMS,
        compiler_params=_cp(("arbitrary",)),
    )(x, g_mix, w_int, w_ba, qg_t, kg_t, later_w)


TQ = 256
TW = 768
VAR0 = 384
TOEP = 1024


def _bias_tables(rb_t):
    def body(rb_ref, tab_ref, tabt_ref):
        h = pl.program_id(0)
        rb8 = jnp.broadcast_to(_row(rb_ref[...], h), (8, VAR0))
        n = _iota((VAR0, TOEP), 1)
        t = _iota((VAR0, TOEP), 0)

        def line(m):
            onehot = jnp.where(jnp.clip(512 - m, -128, 128) + 128 == t, 1.0, 0.0).astype(BF16)
            return sum(jnp.dot(p, onehot, preferred_element_type=F32) for p in _split3(rb8))[0:1, :]

        def band(r, j, first_key):
            return ((j >> 6) >= (r >> 6)) & ((j >> 6) <= (r >> 6) + 8) & (j >= first_key)

        g = line(jnp.where(n < TW, n, n - TOEP))
        tab = pltpu.roll(jnp.broadcast_to(g, (TQ, TOEP)), 0, 1, stride=1, stride_axis=0)[:, 0:TW]
        gt = line(jnp.where(n < TQ, -n, TOEP - n))
        tabt = pltpu.roll(jnp.broadcast_to(gt, (TW, TOEP)), 0, 1, stride=1, stride_axis=0)[:, 0:TQ]
        for v in range(3):
            first_key = max(512 - TQ * v, 0)
            tab_ref[v, 0] = jnp.where(band(_iota((TQ, TW), 0), _iota((TQ, TW), 1), first_key), tab, NEG)
            tabt_ref[v, 0] = jnp.where(band(_iota((TW, TQ), 1), _iota((TW, TQ), 0), first_key), tabt, NEG)

    return pl.pallas_call(
        body, name="bias_tables", grid=(NHA,),
        in_specs=[_full((NHA, VAR0))],
        out_specs=[pl.BlockSpec((3, 1, TQ, TW), lambda h: (0, h, 0, 0)),
                   pl.BlockSpec((3, 1, TW, TQ), lambda h: (0, h, 0, 0))],
        out_shape=[jax.ShapeDtypeStruct((3, NHA, TQ, TW), F32), jax.ShapeDtypeStruct((3, NHA, TW, TQ), F32)],
        compiler_params=_cp(("arbitrary",)),
    )(rb_t)


def _bias_grad(dtabt):
    def body(d_ref, o_ref):
        a, b = _iota((TQ, TQ), 0), _iota((TQ, TQ), 1)
        anti = jnp.where(a + b == TQ - 1, 1.0, 0.0).astype(BF16)
        drev = sum(jnp.dot(t, anti, preferred_element_type=F32) for t in _split3(d_ref[0]))
        wide = jnp.concatenate([drev, jnp.zeros((TW, TOEP - TQ), F32)], axis=1)
        cols = jnp.sum(pltpu.roll(wide, 0, 1, stride=1, stride_axis=0), axis=0, keepdims=True)
        c = _iota((TOEP, VAR0), 0)
        idx = jnp.clip(512 + TQ - 1 - c, -128, 128) + 128
        onehot = jnp.where(idx == _iota((TOEP, VAR0), 1), 1.0, 0.0).astype(BF16)
        cols8 = jnp.broadcast_to(cols, (8, TOEP))
        o_ref[0] = sum(jnp.dot(t, onehot, preferred_element_type=F32) for t in _split3(cols8))[0:1, :]

    return pl.pallas_call(
        body, name="bias_grad", grid=(NHA,),
        in_specs=[pl.BlockSpec((1, TW, TQ), lambda h: (h, 0, 0))],
        out_specs=pl.BlockSpec((1, 1, VAR0), lambda h: (h, 0, 0)),
        out_shape=jax.ShapeDtypeStruct((NHA, 1, VAR0), F32),
        compiler_params=_cp(("arbitrary",)),
    )(dtabt)


def _kv_spec(col, back):
    return pl.BlockSpec((TQ, AW), lambda i: (jnp.maximum(i - back, 0), col))


def _attn_fwd(an, tab):
    T = an.shape[0]

    def body(q_ref, k2_ref, k1_ref, k0_ref, v2_ref, v1_ref, v0_ref, tab_ref, o_ref):
        i = pl.program_id(0)
        kwin = jnp.concatenate([k2_ref[...], k1_ref[...], k0_ref[...]], axis=0)
        vwin = jnp.concatenate([v2_ref[...], v1_ref[...], v0_ref[...]], axis=0)
        q = q_ref[...]
        lo_half = _iota((TQ, LANES), 1) < DHA

        def scores(h):
            sl = slice(LANES * (h // 2), LANES * (h // 2 + 1))
            mask = lo_half if h % 2 == 0 else jnp.logical_not(lo_half)
            qm = jnp.where(mask, q[:, sl], jnp.zeros((TQ, LANES), q.dtype))
            return _dot_nt(qm, kwin[:, sl]) + tab_ref[0, h]

        s_next = scores(0)
        outs = []
        for h in range(NHA):
            s = s_next
            if h + 1 < NHA:
                s_next = scores(h + 1)
            sl = slice(LANES * (h // 2), LANES * (h // 2 + 1))
            m = jnp.max(s, axis=1, keepdims=True)
            e = jnp.exp(s - m)
            l = jnp.sum(e, axis=1, keepdims=True)
            outs.append(_dot(e, vwin[:, sl]) / l)
            if h % 2 == 1:
                o_ref[:, sl] = jnp.where(lo_half, outs[h - 1], outs[h])

    return pl.pallas_call(
        body, name="attn_fwd", grid=(T // TQ,),
        in_specs=[pl.BlockSpec((TQ, AW), lambda i: (i, 0)),
                  _kv_spec(1, 2), _kv_spec(1, 1), _kv_spec(1, 0), _kv_spec(2, 2), _kv_spec(2, 1), _kv_spec(2, 0),
                  pl.BlockSpec((1, NHA, TQ, TW), lambda i: (jnp.minimum(i, 2), 0, 0, 0))],
        out_specs=_rows(TQ, AW),
        out_shape=jax.ShapeDtypeStruct((T, AW), F32),
        compiler_params=_cp(("arbitrary",)),
    )(an, an, an, an, an, an, an, tab)


def _attn_bwd(an, dout, tabt, send):
    T = an.shape[0]
    nq = T // TQ

    def qi(i):
        return jnp.minimum(i, nq - 1)

    def kv_spec(col, back):
        return pl.BlockSpec((TQ, AW), lambda i: (jnp.maximum(qi(i) - back, 0), col))

    def body(q_ref, do_ref, k2_ref, k1_ref, k0_ref, v2_ref, v1_ref, v0_ref, tabt_ref, send_ref,
             dq_ref, dk_ref, dv_ref, dtab_ref, recv_ref, dk_acc, dv_acc, send_sems, recv_sems, local_sem):
        i = pl.program_id(0)
        sc = _Scatter(send_ref, recv_ref, send_sems, recv_sems, local_sem)
        pl.when(i == 0)(sc.start)

        @pl.when(i == 0)
        def _():
            dtab_ref[...] = jnp.zeros_like(dtab_ref)

        new = i % 3
        dk_acc[new] = jnp.zeros((TQ, AW), F32)
        dv_acc[new] = jnp.zeros((TQ, AW), F32)

        @pl.when(i < nq)
        def _():
            kwin = jnp.concatenate([k2_ref[...], k1_ref[...], k0_ref[...]], axis=0)
            vwin = jnp.concatenate([v2_ref[...], v1_ref[...], v0_ref[...]], axis=0)
            q = q_ref[...]
            do = do_ref[...].astype(_MXU)
            lo_half = _iota((TQ, LANES), 1) < DHA

            def front(h):
                sl = slice(LANES * (h // 2), LANES * (h // 2 + 1))
                mask = lo_half if h % 2 == 0 else jnp.logical_not(lo_half)
                zero = jnp.zeros((TQ, LANES), q.dtype)
                qm = jnp.where(mask, q[:, sl], zero)
                dom = jnp.where(mask, do[:, sl], zero)
                st = _dot_nt(kwin[:, sl], qm) + tabt_ref[0, h]
                return st, _dot_nt(vwin[:, sl], dom), qm, dom, mask

            pairs = {}

            def back(h, ptb, dsb, qm, dom, mask):
                sl = slice(LANES * (h // 2), LANES * (h // 2 + 1))
                dv = _dot(ptb, dom)
                dk = _dot(dsb, qm)
                dq = jnp.where(mask, _dot_tn(dsb, kwin[:, sl]), 0.0)
                if h % 2 == 0:
                    pairs[h // 2] = (dq, dk, dv)
                    return
                dq0, dk0, dv0 = pairs.pop(h // 2)
                dq_ref[:, sl] = dq0 + dq
                dk_pair, dv_pair = dk0 + dk, dv0 + dv
                for w in range(3):
                    slot = (i + 1 + w) % 3
                    rows = slice(TQ * w, TQ * (w + 1))
                    dk_acc[slot, :, sl] += dk_pair[rows]
                    dv_acc[slot, :, sl] += dv_pair[rows]

            nxt = front(0)
            pending = None
            for h in range(NHA):
                st, dpt, qm, dom, mask = nxt
                if h + 1 < NHA:
                    nxt = front(h + 1)
                m = jnp.max(st, axis=0, keepdims=True)
                e = jnp.exp(st - m)
                pt = e * (1.0 / jnp.sum(e, axis=0, keepdims=True))
                delta = jnp.sum(pt * dpt, axis=0, keepdims=True)
                dst = pt * (dpt - delta)
                dtab_ref[h] += dst
                if pending is not None:
                    back(*pending)
                pending = (h, pt.astype(_MXU), dst.astype(_MXU), qm, dom, mask)
            back(*pending)

        @pl.when(i >= 2)
        def _():
            done = (i + 1) % 3
            dk_ref[...] = dk_acc[done]
            dv_ref[...] = dv_acc[done]

        pl.when(i == nq + 1)(sc.finish)

    back2 = pl.BlockSpec((TQ, AW), lambda i: (jnp.maximum(i - 2, 0), 0))
    anyspec = pl.BlockSpec(memory_space=pl.ANY)
    return pl.pallas_call(
        body, name="attn_bwd", grid=(nq + 2,),
        in_specs=[pl.BlockSpec((TQ, AW), lambda i: (qi(i), 0)), pl.BlockSpec((TQ, AW), lambda i: (qi(i), 0)),
                  kv_spec(1, 2), kv_spec(1, 1), kv_spec(1, 0), kv_spec(2, 2), kv_spec(2, 1), kv_spec(2, 0),
                  pl.BlockSpec((1, NHA, TW, TQ), lambda i: (jnp.minimum(i, 2), 0, 0, 0)), anyspec],
        out_specs=[pl.BlockSpec((TQ, AW), lambda i: (qi(i), 0)), back2, back2, _full((NHA, TW, TQ)), anyspec],
        out_shape=[jax.ShapeDtypeStruct((T, AW), F32), jax.ShapeDtypeStruct((T, AW), F32),
                   jax.ShapeDtypeStruct((T, AW), F32), jax.ShapeDtypeStruct((NHA, TW, TQ), F32),
                   jax.ShapeDtypeStruct(send.shape, send.dtype)],
        scratch_shapes=[pltpu.VMEM((3, TQ, AW), F32), pltpu.VMEM((3, TQ, AW), F32)] + _COMM_SEMS,
        compiler_params=_cp(("arbitrary",)),
    )(an, dout, an, an, an, an, an, an, tabt, send)


GR = 128
NG = TG // GR
CPT = TG // CH
CONV_K = 4


def _split3(x):
    a = x.astype(BF16)
    r = x - a.astype(F32)
    b = r.astype(BF16)
    c = (r - b.astype(F32)).astype(BF16)
    return a, b, c


def _ones_dot(ones_b, x):
    return sum(jnp.dot(ones_b, t, preferred_element_type=F32) for t in _split3(x))


def _dot_ones_nt(x, ones_b):
    dn = (((1,), (1,)), ((), ()))
    return sum(lax.dot_general(t, ones_b, dn, preferred_element_type=F32) for t in _split3(x))


def _dn_masks():
    r, c = _iota((GR, GR), 0), _iota((GR, GR), 1)
    same = (r >> 6) == (c >> 6)
    one = lambda m: jnp.where(m, 1.0, 0.0).astype(BF16)
    return dict(
        tril=same & (c <= r), strict=same & (c < r), triu=same & (c >= r), strict_u=same & (c > r),
        tril_b=one(same & (c <= r)), triu_b=one(same & (c >= r)), blk_b=one(same), eye_b=one(r == c),
        eye=jnp.where(r == c, 1.0, 0.0).astype(F32),
        fold_b=one((_iota((GR, CH), 0) & (CH - 1)) == _iota((GR, CH), 1)),
        last=(_iota((GR, 1), 0) & (CH - 1)) == CH - 1,
    )


def _shift_down(x, halo, k):
    if k == 0:
        return x
    xs = pltpu.roll(x, k, 0)
    hs = pltpu.roll(halo, k, 0)
    top = jnp.where(_iota(halo.shape, 0) < k, hs, xs[0:8])
    return jnp.concatenate([top, xs[8:]], axis=0)


def _shift_up(x, halo, k):
    if k == 0:
        return x
    n = x.shape[0]
    xs = pltpu.roll(x, n - k, 0)
    hs = pltpu.roll(halo, 8 - k, 0)
    bot = jnp.where(_iota(halo.shape, 0) >= 8 - k, hs, xs[n - 8:n])
    return jnp.concatenate([xs[0:n - 8], bot], axis=0)


def _conv(x, halo, w):
    y = x * w[CONV_K - 1:CONV_K, :]
    for k in range(1, CONV_K):
        y = y + _shift_down(x, halo, k) * w[CONV_K - 1 - k:CONV_K - k, :]
    return y


def _tri_inv(lmats, eye):
    ps = [-m for m in lmats]
    rs = [eye + p for p in ps]
    for _ in range(5):
        ps = [_dot(p, p) for p in ps]
        rs = [r + _dot(r, p) for r, p in zip(rs, ps)]
    return rs


def _gate_terms(ba_g, bat_g, alog8, dtb8, alog8t, dtb8t, K):
    g8 = -jnp.exp(alog8) * _softplus(ba_g + dtb8)
    g8t = -jnp.exp(alog8t) * _softplus(bat_g + dtb8t)
    gc8 = _ones_dot(K["tril_b"], g8)
    gl8 = _ones_dot(K["blk_b"], g8)
    gcrow8 = _dot_ones_nt(g8t, K["tril_b"])
    return g8, gc8, gl8, gcrow8


def _dn_heads(c_tile, rows, beta8, gc8, gl8, gcrow8, K, pre=None):
    return _dn_heads_groups(c_tile, [(rows, beta8, gc8, gl8, gcrow8)], K, None if pre is None else [pre])[0]


def _dn_heads_groups(c_tile, groups, K, pres=None):
    ds = [_dn_head_vec(c_tile, rows, h, beta8, gc8, gl8, gcrow8, K)
          for rows, beta8, gc8, gl8, gcrow8 in groups for h in range(NHD)]
    pls = [_dot_nt(d["kb"], d["kn"]) for d in ds]
    pms = [_dot_nt(d["qn"], d["kn"]) for d in ds]
    for d, pl_, pm in zip(ds, pls, pms):
        d.update(pl=pl_, pm=pm, lmat=jnp.where(K["strict"], pl_ * d["gam_m"], 0.0), mm=pm * d["gam_m"])
    if pres is None:
        for d, tm in zip(ds, _tri_inv([d["lmat"] for d in ds], K["eye"])):
            d.update(tm=tm, u=_dot(tm, d["vb"]), w=_dot(tm, d["kg"]))
    else:
        for d, (tm, u, w) in zip(ds, [p for pre in pres for p in pre]):
            d.update(tm=tm, u=u, w=w)
    return [ds[NHD * k:NHD * (k + 1)] for k in range(len(groups))]


def _dn_head_vec(c_tile, rows, h, beta8, gc8, gl8, gcrow8, K):
    qr = c_tile[rows, DHD * h:DHD * (h + 1)]
    kr = c_tile[rows, DW + DHD * h:DW + DHD * (h + 1)]
    v = c_tile[rows, 2 * DW + DHD * h:2 * DW + DHD * (h + 1)]
    rq = lax.rsqrt(jnp.sum(qr * qr, axis=1, keepdims=True) + EPS)
    rk = lax.rsqrt(jnp.sum(kr * kr, axis=1, keepdims=True) + EPS)
    qh, kn = qr * rq, kr * rk
    qn = qh * (DHD ** -0.5)
    beta = _col(beta8, h)
    gccol, glcol, gcrow = _col(gc8, NHD + h), _col(gl8, NHD + h), _row(gcrow8, NHD + h)
    diff = gccol - gcrow
    gam_m = jnp.exp(jnp.where(K["tril"], diff, NEG))
    gam = jnp.exp(gccol)
    egl = jnp.exp(glcol - gccol)
    kb, vb = kn * beta, v * beta
    kg = kb * gam
    return dict(qr=qr, kr=kr, v=v, rq=rq, rk=rk, qh=qh, qn=qn, kn=kn, beta=beta, diff=diff, gam_m=gam_m, gam=gam,
                egl=egl, el=jnp.exp(glcol), kb=kb, vb=vb, kg=kg, qd=qn * gam, kd=kn * egl)


def _halo_prev(width):
    return pl.BlockSpec((8, width), lambda i: (jnp.maximum(i * (TG // 8) - 1, 0), 0))


def _dn_prep(draw, conv_w, ba, bat, alog8, dtb8, alog8t, dtb8t):
    T = draw.shape[0]
    nb = T // TG
    hm = lambda w, dt: jax.ShapeDtypeStruct((NHD, T, w), dt)
    hm_spec = lambda w: pl.BlockSpec((NHD, TG, w), lambda i: (0, i, 0))
    pc = lambda r, c: jax.ShapeDtypeStruct((NHD, T // CH, r, c), _MXU)
    pc_spec = lambda r, c: pl.BlockSpec((NHD, CPT, r, c), lambda i: (0, i, 0, 0))

    def body(x_ref, halo_ref, cw_ref, ba_ref, bat_ref, al_ref, dt_ref, alt_ref, dtt_ref,
             u_ref, w_ref, kd_ref, tm_ref, wq_ref, km_ref, mq_ref, wt_ref, elb_ref, cv_ref):
        i = pl.program_id(0)
        K = _dn_masks()
        halo = jnp.where(i > 0, halo_ref[...], 0.0)
        cv = _conv(x_ref[...], halo, cw_ref[...])
        cv_ref[...] = cv
        c_tile = cv * _sigmoid(cv)
        eye128 = jnp.where(_iota((DHD, DHD), 0) == _iota((DHD, DHD), 1), 1.0, 0.0).astype(_MXU)
        def gate_inputs(g):
            rows = slice(GR * g, GR * (g + 1))
            ba_g = ba_ref[rows, :]
            _, gc8, gl8, gcrow8 = _gate_terms(ba_g, bat_ref[:, rows], al_ref[...], dt_ref[...], alt_ref[...],
                                              dtt_ref[...], K)
            return rows, _sigmoid(ba_g), gc8, gl8, gcrow8

        def store(g, rows, ds):
            mmts = [_dot_nt(d["kn"], d["qn"]) * jnp.exp(jnp.where(K["triu"], -d["diff"], NEG)) for d in ds]
            mcs = [_dot(d["mm"], K["fold_b"]) for d in ds]
            mcts = [_dot(m, K["fold_b"]) for m in mmts]
            for h, d in enumerate(ds):
                tm_ref[h, rows, :] = d["tm"].astype(_MXU)
                u_ref[h, rows, :] = d["u"]
                w_ref[h, rows, :] = d["w"].astype(_MXU)
                kd_ref[h, rows, :] = d["kd"].astype(_MXU)
                elb = jnp.broadcast_to(d["el"], (GR, DHD))
                for cc in range(GR // CH):
                    ch = slice(CH * cc, CH * (cc + 1))
                    n = (GR // CH) * g + cc
                    wq_ref[h, n, 0:CH, :] = d["w"][ch].astype(_MXU)
                    wq_ref[h, n, CH:2 * CH, :] = d["qd"][ch].astype(_MXU)
                    km_ref[h, n, 0:DHD, :] = _dot_nt(eye128, d["kd"][ch]).astype(_MXU)
                    km_ref[h, n, DHD:DHD + CH, :] = mcs[h][ch].astype(_MXU)
                    mq_ref[h, n, 0:CH, :] = mcts[h][ch].astype(_MXU)
                    mq_ref[h, n, CH:CH + DHD, :] = _dot_nt(eye128, d["qd"][ch]).astype(_MXU)
                    wt_ref[h, n] = _dot_nt(eye128, d["w"][ch]).astype(_MXU)
                    elb_ref[n:n + 1, DHD * h:DHD * (h + 1)] = elb[CH * cc:CH * cc + 1, :]

        PAIR = 4
        for g0 in range(0, NG, PAIR):
            pair = [gate_inputs(g) for g in range(g0, g0 + PAIR)]
            for k, ds in enumerate(_dn_heads_groups(c_tile, pair, K)):
                store(g0 + k, pair[k][0], ds)

    return pl.pallas_call(
        body, name="dn_prep", grid=(nb,),
        in_specs=[_rows(TG, 1536), _halo_prev(1536), _full((CONV_K, 1536)), _rows(TG, 8),
                  pl.BlockSpec((8, TG), lambda i: (0, i)), _full((1, 8)), _full((1, 8)), _full((8, 1)), _full((8, 1))],
        out_specs=[hm_spec(DHD), hm_spec(DHD), hm_spec(DHD), hm_spec(GR),
                   pc_spec(2 * CH, DHD), pc_spec(DHD + CH, CH), pc_spec(CH + DHD, CH), pc_spec(DHD, CH),
                   pl.BlockSpec((CPT, NHD * DHD), lambda i: (i, 0)), _rows(TG, 1536)],
        out_shape=[hm(DHD, F32), hm(DHD, _MXU), hm(DHD, _MXU), hm(GR, _MXU),
                   pc(2 * CH, DHD), pc(DHD + CH, CH), pc(CH + DHD, CH), pc(DHD, CH),
                   jax.ShapeDtypeStruct((T // CH, NHD * DHD), F32), jax.ShapeDtypeStruct((T, 1536), F32)],
        compiler_params=_cp(("arbitrary",)),
    )(draw, draw, conv_w, ba, bat, alog8, dtb8, alog8t, dtb8t)


def _dn_scan(u, wq, km, elb):
    T = u.shape[1]
    nb = T // TG
    hm_spec = lambda wd: pl.BlockSpec((NHD, TG, wd), lambda i: (0, i, 0))

    def body(u_ref, wq_ref, km_ref, elb_ref, o_ref, vn_ref, sn_ref, S):
        @pl.when(pl.program_id(0) == 0)
        def _():
            S[...] = jnp.zeros_like(S)

        sub8 = _iota((CPT, DHD), 0)
        heads = range(NHD)

        def chunk(cc, carry):
            rs = pl.ds(pl.multiple_of(cc * CH, CH), CH)
            sh = [S[h] for h in heads]
            sb = [s.astype(_MXU) for s in sh]
            r1 = [_dot(wq_ref[h, cc], sb[h]) for h in heads]
            vnb = [(u_ref[h, rs, :] - r1[h][0:CH]).astype(_MXU) for h in heads]
            r2 = [_dot(km_ref[h, cc], vnb[h]) for h in heads]
            for h in heads:
                el = jnp.sum(jnp.where(sub8 == cc, elb_ref[:, DHD * h:DHD * (h + 1)], 0.0), axis=0, keepdims=True)
                S[h] = sh[h] * el + r2[h][0:DHD]
                sn_ref[cc, h] = sb[h]
                vn_ref[h, rs, :] = vnb[h]
                o_ref[h, rs, :] = r1[h][CH:2 * CH] + r2[h][DHD:DHD + CH]
            return carry

        lax.fori_loop(0, CPT, chunk, 0)

    return pl.pallas_call(
        body, name="dn_scan", grid=(nb,),
        in_specs=[hm_spec(DHD), pl.BlockSpec((NHD, CPT, 2 * CH, DHD), lambda i: (0, i, 0, 0)),
                  pl.BlockSpec((NHD, CPT, DHD + CH, CH), lambda i: (0, i, 0, 0)),
                  pl.BlockSpec((CPT, NHD * DHD), lambda i: (i, 0))],
        out_specs=[hm_spec(DHD), hm_spec(DHD), pl.BlockSpec((CPT, NHD, DHD, DHD), lambda i: (i, 0, 0, 0))],
        out_shape=[jax.ShapeDtypeStruct((NHD, T, DHD), F32), jax.ShapeDtypeStruct((NHD, T, DHD), _MXU),
                   jax.ShapeDtypeStruct((T // CH, NHD, DHD, DHD), _MXU)],
        scratch_shapes=[pltpu.VMEM((NHD, DHD, DHD), F32)],
        compiler_params=_cp(("arbitrary",)),
    )(u, wq, km, elb)


def _dn_scan_bwd(do, mq, kd, wt, sn, vn, elb):
    T = do.shape[1]
    nb = T // TG
    rev = lambda wd: pl.BlockSpec((NHD, TG, wd), lambda i: (0, nb - 1 - i, 0))
    rev_t = lambda r: pl.BlockSpec((NHD, CPT, r, CH), lambda i: (0, nb - 1 - i, 0, 0))

    def body(do_ref, mq_ref, kd_ref, wt_ref, sn_ref, vn_ref, elb_ref,
             du_ref, dw_ref, dqd_ref, dkd_ref, dgx_ref, dS):
        @pl.when(pl.program_id(0) == 0)
        def _():
            dS[...] = jnp.zeros_like(dS)

        last_row = _iota((CH, DHD), 0) == CH - 1
        sub8 = _iota((CPT, DHD), 0)
        heads = range(NHD)

        def chunk(k, carry):
            cc = CPT - 1 - k
            rs = pl.ds(pl.multiple_of(cc * CH, CH), CH)
            dsh = [dS[h] for h in heads]
            dsb = [d.astype(_MXU) for d in dsh]
            doc = [do_ref[h, rs, :].astype(_MXU) for h in heads]
            a = [_dot(mq_ref[h, cc], doc[h]) for h in heads]
            b = [_dot(kd_ref[h, rs, :], dsb[h]) for h in heads]
            dvn = [a[h][0:CH] + b[h] for h in heads]
            dvnb = [d.astype(_MXU) for d in dvn]
            e = [_dot(wt_ref[h, cc], dvnb[h]) for h in heads]
            for h in heads:
                el = jnp.sum(jnp.where(sub8 == cc, elb_ref[:, DHD * h:DHD * (h + 1)], 0.0), axis=0, keepdims=True)
                sn = sn_ref[cc, h]
                dS[h] = a[h][CH:CH + DHD] + dsh[h] * el - e[h]
                du_ref[h, rs, :] = dvn[h]
                c = _dot_nt(jnp.concatenate([doc[h], dvnb[h]], axis=0), sn)
                dqd_ref[h, rs, :] = c[0:CH]
                dw_ref[h, rs, :] = -c[CH:2 * CH]
                dkd_ref[h, rs, :] = _dot_nt(vn_ref[h, rs, :], dsb[h])
                part = jnp.sum(dsh[h] * sn.astype(F32), axis=0, keepdims=True) * el
                dgx_ref[h, rs, :] = jnp.where(last_row, part, 0.0)
            return carry

        lax.fori_loop(0, CPT, chunk, 0)

    o = jax.ShapeDtypeStruct((NHD, T, DHD), F32)
    return pl.pallas_call(
        body, name="dn_scan_bwd", grid=(nb,),
        in_specs=[rev(DHD), rev_t(CH + DHD), rev(DHD), rev_t(DHD),
                  pl.BlockSpec((CPT, NHD, DHD, DHD), lambda i: (nb - 1 - i, 0, 0, 0)), rev(DHD),
                  pl.BlockSpec((CPT, NHD * DHD), lambda i: (nb - 1 - i, 0))],
        out_specs=[rev(DHD)] * 5,
        out_shape=[o] * 5,
        scratch_shapes=[pltpu.VMEM((NHD, DHD, DHD), F32)],
        compiler_params=_cp(("arbitrary",)),
    )(do, mq, kd, wt, sn, vn, elb)


def _put_col(acc, k, col):
    return jnp.where(_iota(acc.shape, 1) == k, col, acc)


def _dn_post_bwd(draw, cv, conv_w, ba, bat, alog8, dtb8, alog8t, dtb8t, du, dw, dqd, dkd, dgx, do, vn, tm, u, w):
    T = draw.shape[0]
    nb = T // TG
    hm_spec = lambda wd: pl.BlockSpec((NHD, TG, wd), lambda i: (0, nb - 1 - i, 0))
    rrows = lambda w: pl.BlockSpec((TG, w), lambda i: (nb - 1 - i, 0))

    def body(x_ref, cv_ref, cw_ref, ba_ref, bat_ref, al_ref, dt_ref, alt_ref, dtt_ref,
             du_ref, dw_ref, dqd_ref, dkd_ref, dgx_ref, do_ref, vn_ref, tm_ref, u_ref, w_ref,
             dx_ref, dba_ref, sm_ref, dcw_ref, dc_ref, nxt_ref):
        i = pl.program_id(0)

        @pl.when(i == 0)
        def _():
            sm_ref[...] = jnp.zeros_like(sm_ref)
            dcw_ref[...] = jnp.zeros_like(dcw_ref)
            nxt_ref[...] = jnp.zeros_like(nxt_ref)

        K = _dn_masks()
        cv = cv_ref[...]
        sg = _sigmoid(cv)
        c_tile = cv * sg
        dsilu = sg * (1.0 + cv * (1.0 - sg))
        for g in range(NG):
            rows = slice(GR * g, GR * (g + 1))
            ba_g = ba_ref[rows, :]
            g8, gc8, gl8, gcrow8 = _gate_terms(ba_g, bat_ref[:, rows], al_ref[...], dt_ref[...], alt_ref[...],
                                               dtt_ref[...], K)
            beta8 = _sigmoid(ba_g)
            dgc8 = jnp.zeros((GR, 8), F32)
            rd8 = jnp.zeros((GR, 8), F32)
            dbeta8 = jnp.zeros((GR, 8), F32)
            pre = [(tm_ref[h, rows, :], u_ref[h, rows, :], w_ref[h, rows, :]) for h in range(NHD)]
            ds = _dn_heads(c_tile, rows, beta8, gc8, gl8, gcrow8, K, pre)
            H = range(NHD)
            eye_b = K["eye_b"].astype(_MXU)
            gam_t = [jnp.exp(jnp.where(K["triu"], -d["diff"], NEG)) for d in ds]
            doh = [do_ref[h, rows, :] for h in H]
            vnh = [vn_ref[h, rows, :] for h in H]
            tt = [_dot_nt(eye_b, d["tm"]) for d in ds]
            dvb = [_dot(tt[h], du_ref[h, rows, :]) for h in H]
            dkg = [_dot(tt[h], dw_ref[h, rows, :]) for h in H]
            plt = [_dot_nt(d["kn"], d["kb"]) for d in ds]
            pmt = [_dot_nt(d["kn"], d["qn"]) for d in ds]
            da = [-(_dot_nt(dvb[h], ds[h]["u"]) + _dot_nt(dkg[h], ds[h]["w"])) for h in H]
            dat = [-(_dot_nt(ds[h]["u"], dvb[h]) + _dot_nt(ds[h]["w"], dkg[h])) for h in H]
            dpm = [jnp.where(K["tril"], _dot_nt(doh[h], vnh[h]), 0.0) * ds[h]["gam_m"] for h in H]
            dpmt = [jnp.where(K["triu"], _dot_nt(vnh[h], doh[h]), 0.0) * gam_t[h] for h in H]
            dpl = [jnp.where(K["strict"], da[h], 0.0) * ds[h]["gam_m"] for h in H]
            dplt = [jnp.where(K["strict_u"], dat[h], 0.0) * gam_t[h] for h in H]
            dkb = [_dot(dpl[h], ds[h]["kn"]) + dkg[h] * ds[h]["gam"] for h in H]
            dqn = [_dot(dpm[h], ds[h]["kn"]) + dqd_ref[h, rows, :] * ds[h]["gam"] for h in H]
            dknm = [_dot(dplt[h], ds[h]["kb"]) + _dot(dpmt[h], ds[h]["qn"]) for h in H]
            for h, d in enumerate(ds):
                kn, dqdh, dkdh = d["kn"], dqd_ref[h, rows, :], dkd_ref[h, rows, :]
                dkn = dknm[h] + dkdh * d["egl"] + dkb[h] * d["beta"]
                dkd_kd = dkdh * d["kd"]
                rd = jnp.sum(dkd_kd, axis=1, keepdims=True)
                dgc = jnp.sum(dpl[h] * d["pl"] + dpm[h] * d["pm"] - dplt[h] * plt[h] - dpmt[h] * pmt[h]
                              + dqdh * d["qd"] + dkg[h] * d["kg"] - dkd_kd + dgx_ref[h, rows, :],
                              axis=1, keepdims=True)
                dgc8 = _put_col(dgc8, NHD + h, dgc)
                rd8 = _put_col(rd8, NHD + h, rd)
                dbeta = jnp.sum(dkb[h] * kn + dvb[h] * d["v"], axis=1, keepdims=True)
                dbeta8 = _put_col(dbeta8, h, dbeta)
                dqh = dqn[h] * (DHD ** -0.5)
                qh = d["qh"]
                dqr = d["rq"] * (dqh - qh * jnp.sum(dqh * qh, axis=1, keepdims=True))
                dkr = d["rk"] * (dkn - kn * jnp.sum(dkn * kn, axis=1, keepdims=True))
                cq = slice(DHD * h, DHD * (h + 1))
                ck = slice(DW + DHD * h, DW + DHD * (h + 1))
                cvv = slice(2 * DW + DHD * h, 2 * DW + DHD * (h + 1))
                dc_ref[rows, cq] = dqr * dsilu[rows, cq]
                dc_ref[rows, ck] = dkr * dsilu[rows, ck]
                dc_ref[rows, cvv] = dvb[h] * d["beta"] * dsilu[rows, cvv]
            dgc8 = dgc8 + jnp.where(K["last"], _ones_dot(K["blk_b"], rd8), 0.0)
            dg8 = _ones_dot(K["triu_b"], dgc8)
            sgm = _sigmoid(ba_g + dt_ref[...])
            dalpha = dg8 * (-jnp.exp(al_ref[...])) * sgm
            lane8 = _iota((GR, 8), 1)
            dba_ref[rows, :] = jnp.where(lane8 < NHD, dbeta8 * beta8 * (1.0 - beta8), dalpha)
            valid = lane8 >= NHD
            sm_ref[0:1, 0:8] += jnp.sum(jnp.where(valid, dg8 * g8, 0.0), axis=0, keepdims=True)
            sm_ref[1:2, 0:8] += jnp.sum(jnp.where(valid, dalpha, 0.0), axis=0, keepdims=True)

        dcv = dc_ref[...]
        xv = x_ref[...]
        nxt = nxt_ref[...]
        w = cw_ref[...]
        dx = dcv * w[CONV_K - 1:CONV_K, :]
        dcw_ref[CONV_K - 1:CONV_K, :] += jnp.sum(dcv * xv, axis=0, keepdims=True)
        for k in range(1, CONV_K):
            j = CONV_K - 1 - k
            up = _shift_up(dcv, nxt, k)
            dx = dx + up * w[j:j + 1, :]
            dcw_ref[j:j + 1, :] += jnp.sum(up * xv, axis=0, keepdims=True)
        dx_ref[...] = dx
        nxt_ref[...] = dcv[0:8]

    return pl.pallas_call(
        body, name="dn_post_bwd", grid=(nb,),
        in_specs=[rrows(1536), rrows(1536), _full((CONV_K, 1536)), rrows(8),
                  pl.BlockSpec((8, TG), lambda i: (0, nb - 1 - i)), _full((1, 8)), _full((1, 8)), _full((8, 1)),
                  _full((8, 1)),
                  hm_spec(DHD), hm_spec(DHD), hm_spec(DHD), hm_spec(DHD), hm_spec(DHD), hm_spec(DHD), hm_spec(DHD),
                  hm_spec(GR), hm_spec(DHD), hm_spec(DHD)],
        out_specs=[rrows(1536), rrows(8), _full((8, LANES)), _full((8, 1536))],
        out_shape=[jax.ShapeDtypeStruct((T, 1536), F32), jax.ShapeDtypeStruct((T, 8), F32),
                   jax.ShapeDtypeStruct((8, LANES), F32), jax.ShapeDtypeStruct((8, 1536), F32)],
        scratch_shapes=[pltpu.VMEM((TG, 1536), F32), pltpu.VMEM((8, 1536), F32)],
        compiler_params=_cp(("arbitrary",)),
    )(draw, cv, conv_w, ba, bat, alog8, dtb8, alog8t, dtb8t, du, dw, dqd, dkd, dgx, do, vn, tm, u, w)


def _rms(x):
    return lax.rsqrt(jnp.mean(x * x, axis=1, keepdims=True) + EPS)


def _rms_bwd(dy, xh, r, g):
    dxh = dy * g
    return r * (dxh - xh * jnp.mean(dxh * xh, axis=1, keepdims=True))


def _hm_rows(tm):
    return pl.BlockSpec((NHD, tm, DHD), lambda i: (0, i, 0))


def _post_mix(apre, o, z, x, w_out, g_a, g_dn):
    T = x.shape[0]

    def body(ap_ref, o_ref, z_ref, x_ref, w_ref, ga_ref, gd_ref, x1_ref, mix_ref):
        ap = ap_ref[...]
        parts = [ap * _rms(ap) * ga_ref[...]]
        zz = z_ref[...]
        for h in range(NHD):
            oh = o_ref[h]
            zh = zz[:, DHD * h:DHD * (h + 1)]
            parts.append(oh * _rms(oh) * gd_ref[...] * (zh * _sigmoid(zh)))
        mix = jnp.concatenate(parts, axis=1).astype(_MXU)
        mix_ref[...] = mix
        x1_ref[...] = x_ref[...] + jnp.dot(mix, w_ref[...], preferred_element_type=F32)

    return pl.pallas_call(
        body, name="post_mix", grid=(T // TM,),
        in_specs=[_rows(TM, AW), _hm_rows(TM), _rows(TM, DW), _rows(TM, D), _full((D, D)), _full((1, AW)),
                  _full((1, DHD))],
        out_specs=[_rows(TM, D), _rows(TM, D)],
        out_shape=[jax.ShapeDtypeStruct((T, D), F32), jax.ShapeDtypeStruct((T, D), _MXU)],
        compiler_params=_cp(("arbitrary",)),
    )(apre, o, z, x, w_out, g_a, g_dn)


def _ffn(x1, tgt, wl_all, g_ffn):
    T = x1.shape[0]
    SH = FF // N_DEV
    nt = (((1,), (1,)), ((), ()))

    def body(x_ref, t_ref, wl_hbm, g_ref,
             dx1_ref, dx1b_ref, h2_ref, act_ref, dgu_ref, dyb_ref, loss_ref, dg_ref, wg, wu, wd, sem):
        @pl.when(pl.program_id(0) == 0)
        def _():
            cps = [pltpu.make_async_copy(wl_hbm.at[dev, pl.ds(128 + SH * k, SH), :], dst.at[pl.ds(SH * dev, SH), :],
                                         sem.at[N_DEV * k + dev])
                   for k, dst in enumerate((wg, wu, wd)) for dev in range(N_DEV)]
            for cp in cps:
                cp.start()
            for cp in cps:
                cp.wait()
            loss_ref[...] = jnp.zeros_like(loss_ref)
            dg_ref[...] = jnp.zeros_like(dg_ref)

        xv = x_ref[...]
        r = _rms(xv)
        xh = xv * r
        gg = g_ref[...]
        h2 = (xh * gg).astype(_MXU)
        h2_ref[...] = h2
        gate = lax.dot_general(h2, wg[...], nt, preferred_element_type=F32)
        up = lax.dot_general(h2, wu[...], nt, preferred_element_type=F32)
        sg = _sigmoid(gate)
        silu = gate * sg
        act = (silu * up).astype(_MXU)
        act_ref[...] = act
        y = xv + jnp.dot(act, wd[...], preferred_element_type=F32)
        err = y - t_ref[...]
        loss_ref[...] += jnp.sum(err * err, axis=0, keepdims=True)
        dy = err * (1.0 / D)
        dyb = dy.astype(_MXU)
        dyb_ref[...] = dyb
        dact = lax.dot_general(dyb, wd[...], nt, preferred_element_type=F32)
        dgate = (dact * up * (sg * (1.0 + gate * (1.0 - sg)))).astype(_MXU)
        dup = (dact * silu).astype(_MXU)
        dgu_ref[:, 0:FF] = dgate
        dgu_ref[:, FF:2 * FF] = dup
        dh2 = (jnp.dot(dgate, wg[...], preferred_element_type=F32)
               + jnp.dot(dup, wu[...], preferred_element_type=F32))
        dg_ref[...] += jnp.sum(dh2 * xh, axis=0, keepdims=True)
        dx1 = dy + _rms_bwd(dh2, xh, r, gg)
        dx1_ref[...] = dx1
        dx1b_ref[...] = dx1.astype(_MXU)

    anyspec = pl.BlockSpec(memory_space=pl.ANY)
    sd = lambda w, dt: jax.ShapeDtypeStruct((T, w), dt)
    return pl.pallas_call(
        body, name="ffn", grid=(T // TF,),
        in_specs=[_rows(TF, D), _rows(TF, D), anyspec, _full((1, D))],
        out_specs=[_rows(TF, D), _rows(TF, D), _rows(TF, D), _rows(TF, FF), _rows(TF, 2 * FF), _rows(TF, D),
                   _full((1, D)), _full((1, D))],
        out_shape=[sd(D, F32), sd(D, _MXU), sd(D, _MXU), sd(FF, _MXU), sd(2 * FF, _MXU), sd(D, _MXU),
                   jax.ShapeDtypeStruct((1, D), F32), jax.ShapeDtypeStruct((1, D), F32)],
        scratch_shapes=[pltpu.VMEM((FF, D), _MXU)] * 3 + [pltpu.SemaphoreType.DMA((3 * N_DEV,))],
        compiler_params=_cp(("arbitrary",)),
    )(x1, tgt, wl_all, g_ffn)


def _mix_bwd(dx1b, w_out, apre, o, z, g_a, g_dn):
    T = dx1b.shape[0]

    def body(dx_ref, w_ref, ap_ref, o_ref, z_ref, ga_ref, gd_ref, dap_ref, do_ref, dz_ref, dga_ref, dgd_ref):
        @pl.when(pl.program_id(0) == 0)
        def _():
            dga_ref[...] = jnp.zeros_like(dga_ref)
            dgd_ref[...] = jnp.zeros_like(dgd_ref)

        dmix = lax.dot_general(dx_ref[...], w_ref[...], (((1,), (1,)), ((), ())), preferred_element_type=F32)
        ap = ap_ref[...]
        ra = _rms(ap)
        ah = ap * ra
        da = dmix[:, 0:AW]
        dga_ref[...] += jnp.sum(da * ah, axis=0, keepdims=True)
        dap_ref[...] = _rms_bwd(da, ah, ra, ga_ref[...])
        zz = z_ref[...]
        gd = gd_ref[...]
        for h in range(NHD):
            cs = slice(DHD * h, DHD * (h + 1))
            dd = dmix[:, AW + DHD * h:AW + DHD * (h + 1)]
            oh = o_ref[h]
            ro = _rms(oh)
            ohh = oh * ro
            zh = zz[:, cs]
            sz = _sigmoid(zh)
            dz_ref[:, cs] = dd * (ohh * gd) * (sz * (1.0 + zh * (1.0 - sz)))
            don = dd * (zh * sz)
            dgd_ref[...] += jnp.sum(don * ohh, axis=0, keepdims=True)
            do_ref[h] = _rms_bwd(don, ohh, ro, gd)

    return pl.pallas_call(
        body, name="mix_bwd", grid=(T // TM,),
        in_specs=[_rows(TM, D), _full((D, D)), _rows(TM, AW), _hm_rows(TM), _rows(TM, DW), _full((1, AW)),
                  _full((1, DHD))],
        out_specs=[_rows(TM, AW), _hm_rows(TM), _rows(TM, DW), _full((1, AW)), _full((1, DHD))],
        out_shape=[jax.ShapeDtypeStruct((T, AW), F32), jax.ShapeDtypeStruct((NHD, T, DHD), F32),
                   jax.ShapeDtypeStruct((T, DW), F32), jax.ShapeDtypeStruct((1, AW), F32),
                   jax.ShapeDtypeStruct((1, DHD), F32)],
        compiler_params=_cp(("arbitrary",)),
    )(dx1b, w_out, apre, o, z, g_a, g_dn)


DPW = 3712


def _inproj_bwd(dqn, dkn, dv, araw, ddraw, dz, dba, x, dx1, w_int, w_ba, g_mix, qg_t, kg_t):
    T = x.shape[0]

    def body(dqn_ref, dkn_ref, dv_ref, ar_ref, dd_ref, dz_ref, dba_ref, x_ref, dx1_ref, w_hbm, wba_ref, g_ref, qg_ref,
             kg_ref, dx_ref, dp_ref, dgm_ref, dqg_ref, dkg_ref, w_ref, w_sem):
        @pl.when(pl.program_id(0) == 0)
        def _():
            cp = pltpu.make_async_copy(w_hbm, w_ref, w_sem)
            cp.start()
            cp.wait()
            dgm_ref[...] = jnp.zeros_like(dgm_ref)
            dqg_ref[...] = jnp.zeros_like(dqg_ref)
            dkg_ref[...] = jnp.zeros_like(dkg_ref)

        bd = _block_ones(AW // 2, DHA)

        def head_norm_bwd(raw, dyn, gain, dg_ref):
            r = _head_rms(raw, bd, DHA)
            xh = raw * r
            dg_ref[...] += jnp.sum(dyn * xh, axis=0, keepdims=True)
            dxh = dyn * gain
            return r * (dxh - xh * (_head_sum(dxh * xh, bd) * (1.0 / DHA)))

        def segment(lo, val):
            vb = val.astype(_MXU)
            dp_ref[:, lo:lo + val.shape[1]] = vb
            return jnp.dot(vb, w_ref[lo:lo + val.shape[1], :], preferred_element_type=F32)

        dh = segment(1536, dd_ref[...]) + segment(2 * AW, dv_ref[...]) + segment(3072, dz_ref[...])
        dbab = dba_ref[...].astype(_MXU)
        dp_ref[:, 3584:DPW] = jnp.zeros((TM, DPW - 3584), _MXU)
        dp_ref[:, 3584:3592] = dbab
        dh = dh + lax.dot_general(dbab, wba_ref[...], (((1,), (1,)), ((), ())), preferred_element_type=F32)
        ar = ar_ref[...]
        dq = head_norm_bwd(ar[:, 0:AW], dqn_ref[...] * (DHA ** -0.5), qg_ref[...], dqg_ref)
        dk = head_norm_bwd(ar[:, AW:2 * AW], dkn_ref[...], kg_ref[...], dkg_ref)
        dh = dh + segment(0, dq) + segment(AW, dk)
        xv = x_ref[...]
        r = _rms(xv)
        xh = xv * r
        dgm_ref[...] += jnp.sum(dh * xh, axis=0, keepdims=True)
        dx_ref[...] = dx1_ref[...] + _rms_bwd(dh, xh, r, g_ref[...])

    return pl.pallas_call(
        body, name="inproj_bwd", grid=(T // TM,),
        in_specs=[_rows(TM, AW), _rows(TM, AW), _rows(TM, AW), _rows(TM, 1536), _rows(TM, 1536), _rows(TM, DW),
                  _rows(TM, 8), _rows(TM, D), _rows(TM, D), pl.BlockSpec(memory_space=pl.ANY), _full((D, 8)),
                  _full((1, D)), _full((1, AW)), _full((1, AW))],
        out_specs=[_rows(TM, D), _rows(TM, DPW), _full((1, D)), _full((1, AW)), _full((1, AW))],
        out_shape=[jax.ShapeDtypeStruct((T, D), F32), jax.ShapeDtypeStruct((T, DPW), _MXU),
                   jax.ShapeDtypeStruct((1, D), F32), jax.ShapeDtypeStruct((1, AW), F32),
                   jax.ShapeDtypeStruct((1, AW), F32)],
        scratch_shapes=[pltpu.VMEM((3584, D), _MXU), pltpu.SemaphoreType.DMA],
        compiler_params=_cp(("arbitrary",)),
    )(dqn, dkn, dv, araw, ddraw, dz, dba, x, dx1, w_int, w_ba, g_mix, qg_t, kg_t)


def _wgrad(a, b, name, tk=1024, tn=None, out_dtype=F32, transposed=False):
    T, M = a.shape
    N = b.shape[1]
    tn = N if tn is None else tn
    assert T % tk == 0 and N % tn == 0, (T, tk, N, tn)
    nk = T // tk

    def body(a_ref, b_ref, o_ref, acc):
        k = pl.program_id(1)

        @pl.when(k == 0)
        def _():
            acc[...] = jnp.zeros_like(acc)

        acc[...] += lax.dot_general(a_ref[...], b_ref[...], (((0,), (0,)), ((), ())), preferred_element_type=F32)

        @pl.when(k == nk - 1)
        def _():
            r = acc[...]
            o_ref[...] = (r.T if transposed else r).astype(out_dtype)

    if transposed:
        out_spec, out_shape = pl.BlockSpec((tn, M), lambda j, k: (j, 0)), (N, M)
    else:
        out_spec, out_shape = pl.BlockSpec((M, tn), lambda j, k: (0, j)), (M, N)
    return pl.pallas_call(
        body, name=name, grid=(N // tn, nk),
        in_specs=[pl.BlockSpec((tk, M), lambda j, k: (k, 0)), pl.BlockSpec((tk, tn), lambda j, k: (k, j))],
        out_specs=out_spec,
        out_shape=jax.ShapeDtypeStruct(out_shape, out_dtype),
        scratch_shapes=[pltpu.VMEM((M, tn), F32)],
        compiler_params=_cp(("arbitrary", "arbitrary")),
    )(a, b)


def _adamw(parts, w, m, v, name, tr, send=None):
    K, R, W = parts.shape
    n = R // tr

    def body(p_ref, w_ref, m_ref, v_ref, *rest):
        if send is not None:
            send_ref, g_ref, d_ref, nm_ref, nv_ref, recv_ref, send_sems, recv_sems, local_sem = rest
            sc = _Scatter(send_ref, recv_ref, send_sems, recv_sems, local_sem)
            pl.when(pl.program_id(0) == 0)(sc.start)
        else:
            g_ref, d_ref, nm_ref, nv_ref = rest
        g = p_ref[0].astype(F32)
        for k in range(1, K):
            g = g + p_ref[k].astype(F32)
        g_ref[...] = g
        nm = ADAM_B1 * m_ref[...] + (1.0 - ADAM_B1) * g
        nv = ADAM_B2 * v_ref[...] + (1.0 - ADAM_B2) * (g * g)
        nm_ref[...] = nm
        nv_ref[...] = nv
        m_hat = nm / (1.0 - ADAM_B1 ** ADAM_STEP)
        v_hat = nv / (1.0 - ADAM_B2 ** ADAM_STEP)
        d_ref[...] = -ADAM_LR * (m_hat / (jnp.sqrt(v_hat) + ADAM_EPS) + ADAM_WD * w_ref[...])
        if send is not None:
            pl.when(pl.program_id(0) == n - 1)(sc.finish)

    o = jax.ShapeDtypeStruct((R, W), F32)
    anyspec = pl.BlockSpec(memory_space=pl.ANY)
    hosted = send is not None
    return pl.pallas_call(
        body, name=name, grid=(n,),
        in_specs=[pl.BlockSpec((K, tr, W), lambda i: (0, i, 0)), _rows(tr, W), _rows(tr, W), _rows(tr, W)]
        + ([anyspec] if hosted else []),
        out_specs=[_rows(tr, W)] * 4 + ([anyspec] if hosted else []),
        out_shape=[o] * 4 + ([jax.ShapeDtypeStruct(send.shape, send.dtype)] if hosted else []),
        scratch_shapes=_COMM_SEMS if hosted else [],
        compiler_params=_cp(("arbitrary",)),
    )(*((parts, w, m, v) + ((send,) if hosted else ())))


SM_ROWS = 136
R_GMIX, R_GFFN, R_QG, R_KG, R_GA, R_GDN, R_ALOG, R_DT, R_LOSS, R_CONV, R_REL = 0, 8, 16, 24, 32, 40, 48, 49, 56, 64, 112


def _small_reduce(gathered):
    def body(p_ref, o_ref):
        s = p_ref[0]
        for k in range(1, N_DEV):
            s = s + p_ref[k]
        o_ref[...] = s
        for r0 in (R_QG, R_KG):
            rs = jnp.sum(s[r0:r0 + 4], axis=0, keepdims=True)
            o_ref[r0:r0 + 1, :] = rs + pltpu.roll(rs, DHA, 1)
        tot = jnp.sum(jnp.sum(s[R_LOSS:R_LOSS + 8], axis=0, keepdims=True), axis=1, keepdims=True)
        o_ref[R_LOSS:R_LOSS + 1, :] = jnp.broadcast_to(tot * (0.5 / D), (1, LANES))

    return pl.pallas_call(
        body, name="small_reduce",
        out_shape=jax.ShapeDtypeStruct((SM_ROWS, LANES), F32),
    )(gathered)


_WIRE = jnp.bfloat16
RA_USED, RA = 449, 464
RL = 128 + 3 * 352


def _pack_rows(parts, rows=None):
    p = jnp.concatenate([t.reshape(-1, D) for t in parts], axis=0) if len(parts) > 1 else parts[0].reshape(-1, D)
    return p if rows is None else jnp.pad(p, ((0, rows - p.shape[0]), (0, 0)))


def _unpack_rows(packed, shapes):
    out, r = [], 0
    for shp in shapes:
        nr = math.prod(shp) // D
        out.append(packed[r:r + nr].reshape(shp))
        r += nr
    return out


def _pad8(t):
    return jnp.pad(t, ((0, (-t.shape[0]) % 8), (0, 0)))


PART = 8 * LANES


def _pack_lanes(parts):
    rows = []
    for p in parts:
        f = p.reshape(-1)
        rows.append(jnp.pad(f, (0, (-f.shape[0]) % PART)).reshape(-1, LANES))
    return jnp.concatenate(rows, axis=0)


def _unpack_lanes(packed, shapes):
    out, r = [], 0
    for shp in shapes:
        n = math.prod(shp)
        nr = 8 * -(-n // PART)
        out.append(packed[r:r + nr].reshape(-1)[:n].reshape(shp))
        r += nr
    return out


def kernel(x, norm_mix_g, w_in, attn_q_norm_g, attn_k_norm_g, rel_bias, attn_out_norm_g, conv_w, a_log, dt_bias, dn_out_norm_g, w_out, norm_ffn_g, w_gate, w_up, w_down, loss_target, m_norm_mix_g, m_w_in, m_attn_q_norm_g, m_attn_k_norm_g, m_rel_bias, m_attn_out_norm_g, m_conv_w, m_a_log, m_dt_bias, m_dn_out_norm_g, m_w_out, m_norm_ffn_g, m_w_gate, m_w_up, m_w_down, v_norm_mix_g, v_w_in, v_attn_q_norm_g, v_attn_k_norm_g, v_rel_bias, v_attn_out_norm_g, v_conv_w, v_a_log, v_dt_bias, v_dn_out_norm_g, v_w_out, v_norm_ffn_g, v_w_gate, v_w_up, v_w_down):
    xs, tgt = x[0], loss_target[0]
    my_idx = 4 * lax.axis_index("x") + 2 * lax.axis_index("y") + lax.axis_index("c")
    late_w = (w_out[0], w_gate[0], w_up[0], w_down[0])

    wa_all = _all_gather(_pack_rows([w_in[0].T.astype(_MXU)], RA), "gather_w_in")
    cw_all = _all_gather(jnp.pad(conv_w[0], ((0, 4), (0, 64))), "gather_conv")
    W_in_t = wa_all[:, 0:RA_USED].reshape(N_DEV * RA_USED, D)
    W_int, W_ba = W_in_t[0:3584], W_in_t[3584:3592].T
    conv_full = cw_all[:, 0:CONV_K, 0:192].transpose(1, 0, 2).reshape(CONV_K, 1536)

    qg_t = jnp.tile(attn_q_norm_g, (1, NHA))
    kg_t = jnp.tile(attn_k_norm_g, (1, NHA))
    z4 = jnp.zeros((1, NHD), F32)
    alog8 = jnp.concatenate([z4, a_log], axis=1)
    dtb8 = jnp.concatenate([z4, dt_bias], axis=1)

    late_t = lambda ts: (ts[0], ts[1].T, ts[2].T, ts[3])
    araw, an, draw, z, ba, hb, wl_all = _inproj(xs, norm_mix_g, W_int, W_ba, qg_t, kg_t,
                                                _pack_rows([w.astype(_MXU) for w in late_t(late_w)]))
    W_out = wl_all[:, 0:128].reshape(D, D)
    tab, tabt = _bias_tables(jnp.pad(rel_bias[0].T, ((0, 0), (0, VAR0 - 257))))
    apre = _attn_fwd(an, tab)
    bat = ba.T
    dn_args = (draw, conv_full, ba, bat, alog8, dtb8, alog8.T, dtb8.T)
    u, w, kd, tm, wq, km, mq, wt, elb, cv = _dn_prep(*dn_args)
    o, vn, sn = _dn_scan(u, wq, km, elb)
    x1, mix = _post_mix(apre, o, z, xs, W_out, attn_out_norm_g, dn_out_norm_g)

    dx1, dx1b, h2, act, dgu, dyb, loss_row, dgffn = _ffn(x1, tgt, wl_all, norm_ffn_g)

    gW_out = _wgrad(mix, dx1b, "wgrad_out", out_dtype=_WIRE)
    gW_gu_t = _wgrad(h2, dgu, "wgrad_gate_up", tn=FF, out_dtype=_WIRE, transposed=True)
    gW_down = _wgrad(dyb, act, "wgrad_down", out_dtype=_WIRE, transposed=True)
    send_late = jnp.concatenate(
        [gW_out.reshape(N_DEV, 128, D), gW_gu_t[0:FF].reshape(N_DEV, 352, D), gW_gu_t[FF:].reshape(N_DEV, 352, D),
         gW_down.reshape(N_DEV, 352, D)], axis=1)

    dap, do, dz, dga, dgdn = _mix_bwd(dx1b, W_out, apre, o, z, attn_out_norm_g, dn_out_norm_g)
    dqn, dkn, dv, dtabt, recv_late = _attn_bwd(an, dap, tabt, send_late)
    drel = _bias_grad(dtabt)
    du, dw, dqd, dkd, dgx = _dn_scan_bwd(do, mq, kd, wt, sn, vn, elb)
    ddraw, dba, sm, dcw = _dn_post_bwd(draw, cv, *dn_args[1:], du, dw, dqd, dkd, dgx, do, vn, tm, u, w)
    gx, dproj, dgmix, dqg, dkg = _inproj_bwd(dqn, dkn, dv, araw, ddraw, dz, dba, xs, dx1, W_int, W_ba, norm_mix_g,
                                             qg_t, kg_t)

    gW_in_t = _wgrad(hb, dproj, "wgrad_in", tk=512, out_dtype=_WIRE, transposed=True)
    send_in = jnp.pad(gW_in_t[0:N_DEV * RA_USED].reshape(N_DEV, RA_USED, D), ((0, 0), (0, RA - RA_USED), (0, 0)))
    late_m = (m_w_out[0], m_w_gate[0], m_w_up[0], m_w_down[0])
    late_v = (v_w_out[0], v_w_gate[0], v_w_up[0], v_w_down[0])
    *outs_late, recv_in = _adamw(recv_late, _pack_rows(late_t(late_w)), _pack_rows(late_t(late_m)),
                                 _pack_rows(late_t(late_v)), "adamw_late", 32, send=send_in)
    outs_in = _adamw(recv_in, _pack_rows([w_in[0].T], RA), _pack_rows([m_w_in[0].T], RA),
                     _pack_rows([v_w_in[0].T], RA), "adamw_w_in", 16)
    late_t_shapes = [t.shape for t in late_t(late_w)]
    big = [[a[0:RA_USED].T] + list(late_t(_unpack_rows(b, late_t_shapes))) for a, b in zip(outs_in, outs_late)]
    bg, bd_, bm, bv = big

    partial = jnp.concatenate(
        [dgmix.reshape(8, LANES), dgffn.reshape(8, LANES), _pad8(dqg.reshape(4, LANES)), _pad8(dkg.reshape(4, LANES)),
         _pad8(dga.reshape(4, LANES)), _pad8(dgdn), sm, loss_row.reshape(8, LANES),
         dcw[0:CONV_K].reshape(48, LANES), drel.reshape(24, LANES)], axis=0)
    S = _small_reduce(_all_gather(partial, "gather_small"))
    loss = S[R_LOSS, 0]
    g_conv = lax.dynamic_slice(S[R_CONV:R_CONV + 48].reshape(CONV_K, 1536), (0, 192 * my_idx), (CONV_K, 192))
    sg = [S[R_GMIX:R_GMIX + 8].reshape(1, D), S[R_QG:R_QG + 1, 0:DHA], S[R_KG:R_KG + 1, 0:DHA],
          S[R_REL:R_REL + 24].reshape(NHA, 384)[:, 0:257].T, S[R_GA:R_GA + 4].reshape(1, AW), g_conv,
          S[R_ALOG:R_ALOG + 1, NHD:2 * NHD], S[R_DT:R_DT + 1, NHD:2 * NHD], S[R_GDN:R_GDN + 1], S[R_GFFN:R_GFFN + 8].reshape(1, D)]
    sw = [norm_mix_g, attn_q_norm_g, attn_k_norm_g, rel_bias[0], attn_out_norm_g, conv_w[0], a_log, dt_bias, dn_out_norm_g, norm_ffn_g]
    smm = [m_norm_mix_g, m_attn_q_norm_g, m_attn_k_norm_g, m_rel_bias[0], m_attn_out_norm_g, m_conv_w[0], m_a_log, m_dt_bias, m_dn_out_norm_g, m_norm_ffn_g]
    svv = [v_norm_mix_g, v_attn_q_norm_g, v_attn_k_norm_g, v_rel_bias[0], v_attn_out_norm_g, v_conv_w[0], v_a_log, v_dt_bias, v_dn_out_norm_g, v_norm_ffn_g]
    s_shapes = [t.shape for t in sw]
    pg = _pack_lanes(sg)
    s_out = _adamw(pg[None], _pack_lanes(sw), _pack_lanes(smm), _pack_lanes(svv), "adamw_small", pg.shape[0])
    s_g, s_d, s_m, s_v = (_unpack_lanes(t, s_shapes) for t in s_out)

    lead = lambda t: t[None]
    def ordered(small, big):
        nm, q, k, rel, ao, cw, al, dtb, dno, nf = small
        wi, wo, wgt, wu, wdn = big
        return [nm, lead(wi), q, k, lead(rel), ao, lead(cw), al, dtb, dno, lead(wo), nf, lead(wgt), lead(wu), lead(wdn)]
    outs = [loss, gx[None]]
    for small, big in ((s_g, bg), (s_d, bd_), (s_m, bm), (s_v, bv)):
        outs += ordered(small, big)
    return tuple(outs)
```

```python
import math

import jax
import jax.numpy as jnp
from jax import lax
from jax.experimental import pallas as pl
from jax.experimental.pallas import tpu as pltpu

F32 = jnp.float32
BF16 = jnp.bfloat16
_MXU = jnp.bfloat16

D = 1024
AW = 512
NHA = 8
DHA = 64
CH = 64
NHD = 4
DHD = 128
DW = 512
FF = 2816
EPS = 1e-6
NEG = -1e30
N_DEV = 8
LANES = 128
VMEM_LIMIT = 56 * 1024 * 1024

ADAM_LR = 0.001
ADAM_B1 = 0.9
ADAM_B2 = 0.999
ADAM_EPS = 1e-08
ADAM_WD = 0.01
ADAM_STEP = 10

MESH_T = pl.DeviceIdType.MESH


def _cp(sem=None, vmem=VMEM_LIMIT):
    kw = dict(vmem_limit_bytes=vmem)
    if sem is not None:
        kw["dimension_semantics"] = sem
    return pltpu.CompilerParams(**kw)


def _dot(a, b):
    return jnp.dot(a.astype(_MXU), b.astype(_MXU), preferred_element_type=F32)


def _dot_nt(a, b):
    return lax.dot_general(a.astype(_MXU), b.astype(_MXU), (((1,), (1,)), ((), ())), preferred_element_type=F32)


def _dot_tn(a, b):
    return lax.dot_general(a.astype(_MXU), b.astype(_MXU), (((0,), (0,)), ((), ())), preferred_element_type=F32)


def _iota(shape, dim):
    return lax.broadcasted_iota(jnp.int32, shape, dim)


def _block_ones(n, blk, dtype=BF16):
    r, c = _iota((n, n), 0), _iota((n, n), 1)
    return jnp.where((r // blk) == (c // blk), 1.0, 0.0).astype(dtype)


def _sigmoid(x):
    return 1.0 / (1.0 + jnp.exp(-x))


def _softplus(x):
    return jnp.maximum(x, 0.0) + jnp.log(1.0 + jnp.exp(-jnp.abs(x)))


def _col(x, k):
    lane = _iota(x.shape, 1)
    return jnp.sum(jnp.where(lane == k, x, 0.0), axis=1, keepdims=True)


def _row(x, k):
    sub = _iota(x.shape, 0)
    return jnp.sum(jnp.where(sub == k, x, 0.0), axis=0, keepdims=True)


def _my_pos():
    return lax.axis_index("x"), lax.axis_index("y"), lax.axis_index("c")


def _all_gather(x2d, name):
    R, W = x2d.shape

    def body(x_ref, out_ref, send_sems, recv_sems, local_sem):
        ag = _Gather(x_ref, out_ref, send_sems, recv_sems, local_sem)
        ag.start()
        ag.forward()
        ag.finish()

    return pl.pallas_call(
        body, name=name,
        out_shape=jax.ShapeDtypeStruct((N_DEV, R, W), x2d.dtype),
        in_specs=[pl.BlockSpec(memory_space=pl.ANY)],
        out_specs=pl.BlockSpec(memory_space=pl.ANY),
        scratch_shapes=_COMM_SEMS,
    )(x2d)


_COMM_SEMS = [pltpu.SemaphoreType.DMA((7,)), pltpu.SemaphoreType.DMA((7,)), pltpu.SemaphoreType.DMA]


class _Gather:
    def __init__(self, x_ref, out_ref, send_sems, recv_sems, local_sem):
        x, y, c = _my_pos()
        me, sibling = (x, y, c), (x, y, 1 - c)
        chips = [(1 - x, y), (x, 1 - y), (1 - x, 1 - y)]

        def slot(px, py, pc):
            return out_ref.at[4 * px + 2 * py + pc]

        def copy(k, block, to, src=None):
            return pltpu.make_async_remote_copy(
                src_ref=slot(*block) if src is None else src, dst_ref=slot(*block),
                send_sem=send_sems.at[k], recv_sem=recv_sems.at[k], device_id=to, device_id_type=MESH_T)

        self.mine = pltpu.make_async_copy(x_ref, slot(*me), local_sem)
        self.first = [copy(0, me, sibling, src=x_ref)]
        self.first += [copy(1 + j, me, (*chip, c), src=x_ref) for j, chip in enumerate(chips)]
        self.passed = [copy(4 + j, (*chip, c), sibling) for j, chip in enumerate(chips)]
        self.from_chips = [copy(1 + j, (*chip, c), me) for j, chip in enumerate(chips)]
        self.from_sibling = [copy(0, sibling, me)] + [copy(4 + j, (*chip, 1 - c), me) for j, chip in enumerate(chips)]

    def start(self):
        self.mine.start()
        for cp in self.first:
            cp.start()

    def forward(self):
        for arrived, onward in zip(self.from_chips, self.passed):
            arrived.wait_recv()
            onward.start()

    def finish(self):
        for cp in self.from_sibling:
            cp.wait_recv()
        for cp in self.first + self.passed:
            cp.wait_send()
        self.mine.wait()


class _Scatter:
    def __init__(self, s_ref, r_ref, send_sems, recv_sems, local_sem):
        x, y, c = _my_pos()
        self.mine = pltpu.make_async_copy(s_ref.at[4 * x + 2 * y + c], r_ref.at[0], local_sem)
        self.copies = []
        for m in range(1, N_DEV):
            px = x ^ ((m >> 2) & 1)
            py = y ^ ((m >> 1) & 1)
            pc = c ^ (m & 1)
            self.copies.append(pltpu.make_async_remote_copy(
                src_ref=s_ref.at[4 * px + 2 * py + pc], dst_ref=r_ref.at[m],
                send_sem=send_sems.at[m - 1], recv_sem=recv_sems.at[m - 1],
                device_id=(px, py, pc), device_id_type=MESH_T))

    def start(self):
        self.mine.start()
        for cp in self.copies:
            cp.start()

    def finish(self):
        for cp in self.copies:
            cp.wait_recv()
        for cp in self.copies:
            cp.wait_send()
        self.mine.wait()


TM = 512
TF = 256
TG = 512


def _full(shape):
    nd = len(shape)
    return pl.BlockSpec(shape, lambda i: (0,) * nd)


def _rows(tm, w):
    return pl.BlockSpec((tm, w), lambda i: (i, 0))


def _head_sum(x, bd):
    one_pass = lambda t: jnp.dot(t.astype(_MXU), bd.astype(_MXU), preferred_element_type=F32)
    return jnp.concatenate([one_pass(x[:, 0:256]), one_pass(x[:, 256:512])], axis=1)


def _head_rms(x, bd, width):
    return lax.rsqrt(_head_sum(x * x, bd) * (1.0 / width) + EPS)


def _inproj(x, g_mix, w_int, w_ba, qg_t, kg_t, later_w):
    T = x.shape[0]
    nt = T // TM
    ntd = (((1,), (1,)), ((), ()))

    def body(x_ref, g_ref, w_ref, wba_ref, qg_ref, kg_ref, lw_ref, araw_ref, an_ref, draw_ref, z_ref, ba_ref, h_ref,
             lw_all, send_sems, recv_sems, local_sem):
        i = pl.program_id(0)
        ag = _Gather(lw_ref, lw_all, send_sems, recv_sems, local_sem)
        pl.when(i == 0)(ag.start)
        pl.when(i == nt // 2)(ag.forward)
        xv = x_ref[...]
        r = lax.rsqrt(jnp.mean(xv * xv, axis=1, keepdims=True) + EPS)
        h = (xv * r * g_ref[...]).astype(_MXU)
        h_ref[...] = h
        proj = lambda lo, hi: lax.dot_general(h, w_ref[lo:hi, :], ntd, preferred_element_type=F32)
        q, k, v = proj(0, AW), proj(AW, 2 * AW), proj(2 * AW, 3 * AW)
        draw_ref[...] = proj(1536, 3072)
        z_ref[...] = proj(3072, 3584)
        ba_ref[...] = jnp.dot(h, wba_ref[...], preferred_element_type=F32)
        araw_ref[:, 0:AW] = q
        araw_ref[:, AW:2 * AW] = k
        araw_ref[:, 2 * AW:3 * AW] = v
        bd = _block_ones(AW // 2, DHA)
        qn = q * _head_rms(q, bd, DHA) * (qg_ref[...] * (DHA ** -0.5))
        kn = k * _head_rms(k, bd, DHA) * kg_ref[...]
        an_ref[:, 0:AW] = qn.astype(_MXU)
        an_ref[:, AW:2 * AW] = kn.astype(_MXU)
        an_ref[:, 2 * AW:3 * AW] = v.astype(_MXU)
        pl.when(i == nt - 1)(ag.finish)

    anyspec = pl.BlockSpec(memory_space=pl.ANY)
    return pl.pallas_call(
        body, name="inproj", grid=(nt,),
        in_specs=[_rows(TM, D), _full((1, D)), _full((3584, D)), _full((D, 8)), _full((1, AW)), _full((1, AW)),
                  anyspec],
        out_specs=[_rows(TM, 1536), _rows(TM, 1536), _rows(TM, 1536), _rows(TM, DW), _rows(TM, 8), _rows(TM, D),
                   anyspec],
        out_shape=[jax.ShapeDtypeStruct((T, 1536), F32), jax.ShapeDtypeStruct((T, 1536), _MXU),
                   jax.ShapeDtypeStruct((T, 1536), F32), jax.ShapeDtypeStruct((T, DW), F32),
                   jax.ShapeDtypeStruct((T, 8), F32), jax.ShapeDtypeStruct((T, D), _MXU),
                   jax.ShapeDtypeStruct((N_DEV,) + later_w.shape, later_w.dtype)],
        scratch_shapes=_COMM_SEMS,
        compiler_params=_cp(("arbitrary",)),
    )(x, g_mix, w_int, w_ba, qg_t, kg_t, later_w)


TQ = 256
TW = 768
VAR0 = 384
TOEP = 1024


def _bias_tables(rb_t, w_shard, c_shard):
    def body(rb_ref, w_ref, c_ref, tab_ref, tabt_ref, w_all, c_all, ws, wr, wl, cs, cr, cl):
        h = pl.program_id(0)
        gathers = [_Gather(w_ref, w_all, ws, wr, wl), _Gather(c_ref, c_all, cs, cr, cl)]

        @pl.when(h == 0)
        def _():
            for ag in gathers:
                ag.start()

        @pl.when(h == NHA // 2)
        def _():
            for ag in gathers:
                ag.forward()

        rb8 = jnp.broadcast_to(_row(rb_ref[...], h), (8, VAR0))
        n = _iota((VAR0, TOEP), 1)
        t = _iota((VAR0, TOEP), 0)

        def line(m):
            onehot = jnp.where(jnp.clip(512 - m, -128, 128) + 128 == t, 1.0, 0.0).astype(BF16)
            return sum(jnp.dot(p, onehot, preferred_element_type=F32) for p in _split3(rb8))[0:1, :]

        def band(r, j, first_key):
            return ((j >> 6) >= (r >> 6)) & ((j >> 6) <= (r >> 6) + 8) & (j >= first_key)

        g = line(jnp.where(n < TW, n, n - TOEP))
        tab = pltpu.roll(jnp.broadcast_to(g, (TQ, TOEP)), 0, 1, stride=1, stride_axis=0)[:, 0:TW]
        gt = line(jnp.where(n < TQ, -n, TOEP - n))
        tabt = pltpu.roll(jnp.broadcast_to(gt, (TW, TOEP)), 0, 1, stride=1, stride_axis=0)[:, 0:TQ]
        for v in range(3):
            first_key = max(512 - TQ * v, 0)
            tab_ref[v, 0] = jnp.where(band(_iota((TQ, TW), 0), _iota((TQ, TW), 1), first_key), tab, NEG)
            tabt_ref[v, 0] = jnp.where(band(_iota((TW, TQ), 1), _iota((TW, TQ), 0), first_key), tabt, NEG)

        @pl.when(h == NHA - 1)
        def _():
            for ag in gathers:
                ag.finish()

    anyspec = pl.BlockSpec(memory_space=pl.ANY)
    return pl.pallas_call(
        body, name="bias_tables", grid=(NHA,),
        in_specs=[_full((NHA, VAR0)), anyspec, anyspec],
        out_specs=[pl.BlockSpec((3, 1, TQ, TW), lambda h: (0, h, 0, 0)),
                   pl.BlockSpec((3, 1, TW, TQ), lambda h: (0, h, 0, 0)), anyspec, anyspec],
        out_shape=[jax.ShapeDtypeStruct((3, NHA, TQ, TW), F32), jax.ShapeDtypeStruct((3, NHA, TW, TQ), F32),
                   jax.ShapeDtypeStruct((N_DEV,) + w_shard.shape, w_shard.dtype),
                   jax.ShapeDtypeStruct((N_DEV,) + c_shard.shape, c_shard.dtype)],
        scratch_shapes=_COMM_SEMS + _COMM_SEMS,
        compiler_params=_cp(("arbitrary",)),
    )(rb_t, w_shard, c_shard)


def _bias_grad(dtabt):
    def body(d_ref, o_ref):
        a, b = _iota((TQ, TQ), 0), _iota((TQ, TQ), 1)
        anti = jnp.where(a + b == TQ - 1, 1.0, 0.0).astype(BF16)
        drev = sum(jnp.dot(t, anti, preferred_element_type=F32) for t in _split3(d_ref[0]))
        wide = jnp.concatenate([drev, jnp.zeros((TW, TOEP - TQ), F32)], axis=1)
        cols = jnp.sum(pltpu.roll(wide, 0, 1, stride=1, stride_axis=0), axis=0, keepdims=True)
        c = _iota((TOEP, VAR0), 0)
        idx = jnp.clip(512 + TQ - 1 - c, -128, 128) + 128
        onehot = jnp.where(idx == _iota((TOEP, VAR0), 1), 1.0, 0.0).astype(BF16)
        cols8 = jnp.broadcast_to(cols, (8, TOEP))
        o_ref[0] = sum(jnp.dot(t, onehot, preferred_element_type=F32) for t in _split3(cols8))[0:1, :]

    return pl.pallas_call(
        body, name="bias_grad", grid=(NHA,),
        in_specs=[pl.BlockSpec((1, TW, TQ), lambda h: (h, 0, 0))],
        out_specs=pl.BlockSpec((1, 1, VAR0), lambda h: (h, 0, 0)),
        out_shape=jax.ShapeDtypeStruct((NHA, 1, VAR0), F32),
        compiler_params=_cp(("arbitrary",)),
    )(dtabt)


def _kv_spec(col, back):
    return pl.BlockSpec((TQ, AW), lambda i: (jnp.maximum(i - back, 0), col))


def _attn_fwd(an, tab):
    T = an.shape[0]

    def body(q_ref, k2_ref, k1_ref, k0_ref, v2_ref, v1_ref, v0_ref, tab_ref, o_ref):
        i = pl.program_id(0)
        kwin = jnp.concatenate([k2_ref[...], k1_ref[...], k0_ref[...]], axis=0)
        vwin = jnp.concatenate([v2_ref[...], v1_ref[...], v0_ref[...]], axis=0)
        q = q_ref[...]
        lo_half = _iota((TQ, LANES), 1) < DHA

        def scores(h):
            sl = slice(LANES * (h // 2), LANES * (h // 2 + 1))
            mask = lo_half if h % 2 == 0 else jnp.logical_not(lo_half)
            qm = jnp.where(mask, q[:, sl], jnp.zeros((TQ, LANES), q.dtype))
            return _dot_nt(qm, kwin[:, sl]) + tab_ref[0, h]

        s_next = scores(0)
        outs = []
        for h in range(NHA):
            s = s_next
            if h + 1 < NHA:
                s_next = scores(h + 1)
            sl = slice(LANES * (h // 2), LANES * (h // 2 + 1))
            m = jnp.max(s, axis=1, keepdims=True)
            e = jnp.exp(s - m)
            l = jnp.sum(e, axis=1, keepdims=True)
            outs.append(_dot(e, vwin[:, sl]) / l)
            if h % 2 == 1:
                o_ref[:, sl] = jnp.where(lo_half, outs[h - 1], outs[h])

    return pl.pallas_call(
        body, name="attn_fwd", grid=(T // TQ,),
        in_specs=[pl.BlockSpec((TQ, AW), lambda i: (i, 0)),
                  _kv_spec(1, 2), _kv_spec(1, 1), _kv_spec(1, 0), _kv_spec(2, 2), _kv_spec(2, 1), _kv_spec(2, 0),
                  pl.BlockSpec((1, NHA, TQ, TW), lambda i: (jnp.minimum(i, 2), 0, 0, 0))],
        out_specs=_rows(TQ, AW),
        out_shape=jax.ShapeDtypeStruct((T, AW), F32),
        compiler_params=_cp(("arbitrary",)),
    )(an, an, an, an, an, an, an, tab)


def _attn_bwd(an, dout, tabt, send):
    T = an.shape[0]
    nq = T // TQ

    def qi(i):
        return jnp.minimum(i, nq - 1)

    def kv_spec(col, back):
        return pl.BlockSpec((TQ, AW), lambda i: (jnp.maximum(qi(i) - back, 0), col))

    def body(q_ref, do_ref, k2_ref, k1_ref, k0_ref, v2_ref, v1_ref, v0_ref, tabt_ref, send_ref,
             dq_ref, dk_ref, dv_ref, dtab_ref, recv_ref, dk_acc, dv_acc, send_sems, recv_sems, local_sem):
        i = pl.program_id(0)
        sc = _Scatter(send_ref, recv_ref, send_sems, recv_sems, local_sem)
        pl.when(i == 0)(sc.start)

        @pl.when(i == 0)
        def _():
            dtab_ref[...] = jnp.zeros_like(dtab_ref)

        new = i % 3
        dk_acc[new] = jnp.zeros((TQ, AW), F32)
        dv_acc[new] = jnp.zeros((TQ, AW), F32)

        @pl.when(i < nq)
        def _():
            kwin = jnp.concatenate([k2_ref[...], k1_ref[...], k0_ref[...]], axis=0)
            vwin = jnp.concatenate([v2_ref[...], v1_ref[...], v0_ref[...]], axis=0)
            q = q_ref[...]
            do = do_ref[...].astype(_MXU)
            lo_half = _iota((TQ, LANES), 1) < DHA

            def front(h):
                sl = slice(LANES * (h // 2), LANES * (h // 2 + 1))
                mask = lo_half if h % 2 == 0 else jnp.logical_not(lo_half)
                zero = jnp.zeros((TQ, LANES), q.dtype)
                qm = jnp.where(mask, q[:, sl], zero)
                dom = jnp.where(mask, do[:, sl], zero)
                st = _dot_nt(kwin[:, sl], qm) + tabt_ref[0, h]
                return st, _dot_nt(vwin[:, sl], dom), qm, dom, mask

            pairs = {}

            def back(h, ptb, dsb, qm, dom, mask):
                sl = slice(LANES * (h // 2), LANES * (h // 2 + 1))
                dv = _dot(ptb, dom)
                dk = _dot(dsb, qm)
                dq = jnp.where(mask, _dot_tn(dsb, kwin[:, sl]), 0.0)
                if h % 2 == 0:
                    pairs[h // 2] = (dq, dk, dv)
                    return
                dq0, dk0, dv0 = pairs.pop(h // 2)
                dq_ref[:, sl] = dq0 + dq
                dk_pair, dv_pair = dk0 + dk, dv0 + dv
                for w in range(3):
                    slot = (i + 1 + w) % 3
                    rows = slice(TQ * w, TQ * (w + 1))
                    dk_acc[slot, :, sl] += dk_pair[rows]
                    dv_acc[slot, :, sl] += dv_pair[rows]

            nxt = front(0)
            pending = None
            for h in range(NHA):
                st, dpt, qm, dom, mask = nxt
                if h + 1 < NHA:
                    nxt = front(h + 1)
                m = jnp.max(st, axis=0, keepdims=True)
                e = jnp.exp(st - m)
                pt = e * (1.0 / jnp.sum(e, axis=0, keepdims=True))
                delta = jnp.sum(pt * dpt, axis=0, keepdims=True)
                dst = pt * (dpt - delta)
                dtab_ref[h] += dst
                if pending is not None:
                    back(*pending)
                pending = (h, pt.astype(_MXU), dst.astype(_MXU), qm, dom, mask)
            back(*pending)

        @pl.when(i >= 2)
        def _():
            done = (i + 1) % 3
            dk_ref[...] = dk_acc[done]
            dv_ref[...] = dv_acc[done]

        pl.when(i == nq + 1)(sc.finish)

    back2 = pl.BlockSpec((TQ, AW), lambda i: (jnp.maximum(i - 2, 0), 0))
    anyspec = pl.BlockSpec(memory_space=pl.ANY)
    return pl.pallas_call(
        body, name="attn_bwd", grid=(nq + 2,),
        in_specs=[pl.BlockSpec((TQ, AW), lambda i: (qi(i), 0)), pl.BlockSpec((TQ, AW), lambda i: (qi(i), 0)),
                  kv_spec(1, 2), kv_spec(1, 1), kv_spec(1, 0), kv_spec(2, 2), kv_spec(2, 1), kv_spec(2, 0),
                  pl.BlockSpec((1, NHA, TW, TQ), lambda i: (jnp.minimum(i, 2), 0, 0, 0)), anyspec],
        out_specs=[pl.BlockSpec((TQ, AW), lambda i: (qi(i), 0)), back2, back2, _full((NHA, TW, TQ)), anyspec],
        out_shape=[jax.ShapeDtypeStruct((T, AW), F32), jax.ShapeDtypeStruct((T, AW), F32),
                   jax.ShapeDtypeStruct((T, AW), F32), jax.ShapeDtypeStruct((NHA, TW, TQ), F32),
                   jax.ShapeDtypeStruct(send.shape, send.dtype)],
        scratch_shapes=[pltpu.VMEM((3, TQ, AW), F32), pltpu.VMEM((3, TQ, AW), F32)] + _COMM_SEMS,
        compiler_params=_cp(("arbitrary",)),
    )(an, dout, an, an, an, an, an, an, tabt, send)


GR = 128
NG = TG // GR
CPT = TG // CH
CONV_K = 4


def _split3(x):
    a = x.astype(BF16)
    r = x - a.astype(F32)
    b = r.astype(BF16)
    c = (r - b.astype(F32)).astype(BF16)
    return a, b, c


def _ones_dot(ones_b, x):
    return sum(jnp.dot(ones_b, t, preferred_element_type=F32) for t in _split3(x))


def _dot_ones_nt(x, ones_b):
    dn = (((1,), (1,)), ((), ()))
    return sum(lax.dot_general(t, ones_b, dn, preferred_element_type=F32) for t in _split3(x))


def _dn_masks():
    r, c = _iota((GR, GR), 0), _iota((GR, GR), 1)
    same = (r >> 6) == (c >> 6)
    one = lambda m: jnp.where(m, 1.0, 0.0).astype(BF16)
    return dict(
        tril=same & (c <= r), strict=same & (c < r), triu=same & (c >= r), strict_u=same & (c > r),
        tril_b=one(same & (c <= r)), triu_b=one(same & (c >= r)), blk_b=one(same), eye_b=one(r == c),
        eye=jnp.where(r == c, 1.0, 0.0).astype(F32),
        fold_b=one((_iota((GR, CH), 0) & (CH - 1)) == _iota((GR, CH), 1)),
        last=(_iota((GR, 1), 0) & (CH - 1)) == CH - 1,
    )


def _shift_down(x, halo, k):
    if k == 0:
        return x
    xs = pltpu.roll(x, k, 0)
    hs = pltpu.roll(halo, k, 0)
    top = jnp.where(_iota(halo.shape, 0) < k, hs, xs[0:8])
    return jnp.concatenate([top, xs[8:]], axis=0)


def _shift_up(x, halo, k):
    if k == 0:
        return x
    n = x.shape[0]
    xs = pltpu.roll(x, n - k, 0)
    hs = pltpu.roll(halo, 8 - k, 0)
    bot = jnp.where(_iota(halo.shape, 0) >= 8 - k, hs, xs[n - 8:n])
    return jnp.concatenate([xs[0:n - 8], bot], axis=0)


def _conv(x, halo, w):
    y = x * w[CONV_K - 1:CONV_K, :]
    for k in range(1, CONV_K):
        y = y + _shift_down(x, halo, k) * w[CONV_K - 1 - k:CONV_K - k, :]
    return y


def _tri_inv(lmats, eye):
    ps = [-m for m in lmats]
    rs = [eye + p for p in ps]
    for _ in range(5):
        ps = [_dot(p, p) for p in ps]
        rs = [r + _dot(r, p) for r, p in zip(rs, ps)]
    return rs


def _gate_terms(ba_g, bat_g, alog8, dtb8, alog8t, dtb8t, K):
    g8 = -jnp.exp(alog8) * _softplus(ba_g + dtb8)
    g8t = -jnp.exp(alog8t) * _softplus(bat_g + dtb8t)
    gc8 = _ones_dot(K["tril_b"], g8)
    gl8 = _ones_dot(K["blk_b"], g8)
    gcrow8 = _dot_ones_nt(g8t, K["tril_b"])
    return g8, gc8, gl8, gcrow8


def _dn_heads(c_tile, rows, beta8, gc8, gl8, gcrow8, K, pre=None):
    return _dn_heads_groups(c_tile, [(rows, beta8, gc8, gl8, gcrow8)], K, None if pre is None else [pre])[0]


def _dn_heads_groups(c_tile, groups, K, pres=None):
    ds = [_dn_head_vec(c_tile, rows, h, beta8, gc8, gl8, gcrow8, K)
          for rows, beta8, gc8, gl8, gcrow8 in groups for h in range(NHD)]
    pls = [_dot_nt(d["kb"], d["kn"]) for d in ds]
    pms = [_dot_nt(d["qn"], d["kn"]) for d in ds]
    for d, pl_, pm in zip(ds, pls, pms):
        d.update(pl=pl_, pm=pm, lmat=jnp.where(K["strict"], pl_ * d["gam_m"], 0.0), mm=pm * d["gam_m"])
    if pres is None:
        for d, tm in zip(ds, _tri_inv([d["lmat"] for d in ds], K["eye"])):
            d.update(tm=tm, u=_dot(tm, d["vb"]), w=_dot(tm, d["kg"]))
    else:
        for d, (tm, u, w) in zip(ds, [p for pre in pres for p in pre]):
            d.update(tm=tm, u=u, w=w)
    return [ds[NHD * k:NHD * (k + 1)] for k in range(len(groups))]


def _dn_head_vec(c_tile, rows, h, beta8, gc8, gl8, gcrow8, K):
    qr = c_tile[rows, DHD * h:DHD * (h + 1)]
    kr = c_tile[rows, DW + DHD * h:DW + DHD * (h + 1)]
    v = c_tile[rows, 2 * DW + DHD * h:2 * DW + DHD * (h + 1)]
    rq = lax.rsqrt(jnp.sum(qr * qr, axis=1, keepdims=True) + EPS)
    rk = lax.rsqrt(jnp.sum(kr * kr, axis=1, keepdims=True) + EPS)
    qh, kn = qr * rq, kr * rk
    qn = qh * (DHD ** -0.5)
    beta = _col(beta8, h)
    gccol, glcol, gcrow = _col(gc8, NHD + h), _col(gl8, NHD + h), _row(gcrow8, NHD + h)
    diff = gccol - gcrow
    gam_m = jnp.exp(jnp.where(K["tril"], diff, NEG))
    gam = jnp.exp(gccol)
    egl = jnp.exp(glcol - gccol)
    kb, vb = kn * beta, v * beta
    kg = kb * gam
    return dict(qr=qr, kr=kr, v=v, rq=rq, rk=rk, qh=qh, qn=qn, kn=kn, beta=beta, diff=diff, gam_m=gam_m, gam=gam,
                egl=egl, el=jnp.exp(glcol), kb=kb, vb=vb, kg=kg, qd=qn * gam, kd=kn * egl)


def _halo_prev(width):
    return pl.BlockSpec((8, width), lambda i: (jnp.maximum(i * (TG // 8) - 1, 0), 0))


def _dn_prep(draw, conv_w, ba, bat, alog8, dtb8, alog8t, dtb8t):
    T = draw.shape[0]
    nb = T // TG
    hm = lambda w, dt: jax.ShapeDtypeStruct((NHD, T, w), dt)
    hm_spec = lambda w: pl.BlockSpec((NHD, TG, w), lambda i: (0, i, 0))
    pc = lambda r, c: jax.ShapeDtypeStruct((NHD, T // CH, r, c), _MXU)
    pc_spec = lambda r, c: pl.BlockSpec((NHD, CPT, r, c), lambda i: (0, i, 0, 0))

    def body(x_ref, halo_ref, cw_ref, ba_ref, bat_ref, al_ref, dt_ref, alt_ref, dtt_ref,
             u_ref, w_ref, kd_ref, tm_ref, wq_ref, km_ref, mq_ref, wt_ref, elb_ref, cv_ref):
        i = pl.program_id(0)
        K = _dn_masks()
        halo = jnp.where(i > 0, halo_ref[...], 0.0)
        cv = _conv(x_ref[...], halo, cw_ref[...])
        cv_ref[...] = cv
        c_tile = cv * _sigmoid(cv)
        eye128 = jnp.where(_iota((DHD, DHD), 0) == _iota((DHD, DHD), 1), 1.0, 0.0).astype(_MXU)
        def gate_inputs(g):
            rows = slice(GR * g, GR * (g + 1))
            ba_g = ba_ref[rows, :]
            _, gc8, gl8, gcrow8 = _gate_terms(ba_g, bat_ref[:, rows], al_ref[...], dt_ref[...], alt_ref[...],
                                              dtt_ref[...], K)
            return rows, _sigmoid(ba_g), gc8, gl8, gcrow8

        def store(g, rows, ds):
            mmts = [_dot_nt(d["kn"], d["qn"]) * jnp.exp(jnp.where(K["triu"], -d["diff"], NEG)) for d in ds]
            mcs = [_dot(d["mm"], K["fold_b"]) for d in ds]
            mcts = [_dot(m, K["fold_b"]) for m in mmts]
            for h, d in enumerate(ds):
                tm_ref[h, rows, :] = d["tm"].astype(_MXU)
                u_ref[h, rows, :] = d["u"]
                w_ref[h, rows, :] = d["w"].astype(_MXU)
                kd_ref[h, rows, :] = d["kd"].astype(_MXU)
                elb = jnp.broadcast_to(d["el"], (GR, DHD))
                for cc in range(GR // CH):
                    ch = slice(CH * cc, CH * (cc + 1))
                    n = (GR // CH) * g + cc
                    wq_ref[h, n, 0:CH, :] = d["w"][ch].astype(_MXU)
                    wq_ref[h, n, CH:2 * CH, :] = d["qd"][ch].astype(_MXU)
                    km_ref[h, n, 0:DHD, :] = _dot_nt(eye128, d["kd"][ch]).astype(_MXU)
                    km_ref[h, n, DHD:DHD + CH, :] = mcs[h][ch].astype(_MXU)
                    mq_ref[h, n, 0:CH, :] = mcts[h][ch].astype(_MXU)
                    mq_ref[h, n, CH:CH + DHD, :] = _dot_nt(eye128, d["qd"][ch]).astype(_MXU)
                    wt_ref[h, n] = _dot_nt(eye128, d["w"][ch]).astype(_MXU)
                    elb_ref[n:n + 1, DHD * h:DHD * (h + 1)] = elb[CH * cc:CH * cc + 1, :]

        PAIR = 4
        for g0 in range(0, NG, PAIR):
            pair = [gate_inputs(g) for g in range(g0, g0 + PAIR)]
            for k, ds in enumerate(_dn_heads_groups(c_tile, pair, K)):
                store(g0 + k, pair[k][0], ds)

    return pl.pallas_call(
        body, name="dn_prep", grid=(nb,),
        in_specs=[_rows(TG, 1536), _halo_prev(1536), _full((CONV_K, 1536)), _rows(TG, 8),
                  pl.BlockSpec((8, TG), lambda i: (0, i)), _full((1, 8)), _full((1, 8)), _full((8, 1)), _full((8, 1))],
        out_specs=[hm_spec(DHD), hm_spec(DHD), hm_spec(DHD), hm_spec(GR),
                   pc_spec(2 * CH, DHD), pc_spec(DHD + CH, CH), pc_spec(CH + DHD, CH), pc_spec(DHD, CH),
                   pl.BlockSpec((CPT, NHD * DHD), lambda i: (i, 0)), _rows(TG, 1536)],
        out_shape=[hm(DHD, F32), hm(DHD, _MXU), hm(DHD, _MXU), hm(GR, _MXU),
                   pc(2 * CH, DHD), pc(DHD + CH, CH), pc(CH + DHD, CH), pc(DHD, CH),
                   jax.ShapeDtypeStruct((T // CH, NHD * DHD), F32), jax.ShapeDtypeStruct((T, 1536), F32)],
        compiler_params=_cp(("arbitrary",)),
    )(draw, draw, conv_w, ba, bat, alog8, dtb8, alog8t, dtb8t)


def _dn_scan(u, wq, km, elb):
    T = u.shape[1]
    nb = T // TG
    hm_spec = lambda wd: pl.BlockSpec((NHD, TG, wd), lambda i: (0, i, 0))

    def body(u_ref, wq_ref, km_ref, elb_ref, o_ref, vn_ref, sn_ref, S):
        @pl.when(pl.program_id(0) == 0)
        def _():
            S[...] = jnp.zeros_like(S)

        sub8 = _iota((CPT, DHD), 0)
        heads = range(NHD)

        def chunk(cc, carry):
            rs = pl.ds(pl.multiple_of(cc * CH, CH), CH)
            sh = [S[h] for h in heads]
            sb = [s.astype(_MXU) for s in sh]
            r1 = [_dot(wq_ref[h, cc], sb[h]) for h in heads]
            vnb = [(u_ref[h, rs, :] - r1[h][0:CH]).astype(_MXU) for h in heads]
            r2 = [_dot(km_ref[h, cc], vnb[h]) for h in heads]
            for h in heads:
                el = jnp.sum(jnp.where(sub8 == cc, elb_ref[:, DHD * h:DHD * (h + 1)], 0.0), axis=0, keepdims=True)
                S[h] = sh[h] * el + r2[h][0:DHD]
                sn_ref[cc, h] = sb[h]
                vn_ref[h, rs, :] = vnb[h]
                o_ref[h, rs, :] = r1[h][CH:2 * CH] + r2[h][DHD:DHD + CH]
            return carry

        lax.fori_loop(0, CPT, chunk, 0)

    return pl.pallas_call(
        body, name="dn_scan", grid=(nb,),
        in_specs=[hm_spec(DHD), pl.BlockSpec((NHD, CPT, 2 * CH, DHD), lambda i: (0, i, 0, 0)),
                  pl.BlockSpec((NHD, CPT, DHD + CH, CH), lambda i: (0, i, 0, 0)),
                  pl.BlockSpec((CPT, NHD * DHD), lambda i: (i, 0))],
        out_specs=[hm_spec(DHD), hm_spec(DHD), pl.BlockSpec((CPT, NHD, DHD, DHD), lambda i: (i, 0, 0, 0))],
        out_shape=[jax.ShapeDtypeStruct((NHD, T, DHD), F32), jax.ShapeDtypeStruct((NHD, T, DHD), _MXU),
                   jax.ShapeDtypeStruct((T // CH, NHD, DHD, DHD), _MXU)],
        scratch_shapes=[pltpu.VMEM((NHD, DHD, DHD), F32)],
        compiler_params=_cp(("arbitrary",)),
    )(u, wq, km, elb)


def _dn_scan_bwd(do, mq, kd, wt, sn, vn, elb):
    T = do.shape[1]
    nb = T // TG
    rev = lambda wd: pl.BlockSpec((NHD, TG, wd), lambda i: (0, nb - 1 - i, 0))
    rev_t = lambda r: pl.BlockSpec((NHD, CPT, r, CH), lambda i: (0, nb - 1 - i, 0, 0))

    def body(do_ref, mq_ref, kd_ref, wt_ref, sn_ref, vn_ref, elb_ref,
             du_ref, dw_ref, dqd_ref, dkd_ref, dgx_ref, dS):
        @pl.when(pl.program_id(0) == 0)
        def _():
            dS[...] = jnp.zeros_like(dS)

        last_row = _iota((CH, DHD), 0) == CH - 1
        sub8 = _iota((CPT, DHD), 0)
        heads = range(NHD)

        def chunk(k, carry):
            cc = CPT - 1 - k
            rs = pl.ds(pl.multiple_of(cc * CH, CH), CH)
            dsh = [dS[h] for h in heads]
            dsb = [d.astype(_MXU) for d in dsh]
            doc = [do_ref[h, rs, :].astype(_MXU) for h in heads]
            a = [_dot(mq_ref[h, cc], doc[h]) for h in heads]
            b = [_dot(kd_ref[h, rs, :], dsb[h]) for h in heads]
            dvn = [a[h][0:CH] + b[h] for h in heads]
            dvnb = [d.astype(_MXU) for d in dvn]
            e = [_dot(wt_ref[h, cc], dvnb[h]) for h in heads]
            for h in heads:
                el = jnp.sum(jnp.where(sub8 == cc, elb_ref[:, DHD * h:DHD * (h + 1)], 0.0), axis=0, keepdims=True)
                sn = sn_ref[cc, h]
                dS[h] = a[h][CH:CH + DHD] + dsh[h] * el - e[h]
                du_ref[h, rs, :] = dvn[h]
                c = _dot_nt(jnp.concatenate([doc[h], dvnb[h]], axis=0), sn)
                dqd_ref[h, rs, :] = c[0:CH]
                dw_ref[h, rs, :] = -c[CH:2 * CH]
                dkd_ref[h, rs, :] = _dot_nt(vn_ref[h, rs, :], dsb[h])
                part = jnp.sum(dsh[h] * sn.astype(F32), axis=0, keepdims=True) * el
                dgx_ref[h, rs, :] = jnp.where(last_row, part, 0.0)
            return carry

        lax.fori_loop(0, CPT, chunk, 0)

    o = jax.ShapeDtypeStruct((NHD, T, DHD), F32)
    return pl.pallas_call(
        body, name="dn_scan_bwd", grid=(nb,),
        in_specs=[rev(DHD), rev_t(CH + DHD), rev(DHD), rev_t(DHD),
                  pl.BlockSpec((CPT, NHD, DHD, DHD), lambda i: (nb - 1 - i, 0, 0, 0)), rev(DHD),
                  pl.BlockSpec((CPT, NHD * DHD), lambda i: (nb - 1 - i, 0))],
        out_specs=[rev(DHD)] * 5,
        out_shape=[o] * 5,
        scratch_shapes=[pltpu.VMEM((NHD, DHD, DHD), F32)],
        compiler_params=_cp(("arbitrary",)),
    )(do, mq, kd, wt, sn, vn, elb)


def _put_col(acc, k, col):
    return jnp.where(_iota(acc.shape, 1) == k, col, acc)


def _dn_post_bwd(draw, cv, conv_w, ba, bat, alog8, dtb8, alog8t, dtb8t, du, dw, dqd, dkd, dgx, do, vn, tm, u, w):
    T = draw.shape[0]
    nb = T // TG
    hm_spec = lambda wd: pl.BlockSpec((NHD, TG, wd), lambda i: (0, nb - 1 - i, 0))
    rrows = lambda w: pl.BlockSpec((TG, w), lambda i: (nb - 1 - i, 0))

    def body(x_ref, cv_ref, cw_ref, ba_ref, bat_ref, al_ref, dt_ref, alt_ref, dtt_ref,
             du_ref, dw_ref, dqd_ref, dkd_ref, dgx_ref, do_ref, vn_ref, tm_ref, u_ref, w_ref,
             dx_ref, dba_ref, sm_ref, dcw_ref, dc_ref, nxt_ref):
        i = pl.program_id(0)

        @pl.when(i == 0)
        def _():
            sm_ref[...] = jnp.zeros_like(sm_ref)
            dcw_ref[...] = jnp.zeros_like(dcw_ref)
            nxt_ref[...] = jnp.zeros_like(nxt_ref)

        K = _dn_masks()
        cv = cv_ref[...]
        sg = _sigmoid(cv)
        c_tile = cv * sg
        dsilu = sg * (1.0 + cv * (1.0 - sg))
        for g in range(NG):
            rows = slice(GR * g, GR * (g + 1))
            ba_g = ba_ref[rows, :]
            g8, gc8, gl8, gcrow8 = _gate_terms(ba_g, bat_ref[:, rows], al_ref[...], dt_ref[...], alt_ref[...],
                                               dtt_ref[...], K)
            beta8 = _sigmoid(ba_g)
            dgc8 = jnp.zeros((GR, 8), F32)
            rd8 = jnp.zeros((GR, 8), F32)
            dbeta8 = jnp.zeros((GR, 8), F32)
            pre = [(tm_ref[h, rows, :], u_ref[h, rows, :], w_ref[h, rows, :]) for h in range(NHD)]
            ds = _dn_heads(c_tile, rows, beta8, gc8, gl8, gcrow8, K, pre)
            H = range(NHD)
            eye_b = K["eye_b"].astype(_MXU)
            gam_t = [jnp.exp(jnp.where(K["triu"], -d["diff"], NEG)) for d in ds]
            doh = [do_ref[h, rows, :] for h in H]
            vnh = [vn_ref[h, rows, :] for h in H]
            tt = [_dot_nt(eye_b, d["tm"]) for d in ds]
            dvb = [_dot(tt[h], du_ref[h, rows, :]) for h in H]
            dkg = [_dot(tt[h], dw_ref[h, rows, :]) for h in H]
            plt = [_dot_nt(d["kn"], d["kb"]) for d in ds]
            pmt = [_dot_nt(d["kn"], d["qn"]) for d in ds]
            da = [-(_dot_nt(dvb[h], ds[h]["u"]) + _dot_nt(dkg[h], ds[h]["w"])) for h in H]
            dat = [-(_dot_nt(ds[h]["u"], dvb[h]) + _dot_nt(ds[h]["w"], dkg[h])) for h in H]
            dpm = [jnp.where(K["tril"], _dot_nt(doh[h], vnh[h]), 0.0) * ds[h]["gam_m"] for h in H]
            dpmt = [jnp.where(K["triu"], _dot_nt(vnh[h], doh[h]), 0.0) * gam_t[h] for h in H]
            dpl = [jnp.where(K["strict"], da[h], 0.0) * ds[h]["gam_m"] for h in H]
            dplt = [jnp.where(K["strict_u"], dat[h], 0.0) * gam_t[h] for h in H]
            dkb = [_dot(dpl[h], ds[h]["kn"]) + dkg[h] * ds[h]["gam"] for h in H]
            dqn = [_dot(dpm[h], ds[h]["kn"]) + dqd_ref[h, rows, :] * ds[h]["gam"] for h in H]
            dknm = [_dot(dplt[h], ds[h]["kb"]) + _dot(dpmt[h], ds[h]["qn"]) for h in H]
            for h, d in enumerate(ds):
                kn, dqdh, dkdh = d["kn"], dqd_ref[h, rows, :], dkd_ref[h, rows, :]
                dkn = dknm[h] + dkdh * d["egl"] + dkb[h] * d["beta"]
                dkd_kd = dkdh * d["kd"]
                rd = jnp.sum(dkd_kd, axis=1, keepdims=True)
                dgc = jnp.sum(dpl[h] * d["pl"] + dpm[h] * d["pm"] - dplt[h] * plt[h] - dpmt[h] * pmt[h]
                              + dqdh * d["qd"] + dkg[h] * d["kg"] - dkd_kd + dgx_ref[h, rows, :],
                              axis=1, keepdims=True)
                dgc8 = _put_col(dgc8, NHD + h, dgc)
                rd8 = _put_col(rd8, NHD + h, rd)
                dbeta = jnp.sum(dkb[h] * kn + dvb[h] * d["v"], axis=1, keepdims=True)
                dbeta8 = _put_col(dbeta8, h, dbeta)
                dqh = dqn[h] * (DHD ** -0.5)
                qh = d["qh"]
                dqr = d["rq"] * (dqh - qh * jnp.sum(dqh * qh, axis=1, keepdims=True))
                dkr = d["rk"] * (dkn - kn * jnp.sum(dkn * kn, axis=1, keepdims=True))
                cq = slice(DHD * h, DHD * (h + 1))
                ck = slice(DW + DHD * h, DW + DHD * (h + 1))
                cvv = slice(2 * DW + DHD * h, 2 * DW + DHD * (h + 1))
                dc_ref[rows, cq] = dqr * dsilu[rows, cq]
                dc_ref[rows, ck] = dkr * dsilu[rows, ck]
                dc_ref[rows, cvv] = dvb[h] * d["beta"] * dsilu[rows, cvv]
            dgc8 = dgc8 + jnp.where(K["last"], _ones_dot(K["blk_b"], rd8), 0.0)
            dg8 = _ones_dot(K["triu_b"], dgc8)
            sgm = _sigmoid(ba_g + dt_ref[...])
            dalpha = dg8 * (-jnp.exp(al_ref[...])) * sgm
            lane8 = _iota((GR, 8), 1)
            dba_ref[rows, :] = jnp.where(lane8 < NHD, dbeta8 * beta8 * (1.0 - beta8), dalpha)
            valid = lane8 >= NHD
            sm_ref[0:1, 0:8] += jnp.sum(jnp.where(valid, dg8 * g8, 0.0), axis=0, keepdims=True)
            sm_ref[1:2, 0:8] += jnp.sum(jnp.where(valid, dalpha, 0.0), axis=0, keepdims=True)

        dcv = dc_ref[...]
        xv = x_ref[...]
        nxt = nxt_ref[...]
        w = cw_ref[...]
        dx = dcv * w[CONV_K - 1:CONV_K, :]
        dcw_ref[CONV_K - 1:CONV_K, :] += jnp.sum(dcv * xv, axis=0, keepdims=True)
        for k in range(1, CONV_K):
            j = CONV_K - 1 - k
            up = _shift_up(dcv, nxt, k)
            dx = dx + up * w[j:j + 1, :]
            dcw_ref[j:j + 1, :] += jnp.sum(up * xv, axis=0, keepdims=True)
        dx_ref[...] = dx
        nxt_ref[...] = dcv[0:8]

    return pl.pallas_call(
        body, name="dn_post_bwd", grid=(nb,),
        in_specs=[rrows(1536), rrows(1536), _full((CONV_K, 1536)), rrows(8),
                  pl.BlockSpec((8, TG), lambda i: (0, nb - 1 - i)), _full((1, 8)), _full((1, 8)), _full((8, 1)),
                  _full((8, 1)),
                  hm_spec(DHD), hm_spec(DHD), hm_spec(DHD), hm_spec(DHD), hm_spec(DHD), hm_spec(DHD), hm_spec(DHD),
                  hm_spec(GR), hm_spec(DHD), hm_spec(DHD)],
        out_specs=[rrows(1536), rrows(8), _full((8, LANES)), _full((8, 1536))],
        out_shape=[jax.ShapeDtypeStruct((T, 1536), F32), jax.ShapeDtypeStruct((T, 8), F32),
                   jax.ShapeDtypeStruct((8, LANES), F32), jax.ShapeDtypeStruct((8, 1536), F32)],
        scratch_shapes=[pltpu.VMEM((TG, 1536), F32), pltpu.VMEM((8, 1536), F32)],
        compiler_params=_cp(("arbitrary",)),
    )(draw, cv, conv_w, ba, bat, alog8, dtb8, alog8t, dtb8t, du, dw, dqd, dkd, dgx, do, vn, tm, u, w)


def _rms(x):
    return lax.rsqrt(jnp.mean(x * x, axis=1, keepdims=True) + EPS)


def _rms_bwd(dy, xh, r, g):
    dxh = dy * g
    return r * (dxh - xh * jnp.mean(dxh * xh, axis=1, keepdims=True))


def _hm_rows(tm):
    return pl.BlockSpec((NHD, tm, DHD), lambda i: (0, i, 0))


def _post_mix(apre, o, z, x, w_out, g_a, g_dn):
    T = x.shape[0]

    def body(ap_ref, o_ref, z_ref, x_ref, w_ref, ga_ref, gd_ref, x1_ref, mix_ref):
        ap = ap_ref[...]
        parts = [ap * _rms(ap) * ga_ref[...]]
        zz = z_ref[...]
        for h in range(NHD):
            oh = o_ref[h]
            zh = zz[:, DHD * h:DHD * (h + 1)]
            parts.append(oh * _rms(oh) * gd_ref[...] * (zh * _sigmoid(zh)))
        mix = jnp.concatenate(parts, axis=1).astype(_MXU)
        mix_ref[...] = mix
        x1_ref[...] = x_ref[...] + jnp.dot(mix, w_ref[...], preferred_element_type=F32)

    return pl.pallas_call(
        body, name="post_mix", grid=(T // TM,),
        in_specs=[_rows(TM, AW), _hm_rows(TM), _rows(TM, DW), _rows(TM, D), _full((D, D)), _full((1, AW)),
                  _full((1, DHD))],
        out_specs=[_rows(TM, D), _rows(TM, D)],
        out_shape=[jax.ShapeDtypeStruct((T, D), F32), jax.ShapeDtypeStruct((T, D), _MXU)],
        compiler_params=_cp(("arbitrary",)),
    )(apre, o, z, x, w_out, g_a, g_dn)


def _ffn(x1, tgt, wl_all, g_ffn):
    T = x1.shape[0]
    SH = FF // N_DEV
    nt = (((1,), (1,)), ((), ()))

    def body(x_ref, t_ref, wl_hbm, g_ref,
             dx1_ref, dx1b_ref, h2_ref, act_ref, dgu_ref, dyb_ref, loss_ref, dg_ref, wg, wu, wd, sem):
        @pl.when(pl.program_id(0) == 0)
        def _():
            cps = [pltpu.make_async_copy(wl_hbm.at[dev, pl.ds(128 + SH * k, SH), :], dst.at[pl.ds(SH * dev, SH), :],
                                         sem.at[N_DEV * k + dev])
                   for k, dst in enumerate((wg, wu, wd)) for dev in range(N_DEV)]
            for cp in cps:
                cp.start()
            for cp in cps:
                cp.wait()
            loss_ref[...] = jnp.zeros_like(loss_ref)
            dg_ref[...] = jnp.zeros_like(dg_ref)

        xv = x_ref[...]
        r = _rms(xv)
        xh = xv * r
        gg = g_ref[...]
        h2 = (xh * gg).astype(_MXU)
        h2_ref[...] = h2
        gate = lax.dot_general(h2, wg[...], nt, preferred_element_type=F32)
        up = lax.dot_general(h2, wu[...], nt, preferred_element_type=F32)
        sg = _sigmoid(gate)
        silu = gate * sg
        act = (silu * up).astype(_MXU)
        act_ref[...] = act
        y = xv + jnp.dot(act, wd[...], preferred_element_type=F32)
        err = y - t_ref[...]
        loss_ref[...] += jnp.sum(err * err, axis=0, keepdims=True)
        dy = err * (1.0 / D)
        dyb = dy.astype(_MXU)
        dyb_ref[...] = dyb
        dact = lax.dot_general(dyb, wd[...], nt, preferred_element_type=F32)
        dgate = (dact * up * (sg * (1.0 + gate * (1.0 - sg)))).astype(_MXU)
        dup = (dact * silu).astype(_MXU)
        dgu_ref[:, 0:FF] = dgate
        dgu_ref[:, FF:2 * FF] = dup
        dh2 = (jnp.dot(dgate, wg[...], preferred_element_type=F32)
               + jnp.dot(dup, wu[...], preferred_element_type=F32))
        dg_ref[...] += jnp.sum(dh2 * xh, axis=0, keepdims=True)
        dx1 = dy + _rms_bwd(dh2, xh, r, gg)
        dx1_ref[...] = dx1
        dx1b_ref[...] = dx1.astype(_MXU)

    anyspec = pl.BlockSpec(memory_space=pl.ANY)
    sd = lambda w, dt: jax.ShapeDtypeStruct((T, w), dt)
    return pl.pallas_call(
        body, name="ffn", grid=(T // TF,),
        in_specs=[_rows(TF, D), _rows(TF, D), anyspec, _full((1, D))],
        out_specs=[_rows(TF, D), _rows(TF, D), _rows(TF, D), _rows(TF, FF), _rows(TF, 2 * FF), _rows(TF, D),
                   _full((1, D)), _full((1, D))],
        out_shape=[sd(D, F32), sd(D, _MXU), sd(D, _MXU), sd(FF, _MXU), sd(2 * FF, _MXU), sd(D, _MXU),
                   jax.ShapeDtypeStruct((1, D), F32), jax.ShapeDtypeStruct((1, D), F32)],
        scratch_shapes=[pltpu.VMEM((FF, D), _MXU)] * 3 + [pltpu.SemaphoreType.DMA((3 * N_DEV,))],
        compiler_params=_cp(("arbitrary",)),
    )(x1, tgt, wl_all, g_ffn)


def _mix_bwd(dx1b, w_out, apre, o, z, g_a, g_dn):
    T = dx1b.shape[0]

    def body(dx_ref, w_ref, ap_ref, o_ref, z_ref, ga_ref, gd_ref, dap_ref, do_ref, dz_ref, dga_ref, dgd_ref):
        @pl.when(pl.program_id(0) == 0)
        def _():
            dga_ref[...] = jnp.zeros_like(dga_ref)
            dgd_ref[...] = jnp.zeros_like(dgd_ref)

        dmix = lax.dot_general(dx_ref[...], w_ref[...], (((1,), (1,)), ((), ())), preferred_element_type=F32)
        ap = ap_ref[...]
        ra = _rms(ap)
        ah = ap * ra
        da = dmix[:, 0:AW]
        dga_ref[...] += jnp.sum(da * ah, axis=0, keepdims=True)
        dap_ref[...] = _rms_bwd(da, ah, ra, ga_ref[...])
        zz = z_ref[...]
        gd = gd_ref[...]
        for h in range(NHD):
            cs = slice(DHD * h, DHD * (h + 1))
            dd = dmix[:, AW + DHD * h:AW + DHD * (h + 1)]
            oh = o_ref[h]
            ro = _rms(oh)
            ohh = oh * ro
            zh = zz[:, cs]
            sz = _sigmoid(zh)
            dz_ref[:, cs] = dd * (ohh * gd) * (sz * (1.0 + zh * (1.0 - sz)))
            don = dd * (zh * sz)
            dgd_ref[...] += jnp.sum(don * ohh, axis=0, keepdims=True)
            do_ref[h] = _rms_bwd(don, ohh, ro, gd)

    return pl.pallas_call(
        body, name="mix_bwd", grid=(T // TM,),
        in_specs=[_rows(TM, D), _full((D, D)), _rows(TM, AW), _hm_rows(TM), _rows(TM, DW), _full((1, AW)),
                  _full((1, DHD))],
        out_specs=[_rows(TM, AW), _hm_rows(TM), _rows(TM, DW), _full((1, AW)), _full((1, DHD))],
        out_shape=[jax.ShapeDtypeStruct((T, AW), F32), jax.ShapeDtypeStruct((NHD, T, DHD), F32),
                   jax.ShapeDtypeStruct((T, DW), F32), jax.ShapeDtypeStruct((1, AW), F32),
                   jax.ShapeDtypeStruct((1, DHD), F32)],
        compiler_params=_cp(("arbitrary",)),
    )(dx1b, w_out, apre, o, z, g_a, g_dn)


DPW = 3712


def _inproj_bwd(dqn, dkn, dv, araw, ddraw, dz, dba, x, dx1, w_int, w_ba, g_mix, qg_t, kg_t):
    T = x.shape[0]

    def body(dqn_ref, dkn_ref, dv_ref, ar_ref, dd_ref, dz_ref, dba_ref, x_ref, dx1_ref, w_hbm, wba_ref, g_ref, qg_ref,
             kg_ref, dx_ref, dp_ref, dgm_ref, dqg_ref, dkg_ref, w_ref, w_sem):
        @pl.when(pl.program_id(0) == 0)
        def _():
            cp = pltpu.make_async_copy(w_hbm, w_ref, w_sem)
            cp.start()
            cp.wait()
            dgm_ref[...] = jnp.zeros_like(dgm_ref)
            dqg_ref[...] = jnp.zeros_like(dqg_ref)
            dkg_ref[...] = jnp.zeros_like(dkg_ref)

        bd = _block_ones(AW // 2, DHA)

        def head_norm_bwd(raw, dyn, gain, dg_ref):
            r = _head_rms(raw, bd, DHA)
            xh = raw * r
            dg_ref[...] += jnp.sum(dyn * xh, axis=0, keepdims=True)
            dxh = dyn * gain
            return r * (dxh - xh * (_head_sum(dxh * xh, bd) * (1.0 / DHA)))

        def segment(lo, val):
            vb = val.astype(_MXU)
            dp_ref[:, lo:lo + val.shape[1]] = vb
            return jnp.dot(vb, w_ref[lo:lo + val.shape[1], :], preferred_element_type=F32)

        dh = segment(1536, dd_ref[...]) + segment(2 * AW, dv_ref[...]) + segment(3072, dz_ref[...])
        dbab = dba_ref[...].astype(_MXU)
        dp_ref[:, 3584:DPW] = jnp.zeros((TM, DPW - 3584), _MXU)
        dp_ref[:, 3584:3592] = dbab
        dh = dh + lax.dot_general(dbab, wba_ref[...], (((1,), (1,)), ((), ())), preferred_element_type=F32)
        ar = ar_ref[...]
        dq = head_norm_bwd(ar[:, 0:AW], dqn_ref[...] * (DHA ** -0.5), qg_ref[...], dqg_ref)
        dk = head_norm_bwd(ar[:, AW:2 * AW], dkn_ref[...], kg_ref[...], dkg_ref)
        dh = dh + segment(0, dq) + segment(AW, dk)
        xv = x_ref[...]
        r = _rms(xv)
        xh = xv * r
        dgm_ref[...] += jnp.sum(dh * xh, axis=0, keepdims=True)
        dx_ref[...] = dx1_ref[...] + _rms_bwd(dh, xh, r, g_ref[...])

    return pl.pallas_call(
        body, name="inproj_bwd", grid=(T // TM,),
        in_specs=[_rows(TM, AW), _rows(TM, AW), _rows(TM, AW), _rows(TM, 1536), _rows(TM, 1536), _rows(TM, DW),
                  _rows(TM, 8), _rows(TM, D), _rows(TM, D), pl.BlockSpec(memory_space=pl.ANY), _full((D, 8)),
                  _full((1, D)), _full((1, AW)), _full((1, AW))],
        out_specs=[_rows(TM, D), _rows(TM, DPW), _full((1, D)), _full((1, AW)), _full((1, AW))],
        out_shape=[jax.ShapeDtypeStruct((T, D), F32), jax.ShapeDtypeStruct((T, DPW), _MXU),
                   jax.ShapeDtypeStruct((1, D), F32), jax.ShapeDtypeStruct((1, AW), F32),
                   jax.ShapeDtypeStruct((1, AW), F32)],
        scratch_shapes=[pltpu.VMEM((3584, D), _MXU), pltpu.SemaphoreType.DMA],
        compiler_params=_cp(("arbitrary",)),
    )(dqn, dkn, dv, araw, ddraw, dz, dba, x, dx1, w_int, w_ba, g_mix, qg_t, kg_t)


def _wgrad(a, b, name, tk=1024, tn=None, out_dtype=F32, transposed=False):
    T, M = a.shape
    N = b.shape[1]
    tn = N if tn is None else tn
    assert T % tk == 0 and N % tn == 0, (T, tk, N, tn)
    nk = T // tk

    def body(a_ref, b_ref, o_ref, acc):
        k = pl.program_id(1)

        @pl.when(k == 0)
        def _():
            acc[...] = jnp.zeros_like(acc)

        acc[...] += lax.dot_general(a_ref[...], b_ref[...], (((0,), (0,)), ((), ())), preferred_element_type=F32)

        @pl.when(k == nk - 1)
        def _():
            r = acc[...]
            o_ref[...] = (r.T if transposed else r).astype(out_dtype)

    if transposed:
        out_spec, out_shape = pl.BlockSpec((tn, M), lambda j, k: (j, 0)), (N, M)
    else:
        out_spec, out_shape = pl.BlockSpec((M, tn), lambda j, k: (0, j)), (M, N)
    return pl.pallas_call(
        body, name=name, grid=(N // tn, nk),
        in_specs=[pl.BlockSpec((tk, M), lambda j, k: (k, 0)), pl.BlockSpec((tk, tn), lambda j, k: (k, j))],
        out_specs=out_spec,
        out_shape=jax.ShapeDtypeStruct(out_shape, out_dtype),
        scratch_shapes=[pltpu.VMEM((M, tn), F32)],
        compiler_params=_cp(("arbitrary", "arbitrary")),
    )(a, b)


def _adamw(parts, w, m, v, name, tr, send=None):
    K, R, W = parts.shape
    n = R // tr

    def body(p_ref, w_ref, m_ref, v_ref, *rest):
        if send is not None:
            send_ref, g_ref, d_ref, nm_ref, nv_ref, recv_ref, send_sems, recv_sems, local_sem = rest
            sc = _Scatter(send_ref, recv_ref, send_sems, recv_sems, local_sem)
            pl.when(pl.program_id(0) == 0)(sc.start)
        else:
            g_ref, d_ref, nm_ref, nv_ref = rest
        g = p_ref[0].astype(F32)
        for k in range(1, K):
            g = g + p_ref[k].astype(F32)
        g_ref[...] = g
        nm = ADAM_B1 * m_ref[...] + (1.0 - ADAM_B1) * g
        nv = ADAM_B2 * v_ref[...] + (1.0 - ADAM_B2) * (g * g)
        nm_ref[...] = nm
        nv_ref[...] = nv
        m_hat = nm / (1.0 - ADAM_B1 ** ADAM_STEP)
        v_hat = nv / (1.0 - ADAM_B2 ** ADAM_STEP)
        d_ref[...] = -ADAM_LR * (m_hat / (jnp.sqrt(v_hat) + ADAM_EPS) + ADAM_WD * w_ref[...])
        if send is not None:
            pl.when(pl.program_id(0) == n - 1)(sc.finish)

    o = jax.ShapeDtypeStruct((R, W), F32)
    anyspec = pl.BlockSpec(memory_space=pl.ANY)
    hosted = send is not None
    return pl.pallas_call(
        body, name=name, grid=(n,),
        in_specs=[pl.BlockSpec((K, tr, W), lambda i: (0, i, 0)), _rows(tr, W), _rows(tr, W), _rows(tr, W)]
        + ([anyspec] if hosted else []),
        out_specs=[_rows(tr, W)] * 4 + ([anyspec] if hosted else []),
        out_shape=[o] * 4 + ([jax.ShapeDtypeStruct(send.shape, send.dtype)] if hosted else []),
        scratch_shapes=_COMM_SEMS if hosted else [],
        compiler_params=_cp(("arbitrary",)),
    )(*((parts, w, m, v) + ((send,) if hosted else ())))


SM_ROWS = 136
R_GMIX, R_GFFN, R_QG, R_KG, R_GA, R_GDN, R_ALOG, R_DT, R_LOSS, R_CONV, R_REL = 0, 8, 16, 24, 32, 40, 48, 49, 56, 64, 112


def _small_reduce(gathered):
    def body(p_ref, o_ref):
        s = p_ref[0]
        for k in range(1, N_DEV):
            s = s + p_ref[k]
        o_ref[...] = s
        for r0 in (R_QG, R_KG):
            rs = jnp.sum(s[r0:r0 + 4], axis=0, keepdims=True)
            o_ref[r0:r0 + 1, :] = rs + pltpu.roll(rs, DHA, 1)
        tot = jnp.sum(jnp.sum(s[R_LOSS:R_LOSS + 8], axis=0, keepdims=True), axis=1, keepdims=True)
        o_ref[R_LOSS:R_LOSS + 1, :] = jnp.broadcast_to(tot * (0.5 / D), (1, LANES))

    return pl.pallas_call(
        body, name="small_reduce",
        out_shape=jax.ShapeDtypeStruct((SM_ROWS, LANES), F32),
    )(gathered)


_WIRE = jnp.bfloat16
RA_USED, RA = 449, 464
RL = 128 + 3 * 352


def _pack_rows(parts, rows=None):
    p = jnp.concatenate([t.reshape(-1, D) for t in parts], axis=0) if len(parts) > 1 else parts[0].reshape(-1, D)
    return p if rows is None else jnp.pad(p, ((0, rows - p.shape[0]), (0, 0)))


def _unpack_rows(packed, shapes):
    out, r = [], 0
    for shp in shapes:
        nr = math.prod(shp) // D
        out.append(packed[r:r + nr].reshape(shp))
        r += nr
    return out


def _pad8(t):
    return jnp.pad(t, ((0, (-t.shape[0]) % 8), (0, 0)))


PART = 8 * LANES


def _pack_lanes(parts):
    rows = []
    for p in parts:
        f = p.reshape(-1)
        rows.append(jnp.pad(f, (0, (-f.shape[0]) % PART)).reshape(-1, LANES))
    return jnp.concatenate(rows, axis=0)


def _unpack_lanes(packed, shapes):
    out, r = [], 0
    for shp in shapes:
        n = math.prod(shp)
        nr = 8 * -(-n // PART)
        out.append(packed[r:r + nr].reshape(-1)[:n].reshape(shp))
        r += nr
    return out


def kernel(x, norm_mix_g, w_in, attn_q_norm_g, attn_k_norm_g, rel_bias, attn_out_norm_g, conv_w, a_log, dt_bias, dn_out_norm_g, w_out, norm_ffn_g, w_gate, w_up, w_down, loss_target, m_norm_mix_g, m_w_in, m_attn_q_norm_g, m_attn_k_norm_g, m_rel_bias, m_attn_out_norm_g, m_conv_w, m_a_log, m_dt_bias, m_dn_out_norm_g, m_w_out, m_norm_ffn_g, m_w_gate, m_w_up, m_w_down, v_norm_mix_g, v_w_in, v_attn_q_norm_g, v_attn_k_norm_g, v_rel_bias, v_attn_out_norm_g, v_conv_w, v_a_log, v_dt_bias, v_dn_out_norm_g, v_w_out, v_norm_ffn_g, v_w_gate, v_w_up, v_w_down):
    xs, tgt = x[0], loss_target[0]
    my_idx = 4 * lax.axis_index("x") + 2 * lax.axis_index("y") + lax.axis_index("c")
    late_w = (w_out[0], w_gate[0], w_up[0], w_down[0])

    tab, tabt, wa_all, cw_all = _bias_tables(
        jnp.pad(rel_bias[0].T, ((0, 0), (0, VAR0 - 257))), _pack_rows([w_in[0].T.astype(_MXU)], RA),
        jnp.pad(conv_w[0], ((0, 4), (0, 64))))
    W_in_t = wa_all[:, 0:RA_USED].reshape(N_DEV * RA_USED, D)
    W_int, W_ba = W_in_t[0:3584], W_in_t[3584:3592].T
    conv_full = cw_all[:, 0:CONV_K, 0:192].transpose(1, 0, 2).reshape(CONV_K, 1536)

    qg_t = jnp.tile(attn_q_norm_g, (1, NHA))
    kg_t = jnp.tile(attn_k_norm_g, (1, NHA))
    z4 = jnp.zeros((1, NHD), F32)
    alog8 = jnp.concatenate([z4, a_log], axis=1)
    dtb8 = jnp.concatenate([z4, dt_bias], axis=1)

    late_t = lambda ts: (ts[0], ts[1].T, ts[2].T, ts[3])
    araw, an, draw, z, ba, hb, wl_all = _inproj(xs, norm_mix_g, W_int, W_ba, qg_t, kg_t,
                                                _pack_rows([w.astype(_MXU) for w in late_t(late_w)]))
    W_out = wl_all[:, 0:128].reshape(D, D)
    apre = _attn_fwd(an, tab)
    bat = ba.T
    dn_args = (draw, conv_full, ba, bat, alog8, dtb8, alog8.T, dtb8.T)
    u, w, kd, tm, wq, km, mq, wt, elb, cv = _dn_prep(*dn_args)
    o, vn, sn = _dn_scan(u, wq, km, elb)
    x1, mix = _post_mix(apre, o, z, xs, W_out, attn_out_norm_g, dn_out_norm_g)

    dx1, dx1b, h2, act, dgu, dyb, loss_row, dgffn = _ffn(x1, tgt, wl_all, norm_ffn_g)

    gW_out = _wgrad(mix, dx1b, "wgrad_out", out_dtype=_WIRE)
    gW_gu_t = _wgrad(h2, dgu, "wgrad_gate_up", tn=FF, out_dtype=_WIRE, transposed=True)
    gW_down = _wgrad(dyb, act, "wgrad_down", out_dtype=_WIRE, transposed=True)
    send_late = jnp.concatenate(
        [gW_out.reshape(N_DEV, 128, D), gW_gu_t[0:FF].reshape(N_DEV, 352, D), gW_gu_t[FF:].reshape(N_DEV, 352, D),
         gW_down.reshape(N_DEV, 352, D)], axis=1)

    dap, do, dz, dga, dgdn = _mix_bwd(dx1b, W_out, apre, o, z, attn_out_norm_g, dn_out_norm_g)
    dqn, dkn, dv, dtabt, recv_late = _attn_bwd(an, dap, tabt, send_late)
    drel = _bias_grad(dtabt)
    du, dw, dqd, dkd, dgx = _dn_scan_bwd(do, mq, kd, wt, sn, vn, elb)
    ddraw, dba, sm, dcw = _dn_post_bwd(draw, cv, *dn_args[1:], du, dw, dqd, dkd, dgx, do, vn, tm, u, w)
    gx, dproj, dgmix, dqg, dkg = _inproj_bwd(dqn, dkn, dv, araw, ddraw, dz, dba, xs, dx1, W_int, W_ba, norm_mix_g,
                                             qg_t, kg_t)

    gW_in_t = _wgrad(hb, dproj, "wgrad_in", tk=512, out_dtype=_WIRE, transposed=True)
    send_in = jnp.pad(gW_in_t[0:N_DEV * RA_USED].reshape(N_DEV, RA_USED, D), ((0, 0), (0, RA - RA_USED), (0, 0)))
    late_m = (m_w_out[0], m_w_gate[0], m_w_up[0], m_w_down[0])
    late_v = (v_w_out[0], v_w_gate[0], v_w_up[0], v_w_down[0])
    *outs_late, recv_in = _adamw(recv_late, _pack_rows(late_t(late_w)), _pack_rows(late_t(late_m)),
                                 _pack_rows(late_t(late_v)), "adamw_late", 32, send=send_in)
    outs_in = _adamw(recv_in, _pack_rows([w_in[0].T], RA), _pack_rows([m_w_in[0].T], RA),
                     _pack_rows([v_w_in[0].T], RA), "adamw_w_in", 16)
    late_t_shapes = [t.shape for t in late_t(late_w)]
    big = [[a[0:RA_USED].T] + list(late_t(_unpack_rows(b, late_t_shapes))) for a, b in zip(outs_in, outs_late)]
    bg, bd_, bm, bv = big

    partial = jnp.concatenate(
        [dgmix.reshape(8, LANES), dgffn.reshape(8, LANES), _pad8(dqg.reshape(4, LANES)), _pad8(dkg.reshape(4, LANES)),
         _pad8(dga.reshape(4, LANES)), _pad8(dgdn), sm, loss_row.reshape(8, LANES),
         dcw[0:CONV_K].reshape(48, LANES), drel.reshape(24, LANES)], axis=0)
    S = _small_reduce(_all_gather(partial, "gather_small"))
    loss = S[R_LOSS, 0]
    g_conv = lax.dynamic_slice(S[R_CONV:R_CONV + 48].reshape(CONV_K, 1536), (0, 192 * my_idx), (CONV_K, 192))
    sg = [S[R_GMIX:R_GMIX + 8].reshape(1, D), S[R_QG:R_QG + 1, 0:DHA], S[R_KG:R_KG + 1, 0:DHA],
          S[R_REL:R_REL + 24].reshape(NHA, 384)[:, 0:257].T, S[R_GA:R_GA + 4].reshape(1, AW), g_conv,
          S[R_ALOG:R_ALOG + 1, NHD:2 * NHD], S[R_DT:R_DT + 1, NHD:2 * NHD], S[R_GDN:R_GDN + 1], S[R_GFFN:R_GFFN + 8].reshape(1, D)]
    sw = [norm_mix_g, attn_q_norm_g, attn_k_norm_g, rel_bias[0], attn_out_norm_g, conv_w[0], a_log, dt_bias, dn_out_norm_g, norm_ffn_g]
    smm = [m_norm_mix_g, m_attn_q_norm_g, m_attn_k_norm_g, m_rel_bias[0], m_attn_out_norm_g, m_conv_w[0], m_a_log, m_dt_bias, m_dn_out_norm_g, m_norm_ffn_g]
    svv = [v_norm_mix_g, v_attn_q_norm_g, v_attn_k_norm_g, v_rel_bias[0], v_attn_out_norm_g, v_conv_w[0], v_a_log, v_dt_bias, v_dn_out_norm_g, v_norm_ffn_g]
    s_shapes = [t.shape for t in sw]
    pg = _pack_lanes(sg)
    s_out = _adamw(pg[None], _pack_lanes(sw), _pack_lanes(smm), _pack_lanes(svv), "adamw_small", pg.shape[0])
    s_g, s_d, s_m, s_v = (_unpack_lanes(t, s_shapes) for t in s_out)

    lead = lambda t: t[None]
    def ordered(small, big):
        nm, q, k, rel, ao, cw, al, dtb, dno, nf = small
        wi, wo, wgt, wu, wdn = big
        return [nm, lead(wi), q, k, lead(rel), ao, lead(cw), al, dtb, dno, lead(wo), nf, lead(wgt), lead(wu), lead(wdn)]
    outs = [loss, gx[None]]
    for small, big in ((s_g, bg), (s_d, bd_), (s_m, bm), (s_v, bv)):
        outs += ordered(small, big)
    return tuple(outs)
```

```python
import math

import jax
import jax.numpy as jnp
from jax import lax
from jax.experimental import pallas as pl
from jax.experimental.pallas import tpu as pltpu

F32 = jnp.float32
BF16 = jnp.bfloat16
_MXU = jnp.bfloat16

D = 1024
AW = 512
NHA = 8
DHA = 64
CH = 64
NHD = 4
DHD = 128
DW = 512
FF = 2816
EPS = 1e-6
NEG = -1e30
N_DEV = 8
LANES = 128
VMEM_LIMIT = 56 * 1024 * 1024

ADAM_LR = 0.001
ADAM_B1 = 0.9
ADAM_B2 = 0.999
ADAM_EPS = 1e-08
ADAM_WD = 0.01
ADAM_STEP = 10

MESH_T = pl.DeviceIdType.MESH


def _cp(sem=None, vmem=VMEM_LIMIT):
    kw = dict(vmem_limit_bytes=vmem)
    if sem is not None:
        kw["dimension_semantics"] = sem
    return pltpu.CompilerParams(**kw)


def _dot(a, b):
    return jnp.dot(a.astype(_MXU), b.astype(_MXU), preferred_element_type=F32)


def _dot_nt(a, b):
    return lax.dot_general(a.astype(_MXU), b.astype(_MXU), (((1,), (1,)), ((), ())), preferred_element_type=F32)


def _dot_tn(a, b):
    return lax.dot_general(a.astype(_MXU), b.astype(_MXU), (((0,), (0,)), ((), ())), preferred_element_type=F32)


def _iota(shape, dim):
    return lax.broadcasted_iota(jnp.int32, shape, dim)


def _block_ones(n, blk, dtype=BF16):
    r, c = _iota((n, n), 0), _iota((n, n), 1)
    return jnp.where((r // blk) == (c // blk), 1.0, 0.0).astype(dtype)


def _sigmoid(x):
    return 1.0 / (1.0 + jnp.exp(-x))


def _softplus(x):
    return jnp.maximum(x, 0.0) + jnp.log(1.0 + jnp.exp(-jnp.abs(x)))


def _col(x, k):
    lane = _iota(x.shape, 1)
    return jnp.sum(jnp.where(lane == k, x, 0.0), axis=1, keepdims=True)


def _row(x, k):
    sub = _iota(x.shape, 0)
    return jnp.sum(jnp.where(sub == k, x, 0.0), axis=0, keepdims=True)


def _my_pos():
    return lax.axis_index("x"), lax.axis_index("y"), lax.axis_index("c")


def _all_gather(x2d, name):
    R, W = x2d.shape

    def body(x_ref, out_ref, send_sems, recv_sems, local_sem):
        ag = _Gather(x_ref, out_ref, send_sems, recv_sems, local_sem)
        ag.start()
        ag.forward()
        ag.finish()

    return pl.pallas_call(
        body, name=name,
        out_shape=jax.ShapeDtypeStruct((N_DEV, R, W), x2d.dtype),
        in_specs=[pl.BlockSpec(memory_space=pl.ANY)],
        out_specs=pl.BlockSpec(memory_space=pl.ANY),
        scratch_shapes=_COMM_SEMS,
    )(x2d)


_COMM_SEMS = [pltpu.SemaphoreType.DMA((7,)), pltpu.SemaphoreType.DMA((7,)), pltpu.SemaphoreType.DMA]


class _Gather:
    def __init__(self, x_ref, out_ref, send_sems, recv_sems, local_sem):
        x, y, c = _my_pos()
        me, sibling = (x, y, c), (x, y, 1 - c)
        chips = [(1 - x, y), (x, 1 - y), (1 - x, 1 - y)]

        def slot(px, py, pc):
            return out_ref.at[4 * px + 2 * py + pc]

        def copy(k, block, to, src=None):
            return pltpu.make_async_remote_copy(
                src_ref=slot(*block) if src is None else src, dst_ref=slot(*block),
                send_sem=send_sems.at[k], recv_sem=recv_sems.at[k], device_id=to, device_id_type=MESH_T)

        self.mine = pltpu.make_async_copy(x_ref, slot(*me), local_sem)
        self.first = [copy(0, me, sibling, src=x_ref)]
        self.first += [copy(1 + j, me, (*chip, c), src=x_ref) for j, chip in enumerate(chips)]
        self.passed = [copy(4 + j, (*chip, c), sibling) for j, chip in enumerate(chips)]
        self.from_chips = [copy(1 + j, (*chip, c), me) for j, chip in enumerate(chips)]
        self.from_sibling = [copy(0, sibling, me)] + [copy(4 + j, (*chip, 1 - c), me) for j, chip in enumerate(chips)]

    def start(self):
        self.mine.start()
        for cp in self.first:
            cp.start()

    def forward(self):
        for arrived, onward in zip(self.from_chips, self.passed):
            arrived.wait_recv()
            onward.start()

    def finish(self):
        for cp in self.from_sibling:
            cp.wait_recv()
        for cp in self.first + self.passed:
            cp.wait_send()
        self.mine.wait()


class _Scatter:
    def __init__(self, s_ref, r_ref, send_sems, recv_sems, local_sem):
        x, y, c = _my_pos()
        self.mine = pltpu.make_async_copy(s_ref.at[4 * x + 2 * y + c], r_ref.at[0], local_sem)
        self.copies = []
        for m in range(1, N_DEV):
            px = x ^ ((m >> 2) & 1)
            py = y ^ ((m >> 1) & 1)
            pc = c ^ (m & 1)
            self.copies.append(pltpu.make_async_remote_copy(
                src_ref=s_ref.at[4 * px + 2 * py + pc], dst_ref=r_ref.at[m],
                send_sem=send_sems.at[m - 1], recv_sem=recv_sems.at[m - 1],
                device_id=(px, py, pc), device_id_type=MESH_T))

    def start(self):
        self.mine.start()
        for cp in self.copies:
            cp.start()

    def finish(self):
        for cp in self.copies:
            cp.wait_recv()
        for cp in self.copies:
            cp.wait_send()
        self.mine.wait()


TM = 512
TF = 256
TG = 512
TS = 1024


def _full(shape):
    nd = len(shape)
    return pl.BlockSpec(shape, lambda i: (0,) * nd)


def _rows(tm, w):
    return pl.BlockSpec((tm, w), lambda i: (i, 0))


def _head_sum(x, bd):
    one_pass = lambda t: jnp.dot(t.astype(_MXU), bd.astype(_MXU), preferred_element_type=F32)
    return jnp.concatenate([one_pass(x[:, 0:256]), one_pass(x[:, 256:512])], axis=1)


def _head_rms(x, bd, width):
    return lax.rsqrt(_head_sum(x * x, bd) * (1.0 / width) + EPS)


def _inproj(x, g_mix, w_int, w_ba, qg_t, kg_t, later_w):
    T = x.shape[0]
    nt = T // TM
    ntd = (((1,), (1,)), ((), ()))

    def body(x_ref, g_ref, w_ref, wba_ref, qg_ref, kg_ref, lw_ref, araw_ref, an_ref, draw_ref, z_ref, ba_ref, h_ref,
             lw_all, send_sems, recv_sems, local_sem):
        i = pl.program_id(0)
        ag = _Gather(lw_ref, lw_all, send_sems, recv_sems, local_sem)
        pl.when(i == 0)(ag.start)
        pl.when(i == nt // 2)(ag.forward)
        xv = x_ref[...]
        r = lax.rsqrt(jnp.mean(xv * xv, axis=1, keepdims=True) + EPS)
        h = (xv * r * g_ref[...]).astype(_MXU)
        h_ref[...] = h
        proj = lambda lo, hi: lax.dot_general(h, w_ref[lo:hi, :], ntd, preferred_element_type=F32)
        q, k, v = proj(0, AW), proj(AW, 2 * AW), proj(2 * AW, 3 * AW)
        draw_ref[...] = proj(1536, 3072)
        z_ref[...] = proj(3072, 3584)
        ba_ref[...] = jnp.dot(h, wba_ref[...], preferred_element_type=F32)
        araw_ref[:, 0:AW] = q
        araw_ref[:, AW:2 * AW] = k
        araw_ref[:, 2 * AW:3 * AW] = v
        bd = _block_ones(AW // 2, DHA)
        qn = q * _head_rms(q, bd, DHA) * (qg_ref[...] * (DHA ** -0.5))
        kn = k * _head_rms(k, bd, DHA) * kg_ref[...]
        an_ref[:, 0:AW] = qn.astype(_MXU)
        an_ref[:, AW:2 * AW] = kn.astype(_MXU)
        an_ref[:, 2 * AW:3 * AW] = v.astype(_MXU)
        pl.when(i == nt - 1)(ag.finish)

    anyspec = pl.BlockSpec(memory_space=pl.ANY)
    return pl.pallas_call(
        body, name="inproj", grid=(nt,),
        in_specs=[_rows(TM, D), _full((1, D)), _full((3584, D)), _full((D, 8)), _full((1, AW)), _full((1, AW)),
                  anyspec],
        out_specs=[_rows(TM, 1536), _rows(TM, 1536), _rows(TM, 1536), _rows(TM, DW), _rows(TM, 8), _rows(TM, D),
                   anyspec],
        out_shape=[jax.ShapeDtypeStruct((T, 1536), F32), jax.ShapeDtypeStruct((T, 1536), _MXU),
                   jax.ShapeDtypeStruct((T, 1536), F32), jax.ShapeDtypeStruct((T, DW), F32),
                   jax.ShapeDtypeStruct((T, 8), F32), jax.ShapeDtypeStruct((T, D), _MXU),
                   jax.ShapeDtypeStruct((N_DEV,) + later_w.shape, later_w.dtype)],
        scratch_shapes=_COMM_SEMS,
        compiler_params=_cp(("arbitrary",)),
    )(x, g_mix, w_int, w_ba, qg_t, kg_t, later_w)


TQ = 256
TW = 768
VAR0 = 384
TOEP = 1024


def _bias_tables(rb_t, w_shard, c_shard):
    def body(rb_ref, w_ref, c_ref, tab_ref, tabt_ref, w_all, c_all, ws, wr, wl, cs, cr, cl):
        h = pl.program_id(0)
        gathers = [_Gather(w_ref, w_all, ws, wr, wl), _Gather(c_ref, c_all, cs, cr, cl)]

        @pl.when(h == 0)
        def _():
            for ag in gathers:
                ag.start()

        @pl.when(h == NHA // 2)
        def _():
            for ag in gathers:
                ag.forward()

        rb8 = jnp.broadcast_to(_row(rb_ref[...], h), (8, VAR0))
        n = _iota((VAR0, TOEP), 1)
        t = _iota((VAR0, TOEP), 0)

        def line(m):
            onehot = jnp.where(jnp.clip(512 - m, -128, 128) + 128 == t, 1.0, 0.0).astype(BF16)
            return sum(jnp.dot(p, onehot, preferred_element_type=F32) for p in _split3(rb8))[0:1, :]

        def band(r, j, first_key):
            return ((j >> 6) >= (r >> 6)) & ((j >> 6) <= (r >> 6) + 8) & (j >= first_key)

        g = line(jnp.where(n < TW, n, n - TOEP))
        tab = pltpu.roll(jnp.broadcast_to(g, (TQ, TOEP)), 0, 1, stride=1, stride_axis=0)[:, 0:TW]
        gt = line(jnp.where(n < TQ, -n, TOEP - n))
        tabt = pltpu.roll(jnp.broadcast_to(gt, (TW, TOEP)), 0, 1, stride=1, stride_axis=0)[:, 0:TQ]
        for v in range(3):
            first_key = max(512 - TQ * v, 0)
            tab_ref[v, 0] = jnp.where(band(_iota((TQ, TW), 0), _iota((TQ, TW), 1), first_key), tab, NEG)
            tabt_ref[v, 0] = jnp.where(band(_iota((TW, TQ), 1), _iota((TW, TQ), 0), first_key), tabt, NEG)

        @pl.when(h == NHA - 1)
        def _():
            for ag in gathers:
                ag.finish()

    anyspec = pl.BlockSpec(memory_space=pl.ANY)
    return pl.pallas_call(
        body, name="bias_tables", grid=(NHA,),
        in_specs=[_full((NHA, VAR0)), anyspec, anyspec],
        out_specs=[pl.BlockSpec((3, 1, TQ, TW), lambda h: (0, h, 0, 0)),
                   pl.BlockSpec((3, 1, TW, TQ), lambda h: (0, h, 0, 0)), anyspec, anyspec],
        out_shape=[jax.ShapeDtypeStruct((3, NHA, TQ, TW), F32), jax.ShapeDtypeStruct((3, NHA, TW, TQ), F32),
                   jax.ShapeDtypeStruct((N_DEV,) + w_shard.shape, w_shard.dtype),
                   jax.ShapeDtypeStruct((N_DEV,) + c_shard.shape, c_shard.dtype)],
        scratch_shapes=_COMM_SEMS + _COMM_SEMS,
        compiler_params=_cp(("arbitrary",)),
    )(rb_t, w_shard, c_shard)


def _bias_grad(dtabt):
    def body(d_ref, o_ref):
        a, b = _iota((TQ, TQ), 0), _iota((TQ, TQ), 1)
        anti = jnp.where(a + b == TQ - 1, 1.0, 0.0).astype(BF16)
        drev = sum(jnp.dot(t, anti, preferred_element_type=F32) for t in _split3(d_ref[0]))
        wide = jnp.concatenate([drev, jnp.zeros((TW, TOEP - TQ), F32)], axis=1)
        cols = jnp.sum(pltpu.roll(wide, 0, 1, stride=1, stride_axis=0), axis=0, keepdims=True)
        c = _iota((TOEP, VAR0), 0)
        idx = jnp.clip(512 + TQ - 1 - c, -128, 128) + 128
        onehot = jnp.where(idx == _iota((TOEP, VAR0), 1), 1.0, 0.0).astype(BF16)
        cols8 = jnp.broadcast_to(cols, (8, TOEP))
        o_ref[0] = sum(jnp.dot(t, onehot, preferred_element_type=F32) for t in _split3(cols8))[0:1, :]

    return pl.pallas_call(
        body, name="bias_grad", grid=(NHA,),
        in_specs=[pl.BlockSpec((1, TW, TQ), lambda h: (h, 0, 0))],
        out_specs=pl.BlockSpec((1, 1, VAR0), lambda h: (h, 0, 0)),
        out_shape=jax.ShapeDtypeStruct((NHA, 1, VAR0), F32),
        compiler_params=_cp(("arbitrary",)),
    )(dtabt)


def _kv_spec(col, back):
    return pl.BlockSpec((TQ, AW), lambda i: (jnp.maximum(i - back, 0), col))


def _attn_fwd(an, tab):
    T = an.shape[0]

    def body(q_ref, k2_ref, k1_ref, k0_ref, v2_ref, v1_ref, v0_ref, tab_ref, o_ref):
        i = pl.program_id(0)
        kwin = jnp.concatenate([k2_ref[...], k1_ref[...], k0_ref[...]], axis=0)
        vwin = jnp.concatenate([v2_ref[...], v1_ref[...], v0_ref[...]], axis=0)
        q = q_ref[...]
        lo_half = _iota((TQ, LANES), 1) < DHA

        def scores(h):
            sl = slice(LANES * (h // 2), LANES * (h // 2 + 1))
            mask = lo_half if h % 2 == 0 else jnp.logical_not(lo_half)
            qm = jnp.where(mask, q[:, sl], jnp.zeros((TQ, LANES), q.dtype))
            return _dot_nt(qm, kwin[:, sl]) + tab_ref[0, h]

        s_next = scores(0)
        outs = []
        for h in range(NHA):
            s = s_next
            if h + 1 < NHA:
                s_next = scores(h + 1)
            sl = slice(LANES * (h // 2), LANES * (h // 2 + 1))
            m = jnp.max(s, axis=1, keepdims=True)
            e = jnp.exp(s - m)
            l = jnp.sum(e, axis=1, keepdims=True)
            outs.append(_dot(e, vwin[:, sl]) / l)
            if h % 2 == 1:
                o_ref[:, sl] = jnp.where(lo_half, outs[h - 1], outs[h])

    return pl.pallas_call(
        body, name="attn_fwd", grid=(T // TQ,),
        in_specs=[pl.BlockSpec((TQ, AW), lambda i: (i, 0)),
                  _kv_spec(1, 2), _kv_spec(1, 1), _kv_spec(1, 0), _kv_spec(2, 2), _kv_spec(2, 1), _kv_spec(2, 0),
                  pl.BlockSpec((1, NHA, TQ, TW), lambda i: (jnp.minimum(i, 2), 0, 0, 0))],
        out_specs=_rows(TQ, AW),
        out_shape=jax.ShapeDtypeStruct((T, AW), F32),
        compiler_params=_cp(("arbitrary",)),
    )(an, an, an, an, an, an, an, tab)


def _attn_bwd(an, dout, tabt, send):
    T = an.shape[0]
    nq = T // TQ

    def qi(i):
        return jnp.minimum(i, nq - 1)

    def kv_spec(col, back):
        return pl.BlockSpec((TQ, AW), lambda i: (jnp.maximum(qi(i) - back, 0), col))

    def body(q_ref, do_ref, k2_ref, k1_ref, k0_ref, v2_ref, v1_ref, v0_ref, tabt_ref, send_ref,
             dq_ref, dk_ref, dv_ref, dtab_ref, recv_ref, dk_acc, dv_acc, send_sems, recv_sems, local_sem):
        i = pl.program_id(0)
        sc = _Scatter(send_ref, recv_ref, send_sems, recv_sems, local_sem)
        pl.when(i == 0)(sc.start)

        @pl.when(i == 0)
        def _():
            dtab_ref[...] = jnp.zeros_like(dtab_ref)

        new = i % 3
        dk_acc[new] = jnp.zeros((TQ, AW), F32)
        dv_acc[new] = jnp.zeros((TQ, AW), F32)

        @pl.when(i < nq)
        def _():
            kwin = jnp.concatenate([k2_ref[...], k1_ref[...], k0_ref[...]], axis=0)
            vwin = jnp.concatenate([v2_ref[...], v1_ref[...], v0_ref[...]], axis=0)
            q = q_ref[...]
            do = do_ref[...].astype(_MXU)
            lo_half = _iota((TQ, LANES), 1) < DHA

            def front(h):
                sl = slice(LANES * (h // 2), LANES * (h // 2 + 1))
                mask = lo_half if h % 2 == 0 else jnp.logical_not(lo_half)
                zero = jnp.zeros((TQ, LANES), q.dtype)
                qm = jnp.where(mask, q[:, sl], zero)
                dom = jnp.where(mask, do[:, sl], zero)
                st = _dot_nt(kwin[:, sl], qm) + tabt_ref[0, h]
                return st, _dot_nt(vwin[:, sl], dom), qm, dom, mask

            pairs = {}

            def back(h, ptb, dsb, qm, dom, mask):
                sl = slice(LANES * (h // 2), LANES * (h // 2 + 1))
                dv = _dot(ptb, dom)
                dk = _dot(dsb, qm)
                dq = jnp.where(mask, _dot_tn(dsb, kwin[:, sl]), 0.0)
                if h % 2 == 0:
                    pairs[h // 2] = (dq, dk, dv)
                    return
                dq0, dk0, dv0 = pairs.pop(h // 2)
                dq_ref[:, sl] = dq0 + dq
                dk_pair, dv_pair = dk0 + dk, dv0 + dv
                for w in range(3):
                    slot = (i + 1 + w) % 3
                    rows = slice(TQ * w, TQ * (w + 1))
                    dk_acc[slot, :, sl] += dk_pair[rows]
                    dv_acc[slot, :, sl] += dv_pair[rows]

            nxt = front(0)
            pending = None
            for h in range(NHA):
                st, dpt, qm, dom, mask = nxt
                if h + 1 < NHA:
                    nxt = front(h + 1)
                m = jnp.max(st, axis=0, keepdims=True)
                e = jnp.exp(st - m)
                pt = e * (1.0 / jnp.sum(e, axis=0, keepdims=True))
                delta = jnp.sum(pt * dpt, axis=0, keepdims=True)
                dst = pt * (dpt - delta)
                dtab_ref[h] += dst
                if pending is not None:
                    back(*pending)
                pending = (h, pt.astype(_MXU), dst.astype(_MXU), qm, dom, mask)
            back(*pending)

        @pl.when(i >= 2)
        def _():
            done = (i + 1) % 3
            dk_ref[...] = dk_acc[done]
            dv_ref[...] = dv_acc[done]

        pl.when(i == nq + 1)(sc.finish)

    back2 = pl.BlockSpec((TQ, AW), lambda i: (jnp.maximum(i - 2, 0), 0))
    anyspec = pl.BlockSpec(memory_space=pl.ANY)
    return pl.pallas_call(
        body, name="attn_bwd", grid=(nq + 2,),
        in_specs=[pl.BlockSpec((TQ, AW), lambda i: (qi(i), 0)), pl.BlockSpec((TQ, AW), lambda i: (qi(i), 0)),
                  kv_spec(1, 2), kv_spec(1, 1), kv_spec(1, 0), kv_spec(2, 2), kv_spec(2, 1), kv_spec(2, 0),
                  pl.BlockSpec((1, NHA, TW, TQ), lambda i: (jnp.minimum(i, 2), 0, 0, 0)), anyspec],
        out_specs=[pl.BlockSpec((TQ, AW), lambda i: (qi(i), 0)), back2, back2, _full((NHA, TW, TQ)), anyspec],
        out_shape=[jax.ShapeDtypeStruct((T, AW), F32), jax.ShapeDtypeStruct((T, AW), F32),
                   jax.ShapeDtypeStruct((T, AW), F32), jax.ShapeDtypeStruct((NHA, TW, TQ), F32),
                   jax.ShapeDtypeStruct(send.shape, send.dtype)],
        scratch_shapes=[pltpu.VMEM((3, TQ, AW), F32), pltpu.VMEM((3, TQ, AW), F32)] + _COMM_SEMS,
        compiler_params=_cp(("arbitrary",)),
    )(an, dout, an, an, an, an, an, an, tabt, send)


GR = 128
NG = TG // GR
CPT = TG // CH
CONV_K = 4


def _split3(x):
    a = x.astype(BF16)
    r = x - a.astype(F32)
    b = r.astype(BF16)
    c = (r - b.astype(F32)).astype(BF16)
    return a, b, c


def _ones_dot(ones_b, x):
    return sum(jnp.dot(ones_b, t, preferred_element_type=F32) for t in _split3(x))


def _dot_ones_nt(x, ones_b):
    dn = (((1,), (1,)), ((), ()))
    return sum(lax.dot_general(t, ones_b, dn, preferred_element_type=F32) for t in _split3(x))


def _dn_masks():
    r, c = _iota((GR, GR), 0), _iota((GR, GR), 1)
    same = (r >> 6) == (c >> 6)
    one = lambda m: jnp.where(m, 1.0, 0.0).astype(BF16)
    return dict(
        tril=same & (c <= r), strict=same & (c < r), triu=same & (c >= r), strict_u=same & (c > r),
        tril_b=one(same & (c <= r)), triu_b=one(same & (c >= r)), blk_b=one(same), eye_b=one(r == c),
        eye=jnp.where(r == c, 1.0, 0.0).astype(F32),
        fold_b=one((_iota((GR, CH), 0) & (CH - 1)) == _iota((GR, CH), 1)),
        last=(_iota((GR, 1), 0) & (CH - 1)) == CH - 1,
    )


def _shift_down(x, halo, k):
    if k == 0:
        return x
    xs = pltpu.roll(x, k, 0)
    hs = pltpu.roll(halo, k, 0)
    top = jnp.where(_iota(halo.shape, 0) < k, hs, xs[0:8])
    return jnp.concatenate([top, xs[8:]], axis=0)


def _shift_up(x, halo, k):
    if k == 0:
        return x
    n = x.shape[0]
    xs = pltpu.roll(x, n - k, 0)
    hs = pltpu.roll(halo, 8 - k, 0)
    bot = jnp.where(_iota(halo.shape, 0) >= 8 - k, hs, xs[n - 8:n])
    return jnp.concatenate([xs[0:n - 8], bot], axis=0)


def _conv(x, halo, w):
    y = x * w[CONV_K - 1:CONV_K, :]
    for k in range(1, CONV_K):
        y = y + _shift_down(x, halo, k) * w[CONV_K - 1 - k:CONV_K - k, :]
    return y


def _tri_inv(lmats, eye):
    ps = [-m for m in lmats]
    rs = [eye + p for p in ps]
    for _ in range(5):
        ps = [_dot(p, p) for p in ps]
        rs = [r + _dot(r, p) for r, p in zip(rs, ps)]
    return rs


def _gate_terms(ba_g, bat_g, alog8, dtb8, alog8t, dtb8t, K):
    g8 = -jnp.exp(alog8) * _softplus(ba_g + dtb8)
    g8t = -jnp.exp(alog8t) * _softplus(bat_g + dtb8t)
    gc8 = _ones_dot(K["tril_b"], g8)
    gl8 = _ones_dot(K["blk_b"], g8)
    gcrow8 = _dot_ones_nt(g8t, K["tril_b"])
    return g8, gc8, gl8, gcrow8


def _dn_heads(c_tile, rows, beta8, gc8, gl8, gcrow8, K, pre=None):
    return _dn_heads_groups(c_tile, [(rows, beta8, gc8, gl8, gcrow8)], K, None if pre is None else [pre])[0]


def _dn_heads_groups(c_tile, groups, K, pres=None):
    ds = [_dn_head_vec(c_tile, rows, h, beta8, gc8, gl8, gcrow8, K)
          for rows, beta8, gc8, gl8, gcrow8 in groups for h in range(NHD)]
    pls = [_dot_nt(d["kb"], d["kn"]) for d in ds]
    pms = [_dot_nt(d["qn"], d["kn"]) for d in ds]
    for d, pl_, pm in zip(ds, pls, pms):
        d.update(pl=pl_, pm=pm, lmat=jnp.where(K["strict"], pl_ * d["gam_m"], 0.0), mm=pm * d["gam_m"])
    if pres is None:
        for d, tm in zip(ds, _tri_inv([d["lmat"] for d in ds], K["eye"])):
            d.update(tm=tm, u=_dot(tm, d["vb"]), w=_dot(tm, d["kg"]))
    else:
        for d, (tm, u, w) in zip(ds, [p for pre in pres for p in pre]):
            d.update(tm=tm, u=u, w=w)
    return [ds[NHD * k:NHD * (k + 1)] for k in range(len(groups))]


def _dn_head_vec(c_tile, rows, h, beta8, gc8, gl8, gcrow8, K):
    qr = c_tile[rows, DHD * h:DHD * (h + 1)]
    kr = c_tile[rows, DW + DHD * h:DW + DHD * (h + 1)]
    v = c_tile[rows, 2 * DW + DHD * h:2 * DW + DHD * (h + 1)]
    rq = lax.rsqrt(jnp.sum(qr * qr, axis=1, keepdims=True) + EPS)
    rk = lax.rsqrt(jnp.sum(kr * kr, axis=1, keepdims=True) + EPS)
    qh, kn = qr * rq, kr * rk
    qn = qh * (DHD ** -0.5)
    beta = _col(beta8, h)
    gccol, glcol, gcrow = _col(gc8, NHD + h), _col(gl8, NHD + h), _row(gcrow8, NHD + h)
    diff = gccol - gcrow
    gam_m = jnp.exp(jnp.where(K["tril"], diff, NEG))
    gam = jnp.exp(gccol)
    egl = jnp.exp(glcol - gccol)
    kb, vb = kn * beta, v * beta
    kg = kb * gam
    return dict(qr=qr, kr=kr, v=v, rq=rq, rk=rk, qh=qh, qn=qn, kn=kn, beta=beta, diff=diff, gam_m=gam_m, gam=gam,
                egl=egl, el=jnp.exp(glcol), kb=kb, vb=vb, kg=kg, qd=qn * gam, kd=kn * egl)


def _halo_prev(width):
    return pl.BlockSpec((8, width), lambda i: (jnp.maximum(i * (TG // 8) - 1, 0), 0))


def _dn_prep(draw, conv_w, ba, bat, alog8, dtb8, alog8t, dtb8t):
    T = draw.shape[0]
    nb = T // TG
    hm = lambda w, dt: jax.ShapeDtypeStruct((NHD, T, w), dt)
    hm_spec = lambda w: pl.BlockSpec((NHD, TG, w), lambda i: (0, i, 0))
    pc = lambda r, c: jax.ShapeDtypeStruct((NHD, T // CH, r, c), _MXU)
    pc_spec = lambda r, c: pl.BlockSpec((NHD, CPT, r, c), lambda i: (0, i, 0, 0))

    def body(x_ref, halo_ref, cw_ref, ba_ref, bat_ref, al_ref, dt_ref, alt_ref, dtt_ref,
             u_ref, w_ref, kd_ref, tm_ref, wq_ref, km_ref, mq_ref, wt_ref, elb_ref, cv_ref):
        i = pl.program_id(0)
        K = _dn_masks()
        halo = jnp.where(i > 0, halo_ref[...], 0.0)
        cv = _conv(x_ref[...], halo, cw_ref[...])
        cv_ref[...] = cv
        c_tile = cv * _sigmoid(cv)
        eye128 = jnp.where(_iota((DHD, DHD), 0) == _iota((DHD, DHD), 1), 1.0, 0.0).astype(_MXU)
        def gate_inputs(g):
            rows = slice(GR * g, GR * (g + 1))
            ba_g = ba_ref[rows, :]
            _, gc8, gl8, gcrow8 = _gate_terms(ba_g, bat_ref[:, rows], al_ref[...], dt_ref[...], alt_ref[...],
                                              dtt_ref[...], K)
            return rows, _sigmoid(ba_g), gc8, gl8, gcrow8

        def store(g, rows, ds):
            mmts = [_dot_nt(d["kn"], d["qn"]) * jnp.exp(jnp.where(K["triu"], -d["diff"], NEG)) for d in ds]
            mcs = [_dot(d["mm"], K["fold_b"]) for d in ds]
            mcts = [_dot(m, K["fold_b"]) for m in mmts]
            for h, d in enumerate(ds):
                tm_ref[h, rows, :] = d["tm"].astype(_MXU)
                u_ref[h, rows, :] = d["u"]
                w_ref[h, rows, :] = d["w"].astype(_MXU)
                kd_ref[h, rows, :] = d["kd"].astype(_MXU)
                elb = jnp.broadcast_to(d["el"], (GR, DHD))
                for cc in range(GR // CH):
                    ch = slice(CH * cc, CH * (cc + 1))
                    n = (GR // CH) * g + cc
                    wq_ref[h, n, 0:CH, :] = d["w"][ch].astype(_MXU)
                    wq_ref[h, n, CH:2 * CH, :] = d["qd"][ch].astype(_MXU)
                    km_ref[h, n, 0:DHD, :] = _dot_nt(eye128, d["kd"][ch]).astype(_MXU)
                    km_ref[h, n, DHD:DHD + CH, :] = mcs[h][ch].astype(_MXU)
                    mq_ref[h, n, 0:CH, :] = mcts[h][ch].astype(_MXU)
                    mq_ref[h, n, CH:CH + DHD, :] = _dot_nt(eye128, d["qd"][ch]).astype(_MXU)
                    wt_ref[h, n] = _dot_nt(eye128, d["w"][ch]).astype(_MXU)
                    elb_ref[n:n + 1, DHD * h:DHD * (h + 1)] = elb[CH * cc:CH * cc + 1, :]

        PAIR = 4
        for g0 in range(0, NG, PAIR):
            pair = [gate_inputs(g) for g in range(g0, g0 + PAIR)]
            for k, ds in enumerate(_dn_heads_groups(c_tile, pair, K)):
                store(g0 + k, pair[k][0], ds)

    return pl.pallas_call(
        body, name="dn_prep", grid=(nb,),
        in_specs=[_rows(TG, 1536), _halo_prev(1536), _full((CONV_K, 1536)), _rows(TG, 8),
                  pl.BlockSpec((8, TG), lambda i: (0, i)), _full((1, 8)), _full((1, 8)), _full((8, 1)), _full((8, 1))],
        out_specs=[hm_spec(DHD), hm_spec(DHD), hm_spec(DHD), hm_spec(GR),
                   pc_spec(2 * CH, DHD), pc_spec(DHD + CH, CH), pc_spec(CH + DHD, CH), pc_spec(DHD, CH),
                   pl.BlockSpec((CPT, NHD * DHD), lambda i: (i, 0)), _rows(TG, 1536)],
        out_shape=[hm(DHD, F32), hm(DHD, _MXU), hm(DHD, _MXU), hm(GR, _MXU),
                   pc(2 * CH, DHD), pc(DHD + CH, CH), pc(CH + DHD, CH), pc(DHD, CH),
                   jax.ShapeDtypeStruct((T // CH, NHD * DHD), F32), jax.ShapeDtypeStruct((T, 1536), F32)],
        compiler_params=_cp(("arbitrary",)),
    )(draw, draw, conv_w, ba, bat, alog8, dtb8, alog8t, dtb8t)


def _dn_scan(u, wq, km, elb):
    T = u.shape[1]
    TG, CPT = TS, TS // CH
    nb = T // TG
    hm_spec = lambda wd: pl.BlockSpec((NHD, TG, wd), lambda i: (0, i, 0))

    def body(u_ref, wq_ref, km_ref, elb_ref, o_ref, vn_ref, sn_ref, S):
        @pl.when(pl.program_id(0) == 0)
        def _():
            S[...] = jnp.zeros_like(S)

        sub8 = _iota((CPT, DHD), 0)
        heads = range(NHD)

        def chunk(cc, carry):
            rs = pl.ds(pl.multiple_of(cc * CH, CH), CH)
            sh = [S[h] for h in heads]
            sb = [s.astype(_MXU) for s in sh]
            r1 = [_dot(wq_ref[h, cc], sb[h]) for h in heads]
            vnb = [(u_ref[h, rs, :] - r1[h][0:CH]).astype(_MXU) for h in heads]
            r2 = [_dot(km_ref[h, cc], vnb[h]) for h in heads]
            for h in heads:
                el = jnp.sum(jnp.where(sub8 == cc, elb_ref[:, DHD * h:DHD * (h + 1)], 0.0), axis=0, keepdims=True)
                S[h] = sh[h] * el + r2[h][0:DHD]
                sn_ref[cc, h] = sb[h]
                vn_ref[h, rs, :] = vnb[h]
                o_ref[h, rs, :] = r1[h][CH:2 * CH] + r2[h][DHD:DHD + CH]
            return carry

        lax.fori_loop(0, CPT, chunk, 0)

    return pl.pallas_call(
        body, name="dn_scan", grid=(nb,),
        in_specs=[hm_spec(DHD), pl.BlockSpec((NHD, CPT, 2 * CH, DHD), lambda i: (0, i, 0, 0)),
                  pl.BlockSpec((NHD, CPT, DHD + CH, CH), lambda i: (0, i, 0, 0)),
                  pl.BlockSpec((CPT, NHD * DHD), lambda i: (i, 0))],
        out_specs=[hm_spec(DHD), hm_spec(DHD), pl.BlockSpec((CPT, NHD, DHD, DHD), lambda i: (i, 0, 0, 0))],
        out_shape=[jax.ShapeDtypeStruct((NHD, T, DHD), F32), jax.ShapeDtypeStruct((NHD, T, DHD), _MXU),
                   jax.ShapeDtypeStruct((T // CH, NHD, DHD, DHD), _MXU)],
        scratch_shapes=[pltpu.VMEM((NHD, DHD, DHD), F32)],
        compiler_params=_cp(("arbitrary",)),
    )(u, wq, km, elb)


def _dn_scan_bwd(do, mq, kd, wt, sn, vn, elb):
    T = do.shape[1]
    TG, CPT = TS, TS // CH
    nb = T // TG
    rev = lambda wd: pl.BlockSpec((NHD, TG, wd), lambda i: (0, nb - 1 - i, 0))
    rev_t = lambda r: pl.BlockSpec((NHD, CPT, r, CH), lambda i: (0, nb - 1 - i, 0, 0))

    def body(do_ref, mq_ref, kd_ref, wt_ref, sn_ref, vn_ref, elb_ref,
             du_ref, dw_ref, dqd_ref, dkd_ref, dgx_ref, dS):
        @pl.when(pl.program_id(0) == 0)
        def _():
            dS[...] = jnp.zeros_like(dS)

        last_row = _iota((CH, DHD), 0) == CH - 1
        sub8 = _iota((CPT, DHD), 0)
        heads = range(NHD)

        def chunk(k, carry):
            cc = CPT - 1 - k
            rs = pl.ds(pl.multiple_of(cc * CH, CH), CH)
            dsh = [dS[h] for h in heads]
            dsb = [d.astype(_MXU) for d in dsh]
            doc = [do_ref[h, rs, :].astype(_MXU) for h in heads]
            a = [_dot(mq_ref[h, cc], doc[h]) for h in heads]
            b = [_dot(kd_ref[h, rs, :], dsb[h]) for h in heads]
            dvn = [a[h][0:CH] + b[h] for h in heads]
            dvnb = [d.astype(_MXU) for d in dvn]
            e = [_dot(wt_ref[h, cc], dvnb[h]) for h in heads]
            for h in heads:
                el = jnp.sum(jnp.where(sub8 == cc, elb_ref[:, DHD * h:DHD * (h + 1)], 0.0), axis=0, keepdims=True)
                sn = sn_ref[cc, h]
                dS[h] = a[h][CH:CH + DHD] + dsh[h] * el - e[h]
                du_ref[h, rs, :] = dvn[h]
                c = _dot_nt(jnp.concatenate([doc[h], dvnb[h]], axis=0), sn)
                dqd_ref[h, rs, :] = c[0:CH]
                dw_ref[h, rs, :] = -c[CH:2 * CH]
                dkd_ref[h, rs, :] = _dot_nt(vn_ref[h, rs, :], dsb[h])
                part = jnp.sum(dsh[h] * sn.astype(F32), axis=0, keepdims=True) * el
                dgx_ref[h, rs, :] = jnp.where(last_row, part, 0.0)
            return carry

        lax.fori_loop(0, CPT, chunk, 0)

    o = jax.ShapeDtypeStruct((NHD, T, DHD), F32)
    return pl.pallas_call(
        body, name="dn_scan_bwd", grid=(nb,),
        in_specs=[rev(DHD), rev_t(CH + DHD), rev(DHD), rev_t(DHD),
                  pl.BlockSpec((CPT, NHD, DHD, DHD), lambda i: (nb - 1 - i, 0, 0, 0)), rev(DHD),
                  pl.BlockSpec((CPT, NHD * DHD), lambda i: (nb - 1 - i, 0))],
        out_specs=[rev(DHD)] * 5,
        out_shape=[o] * 5,
        scratch_shapes=[pltpu.VMEM((NHD, DHD, DHD), F32)],
        compiler_params=_cp(("arbitrary",)),
    )(do, mq, kd, wt, sn, vn, elb)


def _put_col(acc, k, col):
    return jnp.where(_iota(acc.shape, 1) == k, col, acc)


def _dn_post_bwd(draw, cv, conv_w, ba, bat, alog8, dtb8, alog8t, dtb8t, du, dw, dqd, dkd, dgx, do, vn, tm, u, w):
    T = draw.shape[0]
    nb = T // TG
    hm_spec = lambda wd: pl.BlockSpec((NHD, TG, wd), lambda i: (0, nb - 1 - i, 0))
    rrows = lambda w: pl.BlockSpec((TG, w), lambda i: (nb - 1 - i, 0))

    def body(x_ref, cv_ref, cw_ref, ba_ref, bat_ref, al_ref, dt_ref, alt_ref, dtt_ref,
             du_ref, dw_ref, dqd_ref, dkd_ref, dgx_ref, do_ref, vn_ref, tm_ref, u_ref, w_ref,
             dx_ref, dba_ref, sm_ref, dcw_ref, dc_ref, nxt_ref):
        i = pl.program_id(0)

        @pl.when(i == 0)
        def _():
            sm_ref[...] = jnp.zeros_like(sm_ref)
            dcw_ref[...] = jnp.zeros_like(dcw_ref)
            nxt_ref[...] = jnp.zeros_like(nxt_ref)

        K = _dn_masks()
        cv = cv_ref[...]
        sg = _sigmoid(cv)
        c_tile = cv * sg
        dsilu = sg * (1.0 + cv * (1.0 - sg))
        for g in range(NG):
            rows = slice(GR * g, GR * (g + 1))
            ba_g = ba_ref[rows, :]
            g8, gc8, gl8, gcrow8 = _gate_terms(ba_g, bat_ref[:, rows], al_ref[...], dt_ref[...], alt_ref[...],
                                               dtt_ref[...], K)
            beta8 = _sigmoid(ba_g)
            dgc8 = jnp.zeros((GR, 8), F32)
            rd8 = jnp.zeros((GR, 8), F32)
            dbeta8 = jnp.zeros((GR, 8), F32)
            pre = [(tm_ref[h, rows, :], u_ref[h, rows, :], w_ref[h, rows, :]) for h in range(NHD)]
            ds = _dn_heads(c_tile, rows, beta8, gc8, gl8, gcrow8, K, pre)
            H = range(NHD)
            eye_b = K["eye_b"].astype(_MXU)
            gam_t = [jnp.exp(jnp.where(K["triu"], -d["diff"], NEG)) for d in ds]
            doh = [do_ref[h, rows, :] for h in H]
            vnh = [vn_ref[h, rows, :] for h in H]
            tt = [_dot_nt(eye_b, d["tm"]) for d in ds]
            dvb = [_dot(tt[h], du_ref[h, rows, :]) for h in H]
            dkg = [_dot(tt[h], dw_ref[h, rows, :]) for h in H]
            plt = [_dot_nt(d["kn"], d["kb"]) for d in ds]
            pmt = [_dot_nt(d["kn"], d["qn"]) for d in ds]
            da = [-(_dot_nt(dvb[h], ds[h]["u"]) + _dot_nt(dkg[h], ds[h]["w"])) for h in H]
            dat = [-(_dot_nt(ds[h]["u"], dvb[h]) + _dot_nt(ds[h]["w"], dkg[h])) for h in H]
            dpm = [jnp.where(K["tril"], _dot_nt(doh[h], vnh[h]), 0.0) * ds[h]["gam_m"] for h in H]
            dpmt = [jnp.where(K["triu"], _dot_nt(vnh[h], doh[h]), 0.0) * gam_t[h] for h in H]
            dpl = [jnp.where(K["strict"], da[h], 0.0) * ds[h]["gam_m"] for h in H]
            dplt = [jnp.where(K["strict_u"], dat[h], 0.0) * gam_t[h] for h in H]
            dkb = [_dot(dpl[h], ds[h]["kn"]) + dkg[h] * ds[h]["gam"] for h in H]
            dqn = [_dot(dpm[h], ds[h]["kn"]) + dqd_ref[h, rows, :] * ds[h]["gam"] for h in H]
            dknm = [_dot(dplt[h], ds[h]["kb"]) + _dot(dpmt[h], ds[h]["qn"]) for h in H]
            for h, d in enumerate(ds):
                kn, dqdh, dkdh = d["kn"], dqd_ref[h, rows, :], dkd_ref[h, rows, :]
                dkn = dknm[h] + dkdh * d["egl"] + dkb[h] * d["beta"]
                dkd_kd = dkdh * d["kd"]
                rd = jnp.sum(dkd_kd, axis=1, keepdims=True)
                dgc = jnp.sum(dpl[h] * d["pl"] + dpm[h] * d["pm"] - dplt[h] * plt[h] - dpmt[h] * pmt[h]
                              + dqdh * d["qd"] + dkg[h] * d["kg"] - dkd_kd + dgx_ref[h, rows, :],
                              axis=1, keepdims=True)
                dgc8 = _put_col(dgc8, NHD + h, dgc)
                rd8 = _put_col(rd8, NHD + h, rd)
                dbeta = jnp.sum(dkb[h] * kn + dvb[h] * d["v"], axis=1, keepdims=True)
                dbeta8 = _put_col(dbeta8, h, dbeta)
                dqh = dqn[h] * (DHD ** -0.5)
                qh = d["qh"]
                dqr = d["rq"] * (dqh - qh * jnp.sum(dqh * qh, axis=1, keepdims=True))
                dkr = d["rk"] * (dkn - kn * jnp.sum(dkn * kn, axis=1, keepdims=True))
                cq = slice(DHD * h, DHD * (h + 1))
                ck = slice(DW + DHD * h, DW + DHD * (h + 1))
                cvv = slice(2 * DW + DHD * h, 2 * DW + DHD * (h + 1))
                dc_ref[rows, cq] = dqr * dsilu[rows, cq]
                dc_ref[rows, ck] = dkr * dsilu[rows, ck]
                dc_ref[rows, cvv] = dvb[h] * d["beta"] * dsilu[rows, cvv]
            dgc8 = dgc8 + jnp.where(K["last"], _ones_dot(K["blk_b"], rd8), 0.0)
            dg8 = _ones_dot(K["triu_b"], dgc8)
            sgm = _sigmoid(ba_g + dt_ref[...])
            dalpha = dg8 * (-jnp.exp(al_ref[...])) * sgm
            lane8 = _iota((GR, 8), 1)
            dba_ref[rows, :] = jnp.where(lane8 < NHD, dbeta8 * beta8 * (1.0 - beta8), dalpha)
            valid = lane8 >= NHD
            sm_ref[0:1, 0:8] += jnp.sum(jnp.where(valid, dg8 * g8, 0.0), axis=0, keepdims=True)
            sm_ref[1:2, 0:8] += jnp.sum(jnp.where(valid, dalpha, 0.0), axis=0, keepdims=True)

        dcv = dc_ref[...]
        xv = x_ref[...]
        nxt = nxt_ref[...]
        w = cw_ref[...]
        dx = dcv * w[CONV_K - 1:CONV_K, :]
        dcw_ref[CONV_K - 1:CONV_K, :] += jnp.sum(dcv * xv, axis=0, keepdims=True)
        for k in range(1, CONV_K):
            j = CONV_K - 1 - k
            up = _shift_up(dcv, nxt, k)
            dx = dx + up * w[j:j + 1, :]
            dcw_ref[j:j + 1, :] += jnp.sum(up * xv, axis=0, keepdims=True)
        dx_ref[...] = dx
        nxt_ref[...] = dcv[0:8]

    return pl.pallas_call(
        body, name="dn_post_bwd", grid=(nb,),
        in_specs=[rrows(1536), rrows(1536), _full((CONV_K, 1536)), rrows(8),
                  pl.BlockSpec((8, TG), lambda i: (0, nb - 1 - i)), _full((1, 8)), _full((1, 8)), _full((8, 1)),
                  _full((8, 1)),
                  hm_spec(DHD), hm_spec(DHD), hm_spec(DHD), hm_spec(DHD), hm_spec(DHD), hm_spec(DHD), hm_spec(DHD),
                  hm_spec(GR), hm_spec(DHD), hm_spec(DHD)],
        out_specs=[rrows(1536), rrows(8), _full((8, LANES)), _full((8, 1536))],
        out_shape=[jax.ShapeDtypeStruct((T, 1536), F32), jax.ShapeDtypeStruct((T, 8), F32),
                   jax.ShapeDtypeStruct((8, LANES), F32), jax.ShapeDtypeStruct((8, 1536), F32)],
        scratch_shapes=[pltpu.VMEM((TG, 1536), F32), pltpu.VMEM((8, 1536), F32)],
        compiler_params=_cp(("arbitrary",)),
    )(draw, cv, conv_w, ba, bat, alog8, dtb8, alog8t, dtb8t, du, dw, dqd, dkd, dgx, do, vn, tm, u, w)


def _rms(x):
    return lax.rsqrt(jnp.mean(x * x, axis=1, keepdims=True) + EPS)


def _rms_bwd(dy, xh, r, g):
    dxh = dy * g
    return r * (dxh - xh * jnp.mean(dxh * xh, axis=1, keepdims=True))


def _hm_rows(tm):
    return pl.BlockSpec((NHD, tm, DHD), lambda i: (0, i, 0))


def _post_mix(apre, o, z, x, w_out, g_a, g_dn):
    T = x.shape[0]

    def body(ap_ref, o_ref, z_ref, x_ref, w_ref, ga_ref, gd_ref, x1_ref, mix_ref):
        ap = ap_ref[...]
        parts = [ap * _rms(ap) * ga_ref[...]]
        zz = z_ref[...]
        for h in range(NHD):
            oh = o_ref[h]
            zh = zz[:, DHD * h:DHD * (h + 1)]
            parts.append(oh * _rms(oh) * gd_ref[...] * (zh * _sigmoid(zh)))
        mix = jnp.concatenate(parts, axis=1).astype(_MXU)
        mix_ref[...] = mix
        x1_ref[...] = x_ref[...] + jnp.dot(mix, w_ref[...], preferred_element_type=F32)

    return pl.pallas_call(
        body, name="post_mix", grid=(T // TM,),
        in_specs=[_rows(TM, AW), _hm_rows(TM), _rows(TM, DW), _rows(TM, D), _full((D, D)), _full((1, AW)),
                  _full((1, DHD))],
        out_specs=[_rows(TM, D), _rows(TM, D)],
        out_shape=[jax.ShapeDtypeStruct((T, D), F32), jax.ShapeDtypeStruct((T, D), _MXU)],
        compiler_params=_cp(("arbitrary",)),
    )(apre, o, z, x, w_out, g_a, g_dn)


def _ffn(x1, tgt, wl_all, g_ffn):
    T = x1.shape[0]
    SH = FF // N_DEV
    nt = (((1,), (1,)), ((), ()))

    def body(x_ref, t_ref, wl_hbm, g_ref,
             dx1_ref, dx1b_ref, h2_ref, act_ref, dgu_ref, dyb_ref, loss_ref, dg_ref, wg, wu, wd, sem):
        @pl.when(pl.program_id(0) == 0)
        def _():
            cps = [pltpu.make_async_copy(wl_hbm.at[dev, pl.ds(128 + SH * k, SH), :], dst.at[pl.ds(SH * dev, SH), :],
                                         sem.at[N_DEV * k + dev])
                   for k, dst in enumerate((wg, wu, wd)) for dev in range(N_DEV)]
            for cp in cps:
                cp.start()
            for cp in cps:
                cp.wait()
            loss_ref[...] = jnp.zeros_like(loss_ref)
            dg_ref[...] = jnp.zeros_like(dg_ref)

        xv = x_ref[...]
        r = _rms(xv)
        xh = xv * r
        gg = g_ref[...]
        h2 = (xh * gg).astype(_MXU)
        h2_ref[...] = h2
        gate = lax.dot_general(h2, wg[...], nt, preferred_element_type=F32)
        up = lax.dot_general(h2, wu[...], nt, preferred_element_type=F32)
        sg = _sigmoid(gate)
        silu = gate * sg
        act = (silu * up).astype(_MXU)
        act_ref[...] = act
        y = xv + jnp.dot(act, wd[...], preferred_element_type=F32)
        err = y - t_ref[...]
        loss_ref[...] += jnp.sum(err * err, axis=0, keepdims=True)
        dy = err * (1.0 / D)
        dyb = dy.astype(_MXU)
        dyb_ref[...] = dyb
        dact = lax.dot_general(dyb, wd[...], nt, preferred_element_type=F32)
        dgate = (dact * up * (sg * (1.0 + gate * (1.0 - sg)))).astype(_MXU)
        dup = (dact * silu).astype(_MXU)
        dgu_ref[:, 0:FF] = dgate
        dgu_ref[:, FF:2 * FF] = dup
        dh2 = (jnp.dot(dgate, wg[...], preferred_element_type=F32)
               + jnp.dot(dup, wu[...], preferred_element_type=F32))
        dg_ref[...] += jnp.sum(dh2 * xh, axis=0, keepdims=True)
        dx1 = dy + _rms_bwd(dh2, xh, r, gg)
        dx1_ref[...] = dx1
        dx1b_ref[...] = dx1.astype(_MXU)

    anyspec = pl.BlockSpec(memory_space=pl.ANY)
    sd = lambda w, dt: jax.ShapeDtypeStruct((T, w), dt)
    return pl.pallas_call(
        body, name="ffn", grid=(T // TF,),
        in_specs=[_rows(TF, D), _rows(TF, D), anyspec, _full((1, D))],
        out_specs=[_rows(TF, D), _rows(TF, D), _rows(TF, D), _rows(TF, FF), _rows(TF, 2 * FF), _rows(TF, D),
                   _full((1, D)), _full((1, D))],
        out_shape=[sd(D, F32), sd(D, _MXU), sd(D, _MXU), sd(FF, _MXU), sd(2 * FF, _MXU), sd(D, _MXU),
                   jax.ShapeDtypeStruct((1, D), F32), jax.ShapeDtypeStruct((1, D), F32)],
        scratch_shapes=[pltpu.VMEM((FF, D), _MXU)] * 3 + [pltpu.SemaphoreType.DMA((3 * N_DEV,))],
        compiler_params=_cp(("arbitrary",)),
    )(x1, tgt, wl_all, g_ffn)


def _mix_bwd(dx1b, w_out, apre, o, z, g_a, g_dn):
    T = dx1b.shape[0]

    def body(dx_ref, w_ref, ap_ref, o_ref, z_ref, ga_ref, gd_ref, dap_ref, do_ref, dz_ref, dga_ref, dgd_ref):
        @pl.when(pl.program_id(0) == 0)
        def _():
            dga_ref[...] = jnp.zeros_like(dga_ref)
            dgd_ref[...] = jnp.zeros_like(dgd_ref)

        dmix = lax.dot_general(dx_ref[...], w_ref[...], (((1,), (1,)), ((), ())), preferred_element_type=F32)
        ap = ap_ref[...]
        ra = _rms(ap)
        ah = ap * ra
        da = dmix[:, 0:AW]
        dga_ref[...] += jnp.sum(da * ah, axis=0, keepdims=True)
        dap_ref[...] = _rms_bwd(da, ah, ra, ga_ref[...])
        zz = z_ref[...]
        gd = gd_ref[...]
        for h in range(NHD):
            cs = slice(DHD * h, DHD * (h + 1))
            dd = dmix[:, AW + DHD * h:AW + DHD * (h + 1)]
            oh = o_ref[h]
            ro = _rms(oh)
            ohh = oh * ro
            zh = zz[:, cs]
            sz = _sigmoid(zh)
            dz_ref[:, cs] = dd * (ohh * gd) * (sz * (1.0 + zh * (1.0 - sz)))
            don = dd * (zh * sz)
            dgd_ref[...] += jnp.sum(don * ohh, axis=0, keepdims=True)
            do_ref[h] = _rms_bwd(don, ohh, ro, gd)

    return pl.pallas_call(
        body, name="mix_bwd", grid=(T // TM,),
        in_specs=[_rows(TM, D), _full((D, D)), _rows(TM, AW), _hm_rows(TM), _rows(TM, DW), _full((1, AW)),
                  _full((1, DHD))],
        out_specs=[_rows(TM, AW), _hm_rows(TM), _rows(TM, DW), _full((1, AW)), _full((1, DHD))],
        out_shape=[jax.ShapeDtypeStruct((T, AW), F32), jax.ShapeDtypeStruct((NHD, T, DHD), F32),
                   jax.ShapeDtypeStruct((T, DW), F32), jax.ShapeDtypeStruct((1, AW), F32),
                   jax.ShapeDtypeStruct((1, DHD), F32)],
        compiler_params=_cp(("arbitrary",)),
    )(dx1b, w_out, apre, o, z, g_a, g_dn)


DPW = 3712


def _inproj_bwd(dqn, dkn, dv, araw, ddraw, dz, dba, x, dx1, w_int, w_ba, g_mix, qg_t, kg_t):
    T = x.shape[0]

    def body(dqn_ref, dkn_ref, dv_ref, ar_ref, dd_ref, dz_ref, dba_ref, x_ref, dx1_ref, w_hbm, wba_ref, g_ref, qg_ref,
             kg_ref, dx_ref, dp_ref, dgm_ref, dqg_ref, dkg_ref, w_ref, w_sem):
        @pl.when(pl.program_id(0) == 0)
        def _():
            cp = pltpu.make_async_copy(w_hbm, w_ref, w_sem)
            cp.start()
            cp.wait()
            dgm_ref[...] = jnp.zeros_like(dgm_ref)
            dqg_ref[...] = jnp.zeros_like(dqg_ref)
            dkg_ref[...] = jnp.zeros_like(dkg_ref)

        bd = _block_ones(AW // 2, DHA)

        def head_norm_bwd(raw, dyn, gain, dg_ref):
            r = _head_rms(raw, bd, DHA)
            xh = raw * r
            dg_ref[...] += jnp.sum(dyn * xh, axis=0, keepdims=True)
            dxh = dyn * gain
            return r * (dxh - xh * (_head_sum(dxh * xh, bd) * (1.0 / DHA)))

        def segment(lo, val):
            vb = val.astype(_MXU)
            dp_ref[:, lo:lo + val.shape[1]] = vb
            return jnp.dot(vb, w_ref[lo:lo + val.shape[1], :], preferred_element_type=F32)

        dh = segment(1536, dd_ref[...]) + segment(2 * AW, dv_ref[...]) + segment(3072, dz_ref[...])
        dbab = dba_ref[...].astype(_MXU)
        dp_ref[:, 3584:DPW] = jnp.zeros((TM, DPW - 3584), _MXU)
        dp_ref[:, 3584:3592] = dbab
        dh = dh + lax.dot_general(dbab, wba_ref[...], (((1,), (1,)), ((), ())), preferred_element_type=F32)
        ar = ar_ref[...]
        dq = head_norm_bwd(ar[:, 0:AW], dqn_ref[...] * (DHA ** -0.5), qg_ref[...], dqg_ref)
        dk = head_norm_bwd(ar[:, AW:2 * AW], dkn_ref[...], kg_ref[...], dkg_ref)
        dh = dh + segment(0, dq) + segment(AW, dk)
        xv = x_ref[...]
        r = _rms(xv)
        xh = xv * r
        dgm_ref[...] += jnp.sum(dh * xh, axis=0, keepdims=True)
        dx_ref[...] = dx1_ref[...] + _rms_bwd(dh, xh, r, g_ref[...])

    return pl.pallas_call(
        body, name="inproj_bwd", grid=(T // TM,),
        in_specs=[_rows(TM, AW), _rows(TM, AW), _rows(TM, AW), _rows(TM, 1536), _rows(TM, 1536), _rows(TM, DW),
                  _rows(TM, 8), _rows(TM, D), _rows(TM, D), pl.BlockSpec(memory_space=pl.ANY), _full((D, 8)),
                  _full((1, D)), _full((1, AW)), _full((1, AW))],
        out_specs=[_rows(TM, D), _rows(TM, DPW), _full((1, D)), _full((1, AW)), _full((1, AW))],
        out_shape=[jax.ShapeDtypeStruct((T, D), F32), jax.ShapeDtypeStruct((T, DPW), _MXU),
                   jax.ShapeDtypeStruct((1, D), F32), jax.ShapeDtypeStruct((1, AW), F32),
                   jax.ShapeDtypeStruct((1, AW), F32)],
        scratch_shapes=[pltpu.VMEM((3584, D), _MXU), pltpu.SemaphoreType.DMA],
        compiler_params=_cp(("arbitrary",)),
    )(dqn, dkn, dv, araw, ddraw, dz, dba, x, dx1, w_int, w_ba, g_mix, qg_t, kg_t)


def _wgrad(a, b, name, tk=1024, tn=None, out_dtype=F32, transposed=False):
    T, M = a.shape
    N = b.shape[1]
    tn = N if tn is None else tn
    assert T % tk == 0 and N % tn == 0, (T, tk, N, tn)
    nk = T // tk

    def body(a_ref, b_ref, o_ref, acc):
        k = pl.program_id(1)

        @pl.when(k == 0)
        def _():
            acc[...] = jnp.zeros_like(acc)

        acc[...] += lax.dot_general(a_ref[...], b_ref[...], (((0,), (0,)), ((), ())), preferred_element_type=F32)

        @pl.when(k == nk - 1)
        def _():
            r = acc[...]
            o_ref[...] = (r.T if transposed else r).astype(out_dtype)

    if transposed:
        out_spec, out_shape = pl.BlockSpec((tn, M), lambda j, k: (j, 0)), (N, M)
    else:
        out_spec, out_shape = pl.BlockSpec((M, tn), lambda j, k: (0, j)), (M, N)
    return pl.pallas_call(
        body, name=name, grid=(N // tn, nk),
        in_specs=[pl.BlockSpec((tk, M), lambda j, k: (k, 0)), pl.BlockSpec((tk, tn), lambda j, k: (k, j))],
        out_specs=out_spec,
        out_shape=jax.ShapeDtypeStruct(out_shape, out_dtype),
        scratch_shapes=[pltpu.VMEM((M, tn), F32)],
        compiler_params=_cp(("arbitrary", "arbitrary")),
    )(a, b)


def _adamw(parts, w, m, v, name, tr, send=None):
    K, R, W = parts.shape
    n = R // tr

    def body(p_ref, w_ref, m_ref, v_ref, *rest):
        if send is not None:
            send_ref, g_ref, d_ref, nm_ref, nv_ref, recv_ref, send_sems, recv_sems, local_sem = rest
            sc = _Scatter(send_ref, recv_ref, send_sems, recv_sems, local_sem)
            pl.when(pl.program_id(0) == 0)(sc.start)
        else:
            g_ref, d_ref, nm_ref, nv_ref = rest
        g = p_ref[0].astype(F32)
        for k in range(1, K):
            g = g + p_ref[k].astype(F32)
        g_ref[...] = g
        nm = ADAM_B1 * m_ref[...] + (1.0 - ADAM_B1) * g
        nv = ADAM_B2 * v_ref[...] + (1.0 - ADAM_B2) * (g * g)
        nm_ref[...] = nm
        nv_ref[...] = nv
        m_hat = nm / (1.0 - ADAM_B1 ** ADAM_STEP)
        v_hat = nv / (1.0 - ADAM_B2 ** ADAM_STEP)
        d_ref[...] = -ADAM_LR * (m_hat / (jnp.sqrt(v_hat) + ADAM_EPS) + ADAM_WD * w_ref[...])
        if send is not None:
            pl.when(pl.program_id(0) == n - 1)(sc.finish)

    o = jax.ShapeDtypeStruct((R, W), F32)
    anyspec = pl.BlockSpec(memory_space=pl.ANY)
    hosted = send is not None
    return pl.pallas_call(
        body, name=name, grid=(n,),
        in_specs=[pl.BlockSpec((K, tr, W), lambda i: (0, i, 0)), _rows(tr, W), _rows(tr, W), _rows(tr, W)]
        + ([anyspec] if hosted else []),
        out_specs=[_rows(tr, W)] * 4 + ([anyspec] if hosted else []),
        out_shape=[o] * 4 + ([jax.ShapeDtypeStruct(send.shape, send.dtype)] if hosted else []),
        scratch_shapes=_COMM_SEMS if hosted else [],
        compiler_params=_cp(("arbitrary",)),
    )(*((parts, w, m, v) + ((send,) if hosted else ())))


SM_ROWS = 136
R_GMIX, R_GFFN, R_QG, R_KG, R_GA, R_GDN, R_ALOG, R_DT, R_LOSS, R_CONV, R_REL = 0, 8, 16, 24, 32, 40, 48, 49, 56, 64, 112


def _small_reduce(gathered):
    def body(p_ref, o_ref):
        s = p_ref[0]
        for k in range(1, N_DEV):
            s = s + p_ref[k]
        o_ref[...] = s
        for r0 in (R_QG, R_KG):
            rs = jnp.sum(s[r0:r0 + 4], axis=0, keepdims=True)
            o_ref[r0:r0 + 1, :] = rs + pltpu.roll(rs, DHA, 1)
        tot = jnp.sum(jnp.sum(s[R_LOSS:R_LOSS + 8], axis=0, keepdims=True), axis=1, keepdims=True)
        o_ref[R_LOSS:R_LOSS + 1, :] = jnp.broadcast_to(tot * (0.5 / D), (1, LANES))

    return pl.pallas_call(
        body, name="small_reduce",
        out_shape=jax.ShapeDtypeStruct((SM_ROWS, LANES), F32),
    )(gathered)


_WIRE = jnp.bfloat16
RA_USED, RA = 449, 464
RL = 128 + 3 * 352


def _pack_rows(parts, rows=None):
    p = jnp.concatenate([t.reshape(-1, D) for t in parts], axis=0) if len(parts) > 1 else parts[0].reshape(-1, D)
    return p if rows is None else jnp.pad(p, ((0, rows - p.shape[0]), (0, 0)))


def _unpack_rows(packed, shapes):
    out, r = [], 0
    for shp in shapes:
        nr = math.prod(shp) // D
        out.append(packed[r:r + nr].reshape(shp))
        r += nr
    return out


def _pad8(t):
    return jnp.pad(t, ((0, (-t.shape[0]) % 8), (0, 0)))


PART = 8 * LANES


def _pack_lanes(parts):
    rows = []
    for p in parts:
        f = p.reshape(-1)
        rows.append(jnp.pad(f, (0, (-f.shape[0]) % PART)).reshape(-1, LANES))
    return jnp.concatenate(rows, axis=0)


def _unpack_lanes(packed, shapes):
    out, r = [], 0
    for shp in shapes:
        n = math.prod(shp)
        nr = 8 * -(-n // PART)
        out.append(packed[r:r + nr].reshape(-1)[:n].reshape(shp))
        r += nr
    return out


def kernel(x, norm_mix_g, w_in, attn_q_norm_g, attn_k_norm_g, rel_bias, attn_out_norm_g, conv_w, a_log, dt_bias, dn_out_norm_g, w_out, norm_ffn_g, w_gate, w_up, w_down, loss_target, m_norm_mix_g, m_w_in, m_attn_q_norm_g, m_attn_k_norm_g, m_rel_bias, m_attn_out_norm_g, m_conv_w, m_a_log, m_dt_bias, m_dn_out_norm_g, m_w_out, m_norm_ffn_g, m_w_gate, m_w_up, m_w_down, v_norm_mix_g, v_w_in, v_attn_q_norm_g, v_attn_k_norm_g, v_rel_bias, v_attn_out_norm_g, v_conv_w, v_a_log, v_dt_bias, v_dn_out_norm_g, v_w_out, v_norm_ffn_g, v_w_gate, v_w_up, v_w_down):
    xs, tgt = x[0], loss_target[0]
    my_idx = 4 * lax.axis_index("x") + 2 * lax.axis_index("y") + lax.axis_index("c")
    late_w = (w_out[0], w_gate[0], w_up[0], w_down[0])

    tab, tabt, wa_all, cw_all = _bias_tables(
        jnp.pad(rel_bias[0].T, ((0, 0), (0, VAR0 - 257))), _pack_rows([w_in[0].T.astype(_MXU)], RA),
        jnp.pad(conv_w[0], ((0, 4), (0, 64))))
    W_in_t = wa_all[:, 0:RA_USED].reshape(N_DEV * RA_USED, D)
    W_int, W_ba = W_in_t[0:3584], W_in_t[3584:3592].T
    conv_full = cw_all[:, 0:CONV_K, 0:192].transpose(1, 0, 2).reshape(CONV_K, 1536)

    qg_t = jnp.tile(attn_q_norm_g, (1, NHA))
    kg_t = jnp.tile(attn_k_norm_g, (1, NHA))
    z4 = jnp.zeros((1, NHD), F32)
    alog8 = jnp.concatenate([z4, a_log], axis=1)
    dtb8 = jnp.concatenate([z4, dt_bias], axis=1)

    late_t = lambda ts: (ts[0], ts[1].T, ts[2].T, ts[3])
    araw, an, draw, z, ba, hb, wl_all = _inproj(xs, norm_mix_g, W_int, W_ba, qg_t, kg_t,
                                                _pack_rows([w.astype(_MXU) for w in late_t(late_w)]))
    W_out = wl_all[:, 0:128].reshape(D, D)
    apre = _attn_fwd(an, tab)
    bat = ba.T
    dn_args = (draw, conv_full, ba, bat, alog8, dtb8, alog8.T, dtb8.T)
    u, w, kd, tm, wq, km, mq, wt, elb, cv = _dn_prep(*dn_args)
    o, vn, sn = _dn_scan(u, wq, km, elb)
    x1, mix = _post_mix(apre, o, z, xs, W_out, attn_out_norm_g, dn_out_norm_g)

    dx1, dx1b, h2, act, dgu, dyb, loss_row, dgffn = _ffn(x1, tgt, wl_all, norm_ffn_g)

    gW_out = _wgrad(mix, dx1b, "wgrad_out", out_dtype=_WIRE)
    gW_gu_t = _wgrad(h2, dgu, "wgrad_gate_up", tn=FF, out_dtype=_WIRE, transposed=True)
    gW_down = _wgrad(dyb, act, "wgrad_down", out_dtype=_WIRE, transposed=True)
    send_late = jnp.concatenate(
        [gW_out.reshape(N_DEV, 128, D), gW_gu_t[0:FF].reshape(N_DEV, 352, D), gW_gu_t[FF:].reshape(N_DEV, 352, D),
         gW_down.reshape(N_DEV, 352, D)], axis=1)

    dap, do, dz, dga, dgdn = _mix_bwd(dx1b, W_out, apre, o, z, attn_out_norm_g, dn_out_norm_g)
    dqn, dkn, dv, dtabt, recv_late = _attn_bwd(an, dap, tabt, send_late)
    drel = _bias_grad(dtabt)
    du, dw, dqd, dkd, dgx = _dn_scan_bwd(do, mq, kd, wt, sn, vn, elb)
    ddraw, dba, sm, dcw = _dn_post_bwd(draw, cv, *dn_args[1:], du, dw, dqd, dkd, dgx, do, vn, tm, u, w)
    gx, dproj, dgmix, dqg, dkg = _inproj_bwd(dqn, dkn, dv, araw, ddraw, dz, dba, xs, dx1, W_int, W_ba, norm_mix_g,
                                             qg_t, kg_t)

    gW_in_t = _wgrad(hb, dproj, "wgrad_in", tk=512, out_dtype=_WIRE, transposed=True)
    send_in = jnp.pad(gW_in_t[0:N_DEV * RA_USED].reshape(N_DEV, RA_USED, D), ((0, 0), (0, RA - RA_USED), (0, 0)))
    late_m = (m_w_out[0], m_w_gate[0], m_w_up[0], m_w_down[0])
    late_v = (v_w_out[0], v_w_gate[0], v_w_up[0], v_w_down[0])
    *outs_late, recv_in = _adamw(recv_late, _pack_rows(late_t(late_w)), _pack_rows(late_t(late_m)),
                                 _pack_rows(late_t(late_v)), "adamw_late", 32, send=send_in)
    outs_in = _adamw(recv_in, _pack_rows([w_in[0].T], RA), _pack_rows([m_w_in[0].T], RA),
                     _pack_rows([v_w_in[0].T], RA), "adamw_w_in", 16)
    late_t_shapes = [t.shape for t in late_t(late_w)]
    big = [[a[0:RA_USED].T] + list(late_t(_unpack_rows(b, late_t_shapes))) for a, b in zip(outs_in, outs_late)]
    bg, bd_, bm, bv = big

    partial = jnp.concatenate(
        [dgmix.reshape(8, LANES), dgffn.reshape(8, LANES), _pad8(dqg.reshape(4, LANES)), _pad8(dkg.reshape(4, LANES)),
         _pad8(dga.reshape(4, LANES)), _pad8(dgdn), sm, loss_row.reshape(8, LANES),
         dcw[0:CONV_K].reshape(48, LANES), drel.reshape(24, LANES)], axis=0)
    S = _small_reduce(_all_gather(partial, "gather_small"))
    loss = S[R_LOSS, 0]
    g_conv = lax.dynamic_slice(S[R_CONV:R_CONV + 48].reshape(CONV_K, 1536), (0, 192 * my_idx), (CONV_K, 192))
    sg = [S[R_GMIX:R_GMIX + 8].reshape(1, D), S[R_QG:R_QG + 1, 0:DHA], S[R_KG:R_KG + 1, 0:DHA],
          S[R_REL:R_REL + 24].reshape(NHA, 384)[:, 0:257].T, S[R_GA:R_GA + 4].reshape(1, AW), g_conv,
          S[R_ALOG:R_ALOG + 1, NHD:2 * NHD], S[R_DT:R_DT + 1, NHD:2 * NHD], S[R_GDN:R_GDN + 1], S[R_GFFN:R_GFFN + 8].reshape(1, D)]
    sw = [norm_mix_g, attn_q_norm_g, attn_k_norm_g, rel_bias[0], attn_out_norm_g, conv_w[0], a_log, dt_bias, dn_out_norm_g, norm_ffn_g]
    smm = [m_norm_mix_g, m_attn_q_norm_g, m_attn_k_norm_g, m_rel_bias[0], m_attn_out_norm_g, m_conv_w[0], m_a_log, m_dt_bias, m_dn_out_norm_g, m_norm_ffn_g]
    svv = [v_norm_mix_g, v_attn_q_norm_g, v_attn_k_norm_g, v_rel_bias[0], v_attn_out_norm_g, v_conv_w[0], v_a_log, v_dt_bias, v_dn_out_norm_g, v_norm_ffn_g]
    s_shapes = [t.shape for t in sw]
    pg = _pack_lanes(sg)
    s_out = _adamw(pg[None], _pack_lanes(sw), _pack_lanes(smm), _pack_lanes(svv), "adamw_small", pg.shape[0])
    s_g, s_d, s_m, s_v = (_unpack_lanes(t, s_shapes) for t in s_out)

    lead = lambda t: t[None]
    def ordered(small, big):
        nm, q, k, rel, ao, cw, al, dtb, dno, nf = small
        wi, wo, wgt, wu, wdn = big
        return [nm, lead(wi), q, k, lead(rel), ao, lead(cw), al, dtb, dno, lead(wo), nf, lead(wgt), lead(wu), lead(wdn)]
    outs = [loss, gx[None]]
    for small, big in ((s_g, bg), (s_d, bd_), (s_m, bm), (s_v, bv)):
        outs += ordered(small, big)
    return tuple(outs)
```

```python
import math

import jax
import jax.numpy as jnp
from jax import lax
from jax.experimental import pallas as pl
from jax.experimental.pallas import tpu as pltpu

F32 = jnp.float32
BF16 = jnp.bfloat16
_MXU = jnp.bfloat16

D = 1024
AW = 512
NHA = 8
DHA = 64
CH = 64
NHD = 4
DHD = 128
DW = 512
FF = 2816
EPS = 1e-6
NEG = -1e30
N_DEV = 8
LANES = 128
VMEM_LIMIT = 56 * 1024 * 1024

ADAM_LR = 0.001
ADAM_B1 = 0.9
ADAM_B2 = 0.999
ADAM_EPS = 1e-08
ADAM_WD = 0.01
ADAM_STEP = 10

MESH_T = pl.DeviceIdType.MESH


def _cp(sem=None, vmem=VMEM_LIMIT):
    kw = dict(vmem_limit_bytes=vmem)
    if sem is not None:
        kw["dimension_semantics"] = sem
    return pltpu.CompilerParams(**kw)


def _dot(a, b):
    return jnp.dot(a.astype(_MXU), b.astype(_MXU), preferred_element_type=F32)


def _dot_nt(a, b):
    return lax.dot_general(a.astype(_MXU), b.astype(_MXU), (((1,), (1,)), ((), ())), preferred_element_type=F32)


def _dot_tn(a, b):
    return lax.dot_general(a.astype(_MXU), b.astype(_MXU), (((0,), (0,)), ((), ())), preferred_element_type=F32)


def _iota(shape, dim):
    return lax.broadcasted_iota(jnp.int32, shape, dim)


def _block_ones(n, blk, dtype=BF16):
    r, c = _iota((n, n), 0), _iota((n, n), 1)
    return jnp.where((r // blk) == (c // blk), 1.0, 0.0).astype(dtype)


def _sigmoid(x):
    return 1.0 / (1.0 + jnp.exp(-x))


def _softplus(x):
    return jnp.maximum(x, 0.0) + jnp.log(1.0 + jnp.exp(-jnp.abs(x)))


def _col(x, k):
    lane = _iota(x.shape, 1)
    return jnp.sum(jnp.where(lane == k, x, 0.0), axis=1, keepdims=True)


def _row(x, k):
    sub = _iota(x.shape, 0)
    return jnp.sum(jnp.where(sub == k, x, 0.0), axis=0, keepdims=True)


def _my_pos():
    return lax.axis_index("x"), lax.axis_index("y"), lax.axis_index("c")


_COMM_SEMS = [pltpu.SemaphoreType.DMA((7,)), pltpu.SemaphoreType.DMA((7,)), pltpu.SemaphoreType.DMA]


class _Gather:
    def __init__(self, x_ref, out_ref, send_sems, recv_sems, local_sem):
        x, y, c = _my_pos()
        me, sibling = (x, y, c), (x, y, 1 - c)
        chips = [(1 - x, y), (x, 1 - y), (1 - x, 1 - y)]

        def slot(px, py, pc):
            return out_ref.at[4 * px + 2 * py + pc]

        def copy(k, block, to, src=None):
            return pltpu.make_async_remote_copy(
                src_ref=slot(*block) if src is None else src, dst_ref=slot(*block),
                send_sem=send_sems.at[k], recv_sem=recv_sems.at[k], device_id=to, device_id_type=MESH_T)

        self.mine = pltpu.make_async_copy(x_ref, slot(*me), local_sem)
        self.first = [copy(0, me, sibling, src=x_ref)]
        self.first += [copy(1 + j, me, (*chip, c), src=x_ref) for j, chip in enumerate(chips)]
        self.passed = [copy(4 + j, (*chip, c), sibling) for j, chip in enumerate(chips)]
        self.from_chips = [copy(1 + j, (*chip, c), me) for j, chip in enumerate(chips)]
        self.from_sibling = [copy(0, sibling, me)] + [copy(4 + j, (*chip, 1 - c), me) for j, chip in enumerate(chips)]

    def start(self):
        self.mine.start()
        for cp in self.first:
            cp.start()

    def forward(self):
        for arrived, onward in zip(self.from_chips, self.passed):
            arrived.wait_recv()
            onward.start()

    def finish(self):
        for cp in self.from_sibling:
            cp.wait_recv()
        for cp in self.first + self.passed:
            cp.wait_send()
        self.mine.wait()


class _Scatter:
    def __init__(self, s_ref, r_ref, send_sems, recv_sems, local_sem):
        x, y, c = _my_pos()
        self.mine = pltpu.make_async_copy(s_ref.at[4 * x + 2 * y + c], r_ref.at[0], local_sem)
        self.copies = []
        for m in range(1, N_DEV):
            px = x ^ ((m >> 2) & 1)
            py = y ^ ((m >> 1) & 1)
            pc = c ^ (m & 1)
            self.copies.append(pltpu.make_async_remote_copy(
                src_ref=s_ref.at[4 * px + 2 * py + pc], dst_ref=r_ref.at[m],
                send_sem=send_sems.at[m - 1], recv_sem=recv_sems.at[m - 1],
                device_id=(px, py, pc), device_id_type=MESH_T))

    def start(self):
        self.mine.start()
        for cp in self.copies:
            cp.start()

    def finish(self):
        for cp in self.copies:
            cp.wait_recv()
        for cp in self.copies:
            cp.wait_send()
        self.mine.wait()


TM = 512
TF = 256
TG = 512


def _full(shape):
    nd = len(shape)
    return pl.BlockSpec(shape, lambda i: (0,) * nd)


def _rows(tm, w):
    return pl.BlockSpec((tm, w), lambda i: (i, 0))


def _head_sum(x, bd):
    one_pass = lambda t: jnp.dot(t.astype(_MXU), bd.astype(_MXU), preferred_element_type=F32)
    return jnp.concatenate([one_pass(x[:, 0:256]), one_pass(x[:, 256:512])], axis=1)


def _head_rms(x, bd, width):
    return lax.rsqrt(_head_sum(x * x, bd) * (1.0 / width) + EPS)


def _inproj(x, g_mix, w_int, w_ba, qg_t, kg_t, later_w):
    T = x.shape[0]
    nt = T // TM
    ntd = (((1,), (1,)), ((), ()))

    def body(x_ref, g_ref, w_ref, wba_ref, qg_ref, kg_ref, lw_ref, araw_ref, an_ref, draw_ref, z_ref, ba_ref, h_ref,
             lw_all, send_sems, recv_sems, local_sem):
        i = pl.program_id(0)
        ag = _Gather(lw_ref, lw_all, send_sems, recv_sems, local_sem)
        pl.when(i == 0)(ag.start)
        pl.when(i == nt // 2)(ag.forward)
        xv = x_ref[...]
        r = lax.rsqrt(jnp.mean(xv * xv, axis=1, keepdims=True) + EPS)
        h = (xv * r * g_ref[...]).astype(_MXU)
        h_ref[...] = h
        proj = lambda lo, hi: lax.dot_general(h, w_ref[lo:hi, :], ntd, preferred_element_type=F32)
        q, k, v = proj(0, AW), proj(AW, 2 * AW), proj(2 * AW, 3 * AW)
        draw_ref[...] = proj(1536, 3072)
        z_ref[...] = proj(3072, 3584)
        ba_ref[...] = jnp.dot(h, wba_ref[...], preferred_element_type=F32)
        araw_ref[:, 0:AW] = q
        araw_ref[:, AW:2 * AW] = k
        araw_ref[:, 2 * AW:3 * AW] = v
        bd = _block_ones(AW // 2, DHA)
        qn = q * _head_rms(q, bd, DHA) * (qg_ref[...] * (DHA ** -0.5))
        kn = k * _head_rms(k, bd, DHA) * kg_ref[...]
        an_ref[:, 0:AW] = qn.astype(_MXU)
        an_ref[:, AW:2 * AW] = kn.astype(_MXU)
        an_ref[:, 2 * AW:3 * AW] = v.astype(_MXU)
        pl.when(i == nt - 1)(ag.finish)

    anyspec = pl.BlockSpec(memory_space=pl.ANY)
    return pl.pallas_call(
        body, name="inproj", grid=(nt,),
        in_specs=[_rows(TM, D), _full((1, D)), _full((3584, D)), _full((D, 8)), _full((1, AW)), _full((1, AW)),
                  anyspec],
        out_specs=[_rows(TM, 1536), _rows(TM, 1536), _rows(TM, 1536), _rows(TM, DW), _rows(TM, 8), _rows(TM, D),
                   anyspec],
        out_shape=[jax.ShapeDtypeStruct((T, 1536), F32), jax.ShapeDtypeStruct((T, 1536), _MXU),
                   jax.ShapeDtypeStruct((T, 1536), F32), jax.ShapeDtypeStruct((T, DW), F32),
                   jax.ShapeDtypeStruct((T, 8), F32), jax.ShapeDtypeStruct((T, D), _MXU),
                   jax.ShapeDtypeStruct((N_DEV,) + later_w.shape, later_w.dtype)],
        scratch_shapes=_COMM_SEMS,
        compiler_params=_cp(("arbitrary",)),
    )(x, g_mix, w_int, w_ba, qg_t, kg_t, later_w)


TQ = 256
TW = 768
VAR0 = 384
TOEP = 1024


def _bias_tables(rb_t, w_shard, c_shard):
    def body(rb_ref, w_ref, c_ref, tab_ref, tabt_ref, w_all, c_all, ws, wr, wl, cs, cr, cl):
        h = pl.program_id(0)
        gathers = [_Gather(w_ref, w_all, ws, wr, wl), _Gather(c_ref, c_all, cs, cr, cl)]

        @pl.when(h == 0)
        def _():
            for ag in gathers:
                ag.start()

        @pl.when(h == NHA // 2)
        def _():
            for ag in gathers:
                ag.forward()

        rb8 = jnp.broadcast_to(_row(rb_ref[...], h), (8, VAR0))
        n = _iota((VAR0, TOEP), 1)
        t = _iota((VAR0, TOEP), 0)

        def line(m):
            onehot = jnp.where(jnp.clip(512 - m, -128, 128) + 128 == t, 1.0, 0.0).astype(BF16)
            return sum(jnp.dot(p, onehot, preferred_element_type=F32) for p in _split3(rb8))[0:1, :]

        def band(r, j, first_key):
            return ((j >> 6) >= (r >> 6)) & ((j >> 6) <= (r >> 6) + 8) & (j >= first_key)

        g = line(jnp.where(n < TW, n, n - TOEP))
        tab = pltpu.roll(jnp.broadcast_to(g, (TQ, TOEP)), 0, 1, stride=1, stride_axis=0)[:, 0:TW]
        gt = line(jnp.where(n < TQ, -n, TOEP - n))
        tabt = pltpu.roll(jnp.broadcast_to(gt, (TW, TOEP)), 0, 1, stride=1, stride_axis=0)[:, 0:TQ]
        for v in range(3):
            first_key = max(512 - TQ * v, 0)
            tab_ref[v, 0] = jnp.where(band(_iota((TQ, TW), 0), _iota((TQ, TW), 1), first_key), tab, NEG)
            tabt_ref[v, 0] = jnp.where(band(_iota((TW, TQ), 1), _iota((TW, TQ), 0), first_key), tabt, NEG)

        @pl.when(h == NHA - 1)
        def _():
            for ag in gathers:
                ag.finish()

    anyspec = pl.BlockSpec(memory_space=pl.ANY)
    return pl.pallas_call(
        body, name="bias_tables", grid=(NHA,),
        in_specs=[_full((NHA, VAR0)), anyspec, anyspec],
        out_specs=[pl.BlockSpec((3, 1, TQ, TW), lambda h: (0, h, 0, 0)),
                   pl.BlockSpec((3, 1, TW, TQ), lambda h: (0, h, 0, 0)), anyspec, anyspec],
        out_shape=[jax.ShapeDtypeStruct((3, NHA, TQ, TW), F32), jax.ShapeDtypeStruct((3, NHA, TW, TQ), F32),
                   jax.ShapeDtypeStruct((N_DEV,) + w_shard.shape, w_shard.dtype),
                   jax.ShapeDtypeStruct((N_DEV,) + c_shard.shape, c_shard.dtype)],
        scratch_shapes=_COMM_SEMS + _COMM_SEMS,
        compiler_params=_cp(("arbitrary",)),
    )(rb_t, w_shard, c_shard)


def _bias_grad(dtabt):
    def body(d_ref, o_ref):
        a, b = _iota((TQ, TQ), 0), _iota((TQ, TQ), 1)
        anti = jnp.where(a + b == TQ - 1, 1.0, 0.0).astype(BF16)
        drev = sum(jnp.dot(t, anti, preferred_element_type=F32) for t in _split3(d_ref[0]))
        wide = jnp.concatenate([drev, jnp.zeros((TW, TOEP - TQ), F32)], axis=1)
        cols = jnp.sum(pltpu.roll(wide, 0, 1, stride=1, stride_axis=0), axis=0, keepdims=True)
        c = _iota((TOEP, VAR0), 0)
        idx = jnp.clip(512 + TQ - 1 - c, -128, 128) + 128
        onehot = jnp.where(idx == _iota((TOEP, VAR0), 1), 1.0, 0.0).astype(BF16)
        cols8 = jnp.broadcast_to(cols, (8, TOEP))
        o_ref[0] = sum(jnp.dot(t, onehot, preferred_element_type=F32) for t in _split3(cols8))[0:1, :]

    return pl.pallas_call(
        body, name="bias_grad", grid=(NHA,),
        in_specs=[pl.BlockSpec((1, TW, TQ), lambda h: (h, 0, 0))],
        out_specs=pl.BlockSpec((1, 1, VAR0), lambda h: (h, 0, 0)),
        out_shape=jax.ShapeDtypeStruct((NHA, 1, VAR0), F32),
        compiler_params=_cp(("arbitrary",)),
    )(dtabt)


def _kv_spec(col, back):
    return pl.BlockSpec((TQ, AW), lambda i: (jnp.maximum(i - back, 0), col))


def _attn_fwd(an, tab):
    T = an.shape[0]

    def body(q_ref, k2_ref, k1_ref, k0_ref, v2_ref, v1_ref, v0_ref, tab_ref, o_ref):
        i = pl.program_id(0)
        kwin = jnp.concatenate([k2_ref[...], k1_ref[...], k0_ref[...]], axis=0)
        vwin = jnp.concatenate([v2_ref[...], v1_ref[...], v0_ref[...]], axis=0)
        q = q_ref[...]
        lo_half = _iota((TQ, LANES), 1) < DHA

        def scores(h):
            sl = slice(LANES * (h // 2), LANES * (h // 2 + 1))
            mask = lo_half if h % 2 == 0 else jnp.logical_not(lo_half)
            qm = jnp.where(mask, q[:, sl], jnp.zeros((TQ, LANES), q.dtype))
            return _dot_nt(qm, kwin[:, sl]) + tab_ref[0, h]

        s_next = scores(0)
        outs = []
        for h in range(NHA):
            s = s_next
            if h + 1 < NHA:
                s_next = scores(h + 1)
            sl = slice(LANES * (h // 2), LANES * (h // 2 + 1))
            m = jnp.max(s, axis=1, keepdims=True)
            e = jnp.exp(s - m)
            l = jnp.sum(e, axis=1, keepdims=True)
            outs.append(_dot(e, vwin[:, sl]) / l)
            if h % 2 == 1:
                o_ref[:, sl] = jnp.where(lo_half, outs[h - 1], outs[h])

    return pl.pallas_call(
        body, name="attn_fwd", grid=(T // TQ,),
        in_specs=[pl.BlockSpec((TQ, AW), lambda i: (i, 0)),
                  _kv_spec(1, 2), _kv_spec(1, 1), _kv_spec(1, 0), _kv_spec(2, 2), _kv_spec(2, 1), _kv_spec(2, 0),
                  pl.BlockSpec((1, NHA, TQ, TW), lambda i: (jnp.minimum(i, 2), 0, 0, 0))],
        out_specs=_rows(TQ, AW),
        out_shape=jax.ShapeDtypeStruct((T, AW), F32),
        compiler_params=_cp(("arbitrary",)),
    )(an, an, an, an, an, an, an, tab)


def _attn_bwd(an, dout, tabt, send):
    T = an.shape[0]
    nq = T // TQ

    def qi(i):
        return jnp.minimum(i, nq - 1)

    def kv_spec(col, back):
        return pl.BlockSpec((TQ, AW), lambda i: (jnp.maximum(qi(i) - back, 0), col))

    def body(q_ref, do_ref, k2_ref, k1_ref, k0_ref, v2_ref, v1_ref, v0_ref, tabt_ref, send_ref,
             dq_ref, dk_ref, dv_ref, dtab_ref, recv_ref, dk_acc, dv_acc, send_sems, recv_sems, local_sem):
        i = pl.program_id(0)
        sc = _Scatter(send_ref, recv_ref, send_sems, recv_sems, local_sem)
        pl.when(i == 0)(sc.start)

        @pl.when(i == 0)
        def _():
            dtab_ref[...] = jnp.zeros_like(dtab_ref)

        new = i % 3
        dk_acc[new] = jnp.zeros((TQ, AW), F32)
        dv_acc[new] = jnp.zeros((TQ, AW), F32)

        @pl.when(i < nq)
        def _():
            kwin = jnp.concatenate([k2_ref[...], k1_ref[...], k0_ref[...]], axis=0)
            vwin = jnp.concatenate([v2_ref[...], v1_ref[...], v0_ref[...]], axis=0)
            q = q_ref[...]
            do = do_ref[...].astype(_MXU)
            lo_half = _iota((TQ, LANES), 1) < DHA

            def front(h):
                sl = slice(LANES * (h // 2), LANES * (h // 2 + 1))
                mask = lo_half if h % 2 == 0 else jnp.logical_not(lo_half)
                zero = jnp.zeros((TQ, LANES), q.dtype)
                qm = jnp.where(mask, q[:, sl], zero)
                dom = jnp.where(mask, do[:, sl], zero)
                st = _dot_nt(kwin[:, sl], qm) + tabt_ref[0, h]
                return st, _dot_nt(vwin[:, sl], dom), qm, dom, mask

            pairs = {}

            def back(h, ptb, dsb, qm, dom, mask):
                sl = slice(LANES * (h // 2), LANES * (h // 2 + 1))
                dv = _dot(ptb, dom)
                dk = _dot(dsb, qm)
                dq = jnp.where(mask, _dot_tn(dsb, kwin[:, sl]), 0.0)
                if h % 2 == 0:
                    pairs[h // 2] = (dq, dk, dv)
                    return
                dq0, dk0, dv0 = pairs.pop(h // 2)
                dq_ref[:, sl] = dq0 + dq
                dk_pair, dv_pair = dk0 + dk, dv0 + dv
                for w in range(3):
                    slot = (i + 1 + w) % 3
                    rows = slice(TQ * w, TQ * (w + 1))
                    dk_acc[slot, :, sl] += dk_pair[rows]
                    dv_acc[slot, :, sl] += dv_pair[rows]

            nxt = front(0)
            pending = None
            for h in range(NHA):
                st, dpt, qm, dom, mask = nxt
                if h + 1 < NHA:
                    nxt = front(h + 1)
                m = jnp.max(st, axis=0, keepdims=True)
                e = jnp.exp(st - m)
                pt = e * (1.0 / jnp.sum(e, axis=0, keepdims=True))
                delta = jnp.sum(pt * dpt, axis=0, keepdims=True)
                dst = pt * (dpt - delta)
                dtab_ref[h] += dst
                if pending is not None:
                    back(*pending)
                pending = (h, pt.astype(_MXU), dst.astype(_MXU), qm, dom, mask)
            back(*pending)

        @pl.when(i >= 2)
        def _():
            done = (i + 1) % 3
            dk_ref[...] = dk_acc[done]
            dv_ref[...] = dv_acc[done]

        pl.when(i == nq + 1)(sc.finish)

    back2 = pl.BlockSpec((TQ, AW), lambda i: (jnp.maximum(i - 2, 0), 0))
    anyspec = pl.BlockSpec(memory_space=pl.ANY)
    return pl.pallas_call(
        body, name="attn_bwd", grid=(nq + 2,),
        in_specs=[pl.BlockSpec((TQ, AW), lambda i: (qi(i), 0)), pl.BlockSpec((TQ, AW), lambda i: (qi(i), 0)),
                  kv_spec(1, 2), kv_spec(1, 1), kv_spec(1, 0), kv_spec(2, 2), kv_spec(2, 1), kv_spec(2, 0),
                  pl.BlockSpec((1, NHA, TW, TQ), lambda i: (jnp.minimum(i, 2), 0, 0, 0)), anyspec],
        out_specs=[pl.BlockSpec((TQ, AW), lambda i: (qi(i), 0)), back2, back2, _full((NHA, TW, TQ)), anyspec],
        out_shape=[jax.ShapeDtypeStruct((T, AW), F32), jax.ShapeDtypeStruct((T, AW), F32),
                   jax.ShapeDtypeStruct((T, AW), F32), jax.ShapeDtypeStruct((NHA, TW, TQ), F32),
                   jax.ShapeDtypeStruct(send.shape, send.dtype)],
        scratch_shapes=[pltpu.VMEM((3, TQ, AW), F32), pltpu.VMEM((3, TQ, AW), F32)] + _COMM_SEMS,
        compiler_params=_cp(("arbitrary",)),
    )(an, dout, an, an, an, an, an, an, tabt, send)


GR = 128
NG = TG // GR
CPT = TG // CH
CONV_K = 4


def _split3(x):
    a = x.astype(BF16)
    r = x - a.astype(F32)
    b = r.astype(BF16)
    c = (r - b.astype(F32)).astype(BF16)
    return a, b, c


def _ones_dot(ones_b, x):
    return sum(jnp.dot(ones_b, t, preferred_element_type=F32) for t in _split3(x))


def _dot_ones_nt(x, ones_b):
    dn = (((1,), (1,)), ((), ()))
    return sum(lax.dot_general(t, ones_b, dn, preferred_element_type=F32) for t in _split3(x))


def _dn_masks():
    r, c = _iota((GR, GR), 0), _iota((GR, GR), 1)
    same = (r >> 6) == (c >> 6)
    one = lambda m: jnp.where(m, 1.0, 0.0).astype(BF16)
    return dict(
        tril=same & (c <= r), strict=same & (c < r), triu=same & (c >= r), strict_u=same & (c > r),
        tril_b=one(same & (c <= r)), triu_b=one(same & (c >= r)), blk_b=one(same), eye_b=one(r == c),
        eye=jnp.where(r == c, 1.0, 0.0).astype(F32),
        fold_b=one((_iota((GR, CH), 0) & (CH - 1)) == _iota((GR, CH), 1)),
        last=(_iota((GR, 1), 0) & (CH - 1)) == CH - 1,
    )


def _shift_down(x, halo, k):
    if k == 0:
        return x
    xs = pltpu.roll(x, k, 0)
    hs = pltpu.roll(halo, k, 0)
    top = jnp.where(_iota(halo.shape, 0) < k, hs, xs[0:8])
    return jnp.concatenate([top, xs[8:]], axis=0)


def _shift_up(x, halo, k):
    if k == 0:
        return x
    n = x.shape[0]
    xs = pltpu.roll(x, n - k, 0)
    hs = pltpu.roll(halo, 8 - k, 0)
    bot = jnp.where(_iota(halo.shape, 0) >= 8 - k, hs, xs[n - 8:n])
    return jnp.concatenate([xs[0:n - 8], bot], axis=0)


def _conv(x, halo, w):
    y = x * w[CONV_K - 1:CONV_K, :]
    for k in range(1, CONV_K):
        y = y + _shift_down(x, halo, k) * w[CONV_K - 1 - k:CONV_K - k, :]
    return y


def _tri_inv(lmats, eye):
    ps = [-m for m in lmats]
    rs = [eye + p for p in ps]
    for _ in range(5):
        ps = [_dot(p, p) for p in ps]
        rs = [r + _dot(r, p) for r, p in zip(rs, ps)]
    return rs


def _gate_terms(ba_g, bat_g, alog8, dtb8, alog8t, dtb8t, K):
    g8 = -jnp.exp(alog8) * _softplus(ba_g + dtb8)
    g8t = -jnp.exp(alog8t) * _softplus(bat_g + dtb8t)
    gc8 = _ones_dot(K["tril_b"], g8)
    gl8 = _ones_dot(K["blk_b"], g8)
    gcrow8 = _dot_ones_nt(g8t, K["tril_b"])
    return g8, gc8, gl8, gcrow8


def _dn_heads(c_tile, rows, beta8, gc8, gl8, gcrow8, K, pre=None):
    return _dn_heads_groups(c_tile, [(rows, beta8, gc8, gl8, gcrow8)], K, None if pre is None else [pre])[0]


def _dn_heads_groups(c_tile, groups, K, pres=None):
    ds = [_dn_head_vec(c_tile, rows, h, beta8, gc8, gl8, gcrow8, K)
          for rows, beta8, gc8, gl8, gcrow8 in groups for h in range(NHD)]
    pls = [_dot_nt(d["kb"], d["kn"]) for d in ds]
    pms = [_dot_nt(d["qn"], d["kn"]) for d in ds]
    for d, pl_, pm in zip(ds, pls, pms):
        d.update(pl=pl_, pm=pm, lmat=jnp.where(K["strict"], pl_ * d["gam_m"], 0.0), mm=pm * d["gam_m"])
    if pres is None:
        for d, tm in zip(ds, _tri_inv([d["lmat"] for d in ds], K["eye"])):
            d.update(tm=tm, u=_dot(tm, d["vb"]), w=_dot(tm, d["kg"]))
    else:
        for d, (tm, u, w) in zip(ds, [p for pre in pres for p in pre]):
            d.update(tm=tm, u=u, w=w)
    return [ds[NHD * k:NHD * (k + 1)] for k in range(len(groups))]


def _dn_head_vec(c_tile, rows, h, beta8, gc8, gl8, gcrow8, K):
    qr = c_tile[rows, DHD * h:DHD * (h + 1)]
    kr = c_tile[rows, DW + DHD * h:DW + DHD * (h + 1)]
    v = c_tile[rows, 2 * DW + DHD * h:2 * DW + DHD * (h + 1)]
    rq = lax.rsqrt(jnp.sum(qr * qr, axis=1, keepdims=True) + EPS)
    rk = lax.rsqrt(jnp.sum(kr * kr, axis=1, keepdims=True) + EPS)
    qh, kn = qr * rq, kr * rk
    qn = qh * (DHD ** -0.5)
    beta = _col(beta8, h)
    gccol, glcol, gcrow = _col(gc8, NHD + h), _col(gl8, NHD + h), _row(gcrow8, NHD + h)
    diff = gccol - gcrow
    gam_m = jnp.exp(jnp.where(K["tril"], diff, NEG))
    gam = jnp.exp(gccol)
    egl = jnp.exp(glcol - gccol)
    kb, vb = kn * beta, v * beta
    kg = kb * gam
    return dict(qr=qr, kr=kr, v=v, rq=rq, rk=rk, qh=qh, qn=qn, kn=kn, beta=beta, diff=diff, gam_m=gam_m, gam=gam,
                egl=egl, el=jnp.exp(glcol), kb=kb, vb=vb, kg=kg, qd=qn * gam, kd=kn * egl)


def _halo_prev(width):
    return pl.BlockSpec((8, width), lambda i: (jnp.maximum(i * (TG // 8) - 1, 0), 0))


def _dn_prep(draw, conv_w, ba, bat, alog8, dtb8, alog8t, dtb8t):
    T = draw.shape[0]
    nb = T // TG
    hm = lambda w, dt: jax.ShapeDtypeStruct((NHD, T, w), dt)
    hm_spec = lambda w: pl.BlockSpec((NHD, TG, w), lambda i: (0, i, 0))
    pc = lambda r, c: jax.ShapeDtypeStruct((NHD, T // CH, r, c), _MXU)
    pc_spec = lambda r, c: pl.BlockSpec((NHD, CPT, r, c), lambda i: (0, i, 0, 0))

    def body(x_ref, halo_ref, cw_ref, ba_ref, bat_ref, al_ref, dt_ref, alt_ref, dtt_ref,
             u_ref, w_ref, kd_ref, tm_ref, wq_ref, km_ref, mq_ref, wt_ref, elb_ref, cv_ref):
        i = pl.program_id(0)
        K = _dn_masks()
        halo = jnp.where(i > 0, halo_ref[...], 0.0)
        cv = _conv(x_ref[...], halo, cw_ref[...])
        cv_ref[...] = cv
        c_tile = cv * _sigmoid(cv)
        eye128 = jnp.where(_iota((DHD, DHD), 0) == _iota((DHD, DHD), 1), 1.0, 0.0).astype(_MXU)
        def gate_inputs(g):
            rows = slice(GR * g, GR * (g + 1))
            ba_g = ba_ref[rows, :]
            _, gc8, gl8, gcrow8 = _gate_terms(ba_g, bat_ref[:, rows], al_ref[...], dt_ref[...], alt_ref[...],
                                              dtt_ref[...], K)
            return rows, _sigmoid(ba_g), gc8, gl8, gcrow8

        def store(g, rows, ds):
            mmts = [_dot_nt(d["kn"], d["qn"]) * jnp.exp(jnp.where(K["triu"], -d["diff"], NEG)) for d in ds]
            mcs = [_dot(d["mm"], K["fold_b"]) for d in ds]
            mcts = [_dot(m, K["fold_b"]) for m in mmts]
            for h, d in enumerate(ds):
                tm_ref[h, rows, :] = d["tm"].astype(_MXU)
                u_ref[h, rows, :] = d["u"]
                w_ref[h, rows, :] = d["w"].astype(_MXU)
                kd_ref[h, rows, :] = d["kd"].astype(_MXU)
                elb = jnp.broadcast_to(d["el"], (GR, DHD))
                for cc in range(GR // CH):
                    ch = slice(CH * cc, CH * (cc + 1))
                    n = (GR // CH) * g + cc
                    wq_ref[h, n, 0:CH, :] = d["w"][ch].astype(_MXU)
                    wq_ref[h, n, CH:2 * CH, :] = d["qd"][ch].astype(_MXU)
                    km_ref[h, n, 0:DHD, :] = _dot_nt(eye128, d["kd"][ch]).astype(_MXU)
                    km_ref[h, n, DHD:DHD + CH, :] = mcs[h][ch].astype(_MXU)
                    mq_ref[h, n, 0:CH, :] = mcts[h][ch].astype(_MXU)
                    mq_ref[h, n, CH:CH + DHD, :] = _dot_nt(eye128, d["qd"][ch]).astype(_MXU)
                    wt_ref[h, n] = _dot_nt(eye128, d["w"][ch]).astype(_MXU)
                    elb_ref[n:n + 1, DHD * h:DHD * (h + 1)] = elb[CH * cc:CH * cc + 1, :]

        PAIR = 4
        for g0 in range(0, NG, PAIR):
            pair = [gate_inputs(g) for g in range(g0, g0 + PAIR)]
            for k, ds in enumerate(_dn_heads_groups(c_tile, pair, K)):
                store(g0 + k, pair[k][0], ds)

    return pl.pallas_call(
        body, name="dn_prep", grid=(nb,),
        in_specs=[_rows(TG, 1536), _halo_prev(1536), _full((CONV_K, 1536)), _rows(TG, 8),
                  pl.BlockSpec((8, TG), lambda i: (0, i)), _full((1, 8)), _full((1, 8)), _full((8, 1)), _full((8, 1))],
        out_specs=[hm_spec(DHD), hm_spec(DHD), hm_spec(DHD), hm_spec(GR),
                   pc_spec(2 * CH, DHD), pc_spec(DHD + CH, CH), pc_spec(CH + DHD, CH), pc_spec(DHD, CH),
                   pl.BlockSpec((CPT, NHD * DHD), lambda i: (i, 0)), _rows(TG, 1536)],
        out_shape=[hm(DHD, F32), hm(DHD, _MXU), hm(DHD, _MXU), hm(GR, _MXU),
                   pc(2 * CH, DHD), pc(DHD + CH, CH), pc(CH + DHD, CH), pc(DHD, CH),
                   jax.ShapeDtypeStruct((T // CH, NHD * DHD), F32), jax.ShapeDtypeStruct((T, 1536), F32)],
        compiler_params=_cp(("arbitrary",)),
    )(draw, draw, conv_w, ba, bat, alog8, dtb8, alog8t, dtb8t)


def _dn_scan(u, wq, km, elb):
    T = u.shape[1]
    nb = T // TG
    hm_spec = lambda wd: pl.BlockSpec((NHD, TG, wd), lambda i: (0, i, 0))

    def body(u_ref, wq_ref, km_ref, elb_ref, o_ref, vn_ref, sn_ref, S):
        @pl.when(pl.program_id(0) == 0)
        def _():
            S[...] = jnp.zeros_like(S)

        sub8 = _iota((CPT, DHD), 0)
        heads = range(NHD)

        def chunk(cc, carry):
            rs = pl.ds(pl.multiple_of(cc * CH, CH), CH)
            sh = [S[h] for h in heads]
            sb = [s.astype(_MXU) for s in sh]
            r1 = [_dot(wq_ref[h, cc], sb[h]) for h in heads]
            vnb = [(u_ref[h, rs, :] - r1[h][0:CH]).astype(_MXU) for h in heads]
            r2 = [_dot(km_ref[h, cc], vnb[h]) for h in heads]
            for h in heads:
                el = jnp.sum(jnp.where(sub8 == cc, elb_ref[:, DHD * h:DHD * (h + 1)], 0.0), axis=0, keepdims=True)
                S[h] = sh[h] * el + r2[h][0:DHD]
                sn_ref[cc, h] = sb[h]
                vn_ref[h, rs, :] = vnb[h]
                o_ref[h, rs, :] = r1[h][CH:2 * CH] + r2[h][DHD:DHD + CH]
            return carry

        lax.fori_loop(0, CPT, chunk, 0)

    return pl.pallas_call(
        body, name="dn_scan", grid=(nb,),
        in_specs=[hm_spec(DHD), pl.BlockSpec((NHD, CPT, 2 * CH, DHD), lambda i: (0, i, 0, 0)),
                  pl.BlockSpec((NHD, CPT, DHD + CH, CH), lambda i: (0, i, 0, 0)),
                  pl.BlockSpec((CPT, NHD * DHD), lambda i: (i, 0))],
        out_specs=[hm_spec(DHD), hm_spec(DHD), pl.BlockSpec((CPT, NHD, DHD, DHD), lambda i: (i, 0, 0, 0))],
        out_shape=[jax.ShapeDtypeStruct((NHD, T, DHD), F32), jax.ShapeDtypeStruct((NHD, T, DHD), _MXU),
                   jax.ShapeDtypeStruct((T // CH, NHD, DHD, DHD), _MXU)],
        scratch_shapes=[pltpu.VMEM((NHD, DHD, DHD), F32)],
        compiler_params=_cp(("arbitrary",)),
    )(u, wq, km, elb)


def _dn_scan_bwd(do, mq, kd, wt, sn, vn, elb):
    T = do.shape[1]
    nb = T // TG
    rev = lambda wd: pl.BlockSpec((NHD, TG, wd), lambda i: (0, nb - 1 - i, 0))
    rev_t = lambda r: pl.BlockSpec((NHD, CPT, r, CH), lambda i: (0, nb - 1 - i, 0, 0))

    def body(do_ref, mq_ref, kd_ref, wt_ref, sn_ref, vn_ref, elb_ref,
             du_ref, dw_ref, dqd_ref, dkd_ref, dgx_ref, dS):
        @pl.when(pl.program_id(0) == 0)
        def _():
            dS[...] = jnp.zeros_like(dS)

        last_row = _iota((CH, DHD), 0) == CH - 1
        sub8 = _iota((CPT, DHD), 0)
        heads = range(NHD)

        def chunk(k, carry):
            cc = CPT - 1 - k
            rs = pl.ds(pl.multiple_of(cc * CH, CH), CH)
            dsh = [dS[h] for h in heads]
            dsb = [d.astype(_MXU) for d in dsh]
            doc = [do_ref[h, rs, :].astype(_MXU) for h in heads]
            a = [_dot(mq_ref[h, cc], doc[h]) for h in heads]
            b = [_dot(kd_ref[h, rs, :], dsb[h]) for h in heads]
            dvn = [a[h][0:CH] + b[h] for h in heads]
            dvnb = [d.astype(_MXU) for d in dvn]
            e = [_dot(wt_ref[h, cc], dvnb[h]) for h in heads]
            for h in heads:
                el = jnp.sum(jnp.where(sub8 == cc, elb_ref[:, DHD * h:DHD * (h + 1)], 0.0), axis=0, keepdims=True)
                sn = sn_ref[cc, h]
                dS[h] = a[h][CH:CH + DHD] + dsh[h] * el - e[h]
                du_ref[h, rs, :] = dvn[h]
                c = _dot_nt(jnp.concatenate([doc[h], dvnb[h]], axis=0), sn)
                dqd_ref[h, rs, :] = c[0:CH]
                dw_ref[h, rs, :] = -c[CH:2 * CH]
                dkd_ref[h, rs, :] = _dot_nt(vn_ref[h, rs, :], dsb[h])
                part = jnp.sum(dsh[h] * sn.astype(F32), axis=0, keepdims=True) * el
                dgx_ref[h, rs, :] = jnp.where(last_row, part, 0.0)
            return carry

        lax.fori_loop(0, CPT, chunk, 0)

    o = jax.ShapeDtypeStruct((NHD, T, DHD), F32)
    return pl.pallas_call(
        body, name="dn_scan_bwd", grid=(nb,),
        in_specs=[rev(DHD), rev_t(CH + DHD), rev(DHD), rev_t(DHD),
                  pl.BlockSpec((CPT, NHD, DHD, DHD), lambda i: (nb - 1 - i, 0, 0, 0)), rev(DHD),
                  pl.BlockSpec((CPT, NHD * DHD), lambda i: (nb - 1 - i, 0))],
        out_specs=[rev(DHD)] * 5,
        out_shape=[o] * 5,
        scratch_shapes=[pltpu.VMEM((NHD, DHD, DHD), F32)],
        compiler_params=_cp(("arbitrary",)),
    )(do, mq, kd, wt, sn, vn, elb)


def _put_col(acc, k, col):
    return jnp.where(_iota(acc.shape, 1) == k, col, acc)


def _dn_post_bwd(draw, cv, conv_w, ba, bat, alog8, dtb8, alog8t, dtb8t, du, dw, dqd, dkd, dgx, do, vn, tm, u, w):
    T = draw.shape[0]
    nb = T // TG
    hm_spec = lambda wd: pl.BlockSpec((NHD, TG, wd), lambda i: (0, nb - 1 - i, 0))
    rrows = lambda w: pl.BlockSpec((TG, w), lambda i: (nb - 1 - i, 0))

    def body(x_ref, cv_ref, cw_ref, ba_ref, bat_ref, al_ref, dt_ref, alt_ref, dtt_ref,
             du_ref, dw_ref, dqd_ref, dkd_ref, dgx_ref, do_ref, vn_ref, tm_ref, u_ref, w_ref,
             dx_ref, dba_ref, sm_ref, dcw_ref, dc_ref, nxt_ref):
        i = pl.program_id(0)

        @pl.when(i == 0)
        def _():
            sm_ref[...] = jnp.zeros_like(sm_ref)
            dcw_ref[...] = jnp.zeros_like(dcw_ref)
            nxt_ref[...] = jnp.zeros_like(nxt_ref)

        K = _dn_masks()
        cv = cv_ref[...]
        sg = _sigmoid(cv)
        c_tile = cv * sg
        dsilu = sg * (1.0 + cv * (1.0 - sg))
        for g in range(NG):
            rows = slice(GR * g, GR * (g + 1))
            ba_g = ba_ref[rows, :]
            g8, gc8, gl8, gcrow8 = _gate_terms(ba_g, bat_ref[:, rows], al_ref[...], dt_ref[...], alt_ref[...],
                                               dtt_ref[...], K)
            beta8 = _sigmoid(ba_g)
            dgc8 = jnp.zeros((GR, 8), F32)
            rd8 = jnp.zeros((GR, 8), F32)
            dbeta8 = jnp.zeros((GR, 8), F32)
            pre = [(tm_ref[h, rows, :], u_ref[h, rows, :], w_ref[h, rows, :]) for h in range(NHD)]
            ds = _dn_heads(c_tile, rows, beta8, gc8, gl8, gcrow8, K, pre)
            H = range(NHD)
            eye_b = K["eye_b"].astype(_MXU)
            gam_t = [jnp.exp(jnp.where(K["triu"], -d["diff"], NEG)) for d in ds]
            doh = [do_ref[h, rows, :] for h in H]
            vnh = [vn_ref[h, rows, :] for h in H]
            tt = [_dot_nt(eye_b, d["tm"]) for d in ds]
            dvb = [_dot(tt[h], du_ref[h, rows, :]) for h in H]
            dkg = [_dot(tt[h], dw_ref[h, rows, :]) for h in H]
            plt = [_dot_nt(d["kn"], d["kb"]) for d in ds]
            pmt = [_dot_nt(d["kn"], d["qn"]) for d in ds]
            da = [-(_dot_nt(dvb[h], ds[h]["u"]) + _dot_nt(dkg[h], ds[h]["w"])) for h in H]
            dat = [-(_dot_nt(ds[h]["u"], dvb[h]) + _dot_nt(ds[h]["w"], dkg[h])) for h in H]
            dpm = [jnp.where(K["tril"], _dot_nt(doh[h], vnh[h]), 0.0) * ds[h]["gam_m"] for h in H]
            dpmt = [jnp.where(K["triu"], _dot_nt(vnh[h], doh[h]), 0.0) * gam_t[h] for h in H]
            dpl = [jnp.where(K["strict"], da[h], 0.0) * ds[h]["gam_m"] for h in H]
            dplt = [jnp.where(K["strict_u"], dat[h], 0.0) * gam_t[h] for h in H]
            dkb = [_dot(dpl[h], ds[h]["kn"]) + dkg[h] * ds[h]["gam"] for h in H]
            dqn = [_dot(dpm[h], ds[h]["kn"]) + dqd_ref[h, rows, :] * ds[h]["gam"] for h in H]
            dknm = [_dot(dplt[h], ds[h]["kb"]) + _dot(dpmt[h], ds[h]["qn"]) for h in H]
            for h, d in enumerate(ds):
                kn, dqdh, dkdh = d["kn"], dqd_ref[h, rows, :], dkd_ref[h, rows, :]
                dkn = dknm[h] + dkdh * d["egl"] + dkb[h] * d["beta"]
                dkd_kd = dkdh * d["kd"]
                rd = jnp.sum(dkd_kd, axis=1, keepdims=True)
                dgc = jnp.sum(dpl[h] * d["pl"] + dpm[h] * d["pm"] - dplt[h] * plt[h] - dpmt[h] * pmt[h]
                              + dqdh * d["qd"] + dkg[h] * d["kg"] - dkd_kd + dgx_ref[h, rows, :],
                              axis=1, keepdims=True)
                dgc8 = _put_col(dgc8, NHD + h, dgc)
                rd8 = _put_col(rd8, NHD + h, rd)
                dbeta = jnp.sum(dkb[h] * kn + dvb[h] * d["v"], axis=1, keepdims=True)
                dbeta8 = _put_col(dbeta8, h, dbeta)
                dqh = dqn[h] * (DHD ** -0.5)
                qh = d["qh"]
                dqr = d["rq"] * (dqh - qh * jnp.sum(dqh * qh, axis=1, keepdims=True))
                dkr = d["rk"] * (dkn - kn * jnp.sum(dkn * kn, axis=1, keepdims=True))
                cq = slice(DHD * h, DHD * (h + 1))
                ck = slice(DW + DHD * h, DW + DHD * (h + 1))
                cvv = slice(2 * DW + DHD * h, 2 * DW + DHD * (h + 1))
                dc_ref[rows, cq] = dqr * dsilu[rows, cq]
                dc_ref[rows, ck] = dkr * dsilu[rows, ck]
                dc_ref[rows, cvv] = dvb[h] * d["beta"] * dsilu[rows, cvv]
            dgc8 = dgc8 + jnp.where(K["last"], _ones_dot(K["blk_b"], rd8), 0.0)
            dg8 = _ones_dot(K["triu_b"], dgc8)
            sgm = _sigmoid(ba_g + dt_ref[...])
            dalpha = dg8 * (-jnp.exp(al_ref[...])) * sgm
            lane8 = _iota((GR, 8), 1)
            dba_ref[rows, :] = jnp.where(lane8 < NHD, dbeta8 * beta8 * (1.0 - beta8), dalpha)
            valid = lane8 >= NHD
            sm_ref[0:1, 0:8] += jnp.sum(jnp.where(valid, dg8 * g8, 0.0), axis=0, keepdims=True)
            sm_ref[1:2, 0:8] += jnp.sum(jnp.where(valid, dalpha, 0.0), axis=0, keepdims=True)

        dcv = dc_ref[...]
        xv = x_ref[...]
        nxt = nxt_ref[...]
        w = cw_ref[...]
        dx = dcv * w[CONV_K - 1:CONV_K, :]
        dcw_ref[CONV_K - 1:CONV_K, :] += jnp.sum(dcv * xv, axis=0, keepdims=True)
        for k in range(1, CONV_K):
            j = CONV_K - 1 - k
            up = _shift_up(dcv, nxt, k)
            dx = dx + up * w[j:j + 1, :]
            dcw_ref[j:j + 1, :] += jnp.sum(up * xv, axis=0, keepdims=True)
        dx_ref[...] = dx
        nxt_ref[...] = dcv[0:8]

    return pl.pallas_call(
        body, name="dn_post_bwd", grid=(nb,),
        in_specs=[rrows(1536), rrows(1536), _full((CONV_K, 1536)), rrows(8),
                  pl.BlockSpec((8, TG), lambda i: (0, nb - 1 - i)), _full((1, 8)), _full((1, 8)), _full((8, 1)),
                  _full((8, 1)),
                  hm_spec(DHD), hm_spec(DHD), hm_spec(DHD), hm_spec(DHD), hm_spec(DHD), hm_spec(DHD), hm_spec(DHD),
                  hm_spec(GR), hm_spec(DHD), hm_spec(DHD)],
        out_specs=[rrows(1536), rrows(8), _full((8, LANES)), _full((8, 1536))],
        out_shape=[jax.ShapeDtypeStruct((T, 1536), F32), jax.ShapeDtypeStruct((T, 8), F32),
                   jax.ShapeDtypeStruct((8, LANES), F32), jax.ShapeDtypeStruct((8, 1536), F32)],
        scratch_shapes=[pltpu.VMEM((TG, 1536), F32), pltpu.VMEM((8, 1536), F32)],
        compiler_params=_cp(("arbitrary",)),
    )(draw, cv, conv_w, ba, bat, alog8, dtb8, alog8t, dtb8t, du, dw, dqd, dkd, dgx, do, vn, tm, u, w)


def _rms(x):
    return lax.rsqrt(jnp.mean(x * x, axis=1, keepdims=True) + EPS)


def _rms_bwd(dy, xh, r, g):
    dxh = dy * g
    return r * (dxh - xh * jnp.mean(dxh * xh, axis=1, keepdims=True))


def _hm_rows(tm):
    return pl.BlockSpec((NHD, tm, DHD), lambda i: (0, i, 0))


def _post_mix(apre, o, z, x, w_out, g_a, g_dn):
    T = x.shape[0]

    def body(ap_ref, o_ref, z_ref, x_ref, w_ref, ga_ref, gd_ref, x1_ref, mix_ref):
        ap = ap_ref[...]
        parts = [ap * _rms(ap) * ga_ref[...]]
        zz = z_ref[...]
        for h in range(NHD):
            oh = o_ref[h]
            zh = zz[:, DHD * h:DHD * (h + 1)]
            parts.append(oh * _rms(oh) * gd_ref[...] * (zh * _sigmoid(zh)))
        mix = jnp.concatenate(parts, axis=1).astype(_MXU)
        mix_ref[...] = mix
        x1_ref[...] = x_ref[...] + jnp.dot(mix, w_ref[...], preferred_element_type=F32)

    return pl.pallas_call(
        body, name="post_mix", grid=(T // TM,),
        in_specs=[_rows(TM, AW), _hm_rows(TM), _rows(TM, DW), _rows(TM, D), _full((D, D)), _full((1, AW)),
                  _full((1, DHD))],
        out_specs=[_rows(TM, D), _rows(TM, D)],
        out_shape=[jax.ShapeDtypeStruct((T, D), F32), jax.ShapeDtypeStruct((T, D), _MXU)],
        compiler_params=_cp(("arbitrary",)),
    )(apre, o, z, x, w_out, g_a, g_dn)


def _ffn(x1, tgt, wl_all, g_ffn):
    T = x1.shape[0]
    SH = FF // N_DEV
    nt = (((1,), (1,)), ((), ()))

    def body(x_ref, t_ref, wl_hbm, g_ref,
             dx1_ref, dx1b_ref, h2_ref, act_ref, dgu_ref, dyb_ref, loss_ref, dg_ref, wg, wu, wd, sem):
        @pl.when(pl.program_id(0) == 0)
        def _():
            cps = [pltpu.make_async_copy(wl_hbm.at[dev, pl.ds(128 + SH * k, SH), :], dst.at[pl.ds(SH * dev, SH), :],
                                         sem.at[N_DEV * k + dev])
                   for k, dst in enumerate((wg, wu, wd)) for dev in range(N_DEV)]
            for cp in cps:
                cp.start()
            for cp in cps:
                cp.wait()
            loss_ref[...] = jnp.zeros_like(loss_ref)
            dg_ref[...] = jnp.zeros_like(dg_ref)

        xv = x_ref[...]
        r = _rms(xv)
        xh = xv * r
        gg = g_ref[...]
        h2 = (xh * gg).astype(_MXU)
        h2_ref[...] = h2
        gate = lax.dot_general(h2, wg[...], nt, preferred_element_type=F32)
        up = lax.dot_general(h2, wu[...], nt, preferred_element_type=F32)
        sg = _sigmoid(gate)
        silu = gate * sg
        act = (silu * up).astype(_MXU)
        act_ref[...] = act
        y = xv + jnp.dot(act, wd[...], preferred_element_type=F32)
        err = y - t_ref[...]
        loss_ref[...] += jnp.sum(err * err, axis=0, keepdims=True)
        dy = err * (1.0 / D)
        dyb = dy.astype(_MXU)
        dyb_ref[...] = dyb
        dact = lax.dot_general(dyb, wd[...], nt, preferred_element_type=F32)
        dgate = (dact * up * (sg * (1.0 + gate * (1.0 - sg)))).astype(_MXU)
        dup = (dact * silu).astype(_MXU)
        dgu_ref[:, 0:FF] = dgate
        dgu_ref[:, FF:2 * FF] = dup
        dh2 = (jnp.dot(dgate, wg[...], preferred_element_type=F32)
               + jnp.dot(dup, wu[...], preferred_element_type=F32))
        dg_ref[...] += jnp.sum(dh2 * xh, axis=0, keepdims=True)
        dx1 = dy + _rms_bwd(dh2, xh, r, gg)
        dx1_ref[...] = dx1
        dx1b_ref[...] = dx1.astype(_MXU)

    anyspec = pl.BlockSpec(memory_space=pl.ANY)
    sd = lambda w, dt: jax.ShapeDtypeStruct((T, w), dt)
    return pl.pallas_call(
        body, name="ffn", grid=(T // TF,),
        in_specs=[_rows(TF, D), _rows(TF, D), anyspec, _full((1, D))],
        out_specs=[_rows(TF, D), _rows(TF, D), _rows(TF, D), _rows(TF, FF), _rows(TF, 2 * FF), _rows(TF, D),
                   _full((1, D)), _full((1, D))],
        out_shape=[sd(D, F32), sd(D, _MXU), sd(D, _MXU), sd(FF, _MXU), sd(2 * FF, _MXU), sd(D, _MXU),
                   jax.ShapeDtypeStruct((1, D), F32), jax.ShapeDtypeStruct((1, D), F32)],
        scratch_shapes=[pltpu.VMEM((FF, D), _MXU)] * 3 + [pltpu.SemaphoreType.DMA((3 * N_DEV,))],
        compiler_params=_cp(("arbitrary",)),
    )(x1, tgt, wl_all, g_ffn)


def _mix_bwd(dx1b, w_out, apre, o, z, g_a, g_dn):
    T = dx1b.shape[0]

    def body(dx_ref, w_ref, ap_ref, o_ref, z_ref, ga_ref, gd_ref, dap_ref, do_ref, dz_ref, dga_ref, dgd_ref):
        @pl.when(pl.program_id(0) == 0)
        def _():
            dga_ref[...] = jnp.zeros_like(dga_ref)
            dgd_ref[...] = jnp.zeros_like(dgd_ref)

        dmix = lax.dot_general(dx_ref[...], w_ref[...], (((1,), (1,)), ((), ())), preferred_element_type=F32)
        ap = ap_ref[...]
        ra = _rms(ap)
        ah = ap * ra
        da = dmix[:, 0:AW]
        dga_ref[...] += jnp.sum(da * ah, axis=0, keepdims=True)
        dap_ref[...] = _rms_bwd(da, ah, ra, ga_ref[...])
        zz = z_ref[...]
        gd = gd_ref[...]
        for h in range(NHD):
            cs = slice(DHD * h, DHD * (h + 1))
            dd = dmix[:, AW + DHD * h:AW + DHD * (h + 1)]
            oh = o_ref[h]
            ro = _rms(oh)
            ohh = oh * ro
            zh = zz[:, cs]
            sz = _sigmoid(zh)
            dz_ref[:, cs] = dd * (ohh * gd) * (sz * (1.0 + zh * (1.0 - sz)))
            don = dd * (zh * sz)
            dgd_ref[...] += jnp.sum(don * ohh, axis=0, keepdims=True)
            do_ref[h] = _rms_bwd(don, ohh, ro, gd)

    return pl.pallas_call(
        body, name="mix_bwd", grid=(T // TM,),
        in_specs=[_rows(TM, D), _full((D, D)), _rows(TM, AW), _hm_rows(TM), _rows(TM, DW), _full((1, AW)),
                  _full((1, DHD))],
        out_specs=[_rows(TM, AW), _hm_rows(TM), _rows(TM, DW), _full((1, AW)), _full((1, DHD))],
        out_shape=[jax.ShapeDtypeStruct((T, AW), F32), jax.ShapeDtypeStruct((NHD, T, DHD), F32),
                   jax.ShapeDtypeStruct((T, DW), F32), jax.ShapeDtypeStruct((1, AW), F32),
                   jax.ShapeDtypeStruct((1, DHD), F32)],
        compiler_params=_cp(("arbitrary",)),
    )(dx1b, w_out, apre, o, z, g_a, g_dn)


DPW = 3712


def _inproj_bwd(dqn, dkn, dv, araw, ddraw, dz, dba, x, dx1, w_int, w_ba, g_mix, qg_t, kg_t):
    T = x.shape[0]

    def body(dqn_ref, dkn_ref, dv_ref, ar_ref, dd_ref, dz_ref, dba_ref, x_ref, dx1_ref, w_hbm, wba_ref, g_ref, qg_ref,
             kg_ref, dx_ref, dp_ref, dgm_ref, dqg_ref, dkg_ref, w_ref, w_sem):
        @pl.when(pl.program_id(0) == 0)
        def _():
            cp = pltpu.make_async_copy(w_hbm, w_ref, w_sem)
            cp.start()
            cp.wait()
            dgm_ref[...] = jnp.zeros_like(dgm_ref)
            dqg_ref[...] = jnp.zeros_like(dqg_ref)
            dkg_ref[...] = jnp.zeros_like(dkg_ref)

        bd = _block_ones(AW // 2, DHA)

        def head_norm_bwd(raw, dyn, gain, dg_ref):
            r = _head_rms(raw, bd, DHA)
            xh = raw * r
            dg_ref[...] += jnp.sum(dyn * xh, axis=0, keepdims=True)
            dxh = dyn * gain
            return r * (dxh - xh * (_head_sum(dxh * xh, bd) * (1.0 / DHA)))

        def segment(lo, val):
            vb = val.astype(_MXU)
            dp_ref[:, lo:lo + val.shape[1]] = vb
            return jnp.dot(vb, w_ref[lo:lo + val.shape[1], :], preferred_element_type=F32)

        dh = segment(1536, dd_ref[...]) + segment(2 * AW, dv_ref[...]) + segment(3072, dz_ref[...])
        dbab = dba_ref[...].astype(_MXU)
        dp_ref[:, 3584:DPW] = jnp.zeros((TM, DPW - 3584), _MXU)
        dp_ref[:, 3584:3592] = dbab
        dh = dh + lax.dot_general(dbab, wba_ref[...], (((1,), (1,)), ((), ())), preferred_element_type=F32)
        ar = ar_ref[...]
        dq = head_norm_bwd(ar[:, 0:AW], dqn_ref[...] * (DHA ** -0.5), qg_ref[...], dqg_ref)
        dk = head_norm_bwd(ar[:, AW:2 * AW], dkn_ref[...], kg_ref[...], dkg_ref)
        dh = dh + segment(0, dq) + segment(AW, dk)
        xv = x_ref[...]
        r = _rms(xv)
        xh = xv * r
        dgm_ref[...] += jnp.sum(dh * xh, axis=0, keepdims=True)
        dx_ref[...] = dx1_ref[...] + _rms_bwd(dh, xh, r, g_ref[...])

    return pl.pallas_call(
        body, name="inproj_bwd", grid=(T // TM,),
        in_specs=[_rows(TM, AW), _rows(TM, AW), _rows(TM, AW), _rows(TM, 1536), _rows(TM, 1536), _rows(TM, DW),
                  _rows(TM, 8), _rows(TM, D), _rows(TM, D), pl.BlockSpec(memory_space=pl.ANY), _full((D, 8)),
                  _full((1, D)), _full((1, AW)), _full((1, AW))],
        out_specs=[_rows(TM, D), _rows(TM, DPW), _full((1, D)), _full((1, AW)), _full((1, AW))],
        out_shape=[jax.ShapeDtypeStruct((T, D), F32), jax.ShapeDtypeStruct((T, DPW), _MXU),
                   jax.ShapeDtypeStruct((1, D), F32), jax.ShapeDtypeStruct((1, AW), F32),
                   jax.ShapeDtypeStruct((1, AW), F32)],
        scratch_shapes=[pltpu.VMEM((3584, D), _MXU), pltpu.SemaphoreType.DMA],
        compiler_params=_cp(("arbitrary",)),
    )(dqn, dkn, dv, araw, ddraw, dz, dba, x, dx1, w_int, w_ba, g_mix, qg_t, kg_t)


def _wgrad(a, b, name, tk=1024, tn=None, out_dtype=F32, transposed=False, gather=None):
    T, M = a.shape
    N = b.shape[1]
    tn = N if tn is None else tn
    assert T % tk == 0 and N % tn == 0, (T, tk, N, tn)
    nk = T // tk
    hosted = gather is not None
    assert not hosted or (tn == N and nk >= 3)

    def body(a_ref, b_ref, *rest):
        k = pl.program_id(1)
        if hosted:
            g_ref, o_ref, g_all, acc, send_sems, recv_sems, local_sem = rest
            ag = _Gather(g_ref, g_all, send_sems, recv_sems, local_sem)
            pl.when(k == 0)(ag.start)
            pl.when(k == nk // 2)(ag.forward)
        else:
            o_ref, acc = rest

        @pl.when(k == 0)
        def _():
            acc[...] = jnp.zeros_like(acc)

        acc[...] += lax.dot_general(a_ref[...], b_ref[...], (((0,), (0,)), ((), ())), preferred_element_type=F32)

        @pl.when(k == nk - 1)
        def _():
            r = acc[...]
            o_ref[...] = (r.T if transposed else r).astype(out_dtype)
            if hosted:
                ag.finish()

    if transposed:
        out_spec, out_shape = pl.BlockSpec((tn, M), lambda j, k: (j, 0)), (N, M)
    else:
        out_spec, out_shape = pl.BlockSpec((M, tn), lambda j, k: (0, j)), (M, N)
    anyspec = pl.BlockSpec(memory_space=pl.ANY)
    return pl.pallas_call(
        body, name=name, grid=(N // tn, nk),
        in_specs=[pl.BlockSpec((tk, M), lambda j, k: (k, 0)), pl.BlockSpec((tk, tn), lambda j, k: (k, j))]
        + ([anyspec] if hosted else []),
        out_specs=[out_spec, anyspec] if hosted else out_spec,
        out_shape=([jax.ShapeDtypeStruct(out_shape, out_dtype),
                    jax.ShapeDtypeStruct((N_DEV,) + gather.shape, gather.dtype)] if hosted
                   else jax.ShapeDtypeStruct(out_shape, out_dtype)),
        scratch_shapes=[pltpu.VMEM((M, tn), F32)] + (_COMM_SEMS if hosted else []),
        compiler_params=_cp(("arbitrary", "arbitrary")),
    )(*((a, b) + ((gather,) if hosted else ())))


def _adamw(parts, w, m, v, name, tr, send=None):
    K, R, W = parts.shape
    n = R // tr

    def body(p_ref, w_ref, m_ref, v_ref, *rest):
        if send is not None:
            send_ref, g_ref, d_ref, nm_ref, nv_ref, recv_ref, send_sems, recv_sems, local_sem = rest
            sc = _Scatter(send_ref, recv_ref, send_sems, recv_sems, local_sem)
            pl.when(pl.program_id(0) == 0)(sc.start)
        else:
            g_ref, d_ref, nm_ref, nv_ref = rest
        g = p_ref[0].astype(F32)
        for k in range(1, K):
            g = g + p_ref[k].astype(F32)
        g_ref[...] = g
        nm = ADAM_B1 * m_ref[...] + (1.0 - ADAM_B1) * g
        nv = ADAM_B2 * v_ref[...] + (1.0 - ADAM_B2) * (g * g)
        nm_ref[...] = nm
        nv_ref[...] = nv
        m_hat = nm / (1.0 - ADAM_B1 ** ADAM_STEP)
        v_hat = nv / (1.0 - ADAM_B2 ** ADAM_STEP)
        d_ref[...] = -ADAM_LR * (m_hat / (jnp.sqrt(v_hat) + ADAM_EPS) + ADAM_WD * w_ref[...])
        if send is not None:
            pl.when(pl.program_id(0) == n - 1)(sc.finish)

    o = jax.ShapeDtypeStruct((R, W), F32)
    anyspec = pl.BlockSpec(memory_space=pl.ANY)
    hosted = send is not None
    return pl.pallas_call(
        body, name=name, grid=(n,),
        in_specs=[pl.BlockSpec((K, tr, W), lambda i: (0, i, 0)), _rows(tr, W), _rows(tr, W), _rows(tr, W)]
        + ([anyspec] if hosted else []),
        out_specs=[_rows(tr, W)] * 4 + ([anyspec] if hosted else []),
        out_shape=[o] * 4 + ([jax.ShapeDtypeStruct(send.shape, send.dtype)] if hosted else []),
        scratch_shapes=_COMM_SEMS if hosted else [],
        compiler_params=_cp(("arbitrary",)),
    )(*((parts, w, m, v) + ((send,) if hosted else ())))


SM_ROWS = 136
R_GMIX, R_GFFN, R_QG, R_KG, R_GA, R_GDN, R_ALOG, R_DT, R_LOSS, R_CONV, R_REL = 0, 8, 16, 24, 32, 40, 48, 49, 56, 64, 112


def _small_reduce(gathered):
    def body(p_ref, o_ref):
        s = p_ref[0]
        for k in range(1, N_DEV):
            s = s + p_ref[k]
        o_ref[...] = s
        for r0 in (R_QG, R_KG):
            rs = jnp.sum(s[r0:r0 + 4], axis=0, keepdims=True)
            o_ref[r0:r0 + 1, :] = rs + pltpu.roll(rs, DHA, 1)
        tot = jnp.sum(jnp.sum(s[R_LOSS:R_LOSS + 8], axis=0, keepdims=True), axis=1, keepdims=True)
        o_ref[R_LOSS:R_LOSS + 1, :] = jnp.broadcast_to(tot * (0.5 / D), (1, LANES))

    return pl.pallas_call(
        body, name="small_reduce",
        out_shape=jax.ShapeDtypeStruct((SM_ROWS, LANES), F32),
    )(gathered)


_WIRE = jnp.bfloat16
RA_USED, RA = 449, 464
RL = 128 + 3 * 352


def _pack_rows(parts, rows=None):
    p = jnp.concatenate([t.reshape(-1, D) for t in parts], axis=0) if len(parts) > 1 else parts[0].reshape(-1, D)
    return p if rows is None else jnp.pad(p, ((0, rows - p.shape[0]), (0, 0)))


def _unpack_rows(packed, shapes):
    out, r = [], 0
    for shp in shapes:
        nr = math.prod(shp) // D
        out.append(packed[r:r + nr].reshape(shp))
        r += nr
    return out


def _pad8(t):
    return jnp.pad(t, ((0, (-t.shape[0]) % 8), (0, 0)))


PART = 8 * LANES


def _pack_lanes(parts):
    rows = []
    for p in parts:
        f = p.reshape(-1)
        rows.append(jnp.pad(f, (0, (-f.shape[0]) % PART)).reshape(-1, LANES))
    return jnp.concatenate(rows, axis=0)


def _unpack_lanes(packed, shapes):
    out, r = [], 0
    for shp in shapes:
        n = math.prod(shp)
        nr = 8 * -(-n // PART)
        out.append(packed[r:r + nr].reshape(-1)[:n].reshape(shp))
        r += nr
    return out


def kernel(x, norm_mix_g, w_in, attn_q_norm_g, attn_k_norm_g, rel_bias, attn_out_norm_g, conv_w, a_log, dt_bias, dn_out_norm_g, w_out, norm_ffn_g, w_gate, w_up, w_down, loss_target, m_norm_mix_g, m_w_in, m_attn_q_norm_g, m_attn_k_norm_g, m_rel_bias, m_attn_out_norm_g, m_conv_w, m_a_log, m_dt_bias, m_dn_out_norm_g, m_w_out, m_norm_ffn_g, m_w_gate, m_w_up, m_w_down, v_norm_mix_g, v_w_in, v_attn_q_norm_g, v_attn_k_norm_g, v_rel_bias, v_attn_out_norm_g, v_conv_w, v_a_log, v_dt_bias, v_dn_out_norm_g, v_w_out, v_norm_ffn_g, v_w_gate, v_w_up, v_w_down):
    xs, tgt = x[0], loss_target[0]
    my_idx = 4 * lax.axis_index("x") + 2 * lax.axis_index("y") + lax.axis_index("c")
    late_w = (w_out[0], w_gate[0], w_up[0], w_down[0])

    tab, tabt, wa_all, cw_all = _bias_tables(
        jnp.pad(rel_bias[0].T, ((0, 0), (0, VAR0 - 257))), _pack_rows([w_in[0].T.astype(_MXU)], RA),
        jnp.pad(conv_w[0], ((0, 4), (0, 64))))
    W_in_t = wa_all[:, 0:RA_USED].reshape(N_DEV * RA_USED, D)
    W_int, W_ba = W_in_t[0:3584], W_in_t[3584:3592].T
    conv_full = cw_all[:, 0:CONV_K, 0:192].transpose(1, 0, 2).reshape(CONV_K, 1536)

    qg_t = jnp.tile(attn_q_norm_g, (1, NHA))
    kg_t = jnp.tile(attn_k_norm_g, (1, NHA))
    z4 = jnp.zeros((1, NHD), F32)
    alog8 = jnp.concatenate([z4, a_log], axis=1)
    dtb8 = jnp.concatenate([z4, dt_bias], axis=1)

    late_t = lambda ts: (ts[0], ts[1].T, ts[2].T, ts[3])
    araw, an, draw, z, ba, hb, wl_all = _inproj(xs, norm_mix_g, W_int, W_ba, qg_t, kg_t,
                                                _pack_rows([w.astype(_MXU) for w in late_t(late_w)]))
    W_out = wl_all[:, 0:128].reshape(D, D)
    apre = _attn_fwd(an, tab)
    bat = ba.T
    dn_args = (draw, conv_full, ba, bat, alog8, dtb8, alog8.T, dtb8.T)
    u, w, kd, tm, wq, km, mq, wt, elb, cv = _dn_prep(*dn_args)
    o, vn, sn = _dn_scan(u, wq, km, elb)
    x1, mix = _post_mix(apre, o, z, xs, W_out, attn_out_norm_g, dn_out_norm_g)

    dx1, dx1b, h2, act, dgu, dyb, loss_row, dgffn = _ffn(x1, tgt, wl_all, norm_ffn_g)

    gW_out = _wgrad(mix, dx1b, "wgrad_out", out_dtype=_WIRE)
    gW_gu_t = _wgrad(h2, dgu, "wgrad_gate_up", tn=FF, out_dtype=_WIRE, transposed=True)
    gW_down = _wgrad(dyb, act, "wgrad_down", out_dtype=_WIRE, transposed=True)
    send_late = jnp.concatenate(
        [gW_out.reshape(N_DEV, 128, D), gW_gu_t[0:FF].reshape(N_DEV, 352, D), gW_gu_t[FF:].reshape(N_DEV, 352, D),
         gW_down.reshape(N_DEV, 352, D)], axis=1)

    dap, do, dz, dga, dgdn = _mix_bwd(dx1b, W_out, apre, o, z, attn_out_norm_g, dn_out_norm_g)
    dqn, dkn, dv, dtabt, recv_late = _attn_bwd(an, dap, tabt, send_late)
    drel = _bias_grad(dtabt)
    du, dw, dqd, dkd, dgx = _dn_scan_bwd(do, mq, kd, wt, sn, vn, elb)
    ddraw, dba, sm, dcw = _dn_post_bwd(draw, cv, *dn_args[1:], du, dw, dqd, dkd, dgx, do, vn, tm, u, w)
    gx, dproj, dgmix, dqg, dkg = _inproj_bwd(dqn, dkn, dv, araw, ddraw, dz, dba, xs, dx1, W_int, W_ba, norm_mix_g,
                                             qg_t, kg_t)

    partial = jnp.concatenate(
        [dgmix.reshape(8, LANES), dgffn.reshape(8, LANES), _pad8(dqg.reshape(4, LANES)), _pad8(dkg.reshape(4, LANES)),
         _pad8(dga.reshape(4, LANES)), _pad8(dgdn), sm, loss_row.reshape(8, LANES),
         dcw[0:CONV_K].reshape(48, LANES), drel.reshape(24, LANES)], axis=0)
    gW_in_t, partial_all = _wgrad(hb, dproj, "wgrad_in", tk=512, out_dtype=_WIRE, transposed=True,
                                  gather=partial)
    send_in = jnp.pad(gW_in_t[0:N_DEV * RA_USED].reshape(N_DEV, RA_USED, D), ((0, 0), (0, RA - RA_USED), (0, 0)))
    late_m = (m_w_out[0], m_w_gate[0], m_w_up[0], m_w_down[0])
    late_v = (v_w_out[0], v_w_gate[0], v_w_up[0], v_w_down[0])
    *outs_late, recv_in = _adamw(recv_late, _pack_rows(late_t(late_w)), _pack_rows(late_t(late_m)),
                                 _pack_rows(late_t(late_v)), "adamw_late", 32, send=send_in)
    outs_in = _adamw(recv_in, _pack_rows([w_in[0].T], RA), _pack_rows([m_w_in[0].T], RA),
                     _pack_rows([v_w_in[0].T], RA), "adamw_w_in", 16)
    late_t_shapes = [t.shape for t in late_t(late_w)]
    big = [[a[0:RA_USED].T] + list(late_t(_unpack_rows(b, late_t_shapes))) for a, b in zip(outs_in, outs_late)]
    bg, bd_, bm, bv = big

    S = _small_reduce(partial_all)
    loss = S[R_LOSS, 0]
    g_conv = lax.dynamic_slice(S[R_CONV:R_CONV + 48].reshape(CONV_K, 1536), (0, 192 * my_idx), (CONV_K, 192))
    sg = [S[R_GMIX:R_GMIX + 8].reshape(1, D), S[R_QG:R_QG + 1, 0:DHA], S[R_KG:R_KG + 1, 0:DHA],
          S[R_REL:R_REL + 24].reshape(NHA, 384)[:, 0:257].T, S[R_GA:R_GA + 4].reshape(1, AW), g_conv,
          S[R_ALOG:R_ALOG + 1, NHD:2 * NHD], S[R_DT:R_DT + 1, NHD:2 * NHD], S[R_GDN:R_GDN + 1], S[R_GFFN:R_GFFN + 8].reshape(1, D)]
    sw = [norm_mix_g, attn_q_norm_g, attn_k_norm_g, rel_bias[0], attn_out_norm_g, conv_w[0], a_log, dt_bias, dn_out_norm_g, norm_ffn_g]
    smm = [m_norm_mix_g, m_attn_q_norm_g, m_attn_k_norm_g, m_rel_bias[0], m_attn_out_norm_g, m_conv_w[0], m_a_log, m_dt_bias, m_dn_out_norm_g, m_norm_ffn_g]
    svv = [v_norm_mix_g, v_attn_q_norm_g, v_attn_k_norm_g, v_rel_bias[0], v_attn_out_norm_g, v_conv_w[0], v_a_log, v_dt_bias, v_dn_out_norm_g, v_norm_ffn_g]
    s_shapes = [t.shape for t in sw]
    pg = _pack_lanes(sg)
    s_out = _adamw(pg[None], _pack_lanes(sw), _pack_lanes(smm), _pack_lanes(svv), "adamw_small", pg.shape[0])
    s_g, s_d, s_m, s_v = (_unpack_lanes(t, s_shapes) for t in s_out)

    lead = lambda t: t[None]
    def ordered(small, big):
        nm, q, k, rel, ao, cw, al, dtb, dno, nf = small
        wi, wo, wgt, wu, wdn = big
        return [nm, lead(wi), q, k, lead(rel), ao, lead(cw), al, dtb, dno, lead(wo), nf, lead(wgt), lead(wu), lead(wdn)]
    outs = [loss, gx[None]]
    for small, big in ((s_g, bg), (s_d, bd_), (s_m, bm), (s_v, bv)):
        outs += ordered(small, big)
    return tuple(outs)
```

```python
import math

import jax
import jax.numpy as jnp
from jax import lax
from jax.experimental import pallas as pl
from jax.experimental.pallas import tpu as pltpu

F32 = jnp.float32
BF16 = jnp.bfloat16
_MXU = jnp.bfloat16

D = 1024
AW = 512
NHA = 8
DHA = 64
CH = 64
NHD = 4
DHD = 128
DW = 512
FF = 2816
EPS = 1e-6
NEG = -1e30
N_DEV = 8
LANES = 128
VMEM_LIMIT = 56 * 1024 * 1024

ADAM_LR = 0.001
ADAM_B1 = 0.9
ADAM_B2 = 0.999
ADAM_EPS = 1e-08
ADAM_WD = 0.01
ADAM_STEP = 10

MESH_T = pl.DeviceIdType.MESH


def _cp(sem=None, vmem=VMEM_LIMIT):
    kw = dict(vmem_limit_bytes=vmem)
    if sem is not None:
        kw["dimension_semantics"] = sem
    return pltpu.CompilerParams(**kw)


def _dot(a, b):
    return jnp.dot(a.astype(_MXU), b.astype(_MXU), preferred_element_type=F32)


def _dot_nt(a, b):
    return lax.dot_general(a.astype(_MXU), b.astype(_MXU), (((1,), (1,)), ((), ())), preferred_element_type=F32)


def _dot_tn(a, b):
    return lax.dot_general(a.astype(_MXU), b.astype(_MXU), (((0,), (0,)), ((), ())), preferred_element_type=F32)


def _iota(shape, dim):
    return lax.broadcasted_iota(jnp.int32, shape, dim)


def _block_ones(n, blk, dtype=BF16):
    r, c = _iota((n, n), 0), _iota((n, n), 1)
    return jnp.where((r // blk) == (c // blk), 1.0, 0.0).astype(dtype)


def _sigmoid(x):
    return 1.0 / (1.0 + jnp.exp(-x))


def _softplus(x):
    return jnp.maximum(x, 0.0) + jnp.log(1.0 + jnp.exp(-jnp.abs(x)))


def _col(x, k):
    lane = _iota(x.shape, 1)
    return jnp.sum(jnp.where(lane == k, x, 0.0), axis=1, keepdims=True)


def _row(x, k):
    sub = _iota(x.shape, 0)
    return jnp.sum(jnp.where(sub == k, x, 0.0), axis=0, keepdims=True)


def _my_pos():
    return lax.axis_index("x"), lax.axis_index("y"), lax.axis_index("c")


_COMM_SEMS = [pltpu.SemaphoreType.DMA((7,)), pltpu.SemaphoreType.DMA((7,)), pltpu.SemaphoreType.DMA]


class _Gather:
    def __init__(self, x_ref, out_ref, send_sems, recv_sems, local_sem):
        x, y, c = _my_pos()
        me, sibling = (x, y, c), (x, y, 1 - c)
        chips = [(1 - x, y), (x, 1 - y), (1 - x, 1 - y)]

        def slot(px, py, pc):
            return out_ref.at[4 * px + 2 * py + pc]

        def copy(k, block, to, src=None):
            return pltpu.make_async_remote_copy(
                src_ref=slot(*block) if src is None else src, dst_ref=slot(*block),
                send_sem=send_sems.at[k], recv_sem=recv_sems.at[k], device_id=to, device_id_type=MESH_T)

        self.mine = pltpu.make_async_copy(x_ref, slot(*me), local_sem)
        self.first = [copy(0, me, sibling, src=x_ref)]
        self.first += [copy(1 + j, me, (*chip, c), src=x_ref) for j, chip in enumerate(chips)]
        self.passed = [copy(4 + j, (*chip, c), sibling) for j, chip in enumerate(chips)]
        self.from_chips = [copy(1 + j, (*chip, c), me) for j, chip in enumerate(chips)]
        self.from_sibling = [copy(0, sibling, me)] + [copy(4 + j, (*chip, 1 - c), me) for j, chip in enumerate(chips)]

    def start(self):
        self.mine.start()
        for cp in self.first:
            cp.start()

    def forward(self):
        for arrived, onward in zip(self.from_chips, self.passed):
            arrived.wait_recv()
            onward.start()

    def finish(self):
        for cp in self.from_sibling:
            cp.wait_recv()
        for cp in self.first + self.passed:
            cp.wait_send()
        self.mine.wait()


class _Scatter:
    def __init__(self, s_ref, r_ref, send_sems, recv_sems, local_sem):
        x, y, c = _my_pos()
        self.mine = pltpu.make_async_copy(s_ref.at[4 * x + 2 * y + c], r_ref.at[0], local_sem)
        self.copies = []
        for m in range(1, N_DEV):
            px = x ^ ((m >> 2) & 1)
            py = y ^ ((m >> 1) & 1)
            pc = c ^ (m & 1)
            self.copies.append(pltpu.make_async_remote_copy(
                src_ref=s_ref.at[4 * px + 2 * py + pc], dst_ref=r_ref.at[m],
                send_sem=send_sems.at[m - 1], recv_sem=recv_sems.at[m - 1],
                device_id=(px, py, pc), device_id_type=MESH_T))

    def start(self):
        self.mine.start()
        for cp in self.copies:
            cp.start()

    def finish(self):
        for cp in self.copies:
            cp.wait_recv()
        for cp in self.copies:
            cp.wait_send()
        self.mine.wait()


TM = 512
TF = 256
TG = 512


def _full(shape):
    nd = len(shape)
    return pl.BlockSpec(shape, lambda i: (0,) * nd)


def _rows(tm, w):
    return pl.BlockSpec((tm, w), lambda i: (i, 0))


def _head_sum(x, bd):
    one_pass = lambda t: jnp.dot(t.astype(_MXU), bd.astype(_MXU), preferred_element_type=F32)
    return jnp.concatenate([one_pass(x[:, 0:256]), one_pass(x[:, 256:512])], axis=1)


def _head_rms(x, bd, width):
    return lax.rsqrt(_head_sum(x * x, bd) * (1.0 / width) + EPS)


def _inproj(x, g_mix, w_int, w_ba, qg_t, kg_t, later_w):
    T = x.shape[0]
    nt = T // TM
    ntd = (((1,), (1,)), ((), ()))

    def body(x_ref, g_ref, w_ref, wba_ref, qg_ref, kg_ref, lw_ref, araw_ref, an_ref, draw_ref, z_ref, ba_ref, h_ref,
             lw_all, send_sems, recv_sems, local_sem):
        i = pl.program_id(0)
        ag = _Gather(lw_ref, lw_all, send_sems, recv_sems, local_sem)
        pl.when(i == 0)(ag.start)
        pl.when(i == nt // 2)(ag.forward)
        xv = x_ref[...]
        r = lax.rsqrt(jnp.mean(xv * xv, axis=1, keepdims=True) + EPS)
        h = (xv * r * g_ref[...]).astype(_MXU)
        h_ref[...] = h
        proj = lambda lo, hi: lax.dot_general(h, w_ref[lo:hi, :], ntd, preferred_element_type=F32)
        q, k, v = proj(0, AW), proj(AW, 2 * AW), proj(2 * AW, 3 * AW)
        draw_ref[...] = proj(1536, 3072)
        z_ref[...] = proj(3072, 3584)
        ba_ref[...] = jnp.dot(h, wba_ref[...], preferred_element_type=F32)
        araw_ref[:, 0:AW] = q
        araw_ref[:, AW:2 * AW] = k
        bd = _block_ones(AW // 2, DHA)
        qn = q * _head_rms(q, bd, DHA) * (qg_ref[...] * (DHA ** -0.5))
        kn = k * _head_rms(k, bd, DHA) * kg_ref[...]
        an_ref[:, 0:AW] = qn.astype(_MXU)
        an_ref[:, AW:2 * AW] = kn.astype(_MXU)
        an_ref[:, 2 * AW:3 * AW] = v.astype(_MXU)
        pl.when(i == nt - 1)(ag.finish)

    anyspec = pl.BlockSpec(memory_space=pl.ANY)
    return pl.pallas_call(
        body, name="inproj", grid=(nt,),
        in_specs=[_rows(TM, D), _full((1, D)), _full((3584, D)), _full((D, 8)), _full((1, AW)), _full((1, AW)),
                  anyspec],
        out_specs=[_rows(TM, 2 * AW), _rows(TM, 1536), _rows(TM, 1536), _rows(TM, DW), _rows(TM, 8), _rows(TM, D),
                   anyspec],
        out_shape=[jax.ShapeDtypeStruct((T, 2 * AW), F32), jax.ShapeDtypeStruct((T, 1536), _MXU),
                   jax.ShapeDtypeStruct((T, 1536), F32), jax.ShapeDtypeStruct((T, DW), F32),
                   jax.ShapeDtypeStruct((T, 8), F32), jax.ShapeDtypeStruct((T, D), _MXU),
                   jax.ShapeDtypeStruct((N_DEV,) + later_w.shape, later_w.dtype)],
        scratch_shapes=_COMM_SEMS,
        compiler_params=_cp(("arbitrary",)),
    )(x, g_mix, w_int, w_ba, qg_t, kg_t, later_w)


TQ = 256
TW = 768
VAR0 = 384
TOEP = 1024


def _bias_tables(rb_t, w_shard, c_shard):
    def body(rb_ref, w_ref, c_ref, tab_ref, tabt_ref, w_all, c_all, ws, wr, wl, cs, cr, cl):
        h = pl.program_id(0)
        gathers = [_Gather(w_ref, w_all, ws, wr, wl), _Gather(c_ref, c_all, cs, cr, cl)]

        @pl.when(h == 0)
        def _():
            for ag in gathers:
                ag.start()

        @pl.when(h == NHA // 2)
        def _():
            for ag in gathers:
                ag.forward()

        rb8 = jnp.broadcast_to(_row(rb_ref[...], h), (8, VAR0))
        n = _iota((VAR0, TOEP), 1)
        t = _iota((VAR0, TOEP), 0)

        def line(m):
            onehot = jnp.where(jnp.clip(512 - m, -128, 128) + 128 == t, 1.0, 0.0).astype(BF16)
            return sum(jnp.dot(p, onehot, preferred_element_type=F32) for p in _split3(rb8))[0:1, :]

        def band(r, j, first_key):
            return ((j >> 6) >= (r >> 6)) & ((j >> 6) <= (r >> 6) + 8) & (j >= first_key)

        g = line(jnp.where(n < TW, n, n - TOEP))
        tab = pltpu.roll(jnp.broadcast_to(g, (TQ, TOEP)), 0, 1, stride=1, stride_axis=0)[:, 0:TW]
        gt = line(jnp.where(n < TQ, -n, TOEP - n))
        tabt = pltpu.roll(jnp.broadcast_to(gt, (TW, TOEP)), 0, 1, stride=1, stride_axis=0)[:, 0:TQ]
        for v in range(3):
            first_key = max(512 - TQ * v, 0)
            tab_ref[v, 0] = jnp.where(band(_iota((TQ, TW), 0), _iota((TQ, TW), 1), first_key), tab, NEG)
            tabt_ref[v, 0] = jnp.where(band(_iota((TW, TQ), 1), _iota((TW, TQ), 0), first_key), tabt, NEG)

        @pl.when(h == NHA - 1)
        def _():
            for ag in gathers:
                ag.finish()

    anyspec = pl.BlockSpec(memory_space=pl.ANY)
    return pl.pallas_call(
        body, name="bias_tables", grid=(NHA,),
        in_specs=[_full((NHA, VAR0)), anyspec, anyspec],
        out_specs=[pl.BlockSpec((3, 1, TQ, TW), lambda h: (0, h, 0, 0)),
                   pl.BlockSpec((3, 1, TW, TQ), lambda h: (0, h, 0, 0)), anyspec, anyspec],
        out_shape=[jax.ShapeDtypeStruct((3, NHA, TQ, TW), F32), jax.ShapeDtypeStruct((3, NHA, TW, TQ), F32),
                   jax.ShapeDtypeStruct((N_DEV,) + w_shard.shape, w_shard.dtype),
                   jax.ShapeDtypeStruct((N_DEV,) + c_shard.shape, c_shard.dtype)],
        scratch_shapes=_COMM_SEMS + _COMM_SEMS,
        compiler_params=_cp(("arbitrary",)),
    )(rb_t, w_shard, c_shard)


def _bias_grad(dtabt):
    def body(d_ref, o_ref):
        a, b = _iota((TQ, TQ), 0), _iota((TQ, TQ), 1)
        anti = jnp.where(a + b == TQ - 1, 1.0, 0.0).astype(BF16)
        drev = sum(jnp.dot(t, anti, preferred_element_type=F32) for t in _split3(d_ref[0]))
        wide = jnp.concatenate([drev, jnp.zeros((TW, TOEP - TQ), F32)], axis=1)
        cols = jnp.sum(pltpu.roll(wide, 0, 1, stride=1, stride_axis=0), axis=0, keepdims=True)
        c = _iota((TOEP, VAR0), 0)
        idx = jnp.clip(512 + TQ - 1 - c, -128, 128) + 128
        onehot = jnp.where(idx == _iota((TOEP, VAR0), 1), 1.0, 0.0).astype(BF16)
        cols8 = jnp.broadcast_to(cols, (8, TOEP))
        o_ref[0] = sum(jnp.dot(t, onehot, preferred_element_type=F32) for t in _split3(cols8))[0:1, :]

    return pl.pallas_call(
        body, name="bias_grad", grid=(NHA,),
        in_specs=[pl.BlockSpec((1, TW, TQ), lambda h: (h, 0, 0))],
        out_specs=pl.BlockSpec((1, 1, VAR0), lambda h: (h, 0, 0)),
        out_shape=jax.ShapeDtypeStruct((NHA, 1, VAR0), F32),
        compiler_params=_cp(("arbitrary",)),
    )(dtabt)


def _kv_spec(col, back):
    return pl.BlockSpec((TQ, AW), lambda i: (jnp.maximum(i - back, 0), col))


def _attn_fwd(an, tab):
    T = an.shape[0]

    def body(q_ref, k2_ref, k1_ref, k0_ref, v2_ref, v1_ref, v0_ref, tab_ref, o_ref):
        i = pl.program_id(0)
        kwin = jnp.concatenate([k2_ref[...], k1_ref[...], k0_ref[...]], axis=0)
        vwin = jnp.concatenate([v2_ref[...], v1_ref[...], v0_ref[...]], axis=0)
        q = q_ref[...]
        lo_half = _iota((TQ, LANES), 1) < DHA

        def scores(h):
            sl = slice(LANES * (h // 2), LANES * (h // 2 + 1))
            mask = lo_half if h % 2 == 0 else jnp.logical_not(lo_half)
            qm = jnp.where(mask, q[:, sl], jnp.zeros((TQ, LANES), q.dtype))
            return _dot_nt(qm, kwin[:, sl]) + tab_ref[0, h]

        s_next = scores(0)
        outs = []
        for h in range(NHA):
            s = s_next
            if h + 1 < NHA:
                s_next = scores(h + 1)
            sl = slice(LANES * (h // 2), LANES * (h // 2 + 1))
            m = jnp.max(s, axis=1, keepdims=True)
            e = jnp.exp(s - m)
            l = jnp.sum(e, axis=1, keepdims=True)
            outs.append(_dot(e, vwin[:, sl]) / l)
            if h % 2 == 1:
                o_ref[:, sl] = jnp.where(lo_half, outs[h - 1], outs[h])

    return pl.pallas_call(
        body, name="attn_fwd", grid=(T // TQ,),
        in_specs=[pl.BlockSpec((TQ, AW), lambda i: (i, 0)),
                  _kv_spec(1, 2), _kv_spec(1, 1), _kv_spec(1, 0), _kv_spec(2, 2), _kv_spec(2, 1), _kv_spec(2, 0),
                  pl.BlockSpec((1, NHA, TQ, TW), lambda i: (jnp.minimum(i, 2), 0, 0, 0))],
        out_specs=_rows(TQ, AW),
        out_shape=jax.ShapeDtypeStruct((T, AW), F32),
        compiler_params=_cp(("arbitrary",)),
    )(an, an, an, an, an, an, an, tab)


def _attn_bwd(an, dout, tabt, send):
    T = an.shape[0]
    nq = T // TQ

    def qi(i):
        return jnp.minimum(i, nq - 1)

    def kv_spec(col, back):
        return pl.BlockSpec((TQ, AW), lambda i: (jnp.maximum(qi(i) - back, 0), col))

    def body(q_ref, do_ref, k2_ref, k1_ref, k0_ref, v2_ref, v1_ref, v0_ref, tabt_ref, send_ref,
             dq_ref, dk_ref, dv_ref, dtab_ref, recv_ref, dk_acc, dv_acc, send_sems, recv_sems, local_sem):
        i = pl.program_id(0)
        sc = _Scatter(send_ref, recv_ref, send_sems, recv_sems, local_sem)
        pl.when(i == 0)(sc.start)

        @pl.when(i == 0)
        def _():
            dtab_ref[...] = jnp.zeros_like(dtab_ref)

        new = i % 3
        dk_acc[new] = jnp.zeros((TQ, AW), F32)
        dv_acc[new] = jnp.zeros((TQ, AW), F32)

        @pl.when(i < nq)
        def _():
            kwin = jnp.concatenate([k2_ref[...], k1_ref[...], k0_ref[...]], axis=0)
            vwin = jnp.concatenate([v2_ref[...], v1_ref[...], v0_ref[...]], axis=0)
            q = q_ref[...]
            do = do_ref[...].astype(_MXU)
            lo_half = _iota((TQ, LANES), 1) < DHA

            def front(h):
                sl = slice(LANES * (h // 2), LANES * (h // 2 + 1))
                mask = lo_half if h % 2 == 0 else jnp.logical_not(lo_half)
                zero = jnp.zeros((TQ, LANES), q.dtype)
                qm = jnp.where(mask, q[:, sl], zero)
                dom = jnp.where(mask, do[:, sl], zero)
                st = _dot_nt(kwin[:, sl], qm) + tabt_ref[0, h]
                return st, _dot_nt(vwin[:, sl], dom), qm, dom, mask

            pairs = {}

            def back(h, ptb, dsb, qm, dom, mask):
                sl = slice(LANES * (h // 2), LANES * (h // 2 + 1))
                dv = _dot(ptb, dom)
                dk = _dot(dsb, qm)
                dq = jnp.where(mask, _dot_tn(dsb, kwin[:, sl]), 0.0)
                if h % 2 == 0:
                    pairs[h // 2] = (dq, dk, dv)
                    return
                dq0, dk0, dv0 = pairs.pop(h // 2)
                dq_ref[:, sl] = dq0 + dq
                dk_pair, dv_pair = dk0 + dk, dv0 + dv
                for w in range(3):
                    slot = (i + 1 + w) % 3
                    rows = slice(TQ * w, TQ * (w + 1))
                    dk_acc[slot, :, sl] += dk_pair[rows]
                    dv_acc[slot, :, sl] += dv_pair[rows]

            nxt = front(0)
            pending = None
            for h in range(NHA):
                st, dpt, qm, dom, mask = nxt
                if h + 1 < NHA:
                    nxt = front(h + 1)
                m = jnp.max(st, axis=0, keepdims=True)
                e = jnp.exp(st - m)
                pt = e * (1.0 / jnp.sum(e, axis=0, keepdims=True))
                delta = jnp.sum(pt * dpt, axis=0, keepdims=True)
                dst = pt * (dpt - delta)
                dtab_ref[h] += dst
                if pending is not None:
                    back(*pending)
                pending = (h, pt.astype(_MXU), dst.astype(_MXU), qm, dom, mask)
            back(*pending)

        @pl.when(i >= 2)
        def _():
            done = (i + 1) % 3
            dk_ref[...] = dk_acc[done]
            dv_ref[...] = dv_acc[done]

        pl.when(i == nq + 1)(sc.finish)

    back2 = pl.BlockSpec((TQ, AW), lambda i: (jnp.maximum(i - 2, 0), 0))
    anyspec = pl.BlockSpec(memory_space=pl.ANY)
    return pl.pallas_call(
        body, name="attn_bwd", grid=(nq + 2,),
        in_specs=[pl.BlockSpec((TQ, AW), lambda i: (qi(i), 0)), pl.BlockSpec((TQ, AW), lambda i: (qi(i), 0)),
                  kv_spec(1, 2), kv_spec(1, 1), kv_spec(1, 0), kv_spec(2, 2), kv_spec(2, 1), kv_spec(2, 0),
                  pl.BlockSpec((1, NHA, TW, TQ), lambda i: (jnp.minimum(i, 2), 0, 0, 0)), anyspec],
        out_specs=[pl.BlockSpec((TQ, AW), lambda i: (qi(i), 0)), back2, back2, _full((NHA, TW, TQ)), anyspec],
        out_shape=[jax.ShapeDtypeStruct((T, AW), F32), jax.ShapeDtypeStruct((T, AW), F32),
                   jax.ShapeDtypeStruct((T, AW), F32), jax.ShapeDtypeStruct((NHA, TW, TQ), F32),
                   jax.ShapeDtypeStruct(send.shape, send.dtype)],
        scratch_shapes=[pltpu.VMEM((3, TQ, AW), F32), pltpu.VMEM((3, TQ, AW), F32)] + _COMM_SEMS,
        compiler_params=_cp(("arbitrary",)),
    )(an, dout, an, an, an, an, an, an, tabt, send)


GR = 128
NG = TG // GR
CPT = TG // CH
CONV_K = 4


def _split3(x):
    a = x.astype(BF16)
    r = x - a.astype(F32)
    b = r.astype(BF16)
    c = (r - b.astype(F32)).astype(BF16)
    return a, b, c


def _ones_dot(ones_b, x):
    return sum(jnp.dot(ones_b, t, preferred_element_type=F32) for t in _split3(x))


def _dot_ones_nt(x, ones_b):
    dn = (((1,), (1,)), ((), ()))
    return sum(lax.dot_general(t, ones_b, dn, preferred_element_type=F32) for t in _split3(x))


def _dn_masks():
    r, c = _iota((GR, GR), 0), _iota((GR, GR), 1)
    same = (r >> 6) == (c >> 6)
    one = lambda m: jnp.where(m, 1.0, 0.0).astype(BF16)
    return dict(
        tril=same & (c <= r), strict=same & (c < r), triu=same & (c >= r), strict_u=same & (c > r),
        tril_b=one(same & (c <= r)), triu_b=one(same & (c >= r)), blk_b=one(same), eye_b=one(r == c),
        eye=jnp.where(r == c, 1.0, 0.0).astype(F32),
        fold_b=one((_iota((GR, CH), 0) & (CH - 1)) == _iota((GR, CH), 1)),
        last=(_iota((GR, 1), 0) & (CH - 1)) == CH - 1,
    )


def _shift_down(x, halo, k):
    if k == 0:
        return x
    xs = pltpu.roll(x, k, 0)
    hs = pltpu.roll(halo, k, 0)
    top = jnp.where(_iota(halo.shape, 0) < k, hs, xs[0:8])
    return jnp.concatenate([top, xs[8:]], axis=0)


def _shift_up(x, halo, k):
    if k == 0:
        return x
    n = x.shape[0]
    xs = pltpu.roll(x, n - k, 0)
    hs = pltpu.roll(halo, 8 - k, 0)
    bot = jnp.where(_iota(halo.shape, 0) >= 8 - k, hs, xs[n - 8:n])
    return jnp.concatenate([xs[0:n - 8], bot], axis=0)


def _conv(x, halo, w):
    y = x * w[CONV_K - 1:CONV_K, :]
    for k in range(1, CONV_K):
        y = y + _shift_down(x, halo, k) * w[CONV_K - 1 - k:CONV_K - k, :]
    return y


def _tri_inv(lmats, eye):
    ps = [-m for m in lmats]
    rs = [eye + p for p in ps]
    for _ in range(5):
        ps = [_dot(p, p) for p in ps]
        rs = [r + _dot(r, p) for r, p in zip(rs, ps)]
    return rs


def _gate_terms(ba_g, bat_g, alog8, dtb8, alog8t, dtb8t, K):
    g8 = -jnp.exp(alog8) * _softplus(ba_g + dtb8)
    g8t = -jnp.exp(alog8t) * _softplus(bat_g + dtb8t)
    gc8 = _ones_dot(K["tril_b"], g8)
    gl8 = _ones_dot(K["blk_b"], g8)
    gcrow8 = _dot_ones_nt(g8t, K["tril_b"])
    return g8, gc8, gl8, gcrow8


def _dn_heads(c_tile, rows, beta8, gc8, gl8, gcrow8, K, pre=None):
    return _dn_heads_groups(c_tile, [(rows, beta8, gc8, gl8, gcrow8)], K, None if pre is None else [pre])[0]


def _dn_heads_groups(c_tile, groups, K, pres=None):
    ds = [_dn_head_vec(c_tile, rows, h, beta8, gc8, gl8, gcrow8, K)
          for rows, beta8, gc8, gl8, gcrow8 in groups for h in range(NHD)]
    pls = [_dot_nt(d["kb"], d["kn"]) for d in ds]
    pms = [_dot_nt(d["qn"], d["kn"]) for d in ds]
    for d, pl_, pm in zip(ds, pls, pms):
        d.update(pl=pl_, pm=pm, lmat=jnp.where(K["strict"], pl_ * d["gam_m"], 0.0), mm=pm * d["gam_m"])
    if pres is None:
        for d, tm in zip(ds, _tri_inv([d["lmat"] for d in ds], K["eye"])):
            d.update(tm=tm, u=_dot(tm, d["vb"]), w=_dot(tm, d["kg"]))
    else:
        for d, (tm, u, w) in zip(ds, [p for pre in pres for p in pre]):
            d.update(tm=tm, u=u, w=w)
    return [ds[NHD * k:NHD * (k + 1)] for k in range(len(groups))]


def _dn_head_vec(c_tile, rows, h, beta8, gc8, gl8, gcrow8, K):
    qr = c_tile[rows, DHD * h:DHD * (h + 1)]
    kr = c_tile[rows, DW + DHD * h:DW + DHD * (h + 1)]
    v = c_tile[rows, 2 * DW + DHD * h:2 * DW + DHD * (h + 1)]
    rq = lax.rsqrt(jnp.sum(qr * qr, axis=1, keepdims=True) + EPS)
    rk = lax.rsqrt(jnp.sum(kr * kr, axis=1, keepdims=True) + EPS)
    qh, kn = qr * rq, kr * rk
    qn = qh * (DHD ** -0.5)
    beta = _col(beta8, h)
    gccol, glcol, gcrow = _col(gc8, NHD + h), _col(gl8, NHD + h), _row(gcrow8, NHD + h)
    diff = gccol - gcrow
    gam_m = jnp.exp(jnp.where(K["tril"], diff, NEG))
    gam = jnp.exp(gccol)
    egl = jnp.exp(glcol - gccol)
    kb, vb = kn * beta, v * beta
    kg = kb * gam
    return dict(qr=qr, kr=kr, v=v, rq=rq, rk=rk, qh=qh, qn=qn, kn=kn, beta=beta, diff=diff, gam_m=gam_m, gam=gam,
                egl=egl, el=jnp.exp(glcol), kb=kb, vb=vb, kg=kg, qd=qn * gam, kd=kn * egl)


def _halo_prev(width):
    return pl.BlockSpec((8, width), lambda i: (jnp.maximum(i * (TG // 8) - 1, 0), 0))


def _dn_prep(draw, conv_w, ba, bat, alog8, dtb8, alog8t, dtb8t):
    T = draw.shape[0]
    nb = T // TG
    hm = lambda w, dt: jax.ShapeDtypeStruct((NHD, T, w), dt)
    hm_spec = lambda w: pl.BlockSpec((NHD, TG, w), lambda i: (0, i, 0))
    pc = lambda r, c: jax.ShapeDtypeStruct((NHD, T // CH, r, c), _MXU)
    pc_spec = lambda r, c: pl.BlockSpec((NHD, CPT, r, c), lambda i: (0, i, 0, 0))

    def body(x_ref, halo_ref, cw_ref, ba_ref, bat_ref, al_ref, dt_ref, alt_ref, dtt_ref,
             u_ref, w_ref, kd_ref, tm_ref, wq_ref, km_ref, mq_ref, wt_ref, elb_ref, cv_ref):
        i = pl.program_id(0)
        K = _dn_masks()
        halo = jnp.where(i > 0, halo_ref[...], 0.0)
        cv = _conv(x_ref[...], halo, cw_ref[...])
        cv_ref[...] = cv
        c_tile = cv * _sigmoid(cv)
        eye128 = jnp.where(_iota((DHD, DHD), 0) == _iota((DHD, DHD), 1), 1.0, 0.0).astype(_MXU)
        def gate_inputs(g):
            rows = slice(GR * g, GR * (g + 1))
            ba_g = ba_ref[rows, :]
            _, gc8, gl8, gcrow8 = _gate_terms(ba_g, bat_ref[:, rows], al_ref[...], dt_ref[...], alt_ref[...],
                                              dtt_ref[...], K)
            return rows, _sigmoid(ba_g), gc8, gl8, gcrow8

        def store(g, rows, ds):
            mmts = [_dot_nt(d["kn"], d["qn"]) * jnp.exp(jnp.where(K["triu"], -d["diff"], NEG)) for d in ds]
            mcs = [_dot(d["mm"], K["fold_b"]) for d in ds]
            mcts = [_dot(m, K["fold_b"]) for m in mmts]
            for h, d in enumerate(ds):
                tm_ref[h, rows, :] = d["tm"].astype(_MXU)
                u_ref[h, rows, :] = d["u"]
                w_ref[h, rows, :] = d["w"].astype(_MXU)
                kd_ref[h, rows, :] = d["kd"].astype(_MXU)
                elb = jnp.broadcast_to(d["el"], (GR, DHD))
                for cc in range(GR // CH):
                    ch = slice(CH * cc, CH * (cc + 1))
                    n = (GR // CH) * g + cc
                    wq_ref[h, n, 0:CH, :] = d["w"][ch].astype(_MXU)
                    wq_ref[h, n, CH:2 * CH, :] = d["qd"][ch].astype(_MXU)
                    km_ref[h, n, 0:DHD, :] = _dot_nt(eye128, d["kd"][ch]).astype(_MXU)
                    km_ref[h, n, DHD:DHD + CH, :] = mcs[h][ch].astype(_MXU)
                    mq_ref[h, n, 0:CH, :] = mcts[h][ch].astype(_MXU)
                    mq_ref[h, n, CH:CH + DHD, :] = _dot_nt(eye128, d["qd"][ch]).astype(_MXU)
                    wt_ref[h, n] = _dot_nt(eye128, d["w"][ch]).astype(_MXU)
                    elb_ref[n:n + 1, DHD * h:DHD * (h + 1)] = elb[CH * cc:CH * cc + 1, :]

        PAIR = 4
        for g0 in range(0, NG, PAIR):
            pair = [gate_inputs(g) for g in range(g0, g0 + PAIR)]
            for k, ds in enumerate(_dn_heads_groups(c_tile, pair, K)):
                store(g0 + k, pair[k][0], ds)

    return pl.pallas_call(
        body, name="dn_prep", grid=(nb,),
        in_specs=[_rows(TG, 1536), _halo_prev(1536), _full((CONV_K, 1536)), _rows(TG, 8),
                  pl.BlockSpec((8, TG), lambda i: (0, i)), _full((1, 8)), _full((1, 8)), _full((8, 1)), _full((8, 1))],
        out_specs=[hm_spec(DHD), hm_spec(DHD), hm_spec(DHD), hm_spec(GR),
                   pc_spec(2 * CH, DHD), pc_spec(DHD + CH, CH), pc_spec(CH + DHD, CH), pc_spec(DHD, CH),
                   pl.BlockSpec((CPT, NHD * DHD), lambda i: (i, 0)), _rows(TG, 1536)],
        out_shape=[hm(DHD, F32), hm(DHD, _MXU), hm(DHD, _MXU), hm(GR, _MXU),
                   pc(2 * CH, DHD), pc(DHD + CH, CH), pc(CH + DHD, CH), pc(DHD, CH),
                   jax.ShapeDtypeStruct((T // CH, NHD * DHD), F32), jax.ShapeDtypeStruct((T, 1536), F32)],
        compiler_params=_cp(("arbitrary",)),
    )(draw, draw, conv_w, ba, bat, alog8, dtb8, alog8t, dtb8t)


def _dn_scan(u, wq, km, elb):
    T = u.shape[1]
    nb = T // TG
    hm_spec = lambda wd: pl.BlockSpec((NHD, TG, wd), lambda i: (0, i, 0))

    def body(u_ref, wq_ref, km_ref, elb_ref, o_ref, vn_ref, sn_ref, S):
        @pl.when(pl.program_id(0) == 0)
        def _():
            S[...] = jnp.zeros_like(S)

        sub8 = _iota((CPT, DHD), 0)
        heads = range(NHD)

        def chunk(cc, carry):
            rs = pl.ds(pl.multiple_of(cc * CH, CH), CH)
            sh = [S[h] for h in heads]
            sb = [s.astype(_MXU) for s in sh]
            r1 = [_dot(wq_ref[h, cc], sb[h]) for h in heads]
            vnb = [(u_ref[h, rs, :] - r1[h][0:CH]).astype(_MXU) for h in heads]
            r2 = [_dot(km_ref[h, cc], vnb[h]) for h in heads]
            for h in heads:
                el = jnp.sum(jnp.where(sub8 == cc, elb_ref[:, DHD * h:DHD * (h + 1)], 0.0), axis=0, keepdims=True)
                S[h] = sh[h] * el + r2[h][0:DHD]
                sn_ref[cc, h] = sb[h]
                vn_ref[h, rs, :] = vnb[h]
                o_ref[h, rs, :] = r1[h][CH:2 * CH] + r2[h][DHD:DHD + CH]
            return carry

        lax.fori_loop(0, CPT, chunk, 0)

    return pl.pallas_call(
        body, name="dn_scan", grid=(nb,),
        in_specs=[hm_spec(DHD), pl.BlockSpec((NHD, CPT, 2 * CH, DHD), lambda i: (0, i, 0, 0)),
                  pl.BlockSpec((NHD, CPT, DHD + CH, CH), lambda i: (0, i, 0, 0)),
                  pl.BlockSpec((CPT, NHD * DHD), lambda i: (i, 0))],
        out_specs=[hm_spec(DHD), hm_spec(DHD), pl.BlockSpec((CPT, NHD, DHD, DHD), lambda i: (i, 0, 0, 0))],
        out_shape=[jax.ShapeDtypeStruct((NHD, T, DHD), F32), jax.ShapeDtypeStruct((NHD, T, DHD), _MXU),
                   jax.ShapeDtypeStruct((T // CH, NHD, DHD, DHD), _MXU)],
        scratch_shapes=[pltpu.VMEM((NHD, DHD, DHD), F32)],
        compiler_params=_cp(("arbitrary",)),
    )(u, wq, km, elb)


def _dn_scan_bwd(do, mq, kd, wt, sn, vn, elb):
    T = do.shape[1]
    nb = T // TG
    rev = lambda wd: pl.BlockSpec((NHD, TG, wd), lambda i: (0, nb - 1 - i, 0))
    rev_t = lambda r: pl.BlockSpec((NHD, CPT, r, CH), lambda i: (0, nb - 1 - i, 0, 0))

    def body(do_ref, mq_ref, kd_ref, wt_ref, sn_ref, vn_ref, elb_ref,
             du_ref, dw_ref, dqd_ref, dkd_ref, dgx_ref, dS):
        @pl.when(pl.program_id(0) == 0)
        def _():
            dS[...] = jnp.zeros_like(dS)

        last_row = _iota((CH, DHD), 0) == CH - 1
        sub8 = _iota((CPT, DHD), 0)
        heads = range(NHD)

        def chunk(k, carry):
            cc = CPT - 1 - k
            rs = pl.ds(pl.multiple_of(cc * CH, CH), CH)
            dsh = [dS[h] for h in heads]
            dsb = [d.astype(_MXU) for d in dsh]
            doc = [do_ref[h, rs, :].astype(_MXU) for h in heads]
            a = [_dot(mq_ref[h, cc], doc[h]) for h in heads]
            b = [_dot(kd_ref[h, rs, :], dsb[h]) for h in heads]
            dvn = [a[h][0:CH] + b[h] for h in heads]
            dvnb = [d.astype(_MXU) for d in dvn]
            e = [_dot(wt_ref[h, cc], dvnb[h]) for h in heads]
            for h in heads:
                el = jnp.sum(jnp.where(sub8 == cc, elb_ref[:, DHD * h:DHD * (h + 1)], 0.0), axis=0, keepdims=True)
                sn = sn_ref[cc, h]
                dS[h] = a[h][CH:CH + DHD] + dsh[h] * el - e[h]
                du_ref[h, rs, :] = dvn[h]
                c = _dot_nt(jnp.concatenate([doc[h], dvnb[h]], axis=0), sn)
                dqd_ref[h, rs, :] = c[0:CH]
                dw_ref[h, rs, :] = -c[CH:2 * CH]
                dkd_ref[h, rs, :] = _dot_nt(vn_ref[h, rs, :], dsb[h])
                part = jnp.sum(dsh[h] * sn.astype(F32), axis=0, keepdims=True) * el
                dgx_ref[h, rs, :] = jnp.where(last_row, part, 0.0)
            return carry

        lax.fori_loop(0, CPT, chunk, 0)

    o = jax.ShapeDtypeStruct((NHD, T, DHD), F32)
    return pl.pallas_call(
        body, name="dn_scan_bwd", grid=(nb,),
        in_specs=[rev(DHD), rev_t(CH + DHD), rev(DHD), rev_t(DHD),
                  pl.BlockSpec((CPT, NHD, DHD, DHD), lambda i: (nb - 1 - i, 0, 0, 0)), rev(DHD),
                  pl.BlockSpec((CPT, NHD * DHD), lambda i: (nb - 1 - i, 0))],
        out_specs=[rev(DHD)] * 5,
        out_shape=[o] * 5,
        scratch_shapes=[pltpu.VMEM((NHD, DHD, DHD), F32)],
        compiler_params=_cp(("arbitrary",)),
    )(do, mq, kd, wt, sn, vn, elb)


def _put_col(acc, k, col):
    return jnp.where(_iota(acc.shape, 1) == k, col, acc)


def _dn_post_bwd(draw, cv, conv_w, ba, bat, alog8, dtb8, alog8t, dtb8t, du, dw, dqd, dkd, dgx, do, vn, tm, u, w):
    T = draw.shape[0]
    nb = T // TG
    hm_spec = lambda wd: pl.BlockSpec((NHD, TG, wd), lambda i: (0, nb - 1 - i, 0))
    rrows = lambda w: pl.BlockSpec((TG, w), lambda i: (nb - 1 - i, 0))

    def body(x_ref, cv_ref, cw_ref, ba_ref, bat_ref, al_ref, dt_ref, alt_ref, dtt_ref,
             du_ref, dw_ref, dqd_ref, dkd_ref, dgx_ref, do_ref, vn_ref, tm_ref, u_ref, w_ref,
             dx_ref, dba_ref, sm_ref, dcw_ref, dc_ref, nxt_ref):
        i = pl.program_id(0)

        @pl.when(i == 0)
        def _():
            sm_ref[...] = jnp.zeros_like(sm_ref)
            dcw_ref[...] = jnp.zeros_like(dcw_ref)
            nxt_ref[...] = jnp.zeros_like(nxt_ref)

        K = _dn_masks()
        cv = cv_ref[...]
        sg = _sigmoid(cv)
        c_tile = cv * sg
        dsilu = sg * (1.0 + cv * (1.0 - sg))
        for g in range(NG):
            rows = slice(GR * g, GR * (g + 1))
            ba_g = ba_ref[rows, :]
            g8, gc8, gl8, gcrow8 = _gate_terms(ba_g, bat_ref[:, rows], al_ref[...], dt_ref[...], alt_ref[...],
                                               dtt_ref[...], K)
            beta8 = _sigmoid(ba_g)
            dgc8 = jnp.zeros((GR, 8), F32)
            rd8 = jnp.zeros((GR, 8), F32)
            dbeta8 = jnp.zeros((GR, 8), F32)
            pre = [(tm_ref[h, rows, :], u_ref[h, rows, :], w_ref[h, rows, :]) for h in range(NHD)]
            ds = _dn_heads(c_tile, rows, beta8, gc8, gl8, gcrow8, K, pre)
            H = range(NHD)
            eye_b = K["eye_b"].astype(_MXU)
            gam_t = [jnp.exp(jnp.where(K["triu"], -d["diff"], NEG)) for d in ds]
            doh = [do_ref[h, rows, :] for h in H]
            vnh = [vn_ref[h, rows, :] for h in H]
            tt = [_dot_nt(eye_b, d["tm"]) for d in ds]
            dvb = [_dot(tt[h], du_ref[h, rows, :]) for h in H]
            dkg = [_dot(tt[h], dw_ref[h, rows, :]) for h in H]
            plt = [_dot_nt(d["kn"], d["kb"]) for d in ds]
            pmt = [_dot_nt(d["kn"], d["qn"]) for d in ds]
            da = [-(_dot_nt(dvb[h], ds[h]["u"]) + _dot_nt(dkg[h], ds[h]["w"])) for h in H]
            dat = [-(_dot_nt(ds[h]["u"], dvb[h]) + _dot_nt(ds[h]["w"], dkg[h])) for h in H]
            dpm = [jnp.where(K["tril"], _dot_nt(doh[h], vnh[h]), 0.0) * ds[h]["gam_m"] for h in H]
            dpmt = [jnp.where(K["triu"], _dot_nt(vnh[h], doh[h]), 0.0) * gam_t[h] for h in H]
            dpl = [jnp.where(K["strict"], da[h], 0.0) * ds[h]["gam_m"] for h in H]
            dplt = [jnp.where(K["strict_u"], dat[h], 0.0) * gam_t[h] for h in H]
            dkb = [_dot(dpl[h], ds[h]["kn"]) + dkg[h] * ds[h]["gam"] for h in H]
            dqn = [_dot(dpm[h], ds[h]["kn"]) + dqd_ref[h, rows, :] * ds[h]["gam"] for h in H]
            dknm = [_dot(dplt[h], ds[h]["kb"]) + _dot(dpmt[h], ds[h]["qn"]) for h in H]
            for h, d in enumerate(ds):
                kn, dqdh, dkdh = d["kn"], dqd_ref[h, rows, :], dkd_ref[h, rows, :]
                dkn = dknm[h] + dkdh * d["egl"] + dkb[h] * d["beta"]
                dkd_kd = dkdh * d["kd"]
                rd = jnp.sum(dkd_kd, axis=1, keepdims=True)
                dgc = jnp.sum(dpl[h] * d["pl"] + dpm[h] * d["pm"] - dplt[h] * plt[h] - dpmt[h] * pmt[h]
                              + dqdh * d["qd"] + dkg[h] * d["kg"] - dkd_kd + dgx_ref[h, rows, :],
                              axis=1, keepdims=True)
                dgc8 = _put_col(dgc8, NHD + h, dgc)
                rd8 = _put_col(rd8, NHD + h, rd)
                dbeta = jnp.sum(dkb[h] * kn + dvb[h] * d["v"], axis=1, keepdims=True)
                dbeta8 = _put_col(dbeta8, h, dbeta)
                dqh = dqn[h] * (DHD ** -0.5)
                qh = d["qh"]
                dqr = d["rq"] * (dqh - qh * jnp.sum(dqh * qh, axis=1, keepdims=True))
                dkr = d["rk"] * (dkn - kn * jnp.sum(dkn * kn, axis=1, keepdims=True))
                cq = slice(DHD * h, DHD * (h + 1))
                ck = slice(DW + DHD * h, DW + DHD * (h + 1))
                cvv = slice(2 * DW + DHD * h, 2 * DW + DHD * (h + 1))
                dc_ref[rows, cq] = dqr * dsilu[rows, cq]
                dc_ref[rows, ck] = dkr * dsilu[rows, ck]
                dc_ref[rows, cvv] = dvb[h] * d["beta"] * dsilu[rows, cvv]
            dgc8 = dgc8 + jnp.where(K["last"], _ones_dot(K["blk_b"], rd8), 0.0)
            dg8 = _ones_dot(K["triu_b"], dgc8)
            sgm = _sigmoid(ba_g + dt_ref[...])
            dalpha = dg8 * (-jnp.exp(al_ref[...])) * sgm
            lane8 = _iota((GR, 8), 1)
            dba_ref[rows, :] = jnp.where(lane8 < NHD, dbeta8 * beta8 * (1.0 - beta8), dalpha)
            valid = lane8 >= NHD
            sm_ref[0:1, 0:8] += jnp.sum(jnp.where(valid, dg8 * g8, 0.0), axis=0, keepdims=True)
            sm_ref[1:2, 0:8] += jnp.sum(jnp.where(valid, dalpha, 0.0), axis=0, keepdims=True)

        dcv = dc_ref[...]
        xv = x_ref[...]
        nxt = nxt_ref[...]
        w = cw_ref[...]
        dx = dcv * w[CONV_K - 1:CONV_K, :]
        dcw_ref[CONV_K - 1:CONV_K, :] += jnp.sum(dcv * xv, axis=0, keepdims=True)
        for k in range(1, CONV_K):
            j = CONV_K - 1 - k
            up = _shift_up(dcv, nxt, k)
            dx = dx + up * w[j:j + 1, :]
            dcw_ref[j:j + 1, :] += jnp.sum(up * xv, axis=0, keepdims=True)
        dx_ref[...] = dx
        nxt_ref[...] = dcv[0:8]

    return pl.pallas_call(
        body, name="dn_post_bwd", grid=(nb,),
        in_specs=[rrows(1536), rrows(1536), _full((CONV_K, 1536)), rrows(8),
                  pl.BlockSpec((8, TG), lambda i: (0, nb - 1 - i)), _full((1, 8)), _full((1, 8)), _full((8, 1)),
                  _full((8, 1)),
                  hm_spec(DHD), hm_spec(DHD), hm_spec(DHD), hm_spec(DHD), hm_spec(DHD), hm_spec(DHD), hm_spec(DHD),
                  hm_spec(GR), hm_spec(DHD), hm_spec(DHD)],
        out_specs=[rrows(1536), rrows(8), _full((8, LANES)), _full((8, 1536))],
        out_shape=[jax.ShapeDtypeStruct((T, 1536), F32), jax.ShapeDtypeStruct((T, 8), F32),
                   jax.ShapeDtypeStruct((8, LANES), F32), jax.ShapeDtypeStruct((8, 1536), F32)],
        scratch_shapes=[pltpu.VMEM((TG, 1536), F32), pltpu.VMEM((8, 1536), F32)],
        compiler_params=_cp(("arbitrary",)),
    )(draw, cv, conv_w, ba, bat, alog8, dtb8, alog8t, dtb8t, du, dw, dqd, dkd, dgx, do, vn, tm, u, w)


def _rms(x):
    return lax.rsqrt(jnp.mean(x * x, axis=1, keepdims=True) + EPS)


def _rms_bwd(dy, xh, r, g):
    dxh = dy * g
    return r * (dxh - xh * jnp.mean(dxh * xh, axis=1, keepdims=True))


def _hm_rows(tm):
    return pl.BlockSpec((NHD, tm, DHD), lambda i: (0, i, 0))


def _post_mix(apre, o, z, x, w_out, g_a, g_dn):
    T = x.shape[0]

    def body(ap_ref, o_ref, z_ref, x_ref, w_ref, ga_ref, gd_ref, x1_ref, mix_ref):
        ap = ap_ref[...]
        parts = [ap * _rms(ap) * ga_ref[...]]
        zz = z_ref[...]
        for h in range(NHD):
            oh = o_ref[h]
            zh = zz[:, DHD * h:DHD * (h + 1)]
            parts.append(oh * _rms(oh) * gd_ref[...] * (zh * _sigmoid(zh)))
        mix = jnp.concatenate(parts, axis=1).astype(_MXU)
        mix_ref[...] = mix
        x1_ref[...] = x_ref[...] + jnp.dot(mix, w_ref[...], preferred_element_type=F32)

    return pl.pallas_call(
        body, name="post_mix", grid=(T // TM,),
        in_specs=[_rows(TM, AW), _hm_rows(TM), _rows(TM, DW), _rows(TM, D), _full((D, D)), _full((1, AW)),
                  _full((1, DHD))],
        out_specs=[_rows(TM, D), _rows(TM, D)],
        out_shape=[jax.ShapeDtypeStruct((T, D), F32), jax.ShapeDtypeStruct((T, D), _MXU)],
        compiler_params=_cp(("arbitrary",)),
    )(apre, o, z, x, w_out, g_a, g_dn)


def _ffn(x1, tgt, wl_all, g_ffn):
    T = x1.shape[0]
    SH = FF // N_DEV
    nt = (((1,), (1,)), ((), ()))

    def body(x_ref, t_ref, wl_hbm, g_ref,
             dx1_ref, dx1b_ref, h2_ref, act_ref, dgu_ref, dyb_ref, loss_ref, dg_ref, wg, wu, wd, sem):
        @pl.when(pl.program_id(0) == 0)
        def _():
            cps = [pltpu.make_async_copy(wl_hbm.at[dev, pl.ds(128 + SH * k, SH), :], dst.at[pl.ds(SH * dev, SH), :],
                                         sem.at[N_DEV * k + dev])
                   for k, dst in enumerate((wg, wu, wd)) for dev in range(N_DEV)]
            for cp in cps:
                cp.start()
            for cp in cps:
                cp.wait()
            loss_ref[...] = jnp.zeros_like(loss_ref)
            dg_ref[...] = jnp.zeros_like(dg_ref)

        xv = x_ref[...]
        r = _rms(xv)
        xh = xv * r
        gg = g_ref[...]
        h2 = (xh * gg).astype(_MXU)
        h2_ref[...] = h2
        gate = lax.dot_general(h2, wg[...], nt, preferred_element_type=F32)
        up = lax.dot_general(h2, wu[...], nt, preferred_element_type=F32)
        sg = _sigmoid(gate)
        silu = gate * sg
        act = (silu * up).astype(_MXU)
        act_ref[...] = act
        y = xv + jnp.dot(act, wd[...], preferred_element_type=F32)
        err = y - t_ref[...]
        loss_ref[...] += jnp.sum(err * err, axis=0, keepdims=True)
        dy = err * (1.0 / D)
        dyb = dy.astype(_MXU)
        dyb_ref[...] = dyb
        dact = lax.dot_general(dyb, wd[...], nt, preferred_element_type=F32)
        dgate = (dact * up * (sg * (1.0 + gate * (1.0 - sg)))).astype(_MXU)
        dup = (dact * silu).astype(_MXU)
        dgu_ref[:, 0:FF] = dgate
        dgu_ref[:, FF:2 * FF] = dup
        dh2 = (jnp.dot(dgate, wg[...], preferred_element_type=F32)
               + jnp.dot(dup, wu[...], preferred_element_type=F32))
        dg_ref[...] += jnp.sum(dh2 * xh, axis=0, keepdims=True)
        dx1 = dy + _rms_bwd(dh2, xh, r, gg)
        dx1_ref[...] = dx1
        dx1b_ref[...] = dx1.astype(_MXU)

    anyspec = pl.BlockSpec(memory_space=pl.ANY)
    sd = lambda w, dt: jax.ShapeDtypeStruct((T, w), dt)
    return pl.pallas_call(
        body, name="ffn", grid=(T // TF,),
        in_specs=[_rows(TF, D), _rows(TF, D), anyspec, _full((1, D))],
        out_specs=[_rows(TF, D), _rows(TF, D), _rows(TF, D), _rows(TF, FF), _rows(TF, 2 * FF), _rows(TF, D),
                   _full((1, D)), _full((1, D))],
        out_shape=[sd(D, F32), sd(D, _MXU), sd(D, _MXU), sd(FF, _MXU), sd(2 * FF, _MXU), sd(D, _MXU),
                   jax.ShapeDtypeStruct((1, D), F32), jax.ShapeDtypeStruct((1, D), F32)],
        scratch_shapes=[pltpu.VMEM((FF, D), _MXU)] * 3 + [pltpu.SemaphoreType.DMA((3 * N_DEV,))],
        compiler_params=_cp(("arbitrary",)),
    )(x1, tgt, wl_all, g_ffn)


def _mix_bwd(dx1b, w_out, apre, o, z, g_a, g_dn):
    T = dx1b.shape[0]

    def body(dx_ref, w_ref, ap_ref, o_ref, z_ref, ga_ref, gd_ref, dap_ref, do_ref, dz_ref, dga_ref, dgd_ref):
        @pl.when(pl.program_id(0) == 0)
        def _():
            dga_ref[...] = jnp.zeros_like(dga_ref)
            dgd_ref[...] = jnp.zeros_like(dgd_ref)

        dmix = lax.dot_general(dx_ref[...], w_ref[...], (((1,), (1,)), ((), ())), preferred_element_type=F32)
        ap = ap_ref[...]
        ra = _rms(ap)
        ah = ap * ra
        da = dmix[:, 0:AW]
        dga_ref[...] += jnp.sum(da * ah, axis=0, keepdims=True)
        dap_ref[...] = _rms_bwd(da, ah, ra, ga_ref[...])
        zz = z_ref[...]
        gd = gd_ref[...]
        for h in range(NHD):
            cs = slice(DHD * h, DHD * (h + 1))
            dd = dmix[:, AW + DHD * h:AW + DHD * (h + 1)]
            oh = o_ref[h]
            ro = _rms(oh)
            ohh = oh * ro
            zh = zz[:, cs]
            sz = _sigmoid(zh)
            dz_ref[:, cs] = dd * (ohh * gd) * (sz * (1.0 + zh * (1.0 - sz)))
            don = dd * (zh * sz)
            dgd_ref[...] += jnp.sum(don * ohh, axis=0, keepdims=True)
            do_ref[h] = _rms_bwd(don, ohh, ro, gd)

    return pl.pallas_call(
        body, name="mix_bwd", grid=(T // TM,),
        in_specs=[_rows(TM, D), _full((D, D)), _rows(TM, AW), _hm_rows(TM), _rows(TM, DW), _full((1, AW)),
                  _full((1, DHD))],
        out_specs=[_rows(TM, AW), _hm_rows(TM), _rows(TM, DW), _full((1, AW)), _full((1, DHD))],
        out_shape=[jax.ShapeDtypeStruct((T, AW), F32), jax.ShapeDtypeStruct((NHD, T, DHD), F32),
                   jax.ShapeDtypeStruct((T, DW), F32), jax.ShapeDtypeStruct((1, AW), F32),
                   jax.ShapeDtypeStruct((1, DHD), F32)],
        compiler_params=_cp(("arbitrary",)),
    )(dx1b, w_out, apre, o, z, g_a, g_dn)


DPW = 3712


def _inproj_bwd(dqn, dkn, dv, araw, ddraw, dz, dba, x, dx1, w_int, w_ba, g_mix, qg_t, kg_t):
    T = x.shape[0]

    def body(dqn_ref, dkn_ref, dv_ref, ar_ref, dd_ref, dz_ref, dba_ref, x_ref, dx1_ref, w_hbm, wba_ref, g_ref, qg_ref,
             kg_ref, dx_ref, dp_ref, dgm_ref, dqg_ref, dkg_ref, w_ref, w_sem):
        @pl.when(pl.program_id(0) == 0)
        def _():
            cp = pltpu.make_async_copy(w_hbm, w_ref, w_sem)
            cp.start()
            cp.wait()
            dgm_ref[...] = jnp.zeros_like(dgm_ref)
            dqg_ref[...] = jnp.zeros_like(dqg_ref)
            dkg_ref[...] = jnp.zeros_like(dkg_ref)

        bd = _block_ones(AW // 2, DHA)

        def head_norm_bwd(raw, dyn, gain, dg_ref):
            r = _head_rms(raw, bd, DHA)
            xh = raw * r
            dg_ref[...] += jnp.sum(dyn * xh, axis=0, keepdims=True)
            dxh = dyn * gain
            return r * (dxh - xh * (_head_sum(dxh * xh, bd) * (1.0 / DHA)))

        def segment(lo, val):
            vb = val.astype(_MXU)
            dp_ref[:, lo:lo + val.shape[1]] = vb
            return jnp.dot(vb, w_ref[lo:lo + val.shape[1], :], preferred_element_type=F32)

        dh = segment(1536, dd_ref[...]) + segment(2 * AW, dv_ref[...]) + segment(3072, dz_ref[...])
        dbab = dba_ref[...].astype(_MXU)
        dp_ref[:, 3584:DPW] = jnp.zeros((TM, DPW - 3584), _MXU)
        dp_ref[:, 3584:3592] = dbab
        dh = dh + lax.dot_general(dbab, wba_ref[...], (((1,), (1,)), ((), ())), preferred_element_type=F32)
        ar = ar_ref[...]
        dq = head_norm_bwd(ar[:, 0:AW], dqn_ref[...] * (DHA ** -0.5), qg_ref[...], dqg_ref)
        dk = head_norm_bwd(ar[:, AW:2 * AW], dkn_ref[...], kg_ref[...], dkg_ref)
        dh = dh + segment(0, dq) + segment(AW, dk)
        xv = x_ref[...]
        r = _rms(xv)
        xh = xv * r
        dgm_ref[...] += jnp.sum(dh * xh, axis=0, keepdims=True)
        dx_ref[...] = dx1_ref[...] + _rms_bwd(dh, xh, r, g_ref[...])

    return pl.pallas_call(
        body, name="inproj_bwd", grid=(T // TM,),
        in_specs=[_rows(TM, AW), _rows(TM, AW), _rows(TM, AW), _rows(TM, 2 * AW), _rows(TM, 1536), _rows(TM, DW),
                  _rows(TM, 8), _rows(TM, D), _rows(TM, D), pl.BlockSpec(memory_space=pl.ANY), _full((D, 8)),
                  _full((1, D)), _full((1, AW)), _full((1, AW))],
        out_specs=[_rows(TM, D), _rows(TM, DPW), _full((1, D)), _full((1, AW)), _full((1, AW))],
        out_shape=[jax.ShapeDtypeStruct((T, D), F32), jax.ShapeDtypeStruct((T, DPW), _MXU),
                   jax.ShapeDtypeStruct((1, D), F32), jax.ShapeDtypeStruct((1, AW), F32),
                   jax.ShapeDtypeStruct((1, AW), F32)],
        scratch_shapes=[pltpu.VMEM((3584, D), _MXU), pltpu.SemaphoreType.DMA],
        compiler_params=_cp(("arbitrary",)),
    )(dqn, dkn, dv, araw, ddraw, dz, dba, x, dx1, w_int, w_ba, g_mix, qg_t, kg_t)


def _wgrad(a, b, name, tk=1024, tn=None, out_dtype=F32, transposed=False, gather=None):
    T, M = a.shape
    N = b.shape[1]
    tn = N if tn is None else tn
    assert T % tk == 0 and N % tn == 0, (T, tk, N, tn)
    nk = T // tk
    hosted = gather is not None
    assert not hosted or (tn == N and nk >= 3)

    def body(a_ref, b_ref, *rest):
        k = pl.program_id(1)
        if hosted:
            g_ref, o_ref, g_all, acc, send_sems, recv_sems, local_sem = rest
            ag = _Gather(g_ref, g_all, send_sems, recv_sems, local_sem)
            pl.when(k == 0)(ag.start)
            pl.when(k == nk // 2)(ag.forward)
        else:
            o_ref, acc = rest

        @pl.when(k == 0)
        def _():
            acc[...] = jnp.zeros_like(acc)

        acc[...] += lax.dot_general(a_ref[...], b_ref[...], (((0,), (0,)), ((), ())), preferred_element_type=F32)

        @pl.when(k == nk - 1)
        def _():
            r = acc[...]
            o_ref[...] = (r.T if transposed else r).astype(out_dtype)
            if hosted:
                ag.finish()

    if transposed:
        out_spec, out_shape = pl.BlockSpec((tn, M), lambda j, k: (j, 0)), (N, M)
    else:
        out_spec, out_shape = pl.BlockSpec((M, tn), lambda j, k: (0, j)), (M, N)
    anyspec = pl.BlockSpec(memory_space=pl.ANY)
    return pl.pallas_call(
        body, name=name, grid=(N // tn, nk),
        in_specs=[pl.BlockSpec((tk, M), lambda j, k: (k, 0)), pl.BlockSpec((tk, tn), lambda j, k: (k, j))]
        + ([anyspec] if hosted else []),
        out_specs=[out_spec, anyspec] if hosted else out_spec,
        out_shape=([jax.ShapeDtypeStruct(out_shape, out_dtype),
                    jax.ShapeDtypeStruct((N_DEV,) + gather.shape, gather.dtype)] if hosted
                   else jax.ShapeDtypeStruct(out_shape, out_dtype)),
        scratch_shapes=[pltpu.VMEM((M, tn), F32)] + (_COMM_SEMS if hosted else []),
        compiler_params=_cp(("arbitrary", "arbitrary")),
    )(*((a, b) + ((gather,) if hosted else ())))


def _adamw(parts, w, m, v, name, tr, send=None):
    K, R, W = parts.shape
    n = R // tr

    def body(p_ref, w_ref, m_ref, v_ref, *rest):
        if send is not None:
            send_ref, g_ref, d_ref, nm_ref, nv_ref, recv_ref, send_sems, recv_sems, local_sem = rest
            sc = _Scatter(send_ref, recv_ref, send_sems, recv_sems, local_sem)
            pl.when(pl.program_id(0) == 0)(sc.start)
        else:
            g_ref, d_ref, nm_ref, nv_ref = rest
        g = p_ref[0].astype(F32)
        for k in range(1, K):
            g = g + p_ref[k].astype(F32)
        g_ref[...] = g
        nm = ADAM_B1 * m_ref[...] + (1.0 - ADAM_B1) * g
        nv = ADAM_B2 * v_ref[...] + (1.0 - ADAM_B2) * (g * g)
        nm_ref[...] = nm
        nv_ref[...] = nv
        m_hat = nm / (1.0 - ADAM_B1 ** ADAM_STEP)
        v_hat = nv / (1.0 - ADAM_B2 ** ADAM_STEP)
        d_ref[...] = -ADAM_LR * (m_hat / (jnp.sqrt(v_hat) + ADAM_EPS) + ADAM_WD * w_ref[...])
        if send is not None:
            pl.when(pl.program_id(0) == n - 1)(sc.finish)

    o = jax.ShapeDtypeStruct((R, W), F32)
    anyspec = pl.BlockSpec(memory_space=pl.ANY)
    hosted = send is not None
    return pl.pallas_call(
        body, name=name, grid=(n,),
        in_specs=[pl.BlockSpec((K, tr, W), lambda i: (0, i, 0)), _rows(tr, W), _rows(tr, W), _rows(tr, W)]
        + ([anyspec] if hosted else []),
        out_specs=[_rows(tr, W)] * 4 + ([anyspec] if hosted else []),
        out_shape=[o] * 4 + ([jax.ShapeDtypeStruct(send.shape, send.dtype)] if hosted else []),
        scratch_shapes=_COMM_SEMS if hosted else [],
        compiler_params=_cp(("arbitrary",)),
    )(*((parts, w, m, v) + ((send,) if hosted else ())))


SM_ROWS = 136
R_GMIX, R_GFFN, R_QG, R_KG, R_GA, R_GDN, R_ALOG, R_DT, R_LOSS, R_CONV, R_REL = 0, 8, 16, 24, 32, 40, 48, 49, 56, 64, 112


def _small_reduce(gathered):
    def body(p_ref, o_ref):
        s = p_ref[0]
        for k in range(1, N_DEV):
            s = s + p_ref[k]
        o_ref[...] = s
        for r0 in (R_QG, R_KG):
            rs = jnp.sum(s[r0:r0 + 4], axis=0, keepdims=True)
            o_ref[r0:r0 + 1, :] = rs + pltpu.roll(rs, DHA, 1)
        tot = jnp.sum(jnp.sum(s[R_LOSS:R_LOSS + 8], axis=0, keepdims=True), axis=1, keepdims=True)
        o_ref[R_LOSS:R_LOSS + 1, :] = jnp.broadcast_to(tot * (0.5 / D), (1, LANES))

    return pl.pallas_call(
        body, name="small_reduce",
        out_shape=jax.ShapeDtypeStruct((SM_ROWS, LANES), F32),
    )(gathered)


_WIRE = jnp.bfloat16
RA_USED, RA = 449, 464
RL = 128 + 3 * 352


def _pack_rows(parts, rows=None):
    p = jnp.concatenate([t.reshape(-1, D) for t in parts], axis=0) if len(parts) > 1 else parts[0].reshape(-1, D)
    return p if rows is None else jnp.pad(p, ((0, rows - p.shape[0]), (0, 0)))


def _unpack_rows(packed, shapes):
    out, r = [], 0
    for shp in shapes:
        nr = math.prod(shp) // D
        out.append(packed[r:r + nr].reshape(shp))
        r += nr
    return out


def _pad8(t):
    return jnp.pad(t, ((0, (-t.shape[0]) % 8), (0, 0)))


PART = 8 * LANES


def _pack_lanes(parts):
    rows = []
    for p in parts:
        f = p.reshape(-1)
        rows.append(jnp.pad(f, (0, (-f.shape[0]) % PART)).reshape(-1, LANES))
    return jnp.concatenate(rows, axis=0)


def _unpack_lanes(packed, shapes):
    out, r = [], 0
    for shp in shapes:
        n = math.prod(shp)
        nr = 8 * -(-n // PART)
        out.append(packed[r:r + nr].reshape(-1)[:n].reshape(shp))
        r += nr
    return out


def kernel(x, norm_mix_g, w_in, attn_q_norm_g, attn_k_norm_g, rel_bias, attn_out_norm_g, conv_w, a_log, dt_bias, dn_out_norm_g, w_out, norm_ffn_g, w_gate, w_up, w_down, loss_target, m_norm_mix_g, m_w_in, m_attn_q_norm_g, m_attn_k_norm_g, m_rel_bias, m_attn_out_norm_g, m_conv_w, m_a_log, m_dt_bias, m_dn_out_norm_g, m_w_out, m_norm_ffn_g, m_w_gate, m_w_up, m_w_down, v_norm_mix_g, v_w_in, v_attn_q_norm_g, v_attn_k_norm_g, v_rel_bias, v_attn_out_norm_g, v_conv_w, v_a_log, v_dt_bias, v_dn_out_norm_g, v_w_out, v_norm_ffn_g, v_w_gate, v_w_up, v_w_down):
    xs, tgt = x[0], loss_target[0]
    my_idx = 4 * lax.axis_index("x") + 2 * lax.axis_index("y") + lax.axis_index("c")
    late_w = (w_out[0], w_gate[0], w_up[0], w_down[0])

    tab, tabt, wa_all, cw_all = _bias_tables(
        jnp.pad(rel_bias[0].T, ((0, 0), (0, VAR0 - 257))), _pack_rows([w_in[0].T.astype(_MXU)], RA),
        jnp.pad(conv_w[0], ((0, 4), (0, 64))))
    W_in_t = wa_all[:, 0:RA_USED].reshape(N_DEV * RA_USED, D)
    W_int, W_ba = W_in_t[0:3584], W_in_t[3584:3592].T
    conv_full = cw_all[:, 0:CONV_K, 0:192].transpose(1, 0, 2).reshape(CONV_K, 1536)

    qg_t = jnp.tile(attn_q_norm_g, (1, NHA))
    kg_t = jnp.tile(attn_k_norm_g, (1, NHA))
    z4 = jnp.zeros((1, NHD), F32)
    alog8 = jnp.concatenate([z4, a_log], axis=1)
    dtb8 = jnp.concatenate([z4, dt_bias], axis=1)

    late_t = lambda ts: (ts[0], ts[1].T, ts[2].T, ts[3])
    araw, an, draw, z, ba, hb, wl_all = _inproj(xs, norm_mix_g, W_int, W_ba, qg_t, kg_t,
                                                _pack_rows([w.astype(_MXU) for w in late_t(late_w)]))
    W_out = wl_all[:, 0:128].reshape(D, D)
    apre = _attn_fwd(an, tab)
    bat = ba.T
    dn_args = (draw, conv_full, ba, bat, alog8, dtb8, alog8.T, dtb8.T)
    u, w, kd, tm, wq, km, mq, wt, elb, cv = _dn_prep(*dn_args)
    o, vn, sn = _dn_scan(u, wq, km, elb)
    x1, mix = _post_mix(apre, o, z, xs, W_out, attn_out_norm_g, dn_out_norm_g)

    dx1, dx1b, h2, act, dgu, dyb, loss_row, dgffn = _ffn(x1, tgt, wl_all, norm_ffn_g)

    gW_out = _wgrad(mix, dx1b, "wgrad_out", out_dtype=_WIRE)
    gW_gu_t = _wgrad(h2, dgu, "wgrad_gate_up", tn=FF, out_dtype=_WIRE, transposed=True)
    gW_down = _wgrad(dyb, act, "wgrad_down", out_dtype=_WIRE, transposed=True)
    send_late = jnp.concatenate(
        [gW_out.reshape(N_DEV, 128, D), gW_gu_t[0:FF].reshape(N_DEV, 352, D), gW_gu_t[FF:].reshape(N_DEV, 352, D),
         gW_down.reshape(N_DEV, 352, D)], axis=1)

    dap, do, dz, dga, dgdn = _mix_bwd(dx1b, W_out, apre, o, z, attn_out_norm_g, dn_out_norm_g)
    dqn, dkn, dv, dtabt, recv_late = _attn_bwd(an, dap, tabt, send_late)
    drel = _bias_grad(dtabt)
    du, dw, dqd, dkd, dgx = _dn_scan_bwd(do, mq, kd, wt, sn, vn, elb)
    ddraw, dba, sm, dcw = _dn_post_bwd(draw, cv, *dn_args[1:], du, dw, dqd, dkd, dgx, do, vn, tm, u, w)
    gx, dproj, dgmix, dqg, dkg = _inproj_bwd(dqn, dkn, dv, araw, ddraw, dz, dba, xs, dx1, W_int, W_ba, norm_mix_g,
                                             qg_t, kg_t)

    partial = jnp.concatenate(
        [dgmix.reshape(8, LANES), dgffn.reshape(8, LANES), _pad8(dqg.reshape(4, LANES)), _pad8(dkg.reshape(4, LANES)),
         _pad8(dga.reshape(4, LANES)), _pad8(dgdn), sm, loss_row.reshape(8, LANES),
         dcw[0:CONV_K].reshape(48, LANES), drel.reshape(24, LANES)], axis=0)
    gW_in_t, partial_all = _wgrad(hb, dproj, "wgrad_in", tk=512, out_dtype=_WIRE, transposed=True,
                                  gather=partial)
    send_in = jnp.pad(gW_in_t[0:N_DEV * RA_USED].reshape(N_DEV, RA_USED, D), ((0, 0), (0, RA - RA_USED), (0, 0)))
    late_m = (m_w_out[0], m_w_gate[0], m_w_up[0], m_w_down[0])
    late_v = (v_w_out[0], v_w_gate[0], v_w_up[0], v_w_down[0])
    *outs_late, recv_in = _adamw(recv_late, _pack_rows(late_t(late_w)), _pack_rows(late_t(late_m)),
                                 _pack_rows(late_t(late_v)), "adamw_late", 32, send=send_in)
    outs_in = _adamw(recv_in, _pack_rows([w_in[0].T], RA), _pack_rows([m_w_in[0].T], RA),
                     _pack_rows([v_w_in[0].T], RA), "adamw_w_in", 16)
    late_t_shapes = [t.shape for t in late_t(late_w)]
    big = [[a[0:RA_USED].T] + list(late_t(_unpack_rows(b, late_t_shapes))) for a, b in zip(outs_in, outs_late)]
    bg, bd_, bm, bv = big

    S = _small_reduce(partial_all)
    loss = S[R_LOSS, 0]
    g_conv = lax.dynamic_slice(S[R_CONV:R_CONV + 48].reshape(CONV_K, 1536), (0, 192 * my_idx), (CONV_K, 192))
    sg = [S[R_GMIX:R_GMIX + 8].reshape(1, D), S[R_QG:R_QG + 1, 0:DHA], S[R_KG:R_KG + 1, 0:DHA],
          S[R_REL:R_REL + 24].reshape(NHA, 384)[:, 0:257].T, S[R_GA:R_GA + 4].reshape(1, AW), g_conv,
          S[R_ALOG:R_ALOG + 1, NHD:2 * NHD], S[R_DT:R_DT + 1, NHD:2 * NHD], S[R_GDN:R_GDN + 1], S[R_GFFN:R_GFFN + 8].reshape(1, D)]
    sw = [norm_mix_g, attn_q_norm_g, attn_k_norm_g, rel_bias[0], attn_out_norm_g, conv_w[0], a_log, dt_bias, dn_out_norm_g, norm_ffn_g]
    smm = [m_norm_mix_g, m_attn_q_norm_g, m_attn_k_norm_g, m_rel_bias[0], m_attn_out_norm_g, m_conv_w[0], m_a_log, m_dt_bias, m_dn_out_norm_g, m_norm_ffn_g]
    svv = [v_norm_mix_g, v_attn_q_norm_g, v_attn_k_norm_g, v_rel_bias[0], v_attn_out_norm_g, v_conv_w[0], v_a_log, v_dt_bias, v_dn_out_norm_g, v_norm_ffn_g]
    s_shapes = [t.shape for t in sw]
    pg = _pack_lanes(sg)
    s_out = _adamw(pg[None], _pack_lanes(sw), _pack_lanes(smm), _pack_lanes(svv), "adamw_small", pg.shape[0])
    s_g, s_d, s_m, s_v = (_unpack_lanes(t, s_shapes) for t in s_out)

    lead = lambda t: t[None]
    def ordered(small, big):
        nm, q, k, rel, ao, cw, al, dtb, dno, nf = small
        wi, wo, wgt, wu, wdn = big
        return [nm, lead(wi), q, k, lead(rel), ao, lead(cw), al, dtb, dno, lead(wo), nf, lead(wgt), lead(wu), lead(wdn)]
    outs = [loss, gx[None]]
    for small, big in ((s_g, bg), (s_d, bd_), (s_m, bm), (s_v, bv)):
        outs += ordered(small, big)
    return tuple(outs)
```

```python
import math

import jax
import jax.numpy as jnp
from jax import lax
from jax.experimental import pallas as pl
from jax.experimental.pallas import tpu as pltpu

F32 = jnp.float32
BF16 = jnp.bfloat16
_MXU = jnp.bfloat16

D = 1024
AW = 512
NHA = 8
DHA = 64
CH = 64
NHD = 4
DHD = 128
DW = 512
FF = 2816
EPS = 1e-6
NEG = -1e30
N_DEV = 8
LANES = 128
VMEM_LIMIT = 56 * 1024 * 1024

ADAM_LR = 0.001
ADAM_B1 = 0.9
ADAM_B2 = 0.999
ADAM_EPS = 1e-08
ADAM_WD = 0.01
ADAM_STEP = 10

MESH_T = pl.DeviceIdType.MESH


def _cp(sem=None, vmem=VMEM_LIMIT):
    kw = dict(vmem_limit_bytes=vmem)
    if sem is not None:
        kw["dimension_semantics"] = sem
    return pltpu.CompilerParams(**kw)


def _dot(a, b):
    return jnp.dot(a.astype(_MXU), b.astype(_MXU), preferred_element_type=F32)


def _dot_nt(a, b):
    return lax.dot_general(a.astype(_MXU), b.astype(_MXU), (((1,), (1,)), ((), ())), preferred_element_type=F32)


def _dot_tn(a, b):
    return lax.dot_general(a.astype(_MXU), b.astype(_MXU), (((0,), (0,)), ((), ())), preferred_element_type=F32)


def _iota(shape, dim):
    return lax.broadcasted_iota(jnp.int32, shape, dim)


def _block_ones(n, blk, dtype=BF16):
    r, c = _iota((n, n), 0), _iota((n, n), 1)
    return jnp.where((r // blk) == (c // blk), 1.0, 0.0).astype(dtype)


def _sigmoid(x):
    return 1.0 / (1.0 + jnp.exp(-x))


def _softplus(x):
    return jnp.maximum(x, 0.0) + jnp.log(1.0 + jnp.exp(-jnp.abs(x)))


def _col(x, k):
    lane = _iota(x.shape, 1)
    return jnp.sum(jnp.where(lane == k, x, 0.0), axis=1, keepdims=True)


def _row(x, k):
    sub = _iota(x.shape, 0)
    return jnp.sum(jnp.where(sub == k, x, 0.0), axis=0, keepdims=True)


def _my_pos():
    return lax.axis_index("x"), lax.axis_index("y"), lax.axis_index("c")


_COMM_SEMS = [pltpu.SemaphoreType.DMA((7,)), pltpu.SemaphoreType.DMA((7,)), pltpu.SemaphoreType.DMA]


class _Gather:
    def __init__(self, x_ref, out_ref, send_sems, recv_sems, local_sem):
        x, y, c = _my_pos()
        me, sibling = (x, y, c), (x, y, 1 - c)
        chips = [(1 - x, y), (x, 1 - y), (1 - x, 1 - y)]

        def slot(px, py, pc):
            return out_ref.at[4 * px + 2 * py + pc]

        def copy(k, block, to, src=None):
            return pltpu.make_async_remote_copy(
                src_ref=slot(*block) if src is None else src, dst_ref=slot(*block),
                send_sem=send_sems.at[k], recv_sem=recv_sems.at[k], device_id=to, device_id_type=MESH_T)

        self.mine = pltpu.make_async_copy(x_ref, slot(*me), local_sem)
        self.first = [copy(0, me, sibling, src=x_ref)]
        self.first += [copy(1 + j, me, (*chip, c), src=x_ref) for j, chip in enumerate(chips)]
        self.passed = [copy(4 + j, (*chip, c), sibling) for j, chip in enumerate(chips)]
        self.from_chips = [copy(1 + j, (*chip, c), me) for j, chip in enumerate(chips)]
        self.from_sibling = [copy(0, sibling, me)] + [copy(4 + j, (*chip, 1 - c), me) for j, chip in enumerate(chips)]

    def start(self):
        self.mine.start()
        for cp in self.first:
            cp.start()

    def forward(self):
        for arrived, onward in zip(self.from_chips, self.passed):
            arrived.wait_recv()
            onward.start()

    def finish(self):
        for cp in self.from_sibling:
            cp.wait_recv()
        for cp in self.first + self.passed:
            cp.wait_send()
        self.mine.wait()


class _Scatter:
    def __init__(self, s_ref, r_ref, send_sems, recv_sems, local_sem):
        x, y, c = _my_pos()
        self.mine = pltpu.make_async_copy(s_ref.at[4 * x + 2 * y + c], r_ref.at[0], local_sem)
        self.copies = []
        for m in range(1, N_DEV):
            px = x ^ ((m >> 2) & 1)
            py = y ^ ((m >> 1) & 1)
            pc = c ^ (m & 1)
            self.copies.append(pltpu.make_async_remote_copy(
                src_ref=s_ref.at[4 * px + 2 * py + pc], dst_ref=r_ref.at[m],
                send_sem=send_sems.at[m - 1], recv_sem=recv_sems.at[m - 1],
                device_id=(px, py, pc), device_id_type=MESH_T))

    def start(self):
        self.mine.start()
        for cp in self.copies:
            cp.start()

    def finish(self):
        for cp in self.copies:
            cp.wait_recv()
        for cp in self.copies:
            cp.wait_send()
        self.mine.wait()


TM = 512
TF = 256
TG = 512


def _full(shape):
    nd = len(shape)
    return pl.BlockSpec(shape, lambda i: (0,) * nd)


def _rows(tm, w):
    return pl.BlockSpec((tm, w), lambda i: (i, 0))


def _head_sum(x, bd):
    one_pass = lambda t: jnp.dot(t.astype(_MXU), bd.astype(_MXU), preferred_element_type=F32)
    return jnp.concatenate([one_pass(x[:, 0:256]), one_pass(x[:, 256:512])], axis=1)


def _head_rms(x, bd, width):
    return lax.rsqrt(_head_sum(x * x, bd) * (1.0 / width) + EPS)


def _inproj(x, g_mix, w_int, w_ba, qg_t, kg_t, later_w):
    T = x.shape[0]
    nt = T // TM
    ntd = (((1,), (1,)), ((), ()))

    def body(x_ref, g_ref, w_ref, wba_ref, qg_ref, kg_ref, lw_ref, araw_ref, an_ref, draw_ref, z_ref, ba_ref, h_ref,
             lw_all, send_sems, recv_sems, local_sem):
        i = pl.program_id(0)
        ag = _Gather(lw_ref, lw_all, send_sems, recv_sems, local_sem)
        pl.when(i == 0)(ag.start)
        pl.when(i == nt // 2)(ag.forward)
        xv = x_ref[...]
        r = lax.rsqrt(jnp.mean(xv * xv, axis=1, keepdims=True) + EPS)
        h = (xv * r * g_ref[...]).astype(_MXU)
        h_ref[...] = h
        proj = lambda lo, hi: lax.dot_general(h, w_ref[lo:hi, :], ntd, preferred_element_type=F32)
        q, k, v = proj(0, AW), proj(AW, 2 * AW), proj(2 * AW, 3 * AW)
        draw_ref[...] = proj(1536, 3072)
        z_ref[...] = proj(3072, 3584)
        ba_ref[...] = jnp.dot(h, wba_ref[...], preferred_element_type=F32)
        araw_ref[:, 0:AW] = q
        araw_ref[:, AW:2 * AW] = k
        bd = _block_ones(AW // 2, DHA)
        qn = q * _head_rms(q, bd, DHA) * (qg_ref[...] * (DHA ** -0.5))
        kn = k * _head_rms(k, bd, DHA) * kg_ref[...]
        an_ref[:, 0:AW] = qn.astype(_MXU)
        an_ref[:, AW:2 * AW] = kn.astype(_MXU)
        an_ref[:, 2 * AW:3 * AW] = v.astype(_MXU)
        pl.when(i == nt - 1)(ag.finish)

    anyspec = pl.BlockSpec(memory_space=pl.ANY)
    return pl.pallas_call(
        body, name="inproj", grid=(nt,),
        in_specs=[_rows(TM, D), _full((1, D)), _full((3584, D)), _full((D, 8)), _full((1, AW)), _full((1, AW)),
                  anyspec],
        out_specs=[_rows(TM, 2 * AW), _rows(TM, 1536), _rows(TM, 1536), _rows(TM, DW), _rows(TM, 8), _rows(TM, D),
                   anyspec],
        out_shape=[jax.ShapeDtypeStruct((T, 2 * AW), F32), jax.ShapeDtypeStruct((T, 1536), _MXU),
                   jax.ShapeDtypeStruct((T, 1536), F32), jax.ShapeDtypeStruct((T, DW), F32),
                   jax.ShapeDtypeStruct((T, 8), F32), jax.ShapeDtypeStruct((T, D), _MXU),
                   jax.ShapeDtypeStruct((N_DEV,) + later_w.shape, later_w.dtype)],
        scratch_shapes=_COMM_SEMS,
        compiler_params=_cp(("arbitrary",)),
    )(x, g_mix, w_int, w_ba, qg_t, kg_t, later_w)


TQ = 256
TW = 768
VAR0 = 384
TOEP = 1024


def _bias_tables(rb_t, w_shard, c_shard):
    def body(rb_ref, w_ref, c_ref, tab_ref, tabt_ref, w_all, c_all, ws, wr, wl, cs, cr, cl):
        h = pl.program_id(0)
        gathers = [_Gather(w_ref, w_all, ws, wr, wl), _Gather(c_ref, c_all, cs, cr, cl)]

        @pl.when(h == 0)
        def _():
            for ag in gathers:
                ag.start()

        @pl.when(h == NHA // 2)
        def _():
            for ag in gathers:
                ag.forward()

        rb8 = jnp.broadcast_to(_row(rb_ref[...], h), (8, VAR0))
        n = _iota((VAR0, TOEP), 1)
        t = _iota((VAR0, TOEP), 0)

        def line(m):
            onehot = jnp.where(jnp.clip(512 - m, -128, 128) + 128 == t, 1.0, 0.0).astype(BF16)
            return sum(jnp.dot(p, onehot, preferred_element_type=F32) for p in _split3(rb8))[0:1, :]

        def band(r, j, first_key):
            return ((j >> 6) >= (r >> 6)) & ((j >> 6) <= (r >> 6) + 8) & (j >= first_key)

        g = line(jnp.where(n < TW, n, n - TOEP))
        tab = pltpu.roll(jnp.broadcast_to(g, (TQ, TOEP)), 0, 1, stride=1, stride_axis=0)[:, 0:TW]
        gt = line(jnp.where(n < TQ, -n, TOEP - n))
        tabt = pltpu.roll(jnp.broadcast_to(gt, (TW, TOEP)), 0, 1, stride=1, stride_axis=0)[:, 0:TQ]
        for v in range(3):
            first_key = max(512 - TQ * v, 0)
            tab_ref[v, 0] = jnp.where(band(_iota((TQ, TW), 0), _iota((TQ, TW), 1), first_key), tab, NEG)
            tabt_ref[v, 0] = jnp.where(band(_iota((TW, TQ), 1), _iota((TW, TQ), 0), first_key), tabt, NEG)

        @pl.when(h == NHA - 1)
        def _():
            for ag in gathers:
                ag.finish()

    anyspec = pl.BlockSpec(memory_space=pl.ANY)
    return pl.pallas_call(
        body, name="bias_tables", grid=(NHA,),
        in_specs=[_full((NHA, VAR0)), anyspec, anyspec],
        out_specs=[pl.BlockSpec((3, 1, TQ, TW), lambda h: (0, h, 0, 0)),
                   pl.BlockSpec((3, 1, TW, TQ), lambda h: (0, h, 0, 0)), anyspec, anyspec],
        out_shape=[jax.ShapeDtypeStruct((3, NHA, TQ, TW), F32), jax.ShapeDtypeStruct((3, NHA, TW, TQ), F32),
                   jax.ShapeDtypeStruct((N_DEV,) + w_shard.shape, w_shard.dtype),
                   jax.ShapeDtypeStruct((N_DEV,) + c_shard.shape, c_shard.dtype)],
        scratch_shapes=_COMM_SEMS + _COMM_SEMS,
        compiler_params=_cp(("arbitrary",)),
    )(rb_t, w_shard, c_shard)


def _bias_grad(dtabt):
    def body(d_ref, o_ref):
        a, b = _iota((TQ, TQ), 0), _iota((TQ, TQ), 1)
        anti = jnp.where(a + b == TQ - 1, 1.0, 0.0).astype(BF16)
        drev = sum(jnp.dot(t, anti, preferred_element_type=F32) for t in _split3(d_ref[0]))
        wide = jnp.concatenate([drev, jnp.zeros((TW, TOEP - TQ), F32)], axis=1)
        cols = jnp.sum(pltpu.roll(wide, 0, 1, stride=1, stride_axis=0), axis=0, keepdims=True)
        c = _iota((TOEP, VAR0), 0)
        idx = jnp.clip(512 + TQ - 1 - c, -128, 128) + 128
        onehot = jnp.where(idx == _iota((TOEP, VAR0), 1), 1.0, 0.0).astype(BF16)
        cols8 = jnp.broadcast_to(cols, (8, TOEP))
        o_ref[0] = sum(jnp.dot(t, onehot, preferred_element_type=F32) for t in _split3(cols8))[0:1, :]

    return pl.pallas_call(
        body, name="bias_grad", grid=(NHA,),
        in_specs=[pl.BlockSpec((1, TW, TQ), lambda h: (h, 0, 0))],
        out_specs=pl.BlockSpec((1, 1, VAR0), lambda h: (h, 0, 0)),
        out_shape=jax.ShapeDtypeStruct((NHA, 1, VAR0), F32),
        compiler_params=_cp(("arbitrary",)),
    )(dtabt)


def _kv_spec(col, back):
    return pl.BlockSpec((TQ, AW), lambda i: (jnp.maximum(i - back, 0), col))


def _attn_fwd(an, tab):
    T = an.shape[0]

    def body(q_ref, k2_ref, k1_ref, k0_ref, v2_ref, v1_ref, v0_ref, tab_ref, o_ref):
        i = pl.program_id(0)
        kwin = jnp.concatenate([k2_ref[...], k1_ref[...], k0_ref[...]], axis=0)
        vwin = jnp.concatenate([v2_ref[...], v1_ref[...], v0_ref[...]], axis=0)
        q = q_ref[...]
        lo_half = _iota((TQ, LANES), 1) < DHA

        def scores(h):
            sl = slice(LANES * (h // 2), LANES * (h // 2 + 1))
            mask = lo_half if h % 2 == 0 else jnp.logical_not(lo_half)
            qm = jnp.where(mask, q[:, sl], jnp.zeros((TQ, LANES), q.dtype))
            return _dot_nt(qm, kwin[:, sl]) + tab_ref[0, h]

        s_next = scores(0)
        outs = []
        for h in range(NHA):
            s = s_next
            if h + 1 < NHA:
                s_next = scores(h + 1)
            sl = slice(LANES * (h // 2), LANES * (h // 2 + 1))
            m = jnp.max(s, axis=1, keepdims=True)
            e = jnp.exp(s - m)
            l = jnp.sum(e, axis=1, keepdims=True)
            outs.append(_dot(e, vwin[:, sl]) / l)
            if h % 2 == 1:
                o_ref[:, sl] = jnp.where(lo_half, outs[h - 1], outs[h])

    return pl.pallas_call(
        body, name="attn_fwd", grid=(T // TQ,),
        in_specs=[pl.BlockSpec((TQ, AW), lambda i: (i, 0)),
                  _kv_spec(1, 2), _kv_spec(1, 1), _kv_spec(1, 0), _kv_spec(2, 2), _kv_spec(2, 1), _kv_spec(2, 0),
                  pl.BlockSpec((1, NHA, TQ, TW), lambda i: (jnp.minimum(i, 2), 0, 0, 0))],
        out_specs=_rows(TQ, AW),
        out_shape=jax.ShapeDtypeStruct((T, AW), F32),
        compiler_params=_cp(("arbitrary",)),
    )(an, an, an, an, an, an, an, tab)


def _attn_bwd(an, dout, tabt, send):
    T = an.shape[0]
    nq = T // TQ

    def qi(i):
        return jnp.minimum(i, nq - 1)

    def kv_spec(col, back):
        return pl.BlockSpec((TQ, AW), lambda i: (jnp.maximum(qi(i) - back, 0), col))

    def body(q_ref, do_ref, k2_ref, k1_ref, k0_ref, v2_ref, v1_ref, v0_ref, tabt_ref, send_ref,
             dq_ref, dk_ref, dv_ref, dtab_ref, recv_ref, dk_acc, dv_acc, send_sems, recv_sems, local_sem):
        i = pl.program_id(0)
        sc = _Scatter(send_ref, recv_ref, send_sems, recv_sems, local_sem)
        pl.when(i == 0)(sc.start)

        @pl.when(i == 0)
        def _():
            dtab_ref[...] = jnp.zeros_like(dtab_ref)

        new = i % 3
        dk_acc[new] = jnp.zeros((TQ, AW), F32)
        dv_acc[new] = jnp.zeros((TQ, AW), F32)

        @pl.when(i < nq)
        def _():
            kwin = jnp.concatenate([k2_ref[...], k1_ref[...], k0_ref[...]], axis=0)
            vwin = jnp.concatenate([v2_ref[...], v1_ref[...], v0_ref[...]], axis=0)
            q = q_ref[...]
            do = do_ref[...].astype(_MXU)
            lo_half = _iota((TQ, LANES), 1) < DHA

            def front(h):
                sl = slice(LANES * (h // 2), LANES * (h // 2 + 1))
                mask = lo_half if h % 2 == 0 else jnp.logical_not(lo_half)
                zero = jnp.zeros((TQ, LANES), q.dtype)
                qm = jnp.where(mask, q[:, sl], zero)
                dom = jnp.where(mask, do[:, sl], zero)
                st = _dot_nt(kwin[:, sl], qm) + tabt_ref[0, h]
                return st, _dot_nt(vwin[:, sl], dom), qm, dom, mask

            pairs = {}

            def back(h, ptb, dsb, qm, dom, mask):
                sl = slice(LANES * (h // 2), LANES * (h // 2 + 1))
                dv = _dot(ptb, dom)
                dk = _dot(dsb, qm)
                dq = jnp.where(mask, _dot_tn(dsb, kwin[:, sl]), 0.0)
                if h % 2 == 0:
                    pairs[h // 2] = (dq, dk, dv)
                    return
                dq0, dk0, dv0 = pairs.pop(h // 2)
                dq_ref[:, sl] = dq0 + dq
                dk_pair, dv_pair = dk0 + dk, dv0 + dv
                for w in range(3):
                    slot = (i + 1 + w) % 3
                    rows = slice(TQ * w, TQ * (w + 1))
                    dk_acc[slot, :, sl] += dk_pair[rows]
                    dv_acc[slot, :, sl] += dv_pair[rows]

            nxt = front(0)
            pending = None
            for h in range(NHA):
                st, dpt, qm, dom, mask = nxt
                if h + 1 < NHA:
                    nxt = front(h + 1)
                m = jnp.max(st, axis=0, keepdims=True)
                e = jnp.exp(st - m)
                pt = e * (1.0 / jnp.sum(e, axis=0, keepdims=True))
                delta = jnp.sum(pt * dpt, axis=0, keepdims=True)
                dst = pt * (dpt - delta)
                dtab_ref[h] += dst
                if pending is not None:
                    back(*pending)
                pending = (h, pt.astype(_MXU), dst.astype(_MXU), qm, dom, mask)
            back(*pending)

        @pl.when(i >= 2)
        def _():
            done = (i + 1) % 3
            dk_ref[...] = dk_acc[done]
            dv_ref[...] = dv_acc[done]

        pl.when(i == nq + 1)(sc.finish)

    back2 = pl.BlockSpec((TQ, AW), lambda i: (jnp.maximum(i - 2, 0), 0))
    anyspec = pl.BlockSpec(memory_space=pl.ANY)
    return pl.pallas_call(
        body, name="attn_bwd", grid=(nq + 2,),
        in_specs=[pl.BlockSpec((TQ, AW), lambda i: (qi(i), 0)), pl.BlockSpec((TQ, AW), lambda i: (qi(i), 0)),
                  kv_spec(1, 2), kv_spec(1, 1), kv_spec(1, 0), kv_spec(2, 2), kv_spec(2, 1), kv_spec(2, 0),
                  pl.BlockSpec((1, NHA, TW, TQ), lambda i: (jnp.minimum(i, 2), 0, 0, 0)), anyspec],
        out_specs=[pl.BlockSpec((TQ, AW), lambda i: (qi(i), 0)), back2, back2, _full((NHA, TW, TQ)), anyspec],
        out_shape=[jax.ShapeDtypeStruct((T, AW), F32), jax.ShapeDtypeStruct((T, AW), F32),
                   jax.ShapeDtypeStruct((T, AW), F32), jax.ShapeDtypeStruct((NHA, TW, TQ), F32),
                   jax.ShapeDtypeStruct(send.shape, send.dtype)],
        scratch_shapes=[pltpu.VMEM((3, TQ, AW), F32), pltpu.VMEM((3, TQ, AW), F32)] + _COMM_SEMS,
        compiler_params=_cp(("arbitrary",)),
    )(an, dout, an, an, an, an, an, an, tabt, send)


GR = 128
NG = TG // GR
CPT = TG // CH
CONV_K = 4


def _split3(x):
    a = x.astype(BF16)
    r = x - a.astype(F32)
    b = r.astype(BF16)
    c = (r - b.astype(F32)).astype(BF16)
    return a, b, c


def _ones_dot(ones_b, x):
    return sum(jnp.dot(ones_b, t, preferred_element_type=F32) for t in _split3(x))


def _dot_ones_nt(x, ones_b):
    dn = (((1,), (1,)), ((), ()))
    return sum(lax.dot_general(t, ones_b, dn, preferred_element_type=F32) for t in _split3(x))


def _dn_masks():
    r, c = _iota((GR, GR), 0), _iota((GR, GR), 1)
    same = (r >> 6) == (c >> 6)
    one = lambda m: jnp.where(m, 1.0, 0.0).astype(BF16)
    return dict(
        tril=same & (c <= r), strict=same & (c < r), triu=same & (c >= r), strict_u=same & (c > r),
        tril_b=one(same & (c <= r)), triu_b=one(same & (c >= r)), blk_b=one(same), eye_b=one(r == c),
        eye=jnp.where(r == c, 1.0, 0.0).astype(F32),
        fold_b=one((_iota((GR, CH), 0) & (CH - 1)) == _iota((GR, CH), 1)),
        last=(_iota((GR, 1), 0) & (CH - 1)) == CH - 1,
    )


def _shift_down(x, halo, k):
    if k == 0:
        return x
    xs = pltpu.roll(x, k, 0)
    hs = pltpu.roll(halo, k, 0)
    top = jnp.where(_iota(halo.shape, 0) < k, hs, xs[0:8])
    return jnp.concatenate([top, xs[8:]], axis=0)


def _shift_up(x, halo, k):
    if k == 0:
        return x
    n = x.shape[0]
    xs = pltpu.roll(x, n - k, 0)
    hs = pltpu.roll(halo, 8 - k, 0)
    bot = jnp.where(_iota(halo.shape, 0) >= 8 - k, hs, xs[n - 8:n])
    return jnp.concatenate([xs[0:n - 8], bot], axis=0)


def _conv(x, halo, w):
    y = x * w[CONV_K - 1:CONV_K, :]
    for k in range(1, CONV_K):
        y = y + _shift_down(x, halo, k) * w[CONV_K - 1 - k:CONV_K - k, :]
    return y


def _tri_inv(lmats, eye):
    ps = [-m for m in lmats]
    rs = [eye + p for p in ps]
    for _ in range(5):
        ps = [_dot(p, p) for p in ps]
        rs = [r + _dot(r, p) for r, p in zip(rs, ps)]
    return rs


def _gate_terms(ba_g, bat_g, alog8, dtb8, alog8t, dtb8t, K):
    g8 = -jnp.exp(alog8) * _softplus(ba_g + dtb8)
    g8t = -jnp.exp(alog8t) * _softplus(bat_g + dtb8t)
    gc8 = _ones_dot(K["tril_b"], g8)
    gl8 = _ones_dot(K["blk_b"], g8)
    gcrow8 = _dot_ones_nt(g8t, K["tril_b"])
    return g8, gc8, gl8, gcrow8


def _dn_heads(c_tile, rows, beta8, gc8, gl8, gcrow8, K, pre=None):
    return _dn_heads_groups(c_tile, [(rows, beta8, gc8, gl8, gcrow8)], K, None if pre is None else [pre])[0]


def _dn_heads_groups(c_tile, groups, K, pres=None):
    ds = [_dn_head_vec(c_tile, rows, h, beta8, gc8, gl8, gcrow8, K)
          for rows, beta8, gc8, gl8, gcrow8 in groups for h in range(NHD)]
    pls = [_dot_nt(d["kb"], d["kn"]) for d in ds]
    pms = [_dot_nt(d["qn"], d["kn"]) for d in ds]
    for d, pl_, pm in zip(ds, pls, pms):
        d.update(pl=pl_, pm=pm, lmat=jnp.where(K["strict"], pl_ * d["gam_m"], 0.0), mm=pm * d["gam_m"])
    if pres is None:
        for d, tm in zip(ds, _tri_inv([d["lmat"] for d in ds], K["eye"])):
            d.update(tm=tm, u=_dot(tm, d["vb"]), w=_dot(tm, d["kg"]))
    else:
        for d, (tm, u, w) in zip(ds, [p for pre in pres for p in pre]):
            d.update(tm=tm, u=u, w=w)
    return [ds[NHD * k:NHD * (k + 1)] for k in range(len(groups))]


def _dn_head_vec(c_tile, rows, h, beta8, gc8, gl8, gcrow8, K):
    qr = c_tile[rows, DHD * h:DHD * (h + 1)]
    kr = c_tile[rows, DW + DHD * h:DW + DHD * (h + 1)]
    v = c_tile[rows, 2 * DW + DHD * h:2 * DW + DHD * (h + 1)]
    rq = lax.rsqrt(jnp.sum(qr * qr, axis=1, keepdims=True) + EPS)
    rk = lax.rsqrt(jnp.sum(kr * kr, axis=1, keepdims=True) + EPS)
    qh, kn = qr * rq, kr * rk
    qn = qh * (DHD ** -0.5)
    beta = _col(beta8, h)
    gccol, glcol, gcrow = _col(gc8, NHD + h), _col(gl8, NHD + h), _row(gcrow8, NHD + h)
    diff = gccol - gcrow
    gam_m = jnp.exp(jnp.where(K["tril"], diff, NEG))
    gam = jnp.exp(gccol)
    egl = jnp.exp(glcol - gccol)
    kb, vb = kn * beta, v * beta
    kg = kb * gam
    return dict(qr=qr, kr=kr, v=v, rq=rq, rk=rk, qh=qh, qn=qn, kn=kn, beta=beta, diff=diff, gam_m=gam_m, gam=gam,
                egl=egl, el=jnp.exp(glcol), kb=kb, vb=vb, kg=kg, qd=qn * gam, kd=kn * egl)


def _halo_prev(width):
    return pl.BlockSpec((8, width), lambda i: (jnp.maximum(i * (TG // 8) - 1, 0), 0))


def _dn_prep(draw, conv_w, ba, bat, alog8, dtb8, alog8t, dtb8t):
    T = draw.shape[0]
    nb = T // TG
    hm = lambda w, dt: jax.ShapeDtypeStruct((NHD, T, w), dt)
    hm_spec = lambda w: pl.BlockSpec((NHD, TG, w), lambda i: (0, i, 0))
    pc = lambda r, c: jax.ShapeDtypeStruct((NHD, T // CH, r, c), _MXU)
    pc_spec = lambda r, c: pl.BlockSpec((NHD, CPT, r, c), lambda i: (0, i, 0, 0))

    def body(x_ref, halo_ref, cw_ref, ba_ref, bat_ref, al_ref, dt_ref, alt_ref, dtt_ref,
             u_ref, w_ref, kd_ref, tm_ref, wq_ref, km_ref, mq_ref, wt_ref, elb_ref, cv_ref):
        i = pl.program_id(0)
        K = _dn_masks()
        halo = jnp.where(i > 0, halo_ref[...], 0.0)
        cv = _conv(x_ref[...], halo, cw_ref[...])
        cv_ref[...] = cv
        c_tile = cv * _sigmoid(cv)
        eye128 = jnp.where(_iota((DHD, DHD), 0) == _iota((DHD, DHD), 1), 1.0, 0.0).astype(_MXU)
        def gate_inputs(g):
            rows = slice(GR * g, GR * (g + 1))
            ba_g = ba_ref[rows, :]
            _, gc8, gl8, gcrow8 = _gate_terms(ba_g, bat_ref[:, rows], al_ref[...], dt_ref[...], alt_ref[...],
                                              dtt_ref[...], K)
            return rows, _sigmoid(ba_g), gc8, gl8, gcrow8

        def store(g, rows, ds):
            mmts = [_dot_nt(d["kn"], d["qn"]) * jnp.exp(jnp.where(K["triu"], -d["diff"], NEG)) for d in ds]
            mcs = [_dot(d["mm"], K["fold_b"]) for d in ds]
            mcts = [_dot(m, K["fold_b"]) for m in mmts]
            for h, d in enumerate(ds):
                tm_ref[h, rows, :] = d["tm"].astype(_MXU)
                u_ref[h, rows, :] = d["u"]
                w_ref[h, rows, :] = d["w"].astype(_MXU)
                kd_ref[h, rows, :] = d["kd"].astype(_MXU)
                elb = jnp.broadcast_to(d["el"], (GR, DHD))
                for cc in range(GR // CH):
                    ch = slice(CH * cc, CH * (cc + 1))
                    n = (GR // CH) * g + cc
                    wq_ref[h, n, 0:CH, :] = d["w"][ch].astype(_MXU)
                    wq_ref[h, n, CH:2 * CH, :] = d["qd"][ch].astype(_MXU)
                    km_ref[h, n, 0:DHD, :] = _dot_nt(eye128, d["kd"][ch]).astype(_MXU)
                    km_ref[h, n, DHD:DHD + CH, :] = mcs[h][ch].astype(_MXU)
                    mq_ref[h, n, 0:CH, :] = mcts[h][ch].astype(_MXU)
                    mq_ref[h, n, CH:CH + DHD, :] = _dot_nt(eye128, d["qd"][ch]).astype(_MXU)
                    wt_ref[h, n] = _dot_nt(eye128, d["w"][ch]).astype(_MXU)
                    elb_ref[n:n + 1, DHD * h:DHD * (h + 1)] = elb[CH * cc:CH * cc + 1, :]

        PAIR = 4
        for g0 in range(0, NG, PAIR):
            pair = [gate_inputs(g) for g in range(g0, g0 + PAIR)]
            for k, ds in enumerate(_dn_heads_groups(c_tile, pair, K)):
                store(g0 + k, pair[k][0], ds)

    return pl.pallas_call(
        body, name="dn_prep", grid=(nb,),
        in_specs=[_rows(TG, 1536), _halo_prev(1536), _full((CONV_K, 1536)), _rows(TG, 8),
                  pl.BlockSpec((8, TG), lambda i: (0, i)), _full((1, 8)), _full((1, 8)), _full((8, 1)), _full((8, 1))],
        out_specs=[hm_spec(DHD), hm_spec(DHD), hm_spec(DHD), hm_spec(GR),
                   pc_spec(2 * CH, DHD), pc_spec(DHD + CH, CH), pc_spec(CH + DHD, CH), pc_spec(DHD, CH),
                   pl.BlockSpec((CPT, NHD * DHD), lambda i: (i, 0)), _rows(TG, 1536)],
        out_shape=[hm(DHD, F32), hm(DHD, _MXU), hm(DHD, _MXU), hm(GR, _MXU),
                   pc(2 * CH, DHD), pc(DHD + CH, CH), pc(CH + DHD, CH), pc(DHD, CH),
                   jax.ShapeDtypeStruct((T // CH, NHD * DHD), F32), jax.ShapeDtypeStruct((T, 1536), F32)],
        compiler_params=_cp(("arbitrary",)),
    )(draw, draw, conv_w, ba, bat, alog8, dtb8, alog8t, dtb8t)


def _dn_scan(u, wq, km, elb):
    T = u.shape[1]
    nb = T // TG
    hm_spec = lambda wd: pl.BlockSpec((NHD, TG, wd), lambda i: (0, i, 0))

    def body(u_ref, wq_ref, km_ref, elb_ref, o_ref, vn_ref, sn_ref, S):
        @pl.when(pl.program_id(0) == 0)
        def _():
            S[...] = jnp.zeros_like(S)

        sub8 = _iota((CPT, DHD), 0)
        heads = range(NHD)

        def chunk(cc, carry):
            rs = pl.ds(pl.multiple_of(cc * CH, CH), CH)
            sh = [S[h] for h in heads]
            sb = [s.astype(_MXU) for s in sh]
            r1 = [_dot(wq_ref[h, cc], sb[h]) for h in heads]
            vnb = [(u_ref[h, rs, :] - r1[h][0:CH]).astype(_MXU) for h in heads]
            r2 = [_dot(km_ref[h, cc], vnb[h]) for h in heads]
            for h in heads:
                el = jnp.sum(jnp.where(sub8 == cc, elb_ref[:, DHD * h:DHD * (h + 1)], 0.0), axis=0, keepdims=True)
                S[h] = sh[h] * el + r2[h][0:DHD]
                sn_ref[cc, h] = sb[h]
                vn_ref[h, rs, :] = vnb[h]
                o_ref[h, rs, :] = r1[h][CH:2 * CH] + r2[h][DHD:DHD + CH]
            return carry

        lax.fori_loop(0, CPT, chunk, 0)

    return pl.pallas_call(
        body, name="dn_scan", grid=(nb,),
        in_specs=[hm_spec(DHD), pl.BlockSpec((NHD, CPT, 2 * CH, DHD), lambda i: (0, i, 0, 0)),
                  pl.BlockSpec((NHD, CPT, DHD + CH, CH), lambda i: (0, i, 0, 0)),
                  pl.BlockSpec((CPT, NHD * DHD), lambda i: (i, 0))],
        out_specs=[hm_spec(DHD), hm_spec(DHD), pl.BlockSpec((CPT, NHD, DHD, DHD), lambda i: (i, 0, 0, 0))],
        out_shape=[jax.ShapeDtypeStruct((NHD, T, DHD), F32), jax.ShapeDtypeStruct((NHD, T, DHD), _MXU),
                   jax.ShapeDtypeStruct((T // CH, NHD, DHD, DHD), _MXU)],
        scratch_shapes=[pltpu.VMEM((NHD, DHD, DHD), F32)],
        compiler_params=_cp(("arbitrary",)),
    )(u, wq, km, elb)


def _dn_scan_bwd(do, mq, kd, wt, sn, vn, elb):
    T = do.shape[1]
    nb = T // TG
    rev = lambda wd: pl.BlockSpec((NHD, TG, wd), lambda i: (0, nb - 1 - i, 0))
    rev_t = lambda r: pl.BlockSpec((NHD, CPT, r, CH), lambda i: (0, nb - 1 - i, 0, 0))

    def body(do_ref, mq_ref, kd_ref, wt_ref, sn_ref, vn_ref, elb_ref,
             du_ref, dw_ref, dqd_ref, dkd_ref, dgx_ref, dS):
        @pl.when(pl.program_id(0) == 0)
        def _():
            dS[...] = jnp.zeros_like(dS)

        last_row = _iota((CH, DHD), 0) == CH - 1
        sub8 = _iota((CPT, DHD), 0)
        heads = range(NHD)

        def chunk(k, carry):
            cc = CPT - 1 - k
            rs = pl.ds(pl.multiple_of(cc * CH, CH), CH)
            dsh = [dS[h] for h in heads]
            dsb = [d.astype(_MXU) for d in dsh]
            doc = [do_ref[h, rs, :].astype(_MXU) for h in heads]
            a = [_dot(mq_ref[h, cc], doc[h]) for h in heads]
            b = [_dot(kd_ref[h, rs, :], dsb[h]) for h in heads]
            dvn = [a[h][0:CH] + b[h] for h in heads]
            dvnb = [d.astype(_MXU) for d in dvn]
            e = [_dot(wt_ref[h, cc], dvnb[h]) for h in heads]
            for h in heads:
                el = jnp.sum(jnp.where(sub8 == cc, elb_ref[:, DHD * h:DHD * (h + 1)], 0.0), axis=0, keepdims=True)
                sn = sn_ref[cc, h]
                dS[h] = a[h][CH:CH + DHD] + dsh[h] * el - e[h]
                du_ref[h, rs, :] = dvn[h]
                c = _dot_nt(jnp.concatenate([doc[h], dvnb[h]], axis=0), sn)
                dqd_ref[h, rs, :] = c[0:CH]
                dw_ref[h, rs, :] = -c[CH:2 * CH]
                dkd_ref[h, rs, :] = _dot_nt(vn_ref[h, rs, :], dsb[h])
                part = jnp.sum(dsh[h] * sn.astype(F32), axis=0, keepdims=True) * el
                dgx_ref[h, rs, :] = jnp.where(last_row, part, 0.0)
            return carry

        lax.fori_loop(0, CPT, chunk, 0)

    o = jax.ShapeDtypeStruct((NHD, T, DHD), F32)
    return pl.pallas_call(
        body, name="dn_scan_bwd", grid=(nb,),
        in_specs=[rev(DHD), rev_t(CH + DHD), rev(DHD), rev_t(DHD),
                  pl.BlockSpec((CPT, NHD, DHD, DHD), lambda i: (nb - 1 - i, 0, 0, 0)), rev(DHD),
                  pl.BlockSpec((CPT, NHD * DHD), lambda i: (nb - 1 - i, 0))],
        out_specs=[rev(DHD)] * 5,
        out_shape=[o] * 5,
        scratch_shapes=[pltpu.VMEM((NHD, DHD, DHD), F32)],
        compiler_params=_cp(("arbitrary",)),
    )(do, mq, kd, wt, sn, vn, elb)


def _put_col(acc, k, col):
    return jnp.where(_iota(acc.shape, 1) == k, col, acc)


def _dn_post_bwd(draw, cv, conv_w, ba, bat, alog8, dtb8, alog8t, dtb8t, du, dw, dqd, dkd, dgx, do, vn, tm, u, w):
    T = draw.shape[0]
    nb = T // TG
    hm_spec = lambda wd: pl.BlockSpec((NHD, TG, wd), lambda i: (0, nb - 1 - i, 0))
    rrows = lambda w: pl.BlockSpec((TG, w), lambda i: (nb - 1 - i, 0))

    def body(x_ref, cv_ref, cw_ref, ba_ref, bat_ref, al_ref, dt_ref, alt_ref, dtt_ref,
             du_ref, dw_ref, dqd_ref, dkd_ref, dgx_ref, do_ref, vn_ref, tm_ref, u_ref, w_ref,
             dx_ref, dba_ref, sm_ref, dcw_ref, dc_ref, nxt_ref):
        i = pl.program_id(0)

        @pl.when(i == 0)
        def _():
            sm_ref[...] = jnp.zeros_like(sm_ref)
            dcw_ref[...] = jnp.zeros_like(dcw_ref)
            nxt_ref[...] = jnp.zeros_like(nxt_ref)

        K = _dn_masks()
        cv = cv_ref[...]
        sg = _sigmoid(cv)
        c_tile = cv * sg
        dsilu = sg * (1.0 + cv * (1.0 - sg))
        for g in range(NG):
            rows = slice(GR * g, GR * (g + 1))
            ba_g = ba_ref[rows, :]
            g8, gc8, gl8, gcrow8 = _gate_terms(ba_g, bat_ref[:, rows], al_ref[...], dt_ref[...], alt_ref[...],
                                               dtt_ref[...], K)
            beta8 = _sigmoid(ba_g)
            dgc8 = jnp.zeros((GR, 8), F32)
            rd8 = jnp.zeros((GR, 8), F32)
            dbeta8 = jnp.zeros((GR, 8), F32)
            pre = [(tm_ref[h, rows, :], u_ref[h, rows, :], w_ref[h, rows, :]) for h in range(NHD)]
            ds = _dn_heads(c_tile, rows, beta8, gc8, gl8, gcrow8, K, pre)
            H = range(NHD)
            eye_b = K["eye_b"].astype(_MXU)
            gam_t = [jnp.exp(jnp.where(K["triu"], -d["diff"], NEG)) for d in ds]
            doh = [do_ref[h, rows, :] for h in H]
            vnh = [vn_ref[h, rows, :] for h in H]
            tt = [_dot_nt(eye_b, d["tm"]) for d in ds]
            dvb = [_dot(tt[h], du_ref[h, rows, :]) for h in H]
            dkg = [_dot(tt[h], dw_ref[h, rows, :]) for h in H]
            plt = [_dot_nt(d["kn"], d["kb"]) for d in ds]
            pmt = [_dot_nt(d["kn"], d["qn"]) for d in ds]
            da = [-(_dot_nt(dvb[h], ds[h]["u"]) + _dot_nt(dkg[h], ds[h]["w"])) for h in H]
            dat = [-(_dot_nt(ds[h]["u"], dvb[h]) + _dot_nt(ds[h]["w"], dkg[h])) for h in H]
            dpm = [jnp.where(K["tril"], _dot_nt(doh[h], vnh[h]), 0.0) * ds[h]["gam_m"] for h in H]
            dpmt = [jnp.where(K["triu"], _dot_nt(vnh[h], doh[h]), 0.0) * gam_t[h] for h in H]
            dpl = [jnp.where(K["strict"], da[h], 0.0) * ds[h]["gam_m"] for h in H]
            dplt = [jnp.where(K["strict_u"], dat[h], 0.0) * gam_t[h] for h in H]
            dkb = [_dot(dpl[h], ds[h]["kn"]) + dkg[h] * ds[h]["gam"] for h in H]
            dqn = [_dot(dpm[h], ds[h]["kn"]) + dqd_ref[h, rows, :] * ds[h]["gam"] for h in H]
            dknm = [_dot(dplt[h], ds[h]["kb"]) + _dot(dpmt[h], ds[h]["qn"]) for h in H]
            for h, d in enumerate(ds):
                kn, dqdh, dkdh = d["kn"], dqd_ref[h, rows, :], dkd_ref[h, rows, :]
                dkn = dknm[h] + dkdh * d["egl"] + dkb[h] * d["beta"]
                dkd_kd = dkdh * d["kd"]
                rd = jnp.sum(dkd_kd, axis=1, keepdims=True)
                dgc = jnp.sum(dpl[h] * d["pl"] + dpm[h] * d["pm"] - dplt[h] * plt[h] - dpmt[h] * pmt[h]
                              + dqdh * d["qd"] + dkg[h] * d["kg"] - dkd_kd + dgx_ref[h, rows, :],
                              axis=1, keepdims=True)
                dgc8 = _put_col(dgc8, NHD + h, dgc)
                rd8 = _put_col(rd8, NHD + h, rd)
                dbeta = jnp.sum(dkb[h] * kn + dvb[h] * d["v"], axis=1, keepdims=True)
                dbeta8 = _put_col(dbeta8, h, dbeta)
                dqh = dqn[h] * (DHD ** -0.5)
                qh = d["qh"]
                dqr = d["rq"] * (dqh - qh * jnp.sum(dqh * qh, axis=1, keepdims=True))
                dkr = d["rk"] * (dkn - kn * jnp.sum(dkn * kn, axis=1, keepdims=True))
                cq = slice(DHD * h, DHD * (h + 1))
                ck = slice(DW + DHD * h, DW + DHD * (h + 1))
                cvv = slice(2 * DW + DHD * h, 2 * DW + DHD * (h + 1))
                dc_ref[rows, cq] = dqr * dsilu[rows, cq]
                dc_ref[rows, ck] = dkr * dsilu[rows, ck]
                dc_ref[rows, cvv] = dvb[h] * d["beta"] * dsilu[rows, cvv]
            dgc8 = dgc8 + jnp.where(K["last"], _ones_dot(K["blk_b"], rd8), 0.0)
            dg8 = _ones_dot(K["triu_b"], dgc8)
            sgm = _sigmoid(ba_g + dt_ref[...])
            dalpha = dg8 * (-jnp.exp(al_ref[...])) * sgm
            lane8 = _iota((GR, 8), 1)
            dba_ref[rows, :] = jnp.where(lane8 < NHD, dbeta8 * beta8 * (1.0 - beta8), dalpha)
            valid = lane8 >= NHD
            sm_ref[0:1, 0:8] += jnp.sum(jnp.where(valid, dg8 * g8, 0.0), axis=0, keepdims=True)
            sm_ref[1:2, 0:8] += jnp.sum(jnp.where(valid, dalpha, 0.0), axis=0, keepdims=True)

        dcv = dc_ref[...]
        xv = x_ref[...]
        nxt = nxt_ref[...]
        w = cw_ref[...]
        dx = dcv * w[CONV_K - 1:CONV_K, :]
        dcw_ref[CONV_K - 1:CONV_K, :] += jnp.sum(dcv * xv, axis=0, keepdims=True)
        for k in range(1, CONV_K):
            j = CONV_K - 1 - k
            up = _shift_up(dcv, nxt, k)
            dx = dx + up * w[j:j + 1, :]
            dcw_ref[j:j + 1, :] += jnp.sum(up * xv, axis=0, keepdims=True)
        dx_ref[...] = dx
        nxt_ref[...] = dcv[0:8]

    return pl.pallas_call(
        body, name="dn_post_bwd", grid=(nb,),
        in_specs=[rrows(1536), rrows(1536), _full((CONV_K, 1536)), rrows(8),
                  pl.BlockSpec((8, TG), lambda i: (0, nb - 1 - i)), _full((1, 8)), _full((1, 8)), _full((8, 1)),
                  _full((8, 1)),
                  hm_spec(DHD), hm_spec(DHD), hm_spec(DHD), hm_spec(DHD), hm_spec(DHD), hm_spec(DHD), hm_spec(DHD),
                  hm_spec(GR), hm_spec(DHD), hm_spec(DHD)],
        out_specs=[rrows(1536), rrows(8), _full((8, LANES)), _full((8, 1536))],
        out_shape=[jax.ShapeDtypeStruct((T, 1536), F32), jax.ShapeDtypeStruct((T, 8), F32),
                   jax.ShapeDtypeStruct((8, LANES), F32), jax.ShapeDtypeStruct((8, 1536), F32)],
        scratch_shapes=[pltpu.VMEM((TG, 1536), F32), pltpu.VMEM((8, 1536), F32)],
        compiler_params=_cp(("arbitrary",)),
    )(draw, cv, conv_w, ba, bat, alog8, dtb8, alog8t, dtb8t, du, dw, dqd, dkd, dgx, do, vn, tm, u, w)


def _rms(x):
    return lax.rsqrt(jnp.mean(x * x, axis=1, keepdims=True) + EPS)


def _rms_bwd(dy, xh, r, g):
    dxh = dy * g
    return r * (dxh - xh * jnp.mean(dxh * xh, axis=1, keepdims=True))


def _hm_rows(tm):
    return pl.BlockSpec((NHD, tm, DHD), lambda i: (0, i, 0))


def _post_mix(apre, o, z, x, w_out, g_a, g_dn):
    T = x.shape[0]

    def body(ap_ref, o_ref, z_ref, x_ref, w_ref, ga_ref, gd_ref, x1_ref, mix_ref):
        ap = ap_ref[...]
        parts = [ap * _rms(ap) * ga_ref[...]]
        zz = z_ref[...]
        for h in range(NHD):
            oh = o_ref[h]
            zh = zz[:, DHD * h:DHD * (h + 1)]
            parts.append(oh * _rms(oh) * gd_ref[...] * (zh * _sigmoid(zh)))
        mix = jnp.concatenate(parts, axis=1).astype(_MXU)
        mix_ref[...] = mix
        x1_ref[...] = x_ref[...] + jnp.dot(mix, w_ref[...], preferred_element_type=F32)

    return pl.pallas_call(
        body, name="post_mix", grid=(T // TM,),
        in_specs=[_rows(TM, AW), _hm_rows(TM), _rows(TM, DW), _rows(TM, D), _full((D, D)), _full((1, AW)),
                  _full((1, DHD))],
        out_specs=[_rows(TM, D), _rows(TM, D)],
        out_shape=[jax.ShapeDtypeStruct((T, D), F32), jax.ShapeDtypeStruct((T, D), _MXU)],
        compiler_params=_cp(("arbitrary",)),
    )(apre, o, z, x, w_out, g_a, g_dn)


def _ffn(x1, tgt, wl_all, g_ffn):
    T = x1.shape[0]
    SH = FF // N_DEV
    nt = (((1,), (1,)), ((), ()))

    def body(x_ref, t_ref, wl_hbm, g_ref,
             dx1_ref, dx1b_ref, h2_ref, act_ref, dgu_ref, dyb_ref, loss_ref, dg_ref, wg, wu, wd, sem):
        @pl.when(pl.program_id(0) == 0)
        def _():
            cps = [pltpu.make_async_copy(wl_hbm.at[dev, pl.ds(128 + SH * k, SH), :], dst.at[pl.ds(SH * dev, SH), :],
                                         sem.at[N_DEV * k + dev])
                   for k, dst in enumerate((wg, wu, wd)) for dev in range(N_DEV)]
            for cp in cps:
                cp.start()
            for cp in cps:
                cp.wait()
            loss_ref[...] = jnp.zeros_like(loss_ref)
            dg_ref[...] = jnp.zeros_like(dg_ref)

        xv = x_ref[...]
        r = _rms(xv)
        xh = xv * r
        gg = g_ref[...]
        h2 = (xh * gg).astype(_MXU)
        h2_ref[...] = h2
        gate = lax.dot_general(h2, wg[...], nt, preferred_element_type=F32)
        up = lax.dot_general(h2, wu[...], nt, preferred_element_type=F32)
        sg = _sigmoid(gate)
        silu = gate * sg
        act = (silu * up).astype(_MXU)
        act_ref[...] = act
        y = xv + jnp.dot(act, wd[...], preferred_element_type=F32)
        err = y - t_ref[...]
        loss_ref[...] += jnp.sum(err * err, axis=0, keepdims=True)
        dy = err * (1.0 / D)
        dyb = dy.astype(_MXU)
        dyb_ref[...] = dyb
        dact = lax.dot_general(dyb, wd[...], nt, preferred_element_type=F32)
        dgate = (dact * up * (sg * (1.0 + gate * (1.0 - sg)))).astype(_MXU)
        dup = (dact * silu).astype(_MXU)
        dgu_ref[:, 0:FF] = dgate
        dgu_ref[:, FF:2 * FF] = dup
        dh2 = (jnp.dot(dgate, wg[...], preferred_element_type=F32)
               + jnp.dot(dup, wu[...], preferred_element_type=F32))
        dg_ref[...] += jnp.sum(dh2 * xh, axis=0, keepdims=True)
        dx1 = dy + _rms_bwd(dh2, xh, r, gg)
        dx1_ref[...] = dx1
        dx1b_ref[...] = dx1.astype(_MXU)

    anyspec = pl.BlockSpec(memory_space=pl.ANY)
    sd = lambda w, dt: jax.ShapeDtypeStruct((T, w), dt)
    return pl.pallas_call(
        body, name="ffn", grid=(T // TF,),
        in_specs=[_rows(TF, D), _rows(TF, D), anyspec, _full((1, D))],
        out_specs=[_rows(TF, D), _rows(TF, D), _rows(TF, D), _rows(TF, FF), _rows(TF, 2 * FF), _rows(TF, D),
                   _full((1, D)), _full((1, D))],
        out_shape=[sd(D, F32), sd(D, _MXU), sd(D, _MXU), sd(FF, _MXU), sd(2 * FF, _MXU), sd(D, _MXU),
                   jax.ShapeDtypeStruct((1, D), F32), jax.ShapeDtypeStruct((1, D), F32)],
        scratch_shapes=[pltpu.VMEM((FF, D), _MXU)] * 3 + [pltpu.SemaphoreType.DMA((3 * N_DEV,))],
        compiler_params=_cp(("arbitrary",)),
    )(x1, tgt, wl_all, g_ffn)


def _mix_bwd(dx1b, w_out, apre, o, z, g_a, g_dn):
    T = dx1b.shape[0]

    def body(dx_ref, w_ref, ap_ref, o_ref, z_ref, ga_ref, gd_ref, dap_ref, do_ref, dz_ref, dga_ref, dgd_ref):
        @pl.when(pl.program_id(0) == 0)
        def _():
            dga_ref[...] = jnp.zeros_like(dga_ref)
            dgd_ref[...] = jnp.zeros_like(dgd_ref)

        dmix = lax.dot_general(dx_ref[...], w_ref[...], (((1,), (1,)), ((), ())), preferred_element_type=F32)
        ap = ap_ref[...]
        ra = _rms(ap)
        ah = ap * ra
        da = dmix[:, 0:AW]
        dga_ref[...] += jnp.sum(da * ah, axis=0, keepdims=True)
        dap_ref[...] = _rms_bwd(da, ah, ra, ga_ref[...])
        zz = z_ref[...]
        gd = gd_ref[...]
        for h in range(NHD):
            cs = slice(DHD * h, DHD * (h + 1))
            dd = dmix[:, AW + DHD * h:AW + DHD * (h + 1)]
            oh = o_ref[h]
            ro = _rms(oh)
            ohh = oh * ro
            zh = zz[:, cs]
            sz = _sigmoid(zh)
            dz_ref[:, cs] = dd * (ohh * gd) * (sz * (1.0 + zh * (1.0 - sz)))
            don = dd * (zh * sz)
            dgd_ref[...] += jnp.sum(don * ohh, axis=0, keepdims=True)
            do_ref[h] = _rms_bwd(don, ohh, ro, gd)

    return pl.pallas_call(
        body, name="mix_bwd", grid=(T // TM,),
        in_specs=[_rows(TM, D), _full((D, D)), _rows(TM, AW), _hm_rows(TM), _rows(TM, DW), _full((1, AW)),
                  _full((1, DHD))],
        out_specs=[_rows(TM, AW), _hm_rows(TM), _rows(TM, DW), _full((1, AW)), _full((1, DHD))],
        out_shape=[jax.ShapeDtypeStruct((T, AW), F32), jax.ShapeDtypeStruct((NHD, T, DHD), F32),
                   jax.ShapeDtypeStruct((T, DW), F32), jax.ShapeDtypeStruct((1, AW), F32),
                   jax.ShapeDtypeStruct((1, DHD), F32)],
        compiler_params=_cp(("arbitrary",)),
    )(dx1b, w_out, apre, o, z, g_a, g_dn)


DPW = 3712


def _inproj_bwd(dqn, dkn, dv, araw, ddraw, dz, dba, x, dx1, w_int, w_ba, g_mix, qg_t, kg_t):
    T = x.shape[0]

    def body(dqn_ref, dkn_ref, dv_ref, ar_ref, dd_ref, dz_ref, dba_ref, x_ref, dx1_ref, w_hbm, wba_ref, g_ref, qg_ref,
             kg_ref, dx_ref, dp_ref, dgm_ref, dqg_ref, dkg_ref, w_ref, w_sem):
        @pl.when(pl.program_id(0) == 0)
        def _():
            cp = pltpu.make_async_copy(w_hbm, w_ref, w_sem)
            cp.start()
            cp.wait()
            dgm_ref[...] = jnp.zeros_like(dgm_ref)
            dqg_ref[...] = jnp.zeros_like(dqg_ref)
            dkg_ref[...] = jnp.zeros_like(dkg_ref)

        bd = _block_ones(AW // 2, DHA)

        def head_norm_bwd(raw, dyn, gain, dg_ref):
            r = _head_rms(raw, bd, DHA)
            xh = raw * r
            dg_ref[...] += jnp.sum(dyn * xh, axis=0, keepdims=True)
            dxh = dyn * gain
            return r * (dxh - xh * (_head_sum(dxh * xh, bd) * (1.0 / DHA)))

        def segment(lo, val):
            vb = val.astype(_MXU)
            dp_ref[:, lo:lo + val.shape[1]] = vb
            return jnp.dot(vb, w_ref[lo:lo + val.shape[1], :], preferred_element_type=F32)

        dh = segment(1536, dd_ref[...]) + segment(2 * AW, dv_ref[...]) + segment(3072, dz_ref[...])
        dbab = dba_ref[...].astype(_MXU)
        dp_ref[:, 3584:DPW] = jnp.zeros((TM, DPW - 3584), _MXU)
        dp_ref[:, 3584:3592] = dbab
        dh = dh + lax.dot_general(dbab, wba_ref[...], (((1,), (1,)), ((), ())), preferred_element_type=F32)
        ar = ar_ref[...]
        dq = head_norm_bwd(ar[:, 0:AW], dqn_ref[...] * (DHA ** -0.5), qg_ref[...], dqg_ref)
        dk = head_norm_bwd(ar[:, AW:2 * AW], dkn_ref[...], kg_ref[...], dkg_ref)
        dh = dh + segment(0, dq) + segment(AW, dk)
        xv = x_ref[...]
        r = _rms(xv)
        xh = xv * r
        dgm_ref[...] += jnp.sum(dh * xh, axis=0, keepdims=True)
        dx_ref[...] = dx1_ref[...] + _rms_bwd(dh, xh, r, g_ref[...])

    return pl.pallas_call(
        body, name="inproj_bwd", grid=(T // TM,),
        in_specs=[_rows(TM, AW), _rows(TM, AW), _rows(TM, AW), _rows(TM, 2 * AW), _rows(TM, 1536), _rows(TM, DW),
                  _rows(TM, 8), _rows(TM, D), _rows(TM, D), pl.BlockSpec(memory_space=pl.ANY), _full((D, 8)),
                  _full((1, D)), _full((1, AW)), _full((1, AW))],
        out_specs=[_rows(TM, D), _rows(TM, DPW), _full((1, D)), _full((1, AW)), _full((1, AW))],
        out_shape=[jax.ShapeDtypeStruct((T, D), F32), jax.ShapeDtypeStruct((T, DPW), _MXU),
                   jax.ShapeDtypeStruct((1, D), F32), jax.ShapeDtypeStruct((1, AW), F32),
                   jax.ShapeDtypeStruct((1, AW), F32)],
        scratch_shapes=[pltpu.VMEM((3584, D), _MXU), pltpu.SemaphoreType.DMA],
        compiler_params=_cp(("arbitrary",)),
    )(dqn, dkn, dv, araw, ddraw, dz, dba, x, dx1, w_int, w_ba, g_mix, qg_t, kg_t)


def _wgrad(a, b, name, tk=1024, tn=None, out_dtype=F32, transposed=False, gather=None):
    T, M = a.shape
    N = b.shape[1]
    tn = N if tn is None else tn
    assert T % tk == 0 and N % tn == 0, (T, tk, N, tn)
    nk = T // tk
    hosted = gather is not None
    assert not hosted or (tn == N and nk >= 3)

    def body(a_ref, b_ref, *rest):
        k = pl.program_id(1)
        if hosted:
            g_ref, o_ref, g_all, acc, send_sems, recv_sems, local_sem = rest
            ag = _Gather(g_ref, g_all, send_sems, recv_sems, local_sem)
            pl.when(k == 0)(ag.start)
            pl.when(k == nk // 2)(ag.forward)
        else:
            o_ref, acc = rest

        @pl.when(k == 0)
        def _():
            acc[...] = jnp.zeros_like(acc)

        acc[...] += lax.dot_general(a_ref[...], b_ref[...], (((0,), (0,)), ((), ())), preferred_element_type=F32)

        @pl.when(k == nk - 1)
        def _():
            r = acc[...]
            o_ref[...] = (r.T if transposed else r).astype(out_dtype)
            if hosted:
                ag.finish()

    if transposed:
        out_spec, out_shape = pl.BlockSpec((tn, M), lambda j, k: (j, 0)), (N, M)
    else:
        out_spec, out_shape = pl.BlockSpec((M, tn), lambda j, k: (0, j)), (M, N)
    anyspec = pl.BlockSpec(memory_space=pl.ANY)
    return pl.pallas_call(
        body, name=name, grid=(N // tn, nk),
        in_specs=[pl.BlockSpec((tk, M), lambda j, k: (k, 0)), pl.BlockSpec((tk, tn), lambda j, k: (k, j))]
        + ([anyspec] if hosted else []),
        out_specs=[out_spec, anyspec] if hosted else out_spec,
        out_shape=([jax.ShapeDtypeStruct(out_shape, out_dtype),
                    jax.ShapeDtypeStruct((N_DEV,) + gather.shape, gather.dtype)] if hosted
                   else jax.ShapeDtypeStruct(out_shape, out_dtype)),
        scratch_shapes=[pltpu.VMEM((M, tn), F32)] + (_COMM_SEMS if hosted else []),
        compiler_params=_cp(("arbitrary", "arbitrary")),
    )(*((a, b) + ((gather,) if hosted else ())))


def _adamw(parts, w, m, v, name, tr, send=None):
    K, R, W = parts.shape
    n = R // tr

    def body(p_ref, w_ref, m_ref, v_ref, *rest):
        if send is not None:
            send_ref, g_ref, d_ref, nm_ref, nv_ref, recv_ref, send_sems, recv_sems, local_sem = rest
            sc = _Scatter(send_ref, recv_ref, send_sems, recv_sems, local_sem)
            pl.when(pl.program_id(0) == 0)(sc.start)
        else:
            g_ref, d_ref, nm_ref, nv_ref = rest
        g = p_ref[0].astype(F32)
        for k in range(1, K):
            g = g + p_ref[k].astype(F32)
        g_ref[...] = g
        nm = ADAM_B1 * m_ref[...] + (1.0 - ADAM_B1) * g
        nv = ADAM_B2 * v_ref[...] + (1.0 - ADAM_B2) * (g * g)
        nm_ref[...] = nm
        nv_ref[...] = nv
        m_hat = nm / (1.0 - ADAM_B1 ** ADAM_STEP)
        v_hat = nv / (1.0 - ADAM_B2 ** ADAM_STEP)
        d_ref[...] = -ADAM_LR * (m_hat / (jnp.sqrt(v_hat) + ADAM_EPS) + ADAM_WD * w_ref[...])
        if send is not None:
            pl.when(pl.program_id(0) == n - 1)(sc.finish)

    o = jax.ShapeDtypeStruct((R, W), F32)
    anyspec = pl.BlockSpec(memory_space=pl.ANY)
    hosted = send is not None
    return pl.pallas_call(
        body, name=name, grid=(n,),
        in_specs=[pl.BlockSpec((K, tr, W), lambda i: (0, i, 0)), _rows(tr, W), _rows(tr, W), _rows(tr, W)]
        + ([anyspec] if hosted else []),
        out_specs=[_rows(tr, W)] * 4 + ([anyspec] if hosted else []),
        out_shape=[o] * 4 + ([jax.ShapeDtypeStruct(send.shape, send.dtype)] if hosted else []),
        scratch_shapes=_COMM_SEMS if hosted else [],
        compiler_params=_cp(("arbitrary",)),
    )(*((parts, w, m, v) + ((send,) if hosted else ())))


SM_ROWS = 136
R_GMIX, R_GFFN, R_QG, R_KG, R_GA, R_GDN, R_ALOG, R_DT, R_LOSS, R_CONV, R_REL = 0, 8, 16, 24, 32, 40, 48, 49, 56, 64, 112


def _small_reduce(gathered):
    def body(p_ref, o_ref):
        s = p_ref[0]
        for k in range(1, N_DEV):
            s = s + p_ref[k]
        o_ref[...] = s
        for r0 in (R_QG, R_KG):
            rs = jnp.sum(s[r0:r0 + 4], axis=0, keepdims=True)
            o_ref[r0:r0 + 1, :] = rs + pltpu.roll(rs, DHA, 1)
        tot = jnp.sum(jnp.sum(s[R_LOSS:R_LOSS + 8], axis=0, keepdims=True), axis=1, keepdims=True)
        o_ref[R_LOSS:R_LOSS + 1, :] = jnp.broadcast_to(tot * (0.5 / D), (1, LANES))

    return pl.pallas_call(
        body, name="small_reduce",
        out_shape=jax.ShapeDtypeStruct((SM_ROWS, LANES), F32),
    )(gathered)


_WIRE = jnp.bfloat16
RA_USED, RA = 449, 464
RL = 128 + 3 * 352


def _pack_rows(parts, rows=None):
    p = jnp.concatenate([t.reshape(-1, D) for t in parts], axis=0) if len(parts) > 1 else parts[0].reshape(-1, D)
    return p if rows is None else jnp.pad(p, ((0, rows - p.shape[0]), (0, 0)))


def _unpack_rows(packed, shapes):
    out, r = [], 0
    for shp in shapes:
        nr = math.prod(shp) // D
        out.append(packed[r:r + nr].reshape(shp))
        r += nr
    return out


def _pad8(t):
    return jnp.pad(t, ((0, (-t.shape[0]) % 8), (0, 0)))


PART = 8 * LANES


def _pack_lanes(parts):
    rows = []
    for p in parts:
        f = p.reshape(-1)
        rows.append(jnp.pad(f, (0, (-f.shape[0]) % PART)).reshape(-1, LANES))
    return jnp.concatenate(rows, axis=0)


def _unpack_lanes(packed, shapes):
    out, r = [], 0
    for shp in shapes:
        n = math.prod(shp)
        nr = 8 * -(-n // PART)
        out.append(packed[r:r + nr].reshape(-1)[:n].reshape(shp))
        r += nr
    return out


def kernel(x, norm_mix_g, w_in, attn_q_norm_g, attn_k_norm_g, rel_bias, attn_out_norm_g, conv_w, a_log, dt_bias, dn_out_norm_g, w_out, norm_ffn_g, w_gate, w_up, w_down, loss_target, m_norm_mix_g, m_w_in, m_attn_q_norm_g, m_attn_k_norm_g, m_rel_bias, m_attn_out_norm_g, m_conv_w, m_a_log, m_dt_bias, m_dn_out_norm_g, m_w_out, m_norm_ffn_g, m_w_gate, m_w_up, m_w_down, v_norm_mix_g, v_w_in, v_attn_q_norm_g, v_attn_k_norm_g, v_rel_bias, v_attn_out_norm_g, v_conv_w, v_a_log, v_dt_bias, v_dn_out_norm_g, v_w_out, v_norm_ffn_g, v_w_gate, v_w_up, v_w_down):
    xs, tgt = x[0], loss_target[0]
    my_idx = 4 * lax.axis_index("x") + 2 * lax.axis_index("y") + lax.axis_index("c")
    late_w = (w_out[0], w_gate[0], w_up[0], w_down[0])

    tab, tabt, wa_all, cw_all = _bias_tables(
        jnp.pad(rel_bias[0].T, ((0, 0), (0, VAR0 - 257))), _pack_rows([w_in[0].T.astype(_MXU)], RA),
        jnp.pad(conv_w[0], ((0, 4), (0, 64))))
    W_in_t = wa_all[:, 0:RA_USED].reshape(N_DEV * RA_USED, D)
    W_int, W_ba = W_in_t[0:3584], W_in_t[3584:3592].T
    conv_full = cw_all[:, 0:CONV_K, 0:192].transpose(1, 0, 2).reshape(CONV_K, 1536)

    qg_t = jnp.tile(attn_q_norm_g, (1, NHA))
    kg_t = jnp.tile(attn_k_norm_g, (1, NHA))
    z4 = jnp.zeros((1, NHD), F32)
    alog8 = jnp.concatenate([z4, a_log], axis=1)
    dtb8 = jnp.concatenate([z4, dt_bias], axis=1)

    late_t = lambda ts: (ts[0], ts[1].T, ts[2].T, ts[3])
    araw, an, draw, z, ba, hb, wl_all = _inproj(xs, norm_mix_g, W_int, W_ba, qg_t, kg_t,
                                                _pack_rows([w.astype(_MXU) for w in late_t(late_w)]))
    W_out = wl_all[:, 0:128].reshape(D, D)
    apre = _attn_fwd(an, tab)
    bat = ba.T
    dn_args = (draw, conv_full, ba, bat, alog8, dtb8, alog8.T, dtb8.T)
    u, w, kd, tm, wq, km, mq, wt, elb, cv = _dn_prep(*dn_args)
    o, vn, sn = _dn_scan(u, wq, km, elb)
    x1, mix = _post_mix(apre, o, z, xs, W_out, attn_out_norm_g, dn_out_norm_g)

    dx1, dx1b, h2, act, dgu, dyb, loss_row, dgffn = _ffn(x1, tgt, wl_all, norm_ffn_g)

    gW_out = _wgrad(mix, dx1b, "wgrad_out", out_dtype=_WIRE)
    gW_gu_t = _wgrad(h2, dgu, "wgrad_gate_up", tn=FF, out_dtype=_WIRE, transposed=True)
    gW_down = _wgrad(dyb, act, "wgrad_down", out_dtype=_WIRE, transposed=True)
    send_late = jnp.concatenate(
        [gW_out.reshape(N_DEV, 128, D), gW_gu_t[0:FF].reshape(N_DEV, 352, D), gW_gu_t[FF:].reshape(N_DEV, 352, D),
         gW_down.reshape(N_DEV, 352, D)], axis=1)

    dap, do, dz, dga, dgdn = _mix_bwd(dx1b, W_out, apre, o, z, attn_out_norm_g, dn_out_norm_g)
    dqn, dkn, dv, dtabt, recv_late = _attn_bwd(an, dap, tabt, send_late)
    drel = _bias_grad(dtabt)
    du, dw, dqd, dkd, dgx = _dn_scan_bwd(do, mq, kd, wt, sn, vn, elb)
    ddraw, dba, sm, dcw = _dn_post_bwd(draw, cv, *dn_args[1:], du, dw, dqd, dkd, dgx, do, vn, tm, u, w)
    gx, dproj, dgmix, dqg, dkg = _inproj_bwd(dqn, dkn, dv, araw, ddraw, dz, dba, xs, dx1, W_int, W_ba, norm_mix_g,
                                             qg_t, kg_t)

    partial = jnp.concatenate(
        [dgmix.reshape(8, LANES), dgffn.reshape(8, LANES), _pad8(dqg.reshape(4, LANES)), _pad8(dkg.reshape(4, LANES)),
         _pad8(dga.reshape(4, LANES)), _pad8(dgdn), sm, loss_row.reshape(8, LANES),
         dcw[0:CONV_K].reshape(48, LANES), drel.reshape(24, LANES)], axis=0)
    gW_in_t, partial_all = _wgrad(hb, dproj, "wgrad_in", tk=1024, out_dtype=_WIRE, transposed=True,
                                  gather=partial)
    send_in = jnp.pad(gW_in_t[0:N_DEV * RA_USED].reshape(N_DEV, RA_USED, D), ((0, 0), (0, RA - RA_USED), (0, 0)))
    late_m = (m_w_out[0], m_w_gate[0], m_w_up[0], m_w_down[0])
    late_v = (v_w_out[0], v_w_gate[0], v_w_up[0], v_w_down[0])
    *outs_late, recv_in = _adamw(recv_late, _pack_rows(late_t(late_w)), _pack_rows(late_t(late_m)),
                                 _pack_rows(late_t(late_v)), "adamw_late", 32, send=send_in)
    outs_in = _adamw(recv_in, _pack_rows([w_in[0].T], RA), _pack_rows([m_w_in[0].T], RA),
                     _pack_rows([v_w_in[0].T], RA), "adamw_w_in", 16)
    late_t_shapes = [t.shape for t in late_t(late_w)]
    big = [[a[0:RA_USED].T] + list(late_t(_unpack_rows(b, late_t_shapes))) for a, b in zip(outs_in, outs_late)]
    bg, bd_, bm, bv = big

    S = _small_reduce(partial_all)
    loss = S[R_LOSS, 0]
    g_conv = lax.dynamic_slice(S[R_CONV:R_CONV + 48].reshape(CONV_K, 1536), (0, 192 * my_idx), (CONV_K, 192))
    sg = [S[R_GMIX:R_GMIX + 8].reshape(1, D), S[R_QG:R_QG + 1, 0:DHA], S[R_KG:R_KG + 1, 0:DHA],
          S[R_REL:R_REL + 24].reshape(NHA, 384)[:, 0:257].T, S[R_GA:R_GA + 4].reshape(1, AW), g_conv,
          S[R_ALOG:R_ALOG + 1, NHD:2 * NHD], S[R_DT:R_DT + 1, NHD:2 * NHD], S[R_GDN:R_GDN + 1], S[R_GFFN:R_GFFN + 8].reshape(1, D)]
    sw = [norm_mix_g, attn_q_norm_g, attn_k_norm_g, rel_bias[0], attn_out_norm_g, conv_w[0], a_log, dt_bias, dn_out_norm_g, norm_ffn_g]
    smm = [m_norm_mix_g, m_attn_q_norm_g, m_attn_k_norm_g, m_rel_bias[0], m_attn_out_norm_g, m_conv_w[0], m_a_log, m_dt_bias, m_dn_out_norm_g, m_norm_ffn_g]
    svv = [v_norm_mix_g, v_attn_q_norm_g, v_attn_k_norm_g, v_rel_bias[0], v_attn_out_norm_g, v_conv_w[0], v_a_log, v_dt_bias, v_dn_out_norm_g, v_norm_ffn_g]
    s_shapes = [t.shape for t in sw]
    pg = _pack_lanes(sg)
    s_out = _adamw(pg[None], _pack_lanes(sw), _pack_lanes(smm), _pack_lanes(svv), "adamw_small", pg.shape[0])
    s_g, s_d, s_m, s_v = (_unpack_lanes(t, s_shapes) for t in s_out)

    lead = lambda t: t[None]
    def ordered(small, big):
        nm, q, k, rel, ao, cw, al, dtb, dno, nf = small
        wi, wo, wgt, wu, wdn = big
        return [nm, lead(wi), q, k, lead(rel), ao, lead(cw), al, dtb, dno, lead(wo), nf, lead(wgt), lead(wu), lead(wdn)]
    outs = [loss, gx[None]]
    for small, big in ((s_g, bg), (s_d, bd_), (s_m, bm), (s_v, bv)):
        outs += ordered(small, big)
    return tuple(outs)
```

```python
import math

import jax
import jax.numpy as jnp
from jax import lax
from jax.experimental import pallas as pl
from jax.experimental.pallas import tpu as pltpu

F32 = jnp.float32
BF16 = jnp.bfloat16
_MXU = jnp.bfloat16

D = 1024
AW = 512
NHA = 8
DHA = 64
CH = 64
NHD = 4
DHD = 128
DW = 512
FF = 2816
EPS = 1e-6
NEG = -1e30
N_DEV = 8
LANES = 128
VMEM_LIMIT = 56 * 1024 * 1024

ADAM_LR = 0.001
ADAM_B1 = 0.9
ADAM_B2 = 0.999
ADAM_EPS = 1e-08
ADAM_WD = 0.01
ADAM_STEP = 10

MESH_T = pl.DeviceIdType.MESH


def _cp(sem=None, vmem=VMEM_LIMIT):
    kw = dict(vmem_limit_bytes=vmem)
    if sem is not None:
        kw["dimension_semantics"] = sem
    return pltpu.CompilerParams(**kw)


def _dot(a, b):
    return jnp.dot(a.astype(_MXU), b.astype(_MXU), preferred_element_type=F32)


def _dot_nt(a, b):
    return lax.dot_general(a.astype(_MXU), b.astype(_MXU), (((1,), (1,)), ((), ())), preferred_element_type=F32)


def _dot_tn(a, b):
    return lax.dot_general(a.astype(_MXU), b.astype(_MXU), (((0,), (0,)), ((), ())), preferred_element_type=F32)


def _iota(shape, dim):
    return lax.broadcasted_iota(jnp.int32, shape, dim)


def _block_ones(n, blk, dtype=BF16):
    r, c = _iota((n, n), 0), _iota((n, n), 1)
    return jnp.where((r // blk) == (c // blk), 1.0, 0.0).astype(dtype)


def _sigmoid(x):
    return 1.0 / (1.0 + jnp.exp(-x))


def _softplus(x):
    return jnp.maximum(x, 0.0) + jnp.log(1.0 + jnp.exp(-jnp.abs(x)))


def _col(x, k):
    lane = _iota(x.shape, 1)
    return jnp.sum(jnp.where(lane == k, x, 0.0), axis=1, keepdims=True)


def _row(x, k):
    sub = _iota(x.shape, 0)
    return jnp.sum(jnp.where(sub == k, x, 0.0), axis=0, keepdims=True)


def _my_pos():
    return lax.axis_index("x"), lax.axis_index("y"), lax.axis_index("c")


_COMM_SEMS = [pltpu.SemaphoreType.DMA((7,)), pltpu.SemaphoreType.DMA((7,)), pltpu.SemaphoreType.DMA]


class _Gather:
    def __init__(self, x_ref, out_ref, send_sems, recv_sems, local_sem):
        x, y, c = _my_pos()
        me, sibling = (x, y, c), (x, y, 1 - c)
        chips = [(1 - x, y), (x, 1 - y), (1 - x, 1 - y)]

        def slot(px, py, pc):
            return out_ref.at[4 * px + 2 * py + pc]

        def copy(k, block, to, src=None):
            return pltpu.make_async_remote_copy(
                src_ref=slot(*block) if src is None else src, dst_ref=slot(*block),
                send_sem=send_sems.at[k], recv_sem=recv_sems.at[k], device_id=to, device_id_type=MESH_T)

        self.mine = pltpu.make_async_copy(x_ref, slot(*me), local_sem)
        self.first = [copy(0, me, sibling, src=x_ref)]
        self.first += [copy(1 + j, me, (*chip, c), src=x_ref) for j, chip in enumerate(chips)]
        self.passed = [copy(4 + j, (*chip, c), sibling) for j, chip in enumerate(chips)]
        self.from_chips = [copy(1 + j, (*chip, c), me) for j, chip in enumerate(chips)]
        self.from_sibling = [copy(0, sibling, me)] + [copy(4 + j, (*chip, 1 - c), me) for j, chip in enumerate(chips)]

    def start(self):
        self.mine.start()
        for cp in self.first:
            cp.start()

    def forward(self):
        for arrived, onward in zip(self.from_chips, self.passed):
            arrived.wait_recv()
            onward.start()

    def finish(self):
        for cp in self.from_sibling:
            cp.wait_recv()
        for cp in self.first + self.passed:
            cp.wait_send()
        self.mine.wait()


class _Scatter:
    def __init__(self, s_ref, r_ref, send_sems, recv_sems, local_sem):
        x, y, c = _my_pos()
        self.mine = pltpu.make_async_copy(s_ref.at[4 * x + 2 * y + c], r_ref.at[0], local_sem)
        self.copies = []
        for m in range(1, N_DEV):
            px = x ^ ((m >> 2) & 1)
            py = y ^ ((m >> 1) & 1)
            pc = c ^ (m & 1)
            self.copies.append(pltpu.make_async_remote_copy(
                src_ref=s_ref.at[4 * px + 2 * py + pc], dst_ref=r_ref.at[m],
                send_sem=send_sems.at[m - 1], recv_sem=recv_sems.at[m - 1],
                device_id=(px, py, pc), device_id_type=MESH_T))

    def start(self):
        self.mine.start()
        for cp in self.copies:
            cp.start()

    def finish(self):
        for cp in self.copies:
            cp.wait_recv()
        for cp in self.copies:
            cp.wait_send()
        self.mine.wait()


TM = 512
TF = 256
TG = 512


def _full(shape):
    nd = len(shape)
    return pl.BlockSpec(shape, lambda i: (0,) * nd)


def _rows(tm, w):
    return pl.BlockSpec((tm, w), lambda i: (i, 0))


def _head_sum(x, bd):
    one_pass = lambda t: jnp.dot(t.astype(_MXU), bd.astype(_MXU), preferred_element_type=F32)
    return jnp.concatenate([one_pass(x[:, 0:256]), one_pass(x[:, 256:512])], axis=1)


def _head_rms(x, bd, width):
    return lax.rsqrt(_head_sum(x * x, bd) * (1.0 / width) + EPS)


def _inproj(x, g_mix, w_int, w_ba, qg_t, kg_t, later_w):
    T = x.shape[0]
    nt = T // TM
    ntd = (((1,), (1,)), ((), ()))

    def body(x_ref, g_ref, w_ref, wba_ref, qg_ref, kg_ref, lw_ref, araw_ref, an_ref, draw_ref, z_ref, ba_ref, h_ref,
             lw_all, send_sems, recv_sems, local_sem):
        i = pl.program_id(0)
        ag = _Gather(lw_ref, lw_all, send_sems, recv_sems, local_sem)
        pl.when(i == 0)(ag.start)
        pl.when(i == nt // 2)(ag.forward)
        xv = x_ref[...]
        r = lax.rsqrt(jnp.mean(xv * xv, axis=1, keepdims=True) + EPS)
        h = (xv * r * g_ref[...]).astype(_MXU)
        h_ref[...] = h
        proj = lambda lo, hi: lax.dot_general(h, w_ref[lo:hi, :], ntd, preferred_element_type=F32)
        q, k, v = proj(0, AW), proj(AW, 2 * AW), proj(2 * AW, 3 * AW)
        draw_ref[...] = proj(1536, 3072)
        z_ref[...] = proj(3072, 3584)
        ba_ref[...] = jnp.dot(h, wba_ref[...], preferred_element_type=F32)
        araw_ref[:, 0:AW] = q
        araw_ref[:, AW:2 * AW] = k
        bd = _block_ones(AW // 2, DHA)
        qn = q * _head_rms(q, bd, DHA) * (qg_ref[...] * (DHA ** -0.5))
        kn = k * _head_rms(k, bd, DHA) * kg_ref[...]
        an_ref[:, 0:AW] = qn.astype(_MXU)
        an_ref[:, AW:2 * AW] = kn.astype(_MXU)
        an_ref[:, 2 * AW:3 * AW] = v.astype(_MXU)
        pl.when(i == nt - 1)(ag.finish)

    anyspec = pl.BlockSpec(memory_space=pl.ANY)
    return pl.pallas_call(
        body, name="inproj", grid=(nt,),
        in_specs=[_rows(TM, D), _full((1, D)), _full((3584, D)), _full((D, 8)), _full((1, AW)), _full((1, AW)),
                  anyspec],
        out_specs=[_rows(TM, 2 * AW), _rows(TM, 1536), _rows(TM, 1536), _rows(TM, DW), _rows(TM, 8), _rows(TM, D),
                   anyspec],
        out_shape=[jax.ShapeDtypeStruct((T, 2 * AW), F32), jax.ShapeDtypeStruct((T, 1536), _MXU),
                   jax.ShapeDtypeStruct((T, 1536), F32), jax.ShapeDtypeStruct((T, DW), F32),
                   jax.ShapeDtypeStruct((T, 8), F32), jax.ShapeDtypeStruct((T, D), _MXU),
                   jax.ShapeDtypeStruct((N_DEV,) + later_w.shape, later_w.dtype)],
        scratch_shapes=_COMM_SEMS,
        compiler_params=_cp(("arbitrary",)),
    )(x, g_mix, w_int, w_ba, qg_t, kg_t, later_w)


TQ = 256
TW = 768
VAR0 = 384
TOEP = 1024


def _bias_tables(rb_t, w_shard, c_shard):
    def body(rb_ref, w_ref, c_ref, tab_ref, tabt_ref, w_all, c_all, ws, wr, wl, cs, cr, cl):
        h = pl.program_id(0)
        gathers = [_Gather(w_ref, w_all, ws, wr, wl), _Gather(c_ref, c_all, cs, cr, cl)]

        @pl.when(h == 0)
        def _():
            for ag in gathers:
                ag.start()

        @pl.when(h == NHA // 2)
        def _():
            for ag in gathers:
                ag.forward()

        rb8 = jnp.broadcast_to(_row(rb_ref[...], h), (8, VAR0))
        n = _iota((VAR0, TOEP), 1)
        t = _iota((VAR0, TOEP), 0)

        def line(m):
            onehot = jnp.where(jnp.clip(512 - m, -128, 128) + 128 == t, 1.0, 0.0).astype(BF16)
            return sum(jnp.dot(p, onehot, preferred_element_type=F32) for p in _split3(rb8))[0:1, :]

        def band(r, j, first_key):
            return ((j >> 6) >= (r >> 6)) & ((j >> 6) <= (r >> 6) + 8) & (j >= first_key)

        g = line(jnp.where(n < TW, n, n - TOEP))
        tab = pltpu.roll(jnp.broadcast_to(g, (TQ, TOEP)), 0, 1, stride=1, stride_axis=0)[:, 0:TW]
        gt = line(jnp.where(n < TQ, -n, TOEP - n))
        tabt = pltpu.roll(jnp.broadcast_to(gt, (TW, TOEP)), 0, 1, stride=1, stride_axis=0)[:, 0:TQ]
        for v in range(3):
            first_key = max(512 - TQ * v, 0)
            tab_ref[v, 0] = jnp.where(band(_iota((TQ, TW), 0), _iota((TQ, TW), 1), first_key), tab, NEG)
            tabt_ref[v, 0] = jnp.where(band(_iota((TW, TQ), 1), _iota((TW, TQ), 0), first_key), tabt, NEG)

        @pl.when(h == NHA - 1)
        def _():
            for ag in gathers:
                ag.finish()

    anyspec = pl.BlockSpec(memory_space=pl.ANY)
    return pl.pallas_call(
        body, name="bias_tables", grid=(NHA,),
        in_specs=[_full((NHA, VAR0)), anyspec, anyspec],
        out_specs=[pl.BlockSpec((3, 1, TQ, TW), lambda h: (0, h, 0, 0)),
                   pl.BlockSpec((3, 1, TW, TQ), lambda h: (0, h, 0, 0)), anyspec, anyspec],
        out_shape=[jax.ShapeDtypeStruct((3, NHA, TQ, TW), F32), jax.ShapeDtypeStruct((3, NHA, TW, TQ), F32),
                   jax.ShapeDtypeStruct((N_DEV,) + w_shard.shape, w_shard.dtype),
                   jax.ShapeDtypeStruct((N_DEV,) + c_shard.shape, c_shard.dtype)],
        scratch_shapes=_COMM_SEMS + _COMM_SEMS,
        compiler_params=_cp(("arbitrary",)),
    )(rb_t, w_shard, c_shard)


def _bias_grad(dtabt):
    def body(d_ref, o_ref):
        a, b = _iota((TQ, TQ), 0), _iota((TQ, TQ), 1)
        anti = jnp.where(a + b == TQ - 1, 1.0, 0.0).astype(BF16)
        drev = sum(jnp.dot(t, anti, preferred_element_type=F32) for t in _split3(d_ref[0]))
        wide = jnp.concatenate([drev, jnp.zeros((TW, TOEP - TQ), F32)], axis=1)
        cols = jnp.sum(pltpu.roll(wide, 0, 1, stride=1, stride_axis=0), axis=0, keepdims=True)
        c = _iota((TOEP, VAR0), 0)
        idx = jnp.clip(512 + TQ - 1 - c, -128, 128) + 128
        onehot = jnp.where(idx == _iota((TOEP, VAR0), 1), 1.0, 0.0).astype(BF16)
        cols8 = jnp.broadcast_to(cols, (8, TOEP))
        o_ref[0] = sum(jnp.dot(t, onehot, preferred_element_type=F32) for t in _split3(cols8))[0:1, :]

    return pl.pallas_call(
        body, name="bias_grad", grid=(NHA,),
        in_specs=[pl.BlockSpec((1, TW, TQ), lambda h: (h, 0, 0))],
        out_specs=pl.BlockSpec((1, 1, VAR0), lambda h: (h, 0, 0)),
        out_shape=jax.ShapeDtypeStruct((NHA, 1, VAR0), F32),
        compiler_params=_cp(("arbitrary",)),
    )(dtabt)


def _kv_spec(col, back):
    return pl.BlockSpec((TQ, AW), lambda i: (jnp.maximum(i - back, 0), col))


def _attn_fwd(an, tab):
    T = an.shape[0]

    def body(q_ref, k2_ref, k1_ref, k0_ref, v2_ref, v1_ref, v0_ref, tab_ref, o_ref):
        i = pl.program_id(0)
        lo_half = _iota((TQ, LANES), 1) < DHA

        cache = {}

        def window(refs, sl):
            key = (id(refs[0]), sl.start)
            if key not in cache:
                cache[key] = jnp.concatenate([r[:, sl] for r in refs], axis=0)
            return cache[key]

        def scores(h):
            sl = slice(LANES * (h // 2), LANES * (h // 2 + 1))
            mask = lo_half if h % 2 == 0 else jnp.logical_not(lo_half)
            qp = q_ref[:, sl]
            qm = jnp.where(mask, qp, jnp.zeros_like(qp))
            return _dot_nt(qm, window((k2_ref, k1_ref, k0_ref), sl)) + tab_ref[0, h]

        s_next = scores(0)
        outs = []
        for h in range(NHA):
            s = s_next
            if h + 1 < NHA:
                s_next = scores(h + 1)
            sl = slice(LANES * (h // 2), LANES * (h // 2 + 1))
            m = jnp.max(s, axis=1, keepdims=True)
            e = jnp.exp(s - m)
            l = jnp.sum(e, axis=1, keepdims=True)
            outs.append(_dot(e, window((v2_ref, v1_ref, v0_ref), sl)) / l)
            if h % 2 == 1:
                o_ref[:, sl] = jnp.where(lo_half, outs[h - 1], outs[h])

    return pl.pallas_call(
        body, name="attn_fwd", grid=(T // TQ,),
        in_specs=[pl.BlockSpec((TQ, AW), lambda i: (i, 0)),
                  _kv_spec(1, 2), _kv_spec(1, 1), _kv_spec(1, 0), _kv_spec(2, 2), _kv_spec(2, 1), _kv_spec(2, 0),
                  pl.BlockSpec((1, NHA, TQ, TW), lambda i: (jnp.minimum(i, 2), 0, 0, 0))],
        out_specs=_rows(TQ, AW),
        out_shape=jax.ShapeDtypeStruct((T, AW), F32),
        compiler_params=_cp(("arbitrary",)),
    )(an, an, an, an, an, an, an, tab)


def _attn_bwd(an, dout, tabt, send):
    T = an.shape[0]
    nq = T // TQ

    def qi(i):
        return jnp.minimum(i, nq - 1)

    def kv_spec(col, back):
        return pl.BlockSpec((TQ, AW), lambda i: (jnp.maximum(qi(i) - back, 0), col))

    def body(q_ref, do_ref, k2_ref, k1_ref, k0_ref, v2_ref, v1_ref, v0_ref, tabt_ref, send_ref,
             dq_ref, dk_ref, dv_ref, dtab_ref, recv_ref, dk_acc, dv_acc, send_sems, recv_sems, local_sem):
        i = pl.program_id(0)
        sc = _Scatter(send_ref, recv_ref, send_sems, recv_sems, local_sem)
        pl.when(i == 0)(sc.start)

        @pl.when(i == 0)
        def _():
            dtab_ref[...] = jnp.zeros_like(dtab_ref)

        new = i % 3
        dk_acc[new] = jnp.zeros((TQ, AW), F32)
        dv_acc[new] = jnp.zeros((TQ, AW), F32)

        @pl.when(i < nq)
        def _():
            kwin = jnp.concatenate([k2_ref[...], k1_ref[...], k0_ref[...]], axis=0)
            vwin = jnp.concatenate([v2_ref[...], v1_ref[...], v0_ref[...]], axis=0)
            q = q_ref[...]
            do = do_ref[...].astype(_MXU)
            lo_half = _iota((TQ, LANES), 1) < DHA

            def front(h):
                sl = slice(LANES * (h // 2), LANES * (h // 2 + 1))
                mask = lo_half if h % 2 == 0 else jnp.logical_not(lo_half)
                zero = jnp.zeros((TQ, LANES), q.dtype)
                qm = jnp.where(mask, q[:, sl], zero)
                dom = jnp.where(mask, do[:, sl], zero)
                st = _dot_nt(kwin[:, sl], qm) + tabt_ref[0, h]
                return st, _dot_nt(vwin[:, sl], dom), qm, dom, mask

            pairs = {}

            def back(h, ptb, dsb, qm, dom, mask):
                sl = slice(LANES * (h // 2), LANES * (h // 2 + 1))
                dv = _dot(ptb, dom)
                dk = _dot(dsb, qm)
                dq = jnp.where(mask, _dot_tn(dsb, kwin[:, sl]), 0.0)
                if h % 2 == 0:
                    pairs[h // 2] = (dq, dk, dv)
                    return
                dq0, dk0, dv0 = pairs.pop(h // 2)
                dq_ref[:, sl] = dq0 + dq
                dk_pair, dv_pair = dk0 + dk, dv0 + dv
                for w in range(3):
                    slot = (i + 1 + w) % 3
                    rows = slice(TQ * w, TQ * (w + 1))
                    dk_acc[slot, :, sl] += dk_pair[rows]
                    dv_acc[slot, :, sl] += dv_pair[rows]

            nxt = front(0)
            pending = None
            for h in range(NHA):
                st, dpt, qm, dom, mask = nxt
                if h + 1 < NHA:
                    nxt = front(h + 1)
                m = jnp.max(st, axis=0, keepdims=True)
                e = jnp.exp(st - m)
                pt = e * (1.0 / jnp.sum(e, axis=0, keepdims=True))
                delta = jnp.sum(pt * dpt, axis=0, keepdims=True)
                dst = pt * (dpt - delta)
                dtab_ref[h] += dst
                if pending is not None:
                    back(*pending)
                pending = (h, pt.astype(_MXU), dst.astype(_MXU), qm, dom, mask)
            back(*pending)

        @pl.when(i >= 2)
        def _():
            done = (i + 1) % 3
            dk_ref[...] = dk_acc[done]
            dv_ref[...] = dv_acc[done]

        pl.when(i == nq + 1)(sc.finish)

    back2 = pl.BlockSpec((TQ, AW), lambda i: (jnp.maximum(i - 2, 0), 0))
    anyspec = pl.BlockSpec(memory_space=pl.ANY)
    return pl.pallas_call(
        body, name="attn_bwd", grid=(nq + 2,),
        in_specs=[pl.BlockSpec((TQ, AW), lambda i: (qi(i), 0)), pl.BlockSpec((TQ, AW), lambda i: (qi(i), 0)),
                  kv_spec(1, 2), kv_spec(1, 1), kv_spec(1, 0), kv_spec(2, 2), kv_spec(2, 1), kv_spec(2, 0),
                  pl.BlockSpec((1, NHA, TW, TQ), lambda i: (jnp.minimum(i, 2), 0, 0, 0)), anyspec],
        out_specs=[pl.BlockSpec((TQ, AW), lambda i: (qi(i), 0)), back2, back2, _full((NHA, TW, TQ)), anyspec],
        out_shape=[jax.ShapeDtypeStruct((T, AW), F32), jax.ShapeDtypeStruct((T, AW), F32),
                   jax.ShapeDtypeStruct((T, AW), F32), jax.ShapeDtypeStruct((NHA, TW, TQ), F32),
                   jax.ShapeDtypeStruct(send.shape, send.dtype)],
        scratch_shapes=[pltpu.VMEM((3, TQ, AW), F32), pltpu.VMEM((3, TQ, AW), F32)] + _COMM_SEMS,
        compiler_params=_cp(("arbitrary",)),
    )(an, dout, an, an, an, an, an, an, tabt, send)


GR = 128
NG = TG // GR
CPT = TG // CH
CONV_K = 4


def _split3(x):
    a = x.astype(BF16)
    r = x - a.astype(F32)
    b = r.astype(BF16)
    c = (r - b.astype(F32)).astype(BF16)
    return a, b, c


def _ones_dot(ones_b, x):
    return sum(jnp.dot(ones_b, t, preferred_element_type=F32) for t in _split3(x))


def _dot_ones_nt(x, ones_b):
    dn = (((1,), (1,)), ((), ()))
    return sum(lax.dot_general(t, ones_b, dn, preferred_element_type=F32) for t in _split3(x))


def _dn_masks():
    r, c = _iota((GR, GR), 0), _iota((GR, GR), 1)
    same = (r >> 6) == (c >> 6)
    one = lambda m: jnp.where(m, 1.0, 0.0).astype(BF16)
    return dict(
        tril=same & (c <= r), strict=same & (c < r), triu=same & (c >= r), strict_u=same & (c > r),
        tril_b=one(same & (c <= r)), triu_b=one(same & (c >= r)), blk_b=one(same), eye_b=one(r == c),
        eye=jnp.where(r == c, 1.0, 0.0).astype(F32),
        fold_b=one((_iota((GR, CH), 0) & (CH - 1)) == _iota((GR, CH), 1)),
        last=(_iota((GR, 1), 0) & (CH - 1)) == CH - 1,
    )


def _shift_down(x, halo, k):
    if k == 0:
        return x
    xs = pltpu.roll(x, k, 0)
    hs = pltpu.roll(halo, k, 0)
    top = jnp.where(_iota(halo.shape, 0) < k, hs, xs[0:8])
    return jnp.concatenate([top, xs[8:]], axis=0)


def _shift_up(x, halo, k):
    if k == 0:
        return x
    n = x.shape[0]
    xs = pltpu.roll(x, n - k, 0)
    hs = pltpu.roll(halo, 8 - k, 0)
    bot = jnp.where(_iota(halo.shape, 0) >= 8 - k, hs, xs[n - 8:n])
    return jnp.concatenate([xs[0:n - 8], bot], axis=0)


def _conv(x, halo, w):
    y = x * w[CONV_K - 1:CONV_K, :]
    for k in range(1, CONV_K):
        y = y + _shift_down(x, halo, k) * w[CONV_K - 1 - k:CONV_K - k, :]
    return y


def _tri_inv(lmats, eye):
    ps = [-m for m in lmats]
    rs = [eye + p for p in ps]
    for _ in range(5):
        ps = [_dot(p, p) for p in ps]
        rs = [r + _dot(r, p) for r, p in zip(rs, ps)]
    return rs


def _gate_terms(ba_g, bat_g, alog8, dtb8, alog8t, dtb8t, K):
    g8 = -jnp.exp(alog8) * _softplus(ba_g + dtb8)
    g8t = -jnp.exp(alog8t) * _softplus(bat_g + dtb8t)
    gc8 = _ones_dot(K["tril_b"], g8)
    gl8 = _ones_dot(K["blk_b"], g8)
    gcrow8 = _dot_ones_nt(g8t, K["tril_b"])
    return g8, gc8, gl8, gcrow8


def _dn_heads(c_tile, rows, beta8, gc8, gl8, gcrow8, K, pre=None):
    return _dn_heads_groups(c_tile, [(rows, beta8, gc8, gl8, gcrow8)], K, None if pre is None else [pre])[0]


def _dn_heads_groups(c_tile, groups, K, pres=None):
    ds = [_dn_head_vec(c_tile, rows, h, beta8, gc8, gl8, gcrow8, K)
          for rows, beta8, gc8, gl8, gcrow8 in groups for h in range(NHD)]
    pls = [_dot_nt(d["kb"], d["kn"]) for d in ds]
    pms = [_dot_nt(d["qn"], d["kn"]) for d in ds]
    for d, pl_, pm in zip(ds, pls, pms):
        d.update(pl=pl_, pm=pm, lmat=jnp.where(K["strict"], pl_ * d["gam_m"], 0.0), mm=pm * d["gam_m"])
    if pres is None:
        for d, tm in zip(ds, _tri_inv([d["lmat"] for d in ds], K["eye"])):
            d.update(tm=tm, u=_dot(tm, d["vb"]), w=_dot(tm, d["kg"]))
    else:
        for d, (tm, u, w) in zip(ds, [p for pre in pres for p in pre]):
            d.update(tm=tm, u=u, w=w)
    return [ds[NHD * k:NHD * (k + 1)] for k in range(len(groups))]


def _dn_head_vec(c_tile, rows, h, beta8, gc8, gl8, gcrow8, K):
    qr = c_tile[rows, DHD * h:DHD * (h + 1)]
    kr = c_tile[rows, DW + DHD * h:DW + DHD * (h + 1)]
    v = c_tile[rows, 2 * DW + DHD * h:2 * DW + DHD * (h + 1)]
    rq = lax.rsqrt(jnp.sum(qr * qr, axis=1, keepdims=True) + EPS)
    rk = lax.rsqrt(jnp.sum(kr * kr, axis=1, keepdims=True) + EPS)
    qh, kn = qr * rq, kr * rk
    qn = qh * (DHD ** -0.5)
    beta = _col(beta8, h)
    gccol, glcol, gcrow = _col(gc8, NHD + h), _col(gl8, NHD + h), _row(gcrow8, NHD + h)
    diff = gccol - gcrow
    gam_m = jnp.exp(jnp.where(K["tril"], diff, NEG))
    gam = jnp.exp(gccol)
    egl = jnp.exp(glcol - gccol)
    kb, vb = kn * beta, v * beta
    kg = kb * gam
    return dict(qr=qr, kr=kr, v=v, rq=rq, rk=rk, qh=qh, qn=qn, kn=kn, beta=beta, diff=diff, gam_m=gam_m, gam=gam,
                egl=egl, el=jnp.exp(glcol), kb=kb, vb=vb, kg=kg, qd=qn * gam, kd=kn * egl)


def _halo_prev(width):
    return pl.BlockSpec((8, width), lambda i: (jnp.maximum(i * (TG // 8) - 1, 0), 0))


def _dn_prep(draw, conv_w, ba, bat, alog8, dtb8, alog8t, dtb8t):
    T = draw.shape[0]
    nb = T // TG
    hm = lambda w, dt: jax.ShapeDtypeStruct((NHD, T, w), dt)
    hm_spec = lambda w: pl.BlockSpec((NHD, TG, w), lambda i: (0, i, 0))
    pc = lambda r, c: jax.ShapeDtypeStruct((NHD, T // CH, r, c), _MXU)
    pc_spec = lambda r, c: pl.BlockSpec((NHD, CPT, r, c), lambda i: (0, i, 0, 0))

    def body(x_ref, halo_ref, cw_ref, ba_ref, bat_ref, al_ref, dt_ref, alt_ref, dtt_ref,
             u_ref, w_ref, kd_ref, tm_ref, wq_ref, km_ref, mq_ref, wt_ref, elb_ref, cv_ref):
        i = pl.program_id(0)
        K = _dn_masks()
        halo = jnp.where(i > 0, halo_ref[...], 0.0)
        cv = _conv(x_ref[...], halo, cw_ref[...])
        cv_ref[...] = cv
        c_tile = cv * _sigmoid(cv)
        eye128 = jnp.where(_iota((DHD, DHD), 0) == _iota((DHD, DHD), 1), 1.0, 0.0).astype(_MXU)
        def gate_inputs(g):
            rows = slice(GR * g, GR * (g + 1))
            ba_g = ba_ref[rows, :]
            _, gc8, gl8, gcrow8 = _gate_terms(ba_g, bat_ref[:, rows], al_ref[...], dt_ref[...], alt_ref[...],
                                              dtt_ref[...], K)
            return rows, _sigmoid(ba_g), gc8, gl8, gcrow8

        def store(g, rows, ds):
            mmts = [_dot_nt(d["kn"], d["qn"]) * jnp.exp(jnp.where(K["triu"], -d["diff"], NEG)) for d in ds]
            mcs = [_dot(d["mm"], K["fold_b"]) for d in ds]
            mcts = [_dot(m, K["fold_b"]) for m in mmts]
            for h, d in enumerate(ds):
                tm_ref[h, rows, :] = d["tm"].astype(_MXU)
                u_ref[h, rows, :] = d["u"]
                w_ref[h, rows, :] = d["w"].astype(_MXU)
                kd_ref[h, rows, :] = d["kd"].astype(_MXU)
                elb = jnp.broadcast_to(d["el"], (GR, DHD))
                for cc in range(GR // CH):
                    ch = slice(CH * cc, CH * (cc + 1))
                    n = (GR // CH) * g + cc
                    wq_ref[h, n, 0:CH, :] = d["w"][ch].astype(_MXU)
                    wq_ref[h, n, CH:2 * CH, :] = d["qd"][ch].astype(_MXU)
                    km_ref[h, n, 0:DHD, :] = _dot_nt(eye128, d["kd"][ch]).astype(_MXU)
                    km_ref[h, n, DHD:DHD + CH, :] = mcs[h][ch].astype(_MXU)
                    mq_ref[h, n, 0:CH, :] = mcts[h][ch].astype(_MXU)
                    mq_ref[h, n, CH:CH + DHD, :] = _dot_nt(eye128, d["qd"][ch]).astype(_MXU)
                    wt_ref[h, n] = _dot_nt(eye128, d["w"][ch]).astype(_MXU)
                    elb_ref[n:n + 1, DHD * h:DHD * (h + 1)] = elb[CH * cc:CH * cc + 1, :]

        PAIR = 4
        for g0 in range(0, NG, PAIR):
            pair = [gate_inputs(g) for g in range(g0, g0 + PAIR)]
            for k, ds in enumerate(_dn_heads_groups(c_tile, pair, K)):
                store(g0 + k, pair[k][0], ds)

    return pl.pallas_call(
        body, name="dn_prep", grid=(nb,),
        in_specs=[_rows(TG, 1536), _halo_prev(1536), _full((CONV_K, 1536)), _rows(TG, 8),
                  pl.BlockSpec((8, TG), lambda i: (0, i)), _full((1, 8)), _full((1, 8)), _full((8, 1)), _full((8, 1))],
        out_specs=[hm_spec(DHD), hm_spec(DHD), hm_spec(DHD), hm_spec(GR),
                   pc_spec(2 * CH, DHD), pc_spec(DHD + CH, CH), pc_spec(CH + DHD, CH), pc_spec(DHD, CH),
                   pl.BlockSpec((CPT, NHD * DHD), lambda i: (i, 0)), _rows(TG, 1536)],
        out_shape=[hm(DHD, F32), hm(DHD, _MXU), hm(DHD, _MXU), hm(GR, _MXU),
                   pc(2 * CH, DHD), pc(DHD + CH, CH), pc(CH + DHD, CH), pc(DHD, CH),
                   jax.ShapeDtypeStruct((T // CH, NHD * DHD), F32), jax.ShapeDtypeStruct((T, 1536), F32)],
        compiler_params=_cp(("arbitrary",)),
    )(draw, draw, conv_w, ba, bat, alog8, dtb8, alog8t, dtb8t)


def _dn_scan(u, wq, km, elb):
    T = u.shape[1]
    nb = T // TG
    hm_spec = lambda wd: pl.BlockSpec((NHD, TG, wd), lambda i: (0, i, 0))

    def body(u_ref, wq_ref, km_ref, elb_ref, o_ref, vn_ref, sn_ref, S):
        @pl.when(pl.program_id(0) == 0)
        def _():
            S[...] = jnp.zeros_like(S)

        sub8 = _iota((CPT, DHD), 0)
        heads = range(NHD)

        def chunk(cc, carry):
            rs = pl.ds(pl.multiple_of(cc * CH, CH), CH)
            sh = [S[h] for h in heads]
            sb = [s.astype(_MXU) for s in sh]
            r1 = [_dot(wq_ref[h, cc], sb[h]) for h in heads]
            vnb = [(u_ref[h, rs, :] - r1[h][0:CH]).astype(_MXU) for h in heads]
            r2 = [_dot(km_ref[h, cc], vnb[h]) for h in heads]
            for h in heads:
                el = jnp.sum(jnp.where(sub8 == cc, elb_ref[:, DHD * h:DHD * (h + 1)], 0.0), axis=0, keepdims=True)
                S[h] = sh[h] * el + r2[h][0:DHD]
                sn_ref[cc, h] = sb[h]
                vn_ref[h, rs, :] = vnb[h]
                o_ref[h, rs, :] = r1[h][CH:2 * CH] + r2[h][DHD:DHD + CH]
            return carry

        lax.fori_loop(0, CPT, chunk, 0)

    return pl.pallas_call(
        body, name="dn_scan", grid=(nb,),
        in_specs=[hm_spec(DHD), pl.BlockSpec((NHD, CPT, 2 * CH, DHD), lambda i: (0, i, 0, 0)),
                  pl.BlockSpec((NHD, CPT, DHD + CH, CH), lambda i: (0, i, 0, 0)),
                  pl.BlockSpec((CPT, NHD * DHD), lambda i: (i, 0))],
        out_specs=[hm_spec(DHD), hm_spec(DHD), pl.BlockSpec((CPT, NHD, DHD, DHD), lambda i: (i, 0, 0, 0))],
        out_shape=[jax.ShapeDtypeStruct((NHD, T, DHD), F32), jax.ShapeDtypeStruct((NHD, T, DHD), _MXU),
                   jax.ShapeDtypeStruct((T // CH, NHD, DHD, DHD), _MXU)],
        scratch_shapes=[pltpu.VMEM((NHD, DHD, DHD), F32)],
        compiler_params=_cp(("arbitrary",)),
    )(u, wq, km, elb)


def _dn_scan_bwd(do, mq, kd, wt, sn, vn, elb):
    T = do.shape[1]
    nb = T // TG
    rev = lambda wd: pl.BlockSpec((NHD, TG, wd), lambda i: (0, nb - 1 - i, 0))
    rev_t = lambda r: pl.BlockSpec((NHD, CPT, r, CH), lambda i: (0, nb - 1 - i, 0, 0))

    def body(do_ref, mq_ref, kd_ref, wt_ref, sn_ref, vn_ref, elb_ref,
             du_ref, dw_ref, dqd_ref, dkd_ref, dgx_ref, dS):
        @pl.when(pl.program_id(0) == 0)
        def _():
            dS[...] = jnp.zeros_like(dS)

        last_row = _iota((CH, DHD), 0) == CH - 1
        sub8 = _iota((CPT, DHD), 0)
        heads = range(NHD)

        def chunk(k, carry):
            cc = CPT - 1 - k
            rs = pl.ds(pl.multiple_of(cc * CH, CH), CH)
            dsh = [dS[h] for h in heads]
            dsb = [d.astype(_MXU) for d in dsh]
            doc = [do_ref[h, rs, :].astype(_MXU) for h in heads]
            a = [_dot(mq_ref[h, cc], doc[h]) for h in heads]
            b = [_dot(kd_ref[h, rs, :], dsb[h]) for h in heads]
            dvn = [a[h][0:CH] + b[h] for h in heads]
            dvnb = [d.astype(_MXU) for d in dvn]
            e = [_dot(wt_ref[h, cc], dvnb[h]) for h in heads]
            for h in heads:
                el = jnp.sum(jnp.where(sub8 == cc, elb_ref[:, DHD * h:DHD * (h + 1)], 0.0), axis=0, keepdims=True)
                sn = sn_ref[cc, h]
                dS[h] = a[h][CH:CH + DHD] + dsh[h] * el - e[h]
                du_ref[h, rs, :] = dvn[h]
                c = _dot_nt(jnp.concatenate([doc[h], dvnb[h]], axis=0), sn)
                dqd_ref[h, rs, :] = c[0:CH]
                dw_ref[h, rs, :] = -c[CH:2 * CH]
                dkd_ref[h, rs, :] = _dot_nt(vn_ref[h, rs, :], dsb[h])
                part = jnp.sum(dsh[h] * sn.astype(F32), axis=0, keepdims=True) * el
                dgx_ref[h, rs, :] = jnp.where(last_row, part, 0.0)
            return carry

        lax.fori_loop(0, CPT, chunk, 0)

    o = jax.ShapeDtypeStruct((NHD, T, DHD), F32)
    return pl.pallas_call(
        body, name="dn_scan_bwd", grid=(nb,),
        in_specs=[rev(DHD), rev_t(CH + DHD), rev(DHD), rev_t(DHD),
                  pl.BlockSpec((CPT, NHD, DHD, DHD), lambda i: (nb - 1 - i, 0, 0, 0)), rev(DHD),
                  pl.BlockSpec((CPT, NHD * DHD), lambda i: (nb - 1 - i, 0))],
        out_specs=[rev(DHD)] * 5,
        out_shape=[o] * 5,
        scratch_shapes=[pltpu.VMEM((NHD, DHD, DHD), F32)],
        compiler_params=_cp(("arbitrary",)),
    )(do, mq, kd, wt, sn, vn, elb)


def _put_col(acc, k, col):
    return jnp.where(_iota(acc.shape, 1) == k, col, acc)


def _dn_post_bwd(draw, cv, conv_w, ba, bat, alog8, dtb8, alog8t, dtb8t, du, dw, dqd, dkd, dgx, do, vn, tm, u, w):
    T = draw.shape[0]
    nb = T // TG
    hm_spec = lambda wd: pl.BlockSpec((NHD, TG, wd), lambda i: (0, nb - 1 - i, 0))
    rrows = lambda w: pl.BlockSpec((TG, w), lambda i: (nb - 1 - i, 0))

    def body(x_ref, cv_ref, cw_ref, ba_ref, bat_ref, al_ref, dt_ref, alt_ref, dtt_ref,
             du_ref, dw_ref, dqd_ref, dkd_ref, dgx_ref, do_ref, vn_ref, tm_ref, u_ref, w_ref,
             dx_ref, dba_ref, sm_ref, dcw_ref, dc_ref, nxt_ref):
        i = pl.program_id(0)

        @pl.when(i == 0)
        def _():
            sm_ref[...] = jnp.zeros_like(sm_ref)
            dcw_ref[...] = jnp.zeros_like(dcw_ref)
            nxt_ref[...] = jnp.zeros_like(nxt_ref)

        K = _dn_masks()
        cv = cv_ref[...]
        sg = _sigmoid(cv)
        c_tile = cv * sg
        dsilu = sg * (1.0 + cv * (1.0 - sg))
        for g in range(NG):
            rows = slice(GR * g, GR * (g + 1))
            ba_g = ba_ref[rows, :]
            g8, gc8, gl8, gcrow8 = _gate_terms(ba_g, bat_ref[:, rows], al_ref[...], dt_ref[...], alt_ref[...],
                                               dtt_ref[...], K)
            beta8 = _sigmoid(ba_g)
            dgc8 = jnp.zeros((GR, 8), F32)
            rd8 = jnp.zeros((GR, 8), F32)
            dbeta8 = jnp.zeros((GR, 8), F32)
            pre = [(tm_ref[h, rows, :], u_ref[h, rows, :], w_ref[h, rows, :]) for h in range(NHD)]
            ds = _dn_heads(c_tile, rows, beta8, gc8, gl8, gcrow8, K, pre)
            H = range(NHD)
            eye_b = K["eye_b"].astype(_MXU)
            gam_t = [jnp.exp(jnp.where(K["triu"], -d["diff"], NEG)) for d in ds]
            doh = [do_ref[h, rows, :] for h in H]
            vnh = [vn_ref[h, rows, :] for h in H]
            tt = [_dot_nt(eye_b, d["tm"]) for d in ds]
            dvb = [_dot(tt[h], du_ref[h, rows, :]) for h in H]
            dkg = [_dot(tt[h], dw_ref[h, rows, :]) for h in H]
            plt = [_dot_nt(d["kn"], d["kb"]) for d in ds]
            pmt = [_dot_nt(d["kn"], d["qn"]) for d in ds]
            da = [-(_dot_nt(dvb[h], ds[h]["u"]) + _dot_nt(dkg[h], ds[h]["w"])) for h in H]
            dat = [-(_dot_nt(ds[h]["u"], dvb[h]) + _dot_nt(ds[h]["w"], dkg[h])) for h in H]
            dpm = [jnp.where(K["tril"], _dot_nt(doh[h], vnh[h]), 0.0) * ds[h]["gam_m"] for h in H]
            dpmt = [jnp.where(K["triu"], _dot_nt(vnh[h], doh[h]), 0.0) * gam_t[h] for h in H]
            dpl = [jnp.where(K["strict"], da[h], 0.0) * ds[h]["gam_m"] for h in H]
            dplt = [jnp.where(K["strict_u"], dat[h], 0.0) * gam_t[h] for h in H]
            dkb = [_dot(dpl[h], ds[h]["kn"]) + dkg[h] * ds[h]["gam"] for h in H]
            dqn = [_dot(dpm[h], ds[h]["kn"]) + dqd_ref[h, rows, :] * ds[h]["gam"] for h in H]
            dknm = [_dot(dplt[h], ds[h]["kb"]) + _dot(dpmt[h], ds[h]["qn"]) for h in H]
            for h, d in enumerate(ds):
                kn, dqdh, dkdh = d["kn"], dqd_ref[h, rows, :], dkd_ref[h, rows, :]
                dkn = dknm[h] + dkdh * d["egl"] + dkb[h] * d["beta"]
                dkd_kd = dkdh * d["kd"]
                rd = jnp.sum(dkd_kd, axis=1, keepdims=True)
                dgc = jnp.sum(dpl[h] * d["pl"] + dpm[h] * d["pm"] - dplt[h] * plt[h] - dpmt[h] * pmt[h]
                              + dqdh * d["qd"] + dkg[h] * d["kg"] - dkd_kd + dgx_ref[h, rows, :],
                              axis=1, keepdims=True)
                dgc8 = _put_col(dgc8, NHD + h, dgc)
                rd8 = _put_col(rd8, NHD + h, rd)
                dbeta = jnp.sum(dkb[h] * kn + dvb[h] * d["v"], axis=1, keepdims=True)
                dbeta8 = _put_col(dbeta8, h, dbeta)
                dqh = dqn[h] * (DHD ** -0.5)
                qh = d["qh"]
                dqr = d["rq"] * (dqh - qh * jnp.sum(dqh * qh, axis=1, keepdims=True))
                dkr = d["rk"] * (dkn - kn * jnp.sum(dkn * kn, axis=1, keepdims=True))
                cq = slice(DHD * h, DHD * (h + 1))
                ck = slice(DW + DHD * h, DW + DHD * (h + 1))
                cvv = slice(2 * DW + DHD * h, 2 * DW + DHD * (h + 1))
                dc_ref[rows, cq] = dqr * dsilu[rows, cq]
                dc_ref[rows, ck] = dkr * dsilu[rows, ck]
                dc_ref[rows, cvv] = dvb[h] * d["beta"] * dsilu[rows, cvv]
            dgc8 = dgc8 + jnp.where(K["last"], _ones_dot(K["blk_b"], rd8), 0.0)
            dg8 = _ones_dot(K["triu_b"], dgc8)
            sgm = _sigmoid(ba_g + dt_ref[...])
            dalpha = dg8 * (-jnp.exp(al_ref[...])) * sgm
            lane8 = _iota((GR, 8), 1)
            dba_ref[rows, :] = jnp.where(lane8 < NHD, dbeta8 * beta8 * (1.0 - beta8), dalpha)
            valid = lane8 >= NHD
            sm_ref[0:1, 0:8] += jnp.sum(jnp.where(valid, dg8 * g8, 0.0), axis=0, keepdims=True)
            sm_ref[1:2, 0:8] += jnp.sum(jnp.where(valid, dalpha, 0.0), axis=0, keepdims=True)

        dcv = dc_ref[...]
        xv = x_ref[...]
        nxt = nxt_ref[...]
        w = cw_ref[...]
        dx = dcv * w[CONV_K - 1:CONV_K, :]
        dcw_ref[CONV_K - 1:CONV_K, :] += jnp.sum(dcv * xv, axis=0, keepdims=True)
        for k in range(1, CONV_K):
            j = CONV_K - 1 - k
            up = _shift_up(dcv, nxt, k)
            dx = dx + up * w[j:j + 1, :]
            dcw_ref[j:j + 1, :] += jnp.sum(up * xv, axis=0, keepdims=True)
        dx_ref[...] = dx
        nxt_ref[...] = dcv[0:8]

    return pl.pallas_call(
        body, name="dn_post_bwd", grid=(nb,),
        in_specs=[rrows(1536), rrows(1536), _full((CONV_K, 1536)), rrows(8),
                  pl.BlockSpec((8, TG), lambda i: (0, nb - 1 - i)), _full((1, 8)), _full((1, 8)), _full((8, 1)),
                  _full((8, 1)),
                  hm_spec(DHD), hm_spec(DHD), hm_spec(DHD), hm_spec(DHD), hm_spec(DHD), hm_spec(DHD), hm_spec(DHD),
                  hm_spec(GR), hm_spec(DHD), hm_spec(DHD)],
        out_specs=[rrows(1536), rrows(8), _full((8, LANES)), _full((8, 1536))],
        out_shape=[jax.ShapeDtypeStruct((T, 1536), F32), jax.ShapeDtypeStruct((T, 8), F32),
                   jax.ShapeDtypeStruct((8, LANES), F32), jax.ShapeDtypeStruct((8, 1536), F32)],
        scratch_shapes=[pltpu.VMEM((TG, 1536), F32), pltpu.VMEM((8, 1536), F32)],
        compiler_params=_cp(("arbitrary",)),
    )(draw, cv, conv_w, ba, bat, alog8, dtb8, alog8t, dtb8t, du, dw, dqd, dkd, dgx, do, vn, tm, u, w)


def _rms(x):
    return lax.rsqrt(jnp.mean(x * x, axis=1, keepdims=True) + EPS)


def _rms_bwd(dy, xh, r, g):
    dxh = dy * g
    return r * (dxh - xh * jnp.mean(dxh * xh, axis=1, keepdims=True))


def _hm_rows(tm):
    return pl.BlockSpec((NHD, tm, DHD), lambda i: (0, i, 0))


def _post_mix(apre, o, z, x, w_out, g_a, g_dn):
    T = x.shape[0]

    def body(ap_ref, o_ref, z_ref, x_ref, w_ref, ga_ref, gd_ref, x1_ref, mix_ref):
        ap = ap_ref[...]
        parts = [ap * _rms(ap) * ga_ref[...]]
        zz = z_ref[...]
        for h in range(NHD):
            oh = o_ref[h]
            zh = zz[:, DHD * h:DHD * (h + 1)]
            parts.append(oh * _rms(oh) * gd_ref[...] * (zh * _sigmoid(zh)))
        mix = jnp.concatenate(parts, axis=1).astype(_MXU)
        mix_ref[...] = mix
        x1_ref[...] = x_ref[...] + jnp.dot(mix, w_ref[...], preferred_element_type=F32)

    return pl.pallas_call(
        body, name="post_mix", grid=(T // TM,),
        in_specs=[_rows(TM, AW), _hm_rows(TM), _rows(TM, DW), _rows(TM, D), _full((D, D)), _full((1, AW)),
                  _full((1, DHD))],
        out_specs=[_rows(TM, D), _rows(TM, D)],
        out_shape=[jax.ShapeDtypeStruct((T, D), F32), jax.ShapeDtypeStruct((T, D), _MXU)],
        compiler_params=_cp(("arbitrary",)),
    )(apre, o, z, x, w_out, g_a, g_dn)


def _ffn(x1, tgt, wl_all, g_ffn):
    T = x1.shape[0]
    SH = FF // N_DEV
    nt = (((1,), (1,)), ((), ()))

    def body(x_ref, t_ref, wl_hbm, g_ref,
             dx1_ref, dx1b_ref, h2_ref, act_ref, dgu_ref, dyb_ref, loss_ref, dg_ref, wg, wu, wd, sem):
        @pl.when(pl.program_id(0) == 0)
        def _():
            cps = [pltpu.make_async_copy(wl_hbm.at[dev, pl.ds(128 + SH * k, SH), :], dst.at[pl.ds(SH * dev, SH), :],
                                         sem.at[N_DEV * k + dev])
                   for k, dst in enumerate((wg, wu, wd)) for dev in range(N_DEV)]
            for cp in cps:
                cp.start()
            for cp in cps:
                cp.wait()
            loss_ref[...] = jnp.zeros_like(loss_ref)
            dg_ref[...] = jnp.zeros_like(dg_ref)

        xv = x_ref[...]
        r = _rms(xv)
        xh = xv * r
        gg = g_ref[...]
        h2 = (xh * gg).astype(_MXU)
        h2_ref[...] = h2
        gate = lax.dot_general(h2, wg[...], nt, preferred_element_type=F32)
        up = lax.dot_general(h2, wu[...], nt, preferred_element_type=F32)
        sg = _sigmoid(gate)
        silu = gate * sg
        act = (silu * up).astype(_MXU)
        act_ref[...] = act
        y = xv + jnp.dot(act, wd[...], preferred_element_type=F32)
        err = y - t_ref[...]
        loss_ref[...] += jnp.sum(err * err, axis=0, keepdims=True)
        dy = err * (1.0 / D)
        dyb = dy.astype(_MXU)
        dyb_ref[...] = dyb
        dact = lax.dot_general(dyb, wd[...], nt, preferred_element_type=F32)
        dgate = (dact * up * (sg * (1.0 + gate * (1.0 - sg)))).astype(_MXU)
        dup = (dact * silu).astype(_MXU)
        dgu_ref[:, 0:FF] = dgate
        dgu_ref[:, FF:2 * FF] = dup
        dh2 = (jnp.dot(dgate, wg[...], preferred_element_type=F32)
               + jnp.dot(dup, wu[...], preferred_element_type=F32))
        dg_ref[...] += jnp.sum(dh2 * xh, axis=0, keepdims=True)
        dx1 = dy + _rms_bwd(dh2, xh, r, gg)
        dx1_ref[...] = dx1
        dx1b_ref[...] = dx1.astype(_MXU)

    anyspec = pl.BlockSpec(memory_space=pl.ANY)
    sd = lambda w, dt: jax.ShapeDtypeStruct((T, w), dt)
    return pl.pallas_call(
        body, name="ffn", grid=(T // TF,),
        in_specs=[_rows(TF, D), _rows(TF, D), anyspec, _full((1, D))],
        out_specs=[_rows(TF, D), _rows(TF, D), _rows(TF, D), _rows(TF, FF), _rows(TF, 2 * FF), _rows(TF, D),
                   _full((1, D)), _full((1, D))],
        out_shape=[sd(D, F32), sd(D, _MXU), sd(D, _MXU), sd(FF, _MXU), sd(2 * FF, _MXU), sd(D, _MXU),
                   jax.ShapeDtypeStruct((1, D), F32), jax.ShapeDtypeStruct((1, D), F32)],
        scratch_shapes=[pltpu.VMEM((FF, D), _MXU)] * 3 + [pltpu.SemaphoreType.DMA((3 * N_DEV,))],
        compiler_params=_cp(("arbitrary",)),
    )(x1, tgt, wl_all, g_ffn)


def _mix_bwd(dx1b, w_out, apre, o, z, g_a, g_dn):
    T = dx1b.shape[0]

    def body(dx_ref, w_ref, ap_ref, o_ref, z_ref, ga_ref, gd_ref, dap_ref, do_ref, dz_ref, dga_ref, dgd_ref):
        @pl.when(pl.program_id(0) == 0)
        def _():
            dga_ref[...] = jnp.zeros_like(dga_ref)
            dgd_ref[...] = jnp.zeros_like(dgd_ref)

        dmix = lax.dot_general(dx_ref[...], w_ref[...], (((1,), (1,)), ((), ())), preferred_element_type=F32)
        ap = ap_ref[...]
        ra = _rms(ap)
        ah = ap * ra
        da = dmix[:, 0:AW]
        dga_ref[...] += jnp.sum(da * ah, axis=0, keepdims=True)
        dap_ref[...] = _rms_bwd(da, ah, ra, ga_ref[...])
        zz = z_ref[...]
        gd = gd_ref[...]
        for h in range(NHD):
            cs = slice(DHD * h, DHD * (h + 1))
            dd = dmix[:, AW + DHD * h:AW + DHD * (h + 1)]
            oh = o_ref[h]
            ro = _rms(oh)
            ohh = oh * ro
            zh = zz[:, cs]
            sz = _sigmoid(zh)
            dz_ref[:, cs] = dd * (ohh * gd) * (sz * (1.0 + zh * (1.0 - sz)))
            don = dd * (zh * sz)
            dgd_ref[...] += jnp.sum(don * ohh, axis=0, keepdims=True)
            do_ref[h] = _rms_bwd(don, ohh, ro, gd)

    return pl.pallas_call(
        body, name="mix_bwd", grid=(T // TM,),
        in_specs=[_rows(TM, D), _full((D, D)), _rows(TM, AW), _hm_rows(TM), _rows(TM, DW), _full((1, AW)),
                  _full((1, DHD))],
        out_specs=[_rows(TM, AW), _hm_rows(TM), _rows(TM, DW), _full((1, AW)), _full((1, DHD))],
        out_shape=[jax.ShapeDtypeStruct((T, AW), F32), jax.ShapeDtypeStruct((NHD, T, DHD), F32),
                   jax.ShapeDtypeStruct((T, DW), F32), jax.ShapeDtypeStruct((1, AW), F32),
                   jax.ShapeDtypeStruct((1, DHD), F32)],
        compiler_params=_cp(("arbitrary",)),
    )(dx1b, w_out, apre, o, z, g_a, g_dn)


DPW = 3712


def _inproj_bwd(dqn, dkn, dv, araw, ddraw, dz, dba, x, dx1, w_int, w_ba, g_mix, qg_t, kg_t):
    T = x.shape[0]

    def body(dqn_ref, dkn_ref, dv_ref, ar_ref, dd_ref, dz_ref, dba_ref, x_ref, dx1_ref, w_hbm, wba_ref, g_ref, qg_ref,
             kg_ref, dx_ref, dp_ref, dgm_ref, dqg_ref, dkg_ref, w_ref, w_sem):
        @pl.when(pl.program_id(0) == 0)
        def _():
            cp = pltpu.make_async_copy(w_hbm, w_ref, w_sem)
            cp.start()
            cp.wait()
            dgm_ref[...] = jnp.zeros_like(dgm_ref)
            dqg_ref[...] = jnp.zeros_like(dqg_ref)
            dkg_ref[...] = jnp.zeros_like(dkg_ref)

        bd = _block_ones(AW // 2, DHA)

        def head_norm_bwd(raw, dyn, gain, dg_ref):
            r = _head_rms(raw, bd, DHA)
            xh = raw * r
            dg_ref[...] += jnp.sum(dyn * xh, axis=0, keepdims=True)
            dxh = dyn * gain
            return r * (dxh - xh * (_head_sum(dxh * xh, bd) * (1.0 / DHA)))

        def segment(lo, val):
            vb = val.astype(_MXU)
            dp_ref[:, lo:lo + val.shape[1]] = vb
            return jnp.dot(vb, w_ref[lo:lo + val.shape[1], :], preferred_element_type=F32)

        dh = segment(1536, dd_ref[...]) + segment(2 * AW, dv_ref[...]) + segment(3072, dz_ref[...])
        dbab = dba_ref[...].astype(_MXU)
        dp_ref[:, 3584:DPW] = jnp.zeros((TM, DPW - 3584), _MXU)
        dp_ref[:, 3584:3592] = dbab
        dh = dh + lax.dot_general(dbab, wba_ref[...], (((1,), (1,)), ((), ())), preferred_element_type=F32)
        ar = ar_ref[...]
        dq = head_norm_bwd(ar[:, 0:AW], dqn_ref[...] * (DHA ** -0.5), qg_ref[...], dqg_ref)
        dk = head_norm_bwd(ar[:, AW:2 * AW], dkn_ref[...], kg_ref[...], dkg_ref)
        dh = dh + segment(0, dq) + segment(AW, dk)
        xv = x_ref[...]
        r = _rms(xv)
        xh = xv * r
        dgm_ref[...] += jnp.sum(dh * xh, axis=0, keepdims=True)
        dx_ref[...] = dx1_ref[...] + _rms_bwd(dh, xh, r, g_ref[...])

    return pl.pallas_call(
        body, name="inproj_bwd", grid=(T // TM,),
        in_specs=[_rows(TM, AW), _rows(TM, AW), _rows(TM, AW), _rows(TM, 2 * AW), _rows(TM, 1536), _rows(TM, DW),
                  _rows(TM, 8), _rows(TM, D), _rows(TM, D), pl.BlockSpec(memory_space=pl.ANY), _full((D, 8)),
                  _full((1, D)), _full((1, AW)), _full((1, AW))],
        out_specs=[_rows(TM, D), _rows(TM, DPW), _full((1, D)), _full((1, AW)), _full((1, AW))],
        out_shape=[jax.ShapeDtypeStruct((T, D), F32), jax.ShapeDtypeStruct((T, DPW), _MXU),
                   jax.ShapeDtypeStruct((1, D), F32), jax.ShapeDtypeStruct((1, AW), F32),
                   jax.ShapeDtypeStruct((1, AW), F32)],
        scratch_shapes=[pltpu.VMEM((3584, D), _MXU), pltpu.SemaphoreType.DMA],
        compiler_params=_cp(("arbitrary",)),
    )(dqn, dkn, dv, araw, ddraw, dz, dba, x, dx1, w_int, w_ba, g_mix, qg_t, kg_t)


def _wgrad(a, b, name, tk=1024, tn=None, out_dtype=F32, transposed=False, gather=None):
    T, M = a.shape
    N = b.shape[1]
    tn = N if tn is None else tn
    assert T % tk == 0 and N % tn == 0, (T, tk, N, tn)
    nk = T // tk
    hosted = gather is not None
    assert not hosted or (tn == N and nk >= 3)

    def body(a_ref, b_ref, *rest):
        k = pl.program_id(1)
        if hosted:
            g_ref, o_ref, g_all, acc, send_sems, recv_sems, local_sem = rest
            ag = _Gather(g_ref, g_all, send_sems, recv_sems, local_sem)
            pl.when(k == 0)(ag.start)
            pl.when(k == nk // 2)(ag.forward)
        else:
            o_ref, acc = rest

        @pl.when(k == 0)
        def _():
            acc[...] = jnp.zeros_like(acc)

        acc[...] += lax.dot_general(a_ref[...], b_ref[...], (((0,), (0,)), ((), ())), preferred_element_type=F32)

        @pl.when(k == nk - 1)
        def _():
            r = acc[...]
            o_ref[...] = (r.T if transposed else r).astype(out_dtype)
            if hosted:
                ag.finish()

    if transposed:
        out_spec, out_shape = pl.BlockSpec((tn, M), lambda j, k: (j, 0)), (N, M)
    else:
        out_spec, out_shape = pl.BlockSpec((M, tn), lambda j, k: (0, j)), (M, N)
    anyspec = pl.BlockSpec(memory_space=pl.ANY)
    return pl.pallas_call(
        body, name=name, grid=(N // tn, nk),
        in_specs=[pl.BlockSpec((tk, M), lambda j, k: (k, 0)), pl.BlockSpec((tk, tn), lambda j, k: (k, j))]
        + ([anyspec] if hosted else []),
        out_specs=[out_spec, anyspec] if hosted else out_spec,
        out_shape=([jax.ShapeDtypeStruct(out_shape, out_dtype),
                    jax.ShapeDtypeStruct((N_DEV,) + gather.shape, gather.dtype)] if hosted
                   else jax.ShapeDtypeStruct(out_shape, out_dtype)),
        scratch_shapes=[pltpu.VMEM((M, tn), F32)] + (_COMM_SEMS if hosted else []),
        compiler_params=_cp(("arbitrary", "arbitrary")),
    )(*((a, b) + ((gather,) if hosted else ())))


def _adamw(parts, w, m, v, name, tr, send=None):
    K, R, W = parts.shape
    n = R // tr

    def body(p_ref, w_ref, m_ref, v_ref, *rest):
        if send is not None:
            send_ref, g_ref, d_ref, nm_ref, nv_ref, recv_ref, send_sems, recv_sems, local_sem = rest
            sc = _Scatter(send_ref, recv_ref, send_sems, recv_sems, local_sem)
            pl.when(pl.program_id(0) == 0)(sc.start)
        else:
            g_ref, d_ref, nm_ref, nv_ref = rest
        g = p_ref[0].astype(F32)
        for k in range(1, K):
            g = g + p_ref[k].astype(F32)
        g_ref[...] = g
        nm = ADAM_B1 * m_ref[...] + (1.0 - ADAM_B1) * g
        nv = ADAM_B2 * v_ref[...] + (1.0 - ADAM_B2) * (g * g)
        nm_ref[...] = nm
        nv_ref[...] = nv
        m_hat = nm / (1.0 - ADAM_B1 ** ADAM_STEP)
        v_hat = nv / (1.0 - ADAM_B2 ** ADAM_STEP)
        d_ref[...] = -ADAM_LR * (m_hat / (jnp.sqrt(v_hat) + ADAM_EPS) + ADAM_WD * w_ref[...])
        if send is not None:
            pl.when(pl.program_id(0) == n - 1)(sc.finish)

    o = jax.ShapeDtypeStruct((R, W), F32)
    anyspec = pl.BlockSpec(memory_space=pl.ANY)
    hosted = send is not None
    return pl.pallas_call(
        body, name=name, grid=(n,),
        in_specs=[pl.BlockSpec((K, tr, W), lambda i: (0, i, 0)), _rows(tr, W), _rows(tr, W), _rows(tr, W)]
        + ([anyspec] if hosted else []),
        out_specs=[_rows(tr, W)] * 4 + ([anyspec] if hosted else []),
        out_shape=[o] * 4 + ([jax.ShapeDtypeStruct(send.shape, send.dtype)] if hosted else []),
        scratch_shapes=_COMM_SEMS if hosted else [],
        compiler_params=_cp(("arbitrary",)),
    )(*((parts, w, m, v) + ((send,) if hosted else ())))


SM_ROWS = 136
R_GMIX, R_GFFN, R_QG, R_KG, R_GA, R_GDN, R_ALOG, R_DT, R_LOSS, R_CONV, R_REL = 0, 8, 16, 24, 32, 40, 48, 49, 56, 64, 112


def _small_reduce(gathered):
    def body(p_ref, o_ref):
        s = p_ref[0]
        for k in range(1, N_DEV):
            s = s + p_ref[k]
        o_ref[...] = s
        for r0 in (R_QG, R_KG):
            rs = jnp.sum(s[r0:r0 + 4], axis=0, keepdims=True)
            o_ref[r0:r0 + 1, :] = rs + pltpu.roll(rs, DHA, 1)
        tot = jnp.sum(jnp.sum(s[R_LOSS:R_LOSS + 8], axis=0, keepdims=True), axis=1, keepdims=True)
        o_ref[R_LOSS:R_LOSS + 1, :] = jnp.broadcast_to(tot * (0.5 / D), (1, LANES))

    return pl.pallas_call(
        body, name="small_reduce",
        out_shape=jax.ShapeDtypeStruct((SM_ROWS, LANES), F32),
    )(gathered)


_WIRE = jnp.bfloat16
RA_USED, RA = 449, 464
RL = 128 + 3 * 352


def _pack_rows(parts, rows=None):
    p = jnp.concatenate([t.reshape(-1, D) for t in parts], axis=0) if len(parts) > 1 else parts[0].reshape(-1, D)
    return p if rows is None else jnp.pad(p, ((0, rows - p.shape[0]), (0, 0)))


def _unpack_rows(packed, shapes):
    out, r = [], 0
    for shp in shapes:
        nr = math.prod(shp) // D
        out.append(packed[r:r + nr].reshape(shp))
        r += nr
    return out


def _pad8(t):
    return jnp.pad(t, ((0, (-t.shape[0]) % 8), (0, 0)))


PART = 8 * LANES


def _pack_lanes(parts):
    rows = []
    for p in parts:
        f = p.reshape(-1)
        rows.append(jnp.pad(f, (0, (-f.shape[0]) % PART)).reshape(-1, LANES))
    return jnp.concatenate(rows, axis=0)


def _unpack_lanes(packed, shapes):
    out, r = [], 0
    for shp in shapes:
        n = math.prod(shp)
        nr = 8 * -(-n // PART)
        out.append(packed[r:r + nr].reshape(-1)[:n].reshape(shp))
        r += nr
    return out


def kernel(x, norm_mix_g, w_in, attn_q_norm_g, attn_k_norm_g, rel_bias, attn_out_norm_g, conv_w, a_log, dt_bias, dn_out_norm_g, w_out, norm_ffn_g, w_gate, w_up, w_down, loss_target, m_norm_mix_g, m_w_in, m_attn_q_norm_g, m_attn_k_norm_g, m_rel_bias, m_attn_out_norm_g, m_conv_w, m_a_log, m_dt_bias, m_dn_out_norm_g, m_w_out, m_norm_ffn_g, m_w_gate, m_w_up, m_w_down, v_norm_mix_g, v_w_in, v_attn_q_norm_g, v_attn_k_norm_g, v_rel_bias, v_attn_out_norm_g, v_conv_w, v_a_log, v_dt_bias, v_dn_out_norm_g, v_w_out, v_norm_ffn_g, v_w_gate, v_w_up, v_w_down):
    xs, tgt = x[0], loss_target[0]
    my_idx = 4 * lax.axis_index("x") + 2 * lax.axis_index("y") + lax.axis_index("c")
    late_w = (w_out[0], w_gate[0], w_up[0], w_down[0])

    tab, tabt, wa_all, cw_all = _bias_tables(
        jnp.pad(rel_bias[0].T, ((0, 0), (0, VAR0 - 257))), _pack_rows([w_in[0].T.astype(_MXU)], RA),
        jnp.pad(conv_w[0], ((0, 4), (0, 64))))
    W_in_t = wa_all[:, 0:RA_USED].reshape(N_DEV * RA_USED, D)
    W_int, W_ba = W_in_t[0:3584], W_in_t[3584:3592].T
    conv_full = cw_all[:, 0:CONV_K, 0:192].transpose(1, 0, 2).reshape(CONV_K, 1536)

    qg_t = jnp.tile(attn_q_norm_g, (1, NHA))
    kg_t = jnp.tile(attn_k_norm_g, (1, NHA))
    z4 = jnp.zeros((1, NHD), F32)
    alog8 = jnp.concatenate([z4, a_log], axis=1)
    dtb8 = jnp.concatenate([z4, dt_bias], axis=1)

    late_t = lambda ts: (ts[0], ts[1].T, ts[2].T, ts[3])
    araw, an, draw, z, ba, hb, wl_all = _inproj(xs, norm_mix_g, W_int, W_ba, qg_t, kg_t,
                                                _pack_rows([w.astype(_MXU) for w in late_t(late_w)]))
    W_out = wl_all[:, 0:128].reshape(D, D)
    apre = _attn_fwd(an, tab)
    bat = ba.T
    dn_args = (draw, conv_full, ba, bat, alog8, dtb8, alog8.T, dtb8.T)
    u, w, kd, tm, wq, km, mq, wt, elb, cv = _dn_prep(*dn_args)
    o, vn, sn = _dn_scan(u, wq, km, elb)
    x1, mix = _post_mix(apre, o, z, xs, W_out, attn_out_norm_g, dn_out_norm_g)

    dx1, dx1b, h2, act, dgu, dyb, loss_row, dgffn = _ffn(x1, tgt, wl_all, norm_ffn_g)

    gW_out = _wgrad(mix, dx1b, "wgrad_out", out_dtype=_WIRE)
    gW_gu_t = _wgrad(h2, dgu, "wgrad_gate_up", tn=FF, out_dtype=_WIRE, transposed=True)
    gW_down = _wgrad(dyb, act, "wgrad_down", out_dtype=_WIRE, transposed=True)
    send_late = jnp.concatenate(
        [gW_out.reshape(N_DEV, 128, D), gW_gu_t[0:FF].reshape(N_DEV, 352, D), gW_gu_t[FF:].reshape(N_DEV, 352, D),
         gW_down.reshape(N_DEV, 352, D)], axis=1)

    dap, do, dz, dga, dgdn = _mix_bwd(dx1b, W_out, apre, o, z, attn_out_norm_g, dn_out_norm_g)
    dqn, dkn, dv, dtabt, recv_late = _attn_bwd(an, dap, tabt, send_late)
    drel = _bias_grad(dtabt)
    du, dw, dqd, dkd, dgx = _dn_scan_bwd(do, mq, kd, wt, sn, vn, elb)
    ddraw, dba, sm, dcw = _dn_post_bwd(draw, cv, *dn_args[1:], du, dw, dqd, dkd, dgx, do, vn, tm, u, w)
    gx, dproj, dgmix, dqg, dkg = _inproj_bwd(dqn, dkn, dv, araw, ddraw, dz, dba, xs, dx1, W_int, W_ba, norm_mix_g,
                                             qg_t, kg_t)

    partial = jnp.concatenate(
        [dgmix.reshape(8, LANES), dgffn.reshape(8, LANES), _pad8(dqg.reshape(4, LANES)), _pad8(dkg.reshape(4, LANES)),
         _pad8(dga.reshape(4, LANES)), _pad8(dgdn), sm, loss_row.reshape(8, LANES),
         dcw[0:CONV_K].reshape(48, LANES), drel.reshape(24, LANES)], axis=0)
    gW_in_t, partial_all = _wgrad(hb, dproj, "wgrad_in", tk=1024, out_dtype=_WIRE, transposed=True,
                                  gather=partial)
    send_in = jnp.pad(gW_in_t[0:N_DEV * RA_USED].reshape(N_DEV, RA_USED, D), ((0, 0), (0, RA - RA_USED), (0, 0)))
    late_m = (m_w_out[0], m_w_gate[0], m_w_up[0], m_w_down[0])
    late_v = (v_w_out[0], v_w_gate[0], v_w_up[0], v_w_down[0])
    *outs_late, recv_in = _adamw(recv_late, _pack_rows(late_t(late_w)), _pack_rows(late_t(late_m)),
                                 _pack_rows(late_t(late_v)), "adamw_late", 32, send=send_in)
    outs_in = _adamw(recv_in, _pack_rows([w_in[0].T], RA), _pack_rows([m_w_in[0].T], RA),
                     _pack_rows([v_w_in[0].T], RA), "adamw_w_in", 16)
    late_t_shapes = [t.shape for t in late_t(late_w)]
    big = [[a[0:RA_USED].T] + list(late_t(_unpack_rows(b, late_t_shapes))) for a, b in zip(outs_in, outs_late)]
    bg, bd_, bm, bv = big

    S = _small_reduce(partial_all)
    loss = S[R_LOSS, 0]
    g_conv = lax.dynamic_slice(S[R_CONV:R_CONV + 48].reshape(CONV_K, 1536), (0, 192 * my_idx), (CONV_K, 192))
    sg = [S[R_GMIX:R_GMIX + 8].reshape(1, D), S[R_QG:R_QG + 1, 0:DHA], S[R_KG:R_KG + 1, 0:DHA],
          S[R_REL:R_REL + 24].reshape(NHA, 384)[:, 0:257].T, S[R_GA:R_GA + 4].reshape(1, AW), g_conv,
          S[R_ALOG:R_ALOG + 1, NHD:2 * NHD], S[R_DT:R_DT + 1, NHD:2 * NHD], S[R_GDN:R_GDN + 1], S[R_GFFN:R_GFFN + 8].reshape(1, D)]
    sw = [norm_mix_g, attn_q_norm_g, attn_k_norm_g, rel_bias[0], attn_out_norm_g, conv_w[0], a_log, dt_bias, dn_out_norm_g, norm_ffn_g]
    smm = [m_norm_mix_g, m_attn_q_norm_g, m_attn_k_norm_g, m_rel_bias[0], m_attn_out_norm_g, m_conv_w[0], m_a_log, m_dt_bias, m_dn_out_norm_g, m_norm_ffn_g]
    svv = [v_norm_mix_g, v_attn_q_norm_g, v_attn_k_norm_g, v_rel_bias[0], v_attn_out_norm_g, v_conv_w[0], v_a_log, v_dt_bias, v_dn_out_norm_g, v_norm_ffn_g]
    s_shapes = [t.shape for t in sw]
    pg = _pack_lanes(sg)
    s_out = _adamw(pg[None], _pack_lanes(sw), _pack_lanes(smm), _pack_lanes(svv), "adamw_small", pg.shape[0])
    s_g, s_d, s_m, s_v = (_unpack_lanes(t, s_shapes) for t in s_out)

    lead = lambda t: t[None]
    def ordered(small, big):
        nm, q, k, rel, ao, cw, al, dtb, dno, nf = small
        wi, wo, wgt, wu, wdn = big
        return [nm, lead(wi), q, k, lead(rel), ao, lead(cw), al, dtb, dno, lead(wo), nf, lead(wgt), lead(wu), lead(wdn)]
    outs = [loss, gx[None]]
    for small, big in ((s_g, bg), (s_d, bd_), (s_m, bm), (s_v, bv)):
        outs += ordered(small, big)
    return tuple(outs)
```
